```python
import jax
import jax.numpy as jnp
from jax import lax
import numpy as np


D_MODEL = 1024
BATCH = 16
SEQ = 2048
DEPTH = 2

GRID_W = 64
CTX_LEN = 256
HEAD_DIM = 64
N_HEADS_A = 8
N_KV_A = 2
N_HEADS_B = 8
N_KV_B = 2
MIX_WIDTH = (N_HEADS_A + N_HEADS_B) * HEAD_DIM
Q_BLOCK = 128
WINDOW = 128
D_FF = 4 * D_MODEL
ROPE_THETA = 10000.0
EPS = 1e-6
NEG_BIG = -1e30
N_MOD = 6
COL_SIZES = (N_HEADS_A * HEAD_DIM, N_KV_A * HEAD_DIM, N_KV_A * HEAD_DIM,
             N_HEADS_B * HEAD_DIM, N_KV_B * HEAD_DIM, N_KV_B * HEAD_DIM)
IN_COLS = sum(COL_SIZES)
SPLITS = tuple(int(s) for s in np.cumsum(COL_SIZES)[:-1])

kernel_name = 'hybrid_dit_gqa_window_sink_block'


def rms_norm(x, g):
    xf = x.astype(jnp.float32)
    y = xf * lax.rsqrt(jnp.mean(xf * xf, axis=-1, keepdims=True) + EPS)
    return (y * g.astype(jnp.float32)).astype(x.dtype)


def modulate(x, shift, scale):
    return x * (1 + scale) + shift


def axial_rope_tables(n_tok):
    rows = n_tok // GRID_W
    row_ids = jnp.repeat(jnp.arange(rows, dtype=jnp.int32), GRID_W).astype(jnp.float32)
    col_ids = jnp.tile(jnp.arange(GRID_W, dtype=jnp.int32), rows).astype(jnp.float32)
    axis_dim = HEAD_DIM // 2
    inv = ROPE_THETA ** (-jnp.arange(0, axis_dim, 2, dtype=jnp.float32) / axis_dim)
    ang_r = row_ids[:, None] * inv[None, :]
    ang_c = col_ids[:, None] * inv[None, :]
    return (jnp.cos(ang_r), jnp.sin(ang_r), jnp.cos(ang_c), jnp.sin(ang_c))


def _rotate(a, cos, sin):
    a1, a2 = jnp.split(a, 2, axis=-1)
    return jnp.concatenate([a1 * cos - a2 * sin, a2 * cos + a1 * sin], axis=-1)


def apply_axial_rope(t, rope):
    cos_r, sin_r, cos_c, sin_c = rope
    tf = t.astype(jnp.float32)
    tr, tc = jnp.split(tf, 2, axis=-1)
    return jnp.concatenate([_rotate(tr, cos_r, sin_r), _rotate(tc, cos_c, sin_c)], axis=-1).astype(t.dtype)


def to_gqa(t, n_heads, n_kv):
    b, n, _ = t.shape
    return t.reshape(b, n, n_kv, n_heads // n_kv, HEAD_DIM).transpose(0, 2, 3, 1, 4)


def to_kv(t, n_kv):
    b, n, _ = t.shape
    return t.reshape(b, n, n_kv, HEAD_DIM).transpose(0, 2, 1, 3)


def merge_heads(o):
    b, hk, g, n, d = o.shape
    return o.transpose(0, 3, 1, 2, 4).reshape(b, n, hk * g * d)


def multi_source_attention(q, ks, vs, biases, sink):
    scale = HEAD_DIM ** -0.5
    logits = [jnp.einsum('bhgqd,bhkd->bhgqk', q, k, preferred_element_type=jnp.float32) * scale + bias
              for k, bias in zip(ks, biases)]
    sizes = [l.shape[-1] for l in logits]
    if sink is not None:
        s = sink.astype(jnp.float32)[None, :, :, None, None]
        logits.append(jnp.broadcast_to(s, logits[0].shape[:-1] + (1,)))
    p = jax.nn.softmax(jnp.concatenate(logits, axis=-1), axis=-1)
    out = None
    off = 0
    for v, n in zip(vs, sizes):
        term = jnp.einsum('bhgqk,bhkd->bhgqd', p[..., off:off + n].astype(v.dtype), v)
        out = term if out is None else out + term
        off += n
    return out


def dense_latent_attention(q, k_lat, v_lat, k_ctx, v_ctx):
    b, hk, g, s, d = q.shape
    nblk = s // Q_BLOCK
    qb = jnp.moveaxis(q.reshape(b, hk, g, nblk, Q_BLOCK, d), 3, 0)

    def one_block(qi):
        return multi_source_attention(qi, (k_ctx, k_lat), (v_ctx, v_lat), (0.0, 0.0), None)

    o = lax.map(one_block, qb)
    return jnp.moveaxis(o, 0, 3).reshape(b, hk, g, s, d)


def banded_blocks(t):
    b, hk, s, d = t.shape
    nblk = s // Q_BLOCK
    tp = jnp.pad(t, ((0, 0), (0, 0), (Q_BLOCK, Q_BLOCK), (0, 0))).reshape(b, hk, nblk + 2, Q_BLOCK, d)
    band = jnp.concatenate([tp[:, :, :-2], tp[:, :, 1:-1], tp[:, :, 2:]], axis=3)
    return jnp.moveaxis(band, 2, 0)


def window_bias(s):
    nblk = s // Q_BLOCK
    i = jnp.arange(nblk, dtype=jnp.int32)[:, None, None]
    r = jnp.arange(Q_BLOCK, dtype=jnp.int32)[None, :, None]
    j = jnp.arange(3 * Q_BLOCK, dtype=jnp.int32)[None, None, :]
    qpos = i * Q_BLOCK + r
    kpos = (i - 1) * Q_BLOCK + j
    valid = (jnp.abs(kpos - qpos) <= WINDOW) & (kpos >= 0) & (kpos < s)
    return jnp.where(valid, 0.0, NEG_BIG).astype(jnp.float32)


def windowed_latent_attention(q, k_lat, v_lat, k_ctx, v_ctx, sink):
    b, hk, g, s, d = q.shape
    nblk = s // Q_BLOCK
    qb = jnp.moveaxis(q.reshape(b, hk, g, nblk, Q_BLOCK, d), 3, 0)
    kb = banded_blocks(k_lat)
    vb = banded_blocks(v_lat)
    bias = window_bias(s)

    def one_block(args):
        qi, ki, vi, bi = args
        return multi_source_attention(qi, (ki, k_ctx), (vi, v_ctx), (bi[None, None, None], 0.0), sink)

    o = lax.map(one_block, (qb, kb, vb, bias))
    return jnp.moveaxis(o, 0, 3).reshape(b, hk, g, s, d)


def sq_relu_mlp(u, w_up, w_down):
    return jnp.square(jax.nn.relu(u @ w_up)) @ w_down


def _fwd_setup_inputs(seed: int = 0) -> dict:
    key = jax.random.key(seed)
    ks = jax.random.split(key, 18)
    f32 = jnp.float32
    nrm = lambda k, shape, s: jax.random.normal(k, shape, f32) * s
    return {
        'x': nrm(ks[0], (BATCH, SEQ, D_MODEL), 1.0),
        'c': nrm(ks[1], (BATCH, D_MODEL), 1.0),
        'ctx': nrm(ks[2], (BATCH, CTX_LEN, D_MODEL), 1.0),
        'c_ctx': nrm(ks[3], (D_MODEL,), 1.0),
        'w_ada': nrm(ks[4], (DEPTH, D_MODEL, N_MOD * D_MODEL), 0.02),
        'b_ada': nrm(ks[5], (DEPTH, N_MOD * D_MODEL), 0.01),
        'g_pre_mix': 1.0 + nrm(ks[6], (DEPTH, D_MODEL), 0.05),
        'g_post_mix': 1.0 + nrm(ks[7], (DEPTH, D_MODEL), 0.05),
        'g_pre_mlp': 1.0 + nrm(ks[8], (DEPTH, D_MODEL), 0.05),
        'g_post_mlp': 1.0 + nrm(ks[9], (DEPTH, D_MODEL), 0.05),
        'w_in': nrm(ks[10], (DEPTH, D_MODEL, IN_COLS), D_MODEL ** -0.5),
        'q_norm': 1.0 + nrm(ks[11], (DEPTH, HEAD_DIM), 0.05),
        'k_norm': 1.0 + nrm(ks[12], (DEPTH, HEAD_DIM), 0.05),
        'sink': nrm(ks[13], (DEPTH, N_HEADS_B), 0.5),
        'w_out': nrm(ks[14], (DEPTH, MIX_WIDTH, D_MODEL), MIX_WIDTH ** -0.5),
        'w_up': nrm(ks[15], (DEPTH, D_MODEL, D_FF), D_MODEL ** -0.5),
        'w_down': nrm(ks[16], (DEPTH, D_FF, D_MODEL), D_FF ** -0.5),
    }


def _fwd_reference(x, c, ctx, c_ctx, w_ada, b_ada, g_pre_mix, g_post_mix, g_pre_mlp, g_post_mlp,
              w_in, q_norm, k_norm, sink, w_out, w_up, w_down):
    b, s, d = x.shape
    rope = axial_rope_tables(s)
    silu_c = jax.nn.silu(c)
    silu_cc = jax.nn.silu(c_ctx)
    h, hc = x, ctx
    for l in range(DEPTH):
        last = l == DEPTH - 1
        mod = (silu_c @ w_ada[l] + b_ada[l]).reshape(b, N_MOD, 1, d)
        mod_c = (silu_cc @ w_ada[l] + b_ada[l]).reshape(N_MOD, 1, 1, d)
        sh_a, sc_a, g_a, sh_m, sc_m, g_m = [mod[:, i] for i in range(N_MOD)]
        csh_a, csc_a, cg_a, csh_m, csc_m, cg_m = [mod_c[i] for i in range(N_MOD)]

        u = modulate(rms_norm(h, g_pre_mix[l]), sh_a, sc_a)
        uc = modulate(rms_norm(hc, g_pre_mix[l]), csh_a, csc_a)
        qa, ka, va, qb, kb, vb = jnp.split(u @ w_in[l], SPLITS, axis=-1)
        qac, kac, vac, qbc, kbc, vbc = jnp.split(uc @ w_in[l], SPLITS, axis=-1)

        qa = apply_axial_rope(rms_norm(to_gqa(qa, N_HEADS_A, N_KV_A), q_norm[l]), rope)
        ka = apply_axial_rope(rms_norm(to_kv(ka, N_KV_A), k_norm[l]), rope)
        va = to_kv(va, N_KV_A)
        kac = rms_norm(to_kv(kac, N_KV_A), k_norm[l])
        vac = to_kv(vac, N_KV_A)
        qb = apply_axial_rope(to_gqa(qb, N_HEADS_B, N_KV_B), rope)
        kb = apply_axial_rope(to_kv(kb, N_KV_B), rope)
        vb = to_kv(vb, N_KV_B)
        kbc = to_kv(kbc, N_KV_B)
        vbc = to_kv(vbc, N_KV_B)
        sink_l = sink[l].reshape(N_KV_B, N_HEADS_B // N_KV_B)

        oa = dense_latent_attention(qa, ka, va, kac, vac)
        ob = windowed_latent_attention(qb, kb, vb, kbc, vbc, sink_l)
        mix = jnp.concatenate([merge_heads(oa), merge_heads(ob)], axis=-1) @ w_out[l]
        h = h + g_a * rms_norm(mix, g_post_mix[l])

        if not last:
            qac = rms_norm(to_gqa(qac, N_HEADS_A, N_KV_A), q_norm[l])
            oac = multi_source_attention(qac, (kac,), (vac,), (0.0,), None)
            obc = multi_source_attention(to_gqa(qbc, N_HEADS_B, N_KV_B), (kbc,), (vbc,), (0.0,), sink_l)
            mixc = jnp.concatenate([merge_heads(oac), merge_heads(obc)], axis=-1) @ w_out[l]
            hc = hc + cg_a * rms_norm(mixc, g_post_mix[l])

        y = sq_relu_mlp(modulate(rms_norm(h, g_pre_mlp[l]), sh_m, sc_m), w_up[l], w_down[l])
        h = h + g_m * rms_norm(y, g_post_mlp[l])
        if not last:
            yc = sq_relu_mlp(modulate(rms_norm(hc, g_pre_mlp[l]), csh_m, csc_m), w_up[l], w_down[l])
            hc = hc + cg_m * rms_norm(yc, g_post_mlp[l])
    return h


import jax as _jax
import jax.numpy as _jnp

TWIN_FORMAT = 'train_step'
FWD_PARAMS = ['x', 'c', 'ctx', 'c_ctx', 'w_ada', 'b_ada', 'g_pre_mix', 'g_post_mix', 'g_pre_mlp', 'g_post_mlp', 'w_in', 'q_norm', 'k_norm', 'sink', 'w_out', 'w_up', 'w_down']
TWIN_WEIGHTS = ['c_ctx', 'w_ada', 'b_ada', 'g_pre_mix', 'g_post_mix', 'g_pre_mlp', 'g_post_mlp', 'w_in', 'q_norm', 'k_norm', 'sink', 'w_out', 'w_up', 'w_down']
TWIN_DIFF_INPUT = 'x'
TWIN_INPUTS = ['x', 'c', 'ctx', 'c_ctx', 'w_ada', 'b_ada', 'g_pre_mix', 'g_post_mix', 'g_pre_mlp', 'g_post_mlp', 'w_in', 'q_norm', 'k_norm', 'sink', 'w_out', 'w_up', 'w_down', 'loss_target', 'm_c_ctx', 'm_w_ada', 'm_b_ada', 'm_g_pre_mix', 'm_g_post_mix', 'm_g_pre_mlp', 'm_g_post_mlp', 'm_w_in', 'm_q_norm', 'm_k_norm', 'm_sink', 'm_w_out', 'm_w_up', 'm_w_down', 'v_c_ctx', 'v_w_ada', 'v_b_ada', 'v_g_pre_mix', 'v_g_post_mix', 'v_g_pre_mlp', 'v_g_post_mlp', 'v_w_in', 'v_q_norm', 'v_k_norm', 'v_sink', 'v_w_out', 'v_w_up', 'v_w_down']
TWIN_OUTPUTS = ['loss', 'grad_x', 'grad_c_ctx', 'grad_w_ada', 'grad_b_ada', 'grad_g_pre_mix', 'grad_g_post_mix', 'grad_g_pre_mlp', 'grad_g_post_mlp', 'grad_w_in', 'grad_q_norm', 'grad_k_norm', 'grad_sink', 'grad_w_out', 'grad_w_up', 'grad_w_down', 'delta_c_ctx', 'delta_w_ada', 'delta_b_ada', 'delta_g_pre_mix', 'delta_g_post_mix', 'delta_g_pre_mlp', 'delta_g_post_mlp', 'delta_w_in', 'delta_q_norm', 'delta_k_norm', 'delta_sink', 'delta_w_out', 'delta_w_up', 'delta_w_down', 'new_m_c_ctx', 'new_m_w_ada', 'new_m_b_ada', 'new_m_g_pre_mix', 'new_m_g_post_mix', 'new_m_g_pre_mlp', 'new_m_g_post_mlp', 'new_m_w_in', 'new_m_q_norm', 'new_m_k_norm', 'new_m_sink', 'new_m_w_out', 'new_m_w_up', 'new_m_w_down', 'new_v_c_ctx', 'new_v_w_ada', 'new_v_b_ada', 'new_v_g_pre_mix', 'new_v_g_post_mix', 'new_v_g_pre_mlp', 'new_v_g_post_mlp', 'new_v_w_in', 'new_v_q_norm', 'new_v_k_norm', 'new_v_sink', 'new_v_w_out', 'new_v_w_up', 'new_v_w_down']
TWIN_LEAF_KINDS = {'loss': 'loss', 'grad_x': 'grad_x', 'grad_c_ctx': 'grad_w', 'grad_w_ada': 'grad_w', 'grad_b_ada': 'grad_w', 'grad_g_pre_mix': 'grad_w', 'grad_g_post_mix': 'grad_w', 'grad_g_pre_mlp': 'grad_w', 'grad_g_post_mlp': 'grad_w', 'grad_w_in': 'grad_w', 'grad_q_norm': 'grad_w', 'grad_k_norm': 'grad_w', 'grad_sink': 'grad_w', 'grad_w_out': 'grad_w', 'grad_w_up': 'grad_w', 'grad_w_down': 'grad_w', 'delta_c_ctx': 'delta_w', 'delta_w_ada': 'delta_w', 'delta_b_ada': 'delta_w', 'delta_g_pre_mix': 'delta_w', 'delta_g_post_mix': 'delta_w', 'delta_g_pre_mlp': 'delta_w', 'delta_g_post_mlp': 'delta_w', 'delta_w_in': 'delta_w', 'delta_q_norm': 'delta_w', 'delta_k_norm': 'delta_w', 'delta_sink': 'delta_w', 'delta_w_out': 'delta_w', 'delta_w_up': 'delta_w', 'delta_w_down': 'delta_w', 'new_m_c_ctx': 'new_m', 'new_m_w_ada': 'new_m', 'new_m_b_ada': 'new_m', 'new_m_g_pre_mix': 'new_m', 'new_m_g_post_mix': 'new_m', 'new_m_g_pre_mlp': 'new_m', 'new_m_g_post_mlp': 'new_m', 'new_m_w_in': 'new_m', 'new_m_q_norm': 'new_m', 'new_m_k_norm': 'new_m', 'new_m_sink': 'new_m', 'new_m_w_out': 'new_m', 'new_m_w_up': 'new_m', 'new_m_w_down': 'new_m', 'new_v_c_ctx': 'new_v', 'new_v_w_ada': 'new_v', 'new_v_b_ada': 'new_v', 'new_v_g_pre_mix': 'new_v', 'new_v_g_post_mix': 'new_v', 'new_v_g_pre_mlp': 'new_v', 'new_v_g_post_mlp': 'new_v', 'new_v_w_in': 'new_v', 'new_v_q_norm': 'new_v', 'new_v_k_norm': 'new_v', 'new_v_sink': 'new_v', 'new_v_w_out': 'new_v', 'new_v_w_up': 'new_v', 'new_v_w_down': 'new_v'}


def _forward(args):
    return _fwd_reference(*[args[k] for k in FWD_PARAMS])


def _output_shape():
    out = _jax.eval_shape(lambda: _forward(_fwd_setup_inputs(0)))
    return out.shape, out.dtype

N_MICROBATCH = 1
ADAM_LR = 0.001
ADAM_B1 = 0.9
ADAM_B2 = 0.999
ADAM_EPS = 1e-08
ADAM_WD = 0.01
ADAM_STEP = 10
PER_EXAMPLE_BATCH_AXIS = {'x': 0, 'c': 0, 'ctx': 0, 'loss_target': 0}
SHARED_INPUTS = []
_WEIGHT_DTYPES = {'c_ctx': _jnp.float32, 'w_ada': _jnp.float32, 'b_ada': _jnp.float32, 'g_pre_mix': _jnp.float32, 'g_post_mix': _jnp.float32, 'g_pre_mlp': _jnp.float32, 'g_post_mlp': _jnp.float32, 'w_in': _jnp.float32, 'q_norm': _jnp.float32, 'k_norm': _jnp.float32, 'sink': _jnp.float32, 'w_out': _jnp.float32, 'w_up': _jnp.float32, 'w_down': _jnp.float32}
MOMENT_SCALE = {'c_ctx': 2.382447e+00, 'w_ada': 5.341269e+00, 'b_ada': 9.747468e+00, 'g_pre_mix': 2.294699e+00, 'g_post_mix': 9.420422e+00, 'g_pre_mlp': 1.375735e+00, 'g_post_mlp': 7.138346e+00, 'w_in': 4.202176e+00, 'q_norm': 5.189237e-01, 'k_norm': 5.276672e-01, 'sink': 5.525631e-02, 'w_out': 5.076687e+00, 'w_up': 9.706779e-01, 'w_down': 2.699405e+00}


def _to_microbatches(a, axis):
    t = _jnp.moveaxis(a, axis, 0)
    t = t.reshape((N_MICROBATCH, t.shape[0] // N_MICROBATCH) + t.shape[1:])
    return _jnp.moveaxis(t, 1, axis + 1)


def setup_inputs(seed: int = 0) -> dict:
    inp = _fwd_setup_inputs(seed)
    key = _jax.random.fold_in(_jax.random.key(seed), 7919)
    shape, _ = _output_shape()
    out = dict(inp)
    out["loss_target"] = _jax.random.normal(_jax.random.fold_in(key, 0), shape, _jnp.float32)
    for i, name in enumerate(TWIN_WEIGHTS):
        w = inp[name].astype(_jnp.float32)
        if MOMENT_SCALE is None:
            s = _jnp.sqrt(_jnp.mean(_jnp.square(w)) + 1e-30)
        else:
            s = MOMENT_SCALE[name]
        km, kv = _jax.random.split(_jax.random.fold_in(key, i + 1))
        out[name] = w
        out["m_" + name] = s * _jax.random.normal(km, w.shape, _jnp.float32)
        out["v_" + name] = (s * s) * _jax.random.uniform(kv, w.shape, _jnp.float32, 0.5, 1.5)
    if N_MICROBATCH > 1:
        for name, axis in PER_EXAMPLE_BATCH_AXIS.items():
            out[name] = _to_microbatches(out[name], axis)
    return {'x': out['x'], 'c': out['c'], 'ctx': out['ctx'], 'c_ctx': out['c_ctx'], 'w_ada': out['w_ada'], 'b_ada': out['b_ada'], 'g_pre_mix': out['g_pre_mix'], 'g_post_mix': out['g_post_mix'], 'g_pre_mlp': out['g_pre_mlp'], 'g_post_mlp': out['g_post_mlp'], 'w_in': out['w_in'], 'q_norm': out['q_norm'], 'k_norm': out['k_norm'], 'sink': out['sink'], 'w_out': out['w_out'], 'w_up': out['w_up'], 'w_down': out['w_down'], 'loss_target': out['loss_target'], 'm_c_ctx': out['m_c_ctx'], 'm_w_ada': out['m_w_ada'], 'm_b_ada': out['m_b_ada'], 'm_g_pre_mix': out['m_g_pre_mix'], 'm_g_post_mix': out['m_g_post_mix'], 'm_g_pre_mlp': out['m_g_pre_mlp'], 'm_g_post_mlp': out['m_g_post_mlp'], 'm_w_in': out['m_w_in'], 'm_q_norm': out['m_q_norm'], 'm_k_norm': out['m_k_norm'], 'm_sink': out['m_sink'], 'm_w_out': out['m_w_out'], 'm_w_up': out['m_w_up'], 'm_w_down': out['m_w_down'], 'v_c_ctx': out['v_c_ctx'], 'v_w_ada': out['v_w_ada'], 'v_b_ada': out['v_b_ada'], 'v_g_pre_mix': out['v_g_pre_mix'], 'v_g_post_mix': out['v_g_post_mix'], 'v_g_pre_mlp': out['v_g_pre_mlp'], 'v_g_post_mlp': out['v_g_post_mlp'], 'v_w_in': out['v_w_in'], 'v_q_norm': out['v_q_norm'], 'v_k_norm': out['v_k_norm'], 'v_sink': out['v_sink'], 'v_w_out': out['v_w_out'], 'v_w_up': out['v_w_up'], 'v_w_down': out['v_w_down']}


def _loss(weights, diff, rest, loss_target):
    with _jax.named_scope("forward"):
        args = {**rest, TWIN_DIFF_INPUT: diff, **{k: w.astype(_WEIGHT_DTYPES[k]) for k, w in weights.items()}}
        y = _forward(args)
    with _jax.named_scope("loss_head"):
        err = _jnp.square(y.astype(_jnp.float32) - loss_target)
        return 0.5 * _jnp.sum(_jnp.mean(err, axis=-1)) if err.ndim else 0.5 * err


def _adamw(w, g, m, v):
    m = ADAM_B1 * m + (1.0 - ADAM_B1) * g
    v = ADAM_B2 * v + (1.0 - ADAM_B2) * _jnp.square(g)
    m_hat = m / (1.0 - ADAM_B1 ** ADAM_STEP)
    v_hat = v / (1.0 - ADAM_B2 ** ADAM_STEP)
    delta = -ADAM_LR * (m_hat / (_jnp.sqrt(v_hat) + ADAM_EPS) + ADAM_WD * w)
    return delta, m, v


def reference(x, c, ctx, c_ctx, w_ada, b_ada, g_pre_mix, g_post_mix, g_pre_mlp, g_post_mlp, w_in, q_norm, k_norm, sink, w_out, w_up, w_down, loss_target, m_c_ctx, m_w_ada, m_b_ada, m_g_pre_mix, m_g_post_mix, m_g_pre_mlp, m_g_post_mlp, m_w_in, m_q_norm, m_k_norm, m_sink, m_w_out, m_w_up, m_w_down, v_c_ctx, v_w_ada, v_b_ada, v_g_pre_mix, v_g_post_mix, v_g_pre_mlp, v_g_post_mlp, v_w_in, v_q_norm, v_k_norm, v_sink, v_w_out, v_w_up, v_w_down):
    given = dict(x=x, c=c, ctx=ctx, c_ctx=c_ctx, w_ada=w_ada, b_ada=b_ada, g_pre_mix=g_pre_mix, g_post_mix=g_post_mix, g_pre_mlp=g_pre_mlp, g_post_mlp=g_post_mlp, w_in=w_in, q_norm=q_norm, k_norm=k_norm, sink=sink, w_out=w_out, w_up=w_up, w_down=w_down, loss_target=loss_target, m_c_ctx=m_c_ctx, m_w_ada=m_w_ada, m_b_ada=m_b_ada, m_g_pre_mix=m_g_pre_mix, m_g_post_mix=m_g_post_mix, m_g_pre_mlp=m_g_pre_mlp, m_g_post_mlp=m_g_post_mlp, m_w_in=m_w_in, m_q_norm=m_q_norm, m_k_norm=m_k_norm, m_sink=m_sink, m_w_out=m_w_out, m_w_up=m_w_up, m_w_down=m_w_down, v_c_ctx=v_c_ctx, v_w_ada=v_w_ada, v_b_ada=v_b_ada, v_g_pre_mix=v_g_pre_mix, v_g_post_mix=v_g_post_mix, v_g_pre_mlp=v_g_pre_mlp, v_g_post_mlp=v_g_post_mlp, v_w_in=v_w_in, v_q_norm=v_q_norm, v_k_norm=v_k_norm, v_sink=v_sink, v_w_out=v_w_out, v_w_up=v_w_up, v_w_down=v_w_down)
    weights = {n: given[n] for n in TWIN_WEIGHTS}
    shared = {n: given[n] for n in SHARED_INPUTS}
    per_example = {n: given[n] for n in ['x', 'c', 'ctx']}
    grad_fn = _jax.value_and_grad(_loss, argnums=(0, 1))

    def one_microbatch(ex, loss_target):
        ex = dict(ex)
        diff = ex.pop(TWIN_DIFF_INPUT)
        return grad_fn(weights, diff, {**shared, **ex}, loss_target)

    if N_MICROBATCH == 1:
        loss, (grad_w, grad_x) = one_microbatch(per_example, given["loss_target"])
    else:
        def body(carry, xs):
            loss_sum, grad_sum = carry
            l_k, (gw_k, gx_k) = one_microbatch(xs[0], xs[1])
            with _jax.named_scope("update"):
                return (loss_sum + l_k, _jax.tree.map(_jnp.add, grad_sum, gw_k)), gx_k

        init = (_jnp.zeros((), _jnp.float32), _jax.tree.map(_jnp.zeros_like, weights))
        (loss, grad_w), grad_x = _jax.lax.scan(body, init, (per_example, given["loss_target"]))
    with _jax.named_scope("update"):
        delta_w, new_m, new_v = {}, {}, {}
        for n in TWIN_WEIGHTS:
            delta_w[n], new_m[n], new_v[n] = _adamw(weights[n], grad_w[n], given["m_" + n], given["v_" + n])
    return (loss, grad_x, *[grad_w[n] for n in TWIN_WEIGHTS], *[delta_w[n] for n in TWIN_WEIGHTS],
            *[new_m[n] for n in TWIN_WEIGHTS], *[new_v[n] for n in TWIN_WEIGHTS])
```

```python
import functools

import jax
import jax.numpy as jnp
import numpy as np
from jax import lax
from jax.experimental import pallas as pl
from jax.experimental.pallas import tpu as pltpu

F32 = jnp.float32
BF16 = jnp.bfloat16

HEAD_DIM = 64
GROUP = 4
N_HG = 4
Q_WIDTH = GROUP * HEAD_DIM
WINDOW = 128
GRID_W = 64
ROPE_THETA = 10000.0
EPS = 1e-6
NEG_BIG = -1e30
N_MOD = 6
MOD_ROWS = 8
TM = 256
N_DEV = 8
ADA_ROWS = 8
VMEM_LIMIT = 56 * 1024 * 1024

ADAM_LR = 0.001
ADAM_B1 = 0.9
ADAM_B2 = 0.999
ADAM_EPS = 1e-08
ADAM_WD = 0.01
ADAM_STEP = 10

MESH_AXES = ("x", "y", "c")


def _params(sem=None):
    kw = dict(vmem_limit_bytes=VMEM_LIMIT)
    if sem is not None:
        kw["dimension_semantics"] = sem
    return pltpu.CompilerParams(**kw)


def _my_index():
    return 4 * lax.axis_index("x") + 2 * lax.axis_index("y") + lax.axis_index("c")


def _peer(k):
    x, y, c = lax.axis_index("x"), lax.axis_index("y"), lax.axis_index("c")
    kx, ky, kc = (k >> 2) & 1, (k >> 1) & 1, k & 1
    px = (1 - x) if kx else x
    py = (1 - y) if ky else y
    pc = (1 - c) if kc else c
    return (px, py, pc), 4 * px + 2 * py + pc


def _all_gather(shard, name):
    def body(x_ref, out_ref, send_sems, recv_sems, local_sem):
        me = _my_index()
        mine = pltpu.make_async_copy(x_ref, out_ref.at[me], local_sem)
        mine.start()
        sends = []
        for k in range(1, N_DEV):
            peer, _ = _peer(k)
            cp = pltpu.make_async_remote_copy(
                src_ref=x_ref, dst_ref=out_ref.at[me], send_sem=send_sems.at[k], recv_sem=recv_sems.at[k],
                device_id=peer, device_id_type=pl.DeviceIdType.MESH)
            cp.start()
            sends.append(cp)
        for k in range(1, N_DEV):
            peer, pidx = _peer(k)
            pltpu.make_async_remote_copy(
                src_ref=x_ref, dst_ref=out_ref.at[pidx], send_sem=send_sems.at[k], recv_sem=recv_sems.at[k],
                device_id=peer, device_id_type=pl.DeviceIdType.MESH).wait_recv()
        for cp in sends:
            cp.wait_send()
        mine.wait()

    return pl.pallas_call(
        body, name=name,
        out_shape=jax.ShapeDtypeStruct((N_DEV,) + shard.shape, shard.dtype),
        in_specs=[pl.BlockSpec(memory_space=pl.ANY)],
        out_specs=pl.BlockSpec(memory_space=pl.ANY),
        scratch_shapes=[pltpu.SemaphoreType.DMA((N_DEV,)), pltpu.SemaphoreType.DMA((N_DEV,)), pltpu.SemaphoreType.DMA],
    )(shard)


def _exchange(slots, name):
    def body(x_ref, out_ref, send_sems, recv_sems, local_sem):
        me = _my_index()
        mine = pltpu.make_async_copy(x_ref.at[me], out_ref.at[0], local_sem)
        mine.start()
        sends = []
        for k in range(1, N_DEV):
            peer, pidx = _peer(k)
            cp = pltpu.make_async_remote_copy(
                src_ref=x_ref.at[pidx], dst_ref=out_ref.at[k], send_sem=send_sems.at[k], recv_sem=recv_sems.at[k],
                device_id=peer, device_id_type=pl.DeviceIdType.MESH)
            cp.start()
            sends.append(cp)
        for cp in sends:
            cp.wait_recv()
        for cp in sends:
            cp.wait_send()
        mine.wait()

    return pl.pallas_call(
        body, name=name,
        out_shape=jax.ShapeDtypeStruct(slots.shape, slots.dtype),
        in_specs=[pl.BlockSpec(memory_space=pl.ANY)],
        out_specs=pl.BlockSpec(memory_space=pl.ANY),
        scratch_shapes=[pltpu.SemaphoreType.DMA((N_DEV,)), pltpu.SemaphoreType.DMA((N_DEV,)), pltpu.SemaphoreType.DMA],
    )(slots)


def _mm(a, b, *, name, ta=False, b_mode="nn", out_mode="plain", out_dtype=F32, tm=512, tn=512, tk=512,
        epilogue=None, extra=None):
    if ta:
        kdim, m = a.shape
    else:
        m, kdim = a.shape
    if b_mode == "nn":
        n = b.shape[1]
    elif b_mode == "nt":
        n = b.shape[0]
    elif b_mode == "nn_slots":
        n = b.shape[0] * b.shape[2]
        tn = b.shape[2]
    else:
        n = b.shape[1]
        tk = b.shape[2]
    if out_mode == "slots":
        tn = n // N_DEV
    tm, tn, tk = min(tm, m), min(tn, n), min(tk, kdim)
    assert m % tm == 0 and n % tn == 0 and kdim % tk == 0, (name, m, n, kdim, tm, tn, tk)
    nk = kdim // tk

    a_spec = pl.BlockSpec((tk, tm), lambda i, j, k: (k, i)) if ta else pl.BlockSpec((tm, tk), lambda i, j, k: (i, k))
    if b_mode == "nn":
        b_spec = pl.BlockSpec((tk, tn), lambda i, j, k: (k, j))
    elif b_mode == "nt":
        b_spec = pl.BlockSpec((tn, tk), lambda i, j, k: (j, k))
    elif b_mode == "nn_slots":
        b_spec = pl.BlockSpec((None, tk, tn), lambda i, j, k: (j, k, 0))
    else:
        b_spec = pl.BlockSpec((None, tn, tk), lambda i, j, k: (k, j, 0))
    tb = b_mode in ("nt", "nt_slots")
    if out_mode == "plain":
        o_shape, o_spec = (m, n), pl.BlockSpec((tm, tn), lambda i, j, k: (i, j))
    else:
        o_shape, o_spec = (N_DEV, m, tn), pl.BlockSpec((None, tm, tn), lambda i, j, k: (j, i, 0))
    dims = (((0 if ta else 1,), (1 if tb else 0,)), ((), ()))

    in_specs = [a_spec, b_spec]
    args = [a, b]
    if epilogue == "mul":
        in_specs.append(pl.BlockSpec((tm, tn), lambda i, j, k: (i, j)))
        args.append(extra)
    if epilogue == "relu2":
        out_shape = [jax.ShapeDtypeStruct(o_shape, BF16)] * 2
        out_specs = [o_spec, o_spec]
    else:
        out_shape = jax.ShapeDtypeStruct(o_shape, out_dtype)
        out_specs = o_spec

    def body(*refs):
        a_ref, b_ref = refs[0], refs[1]
        acc_ref = refs[-1]
        k = pl.program_id(2)

        @pl.when(k == 0)
        def _():
            acc_ref[...] = jnp.zeros_like(acc_ref)

        acc_ref[...] += lax.dot_general(a_ref[...], b_ref[...], dims, preferred_element_type=F32)

        @pl.when(k == nk - 1)
        def _():
            acc = acc_ref[...]
            if epilogue == "relu2":
                r = jnp.maximum(acc, 0.0)
                refs[2][...] = (r * r).astype(BF16)
                refs[3][...] = (2.0 * r).astype(BF16)
            elif epilogue == "mul":
                refs[3][...] = (acc * refs[2][...].astype(F32)).astype(out_dtype)
            else:
                refs[2][...] = acc.astype(out_dtype)

    return pl.pallas_call(
        body, name=name, grid=(m // tm, n // tn, nk),
        in_specs=in_specs, out_specs=out_specs, out_shape=out_shape,
        scratch_shapes=[pltpu.VMEM((tm, tn), F32)],
        compiler_params=_params(("parallel", "parallel", "arbitrary")),
    )(*args)


def _row_specs(d, nt):
    row = pl.BlockSpec((TM, d), lambda t: (t, 0))
    vec = pl.BlockSpec((1, d), lambda t: (0, 0))
    mod = pl.BlockSpec((None, None, MOD_ROWS, d), lambda t: (t // nt, jnp.minimum(t % nt, 1), 0, 0))
    return row, vec, mod


def _rms(x):
    r = lax.rsqrt(jnp.mean(x * x, axis=1, keepdims=True) + EPS)
    return r, x * r


def _norm_mod_fwd(h, g, modsel, i_sh, i_sc, nt, name):
    m, d = h.shape
    row, vec, mod = _row_specs(d, nt)

    def body(h_ref, g_ref, ms_ref, u_ref):
        _, xh = _rms(h_ref[...])
        ms = ms_ref[...]
        u_ref[...] = (xh * g_ref[...] * (1.0 + ms[i_sc:i_sc + 1]) + ms[i_sh:i_sh + 1]).astype(BF16)

    return pl.pallas_call(
        body, name=name, grid=(m // TM,), in_specs=[row, vec, mod], out_specs=row,
        out_shape=jax.ShapeDtypeStruct((m, d), BF16), compiler_params=_params(("parallel",)),
    )(h, g, modsel)


def _acc_rows(t, nt, dvec_ref, rows):
    first = (t % nt) <= 1

    @pl.when(first)
    def _():
        dvec_ref[...] = rows

    @pl.when(jnp.logical_not(first))
    def _():
        dvec_ref[...] += rows


def _norm_mod_bwd(h, g, modsel, du, dh_in, i_sh, i_sc, nt, name):
    m, d = h.shape
    row, vec, mod = _row_specs(d, nt)

    def body(h_ref, g_ref, ms_ref, du_ref, dhi_ref, dh_ref, dvec_ref):
        t = pl.program_id(0)
        r, xh = _rms(h_ref[...])
        g_ = g_ref[...]
        ms = ms_ref[...]
        du_ = du_ref[...]
        y = xh * g_
        dy = du_ * (1.0 + ms[i_sc:i_sc + 1])
        dxh = dy * g_
        dx = r * (dxh - xh * jnp.mean(dxh * xh, axis=1, keepdims=True))
        dh_ref[...] = dhi_ref[...] + dx
        rows = jnp.concatenate([
            jnp.sum(du_, axis=0, keepdims=True), jnp.sum(du_ * y, axis=0, keepdims=True),
            jnp.sum(dy * xh, axis=0, keepdims=True), jnp.zeros((MOD_ROWS - 3, d), F32)], axis=0)
        _acc_rows(t, nt, dvec_ref, rows)

    return pl.pallas_call(
        body, name=name, grid=(m // TM,), in_specs=[row, vec, mod, row, row], out_specs=[row, mod],
        out_shape=[jax.ShapeDtypeStruct((m, d), F32), jax.ShapeDtypeStruct(modsel.shape, F32)],
        compiler_params=_params(("arbitrary",)),
    )(h, g, modsel, du, dh_in)


def _gate_fwd(h, z, g, modsel, i_g, nt, name):
    m, d = h.shape
    row, vec, mod = _row_specs(d, nt)

    def body(h_ref, z_ref, g_ref, ms_ref, o_ref):
        _, xh = _rms(z_ref[...])
        ms = ms_ref[...]
        o_ref[...] = h_ref[...] + ms[i_g:i_g + 1] * (xh * g_ref[...])

    return pl.pallas_call(
        body, name=name, grid=(m // TM,), in_specs=[row, row, vec, mod], out_specs=row,
        out_shape=jax.ShapeDtypeStruct((m, d), F32), compiler_params=_params(("parallel",)),
    )(h, z, g, modsel)


def _gate_bwd(z, g, modsel, dh, i_g, nt, name):
    m, d = z.shape
    row, vec, mod = _row_specs(d, nt)

    def body(z_ref, g_ref, ms_ref, dh_ref, dz_ref, dvec_ref):
        t = pl.program_id(0)
        r, xh = _rms(z_ref[...])
        g_ = g_ref[...]
        ms = ms_ref[...]
        dh_ = dh_ref[...]
        dy = dh_ * ms[i_g:i_g + 1]
        dxh = dy * g_
        dz_ref[...] = (r * (dxh - xh * jnp.mean(dxh * xh, axis=1, keepdims=True))).astype(BF16)
        rows = jnp.concatenate([
            jnp.sum(dh_ * (xh * g_), axis=0, keepdims=True), jnp.sum(dy * xh, axis=0, keepdims=True),
            jnp.zeros((MOD_ROWS - 2, d), F32)], axis=0)
        _acc_rows(t, nt, dvec_ref, rows)

    return pl.pallas_call(
        body, name=name, grid=(m // TM,), in_specs=[row, vec, mod, row], out_specs=[row, mod],
        out_shape=[jax.ShapeDtypeStruct((m, d), BF16), jax.ShapeDtypeStruct(modsel.shape, F32)],
        compiler_params=_params(("arbitrary",)),
    )(z, g, modsel, dh)


def _loss_grad(h, target, c_rows, nt, name):
    m, d = h.shape
    row = pl.BlockSpec((TM, d), lambda t: (t, 0))
    ntl = nt - 1
    tgt = pl.BlockSpec((TM, d), lambda t: ((t // nt) * ntl + jnp.maximum(t % nt, 1) - 1, 0))
    acc = pl.BlockSpec((8, 128), lambda t: (0, 0))

    def body(h_ref, t_ref, dh_ref, ss_ref):
        t = pl.program_id(0)

        @pl.when(t == 0)
        def _():
            ss_ref[...] = jnp.zeros_like(ss_ref)

        @pl.when(t % nt == 0)
        def _():
            dh_ref[...] = jnp.zeros_like(dh_ref)

        @pl.when(t % nt != 0)
        def _():
            e = h_ref[...] - t_ref[...]
            dh_ref[...] = e * (1.0 / d)
            ss_ref[...] += jnp.sum(e * e)

    return pl.pallas_call(
        body, name=name, grid=(m // TM,), in_specs=[row, tgt], out_specs=[row, acc],
        out_shape=[jax.ShapeDtypeStruct((m, d), F32), jax.ShapeDtypeStruct((8, 128), F32)],
        compiler_params=_params(("arbitrary",)),
    )(h, target)


QA, KA, VA, QB, KB, VB = 0, 512, 640, 768, 1280, 1408
PROJ_W = 1536
Q_SCALE = HEAD_DIM ** -0.5


def _swap16(x):
    lane = lax.broadcasted_iota(jnp.int32, x.shape, 1)
    n = x.shape[1]
    return jnp.where((lane % 32) < 16, pltpu.roll(x, n - 16, 1), pltpu.roll(x, 16, 1))


def _seg_mean(x, e):
    return jnp.dot(x, e, preferred_element_type=F32, precision=lax.Precision.HIGHEST)


def _rope_tables(t_rows, c_rows):
    s = t_rows - c_rows
    row_ids = jnp.repeat(jnp.arange(s // GRID_W, dtype=jnp.int32), GRID_W).astype(F32)
    col_ids = jnp.tile(jnp.arange(GRID_W, dtype=jnp.int32), s // GRID_W).astype(F32)
    axis_dim = HEAD_DIM // 2
    inv = ROPE_THETA ** (-jnp.arange(0, axis_dim, 2, dtype=F32) / axis_dim)
    ang_r = row_ids[:, None] * inv[None, :]
    ang_c = col_ids[:, None] * inv[None, :]
    cos = jnp.concatenate([jnp.cos(ang_r), jnp.cos(ang_r), jnp.cos(ang_c), jnp.cos(ang_c)], axis=1)
    sin = jnp.concatenate([-jnp.sin(ang_r), jnp.sin(ang_r), -jnp.sin(ang_c), jnp.sin(ang_c)], axis=1)
    cos = jnp.concatenate([jnp.ones((c_rows, HEAD_DIM), F32), cos], axis=0)
    sin = jnp.concatenate([jnp.zeros((c_rows, HEAD_DIM), F32), sin], axis=0)
    return jnp.tile(cos, (1, 8)), jnp.tile(sin, (1, 8))


def _head_mean_matrix():
    i = np.arange(512)
    return jnp.asarray((i[:, None] // HEAD_DIM == i[None, :] // HEAD_DIM).astype(np.float32) / HEAD_DIM)


def _interleave_kv(k, v):
    return jnp.concatenate([k[:, :64], v[:, :64], k[:, 64:], v[:, 64:]], axis=1)


def _prep_fwd(proj, qn, kn, cos, sin, emat, nt, name):
    m = proj.shape[0]
    specs = [
        pl.BlockSpec((TM, PROJ_W), lambda t: (t, 0)),
        pl.BlockSpec((1, 512), lambda t: (0, 0)), pl.BlockSpec((1, 128), lambda t: (0, 0)),
        pl.BlockSpec((TM, 512), lambda t: (t % nt, 0)), pl.BlockSpec((TM, 512), lambda t: (t % nt, 0)),
        pl.BlockSpec((512, 512), lambda t: (0, 0)),
    ]

    def body(p_ref, qn_ref, kn_ref, cos_ref, sin_ref, e_ref, q_ref, kv_ref):
        cos_, sin_, e = cos_ref[...], sin_ref[...], e_ref[...]

        def rope(x, w):
            return x * cos_[:, :w] + _swap16(x) * sin_[:, :w]

        def norm(x, g, w):
            return x * lax.rsqrt(_seg_mean(x * x, e[:w, :w]) + EPS) * g

        qa = rope(norm(p_ref[:, QA:QA + 512], qn_ref[...], 512), 512)
        qb = rope(p_ref[:, QB:QB + 512], 512)
        q_ref[:, 0:512] = (qa * Q_SCALE).astype(BF16)
        q_ref[:, 512:1024] = (qb * Q_SCALE).astype(BF16)
        ka = rope(norm(p_ref[:, KA:KA + 128], kn_ref[...], 128), 128)
        kb = rope(p_ref[:, KB:KB + 128], 128)
        kv_ref[:, 0:256] = _interleave_kv(ka, p_ref[:, VA:VA + 128]).astype(BF16)
        kv_ref[:, 256:512] = _interleave_kv(kb, p_ref[:, VB:VB + 128]).astype(BF16)

    return pl.pallas_call(
        body, name=name, grid=(m // TM,), in_specs=specs,
        out_specs=[pl.BlockSpec((TM, 1024), lambda t: (t, 0)), pl.BlockSpec((TM, 512), lambda t: (t, 0))],
        out_shape=[jax.ShapeDtypeStruct((m, 1024), BF16), jax.ShapeDtypeStruct((m, 512), BF16)],
        compiler_params=_params(("parallel",)),
    )(proj, qn, kn, cos, sin, emat)


def _prep_bwd(proj, dq, dkv, qn, kn, cos, sin, emat, nt, name):
    m = proj.shape[0]
    specs = [
        pl.BlockSpec((TM, PROJ_W), lambda t: (t, 0)),
        pl.BlockSpec((TM, 1024), lambda t: (t, 0)), pl.BlockSpec((TM, 512), lambda t: (t, 0)),
        pl.BlockSpec((1, 512), lambda t: (0, 0)), pl.BlockSpec((1, 128), lambda t: (0, 0)),
        pl.BlockSpec((TM, 512), lambda t: (t % nt, 0)), pl.BlockSpec((TM, 512), lambda t: (t % nt, 0)),
        pl.BlockSpec((512, 512), lambda t: (0, 0)),
    ]

    def body(p_ref, dq_ref, dkv_ref, qn_ref, kn_ref, cos_ref, sin_ref, e_ref, dp_ref, dqn_ref, dkn_ref):
        t = pl.program_id(0)
        cos_, sin_, e = cos_ref[...], sin_ref[...], e_ref[...]

        @pl.when(t == 0)
        def _():
            dqn_ref[...] = jnp.zeros_like(dqn_ref)
            dkn_ref[...] = jnp.zeros_like(dkn_ref)

        def unrope(dy, w):
            return dy * cos_[:, :w] + _swap16(dy * sin_[:, :w])

        def norm_bwd(x, g, dy, w):
            r = lax.rsqrt(_seg_mean(x * x, e[:w, :w]) + EPS)
            xh = x * r
            dxh = dy * g
            dx = r * (dxh - xh * _seg_mean(dxh * xh, e[:w, :w]))
            return dx, jnp.sum(dy * xh, axis=0, keepdims=True)

        dqa, dgq = norm_bwd(p_ref[:, QA:QA + 512], qn_ref[...], unrope(dq_ref[:, 0:512] * Q_SCALE, 512), 512)
        dp_ref[:, QA:QA + 512] = dqa.astype(BF16)
        dp_ref[:, QB:QB + 512] = unrope(dq_ref[:, 512:1024] * Q_SCALE, 512).astype(BF16)
        da = dkv_ref[:, 0:256]
        db = dkv_ref[:, 256:512]
        dka = jnp.concatenate([da[:, 0:64], da[:, 128:192]], axis=1)
        dva = jnp.concatenate([da[:, 64:128], da[:, 192:256]], axis=1)
        dkb = jnp.concatenate([db[:, 0:64], db[:, 128:192]], axis=1)
        dvb = jnp.concatenate([db[:, 64:128], db[:, 192:256]], axis=1)
        dka, dgk = norm_bwd(p_ref[:, KA:KA + 128], kn_ref[...], unrope(dka, 128), 128)
        dp_ref[:, KA:KA + 128] = dka.astype(BF16)
        dp_ref[:, VA:VA + 128] = dva.astype(BF16)
        dp_ref[:, KB:KB + 128] = unrope(dkb, 128).astype(BF16)
        dp_ref[:, VB:VB + 128] = dvb.astype(BF16)
        dqn_ref[0:1, :] += dgq
        dkn_ref[0:1, :] += dgk

    return pl.pallas_call(
        body, name=name, grid=(m // TM,), in_specs=specs,
        out_specs=[pl.BlockSpec((TM, PROJ_W), lambda t: (t, 0)), pl.BlockSpec((8, 512), lambda t: (0, 0)),
                   pl.BlockSpec((8, 128), lambda t: (0, 0))],
        out_shape=[jax.ShapeDtypeStruct((m, PROJ_W), BF16), jax.ShapeDtypeStruct((8, 512), F32),
                   jax.ShapeDtypeStruct((8, 128), F32)],
        compiler_params=_params(("arbitrary",)),
    )(proj, dq, dkv, qn, kn, cos, sin, emat)


def _attn_case(t, hg, q_ref, kv_ref, c_rows, t_rows, fn):
    wl = TM + 2 * WINDOW
    kvd = lambda a, n: kv_ref[pl.ds(a, n), :]
    dense = hg < 2
    ctx = t == 0

    @pl.when(jnp.logical_and(dense, ctx))
    def _():
        kv = kvd(0, c_rows)
        fn(kv[:, :64], kv[:, 64:], None, False, [(0, c_rows)])

    @pl.when(jnp.logical_and(dense, jnp.logical_not(ctx)))
    def _():
        kv = kvd(0, t_rows)
        fn(kv[:, :64], kv[:, 64:], None, False, [(0, t_rows)])

    @pl.when(jnp.logical_and(jnp.logical_not(dense), ctx))
    def _():
        kv = kvd(0, c_rows)
        fn(kv[:, :64], kv[:, 64:], None, True, [(0, c_rows)])

    @pl.when(jnp.logical_and(jnp.logical_not(dense), jnp.logical_not(ctx)))
    def _():
        start = pl.multiple_of(jnp.minimum(c_rows + (t - 1) * TM - WINDOW, t_rows - wl), 128)
        kv = jnp.concatenate([kvd(0, c_rows), kvd(start, wl)], axis=0)
        rows = lax.broadcasted_iota(jnp.int32, (TM, c_rows + wl), 0)
        cols = lax.broadcasted_iota(jnp.int32, (TM, c_rows + wl), 1)
        qpos = (t - 1) * TM + rows
        kpos = start - 2 * c_rows + cols
        mask = jnp.logical_or(cols < c_rows, jnp.logical_and(jnp.abs(kpos - qpos) <= WINDOW, kpos >= 0))
        fn(kv[:, :64], kv[:, 64:], mask, True, [(0, c_rows), (start, wl)])


def _softmax(q, k, mask, sink):
    s = lax.dot_general(q, k, (((1,), (1,)), ((), ())), preferred_element_type=F32)
    if mask is not None:
        s = jnp.where(mask, s, NEG_BIG)
    mx = jnp.max(s, axis=1, keepdims=True)
    if sink is not None:
        mx = jnp.maximum(mx, sink)
    p = jnp.exp(s - mx)
    l = jnp.sum(p, axis=1, keepdims=True)
    ps = None
    if sink is not None:
        ps = jnp.exp(sink - mx)
        l = l + ps
    inv = 1.0 / l
    return p * inv, (None if ps is None else ps * inv)


def _attn_specs(t_rows):
    q_spec = pl.BlockSpec((None, TM, Q_WIDTH), lambda b, hg, t, s: (b, t, hg))
    kv_spec = pl.BlockSpec((None, t_rows, 128), lambda b, hg, t, s: (b, 0, hg))
    return q_spec, kv_spec


def _attn_fwd(q_all, kv_all, sink8, c_rows, name):
    bl, t_rows, _ = q_all.shape
    q_spec, kv_spec = _attn_specs(t_rows)

    def body(sink_ref, q_ref, kv_ref, o_ref):
        hg, t = pl.program_id(1), pl.program_id(2)

        def fn(k, v, mask, use_sink, spans):
            outs = []
            for g in range(GROUP):
                sink = sink_ref[jnp.maximum(hg - 2, 0) * GROUP + g] if use_sink else None
                p, _ = _softmax(q_ref[:, g * 64:(g + 1) * 64], k, mask, sink)
                outs.append(jnp.dot(p.astype(BF16), v, preferred_element_type=F32))
            o_ref[...] = jnp.concatenate(outs, axis=1).astype(BF16)

        _attn_case(t, hg, q_ref, kv_ref, c_rows, t_rows, fn)

    return pl.pallas_call(
        body, name=name,
        grid_spec=pltpu.PrefetchScalarGridSpec(
            num_scalar_prefetch=1, grid=(bl, N_HG, t_rows // TM), in_specs=[q_spec, kv_spec], out_specs=q_spec),
        out_shape=jax.ShapeDtypeStruct(q_all.shape, BF16),
        compiler_params=_params(("parallel", "parallel", "arbitrary")),
    )(sink8, q_all, kv_all)


def _attn_bwd(q_all, kv_all, do, sink8, c_rows, name):
    bl, t_rows, _ = q_all.shape
    q_spec, kv_spec = _attn_specs(t_rows)
    ds_spec = pl.BlockSpec((None, None, 8, 128), lambda b, hg, t, s: (b, hg, 0, 0))

    def body(sink_ref, q_ref, kv_ref, do_ref, dq_ref, dkv_ref, dsk_ref):
        hg, t = pl.program_id(1), pl.program_id(2)

        @pl.when(t == 0)
        def _():
            dkv_ref[...] = jnp.zeros_like(dkv_ref)
            dsk_ref[...] = jnp.zeros_like(dsk_ref)

        def fn(k, v, mask, use_sink, spans):
            dqs, dsinks = [], []
            dk = jnp.zeros(k.shape, F32)
            dv = jnp.zeros(v.shape, F32)
            for g in range(GROUP):
                sink = sink_ref[jnp.maximum(hg - 2, 0) * GROUP + g] if use_sink else None
                q = q_ref[:, g * 64:(g + 1) * 64]
                do_g = do_ref[:, g * 64:(g + 1) * 64]
                p, ps = _softmax(q, k, mask, sink)
                dp = lax.dot_general(do_g, v, (((1,), (1,)), ((), ())), preferred_element_type=F32)
                dd = jnp.sum(p * dp, axis=1, keepdims=True)
                ds = (p * (dp - dd)).astype(BF16)
                dqs.append(jnp.dot(ds, k, preferred_element_type=F32))
                dk = dk + lax.dot_general(ds, q, (((0,), (0,)), ((), ())), preferred_element_type=F32)
                dv = dv + lax.dot_general(p.astype(BF16), do_g, (((0,), (0,)), ((), ())), preferred_element_type=F32)
                if use_sink:
                    dsinks.append(jnp.broadcast_to(-jnp.sum(ps * dd, axis=0, keepdims=True), (1, 128)))
            dq_ref[...] = jnp.concatenate(dqs, axis=1)
            dkv = jnp.concatenate([dk, dv], axis=1)
            off = 0
            for start, size in spans:
                dkv_ref[pl.ds(start, size), :] += dkv[off:off + size]
                off += size
            if use_sink:
                dsk_ref[0:GROUP, :] += jnp.concatenate(dsinks, axis=0)

        _attn_case(t, hg, q_ref, kv_ref, c_rows, t_rows, fn)

    return pl.pallas_call(
        body, name=name,
        grid_spec=pltpu.PrefetchScalarGridSpec(
            num_scalar_prefetch=1, grid=(bl, N_HG, t_rows // TM), in_specs=[q_spec, kv_spec, q_spec],
            out_specs=[q_spec, kv_spec, ds_spec]),
        out_shape=[jax.ShapeDtypeStruct(q_all.shape, F32), jax.ShapeDtypeStruct(kv_all.shape, F32),
                   jax.ShapeDtypeStruct((bl, N_HG, 8, 128), F32)],
        compiler_params=_params(("parallel", "parallel", "arbitrary")),
    )(sink8, q_all, kv_all, do)


def _silu(x):
    return x * jax.nn.sigmoid(x)


def _ada_fwd(c_rows, w_ada, b_cols, name):
    nl, d, w = w_ada.shape
    r = c_rows.shape[0]

    def body(c_ref, w_ref, b_ref, o_ref):
        s = _silu(c_ref[...]).astype(BF16)
        o_ref[...] = jnp.dot(s, w_ref[...].astype(BF16), preferred_element_type=F32) + b_ref[...]

    return pl.pallas_call(
        body, name=name, grid=(nl,),
        in_specs=[pl.BlockSpec((r, d), lambda l: (0, 0)), pl.BlockSpec((None, d, w), lambda l: (l, 0, 0)),
                  pl.BlockSpec((None, 1, w), lambda l: (l, 0, 0))],
        out_specs=pl.BlockSpec((None, r, w), lambda l: (l, 0, 0)),
        out_shape=jax.ShapeDtypeStruct((nl, r, w), F32), compiler_params=_params(("parallel",)),
    )(c_rows, w_ada, b_cols)


def _ada_bwd(c_rows, c_ctx, dmod, w_ada, name):
    nl, d, w = w_ada.shape
    r = c_rows.shape[0]

    def body(c_ref, cc_ref, g_ref, w_ref, dw_ref, dc_ref):
        l = pl.program_id(0)
        s = _silu(c_ref[...]).astype(BF16)
        gm = g_ref[...].astype(BF16)
        dw_ref[...] = lax.dot_general(s, gm, (((0,), (0,)), ((), ())), preferred_element_type=F32)
        ds = lax.dot_general(gm, w_ref[...].astype(BF16), (((1,), (1,)), ((), ())), preferred_element_type=F32)
        rows = lax.broadcasted_iota(jnp.int32, ds.shape, 0)
        dsc = jnp.sum(jnp.where(rows % ADA_ROWS == 2, ds, 0.0), axis=0, keepdims=True)
        x = cc_ref[...]
        sg = jax.nn.sigmoid(x)
        dcc = dsc * (sg * (1.0 + x * (1.0 - sg)))
        out = jnp.concatenate([dcc, jnp.zeros((7, d), F32)], axis=0)

        @pl.when(l == 0)
        def _():
            dc_ref[...] = out

        @pl.when(l != 0)
        def _():
            dc_ref[...] += out

    return pl.pallas_call(
        body, name=name, grid=(nl,),
        in_specs=[pl.BlockSpec((r, d), lambda l: (0, 0)), pl.BlockSpec((1, d), lambda l: (0, 0)),
                  pl.BlockSpec((None, r, w), lambda l: (l, 0, 0)), pl.BlockSpec((None, d, w), lambda l: (l, 0, 0))],
        out_specs=[pl.BlockSpec((None, d, w), lambda l: (l, 0, 0)), pl.BlockSpec((8, d), lambda l: (0, 0))],
        out_shape=[jax.ShapeDtypeStruct((nl, d, w), F32), jax.ShapeDtypeStruct((8, d), F32)],
        compiler_params=_params(("arbitrary",)),
    )(c_rows, c_ctx, dmod, w_ada)


def _adam_math(w, g, m, v):
    m = ADAM_B1 * m + (1.0 - ADAM_B1) * g
    v = ADAM_B2 * v + (1.0 - ADAM_B2) * (g * g)
    m_hat = m / (1.0 - ADAM_B1 ** ADAM_STEP)
    v_hat = v / (1.0 - ADAM_B2 ** ADAM_STEP)
    delta = -ADAM_LR * (m_hat / (jnp.sqrt(v_hat) + ADAM_EPS) + ADAM_WD * w)
    return delta, m, v


def _adamw(w, m, v, g_own, g_recv, name, rows=256):
    nl, r, c = w.shape
    tr = min(rows, r)
    spec = pl.BlockSpec((None, tr, c), lambda l, i: (l, i, 0))
    in_specs = [spec] * 4
    args = [w, m, v, g_own]
    if g_recv is not None:
        in_specs.append(pl.BlockSpec((N_DEV, None, tr, c), lambda l, i: (0, l, i, 0)))
        args.append(g_recv)

    def body(*refs):
        w_ref, m_ref, v_ref, g_ref = refs[:4]
        g = g_ref[...]
        if g_recv is not None:
            for k in range(1, N_DEV):
                g = g + refs[4][k].astype(F32)
        go_ref, d_ref, mo_ref, vo_ref = refs[-4:]
        delta, m_, v_ = _adam_math(w_ref[...], g, m_ref[...], v_ref[...])
        go_ref[...] = g
        d_ref[...] = delta
        mo_ref[...] = m_
        vo_ref[...] = v_

    return pl.pallas_call(
        body, name=name, grid=(nl, r // tr), in_specs=in_specs, out_specs=[spec] * 4,
        out_shape=[jax.ShapeDtypeStruct(w.shape, F32)] * 4, compiler_params=_params(("parallel", "parallel")),
    )(*args)


def _small_adamw(w, m, v, g_all, name):
    def body(w_ref, m_ref, v_ref, g_ref, go_ref, d_ref, mo_ref, vo_ref):
        g = g_ref[0]
        for k in range(1, N_DEV):
            g = g + g_ref[k]
        delta, m_, v_ = _adam_math(w_ref[...], g, m_ref[...], v_ref[...])
        go_ref[...] = g
        d_ref[...] = delta
        mo_ref[...] = m_
        vo_ref[...] = v_

    return pl.pallas_call(
        body, name=name, out_shape=[jax.ShapeDtypeStruct(w.shape, F32)] * 4, compiler_params=_params(),
    )(w, m, v, g_all)


SMALL = ("c_ctx", "b_ada", "g_pre_mix", "g_post_mix", "g_pre_mlp", "g_post_mlp", "q_norm", "k_norm", "sink")


def _pack_small(parts):
    flat = jnp.concatenate([parts[n].reshape(-1) for n in SMALL])
    rows = -(-flat.shape[0] // 1024) * 8
    return jnp.pad(flat, (0, rows * 128 - flat.shape[0])).reshape(rows, 128)


def _unpack_small(packed, like):
    flat = packed.reshape(-1)
    out, off = {}, 0
    for n in SMALL:
        size = int(np.prod(like[n].shape))
        out[n] = flat[off:off + size].reshape(like[n].shape)
        off += size
    return out


def kernel(x, c, ctx, c_ctx, w_ada, b_ada, g_pre_mix, g_post_mix, g_pre_mlp, g_post_mlp, w_in, q_norm, k_norm, sink, w_out, w_up, w_down, loss_target, m_c_ctx, m_w_ada, m_b_ada, m_g_pre_mix, m_g_post_mix, m_g_pre_mlp, m_g_post_mlp, m_w_in, m_q_norm, m_k_norm, m_sink, m_w_out, m_w_up, m_w_down, v_c_ctx, v_w_ada, v_b_ada, v_g_pre_mix, v_g_post_mix, v_g_pre_mlp, v_g_post_mlp, v_w_in, v_q_norm, v_k_norm, v_sink, v_w_out, v_w_up, v_w_down):
    bl, s_rows, d = x.shape
    c_rows = ctx.shape[1]
    assert c_rows == TM and s_rows % TM == 0 and bl == 2
    t_rows = c_rows + s_rows
    nt = t_rows // TM
    m_rows = bl * t_rows
    nl = w_in.shape[0]
    ada_w = w_ada.shape[2]
    d_ff = w_up.shape[2] * N_DEV
    me = _my_index()

    w_in_g = _all_gather(w_in.astype(BF16), "gather_w_in")
    w_out_g = _all_gather(w_out.astype(BF16), "gather_w_out")
    w_up_g = _all_gather(w_up.astype(BF16), "gather_w_up")
    w_down_g = _all_gather(w_down.astype(BF16), "gather_w_down")
    w_in_f = [w_in_g[:, l].transpose(1, 0, 2).reshape(d, PROJ_W) for l in range(nl)]
    w_out_f = [w_out_g[:, l].reshape(-1, d) for l in range(nl)]
    w_up_s = [w_up_g[:, l] for l in range(nl)]
    w_down_f = [w_down_g[:, l].reshape(d_ff, d) for l in range(nl)]

    c_pad = jnp.concatenate([c, c_ctx[None, :], jnp.zeros((ADA_ROWS - bl - 1, d), F32)], axis=0)
    c_all = _all_gather(c_pad, "gather_c").reshape(N_DEV * ADA_ROWS, d)
    b_cols = lax.dynamic_slice(b_ada, (0, me * ada_w), (nl, ada_w))[:, None, :]
    mod_cols = _ada_fwd(c_all, w_ada, b_cols, "ada_fwd")
    mod_g = _all_gather(mod_cols, "gather_mod")
    mod_full = mod_g.transpose(1, 2, 0, 3).reshape(nl, N_DEV * ADA_ROWS, N_MOD, d)
    mine = lax.dynamic_slice(mod_full, (0, me * ADA_ROWS, 0, 0), (nl, bl + 1, N_MOD, d))
    pad = jnp.zeros((bl, 2, MOD_ROWS - N_MOD, d), F32)
    modsel = [jnp.concatenate([jnp.stack([jnp.broadcast_to(mine[l, bl], (bl, N_MOD, d)), mine[l, :bl]], axis=1), pad],
                              axis=2) for l in range(nl)]

    cos, sin = _rope_tables(t_rows, c_rows)
    emat = _head_mean_matrix()
    row = lambda a: a[None, :]
    qn = [jnp.tile(q_norm[l], 8)[None, :] for l in range(nl)]
    kn = [jnp.tile(k_norm[l], 2)[None, :] for l in range(nl)]

    h = jnp.concatenate([ctx, x], axis=1).reshape(m_rows, d)
    saved = []
    for l in range(nl):
        u = _norm_mod_fwd(h, row(g_pre_mix[l]), modsel[l], 0, 1, nt, f"mix_mod_fwd{l}")
        proj = _mm(u, w_in_f[l], name=f"mm_in{l}", tk=d)
        q_all, kv_all = _prep_fwd(proj, qn[l], kn[l], cos, sin, emat, nt, f"prep_fwd{l}")
        o = _attn_fwd(q_all.reshape(bl, t_rows, 1024), kv_all.reshape(bl, t_rows, 512), sink[l], c_rows,
                      f"attn_fwd{l}").reshape(m_rows, 1024)
        mix = _mm(o, w_out_f[l], name=f"mm_out{l}", tk=1024)
        h_mid = _gate_fwd(h, mix, row(g_post_mix[l]), modsel[l], 2, nt, f"mix_gate_fwd{l}")
        v_in = _norm_mod_fwd(h_mid, row(g_pre_mlp[l]), modsel[l], 3, 4, nt, f"mlp_mod_fwd{l}")
        r_act, s_act = _mm(v_in, w_up_s[l], name=f"mm_up{l}", b_mode="nn_slots", epilogue="relu2", tk=d)
        y = _mm(r_act, w_down_f[l], name=f"mm_down{l}", tk=1024)
        h_out = _gate_fwd(h_mid, y, row(g_post_mlp[l]), modsel[l], 5, nt, f"mlp_gate_fwd{l}")
        saved.append((h, u, proj, q_all, kv_all, o, mix, h_mid, v_in, r_act, s_act, y))
        h = h_out

    dh, ss = _loss_grad(h, loss_target.reshape(bl * s_rows, d), c_rows, nt, "loss_grad")
    loss = lax.psum(0.5 * ss[0, 0] / d, MESH_AXES)

    small_g = {n: [None] * nl for n in SMALL if n not in ("c_ctx", "b_ada")}
    dmod_rows = []
    big_g = {"w_in": [], "w_out": [], "w_up": [], "w_down": []}
    for l in reversed(range(nl)):
        h_in, u, proj, q_all, kv_all, o, mix, h_mid, v_in, r_act, s_act, y = saved[l]
        dy, dvec_g2 = _gate_bwd(y, row(g_post_mlp[l]), modsel[l], dh, 5, nt, f"mlp_gate_bwd{l}")
        da = _mm(dy, w_down_f[l], name=f"mm_da{l}", b_mode="nt", epilogue="mul", extra=s_act, out_dtype=BF16, tk=d)
        dw_down = _mm(r_act, dy, name=f"mm_dw_down{l}", ta=True, tk=512)
        dw_up = _mm(v_in, da, name=f"mm_dw_up{l}", ta=True, out_mode="slots", tk=512)
        dv = _mm(da, w_up_s[l], name=f"mm_dv{l}", b_mode="nt_slots")
        dh, dvec_m2 = _norm_mod_bwd(h_mid, row(g_pre_mlp[l]), modsel[l], dv, dh, 3, 4, nt, f"mlp_mod_bwd{l}")
        dmix, dvec_g1 = _gate_bwd(mix, row(g_post_mix[l]), modsel[l], dh, 2, nt, f"mix_gate_bwd{l}")
        do = _mm(dmix, w_out_f[l], name=f"mm_do{l}", b_mode="nt", out_dtype=BF16, tk=d)
        dw_out = _mm(o, dmix, name=f"mm_dw_out{l}", ta=True, tk=512)
        dq, dkv, dsk = _attn_bwd(q_all.reshape(bl, t_rows, 1024), kv_all.reshape(bl, t_rows, 512),
                                 do.reshape(bl, t_rows, 1024), sink[l], c_rows, f"attn_bwd{l}")
        dproj, dqn, dkn = _prep_bwd(proj, dq.reshape(m_rows, 1024), dkv.reshape(m_rows, 512), qn[l], kn[l], cos, sin,
                                    emat, nt, f"prep_bwd{l}")
        dw_in = _mm(u, dproj, name=f"mm_dw_in{l}", ta=True, tk=512)
        du = _mm(dproj, w_in_f[l], name=f"mm_du{l}", b_mode="nt", tk=PROJ_W)
        dh, dvec_m1 = _norm_mod_bwd(h_in, row(g_pre_mix[l]), modsel[l], du, dh, 0, 1, nt, f"mix_mod_bwd{l}")

        small_g["g_pre_mix"][l] = jnp.sum(dvec_m1[:, :, 2], axis=(0, 1))
        small_g["g_post_mix"][l] = jnp.sum(dvec_g1[:, :, 1], axis=(0, 1))
        small_g["g_pre_mlp"][l] = jnp.sum(dvec_m2[:, :, 2], axis=(0, 1))
        small_g["g_post_mlp"][l] = jnp.sum(dvec_g2[:, :, 1], axis=(0, 1))
        small_g["q_norm"][l] = jnp.sum(dqn[0].reshape(8, HEAD_DIM), axis=0)
        small_g["k_norm"][l] = jnp.sum(dkn[0].reshape(2, HEAD_DIM), axis=0)
        small_g["sink"][l] = jnp.sum(dsk[:, 2:, :GROUP, 0], axis=0).reshape(-1)
        dms = jnp.stack([dvec_m1[:, :, 0], dvec_m1[:, :, 1], dvec_g1[:, :, 0],
                         dvec_m2[:, :, 0], dvec_m2[:, :, 1], dvec_g2[:, :, 0]], axis=2)
        rows = jnp.concatenate([dms[:, 1], jnp.sum(dms[:, 0], axis=0)[None]], axis=0)
        dmod_rows.append(jnp.pad(rows.reshape(bl + 1, N_MOD * d), ((0, ADA_ROWS - bl - 1), (0, 0))))
        big_g["w_in"].append(dw_in.reshape(d, N_DEV, PROJ_W // N_DEV).transpose(1, 0, 2))
        big_g["w_out"].append(dw_out.reshape(N_DEV, -1, d))
        big_g["w_up"].append(dw_up)
        big_g["w_down"].append(dw_down.reshape(N_DEV, -1, d))
    dmod_rows = dmod_rows[::-1]
    grad_x = dh.reshape(bl, t_rows, d)[:, c_rows:]

    dmod_g = _all_gather(jnp.stack(dmod_rows), "gather_dmod")
    dmod_mine = lax.dynamic_slice(dmod_g, (0, 0, 0, me * ada_w), (N_DEV, nl, ADA_ROWS, ada_w))
    dmod_mine = dmod_mine.transpose(1, 0, 2, 3).reshape(nl, N_DEV * ADA_ROWS, ada_w)
    dw_ada, dcc = _ada_bwd(c_all, c_ctx[None, :], dmod_mine, w_ada, "ada_bwd")

    parts = {n: jnp.stack(small_g[n]) for n in small_g}
    parts["c_ctx"] = dcc[0]
    parts["b_ada"] = jnp.stack([jnp.sum(r_[: bl + 1], axis=0) for r_ in dmod_rows])
    weights = dict(c_ctx=c_ctx, b_ada=b_ada, g_pre_mix=g_pre_mix, g_post_mix=g_post_mix, g_pre_mlp=g_pre_mlp,
                   g_post_mlp=g_post_mlp, q_norm=q_norm, k_norm=k_norm, sink=sink)
    moms = dict(c_ctx=m_c_ctx, b_ada=m_b_ada, g_pre_mix=m_g_pre_mix, g_post_mix=m_g_post_mix, g_pre_mlp=m_g_pre_mlp,
                g_post_mlp=m_g_post_mlp, q_norm=m_q_norm, k_norm=m_k_norm, sink=m_sink)
    vels = dict(c_ctx=v_c_ctx, b_ada=v_b_ada, g_pre_mix=v_g_pre_mix, g_post_mix=v_g_post_mix, g_pre_mlp=v_g_pre_mlp,
                g_post_mlp=v_g_post_mlp, q_norm=v_q_norm, k_norm=v_k_norm, sink=v_sink)
    small_all = _all_gather(_pack_small(parts), "gather_small")
    s_out = _small_adamw(_pack_small(weights), _pack_small(moms), _pack_small(vels), small_all, "adamw_small")
    s_g, s_d, s_m, s_v = [_unpack_small(a, weights) for a in s_out]

    res = {}
    for n, w_, m_, v_ in (("w_in", w_in, m_w_in, v_w_in), ("w_out", w_out, m_w_out, v_w_out),
                          ("w_up", w_up, m_w_up, v_w_up), ("w_down", w_down, m_w_down, v_w_down)):
        slots = jnp.stack(big_g[n][::-1], axis=1)
        recv = _exchange(slots.astype(BF16), f"exchange_{n}")
        own = lax.dynamic_index_in_dim(slots, me, axis=0, keepdims=False)
        res[n] = _adamw(w_, m_, v_, own, recv, f"adamw_{n}")
    res["w_ada"] = _adamw(w_ada, m_w_ada, v_w_ada, dw_ada, None, "adamw_w_ada")

    order = ("c_ctx", "w_ada", "b_ada", "g_pre_mix", "g_post_mix", "g_pre_mlp", "g_post_mlp", "w_in", "q_norm",
             "k_norm", "sink", "w_out", "w_up", "w_down")
    outs = [loss, grad_x]
    for i, small in enumerate((s_g, s_d, s_m, s_v)):
        outs += [small[n] if n in small else res[n][i] for n in order]
    return tuple(outs)
```

```python
import functools

import jax
import jax.numpy as jnp
import numpy as np
from jax import lax
from jax.experimental import pallas as pl
from jax.experimental.pallas import tpu as pltpu

F32 = jnp.float32
BF16 = jnp.bfloat16

HEAD_DIM = 64
GROUP = 4
N_HG = 4
Q_WIDTH = GROUP * HEAD_DIM
WINDOW = 128
GRID_W = 64
ROPE_THETA = 10000.0
EPS = 1e-6
NEG_BIG = -1e30
N_MOD = 6
MOD_ROWS = 8
TM = 256
N_DEV = 8
ADA_ROWS = 8
VMEM_LIMIT = 56 * 1024 * 1024

ADAM_LR = 0.001
ADAM_B1 = 0.9
ADAM_B2 = 0.999
ADAM_EPS = 1e-08
ADAM_WD = 0.01
ADAM_STEP = 10

MESH_AXES = ("x", "y", "c")


def _params(sem=None):
    kw = dict(vmem_limit_bytes=VMEM_LIMIT)
    if sem is not None:
        kw["dimension_semantics"] = sem
    return pltpu.CompilerParams(**kw)


def _my_index():
    return 4 * lax.axis_index("x") + 2 * lax.axis_index("y") + lax.axis_index("c")


def _peer(k):
    x, y, c = lax.axis_index("x"), lax.axis_index("y"), lax.axis_index("c")
    kx, ky, kc = (k >> 2) & 1, (k >> 1) & 1, k & 1
    px = (1 - x) if kx else x
    py = (1 - y) if ky else y
    pc = (1 - c) if kc else c
    return (px, py, pc), 4 * px + 2 * py + pc


def _all_gather(shard, name):
    def body(x_ref, out_ref, send_sems, recv_sems, local_sem):
        me = _my_index()
        mine = pltpu.make_async_copy(x_ref, out_ref.at[me], local_sem)
        mine.start()
        sends = []
        for k in range(1, N_DEV):
            peer, _ = _peer(k)
            cp = pltpu.make_async_remote_copy(
                src_ref=x_ref, dst_ref=out_ref.at[me], send_sem=send_sems.at[k], recv_sem=recv_sems.at[k],
                device_id=peer, device_id_type=pl.DeviceIdType.MESH)
            cp.start()
            sends.append(cp)
        for k in range(1, N_DEV):
            peer, pidx = _peer(k)
            pltpu.make_async_remote_copy(
                src_ref=x_ref, dst_ref=out_ref.at[pidx], send_sem=send_sems.at[k], recv_sem=recv_sems.at[k],
                device_id=peer, device_id_type=pl.DeviceIdType.MESH).wait_recv()
        for cp in sends:
            cp.wait_send()
        mine.wait()

    return pl.pallas_call(
        body, name=name,
        out_shape=jax.ShapeDtypeStruct((N_DEV,) + shard.shape, shard.dtype),
        in_specs=[pl.BlockSpec(memory_space=pl.ANY)],
        out_specs=pl.BlockSpec(memory_space=pl.ANY),
        scratch_shapes=[pltpu.SemaphoreType.DMA((N_DEV,)), pltpu.SemaphoreType.DMA((N_DEV,)), pltpu.SemaphoreType.DMA],
    )(shard)


def _exchange(slots, name):
    def body(x_ref, out_ref, send_sems, recv_sems, local_sem):
        me = _my_index()
        mine = pltpu.make_async_copy(x_ref.at[me], out_ref.at[0], local_sem)
        mine.start()
        sends = []
        for k in range(1, N_DEV):
            peer, pidx = _peer(k)
            cp = pltpu.make_async_remote_copy(
                src_ref=x_ref.at[pidx], dst_ref=out_ref.at[k], send_sem=send_sems.at[k], recv_sem=recv_sems.at[k],
                device_id=peer, device_id_type=pl.DeviceIdType.MESH)
            cp.start()
            sends.append(cp)
        for cp in sends:
            cp.wait_recv()
        for cp in sends:
            cp.wait_send()
        mine.wait()

    return pl.pallas_call(
        body, name=name,
        out_shape=jax.ShapeDtypeStruct(slots.shape, slots.dtype),
        in_specs=[pl.BlockSpec(memory_space=pl.ANY)],
        out_specs=pl.BlockSpec(memory_space=pl.ANY),
        scratch_shapes=[pltpu.SemaphoreType.DMA((N_DEV,)), pltpu.SemaphoreType.DMA((N_DEV,)), pltpu.SemaphoreType.DMA],
    )(slots)


def _mm(a, b, *, name, ta=False, b_mode="nn", out_mode="plain", out_dtype=F32, tm=512, tn=512, tk=512,
        epilogue=None, extra=None):
    if ta:
        kdim, m = a.shape
    else:
        m, kdim = a.shape
    if b_mode == "nn":
        n = b.shape[1]
    elif b_mode == "nt":
        n = b.shape[0]
    elif b_mode == "nn_slots":
        n = b.shape[0] * b.shape[2]
        tn = b.shape[2]
    else:
        n = b.shape[1]
        tk = b.shape[2]
    if out_mode == "slots":
        tn = n // N_DEV
    tm, tn, tk = min(tm, m), min(tn, n), min(tk, kdim)
    assert m % tm == 0 and n % tn == 0 and kdim % tk == 0, (name, m, n, kdim, tm, tn, tk)
    nk = kdim // tk

    a_spec = pl.BlockSpec((tk, tm), lambda i, j, k: (k, i)) if ta else pl.BlockSpec((tm, tk), lambda i, j, k: (i, k))
    if b_mode == "nn":
        b_spec = pl.BlockSpec((tk, tn), lambda i, j, k: (k, j))
    elif b_mode == "nt":
        b_spec = pl.BlockSpec((tn, tk), lambda i, j, k: (j, k))
    elif b_mode == "nn_slots":
        b_spec = pl.BlockSpec((None, tk, tn), lambda i, j, k: (j, k, 0))
    else:
        b_spec = pl.BlockSpec((None, tn, tk), lambda i, j, k: (k, j, 0))
    tb = b_mode in ("nt", "nt_slots")
    if out_mode == "plain":
        o_shape, o_spec = (m, n), pl.BlockSpec((tm, tn), lambda i, j, k: (i, j))
    else:
        o_shape, o_spec = (N_DEV, m, tn), pl.BlockSpec((None, tm, tn), lambda i, j, k: (j, i, 0))
    dims = (((0 if ta else 1,), (1 if tb else 0,)), ((), ()))

    in_specs = [a_spec, b_spec]
    args = [a, b]
    if epilogue == "mul":
        in_specs.append(pl.BlockSpec((tm, tn), lambda i, j, k: (i, j)))
        args.append(extra)
    if epilogue == "relu2":
        out_shape = [jax.ShapeDtypeStruct(o_shape, BF16)] * 2
        out_specs = [o_spec, o_spec]
    else:
        out_shape = jax.ShapeDtypeStruct(o_shape, out_dtype)
        out_specs = o_spec

    def finish(refs, acc):
        if epilogue == "relu2":
            r = jnp.maximum(acc, 0.0)
            refs[2][...] = (r * r).astype(BF16)
            refs[3][...] = (2.0 * r).astype(BF16)
        elif epilogue == "mul":
            refs[3][...] = (acc * refs[2][...].astype(F32)).astype(out_dtype)
        else:
            refs[2][...] = acc.astype(out_dtype)

    def body(*refs):
        part = lax.dot_general(refs[0][...], refs[1][...], dims, preferred_element_type=F32)
        if nk == 1:
            finish(refs, part)
            return
        acc_ref = refs[-1]
        k = pl.program_id(2)

        @pl.when(k == 0)
        def _():
            acc_ref[...] = part

        @pl.when(jnp.logical_and(k > 0, k < nk - 1))
        def _():
            acc_ref[...] += part

        @pl.when(k == nk - 1)
        def _():
            finish(refs, acc_ref[...] + part)

    return pl.pallas_call(
        body, name=name, grid=(m // tm, n // tn, nk),
        in_specs=in_specs, out_specs=out_specs, out_shape=out_shape,
        scratch_shapes=[] if nk == 1 else [pltpu.VMEM((tm, tn), F32)],
        compiler_params=_params(("parallel", "parallel", "arbitrary")),
    )(*args)


def _row_specs(d, nt):
    row = pl.BlockSpec((TM, d), lambda t: (t, 0))
    vec = pl.BlockSpec((1, d), lambda t: (0, 0))
    mod = pl.BlockSpec((None, None, MOD_ROWS, d), lambda t: (t // nt, jnp.minimum(t % nt, 1), 0, 0))
    return row, vec, mod


def _rms(x):
    r = lax.rsqrt(jnp.mean(x * x, axis=1, keepdims=True) + EPS)
    return r, x * r


def _norm_mod_fwd(h, g, modsel, i_sh, i_sc, nt, name):
    m, d = h.shape
    row, vec, mod = _row_specs(d, nt)

    def body(h_ref, g_ref, ms_ref, u_ref):
        _, xh = _rms(h_ref[...])
        ms = ms_ref[...]
        u_ref[...] = (xh * g_ref[...] * (1.0 + ms[i_sc:i_sc + 1]) + ms[i_sh:i_sh + 1]).astype(BF16)

    return pl.pallas_call(
        body, name=name, grid=(m // TM,), in_specs=[row, vec, mod], out_specs=row,
        out_shape=jax.ShapeDtypeStruct((m, d), BF16), compiler_params=_params(("parallel",)),
    )(h, g, modsel)


def _acc_rows(t, nt, dvec_ref, rows):
    first = (t % nt) <= 1

    @pl.when(first)
    def _():
        dvec_ref[...] = rows

    @pl.when(jnp.logical_not(first))
    def _():
        dvec_ref[...] += rows


def _norm_mod_bwd(h, g, modsel, du, dh_in, i_sh, i_sc, nt, name):
    m, d = h.shape
    row, vec, mod = _row_specs(d, nt)

    def body(h_ref, g_ref, ms_ref, du_ref, dhi_ref, dh_ref, dvec_ref):
        t = pl.program_id(0)
        r, xh = _rms(h_ref[...])
        g_ = g_ref[...]
        ms = ms_ref[...]
        du_ = du_ref[...]
        y = xh * g_
        dy = du_ * (1.0 + ms[i_sc:i_sc + 1])
        dxh = dy * g_
        dx = r * (dxh - xh * jnp.mean(dxh * xh, axis=1, keepdims=True))
        dh_ref[...] = dhi_ref[...] + dx
        rows = jnp.concatenate([
            jnp.sum(du_, axis=0, keepdims=True), jnp.sum(du_ * y, axis=0, keepdims=True),
            jnp.sum(dy * xh, axis=0, keepdims=True), jnp.zeros((MOD_ROWS - 3, d), F32)], axis=0)
        _acc_rows(t, nt, dvec_ref, rows)

    return pl.pallas_call(
        body, name=name, grid=(m // TM,), in_specs=[row, vec, mod, row, row], out_specs=[row, mod],
        out_shape=[jax.ShapeDtypeStruct((m, d), F32), jax.ShapeDtypeStruct(modsel.shape, F32)],
        compiler_params=_params(("arbitrary",)),
    )(h, g, modsel, du, dh_in)


def _gate_fwd(h, z, g, modsel, i_g, nt, name):
    m, d = h.shape
    row, vec, mod = _row_specs(d, nt)

    def body(h_ref, z_ref, g_ref, ms_ref, o_ref):
        _, xh = _rms(z_ref[...])
        ms = ms_ref[...]
        o_ref[...] = h_ref[...] + ms[i_g:i_g + 1] * (xh * g_ref[...])

    return pl.pallas_call(
        body, name=name, grid=(m // TM,), in_specs=[row, row, vec, mod], out_specs=row,
        out_shape=jax.ShapeDtypeStruct((m, d), F32), compiler_params=_params(("parallel",)),
    )(h, z, g, modsel)


def _gate_bwd(z, g, modsel, dh, i_g, nt, name):
    m, d = z.shape
    row, vec, mod = _row_specs(d, nt)

    def body(z_ref, g_ref, ms_ref, dh_ref, dz_ref, dvec_ref):
        t = pl.program_id(0)
        r, xh = _rms(z_ref[...])
        g_ = g_ref[...]
        ms = ms_ref[...]
        dh_ = dh_ref[...]
        dy = dh_ * ms[i_g:i_g + 1]
        dxh = dy * g_
        dz_ref[...] = (r * (dxh - xh * jnp.mean(dxh * xh, axis=1, keepdims=True))).astype(BF16)
        rows = jnp.concatenate([
            jnp.sum(dh_ * (xh * g_), axis=0, keepdims=True), jnp.sum(dy * xh, axis=0, keepdims=True),
            jnp.zeros((MOD_ROWS - 2, d), F32)], axis=0)
        _acc_rows(t, nt, dvec_ref, rows)

    return pl.pallas_call(
        body, name=name, grid=(m // TM,), in_specs=[row, vec, mod, row], out_specs=[row, mod],
        out_shape=[jax.ShapeDtypeStruct((m, d), BF16), jax.ShapeDtypeStruct(modsel.shape, F32)],
        compiler_params=_params(("arbitrary",)),
    )(z, g, modsel, dh)


def _loss_grad(h, target, c_rows, nt, name):
    m, d = h.shape
    row = pl.BlockSpec((TM, d), lambda t: (t, 0))
    ntl = nt - 1
    tgt = pl.BlockSpec((TM, d), lambda t: ((t // nt) * ntl + jnp.maximum(t % nt, 1) - 1, 0))
    acc = pl.BlockSpec((8, 128), lambda t: (0, 0))

    def body(h_ref, t_ref, dh_ref, ss_ref):
        t = pl.program_id(0)

        @pl.when(t == 0)
        def _():
            ss_ref[...] = jnp.zeros_like(ss_ref)

        @pl.when(t % nt == 0)
        def _():
            dh_ref[...] = jnp.zeros_like(dh_ref)

        @pl.when(t % nt != 0)
        def _():
            e = h_ref[...] - t_ref[...]
            dh_ref[...] = e * (1.0 / d)
            ss_ref[...] += jnp.sum(e * e)

    return pl.pallas_call(
        body, name=name, grid=(m // TM,), in_specs=[row, tgt], out_specs=[row, acc],
        out_shape=[jax.ShapeDtypeStruct((m, d), F32), jax.ShapeDtypeStruct((8, 128), F32)],
        compiler_params=_params(("arbitrary",)),
    )(h, target)


QA, KA, VA, QB, KB, VB = 0, 512, 640, 768, 1280, 1408
PROJ_W = 1536
Q_SCALE = HEAD_DIM ** -0.5


def _swap16(x):
    lane = lax.broadcasted_iota(jnp.int32, x.shape, 1)
    n = x.shape[1]
    return jnp.where((lane % 32) < 16, pltpu.roll(x, n - 16, 1), pltpu.roll(x, 16, 1))


def _seg_mean(x, e):
    return jnp.dot(x, e, preferred_element_type=F32, precision=lax.Precision.HIGHEST)


def _rope_tables(t_rows, c_rows):
    s = t_rows - c_rows
    row_ids = jnp.repeat(jnp.arange(s // GRID_W, dtype=jnp.int32), GRID_W).astype(F32)
    col_ids = jnp.tile(jnp.arange(GRID_W, dtype=jnp.int32), s // GRID_W).astype(F32)
    axis_dim = HEAD_DIM // 2
    inv = ROPE_THETA ** (-jnp.arange(0, axis_dim, 2, dtype=F32) / axis_dim)
    ang_r = row_ids[:, None] * inv[None, :]
    ang_c = col_ids[:, None] * inv[None, :]
    cos = jnp.concatenate([jnp.cos(ang_r), jnp.cos(ang_r), jnp.cos(ang_c), jnp.cos(ang_c)], axis=1)
    sin = jnp.concatenate([-jnp.sin(ang_r), jnp.sin(ang_r), -jnp.sin(ang_c), jnp.sin(ang_c)], axis=1)
    cos = jnp.concatenate([jnp.ones((c_rows, HEAD_DIM), F32), cos], axis=0)
    sin = jnp.concatenate([jnp.zeros((c_rows, HEAD_DIM), F32), sin], axis=0)
    return jnp.tile(cos, (1, 8)), jnp.tile(sin, (1, 8))


def _head_mean_matrix():
    i = np.arange(512)
    return jnp.asarray((i[:, None] // HEAD_DIM == i[None, :] // HEAD_DIM).astype(np.float32) / HEAD_DIM)


def _interleave_kv(k, v):
    return jnp.concatenate([k[:, :64], v[:, :64], k[:, 64:], v[:, 64:]], axis=1)


def _prep_fwd(proj, qn, kn, cos, sin, emat, nt, name):
    m = proj.shape[0]
    specs = [
        pl.BlockSpec((TM, PROJ_W), lambda t: (t, 0)),
        pl.BlockSpec((1, 512), lambda t: (0, 0)), pl.BlockSpec((1, 128), lambda t: (0, 0)),
        pl.BlockSpec((TM, 512), lambda t: (t % nt, 0)), pl.BlockSpec((TM, 512), lambda t: (t % nt, 0)),
        pl.BlockSpec((512, 512), lambda t: (0, 0)),
    ]

    def body(p_ref, qn_ref, kn_ref, cos_ref, sin_ref, e_ref, q_ref, kv_ref):
        cos_, sin_, e = cos_ref[...], sin_ref[...], e_ref[...]

        def rope(x, w):
            return x * cos_[:, :w] + _swap16(x) * sin_[:, :w]

        def norm(x, g, w):
            return x * lax.rsqrt(_seg_mean(x * x, e[:w, :w]) + EPS) * g

        qa = rope(norm(p_ref[:, QA:QA + 512], qn_ref[...], 512), 512)
        qb = rope(p_ref[:, QB:QB + 512], 512)
        q_ref[:, 0:512] = (qa * Q_SCALE).astype(BF16)
        q_ref[:, 512:1024] = (qb * Q_SCALE).astype(BF16)
        ka = rope(norm(p_ref[:, KA:KA + 128], kn_ref[...], 128), 128)
        kb = rope(p_ref[:, KB:KB + 128], 128)
        kv_ref[:, 0:256] = _interleave_kv(ka, p_ref[:, VA:VA + 128]).astype(BF16)
        kv_ref[:, 256:512] = _interleave_kv(kb, p_ref[:, VB:VB + 128]).astype(BF16)

    return pl.pallas_call(
        body, name=name, grid=(m // TM,), in_specs=specs,
        out_specs=[pl.BlockSpec((TM, 1024), lambda t: (t, 0)), pl.BlockSpec((TM, 512), lambda t: (t, 0))],
        out_shape=[jax.ShapeDtypeStruct((m, 1024), BF16), jax.ShapeDtypeStruct((m, 512), BF16)],
        compiler_params=_params(("parallel",)),
    )(proj, qn, kn, cos, sin, emat)


def _prep_bwd(proj, dq, dkv, qn, kn, cos, sin, emat, nt, name):
    m = proj.shape[0]
    specs = [
        pl.BlockSpec((TM, PROJ_W), lambda t: (t, 0)),
        pl.BlockSpec((TM, 1024), lambda t: (t, 0)), pl.BlockSpec((TM, 512), lambda t: (t, 0)),
        pl.BlockSpec((1, 512), lambda t: (0, 0)), pl.BlockSpec((1, 128), lambda t: (0, 0)),
        pl.BlockSpec((TM, 512), lambda t: (t % nt, 0)), pl.BlockSpec((TM, 512), lambda t: (t % nt, 0)),
        pl.BlockSpec((512, 512), lambda t: (0, 0)),
    ]

    def body(p_ref, dq_ref, dkv_ref, qn_ref, kn_ref, cos_ref, sin_ref, e_ref, dp_ref, dqn_ref, dkn_ref):
        t = pl.program_id(0)
        cos_, sin_, e = cos_ref[...], sin_ref[...], e_ref[...]

        @pl.when(t == 0)
        def _():
            dqn_ref[...] = jnp.zeros_like(dqn_ref)
            dkn_ref[...] = jnp.zeros_like(dkn_ref)

        def unrope(dy, w):
            return dy * cos_[:, :w] + _swap16(dy * sin_[:, :w])

        def norm_bwd(x, g, dy, w):
            r = lax.rsqrt(_seg_mean(x * x, e[:w, :w]) + EPS)
            xh = x * r
            dxh = dy * g
            dx = r * (dxh - xh * _seg_mean(dxh * xh, e[:w, :w]))
            return dx, jnp.sum(dy * xh, axis=0, keepdims=True)

        dqa, dgq = norm_bwd(p_ref[:, QA:QA + 512], qn_ref[...], unrope(dq_ref[:, 0:512] * Q_SCALE, 512), 512)
        dp_ref[:, QA:QA + 512] = dqa.astype(BF16)
        dp_ref[:, QB:QB + 512] = unrope(dq_ref[:, 512:1024] * Q_SCALE, 512).astype(BF16)
        da = dkv_ref[:, 0:256]
        db = dkv_ref[:, 256:512]
        dka = jnp.concatenate([da[:, 0:64], da[:, 128:192]], axis=1)
        dva = jnp.concatenate([da[:, 64:128], da[:, 192:256]], axis=1)
        dkb = jnp.concatenate([db[:, 0:64], db[:, 128:192]], axis=1)
        dvb = jnp.concatenate([db[:, 64:128], db[:, 192:256]], axis=1)
        dka, dgk = norm_bwd(p_ref[:, KA:KA + 128], kn_ref[...], unrope(dka, 128), 128)
        dp_ref[:, KA:KA + 128] = dka.astype(BF16)
        dp_ref[:, VA:VA + 128] = dva.astype(BF16)
        dp_ref[:, KB:KB + 128] = unrope(dkb, 128).astype(BF16)
        dp_ref[:, VB:VB + 128] = dvb.astype(BF16)
        dqn_ref[0:1, :] += dgq
        dkn_ref[0:1, :] += dgk

    return pl.pallas_call(
        body, name=name, grid=(m // TM,), in_specs=specs,
        out_specs=[pl.BlockSpec((TM, PROJ_W), lambda t: (t, 0)), pl.BlockSpec((8, 512), lambda t: (0, 0)),
                   pl.BlockSpec((8, 128), lambda t: (0, 0))],
        out_shape=[jax.ShapeDtypeStruct((m, PROJ_W), BF16), jax.ShapeDtypeStruct((8, 512), F32),
                   jax.ShapeDtypeStruct((8, 128), F32)],
        compiler_params=_params(("arbitrary",)),
    )(proj, dq, dkv, qn, kn, cos, sin, emat)


def _attn_case(t, hg, q_ref, kv_ref, c_rows, t_rows, fn):
    wl = TM + 2 * WINDOW
    kvd = lambda a, n: kv_ref[pl.ds(a, n), :]
    dense = hg < 2
    ctx = t == 0

    @pl.when(jnp.logical_and(dense, ctx))
    def _():
        kv = kvd(0, c_rows)
        fn(kv[:, :64], kv[:, 64:], None, False, [(0, c_rows)])

    @pl.when(jnp.logical_and(dense, jnp.logical_not(ctx)))
    def _():
        kv = kvd(0, t_rows)
        fn(kv[:, :64], kv[:, 64:], None, False, [(0, t_rows)])

    @pl.when(jnp.logical_and(jnp.logical_not(dense), ctx))
    def _():
        kv = kvd(0, c_rows)
        fn(kv[:, :64], kv[:, 64:], None, True, [(0, c_rows)])

    @pl.when(jnp.logical_and(jnp.logical_not(dense), jnp.logical_not(ctx)))
    def _():
        start = pl.multiple_of(jnp.minimum(c_rows + (t - 1) * TM - WINDOW, t_rows - wl), 128)
        kv = jnp.concatenate([kvd(0, c_rows), kvd(start, wl)], axis=0)
        rows = lax.broadcasted_iota(jnp.int32, (TM, c_rows + wl), 0)
        cols = lax.broadcasted_iota(jnp.int32, (TM, c_rows + wl), 1)
        qpos = (t - 1) * TM + rows
        kpos = start - 2 * c_rows + cols
        mask = jnp.logical_or(cols < c_rows, jnp.logical_and(jnp.abs(kpos - qpos) <= WINDOW, kpos >= 0))
        fn(kv[:, :64], kv[:, 64:], mask, True, [(0, c_rows), (start, wl)])


def _softmax(q, k, mask, sink):
    s = lax.dot_general(q, k, (((1,), (1,)), ((), ())), preferred_element_type=F32)
    if mask is not None:
        s = jnp.where(mask, s, NEG_BIG)
    mx = jnp.max(s, axis=1, keepdims=True)
    if sink is not None:
        mx = jnp.maximum(mx, sink)
    p = jnp.exp(s - mx)
    l = jnp.sum(p, axis=1, keepdims=True)
    ps = None
    if sink is not None:
        ps = jnp.exp(sink - mx)
        l = l + ps
    inv = 1.0 / l
    return p * inv, (None if ps is None else ps * inv)


def _attn_specs(t_rows):
    q_spec = pl.BlockSpec((None, TM, Q_WIDTH), lambda b, hg, t, s: (b, t, hg))
    kv_spec = pl.BlockSpec((None, t_rows, 128), lambda b, hg, t, s: (b, 0, hg))
    return q_spec, kv_spec


def _attn_fwd(q_all, kv_all, sink8, c_rows, name):
    bl, t_rows, _ = q_all.shape
    q_spec, kv_spec = _attn_specs(t_rows)

    def body(sink_ref, q_ref, kv_ref, o_ref):
        hg, t = pl.program_id(1), pl.program_id(2)

        def fn(k, v, mask, use_sink, spans):
            outs = []
            for g in range(GROUP):
                sink = sink_ref[jnp.maximum(hg - 2, 0) * GROUP + g] if use_sink else None
                p, _ = _softmax(q_ref[:, g * 64:(g + 1) * 64], k, mask, sink)
                outs.append(jnp.dot(p.astype(BF16), v, preferred_element_type=F32))
            o_ref[...] = jnp.concatenate(outs, axis=1).astype(BF16)

        _attn_case(t, hg, q_ref, kv_ref, c_rows, t_rows, fn)

    return pl.pallas_call(
        body, name=name,
        grid_spec=pltpu.PrefetchScalarGridSpec(
            num_scalar_prefetch=1, grid=(bl, N_HG, t_rows // TM), in_specs=[q_spec, kv_spec], out_specs=q_spec),
        out_shape=jax.ShapeDtypeStruct(q_all.shape, BF16),
        compiler_params=_params(("parallel", "parallel", "arbitrary")),
    )(sink8, q_all, kv_all)


def _attn_bwd(q_all, kv_all, do, sink8, c_rows, name):
    bl, t_rows, _ = q_all.shape
    q_spec, kv_spec = _attn_specs(t_rows)
    ds_spec = pl.BlockSpec((None, None, 8, 128), lambda b, hg, t, s: (b, hg, 0, 0))

    def body(sink_ref, q_ref, kv_ref, do_ref, dq_ref, dkv_ref, dsk_ref):
        hg, t = pl.program_id(1), pl.program_id(2)

        @pl.when(t == 0)
        def _():
            dkv_ref[...] = jnp.zeros_like(dkv_ref)
            dsk_ref[...] = jnp.zeros_like(dsk_ref)

        def fn(k, v, mask, use_sink, spans):
            dqs, dsinks = [], []
            dk = jnp.zeros(k.shape, F32)
            dv = jnp.zeros(v.shape, F32)
            for g in range(GROUP):
                sink = sink_ref[jnp.maximum(hg - 2, 0) * GROUP + g] if use_sink else None
                q = q_ref[:, g * 64:(g + 1) * 64]
                do_g = do_ref[:, g * 64:(g + 1) * 64]
                p, ps = _softmax(q, k, mask, sink)
                dp = lax.dot_general(do_g, v, (((1,), (1,)), ((), ())), preferred_element_type=F32)
                dd = jnp.sum(p * dp, axis=1, keepdims=True)
                ds = (p * (dp - dd)).astype(BF16)
                dqs.append(jnp.dot(ds, k, preferred_element_type=F32))
                dk = dk + lax.dot_general(ds, q, (((0,), (0,)), ((), ())), preferred_element_type=F32)
                dv = dv + lax.dot_general(p.astype(BF16), do_g, (((0,), (0,)), ((), ())), preferred_element_type=F32)
                if use_sink:
                    dsinks.append(jnp.broadcast_to(-jnp.sum(ps * dd, axis=0, keepdims=True), (1, 128)))
            dq_ref[...] = jnp.concatenate(dqs, axis=1)
            dkv = jnp.concatenate([dk, dv], axis=1)
            off = 0
            for start, size in spans:
                dkv_ref[pl.ds(start, size), :] += dkv[off:off + size]
                off += size
            if use_sink:
                dsk_ref[0:GROUP, :] += jnp.concatenate(dsinks, axis=0)

        _attn_case(t, hg, q_ref, kv_ref, c_rows, t_rows, fn)

    return pl.pallas_call(
        body, name=name,
        grid_spec=pltpu.PrefetchScalarGridSpec(
            num_scalar_prefetch=1, grid=(bl, N_HG, t_rows // TM), in_specs=[q_spec, kv_spec, q_spec],
            out_specs=[q_spec, kv_spec, ds_spec]),
        out_shape=[jax.ShapeDtypeStruct(q_all.shape, F32), jax.ShapeDtypeStruct(kv_all.shape, F32),
                   jax.ShapeDtypeStruct((bl, N_HG, 8, 128), F32)],
        compiler_params=_params(("parallel", "parallel", "arbitrary")),
    )(sink8, q_all, kv_all, do)


def _silu(x):
    return x * jax.nn.sigmoid(x)


def _ada_fwd(c_rows, w_ada, b_cols, name):
    nl, d, w = w_ada.shape
    r = c_rows.shape[0]

    def body(c_ref, w_ref, b_ref, o_ref):
        s = _silu(c_ref[...]).astype(BF16)
        o_ref[...] = jnp.dot(s, w_ref[...].astype(BF16), preferred_element_type=F32) + b_ref[...]

    return pl.pallas_call(
        body, name=name, grid=(nl,),
        in_specs=[pl.BlockSpec((r, d), lambda l: (0, 0)), pl.BlockSpec((None, d, w), lambda l: (l, 0, 0)),
                  pl.BlockSpec((None, 1, w), lambda l: (l, 0, 0))],
        out_specs=pl.BlockSpec((None, r, w), lambda l: (l, 0, 0)),
        out_shape=jax.ShapeDtypeStruct((nl, r, w), F32), compiler_params=_params(("parallel",)),
    )(c_rows, w_ada, b_cols)


def _ada_bwd(c_rows, c_ctx, dmod, w_ada, name):
    nl, d, w = w_ada.shape
    r = c_rows.shape[0]

    def body(c_ref, cc_ref, g_ref, w_ref, dw_ref, dc_ref):
        l = pl.program_id(0)
        s = _silu(c_ref[...]).astype(BF16)
        gm = g_ref[...].astype(BF16)
        dw_ref[...] = lax.dot_general(s, gm, (((0,), (0,)), ((), ())), preferred_element_type=F32)
        ds = lax.dot_general(gm, w_ref[...].astype(BF16), (((1,), (1,)), ((), ())), preferred_element_type=F32)
        rows = lax.broadcasted_iota(jnp.int32, ds.shape, 0)
        dsc = jnp.sum(jnp.where(rows % ADA_ROWS == 2, ds, 0.0), axis=0, keepdims=True)
        x = cc_ref[...]
        sg = jax.nn.sigmoid(x)
        dcc = dsc * (sg * (1.0 + x * (1.0 - sg)))
        out = jnp.concatenate([dcc, jnp.zeros((7, d), F32)], axis=0)

        @pl.when(l == 0)
        def _():
            dc_ref[...] = out

        @pl.when(l != 0)
        def _():
            dc_ref[...] += out

    return pl.pallas_call(
        body, name=name, grid=(nl,),
        in_specs=[pl.BlockSpec((r, d), lambda l: (0, 0)), pl.BlockSpec((1, d), lambda l: (0, 0)),
                  pl.BlockSpec((None, r, w), lambda l: (l, 0, 0)), pl.BlockSpec((None, d, w), lambda l: (l, 0, 0))],
        out_specs=[pl.BlockSpec((None, d, w), lambda l: (l, 0, 0)), pl.BlockSpec((8, d), lambda l: (0, 0))],
        out_shape=[jax.ShapeDtypeStruct((nl, d, w), F32), jax.ShapeDtypeStruct((8, d), F32)],
        compiler_params=_params(("arbitrary",)),
    )(c_rows, c_ctx, dmod, w_ada)


def _adam_math(w, g, m, v):
    m = ADAM_B1 * m + (1.0 - ADAM_B1) * g
    v = ADAM_B2 * v + (1.0 - ADAM_B2) * (g * g)
    m_hat = m / (1.0 - ADAM_B1 ** ADAM_STEP)
    v_hat = v / (1.0 - ADAM_B2 ** ADAM_STEP)
    delta = -ADAM_LR * (m_hat / (jnp.sqrt(v_hat) + ADAM_EPS) + ADAM_WD * w)
    return delta, m, v


def _adamw(w, m, v, g_own, g_recv, name, rows=256):
    nl, r, c = w.shape
    tr = min(rows, r)
    spec = pl.BlockSpec((None, tr, c), lambda l, i: (l, i, 0))
    in_specs = [spec] * 4
    args = [w, m, v, g_own]
    if g_recv is not None:
        in_specs.append(pl.BlockSpec((N_DEV, None, tr, c), lambda l, i: (0, l, i, 0)))
        args.append(g_recv)

    def body(*refs):
        w_ref, m_ref, v_ref, g_ref = refs[:4]
        g = g_ref[...]
        if g_recv is not None:
            for k in range(1, N_DEV):
                g = g + refs[4][k].astype(F32)
        go_ref, d_ref, mo_ref, vo_ref = refs[-4:]
        delta, m_, v_ = _adam_math(w_ref[...], g, m_ref[...], v_ref[...])
        go_ref[...] = g
        d_ref[...] = delta
        mo_ref[...] = m_
        vo_ref[...] = v_

    return pl.pallas_call(
        body, name=name, grid=(nl, r // tr), in_specs=in_specs, out_specs=[spec] * 4,
        out_shape=[jax.ShapeDtypeStruct(w.shape, F32)] * 4, compiler_params=_params(("parallel", "parallel")),
    )(*args)


def _small_adamw(w, m, v, g_all, name):
    def body(w_ref, m_ref, v_ref, g_ref, go_ref, d_ref, mo_ref, vo_ref):
        g = g_ref[0]
        for k in range(1, N_DEV):
            g = g + g_ref[k]
        delta, m_, v_ = _adam_math(w_ref[...], g, m_ref[...], v_ref[...])
        go_ref[...] = g
        d_ref[...] = delta
        mo_ref[...] = m_
        vo_ref[...] = v_

    return pl.pallas_call(
        body, name=name, out_shape=[jax.ShapeDtypeStruct(w.shape, F32)] * 4, compiler_params=_params(),
    )(w, m, v, g_all)


SMALL = ("c_ctx", "b_ada", "g_pre_mix", "g_post_mix", "g_pre_mlp", "g_post_mlp", "q_norm", "k_norm", "sink")


def _pack_small(parts):
    flat = jnp.concatenate([parts[n].reshape(-1) for n in SMALL])
    rows = -(-flat.shape[0] // 1024) * 8
    return jnp.pad(flat, (0, rows * 128 - flat.shape[0])).reshape(rows, 128)


def _unpack_small(packed, like):
    flat = packed.reshape(-1)
    out, off = {}, 0
    for n in SMALL:
        size = int(np.prod(like[n].shape))
        out[n] = flat[off:off + size].reshape(like[n].shape)
        off += size
    return out


def kernel(x, c, ctx, c_ctx, w_ada, b_ada, g_pre_mix, g_post_mix, g_pre_mlp, g_post_mlp, w_in, q_norm, k_norm, sink, w_out, w_up, w_down, loss_target, m_c_ctx, m_w_ada, m_b_ada, m_g_pre_mix, m_g_post_mix, m_g_pre_mlp, m_g_post_mlp, m_w_in, m_q_norm, m_k_norm, m_sink, m_w_out, m_w_up, m_w_down, v_c_ctx, v_w_ada, v_b_ada, v_g_pre_mix, v_g_post_mix, v_g_pre_mlp, v_g_post_mlp, v_w_in, v_q_norm, v_k_norm, v_sink, v_w_out, v_w_up, v_w_down):
    bl, s_rows, d = x.shape
    c_rows = ctx.shape[1]
    assert c_rows == TM and s_rows % TM == 0 and bl == 2
    t_rows = c_rows + s_rows
    nt = t_rows // TM
    m_rows = bl * t_rows
    nl = w_in.shape[0]
    ada_w = w_ada.shape[2]
    d_ff = w_up.shape[2] * N_DEV
    me = _my_index()

    w_in_g = _all_gather(w_in.astype(BF16), "gather_w_in")
    w_out_g = _all_gather(w_out.astype(BF16), "gather_w_out")
    w_up_g = _all_gather(w_up.astype(BF16), "gather_w_up")
    w_down_g = _all_gather(w_down.astype(BF16), "gather_w_down")
    w_in_f = [w_in_g[:, l].transpose(1, 0, 2).reshape(d, PROJ_W) for l in range(nl)]
    w_in_t = [w_in_g[:, l].transpose(0, 2, 1).reshape(PROJ_W, d) for l in range(nl)]
    w_out_f = [w_out_g[:, l].reshape(-1, d) for l in range(nl)]
    w_out_t = [w.T for w in w_out_f]
    w_up_s = [w_up_g[:, l] for l in range(nl)]
    w_up_t = [w_up_g[:, l].transpose(0, 2, 1).reshape(d_ff, d) for l in range(nl)]
    w_down_f = [w_down_g[:, l].reshape(d_ff, d) for l in range(nl)]
    w_down_t = [w.T for w in w_down_f]
    big = dict(tm=1536, tn=512)
    wide = dict(tm=1024, tn=512, tk=1536)

    c_pad = jnp.concatenate([c, c_ctx[None, :], jnp.zeros((ADA_ROWS - bl - 1, d), F32)], axis=0)
    c_all = _all_gather(c_pad, "gather_c").reshape(N_DEV * ADA_ROWS, d)
    b_cols = lax.dynamic_slice(b_ada, (0, me * ada_w), (nl, ada_w))[:, None, :]
    mod_cols = _ada_fwd(c_all, w_ada, b_cols, "ada_fwd")
    mod_g = _all_gather(mod_cols, "gather_mod")
    mod_full = mod_g.transpose(1, 2, 0, 3).reshape(nl, N_DEV * ADA_ROWS, N_MOD, d)
    mine = lax.dynamic_slice(mod_full, (0, me * ADA_ROWS, 0, 0), (nl, bl + 1, N_MOD, d))
    pad = jnp.zeros((bl, 2, MOD_ROWS - N_MOD, d), F32)
    modsel = [jnp.concatenate([jnp.stack([jnp.broadcast_to(mine[l, bl], (bl, N_MOD, d)), mine[l, :bl]], axis=1), pad],
                              axis=2) for l in range(nl)]

    cos, sin = _rope_tables(t_rows, c_rows)
    emat = _head_mean_matrix()
    row = lambda a: a[None, :]
    qn = [jnp.tile(q_norm[l], 8)[None, :] for l in range(nl)]
    kn = [jnp.tile(k_norm[l], 2)[None, :] for l in range(nl)]

    h = jnp.concatenate([ctx, x], axis=1).reshape(m_rows, d)
    saved = []
    for l in range(nl):
        u = _norm_mod_fwd(h, row(g_pre_mix[l]), modsel[l], 0, 1, nt, f"mix_mod_fwd{l}")
        proj = _mm(u, w_in_f[l], name=f"mm_in{l}", tk=d, **big)
        q_all, kv_all = _prep_fwd(proj, qn[l], kn[l], cos, sin, emat, nt, f"prep_fwd{l}")
        o = _attn_fwd(q_all.reshape(bl, t_rows, 1024), kv_all.reshape(bl, t_rows, 512), sink[l], c_rows,
                      f"attn_fwd{l}").reshape(m_rows, 1024)
        mix = _mm(o, w_out_f[l], name=f"mm_out{l}", tk=1024, **big)
        h_mid = _gate_fwd(h, mix, row(g_post_mix[l]), modsel[l], 2, nt, f"mix_gate_fwd{l}")
        v_in = _norm_mod_fwd(h_mid, row(g_pre_mlp[l]), modsel[l], 3, 4, nt, f"mlp_mod_fwd{l}")
        r_act, s_act = _mm(v_in, w_up_s[l], name=f"mm_up{l}", b_mode="nn_slots", epilogue="relu2", tk=d, **big)
        y = _mm(r_act, w_down_f[l], name=f"mm_down{l}", tk=1024, **big)
        h_out = _gate_fwd(h_mid, y, row(g_post_mlp[l]), modsel[l], 5, nt, f"mlp_gate_fwd{l}")
        saved.append((h, u, proj, q_all, kv_all, o, mix, h_mid, v_in, r_act, s_act, y))
        h = h_out

    dh, ss = _loss_grad(h, loss_target.reshape(bl * s_rows, d), c_rows, nt, "loss_grad")
    loss = lax.psum(0.5 * ss[0, 0] / d, MESH_AXES)

    small_g = {n: [None] * nl for n in SMALL if n not in ("c_ctx", "b_ada")}
    dmod_rows = []
    big_g = {"w_in": [], "w_out": [], "w_up": [], "w_down": []}
    for l in reversed(range(nl)):
        h_in, u, proj, q_all, kv_all, o, mix, h_mid, v_in, r_act, s_act, y = saved[l]
        dy, dvec_g2 = _gate_bwd(y, row(g_post_mlp[l]), modsel[l], dh, 5, nt, f"mlp_gate_bwd{l}")
        da = _mm(dy, w_down_t[l], name=f"mm_da{l}", epilogue="mul", extra=s_act, out_dtype=BF16, tk=d, **big)
        dw_down = _mm(r_act.T, dy, name=f"mm_dw_down{l}", **wide)
        dw_up = _mm(v_in.T, da, name=f"mm_dw_up{l}", out_mode="slots", **wide)
        dv = _mm(da, w_up_t[l], name=f"mm_dv{l}", tk=1024, **big)
        dh, dvec_m2 = _norm_mod_bwd(h_mid, row(g_pre_mlp[l]), modsel[l], dv, dh, 3, 4, nt, f"mlp_mod_bwd{l}")
        dmix, dvec_g1 = _gate_bwd(mix, row(g_post_mix[l]), modsel[l], dh, 2, nt, f"mix_gate_bwd{l}")
        do = _mm(dmix, w_out_t[l], name=f"mm_do{l}", out_dtype=BF16, tk=d, **big)
        dw_out = _mm(o.T, dmix, name=f"mm_dw_out{l}", **wide)
        dq, dkv, dsk = _attn_bwd(q_all.reshape(bl, t_rows, 1024), kv_all.reshape(bl, t_rows, 512),
                                 do.reshape(bl, t_rows, 1024), sink[l], c_rows, f"attn_bwd{l}")
        dproj, dqn, dkn = _prep_bwd(proj, dq.reshape(m_rows, 1024), dkv.reshape(m_rows, 512), qn[l], kn[l], cos, sin,
                                    emat, nt, f"prep_bwd{l}")
        dw_in = _mm(u.T, dproj, name=f"mm_dw_in{l}", **wide)
        du = _mm(dproj, w_in_t[l], name=f"mm_du{l}", tk=PROJ_W, **big)
        dh, dvec_m1 = _norm_mod_bwd(h_in, row(g_pre_mix[l]), modsel[l], du, dh, 0, 1, nt, f"mix_mod_bwd{l}")

        small_g["g_pre_mix"][l] = jnp.sum(dvec_m1[:, :, 2], axis=(0, 1))
        small_g["g_post_mix"][l] = jnp.sum(dvec_g1[:, :, 1], axis=(0, 1))
        small_g["g_pre_mlp"][l] = jnp.sum(dvec_m2[:, :, 2], axis=(0, 1))
        small_g["g_post_mlp"][l] = jnp.sum(dvec_g2[:, :, 1], axis=(0, 1))
        small_g["q_norm"][l] = jnp.sum(dqn[0].reshape(8, HEAD_DIM), axis=0)
        small_g["k_norm"][l] = jnp.sum(dkn[0].reshape(2, HEAD_DIM), axis=0)
        small_g["sink"][l] = jnp.sum(dsk[:, 2:, :GROUP, 0], axis=0).reshape(-1)
        dms = jnp.stack([dvec_m1[:, :, 0], dvec_m1[:, :, 1], dvec_g1[:, :, 0],
                         dvec_m2[:, :, 0], dvec_m2[:, :, 1], dvec_g2[:, :, 0]], axis=2)
        rows = jnp.concatenate([dms[:, 1], jnp.sum(dms[:, 0], axis=0)[None]], axis=0)
        dmod_rows.append(jnp.pad(rows.reshape(bl + 1, N_MOD * d), ((0, ADA_ROWS - bl - 1), (0, 0))))
        big_g["w_in"].append(dw_in.reshape(d, N_DEV, PROJ_W // N_DEV).transpose(1, 0, 2))
        big_g["w_out"].append(dw_out.reshape(N_DEV, -1, d))
        big_g["w_up"].append(dw_up)
        big_g["w_down"].append(dw_down.reshape(N_DEV, -1, d))
    dmod_rows = dmod_rows[::-1]
    grad_x = dh.reshape(bl, t_rows, d)[:, c_rows:]

    dmod_g = _all_gather(jnp.stack(dmod_rows), "gather_dmod")
    dmod_mine = lax.dynamic_slice(dmod_g, (0, 0, 0, me * ada_w), (N_DEV, nl, ADA_ROWS, ada_w))
    dmod_mine = dmod_mine.transpose(1, 0, 2, 3).reshape(nl, N_DEV * ADA_ROWS, ada_w)
    dw_ada, dcc = _ada_bwd(c_all, c_ctx[None, :], dmod_mine, w_ada, "ada_bwd")

    parts = {n: jnp.stack(small_g[n]) for n in small_g}
    parts["c_ctx"] = dcc[0]
    parts["b_ada"] = jnp.stack([jnp.sum(r_[: bl + 1], axis=0) for r_ in dmod_rows])
    weights = dict(c_ctx=c_ctx, b_ada=b_ada, g_pre_mix=g_pre_mix, g_post_mix=g_post_mix, g_pre_mlp=g_pre_mlp,
                   g_post_mlp=g_post_mlp, q_norm=q_norm, k_norm=k_norm, sink=sink)
    moms = dict(c_ctx=m_c_ctx, b_ada=m_b_ada, g_pre_mix=m_g_pre_mix, g_post_mix=m_g_post_mix, g_pre_mlp=m_g_pre_mlp,
                g_post_mlp=m_g_post_mlp, q_norm=m_q_norm, k_norm=m_k_norm, sink=m_sink)
    vels = dict(c_ctx=v_c_ctx, b_ada=v_b_ada, g_pre_mix=v_g_pre_mix, g_post_mix=v_g_post_mix, g_pre_mlp=v_g_pre_mlp,
                g_post_mlp=v_g_post_mlp, q_norm=v_q_norm, k_norm=v_k_norm, sink=v_sink)
    small_all = _all_gather(_pack_small(parts), "gather_small")
    s_out = _small_adamw(_pack_small(weights), _pack_small(moms), _pack_small(vels), small_all, "adamw_small")
    s_g, s_d, s_m, s_v = [_unpack_small(a, weights) for a in s_out]

    res = {}
    for n, w_, m_, v_ in (("w_in", w_in, m_w_in, v_w_in), ("w_out", w_out, m_w_out, v_w_out),
                          ("w_up", w_up, m_w_up, v_w_up), ("w_down", w_down, m_w_down, v_w_down)):
        slots = jnp.stack(big_g[n][::-1], axis=1)
        recv = _exchange(slots.astype(BF16), f"exchange_{n}")
        own = lax.dynamic_index_in_dim(slots, me, axis=0, keepdims=False)
        res[n] = _adamw(w_, m_, v_, own, recv, f"adamw_{n}")
    res["w_ada"] = _adamw(w_ada, m_w_ada, v_w_ada, dw_ada, None, "adamw_w_ada")

    order = ("c_ctx", "w_ada", "b_ada", "g_pre_mix", "g_post_mix", "g_pre_mlp", "g_post_mlp", "w_in", "q_norm",
             "k_norm", "sink", "w_out", "w_up", "w_down")
    outs = [loss, grad_x]
    for i, small in enumerate((s_g, s_d, s_m, s_v)):
        outs += [small[n] if n in small else res[n][i] for n in order]
    return tuple(outs)
```

```python
import functools

import jax
import jax.numpy as jnp
import numpy as np
from jax import lax
from jax.experimental import pallas as pl
from jax.experimental.pallas import tpu as pltpu

F32 = jnp.float32
BF16 = jnp.bfloat16

HEAD_DIM = 64
GROUP = 4
N_HG = 4
Q_WIDTH = GROUP * HEAD_DIM
WINDOW = 128
GRID_W = 64
ROPE_THETA = 10000.0
EPS = 1e-6
NEG_BIG = -1e30
N_MOD = 6
MOD_ROWS = 8
TM = 256
N_DEV = 8
ADA_ROWS = 8
VMEM_LIMIT = 56 * 1024 * 1024

ADAM_LR = 0.001
ADAM_B1 = 0.9
ADAM_B2 = 0.999
ADAM_EPS = 1e-08
ADAM_WD = 0.01
ADAM_STEP = 10

MESH_AXES = ("x", "y", "c")


def _params(sem=None):
    kw = dict(vmem_limit_bytes=VMEM_LIMIT)
    if sem is not None:
        kw["dimension_semantics"] = sem
    return pltpu.CompilerParams(**kw)


def _my_index():
    return 4 * lax.axis_index("x") + 2 * lax.axis_index("y") + lax.axis_index("c")


def _peer(k):
    x, y, c = lax.axis_index("x"), lax.axis_index("y"), lax.axis_index("c")
    kx, ky, kc = (k >> 2) & 1, (k >> 1) & 1, k & 1
    px = (1 - x) if kx else x
    py = (1 - y) if ky else y
    pc = (1 - c) if kc else c
    return (px, py, pc), 4 * px + 2 * py + pc


GATHER, TO_OWNER, TO_OWNER_XOR = "gather", "to_owner", "to_owner_xor"


class _Comm:
    def __init__(self, items):
        self.items = list(items)
        self.arrays = [a for a, _ in self.items]

    def out_shapes(self):
        return [jax.ShapeDtypeStruct(((N_DEV,) + a.shape) if kind == GATHER else a.shape, a.dtype)
                for a, kind in self.items]

    def sem_shapes(self):
        n = len(self.items) * N_DEV
        return [pltpu.SemaphoreType.DMA((n,)), pltpu.SemaphoreType.DMA((n,))]

    def _copies(self, in_refs, out_refs, send_sems, recv_sems):
        me = _my_index()
        local, remote = [], []
        for i, ((_, kind), x_ref, o_ref) in enumerate(zip(self.items, in_refs, out_refs)):
            base = i * N_DEV
            own_src = x_ref if kind == GATHER else x_ref.at[me]
            own_dst = o_ref.at[0] if kind == TO_OWNER_XOR else o_ref.at[me]
            local.append(pltpu.make_async_copy(own_src, own_dst, send_sems.at[base]))
            for k in range(1, N_DEV):
                peer, pidx = _peer(k)
                remote.append(pltpu.make_async_remote_copy(
                    src_ref=x_ref if kind == GATHER else x_ref.at[pidx],
                    dst_ref=o_ref.at[k] if kind == TO_OWNER_XOR else o_ref.at[me],
                    send_sem=send_sems.at[base + k], recv_sem=recv_sems.at[base + k],
                    device_id=peer, device_id_type=pl.DeviceIdType.MESH))
        return local, remote

    def start(self, in_refs, out_refs, send_sems, recv_sems):
        local, remote = self._copies(in_refs, out_refs, send_sems, recv_sems)
        for cp in local + remote:
            cp.start()

    def wait(self, in_refs, out_refs, send_sems, recv_sems):
        local, remote = self._copies(in_refs, out_refs, send_sems, recv_sems)
        for cp in remote:
            cp.wait_recv()
        for cp in remote:
            cp.wait_send()
        for cp in local:
            cp.wait()


def _call(body, *, name, grid, in_specs, out_specs, out_shape, args, scratch_shapes=(), prefetch=0, sem=None,
          comm=None):
    single = not isinstance(out_shape, (list, tuple))
    out_shape = [out_shape] if single else list(out_shape)
    out_specs = [out_specs] if single else list(out_specs)
    in_specs, scratch_shapes, args = list(in_specs), list(scratch_shapes), list(args)
    n_in, n_out = len(in_specs), len(out_shape)
    if comm is not None:
        nc = len(comm.arrays)
        hbm = pl.BlockSpec(memory_space=pl.ANY)
        inner = body

        def body(*refs):
            pre, r = refs[:prefetch], refs[prefetch:]
            ins, cin = r[:n_in], r[n_in:n_in + nc]
            outs, cout = r[n_in + nc:n_in + nc + n_out], r[n_in + nc + n_out:n_in + 2 * nc + n_out]
            scr, sems = r[n_in + 2 * nc + n_out:len(r) - 2], r[len(r) - 2:]
            ids = [pl.program_id(i) for i in range(len(grid))]

            def when(flags, fn):
                if flags:
                    pl.when(functools.reduce(jnp.logical_and, flags))(fn)
                else:
                    fn()

            when([i == 0 for i in ids], lambda: comm.start(cin, cout, *sems))
            inner(*pre, *ins, *outs, *scr)
            when([i == n - 1 for i, n in zip(ids, grid)], lambda: comm.wait(cin, cout, *sems))

        in_specs += [hbm] * nc
        out_specs += [hbm] * nc
        out_shape += comm.out_shapes()
        scratch_shapes += comm.sem_shapes()
        args += comm.arrays
        sem = ("arbitrary",) * len(grid)
    kw = dict(name=name, out_shape=out_shape, compiler_params=_params(sem if grid else None))
    if prefetch:
        kw["grid_spec"] = pltpu.PrefetchScalarGridSpec(
            num_scalar_prefetch=prefetch, grid=grid, in_specs=in_specs, out_specs=out_specs,
            scratch_shapes=scratch_shapes)
    else:
        kw.update(in_specs=in_specs, out_specs=out_specs, scratch_shapes=scratch_shapes)
        if grid:
            kw["grid"] = grid
    res = list(pl.pallas_call(body, **kw)(*args))
    outs = res[:n_out]
    return (outs[0] if single else outs), res[n_out:]


def _comm_only(comm, name):
    return _call(lambda: None, name=name, grid=(), in_specs=[], out_specs=[], out_shape=[], args=[], comm=comm)[1]


def _mm(a, b, *, name, ta=False, b_mode="nn", out_mode="plain", out_dtype=F32, tm=512, tn=512, tk=512,
        epilogue=None, extra=None, comm=None):
    if ta:
        kdim, m = a.shape
    else:
        m, kdim = a.shape
    if b_mode == "nn":
        n = b.shape[1]
    elif b_mode == "nt":
        n = b.shape[0]
    elif b_mode == "nn_slots":
        n = b.shape[0] * b.shape[2]
        tn = b.shape[2]
    else:
        n = b.shape[1]
        tk = b.shape[2]
    if out_mode == "slots":
        tn = n // N_DEV
    tm, tn, tk = min(tm, m), min(tn, n), min(tk, kdim)
    assert m % tm == 0 and n % tn == 0 and kdim % tk == 0, (name, m, n, kdim, tm, tn, tk)
    nk = kdim // tk

    a_spec = pl.BlockSpec((tk, tm), lambda i, j, k: (k, i)) if ta else pl.BlockSpec((tm, tk), lambda i, j, k: (i, k))
    if b_mode == "nn":
        b_spec = pl.BlockSpec((tk, tn), lambda i, j, k: (k, j))
    elif b_mode == "nt":
        b_spec = pl.BlockSpec((tn, tk), lambda i, j, k: (j, k))
    elif b_mode == "nn_slots":
        b_spec = pl.BlockSpec((None, tk, tn), lambda i, j, k: (j, k, 0))
    else:
        b_spec = pl.BlockSpec((None, tn, tk), lambda i, j, k: (k, j, 0))
    tb = b_mode in ("nt", "nt_slots")
    if out_mode == "plain":
        o_shape, o_spec = (m, n), pl.BlockSpec((tm, tn), lambda i, j, k: (i, j))
    else:
        o_shape, o_spec = (N_DEV, m, tn), pl.BlockSpec((None, tm, tn), lambda i, j, k: (j, i, 0))
    dims = (((0 if ta else 1,), (1 if tb else 0,)), ((), ()))

    in_specs = [a_spec, b_spec]
    args = [a, b]
    if epilogue == "mul":
        in_specs.append(pl.BlockSpec((tm, tn), lambda i, j, k: (i, j)))
        args.append(extra)
    if epilogue == "relu2":
        out_shape = [jax.ShapeDtypeStruct(o_shape, BF16)] * 2
        out_specs = [o_spec, o_spec]
    else:
        out_shape = jax.ShapeDtypeStruct(o_shape, out_dtype)
        out_specs = o_spec

    def finish(refs, acc):
        if epilogue == "relu2":
            r = jnp.maximum(acc, 0.0)
            refs[2][...] = (r * r).astype(BF16)
            refs[3][...] = (2.0 * r).astype(BF16)
        elif epilogue == "mul":
            refs[3][...] = (acc * refs[2][...].astype(F32)).astype(out_dtype)
        else:
            refs[2][...] = acc.astype(out_dtype)

    def body(*refs):
        part = lax.dot_general(refs[0][...], refs[1][...], dims, preferred_element_type=F32)
        if nk == 1:
            finish(refs, part)
            return
        acc_ref = refs[-1]
        k = pl.program_id(2)

        @pl.when(k == 0)
        def _():
            acc_ref[...] = part

        @pl.when(jnp.logical_and(k > 0, k < nk - 1))
        def _():
            acc_ref[...] += part

        @pl.when(k == nk - 1)
        def _():
            finish(refs, acc_ref[...] + part)

    outs, couts = _call(
        body, name=name, grid=(m // tm, n // tn, nk), in_specs=in_specs, out_specs=out_specs, out_shape=out_shape,
        args=args, scratch_shapes=[] if nk == 1 else [pltpu.VMEM((tm, tn), F32)],
        sem=("parallel", "parallel", "arbitrary"), comm=comm)
    return outs if comm is None else (outs, couts)


def _row_specs(d, nt):
    row = pl.BlockSpec((TM, d), lambda t: (t, 0))
    vec = pl.BlockSpec((1, d), lambda t: (0, 0))
    mod = pl.BlockSpec((None, None, MOD_ROWS, d), lambda t: (t // nt, jnp.minimum(t % nt, 1), 0, 0))
    return row, vec, mod


def _rms(x):
    r = lax.rsqrt(jnp.mean(x * x, axis=1, keepdims=True) + EPS)
    return r, x * r


def _norm_mod_fwd(h, g, modsel, i_sh, i_sc, nt, name):
    m, d = h.shape
    row, vec, mod = _row_specs(d, nt)

    def body(h_ref, g_ref, ms_ref, u_ref):
        _, xh = _rms(h_ref[...])
        ms = ms_ref[...]
        u_ref[...] = (xh * g_ref[...] * (1.0 + ms[i_sc:i_sc + 1]) + ms[i_sh:i_sh + 1]).astype(BF16)

    return pl.pallas_call(
        body, name=name, grid=(m // TM,), in_specs=[row, vec, mod], out_specs=row,
        out_shape=jax.ShapeDtypeStruct((m, d), BF16), compiler_params=_params(("parallel",)),
    )(h, g, modsel)


def _acc_rows(t, nt, dvec_ref, rows):
    first = (t % nt) <= 1

    @pl.when(first)
    def _():
        dvec_ref[...] = rows

    @pl.when(jnp.logical_not(first))
    def _():
        dvec_ref[...] += rows


def _norm_mod_bwd(h, g, modsel, du, dh_in, i_sh, i_sc, nt, name, comm=None):
    m, d = h.shape
    row, vec, mod = _row_specs(d, nt)

    def body(h_ref, g_ref, ms_ref, du_ref, dhi_ref, dh_ref, dvec_ref):
        t = pl.program_id(0)
        r, xh = _rms(h_ref[...])
        g_ = g_ref[...]
        ms = ms_ref[...]
        du_ = du_ref[...]
        y = xh * g_
        dy = du_ * (1.0 + ms[i_sc:i_sc + 1])
        dxh = dy * g_
        dx = r * (dxh - xh * jnp.mean(dxh * xh, axis=1, keepdims=True))
        dh_ref[...] = dhi_ref[...] + dx
        rows = jnp.concatenate([
            jnp.sum(du_, axis=0, keepdims=True), jnp.sum(du_ * y, axis=0, keepdims=True),
            jnp.sum(dy * xh, axis=0, keepdims=True), jnp.zeros((MOD_ROWS - 3, d), F32)], axis=0)
        _acc_rows(t, nt, dvec_ref, rows)

    outs, couts = _call(
        body, name=name, grid=(m // TM,), in_specs=[row, vec, mod, row, row], out_specs=[row, mod],
        out_shape=[jax.ShapeDtypeStruct((m, d), F32), jax.ShapeDtypeStruct(modsel.shape, F32)],
        args=[h, g, modsel, du, dh_in], sem=("arbitrary",), comm=comm)
    return outs if comm is None else (outs, couts)


def _gate_fwd(h, z, g, modsel, i_g, nt, name):
    m, d = h.shape
    row, vec, mod = _row_specs(d, nt)

    def body(h_ref, z_ref, g_ref, ms_ref, o_ref):
        _, xh = _rms(z_ref[...])
        ms = ms_ref[...]
        o_ref[...] = h_ref[...] + ms[i_g:i_g + 1] * (xh * g_ref[...])

    return pl.pallas_call(
        body, name=name, grid=(m // TM,), in_specs=[row, row, vec, mod], out_specs=row,
        out_shape=jax.ShapeDtypeStruct((m, d), F32), compiler_params=_params(("parallel",)),
    )(h, z, g, modsel)


def _gate_bwd(z, g, modsel, dh, i_g, nt, name):
    m, d = z.shape
    row, vec, mod = _row_specs(d, nt)

    def body(z_ref, g_ref, ms_ref, dh_ref, dz_ref, dvec_ref):
        t = pl.program_id(0)
        r, xh = _rms(z_ref[...])
        g_ = g_ref[...]
        ms = ms_ref[...]
        dh_ = dh_ref[...]
        dy = dh_ * ms[i_g:i_g + 1]
        dxh = dy * g_
        dz_ref[...] = (r * (dxh - xh * jnp.mean(dxh * xh, axis=1, keepdims=True))).astype(BF16)
        rows = jnp.concatenate([
            jnp.sum(dh_ * (xh * g_), axis=0, keepdims=True), jnp.sum(dy * xh, axis=0, keepdims=True),
            jnp.zeros((MOD_ROWS - 2, d), F32)], axis=0)
        _acc_rows(t, nt, dvec_ref, rows)

    return pl.pallas_call(
        body, name=name, grid=(m // TM,), in_specs=[row, vec, mod, row], out_specs=[row, mod],
        out_shape=[jax.ShapeDtypeStruct((m, d), BF16), jax.ShapeDtypeStruct(modsel.shape, F32)],
        compiler_params=_params(("arbitrary",)),
    )(z, g, modsel, dh)


def _loss_grad(h, target, c_rows, nt, name):
    m, d = h.shape
    row = pl.BlockSpec((TM, d), lambda t: (t, 0))
    ntl = nt - 1
    tgt = pl.BlockSpec((TM, d), lambda t: ((t // nt) * ntl + jnp.maximum(t % nt, 1) - 1, 0))
    acc = pl.BlockSpec((8, 128), lambda t: (0, 0))

    def body(h_ref, t_ref, dh_ref, ss_ref):
        t = pl.program_id(0)

        @pl.when(t == 0)
        def _():
            ss_ref[...] = jnp.zeros_like(ss_ref)

        @pl.when(t % nt == 0)
        def _():
            dh_ref[...] = jnp.zeros_like(dh_ref)

        @pl.when(t % nt != 0)
        def _():
            e = h_ref[...] - t_ref[...]
            dh_ref[...] = e * (1.0 / d)
            ss_ref[...] += jnp.sum(e * e)

    return pl.pallas_call(
        body, name=name, grid=(m // TM,), in_specs=[row, tgt], out_specs=[row, acc],
        out_shape=[jax.ShapeDtypeStruct((m, d), F32), jax.ShapeDtypeStruct((8, 128), F32)],
        compiler_params=_params(("arbitrary",)),
    )(h, target)


QA, KA, VA, QB, KB, VB = 0, 512, 640, 768, 1280, 1408
PROJ_W = 1536
Q_SCALE = HEAD_DIM ** -0.5


def _swap16(x):
    lane = lax.broadcasted_iota(jnp.int32, x.shape, 1)
    n = x.shape[1]
    return jnp.where((lane % 32) < 16, pltpu.roll(x, n - 16, 1), pltpu.roll(x, 16, 1))


def _seg_mean(x, e):
    return jnp.dot(x, e, preferred_element_type=F32, precision=lax.Precision.HIGHEST)


def _rope_tables(t_rows, c_rows):
    s = t_rows - c_rows
    row_ids = jnp.repeat(jnp.arange(s // GRID_W, dtype=jnp.int32), GRID_W).astype(F32)
    col_ids = jnp.tile(jnp.arange(GRID_W, dtype=jnp.int32), s // GRID_W).astype(F32)
    axis_dim = HEAD_DIM // 2
    inv = ROPE_THETA ** (-jnp.arange(0, axis_dim, 2, dtype=F32) / axis_dim)
    ang_r = row_ids[:, None] * inv[None, :]
    ang_c = col_ids[:, None] * inv[None, :]
    cos = jnp.concatenate([jnp.cos(ang_r), jnp.cos(ang_r), jnp.cos(ang_c), jnp.cos(ang_c)], axis=1)
    sin = jnp.concatenate([-jnp.sin(ang_r), jnp.sin(ang_r), -jnp.sin(ang_c), jnp.sin(ang_c)], axis=1)
    cos = jnp.concatenate([jnp.ones((c_rows, HEAD_DIM), F32), cos], axis=0)
    sin = jnp.concatenate([jnp.zeros((c_rows, HEAD_DIM), F32), sin], axis=0)
    return jnp.tile(cos, (1, 8)), jnp.tile(sin, (1, 8))


def _head_mean_matrix():
    i = np.arange(512)
    return jnp.asarray((i[:, None] // HEAD_DIM == i[None, :] // HEAD_DIM).astype(np.float32) / HEAD_DIM)


def _interleave_kv(k, v):
    return jnp.concatenate([k[:, :64], v[:, :64], k[:, 64:], v[:, 64:]], axis=1)


def _prep_fwd(proj, qn, kn, cos, sin, emat, nt, name):
    m = proj.shape[0]
    specs = [
        pl.BlockSpec((TM, PROJ_W), lambda t: (t, 0)),
        pl.BlockSpec((1, 512), lambda t: (0, 0)), pl.BlockSpec((1, 128), lambda t: (0, 0)),
        pl.BlockSpec((TM, 512), lambda t: (t % nt, 0)), pl.BlockSpec((TM, 512), lambda t: (t % nt, 0)),
        pl.BlockSpec((512, 512), lambda t: (0, 0)),
    ]

    def body(p_ref, qn_ref, kn_ref, cos_ref, sin_ref, e_ref, q_ref, kv_ref):
        cos_, sin_, e = cos_ref[...], sin_ref[...], e_ref[...]

        def rope(x, w):
            return x * cos_[:, :w] + _swap16(x) * sin_[:, :w]

        def norm(x, g, w):
            return x * lax.rsqrt(_seg_mean(x * x, e[:w, :w]) + EPS) * g

        qa = rope(norm(p_ref[:, QA:QA + 512], qn_ref[...], 512), 512)
        qb = rope(p_ref[:, QB:QB + 512], 512)
        q_ref[:, 0:512] = (qa * Q_SCALE).astype(BF16)
        q_ref[:, 512:1024] = (qb * Q_SCALE).astype(BF16)
        ka = rope(norm(p_ref[:, KA:KA + 128], kn_ref[...], 128), 128)
        kb = rope(p_ref[:, KB:KB + 128], 128)
        kv_ref[:, 0:256] = _interleave_kv(ka, p_ref[:, VA:VA + 128]).astype(BF16)
        kv_ref[:, 256:512] = _interleave_kv(kb, p_ref[:, VB:VB + 128]).astype(BF16)

    return pl.pallas_call(
        body, name=name, grid=(m // TM,), in_specs=specs,
        out_specs=[pl.BlockSpec((TM, 1024), lambda t: (t, 0)), pl.BlockSpec((TM, 512), lambda t: (t, 0))],
        out_shape=[jax.ShapeDtypeStruct((m, 1024), BF16), jax.ShapeDtypeStruct((m, 512), BF16)],
        compiler_params=_params(("parallel",)),
    )(proj, qn, kn, cos, sin, emat)


def _prep_bwd(proj, dq, dkv, qn, kn, cos, sin, emat, nt, name, comm=None):
    m = proj.shape[0]
    specs = [
        pl.BlockSpec((TM, PROJ_W), lambda t: (t, 0)),
        pl.BlockSpec((TM, 1024), lambda t: (t, 0)), pl.BlockSpec((TM, 512), lambda t: (t, 0)),
        pl.BlockSpec((1, 512), lambda t: (0, 0)), pl.BlockSpec((1, 128), lambda t: (0, 0)),
        pl.BlockSpec((TM, 512), lambda t: (t % nt, 0)), pl.BlockSpec((TM, 512), lambda t: (t % nt, 0)),
        pl.BlockSpec((512, 512), lambda t: (0, 0)),
    ]

    def body(p_ref, dq_ref, dkv_ref, qn_ref, kn_ref, cos_ref, sin_ref, e_ref, dp_ref, dqn_ref, dkn_ref):
        t = pl.program_id(0)
        cos_, sin_, e = cos_ref[...], sin_ref[...], e_ref[...]

        @pl.when(t == 0)
        def _():
            dqn_ref[...] = jnp.zeros_like(dqn_ref)
            dkn_ref[...] = jnp.zeros_like(dkn_ref)

        def unrope(dy, w):
            return dy * cos_[:, :w] + _swap16(dy * sin_[:, :w])

        def norm_bwd(x, g, dy, w):
            r = lax.rsqrt(_seg_mean(x * x, e[:w, :w]) + EPS)
            xh = x * r
            dxh = dy * g
            dx = r * (dxh - xh * _seg_mean(dxh * xh, e[:w, :w]))
            return dx, jnp.sum(dy * xh, axis=0, keepdims=True)

        dqa, dgq = norm_bwd(p_ref[:, QA:QA + 512], qn_ref[...], unrope(dq_ref[:, 0:512] * Q_SCALE, 512), 512)
        dp_ref[:, QA:QA + 512] = dqa.astype(BF16)
        dp_ref[:, QB:QB + 512] = unrope(dq_ref[:, 512:1024] * Q_SCALE, 512).astype(BF16)
        da = dkv_ref[:, 0:256]
        db = dkv_ref[:, 256:512]
        dka = jnp.concatenate([da[:, 0:64], da[:, 128:192]], axis=1)
        dva = jnp.concatenate([da[:, 64:128], da[:, 192:256]], axis=1)
        dkb = jnp.concatenate([db[:, 0:64], db[:, 128:192]], axis=1)
        dvb = jnp.concatenate([db[:, 64:128], db[:, 192:256]], axis=1)
        dka, dgk = norm_bwd(p_ref[:, KA:KA + 128], kn_ref[...], unrope(dka, 128), 128)
        dp_ref[:, KA:KA + 128] = dka.astype(BF16)
        dp_ref[:, VA:VA + 128] = dva.astype(BF16)
        dp_ref[:, KB:KB + 128] = unrope(dkb, 128).astype(BF16)
        dp_ref[:, VB:VB + 128] = dvb.astype(BF16)
        dqn_ref[0:1, :] += dgq
        dkn_ref[0:1, :] += dgk

    outs, couts = _call(
        body, name=name, grid=(m // TM,), in_specs=specs,
        out_specs=[pl.BlockSpec((TM, PROJ_W), lambda t: (t, 0)), pl.BlockSpec((8, 512), lambda t: (0, 0)),
                   pl.BlockSpec((8, 128), lambda t: (0, 0))],
        out_shape=[jax.ShapeDtypeStruct((m, PROJ_W), BF16), jax.ShapeDtypeStruct((8, 512), F32),
                   jax.ShapeDtypeStruct((8, 128), F32)],
        args=[proj, dq, dkv, qn, kn, cos, sin, emat], sem=("arbitrary",), comm=comm)
    return outs if comm is None else (outs, couts)


def _attn_case(t, hg, q_ref, kv_ref, c_rows, t_rows, fn):
    wl = TM + 2 * WINDOW
    kvd = lambda a, n: kv_ref[pl.ds(a, n), :]
    dense = hg < 2
    ctx = t == 0

    @pl.when(jnp.logical_and(dense, ctx))
    def _():
        kv = kvd(0, c_rows)
        fn(kv[:, :64], kv[:, 64:], None, False, [(0, c_rows)])

    @pl.when(jnp.logical_and(dense, jnp.logical_not(ctx)))
    def _():
        kv = kvd(0, t_rows)
        fn(kv[:, :64], kv[:, 64:], None, False, [(0, t_rows)])

    @pl.when(jnp.logical_and(jnp.logical_not(dense), ctx))
    def _():
        kv = kvd(0, c_rows)
        fn(kv[:, :64], kv[:, 64:], None, True, [(0, c_rows)])

    @pl.when(jnp.logical_and(jnp.logical_not(dense), jnp.logical_not(ctx)))
    def _():
        start = pl.multiple_of(jnp.minimum(c_rows + (t - 1) * TM - WINDOW, t_rows - wl), 128)
        kv = jnp.concatenate([kvd(0, c_rows), kvd(start, wl)], axis=0)
        rows = lax.broadcasted_iota(jnp.int32, (TM, c_rows + wl), 0)
        cols = lax.broadcasted_iota(jnp.int32, (TM, c_rows + wl), 1)
        qpos = (t - 1) * TM + rows
        kpos = start - 2 * c_rows + cols
        mask = jnp.logical_or(cols < c_rows, jnp.logical_and(jnp.abs(kpos - qpos) <= WINDOW, kpos >= 0))
        fn(kv[:, :64], kv[:, 64:], mask, True, [(0, c_rows), (start, wl)])


def _softmax(q, k, mask, sink):
    s = lax.dot_general(q, k, (((1,), (1,)), ((), ())), preferred_element_type=F32)
    if mask is not None:
        s = jnp.where(mask, s, NEG_BIG)
    mx = jnp.max(s, axis=1, keepdims=True)
    if sink is not None:
        mx = jnp.maximum(mx, sink)
    p = jnp.exp(s - mx)
    l = jnp.sum(p, axis=1, keepdims=True)
    ps = None
    if sink is not None:
        ps = jnp.exp(sink - mx)
        l = l + ps
    inv = 1.0 / l
    return p * inv, (None if ps is None else ps * inv)


def _attn_specs(t_rows):
    q_spec = pl.BlockSpec((None, TM, Q_WIDTH), lambda b, hg, t, s: (b, t, hg))
    kv_spec = pl.BlockSpec((None, t_rows, 128), lambda b, hg, t, s: (b, 0, hg))
    return q_spec, kv_spec


def _attn_fwd(q_all, kv_all, sink8, c_rows, name, comm=None):
    bl, t_rows, _ = q_all.shape
    q_spec, kv_spec = _attn_specs(t_rows)

    def body(sink_ref, q_ref, kv_ref, o_ref):
        hg, t = pl.program_id(1), pl.program_id(2)

        def fn(k, v, mask, use_sink, spans):
            outs = []
            for g in range(GROUP):
                sink = sink_ref[jnp.maximum(hg - 2, 0) * GROUP + g] if use_sink else None
                p, _ = _softmax(q_ref[:, g * 64:(g + 1) * 64], k, mask, sink)
                outs.append(jnp.dot(p.astype(BF16), v, preferred_element_type=F32))
            o_ref[...] = jnp.concatenate(outs, axis=1).astype(BF16)

        _attn_case(t, hg, q_ref, kv_ref, c_rows, t_rows, fn)

    outs, couts = _call(
        body, name=name, grid=(bl, N_HG, t_rows // TM), in_specs=[q_spec, kv_spec], out_specs=q_spec,
        out_shape=jax.ShapeDtypeStruct(q_all.shape, BF16), args=[sink8, q_all, kv_all], prefetch=1,
        sem=("parallel", "parallel", "arbitrary"), comm=comm)
    return outs if comm is None else (outs, couts)


def _attn_bwd(q_all, kv_all, do, sink8, c_rows, name, comm=None):
    bl, t_rows, _ = q_all.shape
    q_spec, kv_spec = _attn_specs(t_rows)
    ds_spec = pl.BlockSpec((None, None, 8, 128), lambda b, hg, t, s: (b, hg, 0, 0))

    def body(sink_ref, q_ref, kv_ref, do_ref, dq_ref, dkv_ref, dsk_ref):
        hg, t = pl.program_id(1), pl.program_id(2)

        @pl.when(t == 0)
        def _():
            dkv_ref[...] = jnp.zeros_like(dkv_ref)
            dsk_ref[...] = jnp.zeros_like(dsk_ref)

        def fn(k, v, mask, use_sink, spans):
            dqs, dsinks = [], []
            dk = jnp.zeros(k.shape, F32)
            dv = jnp.zeros(v.shape, F32)
            for g in range(GROUP):
                sink = sink_ref[jnp.maximum(hg - 2, 0) * GROUP + g] if use_sink else None
                q = q_ref[:, g * 64:(g + 1) * 64]
                do_g = do_ref[:, g * 64:(g + 1) * 64]
                p, ps = _softmax(q, k, mask, sink)
                dp = lax.dot_general(do_g, v, (((1,), (1,)), ((), ())), preferred_element_type=F32)
                dd = jnp.sum(p * dp, axis=1, keepdims=True)
                ds = (p * (dp - dd)).astype(BF16)
                dqs.append(jnp.dot(ds, k, preferred_element_type=F32))
                dk = dk + lax.dot_general(ds, q, (((0,), (0,)), ((), ())), preferred_element_type=F32)
                dv = dv + lax.dot_general(p.astype(BF16), do_g, (((0,), (0,)), ((), ())), preferred_element_type=F32)
                if use_sink:
                    dsinks.append(jnp.broadcast_to(-jnp.sum(ps * dd, axis=0, keepdims=True), (1, 128)))
            dq_ref[...] = jnp.concatenate(dqs, axis=1)
            dkv = jnp.concatenate([dk, dv], axis=1)
            off = 0
            for start, size in spans:
                dkv_ref[pl.ds(start, size), :] += dkv[off:off + size]
                off += size
            if use_sink:
                dsk_ref[0:GROUP, :] += jnp.concatenate(dsinks, axis=0)

        _attn_case(t, hg, q_ref, kv_ref, c_rows, t_rows, fn)

    outs, couts = _call(
        body, name=name, grid=(bl, N_HG, t_rows // TM), in_specs=[q_spec, kv_spec, q_spec],
        out_specs=[q_spec, kv_spec, ds_spec],
        out_shape=[jax.ShapeDtypeStruct(q_all.shape, F32), jax.ShapeDtypeStruct(kv_all.shape, F32),
                   jax.ShapeDtypeStruct((bl, N_HG, 8, 128), F32)],
        args=[sink8, q_all, kv_all, do], prefetch=1, sem=("parallel", "parallel", "arbitrary"), comm=comm)
    return outs if comm is None else (outs, couts)


def _silu(x):
    return x * jax.nn.sigmoid(x)


def _ada_fwd(c_rows, w_ada, b_cols, name):
    nl, d, w = w_ada.shape
    r = c_rows.shape[0]

    def body(c_ref, w_ref, b_ref, o_ref):
        s = _silu(c_ref[...]).astype(BF16)
        o_ref[...] = jnp.dot(s, w_ref[...].astype(BF16), preferred_element_type=F32) + b_ref[...]

    return pl.pallas_call(
        body, name=name, grid=(nl,),
        in_specs=[pl.BlockSpec((r, d), lambda l: (0, 0)), pl.BlockSpec((None, d, w), lambda l: (l, 0, 0)),
                  pl.BlockSpec((None, 1, w), lambda l: (l, 0, 0))],
        out_specs=pl.BlockSpec((None, r, w), lambda l: (l, 0, 0)),
        out_shape=jax.ShapeDtypeStruct((nl, r, w), F32), compiler_params=_params(("parallel",)),
    )(c_rows, w_ada, b_cols)


def _ada_bwd(c_rows, c_ctx, dmod, w_ada, name):
    nl, d, w = w_ada.shape
    r = c_rows.shape[0]

    def body(c_ref, cc_ref, g_ref, w_ref, dw_ref, dc_ref):
        l = pl.program_id(0)
        s = _silu(c_ref[...]).astype(BF16)
        gm = g_ref[...].astype(BF16)
        dw_ref[...] = lax.dot_general(s, gm, (((0,), (0,)), ((), ())), preferred_element_type=F32)
        ds = lax.dot_general(gm, w_ref[...].astype(BF16), (((1,), (1,)), ((), ())), preferred_element_type=F32)
        rows = lax.broadcasted_iota(jnp.int32, ds.shape, 0)
        dsc = jnp.sum(jnp.where(rows % ADA_ROWS == 2, ds, 0.0), axis=0, keepdims=True)
        x = cc_ref[...]
        sg = jax.nn.sigmoid(x)
        dcc = dsc * (sg * (1.0 + x * (1.0 - sg)))
        out = jnp.concatenate([dcc, jnp.zeros((7, d), F32)], axis=0)

        @pl.when(l == 0)
        def _():
            dc_ref[...] = out

        @pl.when(l != 0)
        def _():
            dc_ref[...] += out

    return pl.pallas_call(
        body, name=name, grid=(nl,),
        in_specs=[pl.BlockSpec((r, d), lambda l: (0, 0)), pl.BlockSpec((1, d), lambda l: (0, 0)),
                  pl.BlockSpec((None, r, w), lambda l: (l, 0, 0)), pl.BlockSpec((None, d, w), lambda l: (l, 0, 0))],
        out_specs=[pl.BlockSpec((None, d, w), lambda l: (l, 0, 0)), pl.BlockSpec((8, d), lambda l: (0, 0))],
        out_shape=[jax.ShapeDtypeStruct((nl, d, w), F32), jax.ShapeDtypeStruct((8, d), F32)],
        compiler_params=_params(("arbitrary",)),
    )(c_rows, c_ctx, dmod, w_ada)


def _adam_math(w, g, m, v):
    m = ADAM_B1 * m + (1.0 - ADAM_B1) * g
    v = ADAM_B2 * v + (1.0 - ADAM_B2) * (g * g)
    m_hat = m / (1.0 - ADAM_B1 ** ADAM_STEP)
    v_hat = v / (1.0 - ADAM_B2 ** ADAM_STEP)
    delta = -ADAM_LR * (m_hat / (jnp.sqrt(v_hat) + ADAM_EPS) + ADAM_WD * w)
    return delta, m, v


def _adamw(w, m, v, g_own, g_recv, name, rows=256):
    nl, r, c = w.shape
    tr = min(rows, r)
    spec = pl.BlockSpec((None, tr, c), lambda l, i: (l, i, 0))
    per_layer = isinstance(g_own, (list, tuple))
    own = list(g_own) if per_layer else [g_own]
    recv = [] if g_recv is None else list(g_recv)
    in_specs = [spec] * 3 + [pl.BlockSpec((tr, c), lambda l, i: (i, 0)) if per_layer else spec] * len(own)
    in_specs += [pl.BlockSpec((N_DEV, tr, c), lambda l, i: (0, i, 0))] * len(recv)

    def body(*refs):
        w_ref, m_ref, v_ref = refs[:3]
        own_refs, recv_refs = refs[3:3 + len(own)], refs[3 + len(own):3 + len(own) + len(recv)]
        go_ref, d_ref, mo_ref, vo_ref = refs[-4:]

        def update(li):
            g = own_refs[li][...]
            if recv:
                for k in range(1, N_DEV):
                    g = g + recv_refs[li][k].astype(F32)
            delta, m_, v_ = _adam_math(w_ref[...], g, m_ref[...], v_ref[...])
            go_ref[...] = g
            d_ref[...] = delta
            mo_ref[...] = m_
            vo_ref[...] = v_

        if per_layer:
            for li in range(nl):
                pl.when(pl.program_id(0) == li)(functools.partial(update, li))
        else:
            update(0)

    return pl.pallas_call(
        body, name=name, grid=(nl, r // tr), in_specs=in_specs, out_specs=[spec] * 4,
        out_shape=[jax.ShapeDtypeStruct(w.shape, F32)] * 4, compiler_params=_params(("parallel", "parallel")),
    )(w, m, v, *own, *recv)


def _small_adamw(w, m, v, g_all, name):
    def body(w_ref, m_ref, v_ref, g_ref, go_ref, d_ref, mo_ref, vo_ref):
        g = g_ref[0]
        for k in range(1, N_DEV):
            g = g + g_ref[k]
        delta, m_, v_ = _adam_math(w_ref[...], g, m_ref[...], v_ref[...])
        go_ref[...] = g
        d_ref[...] = delta
        mo_ref[...] = m_
        vo_ref[...] = v_

    return pl.pallas_call(
        body, name=name, out_shape=[jax.ShapeDtypeStruct(w.shape, F32)] * 4, compiler_params=_params(),
    )(w, m, v, g_all)


SMALL = ("c_ctx", "b_ada", "g_pre_mix", "g_post_mix", "g_pre_mlp", "g_post_mlp", "q_norm", "k_norm", "sink")


def _pack_small(parts):
    flat = jnp.concatenate([parts[n].reshape(-1) for n in SMALL])
    rows = -(-flat.shape[0] // 1024) * 8
    return jnp.pad(flat, (0, rows * 128 - flat.shape[0])).reshape(rows, 128)


def _unpack_small(packed, like):
    flat = packed.reshape(-1)
    out, off = {}, 0
    for n in SMALL:
        size = int(np.prod(like[n].shape))
        out[n] = flat[off:off + size].reshape(like[n].shape)
        off += size
    return out


def kernel(x, c, ctx, c_ctx, w_ada, b_ada, g_pre_mix, g_post_mix, g_pre_mlp, g_post_mlp, w_in, q_norm, k_norm, sink, w_out, w_up, w_down, loss_target, m_c_ctx, m_w_ada, m_b_ada, m_g_pre_mix, m_g_post_mix, m_g_pre_mlp, m_g_post_mlp, m_w_in, m_q_norm, m_k_norm, m_sink, m_w_out, m_w_up, m_w_down, v_c_ctx, v_w_ada, v_b_ada, v_g_pre_mix, v_g_post_mix, v_g_pre_mlp, v_g_post_mlp, v_w_in, v_q_norm, v_k_norm, v_sink, v_w_out, v_w_up, v_w_down):
    bl, s_rows, d = x.shape
    c_rows = ctx.shape[1]
    assert c_rows == TM and s_rows % TM == 0 and bl == 2
    t_rows = c_rows + s_rows
    nt = t_rows // TM
    m_rows = bl * t_rows
    nl = w_in.shape[0]
    ada_w = w_ada.shape[2]
    d_ff = w_up.shape[2] * N_DEV
    me = _my_index()

    shard = lambda w_, l: w_[l].astype(BF16)
    c_pad = jnp.concatenate([c, c_ctx[None, :], jnp.zeros((ADA_ROWS - bl - 1, d), F32)], axis=0)
    c_all, w_in_g, w_out_g = _comm_only(
        _Comm([(c_pad, GATHER), (shard(w_in, 0), GATHER), (shard(w_out, 0), GATHER)]), "gather_first")
    c_all = c_all.reshape(N_DEV * ADA_ROWS, d)
    gathered = {0: dict(w_in=w_in_g, w_out=w_out_g)}

    def layer_weights(l):
        g_ = gathered[l]
        w_out_f = g_["w_out"].reshape(-1, d)
        w_down_f = g_["w_down"].reshape(d_ff, d)
        return dict(
            w_in_t=g_["w_in"].transpose(0, 2, 1).reshape(PROJ_W, d), w_out_f=w_out_f, w_out_t=w_out_f.T,
            w_up_s=g_["w_up"], w_up_t=g_["w_up"].transpose(0, 2, 1).reshape(d_ff, d),
            w_down_f=w_down_f, w_down_t=w_down_f.T)

    big = dict(tm=1536, tn=512)
    wide = dict(tm=1024, tn=512, tk=1536)

    b_cols = lax.dynamic_slice(b_ada, (0, me * ada_w), (nl, ada_w))[:, None, :]
    mod_cols = _ada_fwd(c_all, w_ada, b_cols, "ada_fwd")
    mod_slots = mod_cols.reshape(nl, N_DEV, ADA_ROWS, ada_w).transpose(1, 0, 2, 3)
    mod_g, = _comm_only(_Comm([(mod_slots, TO_OWNER)]), "exchange_mod")
    mine = mod_g.transpose(1, 2, 0, 3).reshape(nl, ADA_ROWS, N_MOD, d)
    pad = jnp.zeros((bl, 2, MOD_ROWS - N_MOD, d), F32)
    modsel = [jnp.concatenate([jnp.stack([jnp.broadcast_to(mine[l, bl], (bl, N_MOD, d)), mine[l, :bl]], axis=1), pad],
                              axis=2) for l in range(nl)]

    cos, sin = _rope_tables(t_rows, c_rows)
    emat = _head_mean_matrix()
    row = lambda a: a[None, :]
    qn = [jnp.tile(q_norm[l], 8)[None, :] for l in range(nl)]
    kn = [jnp.tile(k_norm[l], 2)[None, :] for l in range(nl)]

    h = jnp.concatenate([ctx, x], axis=1).reshape(m_rows, d)
    saved = []
    weights_of = {}
    for l in range(nl):
        u = _norm_mod_fwd(h, row(g_pre_mix[l]), modsel[l], 0, 1, nt, f"mix_mod_fwd{l}")
        w_in_f = gathered[l]["w_in"].transpose(1, 0, 2).reshape(d, PROJ_W)
        proj = _mm(u, w_in_f, name=f"mm_in{l}", tk=d, **big)
        q_all, kv_all = _prep_fwd(proj, qn[l], kn[l], cos, sin, emat, nt, f"prep_fwd{l}")
        o, (w_up_g, w_down_g) = _attn_fwd(
            q_all.reshape(bl, t_rows, 1024), kv_all.reshape(bl, t_rows, 512), sink[l], c_rows, f"attn_fwd{l}",
            comm=_Comm([(shard(w_up, l), GATHER), (shard(w_down, l), GATHER)]))
        o = o.reshape(m_rows, 1024)
        gathered[l].update(w_up=w_up_g, w_down=w_down_g)
        wl = weights_of[l] = layer_weights(l)
        mix = _mm(o, wl["w_out_f"], name=f"mm_out{l}", tk=1024, **big)
        h_mid = _gate_fwd(h, mix, row(g_post_mix[l]), modsel[l], 2, nt, f"mix_gate_fwd{l}")
        v_in = _norm_mod_fwd(h_mid, row(g_pre_mlp[l]), modsel[l], 3, 4, nt, f"mlp_mod_fwd{l}")
        more = l + 1 < nl
        res_up = _mm(v_in, wl["w_up_s"], name=f"mm_up{l}", b_mode="nn_slots", epilogue="relu2", tk=d,
                     comm=_Comm([(shard(w_in, l + 1), GATHER)]) if more else None, **big)
        (r_act, s_act), nxt_in = res_up if more else (res_up, None)
        res_down = _mm(r_act, wl["w_down_f"], name=f"mm_down{l}", tk=1024,
                       comm=_Comm([(shard(w_out, l + 1), GATHER)]) if more else None, **big)
        y, nxt_out = res_down if more else (res_down, None)
        if more:
            gathered[l + 1] = dict(w_in=nxt_in[0], w_out=nxt_out[0])
        h_out = _gate_fwd(h_mid, y, row(g_post_mlp[l]), modsel[l], 5, nt, f"mlp_gate_fwd{l}")
        saved.append((h, u, proj, q_all, kv_all, o, mix, h_mid, v_in, r_act, s_act, y))
        h = h_out

    dh, ss = _loss_grad(h, loss_target.reshape(bl * s_rows, d), c_rows, nt, "loss_grad")
    loss = lax.psum(0.5 * ss[0, 0] / d, MESH_AXES)

    small_g = {n: [None] * nl for n in SMALL if n not in ("c_ctx", "b_ada")}
    dmod_rows = []
    slots = {n: [None] * nl for n in ("w_in", "w_out", "w_up", "w_down")}
    recvd = {n: [None] * nl for n in slots}
    send = lambda n, l_: (slots[n][l_].astype(BF16), TO_OWNER_XOR)
    for l in reversed(range(nl)):
        h_in, u, proj, q_all, kv_all, o, mix, h_mid, v_in, r_act, s_act, y = saved[l]
        wl = weights_of[l]
        later = l + 1 < nl
        dy, dvec_g2 = _gate_bwd(y, row(g_post_mlp[l]), modsel[l], dh, 5, nt, f"mlp_gate_bwd{l}")
        res = _mm(dy, wl["w_down_t"], name=f"mm_da{l}", epilogue="mul", extra=s_act, out_dtype=BF16, tk=d,
                  comm=_Comm([send("w_out", l + 1)]) if later else None, **big)
        da = res[0] if later else res
        if later:
            recvd["w_out"][l + 1] = res[1][0]
        res = _mm(r_act.T, dy, name=f"mm_dw_down{l}", comm=_Comm([send("w_in", l + 1)]) if later else None, **wide)
        dw_down = res[0] if later else res
        if later:
            recvd["w_in"][l + 1] = res[1][0]
        dw_up = _mm(v_in.T, da, name=f"mm_dw_up{l}", out_mode="slots", **wide)
        slots["w_down"][l] = dw_down.reshape(N_DEV, -1, d)
        slots["w_up"][l] = dw_up
        dv = _mm(da, wl["w_up_t"], name=f"mm_dv{l}", tk=1024, **big)
        dh, dvec_m2 = _norm_mod_bwd(h_mid, row(g_pre_mlp[l]), modsel[l], dv, dh, 3, 4, nt, f"mlp_mod_bwd{l}")
        dmix, dvec_g1 = _gate_bwd(mix, row(g_post_mix[l]), modsel[l], dh, 2, nt, f"mix_gate_bwd{l}")
        do = _mm(dmix, wl["w_out_t"], name=f"mm_do{l}", out_dtype=BF16, tk=d, **big)
        dw_out = _mm(o.T, dmix, name=f"mm_dw_out{l}", **wide)
        slots["w_out"][l] = dw_out.reshape(N_DEV, -1, d)
        (dq, dkv, dsk), (recvd["w_down"][l], recvd["w_up"][l]) = _attn_bwd(
            q_all.reshape(bl, t_rows, 1024), kv_all.reshape(bl, t_rows, 512), do.reshape(bl, t_rows, 1024), sink[l],
            c_rows, f"attn_bwd{l}", comm=_Comm([send("w_down", l), send("w_up", l)]))
        last = l == 0
        res = _prep_bwd(proj, dq.reshape(m_rows, 1024), dkv.reshape(m_rows, 512), qn[l], kn[l], cos, sin, emat, nt,
                        f"prep_bwd{l}", comm=_Comm([send("w_out", l)]) if last else None)
        dproj, dqn, dkn = res[0] if last else res
        if last:
            recvd["w_out"][l] = res[1][0]
        dw_in = _mm(u.T, dproj, name=f"mm_dw_in{l}", **wide)
        slots["w_in"][l] = dw_in.reshape(d, N_DEV, PROJ_W // N_DEV).transpose(1, 0, 2)
        res = _mm(dproj, wl["w_in_t"], name=f"mm_du{l}", tk=PROJ_W,
                  comm=_Comm([send("w_in", l)]) if last else None, **big)
        du = res[0] if last else res
        if last:
            recvd["w_in"][l] = res[1][0]
        dh, dvec_m1 = _norm_mod_bwd(h_in, row(g_pre_mix[l]), modsel[l], du, dh, 0, 1, nt, f"mix_mod_bwd{l}")

        small_g["g_pre_mix"][l] = jnp.sum(dvec_m1[:, :, 2], axis=(0, 1))
        small_g["g_post_mix"][l] = jnp.sum(dvec_g1[:, :, 1], axis=(0, 1))
        small_g["g_pre_mlp"][l] = jnp.sum(dvec_m2[:, :, 2], axis=(0, 1))
        small_g["g_post_mlp"][l] = jnp.sum(dvec_g2[:, :, 1], axis=(0, 1))
        small_g["q_norm"][l] = jnp.sum(dqn[0].reshape(8, HEAD_DIM), axis=0)
        small_g["k_norm"][l] = jnp.sum(dkn[0].reshape(2, HEAD_DIM), axis=0)
        small_g["sink"][l] = jnp.sum(dsk[:, 2:, :GROUP, 0], axis=0).reshape(-1)
        dms = jnp.stack([dvec_m1[:, :, 0], dvec_m1[:, :, 1], dvec_g1[:, :, 0],
                         dvec_m2[:, :, 0], dvec_m2[:, :, 1], dvec_g2[:, :, 0]], axis=2)
        rows = jnp.concatenate([dms[:, 1], jnp.sum(dms[:, 0], axis=0)[None]], axis=0)
        dmod_rows.append(jnp.pad(rows.reshape(bl + 1, N_MOD * d), ((0, ADA_ROWS - bl - 1), (0, 0))))
    dmod_rows = dmod_rows[::-1]
    grad_x = dh.reshape(bl, t_rows, d)[:, c_rows:]

    dmod_slots = jnp.stack(dmod_rows).reshape(nl, ADA_ROWS, N_DEV, ada_w).transpose(2, 0, 1, 3)
    dmod_g, = _comm_only(_Comm([(dmod_slots, TO_OWNER)]), "exchange_dmod")
    dmod_mine = dmod_g.transpose(1, 0, 2, 3).reshape(nl, N_DEV * ADA_ROWS, ada_w)
    dw_ada, dcc = _ada_bwd(c_all, c_ctx[None, :], dmod_mine, w_ada, "ada_bwd")

    parts = {n: jnp.stack(small_g[n]) for n in small_g}
    parts["c_ctx"] = dcc[0]
    parts["b_ada"] = jnp.stack([jnp.sum(r_[: bl + 1], axis=0) for r_ in dmod_rows])
    weights = dict(c_ctx=c_ctx, b_ada=b_ada, g_pre_mix=g_pre_mix, g_post_mix=g_post_mix, g_pre_mlp=g_pre_mlp,
                   g_post_mlp=g_post_mlp, q_norm=q_norm, k_norm=k_norm, sink=sink)
    moms = dict(c_ctx=m_c_ctx, b_ada=m_b_ada, g_pre_mix=m_g_pre_mix, g_post_mix=m_g_post_mix, g_pre_mlp=m_g_pre_mlp,
                g_post_mlp=m_g_post_mlp, q_norm=m_q_norm, k_norm=m_k_norm, sink=m_sink)
    vels = dict(c_ctx=v_c_ctx, b_ada=v_b_ada, g_pre_mix=v_g_pre_mix, g_post_mix=v_g_post_mix, g_pre_mlp=v_g_pre_mlp,
                g_post_mlp=v_g_post_mlp, q_norm=v_q_norm, k_norm=v_k_norm, sink=v_sink)
    small_all, = _comm_only(_Comm([(_pack_small(parts), GATHER)]), "gather_small")
    s_out = _small_adamw(_pack_small(weights), _pack_small(moms), _pack_small(vels), small_all, "adamw_small")
    s_g, s_d, s_m, s_v = [_unpack_small(a, weights) for a in s_out]

    res = {}
    for n, w_, m_, v_ in (("w_in", w_in, m_w_in, v_w_in), ("w_out", w_out, m_w_out, v_w_out),
                          ("w_up", w_up, m_w_up, v_w_up), ("w_down", w_down, m_w_down, v_w_down)):
        own = [lax.dynamic_index_in_dim(slots[n][l], me, axis=0, keepdims=False) for l in range(nl)]
        res[n] = _adamw(w_, m_, v_, own, recvd[n], f"adamw_{n}")
    res["w_ada"] = _adamw(w_ada, m_w_ada, v_w_ada, dw_ada, None, "adamw_w_ada")

    order = ("c_ctx", "w_ada", "b_ada", "g_pre_mix", "g_post_mix", "g_pre_mlp", "g_post_mlp", "w_in", "q_norm",
             "k_norm", "sink", "w_out", "w_up", "w_down")
    outs = [loss, grad_x]
    for i, small in enumerate((s_g, s_d, s_m, s_v)):
        outs += [small[n] if n in small else res[n][i] for n in order]
    return tuple(outs)
```

```python
import functools

import jax
import jax.numpy as jnp
import numpy as np
from jax import lax
from jax.experimental import pallas as pl
from jax.experimental.pallas import tpu as pltpu

F32 = jnp.float32
BF16 = jnp.bfloat16

HEAD_DIM = 64
GROUP = 4
N_HG = 4
Q_WIDTH = GROUP * HEAD_DIM
WINDOW = 128
GRID_W = 64
ROPE_THETA = 10000.0
EPS = 1e-6
NEG_BIG = -1e30
N_MOD = 6
MOD_ROWS = 8
TM = 256
N_DEV = 8
ADA_ROWS = 8
VMEM_LIMIT = 56 * 1024 * 1024

ADAM_LR = 0.001
ADAM_B1 = 0.9
ADAM_B2 = 0.999
ADAM_EPS = 1e-08
ADAM_WD = 0.01
ADAM_STEP = 10

MESH_AXES = ("x", "y", "c")


def _params(sem=None):
    kw = dict(vmem_limit_bytes=VMEM_LIMIT)
    if sem is not None:
        kw["dimension_semantics"] = sem
    return pltpu.CompilerParams(**kw)


def _my_index():
    return 4 * lax.axis_index("x") + 2 * lax.axis_index("y") + lax.axis_index("c")


def _peer(k):
    x, y, c = lax.axis_index("x"), lax.axis_index("y"), lax.axis_index("c")
    kx, ky, kc = (k >> 2) & 1, (k >> 1) & 1, k & 1
    px = (1 - x) if kx else x
    py = (1 - y) if ky else y
    pc = (1 - c) if kc else c
    return (px, py, pc), 4 * px + 2 * py + pc


GATHER, TO_OWNER, TO_OWNER_XOR = "gather", "to_owner", "to_owner_xor"


class _Comm:
    def __init__(self, items):
        self.items = list(items)
        self.arrays = [a for a, _ in self.items]

    def out_shapes(self):
        return [jax.ShapeDtypeStruct(((N_DEV,) + a.shape) if kind == GATHER else a.shape, a.dtype)
                for a, kind in self.items]

    def sem_shapes(self):
        n = len(self.items) * N_DEV
        return [pltpu.SemaphoreType.DMA((n,)), pltpu.SemaphoreType.DMA((n,))]

    def _copies(self, in_refs, out_refs, send_sems, recv_sems):
        me = _my_index()
        local, remote = [], []
        for i, ((_, kind), x_ref, o_ref) in enumerate(zip(self.items, in_refs, out_refs)):
            base = i * N_DEV
            own_src = x_ref if kind == GATHER else x_ref.at[me]
            own_dst = o_ref.at[0] if kind == TO_OWNER_XOR else o_ref.at[me]
            local.append(pltpu.make_async_copy(own_src, own_dst, send_sems.at[base]))
            for k in range(1, N_DEV):
                peer, pidx = _peer(k)
                remote.append(pltpu.make_async_remote_copy(
                    src_ref=x_ref if kind == GATHER else x_ref.at[pidx],
                    dst_ref=o_ref.at[k] if kind == TO_OWNER_XOR else o_ref.at[me],
                    send_sem=send_sems.at[base + k], recv_sem=recv_sems.at[base + k],
                    device_id=peer, device_id_type=pl.DeviceIdType.MESH))
        return local, remote

    def start(self, in_refs, out_refs, send_sems, recv_sems):
        local, remote = self._copies(in_refs, out_refs, send_sems, recv_sems)
        for cp in local + remote:
            cp.start()

    def wait(self, in_refs, out_refs, send_sems, recv_sems):
        local, remote = self._copies(in_refs, out_refs, send_sems, recv_sems)
        for cp in remote:
            cp.wait_recv()
        for cp in remote:
            cp.wait_send()
        for cp in local:
            cp.wait()


def _call(body, *, name, grid, in_specs, out_specs, out_shape, args, scratch_shapes=(), prefetch=0, sem=None,
          comm=None):
    single = not isinstance(out_shape, (list, tuple))
    out_shape = [out_shape] if single else list(out_shape)
    out_specs = [out_specs] if single else list(out_specs)
    in_specs, scratch_shapes, args = list(in_specs), list(scratch_shapes), list(args)
    n_in, n_out = len(in_specs), len(out_shape)
    if comm is not None:
        nc = len(comm.arrays)
        hbm = pl.BlockSpec(memory_space=pl.ANY)
        inner = body

        def body(*refs):
            pre, r = refs[:prefetch], refs[prefetch:]
            ins, cin = r[:n_in], r[n_in:n_in + nc]
            outs, cout = r[n_in + nc:n_in + nc + n_out], r[n_in + nc + n_out:n_in + 2 * nc + n_out]
            scr, sems = r[n_in + 2 * nc + n_out:len(r) - 2], r[len(r) - 2:]
            ids = [pl.program_id(i) for i in range(len(grid))]

            def when(flags, fn):
                if flags:
                    pl.when(functools.reduce(jnp.logical_and, flags))(fn)
                else:
                    fn()

            when([i == 0 for i in ids], lambda: comm.start(cin, cout, *sems))
            inner(*pre, *ins, *outs, *scr)
            when([i == n - 1 for i, n in zip(ids, grid)], lambda: comm.wait(cin, cout, *sems))

        in_specs += [hbm] * nc
        out_specs += [hbm] * nc
        out_shape += comm.out_shapes()
        scratch_shapes += comm.sem_shapes()
        args += comm.arrays
        sem = ("arbitrary",) * len(grid)
    kw = dict(name=name, out_shape=out_shape, compiler_params=_params(sem if grid else None))
    if prefetch:
        kw["grid_spec"] = pltpu.PrefetchScalarGridSpec(
            num_scalar_prefetch=prefetch, grid=grid, in_specs=in_specs, out_specs=out_specs,
            scratch_shapes=scratch_shapes)
    else:
        kw.update(in_specs=in_specs, out_specs=out_specs, scratch_shapes=scratch_shapes)
        if grid:
            kw["grid"] = grid
    res = list(pl.pallas_call(body, **kw)(*args))
    outs = res[:n_out]
    return (outs[0] if single else outs), res[n_out:]


def _comm_only(comm, name):
    return _call(lambda: None, name=name, grid=(), in_specs=[], out_specs=[], out_shape=[], args=[], comm=comm)[1]


def _mm(a, b, *, name, ta=False, b_mode="nn", out_mode="plain", out_dtype=F32, tm=512, tn=512, tk=512,
        epilogue=None, extra=None, comm=None):
    if ta:
        kdim, m = a.shape
    else:
        m, kdim = a.shape
    if b_mode == "nn":
        n = b.shape[1]
    elif b_mode == "nt":
        n = b.shape[0]
    elif b_mode == "nn_slots":
        n = b.shape[0] * b.shape[2]
        tn = b.shape[2]
    else:
        n = b.shape[1]
        tk = b.shape[2]
    if out_mode == "slots":
        tn = n // N_DEV
    tm, tn, tk = min(tm, m), min(tn, n), min(tk, kdim)
    assert m % tm == 0 and n % tn == 0 and kdim % tk == 0, (name, m, n, kdim, tm, tn, tk)
    nk = kdim // tk

    a_spec = pl.BlockSpec((tk, tm), lambda i, j, k: (k, i)) if ta else pl.BlockSpec((tm, tk), lambda i, j, k: (i, k))
    if b_mode == "nn":
        b_spec = pl.BlockSpec((tk, tn), lambda i, j, k: (k, j))
    elif b_mode == "nt":
        b_spec = pl.BlockSpec((tn, tk), lambda i, j, k: (j, k))
    elif b_mode == "nn_slots":
        b_spec = pl.BlockSpec((None, tk, tn), lambda i, j, k: (j, k, 0))
    else:
        b_spec = pl.BlockSpec((None, tn, tk), lambda i, j, k: (k, j, 0))
    tb = b_mode in ("nt", "nt_slots")
    if out_mode == "plain":
        o_shape, o_spec = (m, n), pl.BlockSpec((tm, tn), lambda i, j, k: (i, j))
    else:
        o_shape, o_spec = (N_DEV, m, tn), pl.BlockSpec((None, tm, tn), lambda i, j, k: (j, i, 0))
    dims = (((0 if ta else 1,), (1 if tb else 0,)), ((), ()))

    in_specs = [a_spec, b_spec]
    args = [a, b]
    if epilogue == "mul":
        in_specs.append(pl.BlockSpec((tm, tn), lambda i, j, k: (i, j)))
        args.append(extra)
    if epilogue == "relu2":
        out_shape = [jax.ShapeDtypeStruct(o_shape, BF16)] * 2 + [jax.ShapeDtypeStruct((n, m), BF16)]
        out_specs = [o_spec, o_spec, pl.BlockSpec((tn, tm), lambda i, j, k: (j, i))]
    else:
        out_shape = jax.ShapeDtypeStruct(o_shape, out_dtype)
        out_specs = o_spec

    def finish(refs, acc):
        if epilogue == "relu2":
            r = jnp.maximum(acc, 0.0)
            r2 = (r * r).astype(BF16)
            refs[2][...] = r2
            refs[3][...] = (2.0 * r).astype(BF16)
            refs[4][...] = r2.T
        elif epilogue == "mul":
            refs[3][...] = (acc * refs[2][...].astype(F32)).astype(out_dtype)
        else:
            refs[2][...] = acc.astype(out_dtype)

    def body(*refs):
        part = lax.dot_general(refs[0][...], refs[1][...], dims, preferred_element_type=F32)
        if nk == 1:
            finish(refs, part)
            return
        acc_ref = refs[-1]
        k = pl.program_id(2)

        @pl.when(k == 0)
        def _():
            acc_ref[...] = part

        @pl.when(jnp.logical_and(k > 0, k < nk - 1))
        def _():
            acc_ref[...] += part

        @pl.when(k == nk - 1)
        def _():
            finish(refs, acc_ref[...] + part)

    outs, couts = _call(
        body, name=name, grid=(m // tm, n // tn, nk), in_specs=in_specs, out_specs=out_specs, out_shape=out_shape,
        args=args, scratch_shapes=[] if nk == 1 else [pltpu.VMEM((tm, tn), F32)],
        sem=("parallel", "parallel", "arbitrary"), comm=comm)
    return outs if comm is None else (outs, couts)


def _row_specs(d, nt):
    row = pl.BlockSpec((TM, d), lambda t: (t, 0))
    vec = pl.BlockSpec((1, d), lambda t: (0, 0))
    mod = pl.BlockSpec((None, None, MOD_ROWS, d), lambda t: (t // nt, jnp.minimum(t % nt, 1), 0, 0))
    return row, vec, mod


def _rms(x):
    r = lax.rsqrt(jnp.mean(x * x, axis=1, keepdims=True) + EPS)
    return r, x * r


def _norm_mod_fwd(h, g, modsel, i_sh, i_sc, nt, name):
    m, d = h.shape
    row, vec, mod = _row_specs(d, nt)

    def body(h_ref, g_ref, ms_ref, u_ref, ut_ref):
        _, xh = _rms(h_ref[...])
        ms = ms_ref[...]
        u = (xh * g_ref[...] * (1.0 + ms[i_sc:i_sc + 1]) + ms[i_sh:i_sh + 1]).astype(BF16)
        u_ref[...] = u
        ut_ref[...] = u.T

    return pl.pallas_call(
        body, name=name, grid=(m // TM,), in_specs=[row, vec, mod],
        out_specs=[row, pl.BlockSpec((d, TM), lambda t: (0, t))],
        out_shape=[jax.ShapeDtypeStruct((m, d), BF16), jax.ShapeDtypeStruct((d, m), BF16)],
        compiler_params=_params(("parallel",)),
    )(h, g, modsel)


def _acc_rows(t, nt, dvec_ref, rows):
    first = (t % nt) <= 1

    @pl.when(first)
    def _():
        dvec_ref[...] = rows

    @pl.when(jnp.logical_not(first))
    def _():
        dvec_ref[...] += rows


def _norm_mod_bwd(h, g, modsel, du, dh_in, i_sh, i_sc, nt, name, comm=None):
    m, d = h.shape
    row, vec, mod = _row_specs(d, nt)

    def body(h_ref, g_ref, ms_ref, du_ref, dhi_ref, dh_ref, dvec_ref):
        t = pl.program_id(0)
        r, xh = _rms(h_ref[...])
        g_ = g_ref[...]
        ms = ms_ref[...]
        du_ = du_ref[...]
        y = xh * g_
        dy = du_ * (1.0 + ms[i_sc:i_sc + 1])
        dxh = dy * g_
        dx = r * (dxh - xh * jnp.mean(dxh * xh, axis=1, keepdims=True))
        dh_ref[...] = dhi_ref[...] + dx
        rows = jnp.concatenate([
            jnp.sum(du_, axis=0, keepdims=True), jnp.sum(du_ * y, axis=0, keepdims=True),
            jnp.sum(dy * xh, axis=0, keepdims=True), jnp.zeros((MOD_ROWS - 3, d), F32)], axis=0)
        _acc_rows(t, nt, dvec_ref, rows)

    outs, couts = _call(
        body, name=name, grid=(m // TM,), in_specs=[row, vec, mod, row, row], out_specs=[row, mod],
        out_shape=[jax.ShapeDtypeStruct((m, d), F32), jax.ShapeDtypeStruct(modsel.shape, F32)],
        args=[h, g, modsel, du, dh_in], sem=("arbitrary",), comm=comm)
    return outs if comm is None else (outs, couts)


def _gate_fwd(h, z, g, modsel, i_g, nt, name):
    m, d = h.shape
    row, vec, mod = _row_specs(d, nt)

    def body(h_ref, z_ref, g_ref, ms_ref, o_ref):
        _, xh = _rms(z_ref[...])
        ms = ms_ref[...]
        o_ref[...] = h_ref[...] + ms[i_g:i_g + 1] * (xh * g_ref[...])

    return pl.pallas_call(
        body, name=name, grid=(m // TM,), in_specs=[row, row, vec, mod], out_specs=row,
        out_shape=jax.ShapeDtypeStruct((m, d), F32), compiler_params=_params(("parallel",)),
    )(h, z, g, modsel)


def _gate_bwd(z, g, modsel, dh, i_g, nt, name):
    m, d = z.shape
    row, vec, mod = _row_specs(d, nt)

    def body(z_ref, g_ref, ms_ref, dh_ref, dz_ref, dvec_ref):
        t = pl.program_id(0)
        r, xh = _rms(z_ref[...])
        g_ = g_ref[...]
        ms = ms_ref[...]
        dh_ = dh_ref[...]
        dy = dh_ * ms[i_g:i_g + 1]
        dxh = dy * g_
        dz_ref[...] = (r * (dxh - xh * jnp.mean(dxh * xh, axis=1, keepdims=True))).astype(BF16)
        rows = jnp.concatenate([
            jnp.sum(dh_ * (xh * g_), axis=0, keepdims=True), jnp.sum(dy * xh, axis=0, keepdims=True),
            jnp.zeros((MOD_ROWS - 2, d), F32)], axis=0)
        _acc_rows(t, nt, dvec_ref, rows)

    return pl.pallas_call(
        body, name=name, grid=(m // TM,), in_specs=[row, vec, mod, row], out_specs=[row, mod],
        out_shape=[jax.ShapeDtypeStruct((m, d), BF16), jax.ShapeDtypeStruct(modsel.shape, F32)],
        compiler_params=_params(("arbitrary",)),
    )(z, g, modsel, dh)


def _loss_grad(h, target, c_rows, nt, name):
    m, d = h.shape
    row = pl.BlockSpec((TM, d), lambda t: (t, 0))
    ntl = nt - 1
    tgt = pl.BlockSpec((TM, d), lambda t: ((t // nt) * ntl + jnp.maximum(t % nt, 1) - 1, 0))
    acc = pl.BlockSpec((8, 128), lambda t: (0, 0))

    def body(h_ref, t_ref, dh_ref, ss_ref):
        t = pl.program_id(0)

        @pl.when(t == 0)
        def _():
            ss_ref[...] = jnp.zeros_like(ss_ref)

        @pl.when(t % nt == 0)
        def _():
            dh_ref[...] = jnp.zeros_like(dh_ref)

        @pl.when(t % nt != 0)
        def _():
            e = h_ref[...] - t_ref[...]
            dh_ref[...] = e * (1.0 / d)
            ss_ref[...] += jnp.sum(e * e)

    return pl.pallas_call(
        body, name=name, grid=(m // TM,), in_specs=[row, tgt], out_specs=[row, acc],
        out_shape=[jax.ShapeDtypeStruct((m, d), F32), jax.ShapeDtypeStruct((8, 128), F32)],
        compiler_params=_params(("arbitrary",)),
    )(h, target)


QA, KA, VA, QB, KB, VB = 0, 512, 640, 768, 1280, 1408
PROJ_W = 1536
Q_SCALE = HEAD_DIM ** -0.5


def _swap16(x):
    lane = lax.broadcasted_iota(jnp.int32, x.shape, 1)
    n = x.shape[1]
    return jnp.where((lane % 32) < 16, pltpu.roll(x, n - 16, 1), pltpu.roll(x, 16, 1))


def _seg_mean(x, e):
    return jnp.dot(x, e, preferred_element_type=F32, precision=lax.Precision.HIGHEST)


def _rope_tables(t_rows, c_rows):
    s = t_rows - c_rows
    row_ids = jnp.repeat(jnp.arange(s // GRID_W, dtype=jnp.int32), GRID_W).astype(F32)
    col_ids = jnp.tile(jnp.arange(GRID_W, dtype=jnp.int32), s // GRID_W).astype(F32)
    axis_dim = HEAD_DIM // 2
    inv = ROPE_THETA ** (-jnp.arange(0, axis_dim, 2, dtype=F32) / axis_dim)
    ang_r = row_ids[:, None] * inv[None, :]
    ang_c = col_ids[:, None] * inv[None, :]
    cos = jnp.concatenate([jnp.cos(ang_r), jnp.cos(ang_r), jnp.cos(ang_c), jnp.cos(ang_c)], axis=1)
    sin = jnp.concatenate([-jnp.sin(ang_r), jnp.sin(ang_r), -jnp.sin(ang_c), jnp.sin(ang_c)], axis=1)
    cos = jnp.concatenate([jnp.ones((c_rows, HEAD_DIM), F32), cos], axis=0)
    sin = jnp.concatenate([jnp.zeros((c_rows, HEAD_DIM), F32), sin], axis=0)
    return jnp.tile(cos, (1, 8)), jnp.tile(sin, (1, 8))


def _head_mean_matrix():
    i = np.arange(512)
    return jnp.asarray((i[:, None] // HEAD_DIM == i[None, :] // HEAD_DIM).astype(np.float32) / HEAD_DIM)


def _interleave_kv(k, v):
    return jnp.concatenate([k[:, :64], v[:, :64], k[:, 64:], v[:, 64:]], axis=1)


def _prep_fwd(proj, qn, kn, cos, sin, emat, nt, name):
    m = proj.shape[0]
    specs = [
        pl.BlockSpec((TM, PROJ_W), lambda t: (t, 0)),
        pl.BlockSpec((1, 512), lambda t: (0, 0)), pl.BlockSpec((1, 128), lambda t: (0, 0)),
        pl.BlockSpec((TM, 512), lambda t: (t % nt, 0)), pl.BlockSpec((TM, 512), lambda t: (t % nt, 0)),
        pl.BlockSpec((512, 512), lambda t: (0, 0)),
    ]

    def body(p_ref, qn_ref, kn_ref, cos_ref, sin_ref, e_ref, q_ref, kv_ref):
        cos_, sin_, e = cos_ref[...], sin_ref[...], e_ref[...]

        def rope(x, w):
            return x * cos_[:, :w] + _swap16(x) * sin_[:, :w]

        def norm(x, g, w):
            return x * lax.rsqrt(_seg_mean(x * x, e[:w, :w]) + EPS) * g

        qa = rope(norm(p_ref[:, QA:QA + 512], qn_ref[...], 512), 512)
        qb = rope(p_ref[:, QB:QB + 512], 512)
        q_ref[:, 0:512] = (qa * Q_SCALE).astype(BF16)
        q_ref[:, 512:1024] = (qb * Q_SCALE).astype(BF16)
        ka = rope(norm(p_ref[:, KA:KA + 128], kn_ref[...], 128), 128)
        kb = rope(p_ref[:, KB:KB + 128], 128)
        kv_ref[:, 0:256] = _interleave_kv(ka, p_ref[:, VA:VA + 128]).astype(BF16)
        kv_ref[:, 256:512] = _interleave_kv(kb, p_ref[:, VB:VB + 128]).astype(BF16)

    return pl.pallas_call(
        body, name=name, grid=(m // TM,), in_specs=specs,
        out_specs=[pl.BlockSpec((TM, 1024), lambda t: (t, 0)), pl.BlockSpec((TM, 512), lambda t: (t, 0))],
        out_shape=[jax.ShapeDtypeStruct((m, 1024), BF16), jax.ShapeDtypeStruct((m, 512), BF16)],
        compiler_params=_params(("parallel",)),
    )(proj, qn, kn, cos, sin, emat)


def _prep_bwd(proj, dq, dkv, qn, kn, cos, sin, emat, nt, name, comm=None):
    m = proj.shape[0]
    specs = [
        pl.BlockSpec((TM, PROJ_W), lambda t: (t, 0)),
        pl.BlockSpec((TM, 1024), lambda t: (t, 0)), pl.BlockSpec((TM, 512), lambda t: (t, 0)),
        pl.BlockSpec((1, 512), lambda t: (0, 0)), pl.BlockSpec((1, 128), lambda t: (0, 0)),
        pl.BlockSpec((TM, 512), lambda t: (t % nt, 0)), pl.BlockSpec((TM, 512), lambda t: (t % nt, 0)),
        pl.BlockSpec((512, 512), lambda t: (0, 0)),
    ]

    def body(p_ref, dq_ref, dkv_ref, qn_ref, kn_ref, cos_ref, sin_ref, e_ref, dp_ref, dqn_ref, dkn_ref):
        t = pl.program_id(0)
        cos_, sin_, e = cos_ref[...], sin_ref[...], e_ref[...]

        @pl.when(t == 0)
        def _():
            dqn_ref[...] = jnp.zeros_like(dqn_ref)
            dkn_ref[...] = jnp.zeros_like(dkn_ref)

        def unrope(dy, w):
            return dy * cos_[:, :w] + _swap16(dy * sin_[:, :w])

        def norm_bwd(x, g, dy, w):
            r = lax.rsqrt(_seg_mean(x * x, e[:w, :w]) + EPS)
            xh = x * r
            dxh = dy * g
            dx = r * (dxh - xh * _seg_mean(dxh * xh, e[:w, :w]))
            return dx, jnp.sum(dy * xh, axis=0, keepdims=True)

        dqa, dgq = norm_bwd(p_ref[:, QA:QA + 512], qn_ref[...], unrope(dq_ref[:, 0:512] * Q_SCALE, 512), 512)
        dp_ref[:, QA:QA + 512] = dqa.astype(BF16)
        dp_ref[:, QB:QB + 512] = unrope(dq_ref[:, 512:1024] * Q_SCALE, 512).astype(BF16)
        da = dkv_ref[:, 0:256]
        db = dkv_ref[:, 256:512]
        dka = jnp.concatenate([da[:, 0:64], da[:, 128:192]], axis=1)
        dva = jnp.concatenate([da[:, 64:128], da[:, 192:256]], axis=1)
        dkb = jnp.concatenate([db[:, 0:64], db[:, 128:192]], axis=1)
        dvb = jnp.concatenate([db[:, 64:128], db[:, 192:256]], axis=1)
        dka, dgk = norm_bwd(p_ref[:, KA:KA + 128], kn_ref[...], unrope(dka, 128), 128)
        dp_ref[:, KA:KA + 128] = dka.astype(BF16)
        dp_ref[:, VA:VA + 128] = dva.astype(BF16)
        dp_ref[:, KB:KB + 128] = unrope(dkb, 128).astype(BF16)
        dp_ref[:, VB:VB + 128] = dvb.astype(BF16)
        dqn_ref[0:1, :] += dgq
        dkn_ref[0:1, :] += dgk

    outs, couts = _call(
        body, name=name, grid=(m // TM,), in_specs=specs,
        out_specs=[pl.BlockSpec((TM, PROJ_W), lambda t: (t, 0)), pl.BlockSpec((8, 512), lambda t: (0, 0)),
                   pl.BlockSpec((8, 128), lambda t: (0, 0))],
        out_shape=[jax.ShapeDtypeStruct((m, PROJ_W), BF16), jax.ShapeDtypeStruct((8, 512), F32),
                   jax.ShapeDtypeStruct((8, 128), F32)],
        args=[proj, dq, dkv, qn, kn, cos, sin, emat], sem=("arbitrary",), comm=comm)
    return outs if comm is None else (outs, couts)


def _attn_case(t, hg, q_ref, kv_ref, c_rows, t_rows, fn):
    wl = TM + 2 * WINDOW
    kvd = lambda a, n: kv_ref[pl.ds(a, n), :]
    dense = hg < 2
    ctx = t == 0

    @pl.when(jnp.logical_and(dense, ctx))
    def _():
        kv = kvd(0, c_rows)
        fn(kv[:, :64], kv[:, 64:], None, False, [(0, c_rows)])

    @pl.when(jnp.logical_and(dense, jnp.logical_not(ctx)))
    def _():
        kv = kvd(0, t_rows)
        fn(kv[:, :64], kv[:, 64:], None, False, [(0, t_rows)])

    @pl.when(jnp.logical_and(jnp.logical_not(dense), ctx))
    def _():
        kv = kvd(0, c_rows)
        fn(kv[:, :64], kv[:, 64:], None, True, [(0, c_rows)])

    @pl.when(jnp.logical_and(jnp.logical_not(dense), jnp.logical_not(ctx)))
    def _():
        start = pl.multiple_of(jnp.minimum(c_rows + (t - 1) * TM - WINDOW, t_rows - wl), 128)
        kv = jnp.concatenate([kvd(0, c_rows), kvd(start, wl)], axis=0)
        rows = lax.broadcasted_iota(jnp.int32, (TM, c_rows + wl), 0)
        cols = lax.broadcasted_iota(jnp.int32, (TM, c_rows + wl), 1)
        qpos = (t - 1) * TM + rows
        kpos = start - 2 * c_rows + cols
        mask = jnp.logical_or(cols < c_rows, jnp.logical_and(jnp.abs(kpos - qpos) <= WINDOW, kpos >= 0))
        fn(kv[:, :64], kv[:, 64:], mask, True, [(0, c_rows), (start, wl)])


def _softmax(q, k, mask, sink):
    s = lax.dot_general(q, k, (((1,), (1,)), ((), ())), preferred_element_type=F32)
    if mask is not None:
        s = jnp.where(mask, s, NEG_BIG)
    mx = jnp.max(s, axis=1, keepdims=True)
    if sink is not None:
        mx = jnp.maximum(mx, sink)
    p = jnp.exp(s - mx)
    l = jnp.sum(p, axis=1, keepdims=True)
    ps = None
    if sink is not None:
        ps = jnp.exp(sink - mx)
        l = l + ps
    inv = 1.0 / l
    return p, inv, (None if ps is None else ps * inv)


def _attn_specs(t_rows):
    q_spec = pl.BlockSpec((None, TM, Q_WIDTH), lambda b, hg, t, s: (b, t, hg))
    kv_spec = pl.BlockSpec((None, t_rows, 128), lambda b, hg, t, s: (b, 0, hg))
    return q_spec, kv_spec


def _attn_fwd(q_all, kv_all, sink8, c_rows, name, comm=None):
    bl, t_rows, _ = q_all.shape
    q_spec, kv_spec = _attn_specs(t_rows)

    def body(sink_ref, q_ref, kv_ref, o_ref, ot_ref):
        hg, t = pl.program_id(1), pl.program_id(2)

        def fn(k, v, mask, use_sink, spans):
            outs = []
            for g in range(GROUP):
                sink = sink_ref[jnp.maximum(hg - 2, 0) * GROUP + g] if use_sink else None
                p, inv, _ = _softmax(q_ref[:, g * 64:(g + 1) * 64], k, mask, sink)
                outs.append(jnp.dot(p.astype(BF16), v, preferred_element_type=F32) * inv)
            o = jnp.concatenate(outs, axis=1).astype(BF16)
            o_ref[...] = o
            ot_ref[...] = o.T

        _attn_case(t, hg, q_ref, kv_ref, c_rows, t_rows, fn)

    nt = t_rows // TM
    ot_spec = pl.BlockSpec((Q_WIDTH, TM), lambda b, hg, t, s: (hg, b * nt + t))
    outs, couts = _call(
        body, name=name, grid=(bl, N_HG, nt), in_specs=[q_spec, kv_spec], out_specs=[q_spec, ot_spec],
        out_shape=[jax.ShapeDtypeStruct(q_all.shape, BF16), jax.ShapeDtypeStruct((N_HG * Q_WIDTH, bl * t_rows), BF16)],
        args=[sink8, q_all, kv_all], prefetch=1, sem=("parallel", "parallel", "arbitrary"), comm=comm)
    return outs if comm is None else (outs, couts)


def _attn_bwd(q_all, kv_all, do, sink8, c_rows, name, comm=None):
    bl, t_rows, _ = q_all.shape
    q_spec, kv_spec = _attn_specs(t_rows)
    ds_spec = pl.BlockSpec((None, None, 8, 128), lambda b, hg, t, s: (b, hg, 0, 0))

    def body(sink_ref, q_ref, kv_ref, do_ref, dq_ref, dkv_ref, dsk_ref):
        hg, t = pl.program_id(1), pl.program_id(2)

        @pl.when(t == 0)
        def _():
            dkv_ref[...] = jnp.zeros_like(dkv_ref)
            dsk_ref[...] = jnp.zeros_like(dsk_ref)

        def fn(k, v, mask, use_sink, spans):
            dqs, dsinks = [], []
            dk = jnp.zeros(k.shape, F32)
            dv = jnp.zeros(v.shape, F32)
            for g in range(GROUP):
                sink = sink_ref[jnp.maximum(hg - 2, 0) * GROUP + g] if use_sink else None
                q = q_ref[:, g * 64:(g + 1) * 64]
                do_g = do_ref[:, g * 64:(g + 1) * 64]
                p, inv, ps = _softmax(q, k, mask, sink)
                dp = lax.dot_general(do_g, v, (((1,), (1,)), ((), ())), preferred_element_type=F32)
                dd = jnp.sum(p * dp, axis=1, keepdims=True) * inv
                ds = (p * (dp - dd)).astype(BF16)
                dqs.append(jnp.dot(ds, k, preferred_element_type=F32) * inv)
                q_s = (q.astype(F32) * inv).astype(BF16)
                do_s = (do_g.astype(F32) * inv).astype(BF16)
                dk = dk + lax.dot_general(ds, q_s, (((0,), (0,)), ((), ())), preferred_element_type=F32)
                dv = dv + lax.dot_general(p.astype(BF16), do_s, (((0,), (0,)), ((), ())), preferred_element_type=F32)
                if use_sink:
                    dsinks.append(jnp.broadcast_to(-jnp.sum(ps * dd, axis=0, keepdims=True), (1, 128)))
            dq_ref[...] = jnp.concatenate(dqs, axis=1)
            dkv = jnp.concatenate([dk, dv], axis=1)
            off = 0
            for start, size in spans:
                dkv_ref[pl.ds(start, size), :] += dkv[off:off + size]
                off += size
            if use_sink:
                dsk_ref[0:GROUP, :] += jnp.concatenate(dsinks, axis=0)

        _attn_case(t, hg, q_ref, kv_ref, c_rows, t_rows, fn)

    outs, couts = _call(
        body, name=name, grid=(bl, N_HG, t_rows // TM), in_specs=[q_spec, kv_spec, q_spec],
        out_specs=[q_spec, kv_spec, ds_spec],
        out_shape=[jax.ShapeDtypeStruct(q_all.shape, F32), jax.ShapeDtypeStruct(kv_all.shape, F32),
                   jax.ShapeDtypeStruct((bl, N_HG, 8, 128), F32)],
        args=[sink8, q_all, kv_all, do], prefetch=1, sem=("parallel", "parallel", "arbitrary"), comm=comm)
    return outs if comm is None else (outs, couts)


def _silu(x):
    return x * jax.nn.sigmoid(x)


def _ada_fwd(c_rows, w_ada, b_cols, name):
    nl, d, w = w_ada.shape
    r = c_rows.shape[0]

    def body(c_ref, w_ref, b_ref, o_ref):
        s = _silu(c_ref[...]).astype(BF16)
        o_ref[...] = jnp.dot(s, w_ref[...].astype(BF16), preferred_element_type=F32) + b_ref[...]

    return pl.pallas_call(
        body, name=name, grid=(nl,),
        in_specs=[pl.BlockSpec((r, d), lambda l: (0, 0)), pl.BlockSpec((None, d, w), lambda l: (l, 0, 0)),
                  pl.BlockSpec((None, 1, w), lambda l: (l, 0, 0))],
        out_specs=pl.BlockSpec((None, r, w), lambda l: (l, 0, 0)),
        out_shape=jax.ShapeDtypeStruct((nl, r, w), F32), compiler_params=_params(("parallel",)),
    )(c_rows, w_ada, b_cols)


def _ada_bwd(c_rows, c_ctx, dmod, w_ada, name):
    nl, d, w = w_ada.shape
    r = c_rows.shape[0]

    def body(c_ref, cc_ref, g_ref, w_ref, dw_ref, dc_ref):
        l = pl.program_id(0)
        s = _silu(c_ref[...]).astype(BF16)
        gm = g_ref[...].astype(BF16)
        dw_ref[...] = lax.dot_general(s, gm, (((0,), (0,)), ((), ())), preferred_element_type=F32)
        ds = lax.dot_general(gm, w_ref[...].astype(BF16), (((1,), (1,)), ((), ())), preferred_element_type=F32)
        rows = lax.broadcasted_iota(jnp.int32, ds.shape, 0)
        dsc = jnp.sum(jnp.where(rows % ADA_ROWS == 2, ds, 0.0), axis=0, keepdims=True)
        x = cc_ref[...]
        sg = jax.nn.sigmoid(x)
        dcc = dsc * (sg * (1.0 + x * (1.0 - sg)))
        out = jnp.concatenate([dcc, jnp.zeros((7, d), F32)], axis=0)

        @pl.when(l == 0)
        def _():
            dc_ref[...] = out

        @pl.when(l != 0)
        def _():
            dc_ref[...] += out

    return pl.pallas_call(
        body, name=name, grid=(nl,),
        in_specs=[pl.BlockSpec((r, d), lambda l: (0, 0)), pl.BlockSpec((1, d), lambda l: (0, 0)),
                  pl.BlockSpec((None, r, w), lambda l: (l, 0, 0)), pl.BlockSpec((None, d, w), lambda l: (l, 0, 0))],
        out_specs=[pl.BlockSpec((None, d, w), lambda l: (l, 0, 0)), pl.BlockSpec((8, d), lambda l: (0, 0))],
        out_shape=[jax.ShapeDtypeStruct((nl, d, w), F32), jax.ShapeDtypeStruct((8, d), F32)],
        compiler_params=_params(("arbitrary",)),
    )(c_rows, c_ctx, dmod, w_ada)


def _adam_math(w, g, m, v):
    m = ADAM_B1 * m + (1.0 - ADAM_B1) * g
    v = ADAM_B2 * v + (1.0 - ADAM_B2) * (g * g)
    m_hat = m / (1.0 - ADAM_B1 ** ADAM_STEP)
    v_hat = v / (1.0 - ADAM_B2 ** ADAM_STEP)
    delta = -ADAM_LR * (m_hat / (jnp.sqrt(v_hat) + ADAM_EPS) + ADAM_WD * w)
    return delta, m, v


def _adamw(w, m, v, g_own, g_recv, name, rows=256):
    nl, r, c = w.shape
    tr = min(rows, r)
    spec = pl.BlockSpec((None, tr, c), lambda l, i: (l, i, 0))
    per_layer = isinstance(g_own, (list, tuple))
    own = list(g_own) if per_layer else [g_own]
    recv = [] if g_recv is None else list(g_recv)
    in_specs = [spec] * 3 + [pl.BlockSpec((tr, c), lambda l, i: (i, 0)) if per_layer else spec] * len(own)
    in_specs += [pl.BlockSpec((N_DEV, tr, c), lambda l, i: (0, i, 0))] * len(recv)

    def body(*refs):
        w_ref, m_ref, v_ref = refs[:3]
        own_refs, recv_refs = refs[3:3 + len(own)], refs[3 + len(own):3 + len(own) + len(recv)]
        go_ref, d_ref, mo_ref, vo_ref = refs[-4:]

        def update(li):
            g = own_refs[li][...].astype(F32)
            if recv:
                for k in range(1, N_DEV):
                    g = g + recv_refs[li][k].astype(F32)
            delta, m_, v_ = _adam_math(w_ref[...], g, m_ref[...], v_ref[...])
            go_ref[...] = g
            d_ref[...] = delta
            mo_ref[...] = m_
            vo_ref[...] = v_

        if per_layer:
            for li in range(nl):
                pl.when(pl.program_id(0) == li)(functools.partial(update, li))
        else:
            update(0)

    return pl.pallas_call(
        body, name=name, grid=(nl, r // tr), in_specs=in_specs, out_specs=[spec] * 4,
        out_shape=[jax.ShapeDtypeStruct(w.shape, F32)] * 4, compiler_params=_params(("parallel", "parallel")),
    )(w, m, v, *own, *recv)


def _small_adamw(w, m, v, g_all, name):
    def body(w_ref, m_ref, v_ref, g_ref, go_ref, d_ref, mo_ref, vo_ref):
        g = g_ref[0]
        for k in range(1, N_DEV):
            g = g + g_ref[k]
        delta, m_, v_ = _adam_math(w_ref[...], g, m_ref[...], v_ref[...])
        go_ref[...] = g
        d_ref[...] = delta
        mo_ref[...] = m_
        vo_ref[...] = v_

    return pl.pallas_call(
        body, name=name, out_shape=[jax.ShapeDtypeStruct(w.shape, F32)] * 4, compiler_params=_params(),
    )(w, m, v, g_all)


SMALL = ("c_ctx", "b_ada", "g_pre_mix", "g_post_mix", "g_pre_mlp", "g_post_mlp", "q_norm", "k_norm", "sink")


def _pack_small(parts):
    flat = jnp.concatenate([parts[n].reshape(-1) for n in SMALL])
    rows = -(-flat.shape[0] // 1024) * 8
    return jnp.pad(flat, (0, rows * 128 - flat.shape[0])).reshape(rows, 128)


def _unpack_small(packed, like):
    flat = packed.reshape(-1)
    out, off = {}, 0
    for n in SMALL:
        size = int(np.prod(like[n].shape))
        out[n] = flat[off:off + size].reshape(like[n].shape)
        off += size
    return out


def kernel(x, c, ctx, c_ctx, w_ada, b_ada, g_pre_mix, g_post_mix, g_pre_mlp, g_post_mlp, w_in, q_norm, k_norm, sink, w_out, w_up, w_down, loss_target, m_c_ctx, m_w_ada, m_b_ada, m_g_pre_mix, m_g_post_mix, m_g_pre_mlp, m_g_post_mlp, m_w_in, m_q_norm, m_k_norm, m_sink, m_w_out, m_w_up, m_w_down, v_c_ctx, v_w_ada, v_b_ada, v_g_pre_mix, v_g_post_mix, v_g_pre_mlp, v_g_post_mlp, v_w_in, v_q_norm, v_k_norm, v_sink, v_w_out, v_w_up, v_w_down):
    bl, s_rows, d = x.shape
    c_rows = ctx.shape[1]
    assert c_rows == TM and s_rows % TM == 0 and bl == 2
    t_rows = c_rows + s_rows
    nt = t_rows // TM
    m_rows = bl * t_rows
    nl = w_in.shape[0]
    ada_w = w_ada.shape[2]
    d_ff = w_up.shape[2] * N_DEV
    me = _my_index()

    shard = lambda w_, l: w_[l].astype(BF16)
    c_pad = jnp.concatenate([c, c_ctx[None, :], jnp.zeros((ADA_ROWS - bl - 1, d), F32)], axis=0)
    c_all, w_in_g, w_out_g = _comm_only(
        _Comm([(c_pad, GATHER), (shard(w_in, 0), GATHER), (shard(w_out, 0), GATHER)]), "gather_first")
    c_all = c_all.reshape(N_DEV * ADA_ROWS, d)
    gathered = {0: dict(w_in=w_in_g, w_out=w_out_g)}

    def layer_weights(l):
        g_ = gathered[l]
        w_out_f = g_["w_out"].reshape(-1, d)
        w_down_f = g_["w_down"].reshape(d_ff, d)
        return dict(
            w_in_t=g_["w_in"].transpose(0, 2, 1).reshape(PROJ_W, d), w_out_f=w_out_f, w_out_t=w_out_f.T,
            w_up_s=g_["w_up"], w_up_t=g_["w_up"].transpose(0, 2, 1).reshape(d_ff, d),
            w_down_f=w_down_f, w_down_t=w_down_f.T)

    big = dict(tm=1536, tn=512)
    wide = dict(tm=1024, tn=512, tk=1536)

    b_cols = lax.dynamic_slice(b_ada, (0, me * ada_w), (nl, ada_w))[:, None, :]
    mod_cols = _ada_fwd(c_all, w_ada, b_cols, "ada_fwd")
    mod_slots = mod_cols.reshape(nl, N_DEV, ADA_ROWS, ada_w).transpose(1, 0, 2, 3)
    mod_g, = _comm_only(_Comm([(mod_slots, TO_OWNER)]), "exchange_mod")
    mine = mod_g.transpose(1, 2, 0, 3).reshape(nl, ADA_ROWS, N_MOD, d)
    pad = jnp.zeros((bl, 2, MOD_ROWS - N_MOD, d), F32)
    modsel = [jnp.concatenate([jnp.stack([jnp.broadcast_to(mine[l, bl], (bl, N_MOD, d)), mine[l, :bl]], axis=1), pad],
                              axis=2) for l in range(nl)]

    cos, sin = _rope_tables(t_rows, c_rows)
    emat = _head_mean_matrix()
    row = lambda a: a[None, :]
    qn = [jnp.tile(q_norm[l], 8)[None, :] for l in range(nl)]
    kn = [jnp.tile(k_norm[l], 2)[None, :] for l in range(nl)]

    h = jnp.concatenate([ctx, x], axis=1).reshape(m_rows, d)
    saved = []
    weights_of = {}
    for l in range(nl):
        u, u_t = _norm_mod_fwd(h, row(g_pre_mix[l]), modsel[l], 0, 1, nt, f"mix_mod_fwd{l}")
        w_in_f = gathered[l]["w_in"].transpose(1, 0, 2).reshape(d, PROJ_W)
        proj = _mm(u, w_in_f, name=f"mm_in{l}", tk=d, **big)
        q_all, kv_all = _prep_fwd(proj, qn[l], kn[l], cos, sin, emat, nt, f"prep_fwd{l}")
        (o, o_t), (w_up_g, w_down_g) = _attn_fwd(
            q_all.reshape(bl, t_rows, 1024), kv_all.reshape(bl, t_rows, 512), sink[l], c_rows, f"attn_fwd{l}",
            comm=_Comm([(shard(w_up, l), GATHER), (shard(w_down, l), GATHER)]))
        o = o.reshape(m_rows, 1024)
        gathered[l].update(w_up=w_up_g, w_down=w_down_g)
        wl = weights_of[l] = layer_weights(l)
        mix = _mm(o, wl["w_out_f"], name=f"mm_out{l}", tk=1024, **big)
        h_mid = _gate_fwd(h, mix, row(g_post_mix[l]), modsel[l], 2, nt, f"mix_gate_fwd{l}")
        v_in, v_t = _norm_mod_fwd(h_mid, row(g_pre_mlp[l]), modsel[l], 3, 4, nt, f"mlp_mod_fwd{l}")
        more = l + 1 < nl
        res_up = _mm(v_in, wl["w_up_s"], name=f"mm_up{l}", b_mode="nn_slots", epilogue="relu2", tk=d,
                     comm=_Comm([(shard(w_in, l + 1), GATHER)]) if more else None, **big)
        (r_act, s_act, r_t), nxt_in = res_up if more else (res_up, None)
        res_down = _mm(r_act, wl["w_down_f"], name=f"mm_down{l}", tk=1024,
                       comm=_Comm([(shard(w_out, l + 1), GATHER)]) if more else None, **big)
        y, nxt_out = res_down if more else (res_down, None)
        if more:
            gathered[l + 1] = dict(w_in=nxt_in[0], w_out=nxt_out[0])
        h_out = _gate_fwd(h_mid, y, row(g_post_mlp[l]), modsel[l], 5, nt, f"mlp_gate_fwd{l}")
        saved.append((h, u_t, proj, q_all, kv_all, o_t, mix, h_mid, v_t, r_t, s_act, y))
        h = h_out

    dh, ss = _loss_grad(h, loss_target.reshape(bl * s_rows, d), c_rows, nt, "loss_grad")
    loss = lax.psum(0.5 * ss[0, 0] / d, MESH_AXES)

    small_g = {n: [None] * nl for n in SMALL if n not in ("c_ctx", "b_ada")}
    dmod_rows = []
    slots = {n: [None] * nl for n in ("w_in", "w_out", "w_up", "w_down")}
    recvd = {n: [None] * nl for n in slots}
    send = lambda n, l_: (slots[n][l_], TO_OWNER_XOR)
    for l in reversed(range(nl)):
        h_in, u_t, proj, q_all, kv_all, o_t, mix, h_mid, v_t, r_t, s_act, y = saved[l]
        wl = weights_of[l]
        later = l + 1 < nl
        dy, dvec_g2 = _gate_bwd(y, row(g_post_mlp[l]), modsel[l], dh, 5, nt, f"mlp_gate_bwd{l}")
        res = _mm(dy, wl["w_down_t"], name=f"mm_da{l}", epilogue="mul", extra=s_act, out_dtype=BF16, tk=d,
                  comm=_Comm([send("w_out", l + 1)]) if later else None, **big)
        da = res[0] if later else res
        if later:
            recvd["w_out"][l + 1] = res[1][0]
        res = _mm(r_t, dy, name=f"mm_dw_down{l}", out_dtype=BF16,
                  comm=_Comm([send("w_in", l + 1)]) if later else None, **wide)
        dw_down = res[0] if later else res
        if later:
            recvd["w_in"][l + 1] = res[1][0]
        dw_up = _mm(v_t, da, name=f"mm_dw_up{l}", out_mode="slots", out_dtype=BF16, **wide)
        slots["w_down"][l] = dw_down.reshape(N_DEV, -1, d)
        slots["w_up"][l] = dw_up
        dv = _mm(da, wl["w_up_t"], name=f"mm_dv{l}", tk=1024, **big)
        dh, dvec_m2 = _norm_mod_bwd(h_mid, row(g_pre_mlp[l]), modsel[l], dv, dh, 3, 4, nt, f"mlp_mod_bwd{l}")
        dmix, dvec_g1 = _gate_bwd(mix, row(g_post_mix[l]), modsel[l], dh, 2, nt, f"mix_gate_bwd{l}")
        do = _mm(dmix, wl["w_out_t"], name=f"mm_do{l}", out_dtype=BF16, tk=d, **big)
        dw_out = _mm(o_t, dmix, name=f"mm_dw_out{l}", out_dtype=BF16, **wide)
        slots["w_out"][l] = dw_out.reshape(N_DEV, -1, d)
        (dq, dkv, dsk), (recvd["w_down"][l], recvd["w_up"][l]) = _attn_bwd(
            q_all.reshape(bl, t_rows, 1024), kv_all.reshape(bl, t_rows, 512), do.reshape(bl, t_rows, 1024), sink[l],
            c_rows, f"attn_bwd{l}", comm=_Comm([send("w_down", l), send("w_up", l)]))
        last = l == 0
        res = _prep_bwd(proj, dq.reshape(m_rows, 1024), dkv.reshape(m_rows, 512), qn[l], kn[l], cos, sin, emat, nt,
                        f"prep_bwd{l}", comm=_Comm([send("w_out", l)]) if last else None)
        dproj, dqn, dkn = res[0] if last else res
        if last:
            recvd["w_out"][l] = res[1][0]
        dw_in = _mm(u_t, dproj, name=f"mm_dw_in{l}", out_dtype=BF16, **wide)
        slots["w_in"][l] = dw_in.reshape(d, N_DEV, PROJ_W // N_DEV).transpose(1, 0, 2)
        res = _mm(dproj, wl["w_in_t"], name=f"mm_du{l}", tk=PROJ_W,
                  comm=_Comm([send("w_in", l)]) if last else None, **big)
        du = res[0] if last else res
        if last:
            recvd["w_in"][l] = res[1][0]
        dh, dvec_m1 = _norm_mod_bwd(h_in, row(g_pre_mix[l]), modsel[l], du, dh, 0, 1, nt, f"mix_mod_bwd{l}")

        small_g["g_pre_mix"][l] = jnp.sum(dvec_m1[:, :, 2], axis=(0, 1))
        small_g["g_post_mix"][l] = jnp.sum(dvec_g1[:, :, 1], axis=(0, 1))
        small_g["g_pre_mlp"][l] = jnp.sum(dvec_m2[:, :, 2], axis=(0, 1))
        small_g["g_post_mlp"][l] = jnp.sum(dvec_g2[:, :, 1], axis=(0, 1))
        small_g["q_norm"][l] = jnp.sum(dqn[0].reshape(8, HEAD_DIM), axis=0)
        small_g["k_norm"][l] = jnp.sum(dkn[0].reshape(2, HEAD_DIM), axis=0)
        small_g["sink"][l] = jnp.sum(dsk[:, 2:, :GROUP, 0], axis=0).reshape(-1)
        dms = jnp.stack([dvec_m1[:, :, 0], dvec_m1[:, :, 1], dvec_g1[:, :, 0],
                         dvec_m2[:, :, 0], dvec_m2[:, :, 1], dvec_g2[:, :, 0]], axis=2)
        rows = jnp.concatenate([dms[:, 1], jnp.sum(dms[:, 0], axis=0)[None]], axis=0)
        dmod_rows.append(jnp.pad(rows.reshape(bl + 1, N_MOD * d), ((0, ADA_ROWS - bl - 1), (0, 0))))
    dmod_rows = dmod_rows[::-1]
    grad_x = dh.reshape(bl, t_rows, d)[:, c_rows:]

    dmod_slots = jnp.stack(dmod_rows).reshape(nl, ADA_ROWS, N_DEV, ada_w).transpose(2, 0, 1, 3)
    dmod_g, = _comm_only(_Comm([(dmod_slots, TO_OWNER)]), "exchange_dmod")
    dmod_mine = dmod_g.transpose(1, 0, 2, 3).reshape(nl, N_DEV * ADA_ROWS, ada_w)
    dw_ada, dcc = _ada_bwd(c_all, c_ctx[None, :], dmod_mine, w_ada, "ada_bwd")

    parts = {n: jnp.stack(small_g[n]) for n in small_g}
    parts["c_ctx"] = dcc[0]
    parts["b_ada"] = jnp.stack([jnp.sum(r_[: bl + 1], axis=0) for r_ in dmod_rows])
    weights = dict(c_ctx=c_ctx, b_ada=b_ada, g_pre_mix=g_pre_mix, g_post_mix=g_post_mix, g_pre_mlp=g_pre_mlp,
                   g_post_mlp=g_post_mlp, q_norm=q_norm, k_norm=k_norm, sink=sink)
    moms = dict(c_ctx=m_c_ctx, b_ada=m_b_ada, g_pre_mix=m_g_pre_mix, g_post_mix=m_g_post_mix, g_pre_mlp=m_g_pre_mlp,
                g_post_mlp=m_g_post_mlp, q_norm=m_q_norm, k_norm=m_k_norm, sink=m_sink)
    vels = dict(c_ctx=v_c_ctx, b_ada=v_b_ada, g_pre_mix=v_g_pre_mix, g_post_mix=v_g_post_mix, g_pre_mlp=v_g_pre_mlp,
                g_post_mlp=v_g_post_mlp, q_norm=v_q_norm, k_norm=v_k_norm, sink=v_sink)
    small_all, = _comm_only(_Comm([(_pack_small(parts), GATHER)]), "gather_small")
    s_out = _small_adamw(_pack_small(weights), _pack_small(moms), _pack_small(vels), small_all, "adamw_small")
    s_g, s_d, s_m, s_v = [_unpack_small(a, weights) for a in s_out]

    res = {}
    for n, w_, m_, v_ in (("w_in", w_in, m_w_in, v_w_in), ("w_out", w_out, m_w_out, v_w_out),
                          ("w_up", w_up, m_w_up, v_w_up), ("w_down", w_down, m_w_down, v_w_down)):
        own = [lax.dynamic_index_in_dim(slots[n][l], me, axis=0, keepdims=False) for l in range(nl)]
        res[n] = _adamw(w_, m_, v_, own, recvd[n], f"adamw_{n}")
    res["w_ada"] = _adamw(w_ada, m_w_ada, v_w_ada, dw_ada, None, "adamw_w_ada")

    order = ("c_ctx", "w_ada", "b_ada", "g_pre_mix", "g_post_mix", "g_pre_mlp", "g_post_mlp", "w_in", "q_norm",
             "k_norm", "sink", "w_out", "w_up", "w_down")
    outs = [loss, grad_x]
    for i, small in enumerate((s_g, s_d, s_m, s_v)):
        outs += [small[n] if n in small else res[n][i] for n in order]
    return tuple(outs)
```

```python
import functools

import jax
import jax.numpy as jnp
import numpy as np
from jax import lax
from jax.experimental import pallas as pl
from jax.experimental.pallas import tpu as pltpu

F32 = jnp.float32
BF16 = jnp.bfloat16

HEAD_DIM = 64
GROUP = 4
N_HG = 4
Q_WIDTH = GROUP * HEAD_DIM
WINDOW = 128
GRID_W = 64
ROPE_THETA = 10000.0
EPS = 1e-6
NEG_BIG = -1e30
N_MOD = 6
MOD_ROWS = 8
TM = 256
N_DEV = 8
ADA_ROWS = 8
VMEM_LIMIT = 56 * 1024 * 1024

ADAM_LR = 0.001
ADAM_B1 = 0.9
ADAM_B2 = 0.999
ADAM_EPS = 1e-08
ADAM_WD = 0.01
ADAM_STEP = 10

MESH_AXES = ("x", "y", "c")


def _params(sem=None):
    kw = dict(vmem_limit_bytes=VMEM_LIMIT)
    if sem is not None:
        kw["dimension_semantics"] = sem
    return pltpu.CompilerParams(**kw)


def _my_index():
    return 4 * lax.axis_index("x") + 2 * lax.axis_index("y") + lax.axis_index("c")


def _peer(k):
    x, y, c = lax.axis_index("x"), lax.axis_index("y"), lax.axis_index("c")
    kx, ky, kc = (k >> 2) & 1, (k >> 1) & 1, k & 1
    px = (1 - x) if kx else x
    py = (1 - y) if ky else y
    pc = (1 - c) if kc else c
    return (px, py, pc), 4 * px + 2 * py + pc


GATHER, TO_OWNER, TO_OWNER_XOR = "gather", "to_owner", "to_owner_xor"


class _Comm:
    def __init__(self, items):
        self.items = list(items)
        self.arrays = [a for a, _ in self.items]

    def out_shapes(self):
        return [jax.ShapeDtypeStruct(((N_DEV,) + a.shape) if kind == GATHER else a.shape, a.dtype)
                for a, kind in self.items]

    def sem_shapes(self):
        n = len(self.items) * N_DEV
        return [pltpu.SemaphoreType.DMA((n,)), pltpu.SemaphoreType.DMA((n,))]

    def _copies(self, in_refs, out_refs, send_sems, recv_sems):
        me = _my_index()
        local, remote = [], []
        for i, ((_, kind), x_ref, o_ref) in enumerate(zip(self.items, in_refs, out_refs)):
            base = i * N_DEV
            own_src = x_ref if kind == GATHER else x_ref.at[me]
            own_dst = o_ref.at[0] if kind == TO_OWNER_XOR else o_ref.at[me]
            local.append(pltpu.make_async_copy(own_src, own_dst, send_sems.at[base]))
            for k in range(1, N_DEV):
                peer, pidx = _peer(k)
                remote.append(pltpu.make_async_remote_copy(
                    src_ref=x_ref if kind == GATHER else x_ref.at[pidx],
                    dst_ref=o_ref.at[k] if kind == TO_OWNER_XOR else o_ref.at[me],
                    send_sem=send_sems.at[base + k], recv_sem=recv_sems.at[base + k],
                    device_id=peer, device_id_type=pl.DeviceIdType.MESH))
        return local, remote

    def start(self, in_refs, out_refs, send_sems, recv_sems):
        local, remote = self._copies(in_refs, out_refs, send_sems, recv_sems)
        for cp in local + remote:
            cp.start()

    def wait(self, in_refs, out_refs, send_sems, recv_sems):
        local, remote = self._copies(in_refs, out_refs, send_sems, recv_sems)
        for cp in remote:
            cp.wait_recv()
        for cp in remote:
            cp.wait_send()
        for cp in local:
            cp.wait()


def _call(body, *, name, grid, in_specs, out_specs, out_shape, args, scratch_shapes=(), prefetch=0, sem=None,
          comm=None):
    single = not isinstance(out_shape, (list, tuple))
    out_shape = [out_shape] if single else list(out_shape)
    out_specs = [out_specs] if single else list(out_specs)
    in_specs, scratch_shapes, args = list(in_specs), list(scratch_shapes), list(args)
    n_in, n_out = len(in_specs), len(out_shape)
    if comm is not None:
        nc = len(comm.arrays)
        hbm = pl.BlockSpec(memory_space=pl.ANY)
        inner = body

        def body(*refs):
            pre, r = refs[:prefetch], refs[prefetch:]
            ins, cin = r[:n_in], r[n_in:n_in + nc]
            outs, cout = r[n_in + nc:n_in + nc + n_out], r[n_in + nc + n_out:n_in + 2 * nc + n_out]
            scr, sems = r[n_in + 2 * nc + n_out:len(r) - 2], r[len(r) - 2:]
            ids = [pl.program_id(i) for i in range(len(grid))]

            def when(flags, fn):
                if flags:
                    pl.when(functools.reduce(jnp.logical_and, flags))(fn)
                else:
                    fn()

            when([i == 0 for i in ids], lambda: comm.start(cin, cout, *sems))
            inner(*pre, *ins, *outs, *scr)
            when([i == n - 1 for i, n in zip(ids, grid)], lambda: comm.wait(cin, cout, *sems))

        in_specs += [hbm] * nc
        out_specs += [hbm] * nc
        out_shape += comm.out_shapes()
        scratch_shapes += comm.sem_shapes()
        args += comm.arrays
        sem = ("arbitrary",) * len(grid)
    kw = dict(name=name, out_shape=out_shape, compiler_params=_params(sem if grid else None))
    if prefetch:
        kw["grid_spec"] = pltpu.PrefetchScalarGridSpec(
            num_scalar_prefetch=prefetch, grid=grid, in_specs=in_specs, out_specs=out_specs,
            scratch_shapes=scratch_shapes)
    else:
        kw.update(in_specs=in_specs, out_specs=out_specs, scratch_shapes=scratch_shapes)
        if grid:
            kw["grid"] = grid
    res = list(pl.pallas_call(body, **kw)(*args))
    outs = res[:n_out]
    return (outs[0] if single else outs), res[n_out:]


def _comm_only(comm, name):
    return _call(lambda: None, name=name, grid=(), in_specs=[], out_specs=[], out_shape=[], args=[], comm=comm)[1]


def _mm(a, b, *, name, ta=False, b_mode="nn", out_mode="plain", out_dtype=F32, tm=512, tn=512, tk=512,
        epilogue=None, extra=None, comm=None):
    if ta:
        kdim, m = a.shape
    else:
        m, kdim = a.shape
    if b_mode == "nn":
        n = b.shape[1]
    elif b_mode == "nt":
        n = b.shape[0]
    elif b_mode == "nn_slots":
        n = b.shape[0] * b.shape[2]
        tn = b.shape[2]
    else:
        n = b.shape[1]
        tk = b.shape[2]
    if out_mode == "slots":
        tn = n // N_DEV
    tm, tn, tk = min(tm, m), min(tn, n), min(tk, kdim)
    assert m % tm == 0 and n % tn == 0 and kdim % tk == 0, (name, m, n, kdim, tm, tn, tk)
    nk = kdim // tk

    a_spec = pl.BlockSpec((tk, tm), lambda i, j, k: (k, i)) if ta else pl.BlockSpec((tm, tk), lambda i, j, k: (i, k))
    if b_mode == "nn":
        b_spec = pl.BlockSpec((tk, tn), lambda i, j, k: (k, j))
    elif b_mode == "nt":
        b_spec = pl.BlockSpec((tn, tk), lambda i, j, k: (j, k))
    elif b_mode == "nn_slots":
        b_spec = pl.BlockSpec((None, tk, tn), lambda i, j, k: (j, k, 0))
    else:
        b_spec = pl.BlockSpec((None, tn, tk), lambda i, j, k: (k, j, 0))
    tb = b_mode in ("nt", "nt_slots")
    if out_mode == "plain":
        o_shape, o_spec = (m, n), pl.BlockSpec((tm, tn), lambda i, j, k: (i, j))
    else:
        o_shape, o_spec = (N_DEV, m, tn), pl.BlockSpec((None, tm, tn), lambda i, j, k: (j, i, 0))
    dims = (((0 if ta else 1,), (1 if tb else 0,)), ((), ()))

    in_specs = [a_spec, b_spec]
    args = [a, b]
    if epilogue == "mul":
        in_specs.append(pl.BlockSpec((tm, tn), lambda i, j, k: (i, j)))
        args.append(extra)
    if epilogue == "relu2":
        out_shape = [jax.ShapeDtypeStruct(o_shape, BF16)] * 2 + [jax.ShapeDtypeStruct((n, m), BF16)]
        out_specs = [o_spec, o_spec, pl.BlockSpec((tn, tm), lambda i, j, k: (j, i))]
    else:
        out_shape = jax.ShapeDtypeStruct(o_shape, out_dtype)
        out_specs = o_spec

    def finish(refs, acc):
        if epilogue == "relu2":
            r = jnp.maximum(acc, 0.0)
            r2 = (r * r).astype(BF16)
            refs[2][...] = r2
            refs[3][...] = (2.0 * r).astype(BF16)
            refs[4][...] = r2.T
        elif epilogue == "mul":
            refs[3][...] = (acc * refs[2][...].astype(F32)).astype(out_dtype)
        else:
            refs[2][...] = acc.astype(out_dtype)

    def body(*refs):
        part = lax.dot_general(refs[0][...], refs[1][...], dims, preferred_element_type=F32)
        if nk == 1:
            finish(refs, part)
            return
        acc_ref = refs[-1]
        k = pl.program_id(2)

        @pl.when(k == 0)
        def _():
            acc_ref[...] = part

        @pl.when(jnp.logical_and(k > 0, k < nk - 1))
        def _():
            acc_ref[...] += part

        @pl.when(k == nk - 1)
        def _():
            finish(refs, acc_ref[...] + part)

    outs, couts = _call(
        body, name=name, grid=(m // tm, n // tn, nk), in_specs=in_specs, out_specs=out_specs, out_shape=out_shape,
        args=args, scratch_shapes=[] if nk == 1 else [pltpu.VMEM((tm, tn), F32)],
        sem=("parallel", "parallel", "arbitrary"), comm=comm)
    return outs if comm is None else (outs, couts)


def _row_specs(d, nt):
    row = pl.BlockSpec((TM, d), lambda t: (t, 0))
    vec = pl.BlockSpec((1, d), lambda t: (0, 0))
    mod = pl.BlockSpec((None, None, MOD_ROWS, d), lambda t: (t // nt, jnp.minimum(t % nt, 1), 0, 0))
    return row, vec, mod


def _rms(x):
    r = lax.rsqrt(jnp.mean(x * x, axis=1, keepdims=True) + EPS)
    return r, x * r


def _norm_mod_fwd(h, g, modsel, i_sh, i_sc, nt, name):
    m, d = h.shape
    row, vec, mod = _row_specs(d, nt)

    def body(h_ref, g_ref, ms_ref, u_ref, ut_ref):
        _, xh = _rms(h_ref[...])
        ms = ms_ref[...]
        u = (xh * g_ref[...] * (1.0 + ms[i_sc:i_sc + 1]) + ms[i_sh:i_sh + 1]).astype(BF16)
        u_ref[...] = u
        ut_ref[...] = u.T

    return pl.pallas_call(
        body, name=name, grid=(m // TM,), in_specs=[row, vec, mod],
        out_specs=[row, pl.BlockSpec((d, TM), lambda t: (0, t))],
        out_shape=[jax.ShapeDtypeStruct((m, d), BF16), jax.ShapeDtypeStruct((d, m), BF16)],
        compiler_params=_params(("parallel",)),
    )(h, g, modsel)


def _acc_rows(t, nt, dvec_ref, rows):
    first = (t % nt) <= 1

    @pl.when(first)
    def _():
        dvec_ref[...] = rows

    @pl.when(jnp.logical_not(first))
    def _():
        dvec_ref[...] += rows


def _norm_mod_bwd(h, g, modsel, du, dh_in, i_sh, i_sc, nt, name, comm=None):
    m, d = h.shape
    row, vec, mod = _row_specs(d, nt)

    def body(h_ref, g_ref, ms_ref, du_ref, dhi_ref, dh_ref, dvec_ref):
        t = pl.program_id(0)
        r, xh = _rms(h_ref[...])
        g_ = g_ref[...]
        ms = ms_ref[...]
        du_ = du_ref[...]
        y = xh * g_
        dy = du_ * (1.0 + ms[i_sc:i_sc + 1])
        dxh = dy * g_
        dx = r * (dxh - xh * jnp.mean(dxh * xh, axis=1, keepdims=True))
        dh_ref[...] = dhi_ref[...] + dx
        rows = jnp.concatenate([
            jnp.sum(du_, axis=0, keepdims=True), jnp.sum(du_ * y, axis=0, keepdims=True),
            jnp.sum(dy * xh, axis=0, keepdims=True), jnp.zeros((MOD_ROWS - 3, d), F32)], axis=0)
        _acc_rows(t, nt, dvec_ref, rows)

    outs, couts = _call(
        body, name=name, grid=(m // TM,), in_specs=[row, vec, mod, row, row], out_specs=[row, mod],
        out_shape=[jax.ShapeDtypeStruct((m, d), F32), jax.ShapeDtypeStruct(modsel.shape, F32)],
        args=[h, g, modsel, du, dh_in], sem=("arbitrary",), comm=comm)
    return outs if comm is None else (outs, couts)


def _gate_fwd(h, z, g, modsel, i_g, nt, name):
    m, d = h.shape
    row, vec, mod = _row_specs(d, nt)

    def body(h_ref, z_ref, g_ref, ms_ref, o_ref):
        _, xh = _rms(z_ref[...])
        ms = ms_ref[...]
        o_ref[...] = h_ref[...] + ms[i_g:i_g + 1] * (xh * g_ref[...])

    return pl.pallas_call(
        body, name=name, grid=(m // TM,), in_specs=[row, row, vec, mod], out_specs=row,
        out_shape=jax.ShapeDtypeStruct((m, d), F32), compiler_params=_params(("parallel",)),
    )(h, z, g, modsel)


def _gate_bwd(z, g, modsel, dh, i_g, nt, name):
    m, d = z.shape
    row, vec, mod = _row_specs(d, nt)

    def body(z_ref, g_ref, ms_ref, dh_ref, dz_ref, dvec_ref):
        t = pl.program_id(0)
        r, xh = _rms(z_ref[...])
        g_ = g_ref[...]
        ms = ms_ref[...]
        dh_ = dh_ref[...]
        dy = dh_ * ms[i_g:i_g + 1]
        dxh = dy * g_
        dz_ref[...] = (r * (dxh - xh * jnp.mean(dxh * xh, axis=1, keepdims=True))).astype(BF16)
        rows = jnp.concatenate([
            jnp.sum(dh_ * (xh * g_), axis=0, keepdims=True), jnp.sum(dy * xh, axis=0, keepdims=True),
            jnp.zeros((MOD_ROWS - 2, d), F32)], axis=0)
        _acc_rows(t, nt, dvec_ref, rows)

    return pl.pallas_call(
        body, name=name, grid=(m // TM,), in_specs=[row, vec, mod, row], out_specs=[row, mod],
        out_shape=[jax.ShapeDtypeStruct((m, d), BF16), jax.ShapeDtypeStruct(modsel.shape, F32)],
        compiler_params=_params(("arbitrary",)),
    )(z, g, modsel, dh)


def _loss_grad(h, target, c_rows, nt, name):
    m, d = h.shape
    row = pl.BlockSpec((TM, d), lambda t: (t, 0))
    ntl = nt - 1
    tgt = pl.BlockSpec((TM, d), lambda t: ((t // nt) * ntl + jnp.maximum(t % nt, 1) - 1, 0))
    acc = pl.BlockSpec((8, 128), lambda t: (0, 0))

    def body(h_ref, t_ref, dh_ref, ss_ref):
        t = pl.program_id(0)

        @pl.when(t == 0)
        def _():
            ss_ref[...] = jnp.zeros_like(ss_ref)

        @pl.when(t % nt == 0)
        def _():
            dh_ref[...] = jnp.zeros_like(dh_ref)

        @pl.when(t % nt != 0)
        def _():
            e = h_ref[...] - t_ref[...]
            dh_ref[...] = e * (1.0 / d)
            ss_ref[...] += jnp.sum(e * e)

    return pl.pallas_call(
        body, name=name, grid=(m // TM,), in_specs=[row, tgt], out_specs=[row, acc],
        out_shape=[jax.ShapeDtypeStruct((m, d), F32), jax.ShapeDtypeStruct((8, 128), F32)],
        compiler_params=_params(("arbitrary",)),
    )(h, target)


QA, KA, VA, QB, KB, VB = 0, 512, 640, 768, 1280, 1408
PROJ_W = 1536
Q_SCALE = HEAD_DIM ** -0.5


def _swap16(x):
    lane = lax.broadcasted_iota(jnp.int32, x.shape, 1)
    n = x.shape[1]
    return jnp.where((lane % 32) < 16, pltpu.roll(x, n - 16, 1), pltpu.roll(x, 16, 1))


def _seg_mean(x, e):
    return jnp.dot(x, e, preferred_element_type=F32, precision=lax.Precision.HIGHEST)


def _rope_tables(t_rows, c_rows):
    s = t_rows - c_rows
    row_ids = jnp.repeat(jnp.arange(s // GRID_W, dtype=jnp.int32), GRID_W).astype(F32)
    col_ids = jnp.tile(jnp.arange(GRID_W, dtype=jnp.int32), s // GRID_W).astype(F32)
    axis_dim = HEAD_DIM // 2
    inv = ROPE_THETA ** (-jnp.arange(0, axis_dim, 2, dtype=F32) / axis_dim)
    ang_r = row_ids[:, None] * inv[None, :]
    ang_c = col_ids[:, None] * inv[None, :]
    cos = jnp.concatenate([jnp.cos(ang_r), jnp.cos(ang_r), jnp.cos(ang_c), jnp.cos(ang_c)], axis=1)
    sin = jnp.concatenate([-jnp.sin(ang_r), jnp.sin(ang_r), -jnp.sin(ang_c), jnp.sin(ang_c)], axis=1)
    cos = jnp.concatenate([jnp.ones((c_rows, HEAD_DIM), F32), cos], axis=0)
    sin = jnp.concatenate([jnp.zeros((c_rows, HEAD_DIM), F32), sin], axis=0)
    return jnp.tile(cos, (1, 8)), jnp.tile(sin, (1, 8))


def _head_mean_matrix():
    i = np.arange(512)
    return jnp.asarray((i[:, None] // HEAD_DIM == i[None, :] // HEAD_DIM).astype(np.float32) / HEAD_DIM)


def _interleave_kv(k, v):
    return jnp.concatenate([k[:, :64], v[:, :64], k[:, 64:], v[:, 64:]], axis=1)


def _prep_fwd(proj, qn, kn, cos, sin, emat, nt, name):
    m = proj.shape[0]
    specs = [
        pl.BlockSpec((TM, PROJ_W), lambda t: (t, 0)),
        pl.BlockSpec((1, 512), lambda t: (0, 0)), pl.BlockSpec((1, 128), lambda t: (0, 0)),
        pl.BlockSpec((TM, 512), lambda t: (t % nt, 0)), pl.BlockSpec((TM, 512), lambda t: (t % nt, 0)),
        pl.BlockSpec((512, 512), lambda t: (0, 0)),
    ]

    def body(p_ref, qn_ref, kn_ref, cos_ref, sin_ref, e_ref, q_ref, kv_ref):
        cos_, sin_, e = cos_ref[...], sin_ref[...], e_ref[...]

        def rope(x, w):
            return x * cos_[:, :w] + _swap16(x) * sin_[:, :w]

        def norm(x, g, w):
            return x * lax.rsqrt(_seg_mean(x * x, e[:w, :w]) + EPS) * g

        qa = rope(norm(p_ref[:, QA:QA + 512], qn_ref[...], 512), 512)
        qb = rope(p_ref[:, QB:QB + 512], 512)
        q_ref[:, 0:512] = (qa * Q_SCALE).astype(BF16)
        q_ref[:, 512:1024] = (qb * Q_SCALE).astype(BF16)
        ka = rope(norm(p_ref[:, KA:KA + 128], kn_ref[...], 128), 128)
        kb = rope(p_ref[:, KB:KB + 128], 128)
        kv_ref[:, 0:256] = _interleave_kv(ka, p_ref[:, VA:VA + 128]).astype(BF16)
        kv_ref[:, 256:512] = _interleave_kv(kb, p_ref[:, VB:VB + 128]).astype(BF16)

    return pl.pallas_call(
        body, name=name, grid=(m // TM,), in_specs=specs,
        out_specs=[pl.BlockSpec((TM, 1024), lambda t: (t, 0)), pl.BlockSpec((TM, 512), lambda t: (t, 0))],
        out_shape=[jax.ShapeDtypeStruct((m, 1024), BF16), jax.ShapeDtypeStruct((m, 512), BF16)],
        compiler_params=_params(("parallel",)),
    )(proj, qn, kn, cos, sin, emat)


def _prep_bwd(proj, dq, dkv, qn, kn, cos, sin, emat, nt, name, comm=None):
    m = proj.shape[0]
    specs = [
        pl.BlockSpec((TM, PROJ_W), lambda t: (t, 0)),
        pl.BlockSpec((TM, 1024), lambda t: (t, 0)), pl.BlockSpec((TM, 512), lambda t: (t, 0)),
        pl.BlockSpec((1, 512), lambda t: (0, 0)), pl.BlockSpec((1, 128), lambda t: (0, 0)),
        pl.BlockSpec((TM, 512), lambda t: (t % nt, 0)), pl.BlockSpec((TM, 512), lambda t: (t % nt, 0)),
        pl.BlockSpec((512, 512), lambda t: (0, 0)),
    ]

    def body(p_ref, dq_ref, dkv_ref, qn_ref, kn_ref, cos_ref, sin_ref, e_ref, dp_ref, dqn_ref, dkn_ref):
        t = pl.program_id(0)
        cos_, sin_, e = cos_ref[...], sin_ref[...], e_ref[...]

        @pl.when(t == 0)
        def _():
            dqn_ref[...] = jnp.zeros_like(dqn_ref)
            dkn_ref[...] = jnp.zeros_like(dkn_ref)

        def unrope(dy, w):
            return dy * cos_[:, :w] + _swap16(dy * sin_[:, :w])

        def norm_bwd(x, g, dy, w):
            r = lax.rsqrt(_seg_mean(x * x, e[:w, :w]) + EPS)
            xh = x * r
            dxh = dy * g
            dx = r * (dxh - xh * _seg_mean(dxh * xh, e[:w, :w]))
            return dx, jnp.sum(dy * xh, axis=0, keepdims=True)

        dqa, dgq = norm_bwd(p_ref[:, QA:QA + 512], qn_ref[...], unrope(dq_ref[:, 0:512] * Q_SCALE, 512), 512)
        dp_ref[:, QA:QA + 512] = dqa.astype(BF16)
        dp_ref[:, QB:QB + 512] = unrope(dq_ref[:, 512:1024] * Q_SCALE, 512).astype(BF16)
        da = dkv_ref[:, 0:256]
        db = dkv_ref[:, 256:512]
        dka = jnp.concatenate([da[:, 0:64], da[:, 128:192]], axis=1)
        dva = jnp.concatenate([da[:, 64:128], da[:, 192:256]], axis=1)
        dkb = jnp.concatenate([db[:, 0:64], db[:, 128:192]], axis=1)
        dvb = jnp.concatenate([db[:, 64:128], db[:, 192:256]], axis=1)
        dka, dgk = norm_bwd(p_ref[:, KA:KA + 128], kn_ref[...], unrope(dka, 128), 128)
        dp_ref[:, KA:KA + 128] = dka.astype(BF16)
        dp_ref[:, VA:VA + 128] = dva.astype(BF16)
        dp_ref[:, KB:KB + 128] = unrope(dkb, 128).astype(BF16)
        dp_ref[:, VB:VB + 128] = dvb.astype(BF16)
        dqn_ref[0:1, :] += dgq
        dkn_ref[0:1, :] += dgk

    outs, couts = _call(
        body, name=name, grid=(m // TM,), in_specs=specs,
        out_specs=[pl.BlockSpec((TM, PROJ_W), lambda t: (t, 0)), pl.BlockSpec((8, 512), lambda t: (0, 0)),
                   pl.BlockSpec((8, 128), lambda t: (0, 0))],
        out_shape=[jax.ShapeDtypeStruct((m, PROJ_W), BF16), jax.ShapeDtypeStruct((8, 512), F32),
                   jax.ShapeDtypeStruct((8, 128), F32)],
        args=[proj, dq, dkv, qn, kn, cos, sin, emat], sem=("arbitrary",), comm=comm)
    return outs if comm is None else (outs, couts)


def _attn_case(t, hg, q_ref, kv_ref, c_rows, t_rows, fn):
    wl = TM + 2 * WINDOW
    kvd = lambda a, n: kv_ref[pl.ds(a, n), :]
    dense = hg < 2
    ctx = t == 0

    @pl.when(jnp.logical_and(dense, ctx))
    def _():
        kv = kvd(0, c_rows)
        fn(kv[:, :64], kv[:, 64:], None, False, [(0, c_rows)])

    @pl.when(jnp.logical_and(dense, jnp.logical_not(ctx)))
    def _():
        kv = kvd(0, t_rows)
        fn(kv[:, :64], kv[:, 64:], None, False, [(0, t_rows)])

    @pl.when(jnp.logical_and(jnp.logical_not(dense), ctx))
    def _():
        kv = kvd(0, c_rows)
        fn(kv[:, :64], kv[:, 64:], None, True, [(0, c_rows)])

    @pl.when(jnp.logical_and(jnp.logical_not(dense), jnp.logical_not(ctx)))
    def _():
        start = pl.multiple_of(jnp.minimum(c_rows + (t - 1) * TM - WINDOW, t_rows - wl), 128)
        kv = jnp.concatenate([kvd(0, c_rows), kvd(start, wl)], axis=0)
        rows = lax.broadcasted_iota(jnp.int32, (TM, c_rows + wl), 0)
        cols = lax.broadcasted_iota(jnp.int32, (TM, c_rows + wl), 1)
        qpos = (t - 1) * TM + rows
        kpos = start - 2 * c_rows + cols
        mask = jnp.logical_or(cols < c_rows, jnp.logical_and(jnp.abs(kpos - qpos) <= WINDOW, kpos >= 0))
        fn(kv[:, :64], kv[:, 64:], mask, True, [(0, c_rows), (start, wl)])


def _softmax(q, k, mask, sink):
    s = lax.dot_general(q, k, (((1,), (1,)), ((), ())), preferred_element_type=F32)
    if mask is not None:
        s = jnp.where(mask, s, NEG_BIG)
    mx = jnp.max(s, axis=1, keepdims=True)
    if sink is not None:
        mx = jnp.maximum(mx, sink)
    p = jnp.exp(s - mx)
    l = jnp.sum(p, axis=1, keepdims=True)
    ps = None
    if sink is not None:
        ps = jnp.exp(sink - mx)
        l = l + ps
    inv = 1.0 / l
    return p, inv, (None if ps is None else ps * inv)


def _attn_specs(t_rows):
    q_spec = pl.BlockSpec((None, TM, Q_WIDTH), lambda b, hg, t, s: (b, t, hg))
    kv_spec = pl.BlockSpec((None, t_rows, 128), lambda b, hg, t, s: (b, 0, hg))
    return q_spec, kv_spec


def _attn_fwd(q_all, kv_all, sink8, c_rows, name, comm=None):
    bl, t_rows, _ = q_all.shape
    q_spec, kv_spec = _attn_specs(t_rows)

    def body(sink_ref, q_ref, kv_ref, o_ref, ot_ref):
        hg, t = pl.program_id(1), pl.program_id(2)

        def fn(k, v, mask, use_sink, spans):
            outs = []
            for g in range(GROUP):
                sink = sink_ref[jnp.maximum(hg - 2, 0) * GROUP + g] if use_sink else None
                p, inv, _ = _softmax(q_ref[:, g * 64:(g + 1) * 64], k, mask, sink)
                outs.append(jnp.dot(p.astype(BF16), v, preferred_element_type=F32) * inv)
            o = jnp.concatenate(outs, axis=1).astype(BF16)
            o_ref[...] = o
            ot_ref[...] = o.T

        _attn_case(t, hg, q_ref, kv_ref, c_rows, t_rows, fn)

    nt = t_rows // TM
    ot_spec = pl.BlockSpec((Q_WIDTH, TM), lambda b, hg, t, s: (hg, b * nt + t))
    outs, couts = _call(
        body, name=name, grid=(bl, N_HG, nt), in_specs=[q_spec, kv_spec], out_specs=[q_spec, ot_spec],
        out_shape=[jax.ShapeDtypeStruct(q_all.shape, BF16), jax.ShapeDtypeStruct((N_HG * Q_WIDTH, bl * t_rows), BF16)],
        args=[sink8, q_all, kv_all], prefetch=1, sem=("parallel", "parallel", "arbitrary"), comm=comm)
    return outs if comm is None else (outs, couts)


def _attn_bwd(q_all, kv_all, do, o, sink8, c_rows, name, comm=None):
    bl, t_rows, _ = q_all.shape
    q_spec, kv_spec = _attn_specs(t_rows)
    ds_spec = pl.BlockSpec((None, None, 8, 128), lambda b, hg, t, s: (b, hg, 0, 0))

    def body(sink_ref, q_ref, kv_ref, do_ref, o_ref, dq_ref, dkv_ref, dsk_ref):
        hg, t = pl.program_id(1), pl.program_id(2)

        @pl.when(t == 0)
        def _():
            dkv_ref[...] = jnp.zeros_like(dkv_ref)
            dsk_ref[...] = jnp.zeros_like(dsk_ref)

        def fn(k, v, mask, use_sink, spans):
            dqs, dsinks = [], []
            dk = jnp.zeros(k.shape, F32)
            dv = jnp.zeros(v.shape, F32)
            for g in range(GROUP):
                sink = sink_ref[jnp.maximum(hg - 2, 0) * GROUP + g] if use_sink else None
                q = q_ref[:, g * 64:(g + 1) * 64]
                do_g = do_ref[:, g * 64:(g + 1) * 64]
                p, inv, ps = _softmax(q, k, mask, sink)
                pb = p.astype(BF16)
                dp = lax.dot_general(do_g, v, (((1,), (1,)), ((), ())), preferred_element_type=F32)
                do_f = do_g.astype(F32)
                dd = jnp.sum(do_f * o_ref[:, g * 64:(g + 1) * 64].astype(F32), axis=1, keepdims=True)
                ds = (pb.astype(F32) * (dp - dd)).astype(BF16)
                dqs.append(jnp.dot(ds, k, preferred_element_type=F32) * inv)
                q_s = (q.astype(F32) * inv).astype(BF16)
                do_s = (do_f * inv).astype(BF16)
                dk = dk + lax.dot_general(ds, q_s, (((0,), (0,)), ((), ())), preferred_element_type=F32)
                dv = dv + lax.dot_general(pb, do_s, (((0,), (0,)), ((), ())), preferred_element_type=F32)
                if use_sink:
                    dsinks.append(jnp.broadcast_to(-jnp.sum(ps * dd, axis=0, keepdims=True), (1, 128)))
            dq_ref[...] = jnp.concatenate(dqs, axis=1)
            dkv = jnp.concatenate([dk, dv], axis=1)
            off = 0
            for start, size in spans:
                dkv_ref[pl.ds(start, size), :] += dkv[off:off + size]
                off += size
            if use_sink:
                dsk_ref[0:GROUP, :] += jnp.concatenate(dsinks, axis=0)

        _attn_case(t, hg, q_ref, kv_ref, c_rows, t_rows, fn)

    outs, couts = _call(
        body, name=name, grid=(bl, N_HG, t_rows // TM), in_specs=[q_spec, kv_spec, q_spec, q_spec],
        out_specs=[q_spec, kv_spec, ds_spec],
        out_shape=[jax.ShapeDtypeStruct(q_all.shape, F32), jax.ShapeDtypeStruct(kv_all.shape, F32),
                   jax.ShapeDtypeStruct((bl, N_HG, 8, 128), F32)],
        args=[sink8, q_all, kv_all, do, o], prefetch=1, sem=("parallel", "parallel", "arbitrary"), comm=comm)
    return outs if comm is None else (outs, couts)


def _silu(x):
    return x * jax.nn.sigmoid(x)


def _ada_fwd(c_rows, w_ada, b_cols, name):
    nl, d, w = w_ada.shape
    r = c_rows.shape[0]

    def body(c_ref, w_ref, b_ref, o_ref):
        s = _silu(c_ref[...]).astype(BF16)
        o_ref[...] = jnp.dot(s, w_ref[...].astype(BF16), preferred_element_type=F32) + b_ref[...]

    return pl.pallas_call(
        body, name=name, grid=(nl,),
        in_specs=[pl.BlockSpec((r, d), lambda l: (0, 0)), pl.BlockSpec((None, d, w), lambda l: (l, 0, 0)),
                  pl.BlockSpec((None, 1, w), lambda l: (l, 0, 0))],
        out_specs=pl.BlockSpec((None, r, w), lambda l: (l, 0, 0)),
        out_shape=jax.ShapeDtypeStruct((nl, r, w), F32), compiler_params=_params(("parallel",)),
    )(c_rows, w_ada, b_cols)


def _ada_bwd(c_rows, c_ctx, dmod, w_ada, name):
    nl, d, w = w_ada.shape
    r = c_rows.shape[0]

    def body(c_ref, cc_ref, g_ref, w_ref, dw_ref, dc_ref):
        l = pl.program_id(0)
        s = _silu(c_ref[...]).astype(BF16)
        gm = g_ref[...].astype(BF16)
        dw_ref[...] = lax.dot_general(s, gm, (((0,), (0,)), ((), ())), preferred_element_type=F32)
        ds = lax.dot_general(gm, w_ref[...].astype(BF16), (((1,), (1,)), ((), ())), preferred_element_type=F32)
        rows = lax.broadcasted_iota(jnp.int32, ds.shape, 0)
        dsc = jnp.sum(jnp.where(rows % ADA_ROWS == 2, ds, 0.0), axis=0, keepdims=True)
        x = cc_ref[...]
        sg = jax.nn.sigmoid(x)
        dcc = dsc * (sg * (1.0 + x * (1.0 - sg)))
        out = jnp.concatenate([dcc, jnp.zeros((7, d), F32)], axis=0)

        @pl.when(l == 0)
        def _():
            dc_ref[...] = out

        @pl.when(l != 0)
        def _():
            dc_ref[...] += out

    return pl.pallas_call(
        body, name=name, grid=(nl,),
        in_specs=[pl.BlockSpec((r, d), lambda l: (0, 0)), pl.BlockSpec((1, d), lambda l: (0, 0)),
                  pl.BlockSpec((None, r, w), lambda l: (l, 0, 0)), pl.BlockSpec((None, d, w), lambda l: (l, 0, 0))],
        out_specs=[pl.BlockSpec((None, d, w), lambda l: (l, 0, 0)), pl.BlockSpec((8, d), lambda l: (0, 0))],
        out_shape=[jax.ShapeDtypeStruct((nl, d, w), F32), jax.ShapeDtypeStruct((8, d), F32)],
        compiler_params=_params(("arbitrary",)),
    )(c_rows, c_ctx, dmod, w_ada)


def _adam_math(w, g, m, v):
    m = ADAM_B1 * m + (1.0 - ADAM_B1) * g
    v = ADAM_B2 * v + (1.0 - ADAM_B2) * (g * g)
    m_hat = m / (1.0 - ADAM_B1 ** ADAM_STEP)
    v_hat = v / (1.0 - ADAM_B2 ** ADAM_STEP)
    delta = -ADAM_LR * (m_hat / (jnp.sqrt(v_hat) + ADAM_EPS) + ADAM_WD * w)
    return delta, m, v


def _adamw(w, m, v, g_own, g_recv, name, rows=256):
    nl, r, c = w.shape
    tr = min(rows, r)
    spec = pl.BlockSpec((None, tr, c), lambda l, i: (l, i, 0))
    per_layer = isinstance(g_own, (list, tuple))
    own = list(g_own) if per_layer else [g_own]
    recv = [] if g_recv is None else list(g_recv)
    in_specs = [spec] * 3 + [pl.BlockSpec((tr, c), lambda l, i: (i, 0)) if per_layer else spec] * len(own)
    in_specs += [pl.BlockSpec((N_DEV, tr, c), lambda l, i: (0, i, 0))] * len(recv)

    def body(*refs):
        w_ref, m_ref, v_ref = refs[:3]
        own_refs, recv_refs = refs[3:3 + len(own)], refs[3 + len(own):3 + len(own) + len(recv)]
        go_ref, d_ref, mo_ref, vo_ref = refs[-4:]

        def update(li):
            g = own_refs[li][...].astype(F32)
            if recv:
                for k in range(1, N_DEV):
                    g = g + recv_refs[li][k].astype(F32)
            delta, m_, v_ = _adam_math(w_ref[...], g, m_ref[...], v_ref[...])
            go_ref[...] = g
            d_ref[...] = delta
            mo_ref[...] = m_
            vo_ref[...] = v_

        if per_layer:
            for li in range(nl):
                pl.when(pl.program_id(0) == li)(functools.partial(update, li))
        else:
            update(0)

    return pl.pallas_call(
        body, name=name, grid=(nl, r // tr), in_specs=in_specs, out_specs=[spec] * 4,
        out_shape=[jax.ShapeDtypeStruct(w.shape, F32)] * 4, compiler_params=_params(("parallel", "parallel")),
    )(w, m, v, *own, *recv)


def _small_adamw(w, m, v, g_all, name):
    def body(w_ref, m_ref, v_ref, g_ref, go_ref, d_ref, mo_ref, vo_ref):
        g = g_ref[0]
        for k in range(1, N_DEV):
            g = g + g_ref[k]
        delta, m_, v_ = _adam_math(w_ref[...], g, m_ref[...], v_ref[...])
        go_ref[...] = g
        d_ref[...] = delta
        mo_ref[...] = m_
        vo_ref[...] = v_

    return pl.pallas_call(
        body, name=name, out_shape=[jax.ShapeDtypeStruct(w.shape, F32)] * 4, compiler_params=_params(),
    )(w, m, v, g_all)


SMALL = ("c_ctx", "b_ada", "g_pre_mix", "g_post_mix", "g_pre_mlp", "g_post_mlp", "q_norm", "k_norm", "sink")


def _pack_small(parts):
    flat = jnp.concatenate([parts[n].reshape(-1) for n in SMALL])
    rows = -(-flat.shape[0] // 1024) * 8
    return jnp.pad(flat, (0, rows * 128 - flat.shape[0])).reshape(rows, 128)


def _unpack_small(packed, like):
    flat = packed.reshape(-1)
    out, off = {}, 0
    for n in SMALL:
        size = int(np.prod(like[n].shape))
        out[n] = flat[off:off + size].reshape(like[n].shape)
        off += size
    return out


def kernel(x, c, ctx, c_ctx, w_ada, b_ada, g_pre_mix, g_post_mix, g_pre_mlp, g_post_mlp, w_in, q_norm, k_norm, sink, w_out, w_up, w_down, loss_target, m_c_ctx, m_w_ada, m_b_ada, m_g_pre_mix, m_g_post_mix, m_g_pre_mlp, m_g_post_mlp, m_w_in, m_q_norm, m_k_norm, m_sink, m_w_out, m_w_up, m_w_down, v_c_ctx, v_w_ada, v_b_ada, v_g_pre_mix, v_g_post_mix, v_g_pre_mlp, v_g_post_mlp, v_w_in, v_q_norm, v_k_norm, v_sink, v_w_out, v_w_up, v_w_down):
    bl, s_rows, d = x.shape
    c_rows = ctx.shape[1]
    assert c_rows == TM and s_rows % TM == 0 and bl == 2
    t_rows = c_rows + s_rows
    nt = t_rows // TM
    m_rows = bl * t_rows
    nl = w_in.shape[0]
    ada_w = w_ada.shape[2]
    d_ff = w_up.shape[2] * N_DEV
    me = _my_index()

    shard = lambda w_, l: w_[l].astype(BF16)
    c_pad = jnp.concatenate([c, c_ctx[None, :], jnp.zeros((ADA_ROWS - bl - 1, d), F32)], axis=0)
    c_all, w_in_g, w_out_g = _comm_only(
        _Comm([(c_pad, GATHER), (shard(w_in, 0), GATHER), (shard(w_out, 0), GATHER)]), "gather_first")
    c_all = c_all.reshape(N_DEV * ADA_ROWS, d)
    gathered = {0: dict(w_in=w_in_g, w_out=w_out_g)}

    def layer_weights(l):
        g_ = gathered[l]
        w_out_f = g_["w_out"].reshape(-1, d)
        w_down_f = g_["w_down"].reshape(d_ff, d)
        return dict(
            w_in_t=g_["w_in"].transpose(0, 2, 1).reshape(PROJ_W, d), w_out_f=w_out_f, w_out_t=w_out_f.T,
            w_up_s=g_["w_up"], w_up_t=g_["w_up"].transpose(0, 2, 1).reshape(d_ff, d),
            w_down_f=w_down_f, w_down_t=w_down_f.T)

    big = dict(tm=1536, tn=512)
    deep = dict(tm=1536, tn=256, tk=d_ff)
    wide = dict(tm=1024, tn=256, tk=m_rows)

    b_cols = lax.dynamic_slice(b_ada, (0, me * ada_w), (nl, ada_w))[:, None, :]
    mod_cols = _ada_fwd(c_all, w_ada, b_cols, "ada_fwd")
    mod_slots = mod_cols.reshape(nl, N_DEV, ADA_ROWS, ada_w).transpose(1, 0, 2, 3)
    mod_g, = _comm_only(_Comm([(mod_slots, TO_OWNER)]), "exchange_mod")
    mine = mod_g.transpose(1, 2, 0, 3).reshape(nl, ADA_ROWS, N_MOD, d)
    pad = jnp.zeros((bl, 2, MOD_ROWS - N_MOD, d), F32)
    modsel = [jnp.concatenate([jnp.stack([jnp.broadcast_to(mine[l, bl], (bl, N_MOD, d)), mine[l, :bl]], axis=1), pad],
                              axis=2) for l in range(nl)]

    cos, sin = _rope_tables(t_rows, c_rows)
    emat = _head_mean_matrix()
    row = lambda a: a[None, :]
    qn = [jnp.tile(q_norm[l], 8)[None, :] for l in range(nl)]
    kn = [jnp.tile(k_norm[l], 2)[None, :] for l in range(nl)]

    h = jnp.concatenate([ctx, x], axis=1).reshape(m_rows, d)
    saved = []
    weights_of = {}
    for l in range(nl):
        u, u_t = _norm_mod_fwd(h, row(g_pre_mix[l]), modsel[l], 0, 1, nt, f"mix_mod_fwd{l}")
        w_in_f = gathered[l]["w_in"].transpose(1, 0, 2).reshape(d, PROJ_W)
        proj = _mm(u, w_in_f, name=f"mm_in{l}", tk=d, **big)
        q_all, kv_all = _prep_fwd(proj, qn[l], kn[l], cos, sin, emat, nt, f"prep_fwd{l}")
        (o, o_t), (w_up_g, w_down_g) = _attn_fwd(
            q_all.reshape(bl, t_rows, 1024), kv_all.reshape(bl, t_rows, 512), sink[l], c_rows, f"attn_fwd{l}",
            comm=_Comm([(shard(w_up, l), GATHER), (shard(w_down, l), GATHER)]))
        o = o.reshape(m_rows, 1024)
        gathered[l].update(w_up=w_up_g, w_down=w_down_g)
        wl = weights_of[l] = layer_weights(l)
        mix = _mm(o, wl["w_out_f"], name=f"mm_out{l}", tk=1024, **big)
        h_mid = _gate_fwd(h, mix, row(g_post_mix[l]), modsel[l], 2, nt, f"mix_gate_fwd{l}")
        v_in, v_t = _norm_mod_fwd(h_mid, row(g_pre_mlp[l]), modsel[l], 3, 4, nt, f"mlp_mod_fwd{l}")
        more = l + 1 < nl
        res_up = _mm(v_in, wl["w_up_s"], name=f"mm_up{l}", b_mode="nn_slots", epilogue="relu2", tk=d,
                     comm=_Comm([(shard(w_in, l + 1), GATHER)]) if more else None, **big)
        (r_act, s_act, r_t), nxt_in = res_up if more else (res_up, None)
        res_down = _mm(r_act, wl["w_down_f"], name=f"mm_down{l}",
                       comm=_Comm([(shard(w_out, l + 1), GATHER)]) if more else None, **deep)
        y, nxt_out = res_down if more else (res_down, None)
        if more:
            gathered[l + 1] = dict(w_in=nxt_in[0], w_out=nxt_out[0])
        h_out = _gate_fwd(h_mid, y, row(g_post_mlp[l]), modsel[l], 5, nt, f"mlp_gate_fwd{l}")
        saved.append((h, u_t, proj, q_all, kv_all, o, o_t, mix, h_mid, v_t, r_t, s_act, y))
        h = h_out

    dh, ss = _loss_grad(h, loss_target.reshape(bl * s_rows, d), c_rows, nt, "loss_grad")
    loss = lax.psum(0.5 * ss[0, 0] / d, MESH_AXES)

    small_g = {n: [None] * nl for n in SMALL if n not in ("c_ctx", "b_ada")}
    dmod_rows = []
    slots = {n: [None] * nl for n in ("w_in", "w_out", "w_up", "w_down")}
    recvd = {n: [None] * nl for n in slots}
    send = lambda n, l_: (slots[n][l_], TO_OWNER_XOR)
    for l in reversed(range(nl)):
        h_in, u_t, proj, q_all, kv_all, o, o_t, mix, h_mid, v_t, r_t, s_act, y = saved[l]
        wl = weights_of[l]
        later = l + 1 < nl
        dy, dvec_g2 = _gate_bwd(y, row(g_post_mlp[l]), modsel[l], dh, 5, nt, f"mlp_gate_bwd{l}")
        res = _mm(dy, wl["w_down_t"], name=f"mm_da{l}", epilogue="mul", extra=s_act, out_dtype=BF16, tk=d,
                  comm=_Comm([send("w_out", l + 1)]) if later else None, **big)
        da = res[0] if later else res
        if later:
            recvd["w_out"][l + 1] = res[1][0]
        res = _mm(r_t, dy, name=f"mm_dw_down{l}", out_dtype=BF16,
                  comm=_Comm([send("w_in", l + 1)]) if later else None, **wide)
        dw_down = res[0] if later else res
        if later:
            recvd["w_in"][l + 1] = res[1][0]
        dw_up = _mm(v_t, da, name=f"mm_dw_up{l}", out_mode="slots", out_dtype=BF16, **wide)
        slots["w_down"][l] = dw_down.reshape(N_DEV, -1, d)
        slots["w_up"][l] = dw_up
        dv = _mm(da, wl["w_up_t"], name=f"mm_dv{l}", **deep)
        dh, dvec_m2 = _norm_mod_bwd(h_mid, row(g_pre_mlp[l]), modsel[l], dv, dh, 3, 4, nt, f"mlp_mod_bwd{l}")
        dmix, dvec_g1 = _gate_bwd(mix, row(g_post_mix[l]), modsel[l], dh, 2, nt, f"mix_gate_bwd{l}")
        do = _mm(dmix, wl["w_out_t"], name=f"mm_do{l}", out_dtype=BF16, tk=d, **big)
        dw_out = _mm(o_t, dmix, name=f"mm_dw_out{l}", out_dtype=BF16, **wide)
        slots["w_out"][l] = dw_out.reshape(N_DEV, -1, d)
        (dq, dkv, dsk), (recvd["w_down"][l], recvd["w_up"][l]) = _attn_bwd(
            q_all.reshape(bl, t_rows, 1024), kv_all.reshape(bl, t_rows, 512), do.reshape(bl, t_rows, 1024),
            o.reshape(bl, t_rows, 1024), sink[l], c_rows, f"attn_bwd{l}",
            comm=_Comm([send("w_down", l), send("w_up", l)]))
        last = l == 0
        res = _prep_bwd(proj, dq.reshape(m_rows, 1024), dkv.reshape(m_rows, 512), qn[l], kn[l], cos, sin, emat, nt,
                        f"prep_bwd{l}", comm=_Comm([send("w_out", l)]) if last else None)
        dproj, dqn, dkn = res[0] if last else res
        if last:
            recvd["w_out"][l] = res[1][0]
        dw_in = _mm(u_t, dproj, name=f"mm_dw_in{l}", out_dtype=BF16, **wide)
        slots["w_in"][l] = dw_in.reshape(d, N_DEV, PROJ_W // N_DEV).transpose(1, 0, 2)
        res = _mm(dproj, wl["w_in_t"], name=f"mm_du{l}", tk=PROJ_W,
                  comm=_Comm([send("w_in", l)]) if last else None, **big)
        du = res[0] if last else res
        if last:
            recvd["w_in"][l] = res[1][0]
        dh, dvec_m1 = _norm_mod_bwd(h_in, row(g_pre_mix[l]), modsel[l], du, dh, 0, 1, nt, f"mix_mod_bwd{l}")

        small_g["g_pre_mix"][l] = jnp.sum(dvec_m1[:, :, 2], axis=(0, 1))
        small_g["g_post_mix"][l] = jnp.sum(dvec_g1[:, :, 1], axis=(0, 1))
        small_g["g_pre_mlp"][l] = jnp.sum(dvec_m2[:, :, 2], axis=(0, 1))
        small_g["g_post_mlp"][l] = jnp.sum(dvec_g2[:, :, 1], axis=(0, 1))
        small_g["q_norm"][l] = jnp.sum(dqn[0].reshape(8, HEAD_DIM), axis=0)
        small_g["k_norm"][l] = jnp.sum(dkn[0].reshape(2, HEAD_DIM), axis=0)
        small_g["sink"][l] = jnp.sum(dsk[:, 2:, :GROUP, 0], axis=0).reshape(-1)
        dms = jnp.stack([dvec_m1[:, :, 0], dvec_m1[:, :, 1], dvec_g1[:, :, 0],
                         dvec_m2[:, :, 0], dvec_m2[:, :, 1], dvec_g2[:, :, 0]], axis=2)
        rows = jnp.concatenate([dms[:, 1], jnp.sum(dms[:, 0], axis=0)[None]], axis=0)
        dmod_rows.append(jnp.pad(rows.reshape(bl + 1, N_MOD * d), ((0, ADA_ROWS - bl - 1), (0, 0))))
    dmod_rows = dmod_rows[::-1]
    grad_x = dh.reshape(bl, t_rows, d)[:, c_rows:]

    dmod_slots = jnp.stack(dmod_rows).reshape(nl, ADA_ROWS, N_DEV, ada_w).transpose(2, 0, 1, 3)
    dmod_g, = _comm_only(_Comm([(dmod_slots, TO_OWNER)]), "exchange_dmod")
    dmod_mine = dmod_g.transpose(1, 0, 2, 3).reshape(nl, N_DEV * ADA_ROWS, ada_w)
    dw_ada, dcc = _ada_bwd(c_all, c_ctx[None, :], dmod_mine, w_ada, "ada_bwd")

    parts = {n: jnp.stack(small_g[n]) for n in small_g}
    parts["c_ctx"] = dcc[0]
    parts["b_ada"] = jnp.stack([jnp.sum(r_[: bl + 1], axis=0) for r_ in dmod_rows])
    weights = dict(c_ctx=c_ctx, b_ada=b_ada, g_pre_mix=g_pre_mix, g_post_mix=g_post_mix, g_pre_mlp=g_pre_mlp,
                   g_post_mlp=g_post_mlp, q_norm=q_norm, k_norm=k_norm, sink=sink)
    moms = dict(c_ctx=m_c_ctx, b_ada=m_b_ada, g_pre_mix=m_g_pre_mix, g_post_mix=m_g_post_mix, g_pre_mlp=m_g_pre_mlp,
                g_post_mlp=m_g_post_mlp, q_norm=m_q_norm, k_norm=m_k_norm, sink=m_sink)
    vels = dict(c_ctx=v_c_ctx, b_ada=v_b_ada, g_pre_mix=v_g_pre_mix, g_post_mix=v_g_post_mix, g_pre_mlp=v_g_pre_mlp,
                g_post_mlp=v_g_post_mlp, q_norm=v_q_norm, k_norm=v_k_norm, sink=v_sink)
    small_all, = _comm_only(_Comm([(_pack_small(parts), GATHER)]), "gather_small")
    s_out = _small_adamw(_pack_small(weights), _pack_small(moms), _pack_small(vels), small_all, "adamw_small")
    s_g, s_d, s_m, s_v = [_unpack_small(a, weights) for a in s_out]

    res = {}
    for n, w_, m_, v_ in (("w_in", w_in, m_w_in, v_w_in), ("w_out", w_out, m_w_out, v_w_out),
                          ("w_up", w_up, m_w_up, v_w_up), ("w_down", w_down, m_w_down, v_w_down)):
        own = [lax.dynamic_index_in_dim(slots[n][l], me, axis=0, keepdims=False) for l in range(nl)]
        res[n] = _adamw(w_, m_, v_, own, recvd[n], f"adamw_{n}")
    res["w_ada"] = _adamw(w_ada, m_w_ada, v_w_ada, dw_ada, None, "adamw_w_ada")

    order = ("c_ctx", "w_ada", "b_ada", "g_pre_mix", "g_post_mix", "g_pre_mlp", "g_post_mlp", "w_in", "q_norm",
             "k_norm", "sink", "w_out", "w_up", "w_down")
    outs = [loss, grad_x]
    for i, small in enumerate((s_g, s_d, s_m, s_v)):
        outs += [small[n] if n in small else res[n][i] for n in order]
    return tuple(outs)
```

```python
import functools

import jax
import jax.numpy as jnp
import numpy as np
from jax import lax
from jax.experimental import pallas as pl
from jax.experimental.pallas import tpu as pltpu

F32 = jnp.float32
BF16 = jnp.bfloat16

HEAD_DIM = 64
GROUP = 4
N_HG = 4
Q_WIDTH = GROUP * HEAD_DIM
WINDOW = 128
GRID_W = 64
ROPE_THETA = 10000.0
EPS = 1e-6
NEG_BIG = -1e30
N_MOD = 6
MOD_ROWS = 8
TM = 256
N_DEV = 8
ADA_ROWS = 8
VMEM_LIMIT = 56 * 1024 * 1024

ADAM_LR = 0.001
ADAM_B1 = 0.9
ADAM_B2 = 0.999
ADAM_EPS = 1e-08
ADAM_WD = 0.01
ADAM_STEP = 10

MESH_AXES = ("x", "y", "c")


def _params(sem=None):
    kw = dict(vmem_limit_bytes=VMEM_LIMIT)
    if sem is not None:
        kw["dimension_semantics"] = sem
    return pltpu.CompilerParams(**kw)


def _my_index():
    return 4 * lax.axis_index("x") + 2 * lax.axis_index("y") + lax.axis_index("c")


def _peer(k):
    x, y, c = lax.axis_index("x"), lax.axis_index("y"), lax.axis_index("c")
    kx, ky, kc = (k >> 2) & 1, (k >> 1) & 1, k & 1
    px = (1 - x) if kx else x
    py = (1 - y) if ky else y
    pc = (1 - c) if kc else c
    return (px, py, pc), 4 * px + 2 * py + pc


GATHER, TO_OWNER, TO_OWNER_XOR = "gather", "to_owner", "to_owner_xor"


class _Comm:
    def __init__(self, items):
        self.items = list(items)
        self.arrays = [a for a, _ in self.items]

    def out_shapes(self):
        return [jax.ShapeDtypeStruct(((N_DEV,) + a.shape) if kind == GATHER else a.shape, a.dtype)
                for a, kind in self.items]

    def sem_shapes(self):
        n = len(self.items) * N_DEV
        return [pltpu.SemaphoreType.DMA((n,)), pltpu.SemaphoreType.DMA((n,))]

    def _copies(self, in_refs, out_refs, send_sems, recv_sems):
        me = _my_index()
        local, remote = [], []
        for i, ((_, kind), x_ref, o_ref) in enumerate(zip(self.items, in_refs, out_refs)):
            base = i * N_DEV
            own_src = x_ref if kind == GATHER else x_ref.at[me]
            own_dst = o_ref.at[0] if kind == TO_OWNER_XOR else o_ref.at[me]
            local.append(pltpu.make_async_copy(own_src, own_dst, send_sems.at[base]))
            for k in range(1, N_DEV):
                peer, pidx = _peer(k)
                remote.append(pltpu.make_async_remote_copy(
                    src_ref=x_ref if kind == GATHER else x_ref.at[pidx],
                    dst_ref=o_ref.at[k] if kind == TO_OWNER_XOR else o_ref.at[me],
                    send_sem=send_sems.at[base + k], recv_sem=recv_sems.at[base + k],
                    device_id=peer, device_id_type=pl.DeviceIdType.MESH))
        return local, remote

    def start(self, in_refs, out_refs, send_sems, recv_sems):
        local, remote = self._copies(in_refs, out_refs, send_sems, recv_sems)
        for cp in local + remote:
            cp.start()

    def wait(self, in_refs, out_refs, send_sems, recv_sems):
        local, remote = self._copies(in_refs, out_refs, send_sems, recv_sems)
        for cp in remote:
            cp.wait_recv()
        for cp in remote:
            cp.wait_send()
        for cp in local:
            cp.wait()


def _call(body, *, name, grid, in_specs, out_specs, out_shape, args, scratch_shapes=(), prefetch=0, sem=None,
          comm=None):
    single = not isinstance(out_shape, (list, tuple))
    out_shape = [out_shape] if single else list(out_shape)
    out_specs = [out_specs] if single else list(out_specs)
    in_specs, scratch_shapes, args = list(in_specs), list(scratch_shapes), list(args)
    n_in, n_out = len(in_specs), len(out_shape)
    if comm is not None:
        nc = len(comm.arrays)
        hbm = pl.BlockSpec(memory_space=pl.ANY)
        inner = body

        def body(*refs):
            pre, r = refs[:prefetch], refs[prefetch:]
            ins, cin = r[:n_in], r[n_in:n_in + nc]
            outs, cout = r[n_in + nc:n_in + nc + n_out], r[n_in + nc + n_out:n_in + 2 * nc + n_out]
            scr, sems = r[n_in + 2 * nc + n_out:len(r) - 2], r[len(r) - 2:]
            ids = [pl.program_id(i) for i in range(len(grid))]

            def when(flags, fn):
                if flags:
                    pl.when(functools.reduce(jnp.logical_and, flags))(fn)
                else:
                    fn()

            when([i == 0 for i in ids], lambda: comm.start(cin, cout, *sems))
            inner(*pre, *ins, *outs, *scr)
            when([i == n - 1 for i, n in zip(ids, grid)], lambda: comm.wait(cin, cout, *sems))

        in_specs += [hbm] * nc
        out_specs += [hbm] * nc
        out_shape += comm.out_shapes()
        scratch_shapes += comm.sem_shapes()
        args += comm.arrays
        sem = ("arbitrary",) * len(grid)
    kw = dict(name=name, out_shape=out_shape, compiler_params=_params(sem if grid else None))
    if prefetch:
        kw["grid_spec"] = pltpu.PrefetchScalarGridSpec(
            num_scalar_prefetch=prefetch, grid=grid, in_specs=in_specs, out_specs=out_specs,
            scratch_shapes=scratch_shapes)
    else:
        kw.update(in_specs=in_specs, out_specs=out_specs, scratch_shapes=scratch_shapes)
        if grid:
            kw["grid"] = grid
    res = list(pl.pallas_call(body, **kw)(*args))
    outs = res[:n_out]
    return (outs[0] if single else outs), res[n_out:]


def _comm_only(comm, name):
    return _call(lambda: None, name=name, grid=(), in_specs=[], out_specs=[], out_shape=[], args=[], comm=comm)[1]


def _mm(a, b, *, name, ta=False, b_mode="nn", out_mode="plain", out_dtype=F32, tm=512, tn=512, tk=512,
        epilogue=None, extra=None, comm=None):
    if ta:
        kdim, m = a.shape
    else:
        m, kdim = a.shape
    if b_mode == "nn":
        n = b.shape[1]
    elif b_mode == "nt":
        n = b.shape[0]
    elif b_mode == "nn_slots":
        n = b.shape[0] * b.shape[2]
        tn = b.shape[2]
    else:
        n = b.shape[1]
        tk = b.shape[2]
    if out_mode == "slots":
        tn = n // N_DEV
    tm, tn, tk = min(tm, m), min(tn, n), min(tk, kdim)
    assert m % tm == 0 and n % tn == 0 and kdim % tk == 0, (name, m, n, kdim, tm, tn, tk)
    nk = kdim // tk

    a_spec = pl.BlockSpec((tk, tm), lambda i, j, k: (k, i)) if ta else pl.BlockSpec((tm, tk), lambda i, j, k: (i, k))
    if b_mode == "nn":
        b_spec = pl.BlockSpec((tk, tn), lambda i, j, k: (k, j))
    elif b_mode == "nt":
        b_spec = pl.BlockSpec((tn, tk), lambda i, j, k: (j, k))
    elif b_mode == "nn_slots":
        b_spec = pl.BlockSpec((None, tk, tn), lambda i, j, k: (j, k, 0))
    else:
        b_spec = pl.BlockSpec((None, tn, tk), lambda i, j, k: (k, j, 0))
    tb = b_mode in ("nt", "nt_slots")
    if out_mode == "plain":
        o_shape, o_spec = (m, n), pl.BlockSpec((tm, tn), lambda i, j, k: (i, j))
    else:
        o_shape, o_spec = (N_DEV, m, tn), pl.BlockSpec((None, tm, tn), lambda i, j, k: (j, i, 0))
    dims = (((0 if ta else 1,), (1 if tb else 0,)), ((), ()))

    in_specs = [a_spec, b_spec]
    args = [a, b]
    if epilogue == "mul":
        in_specs.append(pl.BlockSpec((tm, tn), lambda i, j, k: (i, j)))
        args.append(extra)
    if epilogue == "relu2":
        out_shape = [jax.ShapeDtypeStruct(o_shape, BF16)] * 2 + [jax.ShapeDtypeStruct((n, m), BF16)]
        out_specs = [o_spec, o_spec, pl.BlockSpec((tn, tm), lambda i, j, k: (j, i))]
    else:
        out_shape = jax.ShapeDtypeStruct(o_shape, out_dtype)
        out_specs = o_spec

    def finish(refs, acc):
        if epilogue == "relu2":
            r = jnp.maximum(acc, 0.0)
            r2 = (r * r).astype(BF16)
            refs[2][...] = r2
            refs[3][...] = (2.0 * r).astype(BF16)
            refs[4][...] = r2.T
        elif epilogue == "mul":
            refs[3][...] = (acc * refs[2][...].astype(F32)).astype(out_dtype)
        else:
            refs[2][...] = acc.astype(out_dtype)

    def body(*refs):
        part = lax.dot_general(refs[0][...], refs[1][...], dims, preferred_element_type=F32)
        if nk == 1:
            finish(refs, part)
            return
        acc_ref = refs[-1]
        k = pl.program_id(2)

        @pl.when(k == 0)
        def _():
            acc_ref[...] = part

        @pl.when(jnp.logical_and(k > 0, k < nk - 1))
        def _():
            acc_ref[...] += part

        @pl.when(k == nk - 1)
        def _():
            finish(refs, acc_ref[...] + part)

    outs, couts = _call(
        body, name=name, grid=(m // tm, n // tn, nk), in_specs=in_specs, out_specs=out_specs, out_shape=out_shape,
        args=args, scratch_shapes=[] if nk == 1 else [pltpu.VMEM((tm, tn), F32)],
        sem=("parallel", "parallel", "arbitrary"), comm=comm)
    return outs if comm is None else (outs, couts)


def _row_specs(d, nt):
    row = pl.BlockSpec((TM, d), lambda t: (t, 0))
    vec = pl.BlockSpec((1, d), lambda t: (0, 0))
    mod = pl.BlockSpec((None, None, MOD_ROWS, d), lambda t: (t // nt, jnp.minimum(t % nt, 1), 0, 0))
    return row, vec, mod


def _rms(x):
    r = lax.rsqrt(jnp.mean(x * x, axis=1, keepdims=True) + EPS)
    return r, x * r


def _norm_mod_fwd(h, g, modsel, i_sh, i_sc, nt, name):
    m, d = h.shape
    row, vec, mod = _row_specs(d, nt)

    def body(h_ref, g_ref, ms_ref, u_ref, ut_ref):
        _, xh = _rms(h_ref[...])
        ms = ms_ref[...]
        u = (xh * g_ref[...] * (1.0 + ms[i_sc:i_sc + 1]) + ms[i_sh:i_sh + 1]).astype(BF16)
        u_ref[...] = u
        ut_ref[...] = u.T

    return pl.pallas_call(
        body, name=name, grid=(m // TM,), in_specs=[row, vec, mod],
        out_specs=[row, pl.BlockSpec((d, TM), lambda t: (0, t))],
        out_shape=[jax.ShapeDtypeStruct((m, d), BF16), jax.ShapeDtypeStruct((d, m), BF16)],
        compiler_params=_params(("parallel",)),
    )(h, g, modsel)


def _acc_rows(t, nt, dvec_ref, rows):
    first = (t % nt) <= 1

    @pl.when(first)
    def _():
        dvec_ref[...] = rows

    @pl.when(jnp.logical_not(first))
    def _():
        dvec_ref[...] += rows


def _norm_mod_bwd(h, g, modsel, du, dh_in, i_sh, i_sc, nt, name, comm=None):
    m, d = h.shape
    row, vec, mod = _row_specs(d, nt)

    def body(h_ref, g_ref, ms_ref, du_ref, dhi_ref, dh_ref, dvec_ref):
        t = pl.program_id(0)
        r, xh = _rms(h_ref[...])
        g_ = g_ref[...]
        ms = ms_ref[...]
        du_ = du_ref[...]
        y = xh * g_
        dy = du_ * (1.0 + ms[i_sc:i_sc + 1])
        dxh = dy * g_
        dx = r * (dxh - xh * jnp.mean(dxh * xh, axis=1, keepdims=True))
        dh_ref[...] = dhi_ref[...] + dx
        rows = jnp.concatenate([
            jnp.sum(du_, axis=0, keepdims=True), jnp.sum(du_ * y, axis=0, keepdims=True),
            jnp.sum(dy * xh, axis=0, keepdims=True), jnp.zeros((MOD_ROWS - 3, d), F32)], axis=0)
        _acc_rows(t, nt, dvec_ref, rows)

    outs, couts = _call(
        body, name=name, grid=(m // TM,), in_specs=[row, vec, mod, row, row], out_specs=[row, mod],
        out_shape=[jax.ShapeDtypeStruct((m, d), F32), jax.ShapeDtypeStruct(modsel.shape, F32)],
        args=[h, g, modsel, du, dh_in], sem=("arbitrary",), comm=comm)
    return outs if comm is None else (outs, couts)


def _gate_fwd(h, z, g, modsel, i_g, nt, name):
    m, d = h.shape
    row, vec, mod = _row_specs(d, nt)

    def body(h_ref, z_ref, g_ref, ms_ref, o_ref):
        _, xh = _rms(z_ref[...])
        ms = ms_ref[...]
        o_ref[...] = h_ref[...] + ms[i_g:i_g + 1] * (xh * g_ref[...])

    return pl.pallas_call(
        body, name=name, grid=(m // TM,), in_specs=[row, row, vec, mod], out_specs=row,
        out_shape=jax.ShapeDtypeStruct((m, d), F32), compiler_params=_params(("parallel",)),
    )(h, z, g, modsel)


def _gate_bwd(z, g, modsel, dh, i_g, nt, name):
    m, d = z.shape
    row, vec, mod = _row_specs(d, nt)

    def body(z_ref, g_ref, ms_ref, dh_ref, dz_ref, dvec_ref):
        t = pl.program_id(0)
        r, xh = _rms(z_ref[...])
        g_ = g_ref[...]
        ms = ms_ref[...]
        dh_ = dh_ref[...]
        dy = dh_ * ms[i_g:i_g + 1]
        dxh = dy * g_
        dz_ref[...] = (r * (dxh - xh * jnp.mean(dxh * xh, axis=1, keepdims=True))).astype(BF16)
        rows = jnp.concatenate([
            jnp.sum(dh_ * (xh * g_), axis=0, keepdims=True), jnp.sum(dy * xh, axis=0, keepdims=True),
            jnp.zeros((MOD_ROWS - 2, d), F32)], axis=0)
        _acc_rows(t, nt, dvec_ref, rows)

    return pl.pallas_call(
        body, name=name, grid=(m // TM,), in_specs=[row, vec, mod, row], out_specs=[row, mod],
        out_shape=[jax.ShapeDtypeStruct((m, d), BF16), jax.ShapeDtypeStruct(modsel.shape, F32)],
        compiler_params=_params(("arbitrary",)),
    )(z, g, modsel, dh)


def _loss_grad(h, target, c_rows, nt, name):
    m, d = h.shape
    row = pl.BlockSpec((TM, d), lambda t: (t, 0))
    ntl = nt - 1
    tgt = pl.BlockSpec((TM, d), lambda t: ((t // nt) * ntl + jnp.maximum(t % nt, 1) - 1, 0))
    acc = pl.BlockSpec((8, 128), lambda t: (0, 0))

    def body(h_ref, t_ref, dh_ref, ss_ref):
        t = pl.program_id(0)

        @pl.when(t == 0)
        def _():
            ss_ref[...] = jnp.zeros_like(ss_ref)

        @pl.when(t % nt == 0)
        def _():
            dh_ref[...] = jnp.zeros_like(dh_ref)

        @pl.when(t % nt != 0)
        def _():
            e = h_ref[...] - t_ref[...]
            dh_ref[...] = e * (1.0 / d)
            ss_ref[...] += jnp.sum(e * e)

    return pl.pallas_call(
        body, name=name, grid=(m // TM,), in_specs=[row, tgt], out_specs=[row, acc],
        out_shape=[jax.ShapeDtypeStruct((m, d), F32), jax.ShapeDtypeStruct((8, 128), F32)],
        compiler_params=_params(("arbitrary",)),
    )(h, target)


QA, KA, VA, QB, KB, VB = 0, 512, 640, 768, 1280, 1408
PROJ_W = 1536
Q_SCALE = HEAD_DIM ** -0.5


def _swap16(x):
    lane = lax.broadcasted_iota(jnp.int32, x.shape, 1)
    n = x.shape[1]
    return jnp.where((lane % 32) < 16, pltpu.roll(x, n - 16, 1), pltpu.roll(x, 16, 1))


def _seg_mean(x, e):
    return jnp.dot(x, e, preferred_element_type=F32, precision=lax.Precision.HIGHEST)


def _rope_tables(t_rows, c_rows):
    s = t_rows - c_rows
    row_ids = jnp.repeat(jnp.arange(s // GRID_W, dtype=jnp.int32), GRID_W).astype(F32)
    col_ids = jnp.tile(jnp.arange(GRID_W, dtype=jnp.int32), s // GRID_W).astype(F32)
    axis_dim = HEAD_DIM // 2
    inv = ROPE_THETA ** (-jnp.arange(0, axis_dim, 2, dtype=F32) / axis_dim)
    ang_r = row_ids[:, None] * inv[None, :]
    ang_c = col_ids[:, None] * inv[None, :]
    cos = jnp.concatenate([jnp.cos(ang_r), jnp.cos(ang_r), jnp.cos(ang_c), jnp.cos(ang_c)], axis=1)
    sin = jnp.concatenate([-jnp.sin(ang_r), jnp.sin(ang_r), -jnp.sin(ang_c), jnp.sin(ang_c)], axis=1)
    cos = jnp.concatenate([jnp.ones((c_rows, HEAD_DIM), F32), cos], axis=0)
    sin = jnp.concatenate([jnp.zeros((c_rows, HEAD_DIM), F32), sin], axis=0)
    return jnp.tile(cos, (1, 8)), jnp.tile(sin, (1, 8))


def _head_mean_matrix():
    i = np.arange(512)
    return jnp.asarray((i[:, None] // HEAD_DIM == i[None, :] // HEAD_DIM).astype(np.float32) / HEAD_DIM)


def _interleave_kv(k, v):
    return jnp.concatenate([k[:, :64], v[:, :64], k[:, 64:], v[:, 64:]], axis=1)


def _prep_fwd(proj, qn, kn, cos, sin, emat, nt, name):
    m = proj.shape[0]
    specs = [
        pl.BlockSpec((TM, PROJ_W), lambda t: (t, 0)),
        pl.BlockSpec((1, 512), lambda t: (0, 0)), pl.BlockSpec((1, 128), lambda t: (0, 0)),
        pl.BlockSpec((TM, 512), lambda t: (t % nt, 0)), pl.BlockSpec((TM, 512), lambda t: (t % nt, 0)),
        pl.BlockSpec((512, 512), lambda t: (0, 0)),
    ]

    def body(p_ref, qn_ref, kn_ref, cos_ref, sin_ref, e_ref, q_ref, kv_ref):
        cos_, sin_, e = cos_ref[...], sin_ref[...], e_ref[...]

        def rope(x, w):
            return x * cos_[:, :w] + _swap16(x) * sin_[:, :w]

        def norm(x, g, w):
            return x * lax.rsqrt(_seg_mean(x * x, e[:w, :w]) + EPS) * g

        qa = rope(norm(p_ref[:, QA:QA + 512], qn_ref[...], 512), 512)
        qb = rope(p_ref[:, QB:QB + 512], 512)
        q_ref[:, 0:512] = (qa * Q_SCALE).astype(BF16)
        q_ref[:, 512:1024] = (qb * Q_SCALE).astype(BF16)
        ka = rope(norm(p_ref[:, KA:KA + 128], kn_ref[...], 128), 128)
        kb = rope(p_ref[:, KB:KB + 128], 128)
        kv_ref[:, 0:256] = _interleave_kv(ka, p_ref[:, VA:VA + 128]).astype(BF16)
        kv_ref[:, 256:512] = _interleave_kv(kb, p_ref[:, VB:VB + 128]).astype(BF16)

    return pl.pallas_call(
        body, name=name, grid=(m // TM,), in_specs=specs,
        out_specs=[pl.BlockSpec((TM, 1024), lambda t: (t, 0)), pl.BlockSpec((TM, 512), lambda t: (t, 0))],
        out_shape=[jax.ShapeDtypeStruct((m, 1024), BF16), jax.ShapeDtypeStruct((m, 512), BF16)],
        compiler_params=_params(("parallel",)),
    )(proj, qn, kn, cos, sin, emat)


def _prep_bwd(proj, dq, dkv, qn, kn, cos, sin, emat, nt, name, comm=None):
    m = proj.shape[0]
    specs = [
        pl.BlockSpec((TM, PROJ_W), lambda t: (t, 0)),
        pl.BlockSpec((TM, 1024), lambda t: (t, 0)), pl.BlockSpec((TM, 512), lambda t: (t, 0)),
        pl.BlockSpec((1, 512), lambda t: (0, 0)), pl.BlockSpec((1, 128), lambda t: (0, 0)),
        pl.BlockSpec((TM, 512), lambda t: (t % nt, 0)), pl.BlockSpec((TM, 512), lambda t: (t % nt, 0)),
        pl.BlockSpec((512, 512), lambda t: (0, 0)),
    ]

    def body(p_ref, dq_ref, dkv_ref, qn_ref, kn_ref, cos_ref, sin_ref, e_ref, dp_ref, dqn_ref, dkn_ref):
        t = pl.program_id(0)
        cos_, sin_, e = cos_ref[...], sin_ref[...], e_ref[...]

        @pl.when(t == 0)
        def _():
            dqn_ref[...] = jnp.zeros_like(dqn_ref)
            dkn_ref[...] = jnp.zeros_like(dkn_ref)

        def unrope(dy, w):
            return dy * cos_[:, :w] + _swap16(dy * sin_[:, :w])

        def norm_bwd(x, g, dy, w):
            r = lax.rsqrt(_seg_mean(x * x, e[:w, :w]) + EPS)
            xh = x * r
            dxh = dy * g
            dx = r * (dxh - xh * _seg_mean(dxh * xh, e[:w, :w]))
            return dx, jnp.sum(dy * xh, axis=0, keepdims=True)

        dqa, dgq = norm_bwd(p_ref[:, QA:QA + 512], qn_ref[...], unrope(dq_ref[:, 0:512] * Q_SCALE, 512), 512)
        dp_ref[:, QA:QA + 512] = dqa.astype(BF16)
        dp_ref[:, QB:QB + 512] = unrope(dq_ref[:, 512:1024] * Q_SCALE, 512).astype(BF16)
        da = dkv_ref[:, 0:256]
        db = dkv_ref[:, 256:512]
        dka = jnp.concatenate([da[:, 0:64], da[:, 128:192]], axis=1)
        dva = jnp.concatenate([da[:, 64:128], da[:, 192:256]], axis=1)
        dkb = jnp.concatenate([db[:, 0:64], db[:, 128:192]], axis=1)
        dvb = jnp.concatenate([db[:, 64:128], db[:, 192:256]], axis=1)
        dka, dgk = norm_bwd(p_ref[:, KA:KA + 128], kn_ref[...], unrope(dka, 128), 128)
        dp_ref[:, KA:KA + 128] = dka.astype(BF16)
        dp_ref[:, VA:VA + 128] = dva.astype(BF16)
        dp_ref[:, KB:KB + 128] = unrope(dkb, 128).astype(BF16)
        dp_ref[:, VB:VB + 128] = dvb.astype(BF16)
        dqn_ref[0:1, :] += dgq
        dkn_ref[0:1, :] += dgk

    outs, couts = _call(
        body, name=name, grid=(m // TM,), in_specs=specs,
        out_specs=[pl.BlockSpec((TM, PROJ_W), lambda t: (t, 0)), pl.BlockSpec((8, 512), lambda t: (0, 0)),
                   pl.BlockSpec((8, 128), lambda t: (0, 0))],
        out_shape=[jax.ShapeDtypeStruct((m, PROJ_W), BF16), jax.ShapeDtypeStruct((8, 512), F32),
                   jax.ShapeDtypeStruct((8, 128), F32)],
        args=[proj, dq, dkv, qn, kn, cos, sin, emat], sem=("arbitrary",), comm=comm)
    return outs if comm is None else (outs, couts)


def _attn_case(t, hg, kv_ref, c_rows, t_rows, fn, keys_first=False):
    wl = TM + 2 * WINDOW
    kvd = lambda a, n: kv_ref[pl.ds(a, n), :]
    dense = hg < 2
    ctx = t == 0

    @pl.when(jnp.logical_and(dense, ctx))
    def _():
        kv = kvd(0, c_rows)
        fn(kv[:, :64], kv[:, 64:], None, False, [(0, c_rows)])

    @pl.when(jnp.logical_and(dense, jnp.logical_not(ctx)))
    def _():
        kv = kvd(0, t_rows)
        fn(kv[:, :64], kv[:, 64:], None, False, [(0, t_rows)])

    @pl.when(jnp.logical_and(jnp.logical_not(dense), ctx))
    def _():
        kv = kvd(0, c_rows)
        fn(kv[:, :64], kv[:, 64:], None, True, [(0, c_rows)])

    @pl.when(jnp.logical_and(jnp.logical_not(dense), jnp.logical_not(ctx)))
    def _():
        start = pl.multiple_of(jnp.minimum(c_rows + (t - 1) * TM - WINDOW, t_rows - wl), 128)
        kv = jnp.concatenate([kvd(0, c_rows), kvd(start, wl)], axis=0)
        shape = (c_rows + wl, TM) if keys_first else (TM, c_rows + wl)
        q_i = lax.broadcasted_iota(jnp.int32, shape, 1 if keys_first else 0)
        k_i = lax.broadcasted_iota(jnp.int32, shape, 0 if keys_first else 1)
        qpos = (t - 1) * TM + q_i
        kpos = start - 2 * c_rows + k_i
        mask = jnp.logical_or(k_i < c_rows, jnp.logical_and(jnp.abs(kpos - qpos) <= WINDOW, kpos >= 0))
        fn(kv[:, :64], kv[:, 64:], mask, True, [(0, c_rows), (start, wl)])


def _head_columns(cols):
    lane = lax.broadcasted_iota(jnp.int32, (TM, 128), 1)
    out = jnp.zeros((TM, 128), F32)
    for g, col in enumerate(cols):
        out = jnp.where(lane == g, col, out)
    return out


def _attn_specs(t_rows):
    nt = t_rows // TM
    q_spec = pl.BlockSpec((None, TM, Q_WIDTH), lambda b, hg, t, s: (b, t, hg))
    kv_spec = pl.BlockSpec((None, t_rows, 128), lambda b, hg, t, s: (b, 0, hg))
    lse_spec = pl.BlockSpec((None, TM, 128), lambda b, hg, t, s: (hg, b * nt + t, 0))
    return q_spec, kv_spec, lse_spec


def _attn_fwd(q_all, kv_all, sink8, c_rows, name, comm=None):
    bl, t_rows, _ = q_all.shape
    q_spec, kv_spec, lse_spec = _attn_specs(t_rows)

    def body(sink_ref, q_ref, kv_ref, o_ref, ot_ref, lse_ref):
        hg, t = pl.program_id(1), pl.program_id(2)

        def fn(k, v, mask, use_sink, spans):
            outs, lses = [], []
            for g in range(GROUP):
                s = lax.dot_general(q_ref[:, g * 64:(g + 1) * 64], k, (((1,), (1,)), ((), ())),
                                    preferred_element_type=F32)
                if mask is not None:
                    s = jnp.where(mask, s, NEG_BIG)
                mx = jnp.max(s, axis=1, keepdims=True)
                if use_sink:
                    sink = sink_ref[jnp.maximum(hg - 2, 0) * GROUP + g]
                    mx = jnp.maximum(mx, sink)
                p = jnp.exp(s - mx)
                l = jnp.sum(p, axis=1, keepdims=True)
                if use_sink:
                    l = l + jnp.exp(sink - mx)
                outs.append(jnp.dot(p.astype(BF16), v, preferred_element_type=F32) * (1.0 / l))
                lses.append(mx + jnp.log(l))
            o = jnp.concatenate(outs, axis=1).astype(BF16)
            o_ref[...] = o
            ot_ref[...] = o.T
            lse_ref[...] = _head_columns(lses)

        _attn_case(t, hg, kv_ref, c_rows, t_rows, fn)

    nt = t_rows // TM
    ot_spec = pl.BlockSpec((Q_WIDTH, TM), lambda b, hg, t, s: (hg, b * nt + t))
    outs, couts = _call(
        body, name=name, grid=(bl, N_HG, nt), in_specs=[q_spec, kv_spec], out_specs=[q_spec, ot_spec, lse_spec],
        out_shape=[jax.ShapeDtypeStruct(q_all.shape, BF16), jax.ShapeDtypeStruct((N_HG * Q_WIDTH, bl * t_rows), BF16),
                   jax.ShapeDtypeStruct((N_HG, bl * t_rows, 128), F32)],
        args=[sink8, q_all, kv_all], prefetch=1, sem=("parallel", "parallel", "arbitrary"), comm=comm)
    return outs if comm is None else (outs, couts)


def _attn_bwd(q_all, kv_all, do, o, lse, sink8, c_rows, name, comm=None):
    bl, t_rows, _ = q_all.shape
    q_spec, kv_spec, lse_spec = _attn_specs(t_rows)
    ds_spec = pl.BlockSpec((None, None, 8, 128), lambda b, hg, t, s: (b, hg, 0, 0))

    def body(sink_ref, q_ref, kv_ref, do_ref, o_ref, lse_ref, dq_ref, dkv_ref, dsk_ref):
        hg, t = pl.program_id(1), pl.program_id(2)

        @pl.when(t == 0)
        def _():
            dkv_ref[...] = jnp.zeros_like(dkv_ref)
            dsk_ref[...] = jnp.zeros_like(dsk_ref)

        lse_rows = lse_ref[...].T
        dd_cols = [jnp.sum(do_ref[:, g * 64:(g + 1) * 64].astype(F32) * o_ref[:, g * 64:(g + 1) * 64].astype(F32),
                           axis=1, keepdims=True) for g in range(GROUP)]
        dd_rows = _head_columns(dd_cols).T

        def fn(k, v, mask, use_sink, spans):
            k_t = k.T
            dq_t, dsinks = [], []
            dk = jnp.zeros(k.shape, F32)
            dv = jnp.zeros(v.shape, F32)
            for g in range(GROUP):
                q = q_ref[:, g * 64:(g + 1) * 64]
                do_g = do_ref[:, g * 64:(g + 1) * 64]
                lse_g, dd_g = lse_rows[g:g + 1, :], dd_rows[g:g + 1, :]
                s = lax.dot_general(k, q, (((1,), (1,)), ((), ())), preferred_element_type=F32)
                if mask is not None:
                    s = jnp.where(mask, s, NEG_BIG)
                pb = jnp.exp(s - lse_g).astype(BF16)
                dp = lax.dot_general(v, do_g, (((1,), (1,)), ((), ())), preferred_element_type=F32)
                ds = (pb.astype(F32) * (dp - dd_g)).astype(BF16)
                dk = dk + jnp.dot(ds, q, preferred_element_type=F32)
                dv = dv + jnp.dot(pb, do_g, preferred_element_type=F32)
                dq_t.append(jnp.dot(k_t, ds, preferred_element_type=F32))
                if use_sink:
                    p_sink = jnp.exp(sink_ref[jnp.maximum(hg - 2, 0) * GROUP + g] - lse_g)
                    dsinks.append(jnp.broadcast_to(-jnp.sum(p_sink * dd_g, axis=1, keepdims=True), (1, 128)))
            dq_ref[...] = jnp.concatenate(dq_t, axis=0).T
            dkv = jnp.concatenate([dk, dv], axis=1)
            off = 0
            for start, size in spans:
                dkv_ref[pl.ds(start, size), :] += dkv[off:off + size]
                off += size
            if use_sink:
                dsk_ref[0:GROUP, :] += jnp.concatenate(dsinks, axis=0)

        _attn_case(t, hg, kv_ref, c_rows, t_rows, fn, keys_first=True)

    outs, couts = _call(
        body, name=name, grid=(bl, N_HG, t_rows // TM), in_specs=[q_spec, kv_spec, q_spec, q_spec, lse_spec],
        out_specs=[q_spec, kv_spec, ds_spec],
        out_shape=[jax.ShapeDtypeStruct(q_all.shape, F32), jax.ShapeDtypeStruct(kv_all.shape, F32),
                   jax.ShapeDtypeStruct((bl, N_HG, 8, 128), F32)],
        args=[sink8, q_all, kv_all, do, o, lse], prefetch=1, sem=("parallel", "parallel", "arbitrary"), comm=comm)
    return outs if comm is None else (outs, couts)


def _silu(x):
    return x * jax.nn.sigmoid(x)


def _ada_fwd(c_rows, w_ada, b_cols, name):
    nl, d, w = w_ada.shape
    r = c_rows.shape[0]

    def body(c_ref, w_ref, b_ref, o_ref):
        s = _silu(c_ref[...]).astype(BF16)
        o_ref[...] = jnp.dot(s, w_ref[...].astype(BF16), preferred_element_type=F32) + b_ref[...]

    return pl.pallas_call(
        body, name=name, grid=(nl,),
        in_specs=[pl.BlockSpec((r, d), lambda l: (0, 0)), pl.BlockSpec((None, d, w), lambda l: (l, 0, 0)),
                  pl.BlockSpec((None, 1, w), lambda l: (l, 0, 0))],
        out_specs=pl.BlockSpec((None, r, w), lambda l: (l, 0, 0)),
        out_shape=jax.ShapeDtypeStruct((nl, r, w), F32), compiler_params=_params(("parallel",)),
    )(c_rows, w_ada, b_cols)


def _ada_bwd(c_rows, c_ctx, dmod, w_ada, name):
    nl, d, w = w_ada.shape
    r = c_rows.shape[0]

    def body(c_ref, cc_ref, g_ref, w_ref, dw_ref, dc_ref):
        l = pl.program_id(0)
        s = _silu(c_ref[...]).astype(BF16)
        gm = g_ref[...].astype(BF16)
        dw_ref[...] = lax.dot_general(s, gm, (((0,), (0,)), ((), ())), preferred_element_type=F32)
        ds = lax.dot_general(gm, w_ref[...].astype(BF16), (((1,), (1,)), ((), ())), preferred_element_type=F32)
        rows = lax.broadcasted_iota(jnp.int32, ds.shape, 0)
        dsc = jnp.sum(jnp.where(rows % ADA_ROWS == 2, ds, 0.0), axis=0, keepdims=True)
        x = cc_ref[...]
        sg = jax.nn.sigmoid(x)
        dcc = dsc * (sg * (1.0 + x * (1.0 - sg)))
        out = jnp.concatenate([dcc, jnp.zeros((7, d), F32)], axis=0)

        @pl.when(l == 0)
        def _():
            dc_ref[...] = out

        @pl.when(l != 0)
        def _():
            dc_ref[...] += out

    return pl.pallas_call(
        body, name=name, grid=(nl,),
        in_specs=[pl.BlockSpec((r, d), lambda l: (0, 0)), pl.BlockSpec((1, d), lambda l: (0, 0)),
                  pl.BlockSpec((None, r, w), lambda l: (l, 0, 0)), pl.BlockSpec((None, d, w), lambda l: (l, 0, 0))],
        out_specs=[pl.BlockSpec((None, d, w), lambda l: (l, 0, 0)), pl.BlockSpec((8, d), lambda l: (0, 0))],
        out_shape=[jax.ShapeDtypeStruct((nl, d, w), F32), jax.ShapeDtypeStruct((8, d), F32)],
        compiler_params=_params(("arbitrary",)),
    )(c_rows, c_ctx, dmod, w_ada)


def _adam_math(w, g, m, v):
    m = ADAM_B1 * m + (1.0 - ADAM_B1) * g
    v = ADAM_B2 * v + (1.0 - ADAM_B2) * (g * g)
    m_hat = m / (1.0 - ADAM_B1 ** ADAM_STEP)
    v_hat = v / (1.0 - ADAM_B2 ** ADAM_STEP)
    delta = -ADAM_LR * (m_hat / (jnp.sqrt(v_hat) + ADAM_EPS) + ADAM_WD * w)
    return delta, m, v


def _adamw(w, m, v, g_own, g_recv, name, rows=256):
    nl, r, c = w.shape
    tr = min(rows, r)
    spec = pl.BlockSpec((None, tr, c), lambda l, i: (l, i, 0))
    per_layer = isinstance(g_own, (list, tuple))
    own = list(g_own) if per_layer else [g_own]
    recv = [] if g_recv is None else list(g_recv)
    in_specs = [spec] * 3 + [pl.BlockSpec((tr, c), lambda l, i: (i, 0)) if per_layer else spec] * len(own)
    in_specs += [pl.BlockSpec((N_DEV, tr, c), lambda l, i: (0, i, 0))] * len(recv)

    def body(*refs):
        w_ref, m_ref, v_ref = refs[:3]
        own_refs, recv_refs = refs[3:3 + len(own)], refs[3 + len(own):3 + len(own) + len(recv)]
        go_ref, d_ref, mo_ref, vo_ref = refs[-4:]

        def update(li):
            g = own_refs[li][...].astype(F32)
            if recv:
                for k in range(1, N_DEV):
                    g = g + recv_refs[li][k].astype(F32)
            delta, m_, v_ = _adam_math(w_ref[...], g, m_ref[...], v_ref[...])
            go_ref[...] = g
            d_ref[...] = delta
            mo_ref[...] = m_
            vo_ref[...] = v_

        if per_layer:
            for li in range(nl):
                pl.when(pl.program_id(0) == li)(functools.partial(update, li))
        else:
            update(0)

    return pl.pallas_call(
        body, name=name, grid=(nl, r // tr), in_specs=in_specs, out_specs=[spec] * 4,
        out_shape=[jax.ShapeDtypeStruct(w.shape, F32)] * 4, compiler_params=_params(("parallel", "parallel")),
    )(w, m, v, *own, *recv)


def _small_adamw(w, m, v, g_all, name):
    def body(w_ref, m_ref, v_ref, g_ref, go_ref, d_ref, mo_ref, vo_ref):
        g = g_ref[0]
        for k in range(1, N_DEV):
            g = g + g_ref[k]
        delta, m_, v_ = _adam_math(w_ref[...], g, m_ref[...], v_ref[...])
        go_ref[...] = g
        d_ref[...] = delta
        mo_ref[...] = m_
        vo_ref[...] = v_

    return pl.pallas_call(
        body, name=name, out_shape=[jax.ShapeDtypeStruct(w.shape, F32)] * 4, compiler_params=_params(),
    )(w, m, v, g_all)


SMALL = ("c_ctx", "b_ada", "g_pre_mix", "g_post_mix", "g_pre_mlp", "g_post_mlp", "q_norm", "k_norm", "sink")


def _pack_small(parts):
    flat = jnp.concatenate([parts[n].reshape(-1) for n in SMALL])
    rows = -(-flat.shape[0] // 1024) * 8
    return jnp.pad(flat, (0, rows * 128 - flat.shape[0])).reshape(rows, 128)


def _unpack_small(packed, like):
    flat = packed.reshape(-1)
    out, off = {}, 0
    for n in SMALL:
        size = int(np.prod(like[n].shape))
        out[n] = flat[off:off + size].reshape(like[n].shape)
        off += size
    return out


def kernel(x, c, ctx, c_ctx, w_ada, b_ada, g_pre_mix, g_post_mix, g_pre_mlp, g_post_mlp, w_in, q_norm, k_norm, sink, w_out, w_up, w_down, loss_target, m_c_ctx, m_w_ada, m_b_ada, m_g_pre_mix, m_g_post_mix, m_g_pre_mlp, m_g_post_mlp, m_w_in, m_q_norm, m_k_norm, m_sink, m_w_out, m_w_up, m_w_down, v_c_ctx, v_w_ada, v_b_ada, v_g_pre_mix, v_g_post_mix, v_g_pre_mlp, v_g_post_mlp, v_w_in, v_q_norm, v_k_norm, v_sink, v_w_out, v_w_up, v_w_down):
    bl, s_rows, d = x.shape
    c_rows = ctx.shape[1]
    assert c_rows == TM and s_rows % TM == 0 and bl == 2
    t_rows = c_rows + s_rows
    nt = t_rows // TM
    m_rows = bl * t_rows
    nl = w_in.shape[0]
    ada_w = w_ada.shape[2]
    d_ff = w_up.shape[2] * N_DEV
    me = _my_index()

    shard = lambda w_, l: w_[l].astype(BF16)
    c_pad = jnp.concatenate([c, c_ctx[None, :], jnp.zeros((ADA_ROWS - bl - 1, d), F32)], axis=0)
    c_all, w_in_g, w_out_g = _comm_only(
        _Comm([(c_pad, GATHER), (shard(w_in, 0), GATHER), (shard(w_out, 0), GATHER)]), "gather_first")
    c_all = c_all.reshape(N_DEV * ADA_ROWS, d)
    gathered = {0: dict(w_in=w_in_g, w_out=w_out_g)}

    def layer_weights(l):
        g_ = gathered[l]
        w_out_f = g_["w_out"].reshape(-1, d)
        w_down_f = g_["w_down"].reshape(d_ff, d)
        return dict(
            w_in_t=g_["w_in"].transpose(0, 2, 1).reshape(PROJ_W, d), w_out_f=w_out_f, w_out_t=w_out_f.T,
            w_up_s=g_["w_up"], w_up_t=g_["w_up"].transpose(0, 2, 1).reshape(d_ff, d),
            w_down_f=w_down_f, w_down_t=w_down_f.T)

    big = dict(tm=1536, tn=512)
    deep = dict(tm=1536, tn=256, tk=d_ff)
    wide = dict(tm=1024, tn=256, tk=m_rows)

    b_cols = lax.dynamic_slice(b_ada, (0, me * ada_w), (nl, ada_w))[:, None, :]
    mod_cols = _ada_fwd(c_all, w_ada, b_cols, "ada_fwd")
    mod_slots = mod_cols.reshape(nl, N_DEV, ADA_ROWS, ada_w).transpose(1, 0, 2, 3)
    mod_g, = _comm_only(_Comm([(mod_slots, TO_OWNER)]), "exchange_mod")
    mine = mod_g.transpose(1, 2, 0, 3).reshape(nl, ADA_ROWS, N_MOD, d)
    pad = jnp.zeros((bl, 2, MOD_ROWS - N_MOD, d), F32)
    modsel = [jnp.concatenate([jnp.stack([jnp.broadcast_to(mine[l, bl], (bl, N_MOD, d)), mine[l, :bl]], axis=1), pad],
                              axis=2) for l in range(nl)]

    cos, sin = _rope_tables(t_rows, c_rows)
    emat = _head_mean_matrix()
    row = lambda a: a[None, :]
    qn = [jnp.tile(q_norm[l], 8)[None, :] for l in range(nl)]
    kn = [jnp.tile(k_norm[l], 2)[None, :] for l in range(nl)]

    h = jnp.concatenate([ctx, x], axis=1).reshape(m_rows, d)
    saved = []
    weights_of = {}
    for l in range(nl):
        u, u_t = _norm_mod_fwd(h, row(g_pre_mix[l]), modsel[l], 0, 1, nt, f"mix_mod_fwd{l}")
        w_in_f = gathered[l]["w_in"].transpose(1, 0, 2).reshape(d, PROJ_W)
        proj = _mm(u, w_in_f, name=f"mm_in{l}", tk=d, **big)
        q_all, kv_all = _prep_fwd(proj, qn[l], kn[l], cos, sin, emat, nt, f"prep_fwd{l}")
        (o, o_t, lse), (w_up_g, w_down_g) = _attn_fwd(
            q_all.reshape(bl, t_rows, 1024), kv_all.reshape(bl, t_rows, 512), sink[l], c_rows, f"attn_fwd{l}",
            comm=_Comm([(shard(w_up, l), GATHER), (shard(w_down, l), GATHER)]))
        o = o.reshape(m_rows, 1024)
        gathered[l].update(w_up=w_up_g, w_down=w_down_g)
        wl = weights_of[l] = layer_weights(l)
        mix = _mm(o, wl["w_out_f"], name=f"mm_out{l}", tk=1024, **big)
        h_mid = _gate_fwd(h, mix, row(g_post_mix[l]), modsel[l], 2, nt, f"mix_gate_fwd{l}")
        v_in, v_t = _norm_mod_fwd(h_mid, row(g_pre_mlp[l]), modsel[l], 3, 4, nt, f"mlp_mod_fwd{l}")
        more = l + 1 < nl
        res_up = _mm(v_in, wl["w_up_s"], name=f"mm_up{l}", b_mode="nn_slots", epilogue="relu2", tk=d,
                     comm=_Comm([(shard(w_in, l + 1), GATHER)]) if more else None, **big)
        (r_act, s_act, r_t), nxt_in = res_up if more else (res_up, None)
        res_down = _mm(r_act, wl["w_down_f"], name=f"mm_down{l}",
                       comm=_Comm([(shard(w_out, l + 1), GATHER)]) if more else None, **deep)
        y, nxt_out = res_down if more else (res_down, None)
        if more:
            gathered[l + 1] = dict(w_in=nxt_in[0], w_out=nxt_out[0])
        h_out = _gate_fwd(h_mid, y, row(g_post_mlp[l]), modsel[l], 5, nt, f"mlp_gate_fwd{l}")
        saved.append((h, u_t, proj, q_all, kv_all, o, o_t, lse, mix, h_mid, v_t, r_t, s_act, y))
        h = h_out

    dh, ss = _loss_grad(h, loss_target.reshape(bl * s_rows, d), c_rows, nt, "loss_grad")
    loss = lax.psum(0.5 * ss[0, 0] / d, MESH_AXES)

    small_g = {n: [None] * nl for n in SMALL if n not in ("c_ctx", "b_ada")}
    dmod_rows = []
    slots = {n: [None] * nl for n in ("w_in", "w_out", "w_up", "w_down")}
    recvd = {n: [None] * nl for n in slots}
    send = lambda n, l_: (slots[n][l_], TO_OWNER_XOR)
    for l in reversed(range(nl)):
        h_in, u_t, proj, q_all, kv_all, o, o_t, lse, mix, h_mid, v_t, r_t, s_act, y = saved[l]
        wl = weights_of[l]
        later = l + 1 < nl
        dy, dvec_g2 = _gate_bwd(y, row(g_post_mlp[l]), modsel[l], dh, 5, nt, f"mlp_gate_bwd{l}")
        res = _mm(dy, wl["w_down_t"], name=f"mm_da{l}", epilogue="mul", extra=s_act, out_dtype=BF16, tk=d,
                  comm=_Comm([send("w_out", l + 1)]) if later else None, **big)
        da = res[0] if later else res
        if later:
            recvd["w_out"][l + 1] = res[1][0]
        res = _mm(r_t, dy, name=f"mm_dw_down{l}", out_dtype=BF16,
                  comm=_Comm([send("w_in", l + 1)]) if later else None, **wide)
        dw_down = res[0] if later else res
        if later:
            recvd["w_in"][l + 1] = res[1][0]
        dw_up = _mm(v_t, da, name=f"mm_dw_up{l}", out_mode="slots", out_dtype=BF16, **wide)
        slots["w_down"][l] = dw_down.reshape(N_DEV, -1, d)
        slots["w_up"][l] = dw_up
        dv = _mm(da, wl["w_up_t"], name=f"mm_dv{l}", **deep)
        dh, dvec_m2 = _norm_mod_bwd(h_mid, row(g_pre_mlp[l]), modsel[l], dv, dh, 3, 4, nt, f"mlp_mod_bwd{l}")
        dmix, dvec_g1 = _gate_bwd(mix, row(g_post_mix[l]), modsel[l], dh, 2, nt, f"mix_gate_bwd{l}")
        do = _mm(dmix, wl["w_out_t"], name=f"mm_do{l}", out_dtype=BF16, tk=d, **big)
        dw_out = _mm(o_t, dmix, name=f"mm_dw_out{l}", out_dtype=BF16, **wide)
        slots["w_out"][l] = dw_out.reshape(N_DEV, -1, d)
        (dq, dkv, dsk), (recvd["w_down"][l], recvd["w_up"][l]) = _attn_bwd(
            q_all.reshape(bl, t_rows, 1024), kv_all.reshape(bl, t_rows, 512), do.reshape(bl, t_rows, 1024),
            o.reshape(bl, t_rows, 1024), lse, sink[l], c_rows, f"attn_bwd{l}",
            comm=_Comm([send("w_down", l), send("w_up", l)]))
        last = l == 0
        res = _prep_bwd(proj, dq.reshape(m_rows, 1024), dkv.reshape(m_rows, 512), qn[l], kn[l], cos, sin, emat, nt,
                        f"prep_bwd{l}", comm=_Comm([send("w_out", l)]) if last else None)
        dproj, dqn, dkn = res[0] if last else res
        if last:
            recvd["w_out"][l] = res[1][0]
        dw_in = _mm(u_t, dproj, name=f"mm_dw_in{l}", out_dtype=BF16, **wide)
        slots["w_in"][l] = dw_in.reshape(d, N_DEV, PROJ_W // N_DEV).transpose(1, 0, 2)
        res = _mm(dproj, wl["w_in_t"], name=f"mm_du{l}", tk=PROJ_W,
                  comm=_Comm([send("w_in", l)]) if last else None, **big)
        du = res[0] if last else res
        if last:
            recvd["w_in"][l] = res[1][0]
        dh, dvec_m1 = _norm_mod_bwd(h_in, row(g_pre_mix[l]), modsel[l], du, dh, 0, 1, nt, f"mix_mod_bwd{l}")

        small_g["g_pre_mix"][l] = jnp.sum(dvec_m1[:, :, 2], axis=(0, 1))
        small_g["g_post_mix"][l] = jnp.sum(dvec_g1[:, :, 1], axis=(0, 1))
        small_g["g_pre_mlp"][l] = jnp.sum(dvec_m2[:, :, 2], axis=(0, 1))
        small_g["g_post_mlp"][l] = jnp.sum(dvec_g2[:, :, 1], axis=(0, 1))
        small_g["q_norm"][l] = jnp.sum(dqn[0].reshape(8, HEAD_DIM), axis=0)
        small_g["k_norm"][l] = jnp.sum(dkn[0].reshape(2, HEAD_DIM), axis=0)
        small_g["sink"][l] = jnp.sum(dsk[:, 2:, :GROUP, 0], axis=0).reshape(-1)
        dms = jnp.stack([dvec_m1[:, :, 0], dvec_m1[:, :, 1], dvec_g1[:, :, 0],
                         dvec_m2[:, :, 0], dvec_m2[:, :, 1], dvec_g2[:, :, 0]], axis=2)
        rows = jnp.concatenate([dms[:, 1], jnp.sum(dms[:, 0], axis=0)[None]], axis=0)
        dmod_rows.append(jnp.pad(rows.reshape(bl + 1, N_MOD * d), ((0, ADA_ROWS - bl - 1), (0, 0))))
    dmod_rows = dmod_rows[::-1]
    grad_x = dh.reshape(bl, t_rows, d)[:, c_rows:]

    dmod_slots = jnp.stack(dmod_rows).reshape(nl, ADA_ROWS, N_DEV, ada_w).transpose(2, 0, 1, 3)
    dmod_g, = _comm_only(_Comm([(dmod_slots, TO_OWNER)]), "exchange_dmod")
    dmod_mine = dmod_g.transpose(1, 0, 2, 3).reshape(nl, N_DEV * ADA_ROWS, ada_w)
    dw_ada, dcc = _ada_bwd(c_all, c_ctx[None, :], dmod_mine, w_ada, "ada_bwd")

    parts = {n: jnp.stack(small_g[n]) for n in small_g}
    parts["c_ctx"] = dcc[0]
    parts["b_ada"] = jnp.stack([jnp.sum(r_[: bl + 1], axis=0) for r_ in dmod_rows])
    weights = dict(c_ctx=c_ctx, b_ada=b_ada, g_pre_mix=g_pre_mix, g_post_mix=g_post_mix, g_pre_mlp=g_pre_mlp,
                   g_post_mlp=g_post_mlp, q_norm=q_norm, k_norm=k_norm, sink=sink)
    moms = dict(c_ctx=m_c_ctx, b_ada=m_b_ada, g_pre_mix=m_g_pre_mix, g_post_mix=m_g_post_mix, g_pre_mlp=m_g_pre_mlp,
                g_post_mlp=m_g_post_mlp, q_norm=m_q_norm, k_norm=m_k_norm, sink=m_sink)
    vels = dict(c_ctx=v_c_ctx, b_ada=v_b_ada, g_pre_mix=v_g_pre_mix, g_post_mix=v_g_post_mix, g_pre_mlp=v_g_pre_mlp,
                g_post_mlp=v_g_post_mlp, q_norm=v_q_norm, k_norm=v_k_norm, sink=v_sink)
    small_all, = _comm_only(_Comm([(_pack_small(parts), GATHER)]), "gather_small")
    s_out = _small_adamw(_pack_small(weights), _pack_small(moms), _pack_small(vels), small_all, "adamw_small")
    s_g, s_d, s_m, s_v = [_unpack_small(a, weights) for a in s_out]

    res = {}
    for n, w_, m_, v_ in (("w_in", w_in, m_w_in, v_w_in), ("w_out", w_out, m_w_out, v_w_out),
                          ("w_up", w_up, m_w_up, v_w_up), ("w_down", w_down, m_w_down, v_w_down)):
        own = [lax.dynamic_index_in_dim(slots[n][l], me, axis=0, keepdims=False) for l in range(nl)]
        res[n] = _adamw(w_, m_, v_, own, recvd[n], f"adamw_{n}")
    res["w_ada"] = _adamw(w_ada, m_w_ada, v_w_ada, dw_ada, None, "adamw_w_ada")

    order = ("c_ctx", "w_ada", "b_ada", "g_pre_mix", "g_post_mix", "g_pre_mlp", "g_post_mlp", "w_in", "q_norm",
             "k_norm", "sink", "w_out", "w_up", "w_down")
    outs = [loss, grad_x]
    for i, small in enumerate((s_g, s_d, s_m, s_v)):
        outs += [small[n] if n in small else res[n][i] for n in order]
    return tuple(outs)
```

```python
import functools

import jax
import jax.numpy as jnp
import numpy as np
from jax import lax
from jax.experimental import pallas as pl
from jax.experimental.pallas import tpu as pltpu

F32 = jnp.float32
BF16 = jnp.bfloat16

HEAD_DIM = 64
GROUP = 4
N_HG = 4
Q_WIDTH = GROUP * HEAD_DIM
WINDOW = 128
GRID_W = 64
ROPE_THETA = 10000.0
EPS = 1e-6
NEG_BIG = -1e30
N_MOD = 6
MOD_ROWS = 8
TM = 256
N_DEV = 8
ADA_ROWS = 8
VMEM_LIMIT = 56 * 1024 * 1024

ADAM_LR = 0.001
ADAM_B1 = 0.9
ADAM_B2 = 0.999
ADAM_EPS = 1e-08
ADAM_WD = 0.01
ADAM_STEP = 10

MESH_AXES = ("x", "y", "c")


def _params(sem=None):
    kw = dict(vmem_limit_bytes=VMEM_LIMIT)
    if sem is not None:
        kw["dimension_semantics"] = sem
    return pltpu.CompilerParams(**kw)


def _my_index():
    return 4 * lax.axis_index("x") + 2 * lax.axis_index("y") + lax.axis_index("c")


def _peer(k):
    x, y, c = lax.axis_index("x"), lax.axis_index("y"), lax.axis_index("c")
    kx, ky, kc = (k >> 2) & 1, (k >> 1) & 1, k & 1
    px = (1 - x) if kx else x
    py = (1 - y) if ky else y
    pc = (1 - c) if kc else c
    return (px, py, pc), 4 * px + 2 * py + pc


GATHER, TO_OWNER, TO_OWNER_XOR = "gather", "to_owner", "to_owner_xor"


class _Comm:
    def __init__(self, items):
        self.items = list(items)
        self.arrays = [a for a, _ in self.items]

    def out_shapes(self):
        return [jax.ShapeDtypeStruct(((N_DEV,) + a.shape) if kind == GATHER else a.shape, a.dtype)
                for a, kind in self.items]

    def sem_shapes(self):
        n = len(self.items) * N_DEV
        return [pltpu.SemaphoreType.DMA((n,)), pltpu.SemaphoreType.DMA((n,))]

    def _copies(self, in_refs, out_refs, send_sems, recv_sems):
        me = _my_index()
        local, remote = [], []
        for i, ((_, kind), x_ref, o_ref) in enumerate(zip(self.items, in_refs, out_refs)):
            base = i * N_DEV
            own_src = x_ref if kind == GATHER else x_ref.at[me]
            own_dst = o_ref.at[0] if kind == TO_OWNER_XOR else o_ref.at[me]
            local.append(pltpu.make_async_copy(own_src, own_dst, send_sems.at[base]))
            for k in range(1, N_DEV):
                peer, pidx = _peer(k)
                remote.append(pltpu.make_async_remote_copy(
                    src_ref=x_ref if kind == GATHER else x_ref.at[pidx],
                    dst_ref=o_ref.at[k] if kind == TO_OWNER_XOR else o_ref.at[me],
                    send_sem=send_sems.at[base + k], recv_sem=recv_sems.at[base + k],
                    device_id=peer, device_id_type=pl.DeviceIdType.MESH))
        return local, remote

    def start(self, in_refs, out_refs, send_sems, recv_sems):
        local, remote = self._copies(in_refs, out_refs, send_sems, recv_sems)
        for cp in local + remote:
            cp.start()

    def wait(self, in_refs, out_refs, send_sems, recv_sems):
        local, remote = self._copies(in_refs, out_refs, send_sems, recv_sems)
        for cp in remote:
            cp.wait_recv()
        for cp in remote:
            cp.wait_send()
        for cp in local:
            cp.wait()


def _call(body, *, name, grid, in_specs, out_specs, out_shape, args, scratch_shapes=(), prefetch=0, sem=None,
          comm=None):
    single = not isinstance(out_shape, (list, tuple))
    out_shape = [out_shape] if single else list(out_shape)
    out_specs = [out_specs] if single else list(out_specs)
    in_specs, scratch_shapes, args = list(in_specs), list(scratch_shapes), list(args)
    n_in, n_out = len(in_specs), len(out_shape)
    if comm is not None:
        nc = len(comm.arrays)
        hbm = pl.BlockSpec(memory_space=pl.ANY)
        inner = body

        def body(*refs):
            pre, r = refs[:prefetch], refs[prefetch:]
            ins, cin = r[:n_in], r[n_in:n_in + nc]
            outs, cout = r[n_in + nc:n_in + nc + n_out], r[n_in + nc + n_out:n_in + 2 * nc + n_out]
            scr, sems = r[n_in + 2 * nc + n_out:len(r) - 2], r[len(r) - 2:]
            ids = [pl.program_id(i) for i in range(len(grid))]

            def when(flags, fn):
                if flags:
                    pl.when(functools.reduce(jnp.logical_and, flags))(fn)
                else:
                    fn()

            when([i == 0 for i in ids], lambda: comm.start(cin, cout, *sems))
            inner(*pre, *ins, *outs, *scr)
            when([i == n - 1 for i, n in zip(ids, grid)], lambda: comm.wait(cin, cout, *sems))

        in_specs += [hbm] * nc
        out_specs += [hbm] * nc
        out_shape += comm.out_shapes()
        scratch_shapes += comm.sem_shapes()
        args += comm.arrays
        sem = ("arbitrary",) * len(grid)
    kw = dict(name=name, out_shape=out_shape, compiler_params=_params(sem if grid else None))
    if prefetch:
        kw["grid_spec"] = pltpu.PrefetchScalarGridSpec(
            num_scalar_prefetch=prefetch, grid=grid, in_specs=in_specs, out_specs=out_specs,
            scratch_shapes=scratch_shapes)
    else:
        kw.update(in_specs=in_specs, out_specs=out_specs, scratch_shapes=scratch_shapes)
        if grid:
            kw["grid"] = grid
    res = list(pl.pallas_call(body, **kw)(*args))
    outs = res[:n_out]
    return (outs[0] if single else outs), res[n_out:]


def _comm_only(comm, name):
    return _call(lambda: None, name=name, grid=(), in_specs=[], out_specs=[], out_shape=[], args=[], comm=comm)[1]


def _mm(a, b, *, name, ta=False, b_mode="nn", out_mode="plain", out_dtype=F32, tm=512, tn=512, tk=512,
        epilogue=None, extra=None, comm=None):
    if ta:
        kdim, m = a.shape
    else:
        m, kdim = a.shape
    if b_mode == "nn":
        n = b.shape[1]
    elif b_mode == "nt":
        n = b.shape[0]
    elif b_mode == "nn_slots":
        n = b.shape[0] * b.shape[2]
        tn = b.shape[2]
    else:
        n = b.shape[1]
        tk = b.shape[2]
    if out_mode == "slots":
        tn = n // N_DEV
    tm, tn, tk = min(tm, m), min(tn, n), min(tk, kdim)
    assert m % tm == 0 and n % tn == 0 and kdim % tk == 0, (name, m, n, kdim, tm, tn, tk)
    nk = kdim // tk

    a_spec = pl.BlockSpec((tk, tm), lambda i, j, k: (k, i)) if ta else pl.BlockSpec((tm, tk), lambda i, j, k: (i, k))
    if b_mode == "nn":
        b_spec = pl.BlockSpec((tk, tn), lambda i, j, k: (k, j))
    elif b_mode == "nt":
        b_spec = pl.BlockSpec((tn, tk), lambda i, j, k: (j, k))
    elif b_mode == "nn_slots":
        b_spec = pl.BlockSpec((None, tk, tn), lambda i, j, k: (j, k, 0))
    else:
        b_spec = pl.BlockSpec((None, tn, tk), lambda i, j, k: (k, j, 0))
    tb = b_mode in ("nt", "nt_slots")
    if out_mode == "plain":
        o_shape, o_spec = (m, n), pl.BlockSpec((tm, tn), lambda i, j, k: (i, j))
    else:
        o_shape, o_spec = (N_DEV, m, tn), pl.BlockSpec((None, tm, tn), lambda i, j, k: (j, i, 0))
    dims = (((0 if ta else 1,), (1 if tb else 0,)), ((), ()))

    in_specs = [a_spec, b_spec]
    args = [a, b]
    if epilogue == "mul":
        in_specs.append(pl.BlockSpec((tm, tn), lambda i, j, k: (i, j)))
        args.append(extra)
    if epilogue == "relu2":
        out_shape = [jax.ShapeDtypeStruct(o_shape, BF16)] * 2 + [jax.ShapeDtypeStruct((n, m), BF16)]
        out_specs = [o_spec, o_spec, pl.BlockSpec((tn, tm), lambda i, j, k: (j, i))]
    else:
        out_shape = jax.ShapeDtypeStruct(o_shape, out_dtype)
        out_specs = o_spec

    def finish(refs, acc):
        if epilogue == "relu2":
            r = jnp.maximum(acc, 0.0)
            r2 = (r * r).astype(BF16)
            refs[2][...] = r2
            refs[3][...] = (2.0 * r).astype(BF16)
            refs[4][...] = r2.T
        elif epilogue == "mul":
            refs[3][...] = (acc * refs[2][...].astype(F32)).astype(out_dtype)
        else:
            refs[2][...] = acc.astype(out_dtype)

    def body(*refs):
        part = lax.dot_general(refs[0][...], refs[1][...], dims, preferred_element_type=F32)
        if nk == 1:
            finish(refs, part)
            return
        acc_ref = refs[-1]
        k = pl.program_id(2)

        @pl.when(k == 0)
        def _():
            acc_ref[...] = part

        @pl.when(jnp.logical_and(k > 0, k < nk - 1))
        def _():
            acc_ref[...] += part

        @pl.when(k == nk - 1)
        def _():
            finish(refs, acc_ref[...] + part)

    outs, couts = _call(
        body, name=name, grid=(m // tm, n // tn, nk), in_specs=in_specs, out_specs=out_specs, out_shape=out_shape,
        args=args, scratch_shapes=[] if nk == 1 else [pltpu.VMEM((tm, tn), F32)],
        sem=("parallel", "parallel", "arbitrary"), comm=comm)
    return outs if comm is None else (outs, couts)


def _row_specs(d, nt):
    row = pl.BlockSpec((TM, d), lambda t: (t, 0))
    vec = pl.BlockSpec((1, d), lambda t: (0, 0))
    mod = pl.BlockSpec((None, None, MOD_ROWS, d), lambda t: (t // nt, jnp.minimum(t % nt, 1), 0, 0))
    return row, vec, mod


def _rms(x):
    r = lax.rsqrt(jnp.mean(x * x, axis=1, keepdims=True) + EPS)
    return r, x * r


def _norm_mod_fwd(h, g, modsel, i_sh, i_sc, nt, name, comm=None):
    m, d = h.shape
    row, vec, mod = _row_specs(d, nt)

    def body(h_ref, g_ref, ms_ref, u_ref, ut_ref):
        _, xh = _rms(h_ref[...])
        ms = ms_ref[...]
        u = (xh * g_ref[...] * (1.0 + ms[i_sc:i_sc + 1]) + ms[i_sh:i_sh + 1]).astype(BF16)
        u_ref[...] = u
        ut_ref[...] = u.T

    outs, couts = _call(
        body, name=name, grid=(m // TM,), in_specs=[row, vec, mod],
        out_specs=[row, pl.BlockSpec((d, TM), lambda t: (0, t))],
        out_shape=[jax.ShapeDtypeStruct((m, d), BF16), jax.ShapeDtypeStruct((d, m), BF16)],
        args=[h, g, modsel], sem=("parallel",), comm=comm)
    return outs if comm is None else (outs, couts)


def _acc_rows(t, nt, dvec_ref, rows):
    first = (t % nt) <= 1

    @pl.when(first)
    def _():
        dvec_ref[...] = rows

    @pl.when(jnp.logical_not(first))
    def _():
        dvec_ref[...] += rows


def _norm_mod_bwd(h, g, modsel, du, dh_in, i_sh, i_sc, nt, name, comm=None, latent_only=False):
    m, d = h.shape
    row, vec, mod = _row_specs(d, nt)
    dh_rows, dh_spec = m, row
    if latent_only:
        dh_rows = m // nt * (nt - 1)
        dh_spec = pl.BlockSpec((TM, d), lambda t: ((t // nt) * (nt - 1) + jnp.maximum(t % nt, 1) - 1, 0))

    def body(h_ref, g_ref, ms_ref, du_ref, dhi_ref, dh_ref, dvec_ref):
        t = pl.program_id(0)
        r, xh = _rms(h_ref[...])
        g_ = g_ref[...]
        ms = ms_ref[...]
        du_ = du_ref[...]
        y = xh * g_
        dy = du_ * (1.0 + ms[i_sc:i_sc + 1])
        dxh = dy * g_
        dx = r * (dxh - xh * jnp.mean(dxh * xh, axis=1, keepdims=True))
        dh_ref[...] = dhi_ref[...] + dx
        rows = jnp.concatenate([
            jnp.sum(du_, axis=0, keepdims=True), jnp.sum(du_ * y, axis=0, keepdims=True),
            jnp.sum(dy * xh, axis=0, keepdims=True), jnp.zeros((MOD_ROWS - 3, d), F32)], axis=0)
        _acc_rows(t, nt, dvec_ref, rows)

    outs, couts = _call(
        body, name=name, grid=(m // TM,), in_specs=[row, vec, mod, row, row], out_specs=[dh_spec, mod],
        out_shape=[jax.ShapeDtypeStruct((dh_rows, d), F32), jax.ShapeDtypeStruct(modsel.shape, F32)],
        args=[h, g, modsel, du, dh_in], sem=("arbitrary",), comm=comm)
    return outs if comm is None else (outs, couts)


def _gate_fwd(h, z, g, modsel, i_g, nt, name):
    m, d = h.shape
    row, vec, mod = _row_specs(d, nt)

    def body(h_ref, z_ref, g_ref, ms_ref, o_ref):
        _, xh = _rms(z_ref[...])
        ms = ms_ref[...]
        o_ref[...] = h_ref[...] + ms[i_g:i_g + 1] * (xh * g_ref[...])

    return pl.pallas_call(
        body, name=name, grid=(m // TM,), in_specs=[row, row, vec, mod], out_specs=row,
        out_shape=jax.ShapeDtypeStruct((m, d), F32), compiler_params=_params(("parallel",)),
    )(h, z, g, modsel)


def _gate_bwd(z, g, modsel, dh, i_g, nt, name):
    m, d = z.shape
    row, vec, mod = _row_specs(d, nt)

    def body(z_ref, g_ref, ms_ref, dh_ref, dz_ref, dvec_ref):
        t = pl.program_id(0)
        r, xh = _rms(z_ref[...])
        g_ = g_ref[...]
        ms = ms_ref[...]
        dh_ = dh_ref[...]
        dy = dh_ * ms[i_g:i_g + 1]
        dxh = dy * g_
        dz_ref[...] = (r * (dxh - xh * jnp.mean(dxh * xh, axis=1, keepdims=True))).astype(BF16)
        rows = jnp.concatenate([
            jnp.sum(dh_ * (xh * g_), axis=0, keepdims=True), jnp.sum(dy * xh, axis=0, keepdims=True),
            jnp.zeros((MOD_ROWS - 2, d), F32)], axis=0)
        _acc_rows(t, nt, dvec_ref, rows)

    return pl.pallas_call(
        body, name=name, grid=(m // TM,), in_specs=[row, vec, mod, row], out_specs=[row, mod],
        out_shape=[jax.ShapeDtypeStruct((m, d), BF16), jax.ShapeDtypeStruct(modsel.shape, F32)],
        compiler_params=_params(("arbitrary",)),
    )(z, g, modsel, dh)


def _loss_grad(h, target, c_rows, nt, name):
    m, d = h.shape
    row = pl.BlockSpec((TM, d), lambda t: (t, 0))
    ntl = nt - 1
    tgt = pl.BlockSpec((TM, d), lambda t: ((t // nt) * ntl + jnp.maximum(t % nt, 1) - 1, 0))
    acc = pl.BlockSpec((8, 128), lambda t: (0, 0))

    def body(h_ref, t_ref, dh_ref, ss_ref):
        t = pl.program_id(0)

        @pl.when(t == 0)
        def _():
            ss_ref[...] = jnp.zeros_like(ss_ref)

        @pl.when(t % nt == 0)
        def _():
            dh_ref[...] = jnp.zeros_like(dh_ref)

        @pl.when(t % nt != 0)
        def _():
            e = h_ref[...] - t_ref[...]
            dh_ref[...] = e * (1.0 / d)
            ss_ref[...] += jnp.sum(e * e)

    return pl.pallas_call(
        body, name=name, grid=(m // TM,), in_specs=[row, tgt], out_specs=[row, acc],
        out_shape=[jax.ShapeDtypeStruct((m, d), F32), jax.ShapeDtypeStruct((8, 128), F32)],
        compiler_params=_params(("arbitrary",)),
    )(h, target)


QA, KA, VA, QB, KB, VB = 0, 512, 640, 768, 1280, 1408
PROJ_W = 1536
Q_SCALE = HEAD_DIM ** -0.5


def _swap16(x):
    lane = lax.broadcasted_iota(jnp.int32, x.shape, 1)
    n = x.shape[1]
    return jnp.where((lane % 32) < 16, pltpu.roll(x, n - 16, 1), pltpu.roll(x, 16, 1))


def _seg_mean(x, e):
    hi = x.astype(BF16)
    lo = (x - hi.astype(F32)).astype(BF16)
    return jnp.dot(hi, e, preferred_element_type=F32) + jnp.dot(lo, e, preferred_element_type=F32)


def _rope_tables(t_rows, c_rows):
    s = t_rows - c_rows
    row_ids = jnp.repeat(jnp.arange(s // GRID_W, dtype=jnp.int32), GRID_W).astype(F32)
    col_ids = jnp.tile(jnp.arange(GRID_W, dtype=jnp.int32), s // GRID_W).astype(F32)
    axis_dim = HEAD_DIM // 2
    inv = ROPE_THETA ** (-jnp.arange(0, axis_dim, 2, dtype=F32) / axis_dim)
    ang_r = row_ids[:, None] * inv[None, :]
    ang_c = col_ids[:, None] * inv[None, :]
    cos = jnp.concatenate([jnp.cos(ang_r), jnp.cos(ang_r), jnp.cos(ang_c), jnp.cos(ang_c)], axis=1)
    sin = jnp.concatenate([-jnp.sin(ang_r), jnp.sin(ang_r), -jnp.sin(ang_c), jnp.sin(ang_c)], axis=1)
    cos = jnp.concatenate([jnp.ones((c_rows, HEAD_DIM), F32), cos], axis=0)
    sin = jnp.concatenate([jnp.zeros((c_rows, HEAD_DIM), F32), sin], axis=0)
    return jnp.tile(cos, (1, 8)), jnp.tile(sin, (1, 8))


def _head_mean_matrix():
    i = np.arange(512)
    return jnp.asarray((i[:, None] // HEAD_DIM == i[None, :] // HEAD_DIM).astype(np.float32) / HEAD_DIM, dtype=BF16)


def _interleave_kv(k, v):
    return jnp.concatenate([k[:, :64], v[:, :64], k[:, 64:], v[:, 64:]], axis=1)


def _prep_fwd(proj, qn, kn, cos, sin, emat, nt, name):
    m = proj.shape[0]
    specs = [
        pl.BlockSpec((TM, PROJ_W), lambda t: (t, 0)),
        pl.BlockSpec((1, 512), lambda t: (0, 0)), pl.BlockSpec((1, 128), lambda t: (0, 0)),
        pl.BlockSpec((TM, 512), lambda t: (t % nt, 0)), pl.BlockSpec((TM, 512), lambda t: (t % nt, 0)),
        pl.BlockSpec((512, 512), lambda t: (0, 0)),
    ]

    def body(p_ref, qn_ref, kn_ref, cos_ref, sin_ref, e_ref, q_ref, kv_ref):
        cos_, sin_, e = cos_ref[...], sin_ref[...], e_ref[...]

        def rope(x, w):
            return x * cos_[:, :w] + _swap16(x) * sin_[:, :w]

        def norm(x, g, w):
            return x * lax.rsqrt(_seg_mean(x * x, e[:w, :w]) + EPS) * g

        qa = rope(norm(p_ref[:, QA:QA + 512], qn_ref[...], 512), 512)
        qb = rope(p_ref[:, QB:QB + 512], 512)
        q_ref[:, 0:512] = (qa * Q_SCALE).astype(BF16)
        q_ref[:, 512:1024] = (qb * Q_SCALE).astype(BF16)
        ka = rope(norm(p_ref[:, KA:KA + 128], kn_ref[...], 128), 128)
        kb = rope(p_ref[:, KB:KB + 128], 128)
        kv_ref[:, 0:256] = _interleave_kv(ka, p_ref[:, VA:VA + 128]).astype(BF16)
        kv_ref[:, 256:512] = _interleave_kv(kb, p_ref[:, VB:VB + 128]).astype(BF16)

    return pl.pallas_call(
        body, name=name, grid=(m // TM,), in_specs=specs,
        out_specs=[pl.BlockSpec((TM, 1024), lambda t: (t, 0)), pl.BlockSpec((TM, 512), lambda t: (t, 0))],
        out_shape=[jax.ShapeDtypeStruct((m, 1024), BF16), jax.ShapeDtypeStruct((m, 512), BF16)],
        compiler_params=_params(("parallel",)),
    )(proj, qn, kn, cos, sin, emat)


def _prep_bwd(proj, dq, dkv, qn, kn, cos, sin, emat, nt, name, comm=None):
    m = proj.shape[0]
    specs = [
        pl.BlockSpec((TM, PROJ_W), lambda t: (t, 0)),
        pl.BlockSpec((TM, 1024), lambda t: (t, 0)), pl.BlockSpec((TM, 512), lambda t: (t, 0)),
        pl.BlockSpec((1, 512), lambda t: (0, 0)), pl.BlockSpec((1, 128), lambda t: (0, 0)),
        pl.BlockSpec((TM, 512), lambda t: (t % nt, 0)), pl.BlockSpec((TM, 512), lambda t: (t % nt, 0)),
        pl.BlockSpec((512, 512), lambda t: (0, 0)),
    ]

    def body(p_ref, dq_ref, dkv_ref, qn_ref, kn_ref, cos_ref, sin_ref, e_ref, dp_ref, dqn_ref, dkn_ref):
        t = pl.program_id(0)
        cos_, sin_, e = cos_ref[...], sin_ref[...], e_ref[...]

        @pl.when(t == 0)
        def _():
            dqn_ref[...] = jnp.zeros_like(dqn_ref)
            dkn_ref[...] = jnp.zeros_like(dkn_ref)

        def unrope(dy, w):
            return dy * cos_[:, :w] + _swap16(dy * sin_[:, :w])

        def norm_bwd(x, g, dy, w):
            r = lax.rsqrt(_seg_mean(x * x, e[:w, :w]) + EPS)
            xh = x * r
            dxh = dy * g
            dx = r * (dxh - xh * _seg_mean(dxh * xh, e[:w, :w]))
            return dx, jnp.sum(dy * xh, axis=0, keepdims=True)

        dqa, dgq = norm_bwd(p_ref[:, QA:QA + 512], qn_ref[...], unrope(dq_ref[:, 0:512] * Q_SCALE, 512), 512)
        dp_ref[:, QA:QA + 512] = dqa.astype(BF16)
        dp_ref[:, QB:QB + 512] = unrope(dq_ref[:, 512:1024] * Q_SCALE, 512).astype(BF16)
        da = dkv_ref[:, 0:256]
        db = dkv_ref[:, 256:512]
        dka = jnp.concatenate([da[:, 0:64], da[:, 128:192]], axis=1)
        dva = jnp.concatenate([da[:, 64:128], da[:, 192:256]], axis=1)
        dkb = jnp.concatenate([db[:, 0:64], db[:, 128:192]], axis=1)
        dvb = jnp.concatenate([db[:, 64:128], db[:, 192:256]], axis=1)
        dka, dgk = norm_bwd(p_ref[:, KA:KA + 128], kn_ref[...], unrope(dka, 128), 128)
        dp_ref[:, KA:KA + 128] = dka.astype(BF16)
        dp_ref[:, VA:VA + 128] = dva.astype(BF16)
        dp_ref[:, KB:KB + 128] = unrope(dkb, 128).astype(BF16)
        dp_ref[:, VB:VB + 128] = dvb.astype(BF16)
        dqn_ref[0:1, :] += dgq
        dkn_ref[0:1, :] += dgk

    outs, couts = _call(
        body, name=name, grid=(m // TM,), in_specs=specs,
        out_specs=[pl.BlockSpec((TM, PROJ_W), lambda t: (t, 0)), pl.BlockSpec((8, 512), lambda t: (0, 0)),
                   pl.BlockSpec((8, 128), lambda t: (0, 0))],
        out_shape=[jax.ShapeDtypeStruct((m, PROJ_W), BF16), jax.ShapeDtypeStruct((8, 512), F32),
                   jax.ShapeDtypeStruct((8, 128), F32)],
        args=[proj, dq, dkv, qn, kn, cos, sin, emat], sem=("arbitrary",), comm=comm)
    return outs if comm is None else (outs, couts)


def _attn_case(t, hg, kv_ref, c_rows, t_rows, fn, keys_first=False):
    wl = TM + 2 * WINDOW
    kvd = lambda a, n: kv_ref[pl.ds(a, n), :]
    dense = hg < 2
    ctx = t == 0

    @pl.when(jnp.logical_and(dense, ctx))
    def _():
        kv = kvd(0, c_rows)
        fn(kv[:, :64], kv[:, 64:], None, False, [(0, c_rows)])

    @pl.when(jnp.logical_and(dense, jnp.logical_not(ctx)))
    def _():
        kv = kvd(0, t_rows)
        fn(kv[:, :64], kv[:, 64:], None, False, [(0, t_rows)])

    @pl.when(jnp.logical_and(jnp.logical_not(dense), ctx))
    def _():
        kv = kvd(0, c_rows)
        fn(kv[:, :64], kv[:, 64:], None, True, [(0, c_rows)])

    @pl.when(jnp.logical_and(jnp.logical_not(dense), jnp.logical_not(ctx)))
    def _():
        start = pl.multiple_of(jnp.minimum(c_rows + (t - 1) * TM - WINDOW, t_rows - wl), 128)
        kv = jnp.concatenate([kvd(0, c_rows), kvd(start, wl)], axis=0)
        shape = (c_rows + wl, TM) if keys_first else (TM, c_rows + wl)
        q_i = lax.broadcasted_iota(jnp.int32, shape, 1 if keys_first else 0)
        k_i = lax.broadcasted_iota(jnp.int32, shape, 0 if keys_first else 1)
        qpos = (t - 1) * TM + q_i
        kpos = start - 2 * c_rows + k_i
        mask = jnp.logical_or(k_i < c_rows, jnp.logical_and(jnp.abs(kpos - qpos) <= WINDOW, kpos >= 0))
        fn(kv[:, :64], kv[:, 64:], mask, True, [(0, c_rows), (start, wl)])


def _head_columns(cols):
    lane = lax.broadcasted_iota(jnp.int32, (TM, 128), 1)
    out = jnp.zeros((TM, 128), F32)
    for g, col in enumerate(cols):
        out = jnp.where(lane == g, col, out)
    return out


def _attn_specs(t_rows):
    nt = t_rows // TM
    q_spec = pl.BlockSpec((None, TM, Q_WIDTH), lambda b, hg, t, s: (b, t, hg))
    kv_spec = pl.BlockSpec((None, t_rows, 128), lambda b, hg, t, s: (b, 0, hg))
    lse_spec = pl.BlockSpec((None, TM, 128), lambda b, hg, t, s: (hg, b * nt + t, 0))
    return q_spec, kv_spec, lse_spec


def _attn_fwd(q_all, kv_all, sink8, c_rows, name, comm=None):
    bl, t_rows, _ = q_all.shape
    q_spec, kv_spec, lse_spec = _attn_specs(t_rows)

    def body(sink_ref, q_ref, kv_ref, o_ref, ot_ref, lse_ref):
        hg, t = pl.program_id(1), pl.program_id(2)

        def fn(k, v, mask, use_sink, spans):
            outs, lses = [], []
            for g in range(GROUP):
                s = lax.dot_general(q_ref[:, g * 64:(g + 1) * 64], k, (((1,), (1,)), ((), ())),
                                    preferred_element_type=F32)
                if mask is not None:
                    s = jnp.where(mask, s, NEG_BIG)
                mx = jnp.max(s, axis=1, keepdims=True)
                if use_sink:
                    sink = sink_ref[jnp.maximum(hg - 2, 0) * GROUP + g]
                    mx = jnp.maximum(mx, sink)
                p = jnp.exp(s - mx)
                l = jnp.sum(p, axis=1, keepdims=True)
                if use_sink:
                    l = l + jnp.exp(sink - mx)
                outs.append(jnp.dot(p.astype(BF16), v, preferred_element_type=F32) * (1.0 / l))
                lses.append(mx + jnp.log(l))
            o = jnp.concatenate(outs, axis=1).astype(BF16)
            o_ref[...] = o
            ot_ref[...] = o.T
            lse_ref[...] = _head_columns(lses)

        _attn_case(t, hg, kv_ref, c_rows, t_rows, fn)

    nt = t_rows // TM
    ot_spec = pl.BlockSpec((Q_WIDTH, TM), lambda b, hg, t, s: (hg, b * nt + t))
    outs, couts = _call(
        body, name=name, grid=(bl, N_HG, nt), in_specs=[q_spec, kv_spec], out_specs=[q_spec, ot_spec, lse_spec],
        out_shape=[jax.ShapeDtypeStruct(q_all.shape, BF16), jax.ShapeDtypeStruct((N_HG * Q_WIDTH, bl * t_rows), BF16),
                   jax.ShapeDtypeStruct((N_HG, bl * t_rows, 128), F32)],
        args=[sink8, q_all, kv_all], prefetch=1, sem=("parallel", "parallel", "arbitrary"), comm=comm)
    return outs if comm is None else (outs, couts)


def _attn_bwd(q_all, kv_all, do, o, lse, sink8, c_rows, name, comm=None):
    bl, t_rows, _ = q_all.shape
    q_spec, kv_spec, lse_spec = _attn_specs(t_rows)
    ds_spec = pl.BlockSpec((None, None, 8, 128), lambda b, hg, t, s: (b, hg, 0, 0))

    def body(sink_ref, q_ref, kv_ref, do_ref, o_ref, lse_ref, dq_ref, dkv_ref, dsk_ref):
        hg, t = pl.program_id(1), pl.program_id(2)

        @pl.when(t == 0)
        def _():
            dkv_ref[...] = jnp.zeros_like(dkv_ref)
            dsk_ref[...] = jnp.zeros_like(dsk_ref)

        lse_rows = lse_ref[...].T
        dd_cols = [jnp.sum(do_ref[:, g * 64:(g + 1) * 64].astype(F32) * o_ref[:, g * 64:(g + 1) * 64].astype(F32),
                           axis=1, keepdims=True) for g in range(GROUP)]
        dd_rows = _head_columns(dd_cols).T

        def fn(k, v, mask, use_sink, spans):
            k_t = k.T
            dq_t, dsinks = [], []
            dk = jnp.zeros(k.shape, F32)
            dv = jnp.zeros(v.shape, F32)
            for g in range(GROUP):
                q = q_ref[:, g * 64:(g + 1) * 64]
                do_g = do_ref[:, g * 64:(g + 1) * 64]
                lse_g, dd_g = lse_rows[g:g + 1, :], dd_rows[g:g + 1, :]
                s = lax.dot_general(k, q, (((1,), (1,)), ((), ())), preferred_element_type=F32)
                if mask is not None:
                    s = jnp.where(mask, s, NEG_BIG)
                pb = jnp.exp(s - lse_g).astype(BF16)
                dp = lax.dot_general(v, do_g, (((1,), (1,)), ((), ())), preferred_element_type=F32)
                ds = (pb.astype(F32) * (dp - dd_g)).astype(BF16)
                dk = dk + jnp.dot(ds, q, preferred_element_type=F32)
                dv = dv + jnp.dot(pb, do_g, preferred_element_type=F32)
                dq_t.append(jnp.dot(k_t, ds, preferred_element_type=F32))
                if use_sink:
                    p_sink = jnp.exp(sink_ref[jnp.maximum(hg - 2, 0) * GROUP + g] - lse_g)
                    dsinks.append(jnp.broadcast_to(-jnp.sum(p_sink * dd_g, axis=1, keepdims=True), (1, 128)))
            dq_ref[...] = jnp.concatenate(dq_t, axis=0).T
            dkv = jnp.concatenate([dk, dv], axis=1)
            off = 0
            for start, size in spans:
                dkv_ref[pl.ds(start, size), :] += dkv[off:off + size]
                off += size
            if use_sink:
                dsk_ref[0:GROUP, :] += jnp.concatenate(dsinks, axis=0)

        _attn_case(t, hg, kv_ref, c_rows, t_rows, fn, keys_first=True)

    outs, couts = _call(
        body, name=name, grid=(bl, N_HG, t_rows // TM), in_specs=[q_spec, kv_spec, q_spec, q_spec, lse_spec],
        out_specs=[q_spec, kv_spec, ds_spec],
        out_shape=[jax.ShapeDtypeStruct(q_all.shape, F32), jax.ShapeDtypeStruct(kv_all.shape, F32),
                   jax.ShapeDtypeStruct((bl, N_HG, 8, 128), F32)],
        args=[sink8, q_all, kv_all, do, o, lse], prefetch=1, sem=("parallel", "parallel", "arbitrary"), comm=comm)
    return outs if comm is None else (outs, couts)


def _silu(x):
    return x * jax.nn.sigmoid(x)


def _ada_fwd(c_rows, w_ada, b_cols, name, comm=None):
    nl, d, w = w_ada.shape
    r = c_rows.shape[0]

    def body(c_ref, w_ref, b_ref, o_ref):
        s = _silu(c_ref[...]).astype(BF16)
        o_ref[...] = jnp.dot(s, w_ref[...].astype(BF16), preferred_element_type=F32) + b_ref[...]

    outs, couts = _call(
        body, name=name, grid=(nl,),
        in_specs=[pl.BlockSpec((r, d), lambda l: (0, 0)), pl.BlockSpec((None, d, w), lambda l: (l, 0, 0)),
                  pl.BlockSpec((None, 1, w), lambda l: (l, 0, 0))],
        out_specs=pl.BlockSpec((None, r, w), lambda l: (l, 0, 0)),
        out_shape=jax.ShapeDtypeStruct((nl, r, w), F32), args=[c_rows, w_ada, b_cols], sem=("parallel",), comm=comm)
    return outs if comm is None else (outs, couts)


def _ada_bwd(c_rows, c_ctx, dmod, w_ada, name):
    nl, d, w = w_ada.shape
    r = c_rows.shape[0]

    def body(c_ref, cc_ref, g_ref, w_ref, dw_ref, dc_ref):
        l = pl.program_id(0)
        s = _silu(c_ref[...]).astype(BF16)
        gm = g_ref[...].astype(BF16)
        dw_ref[...] = lax.dot_general(s, gm, (((0,), (0,)), ((), ())), preferred_element_type=F32)
        ds = lax.dot_general(gm, w_ref[...].astype(BF16), (((1,), (1,)), ((), ())), preferred_element_type=F32)
        rows = lax.broadcasted_iota(jnp.int32, ds.shape, 0)
        dsc = jnp.sum(jnp.where(rows % ADA_ROWS == 2, ds, 0.0), axis=0, keepdims=True)
        x = cc_ref[...]
        sg = jax.nn.sigmoid(x)
        dcc = dsc * (sg * (1.0 + x * (1.0 - sg)))
        out = jnp.concatenate([dcc, jnp.zeros((7, d), F32)], axis=0)

        @pl.when(l == 0)
        def _():
            dc_ref[...] = out

        @pl.when(l != 0)
        def _():
            dc_ref[...] += out

    return pl.pallas_call(
        body, name=name, grid=(nl,),
        in_specs=[pl.BlockSpec((r, d), lambda l: (0, 0)), pl.BlockSpec((1, d), lambda l: (0, 0)),
                  pl.BlockSpec((None, r, w), lambda l: (l, 0, 0)), pl.BlockSpec((None, d, w), lambda l: (l, 0, 0))],
        out_specs=[pl.BlockSpec((None, d, w), lambda l: (l, 0, 0)), pl.BlockSpec((8, d), lambda l: (0, 0))],
        out_shape=[jax.ShapeDtypeStruct((nl, d, w), F32), jax.ShapeDtypeStruct((8, d), F32)],
        compiler_params=_params(("arbitrary",)),
    )(c_rows, c_ctx, dmod, w_ada)


def _adam_math(w, g, m, v):
    m = ADAM_B1 * m + (1.0 - ADAM_B1) * g
    v = ADAM_B2 * v + (1.0 - ADAM_B2) * (g * g)
    m_hat = m / (1.0 - ADAM_B1 ** ADAM_STEP)
    v_hat = v / (1.0 - ADAM_B2 ** ADAM_STEP)
    delta = -ADAM_LR * (m_hat / (jnp.sqrt(v_hat) + ADAM_EPS) + ADAM_WD * w)
    return delta, m, v


def _adamw(w, m, v, g_own, g_recv, name, rows=256):
    nl, r, c = w.shape
    tr = min(rows, r)
    spec = pl.BlockSpec((None, tr, c), lambda l, i: (l, i, 0))
    per_layer = isinstance(g_own, (list, tuple))
    own = list(g_own) if per_layer else [g_own]
    recv = [] if g_recv is None else list(g_recv)
    in_specs = [spec] * 3 + [pl.BlockSpec((tr, c), lambda l, i: (i, 0)) if per_layer else spec] * len(own)
    in_specs += [pl.BlockSpec((N_DEV, tr, c), lambda l, i: (0, i, 0))] * len(recv)

    def body(*refs):
        w_ref, m_ref, v_ref = refs[:3]
        own_refs, recv_refs = refs[3:3 + len(own)], refs[3 + len(own):3 + len(own) + len(recv)]
        go_ref, d_ref, mo_ref, vo_ref = refs[-4:]

        def update(li):
            g = own_refs[li][...].astype(F32)
            if recv:
                for k in range(1, N_DEV):
                    g = g + recv_refs[li][k].astype(F32)
            delta, m_, v_ = _adam_math(w_ref[...], g, m_ref[...], v_ref[...])
            go_ref[...] = g
            d_ref[...] = delta
            mo_ref[...] = m_
            vo_ref[...] = v_

        if per_layer:
            for li in range(nl):
                pl.when(pl.program_id(0) == li)(functools.partial(update, li))
        else:
            update(0)

    return pl.pallas_call(
        body, name=name, grid=(nl, r // tr), in_specs=in_specs, out_specs=[spec] * 4,
        out_shape=[jax.ShapeDtypeStruct(w.shape, F32)] * 4, compiler_params=_params(("parallel", "parallel")),
    )(w, m, v, *own, *recv)


def _small_adamw(w, m, v, g_all, name):
    def body(w_ref, m_ref, v_ref, g_ref, go_ref, d_ref, mo_ref, vo_ref):
        g = g_ref[0]
        for k in range(1, N_DEV):
            g = g + g_ref[k]
        delta, m_, v_ = _adam_math(w_ref[...], g, m_ref[...], v_ref[...])
        go_ref[...] = g
        d_ref[...] = delta
        mo_ref[...] = m_
        vo_ref[...] = v_

    return pl.pallas_call(
        body, name=name, out_shape=[jax.ShapeDtypeStruct(w.shape, F32)] * 4, compiler_params=_params(),
    )(w, m, v, g_all)


SMALL = ("c_ctx", "b_ada", "g_pre_mix", "g_post_mix", "g_pre_mlp", "g_post_mlp", "q_norm", "k_norm", "sink")


def _pack_small(parts):
    flat = jnp.concatenate([parts[n].reshape(-1) for n in SMALL])
    rows = -(-flat.shape[0] // 1024) * 8
    return jnp.pad(flat, (0, rows * 128 - flat.shape[0])).reshape(rows, 128)


def _unpack_small(packed, like):
    flat = packed.reshape(-1)
    out, off = {}, 0
    for n in SMALL:
        size = int(np.prod(like[n].shape))
        out[n] = flat[off:off + size].reshape(like[n].shape)
        off += size
    return out


def kernel(x, c, ctx, c_ctx, w_ada, b_ada, g_pre_mix, g_post_mix, g_pre_mlp, g_post_mlp, w_in, q_norm, k_norm, sink, w_out, w_up, w_down, loss_target, m_c_ctx, m_w_ada, m_b_ada, m_g_pre_mix, m_g_post_mix, m_g_pre_mlp, m_g_post_mlp, m_w_in, m_q_norm, m_k_norm, m_sink, m_w_out, m_w_up, m_w_down, v_c_ctx, v_w_ada, v_b_ada, v_g_pre_mix, v_g_post_mix, v_g_pre_mlp, v_g_post_mlp, v_w_in, v_q_norm, v_k_norm, v_sink, v_w_out, v_w_up, v_w_down):
    bl, s_rows, d = x.shape
    c_rows = ctx.shape[1]
    assert c_rows == TM and s_rows % TM == 0 and bl == 2
    t_rows = c_rows + s_rows
    nt = t_rows // TM
    m_rows = bl * t_rows
    nl = w_in.shape[0]
    ada_w = w_ada.shape[2]
    d_ff = w_up.shape[2] * N_DEV
    me = _my_index()

    shard = lambda w_, l: w_[l].astype(BF16)
    c_pad = jnp.concatenate([c, c_ctx[None, :], jnp.zeros((ADA_ROWS - bl - 1, d), F32)], axis=0)
    c_all, = _comm_only(_Comm([(c_pad, GATHER)]), "gather_c")
    c_all = c_all.reshape(N_DEV * ADA_ROWS, d)
    gathered = {0: {}}

    def layer_weights(l):
        g_ = gathered[l]
        w_out_f = g_["w_out"].reshape(-1, d)
        w_down_f = g_["w_down"].reshape(d_ff, d)
        return dict(
            w_in_t=g_["w_in"].transpose(0, 2, 1).reshape(PROJ_W, d), w_out_f=w_out_f, w_out_t=w_out_f.T,
            w_up_s=g_["w_up"], w_up_t=g_["w_up"].transpose(0, 2, 1).reshape(d_ff, d),
            w_down_f=w_down_f, w_down_t=w_down_f.T)

    big = dict(tm=1536, tn=512)
    deep = dict(tm=1536, tn=256, tk=d_ff)
    wide = dict(tm=1024, tn=256, tk=m_rows)

    b_cols = lax.dynamic_slice(b_ada, (0, me * ada_w), (nl, ada_w))[:, None, :]
    mod_cols, (gathered[0]["w_in"],) = _ada_fwd(c_all, w_ada, b_cols, "ada_fwd",
                                                comm=_Comm([(shard(w_in, 0), GATHER)]))
    mod_slots = mod_cols.reshape(nl, N_DEV, ADA_ROWS, ada_w).transpose(1, 0, 2, 3)
    mod_g, = _comm_only(_Comm([(mod_slots, TO_OWNER)]), "exchange_mod")
    mine = mod_g.transpose(1, 2, 0, 3).reshape(nl, ADA_ROWS, N_MOD, d)
    pad = jnp.zeros((bl, 2, MOD_ROWS - N_MOD, d), F32)
    modsel = [jnp.concatenate([jnp.stack([jnp.broadcast_to(mine[l, bl], (bl, N_MOD, d)), mine[l, :bl]], axis=1), pad],
                              axis=2) for l in range(nl)]

    cos, sin = _rope_tables(t_rows, c_rows)
    emat = _head_mean_matrix()
    row = lambda a: a[None, :]
    qn = [jnp.tile(q_norm[l], 8)[None, :] for l in range(nl)]
    kn = [jnp.tile(k_norm[l], 2)[None, :] for l in range(nl)]

    h = jnp.concatenate([ctx, x], axis=1).reshape(m_rows, d)
    saved = []
    weights_of = {}
    for l in range(nl):
        if l == 0:
            (u, u_t), (gathered[0]["w_out"],) = _norm_mod_fwd(
                h, row(g_pre_mix[l]), modsel[l], 0, 1, nt, f"mix_mod_fwd{l}", comm=_Comm([(shard(w_out, 0), GATHER)]))
        else:
            u, u_t = _norm_mod_fwd(h, row(g_pre_mix[l]), modsel[l], 0, 1, nt, f"mix_mod_fwd{l}")
        w_in_f = gathered[l]["w_in"].transpose(1, 0, 2).reshape(d, PROJ_W)
        proj = _mm(u, w_in_f, name=f"mm_in{l}", tk=d, **big)
        q_all, kv_all = _prep_fwd(proj, qn[l], kn[l], cos, sin, emat, nt, f"prep_fwd{l}")
        (o, o_t, lse), (w_up_g, w_down_g) = _attn_fwd(
            q_all.reshape(bl, t_rows, 1024), kv_all.reshape(bl, t_rows, 512), sink[l], c_rows, f"attn_fwd{l}",
            comm=_Comm([(shard(w_up, l), GATHER), (shard(w_down, l), GATHER)]))
        o = o.reshape(m_rows, 1024)
        gathered[l].update(w_up=w_up_g, w_down=w_down_g)
        wl = weights_of[l] = layer_weights(l)
        mix = _mm(o, wl["w_out_f"], name=f"mm_out{l}", tk=1024, **big)
        h_mid = _gate_fwd(h, mix, row(g_post_mix[l]), modsel[l], 2, nt, f"mix_gate_fwd{l}")
        v_in, v_t = _norm_mod_fwd(h_mid, row(g_pre_mlp[l]), modsel[l], 3, 4, nt, f"mlp_mod_fwd{l}")
        more = l + 1 < nl
        res_up = _mm(v_in, wl["w_up_s"], name=f"mm_up{l}", b_mode="nn_slots", epilogue="relu2", tk=d,
                     comm=_Comm([(shard(w_in, l + 1), GATHER)]) if more else None, **big)
        (r_act, s_act, r_t), nxt_in = res_up if more else (res_up, None)
        res_down = _mm(r_act, wl["w_down_f"], name=f"mm_down{l}",
                       comm=_Comm([(shard(w_out, l + 1), GATHER)]) if more else None, **deep)
        y, nxt_out = res_down if more else (res_down, None)
        if more:
            gathered[l + 1] = dict(w_in=nxt_in[0], w_out=nxt_out[0])
        h_out = _gate_fwd(h_mid, y, row(g_post_mlp[l]), modsel[l], 5, nt, f"mlp_gate_fwd{l}")
        saved.append((h, u_t, proj, q_all, kv_all, o, o_t, lse, mix, h_mid, v_t, r_t, s_act, y))
        h = h_out

    dh, ss = _loss_grad(h, loss_target.reshape(bl * s_rows, d), c_rows, nt, "loss_grad")
    loss = lax.psum(0.5 * ss[0, 0] / d, MESH_AXES)

    small_g = {n: [None] * nl for n in SMALL if n not in ("c_ctx", "b_ada")}
    dmod_rows = []
    slots = {n: [None] * nl for n in ("w_in", "w_out", "w_up", "w_down")}
    recvd = {n: [None] * nl for n in slots}
    send = lambda n, l_: (slots[n][l_], TO_OWNER_XOR)
    for l in reversed(range(nl)):
        h_in, u_t, proj, q_all, kv_all, o, o_t, lse, mix, h_mid, v_t, r_t, s_act, y = saved[l]
        wl = weights_of[l]
        later = l + 1 < nl
        dy, dvec_g2 = _gate_bwd(y, row(g_post_mlp[l]), modsel[l], dh, 5, nt, f"mlp_gate_bwd{l}")
        res = _mm(dy, wl["w_down_t"], name=f"mm_da{l}", epilogue="mul", extra=s_act, out_dtype=BF16, tk=d,
                  comm=_Comm([send("w_out", l + 1)]) if later else None, **big)
        da = res[0] if later else res
        if later:
            recvd["w_out"][l + 1] = res[1][0]
        res = _mm(r_t, dy, name=f"mm_dw_down{l}", out_dtype=BF16,
                  comm=_Comm([send("w_in", l + 1)]) if later else None, **wide)
        dw_down = res[0] if later else res
        if later:
            recvd["w_in"][l + 1] = res[1][0]
        dw_up = _mm(v_t, da, name=f"mm_dw_up{l}", out_mode="slots", out_dtype=BF16, **wide)
        slots["w_down"][l] = dw_down.reshape(N_DEV, -1, d)
        slots["w_up"][l] = dw_up
        dv = _mm(da, wl["w_up_t"], name=f"mm_dv{l}", **deep)
        dh, dvec_m2 = _norm_mod_bwd(h_mid, row(g_pre_mlp[l]), modsel[l], dv, dh, 3, 4, nt, f"mlp_mod_bwd{l}")
        dmix, dvec_g1 = _gate_bwd(mix, row(g_post_mix[l]), modsel[l], dh, 2, nt, f"mix_gate_bwd{l}")
        do = _mm(dmix, wl["w_out_t"], name=f"mm_do{l}", out_dtype=BF16, tk=d, **big)
        dw_out = _mm(o_t, dmix, name=f"mm_dw_out{l}", out_dtype=BF16, **wide)
        slots["w_out"][l] = dw_out.reshape(N_DEV, -1, d)
        (dq, dkv, dsk), (recvd["w_down"][l], recvd["w_up"][l]) = _attn_bwd(
            q_all.reshape(bl, t_rows, 1024), kv_all.reshape(bl, t_rows, 512), do.reshape(bl, t_rows, 1024),
            o.reshape(bl, t_rows, 1024), lse, sink[l], c_rows, f"attn_bwd{l}",
            comm=_Comm([send("w_down", l), send("w_up", l)]))
        last = l == 0
        res = _prep_bwd(proj, dq.reshape(m_rows, 1024), dkv.reshape(m_rows, 512), qn[l], kn[l], cos, sin, emat, nt,
                        f"prep_bwd{l}", comm=_Comm([send("w_out", l)]) if last else None)
        dproj, dqn, dkn = res[0] if last else res
        if last:
            recvd["w_out"][l] = res[1][0]
        dw_in = _mm(u_t, dproj, name=f"mm_dw_in{l}", out_dtype=BF16, **wide)
        slots["w_in"][l] = dw_in.reshape(d, N_DEV, PROJ_W // N_DEV).transpose(1, 0, 2)
        res = _mm(dproj, wl["w_in_t"], name=f"mm_du{l}", tk=PROJ_W,
                  comm=_Comm([send("w_in", l)]) if last else None, **big)
        du = res[0] if last else res
        if last:
            recvd["w_in"][l] = res[1][0]
        dh, dvec_m1 = _norm_mod_bwd(h_in, row(g_pre_mix[l]), modsel[l], du, dh, 0, 1, nt, f"mix_mod_bwd{l}",
                                    latent_only=last)

        small_g["g_pre_mix"][l] = jnp.sum(dvec_m1[:, :, 2], axis=(0, 1))
        small_g["g_post_mix"][l] = jnp.sum(dvec_g1[:, :, 1], axis=(0, 1))
        small_g["g_pre_mlp"][l] = jnp.sum(dvec_m2[:, :, 2], axis=(0, 1))
        small_g["g_post_mlp"][l] = jnp.sum(dvec_g2[:, :, 1], axis=(0, 1))
        small_g["q_norm"][l] = jnp.sum(dqn[0].reshape(8, HEAD_DIM), axis=0)
        small_g["k_norm"][l] = jnp.sum(dkn[0].reshape(2, HEAD_DIM), axis=0)
        small_g["sink"][l] = jnp.sum(dsk[:, 2:, :GROUP, 0], axis=0).reshape(-1)
        dms = jnp.stack([dvec_m1[:, :, 0], dvec_m1[:, :, 1], dvec_g1[:, :, 0],
                         dvec_m2[:, :, 0], dvec_m2[:, :, 1], dvec_g2[:, :, 0]], axis=2)
        rows = jnp.concatenate([dms[:, 1], jnp.sum(dms[:, 0], axis=0)[None]], axis=0)
        dmod_rows.append(jnp.pad(rows.reshape(bl + 1, N_MOD * d), ((0, ADA_ROWS - bl - 1), (0, 0))))
    dmod_rows = dmod_rows[::-1]
    grad_x = dh.reshape(bl, s_rows, d)

    dmod_slots = jnp.stack(dmod_rows).reshape(nl, ADA_ROWS, N_DEV, ada_w).transpose(2, 0, 1, 3)
    dmod_g, = _comm_only(_Comm([(dmod_slots, TO_OWNER)]), "exchange_dmod")
    dmod_mine = dmod_g.transpose(1, 0, 2, 3).reshape(nl, N_DEV * ADA_ROWS, ada_w)
    dw_ada, dcc = _ada_bwd(c_all, c_ctx[None, :], dmod_mine, w_ada, "ada_bwd")

    parts = {n: jnp.stack(small_g[n]) for n in small_g}
    parts["c_ctx"] = dcc[0]
    parts["b_ada"] = jnp.stack([jnp.sum(r_[: bl + 1], axis=0) for r_ in dmod_rows])
    weights = dict(c_ctx=c_ctx, b_ada=b_ada, g_pre_mix=g_pre_mix, g_post_mix=g_post_mix, g_pre_mlp=g_pre_mlp,
                   g_post_mlp=g_post_mlp, q_norm=q_norm, k_norm=k_norm, sink=sink)
    moms = dict(c_ctx=m_c_ctx, b_ada=m_b_ada, g_pre_mix=m_g_pre_mix, g_post_mix=m_g_post_mix, g_pre_mlp=m_g_pre_mlp,
                g_post_mlp=m_g_post_mlp, q_norm=m_q_norm, k_norm=m_k_norm, sink=m_sink)
    vels = dict(c_ctx=v_c_ctx, b_ada=v_b_ada, g_pre_mix=v_g_pre_mix, g_post_mix=v_g_post_mix, g_pre_mlp=v_g_pre_mlp,
                g_post_mlp=v_g_post_mlp, q_norm=v_q_norm, k_norm=v_k_norm, sink=v_sink)
    small_all, = _comm_only(_Comm([(_pack_small(parts), GATHER)]), "gather_small")
    s_out = _small_adamw(_pack_small(weights), _pack_small(moms), _pack_small(vels), small_all, "adamw_small")
    s_g, s_d, s_m, s_v = [_unpack_small(a, weights) for a in s_out]

    res = {}
    for n, w_, m_, v_ in (("w_in", w_in, m_w_in, v_w_in), ("w_out", w_out, m_w_out, v_w_out),
                          ("w_up", w_up, m_w_up, v_w_up), ("w_down", w_down, m_w_down, v_w_down)):
        own = [lax.dynamic_index_in_dim(slots[n][l], me, axis=0, keepdims=False) for l in range(nl)]
        res[n] = _adamw(w_, m_, v_, own, recvd[n], f"adamw_{n}")
    res["w_ada"] = _adamw(w_ada, m_w_ada, v_w_ada, dw_ada, None, "adamw_w_ada")

    order = ("c_ctx", "w_ada", "b_ada", "g_pre_mix", "g_post_mix", "g_pre_mlp", "g_post_mlp", "w_in", "q_norm",
             "k_norm", "sink", "w_out", "w_up", "w_down")
    outs = [loss, grad_x]
    for i, small in enumerate((s_g, s_d, s_m, s_v)):
        outs += [small[n] if n in small else res[n][i] for n in order]
    return tuple(outs)
```

```python
import functools

import jax
import jax.numpy as jnp
import numpy as np
from jax import lax
from jax.experimental import pallas as pl
from jax.experimental.pallas import tpu as pltpu

F32 = jnp.float32
BF16 = jnp.bfloat16

HEAD_DIM = 64
GROUP = 4
N_HG = 4
Q_WIDTH = GROUP * HEAD_DIM
WINDOW = 128
GRID_W = 64
ROPE_THETA = 10000.0
EPS = 1e-6
NEG_BIG = -1e30
N_MOD = 6
MOD_ROWS = 8
TM = 256
N_DEV = 8
ADA_ROWS = 8
VMEM_LIMIT = 56 * 1024 * 1024

ADAM_LR = 0.001
ADAM_B1 = 0.9
ADAM_B2 = 0.999
ADAM_EPS = 1e-08
ADAM_WD = 0.01
ADAM_STEP = 10

MESH_AXES = ("x", "y", "c")


def _params(sem=None):
    kw = dict(vmem_limit_bytes=VMEM_LIMIT)
    if sem is not None:
        kw["dimension_semantics"] = sem
    return pltpu.CompilerParams(**kw)


def _my_index():
    return 4 * lax.axis_index("x") + 2 * lax.axis_index("y") + lax.axis_index("c")


def _peer(k):
    x, y, c = lax.axis_index("x"), lax.axis_index("y"), lax.axis_index("c")
    kx, ky, kc = (k >> 2) & 1, (k >> 1) & 1, k & 1
    px = (1 - x) if kx else x
    py = (1 - y) if ky else y
    pc = (1 - c) if kc else c
    return (px, py, pc), 4 * px + 2 * py + pc


GATHER, TO_OWNER, TO_OWNER_XOR = "gather", "to_owner", "to_owner_xor"


class _Comm:
    def __init__(self, items):
        self.items = list(items)
        self.arrays = [a for a, _ in self.items]

    def out_shapes(self):
        return [jax.ShapeDtypeStruct(((N_DEV,) + a.shape) if kind == GATHER else a.shape, a.dtype)
                for a, kind in self.items]

    def sem_shapes(self):
        n = len(self.items) * N_DEV
        return [pltpu.SemaphoreType.DMA((n,)), pltpu.SemaphoreType.DMA((n,))]

    def _copies(self, in_refs, out_refs, send_sems, recv_sems):
        me = _my_index()
        local, remote = [], []
        for i, ((_, kind), x_ref, o_ref) in enumerate(zip(self.items, in_refs, out_refs)):
            base = i * N_DEV
            own_src = x_ref if kind == GATHER else x_ref.at[me]
            own_dst = o_ref.at[0] if kind == TO_OWNER_XOR else o_ref.at[me]
            local.append(pltpu.make_async_copy(own_src, own_dst, send_sems.at[base]))
            for k in range(1, N_DEV):
                peer, pidx = _peer(k)
                remote.append(pltpu.make_async_remote_copy(
                    src_ref=x_ref if kind == GATHER else x_ref.at[pidx],
                    dst_ref=o_ref.at[k] if kind == TO_OWNER_XOR else o_ref.at[me],
                    send_sem=send_sems.at[base + k], recv_sem=recv_sems.at[base + k],
                    device_id=peer, device_id_type=pl.DeviceIdType.MESH))
        return local, remote

    def start(self, in_refs, out_refs, send_sems, recv_sems):
        local, remote = self._copies(in_refs, out_refs, send_sems, recv_sems)
        for cp in local + remote:
            cp.start()

    def wait(self, in_refs, out_refs, send_sems, recv_sems):
        local, remote = self._copies(in_refs, out_refs, send_sems, recv_sems)
        for cp in remote:
            cp.wait_recv()
        for cp in remote:
            cp.wait_send()
        for cp in local:
            cp.wait()


def _call(body, *, name, grid, in_specs, out_specs, out_shape, args, scratch_shapes=(), prefetch=0, sem=None,
          comm=None):
    single = not isinstance(out_shape, (list, tuple))
    out_shape = [out_shape] if single else list(out_shape)
    out_specs = [out_specs] if single else list(out_specs)
    in_specs, scratch_shapes, args = list(in_specs), list(scratch_shapes), list(args)
    n_in, n_out = len(in_specs), len(out_shape)
    if comm is not None:
        nc = len(comm.arrays)
        hbm = pl.BlockSpec(memory_space=pl.ANY)
        inner = body

        def body(*refs):
            pre, r = refs[:prefetch], refs[prefetch:]
            ins, cin = r[:n_in], r[n_in:n_in + nc]
            outs, cout = r[n_in + nc:n_in + nc + n_out], r[n_in + nc + n_out:n_in + 2 * nc + n_out]
            scr, sems = r[n_in + 2 * nc + n_out:len(r) - 2], r[len(r) - 2:]
            ids = [pl.program_id(i) for i in range(len(grid))]

            def when(flags, fn):
                if flags:
                    pl.when(functools.reduce(jnp.logical_and, flags))(fn)
                else:
                    fn()

            when([i == 0 for i in ids], lambda: comm.start(cin, cout, *sems))
            inner(*pre, *ins, *outs, *scr)
            when([i == n - 1 for i, n in zip(ids, grid)], lambda: comm.wait(cin, cout, *sems))

        in_specs += [hbm] * nc
        out_specs += [hbm] * nc
        out_shape += comm.out_shapes()
        scratch_shapes += comm.sem_shapes()
        args += comm.arrays
        sem = ("arbitrary",) * len(grid)
    kw = dict(name=name, out_shape=out_shape, compiler_params=_params(sem if grid else None))
    if prefetch:
        kw["grid_spec"] = pltpu.PrefetchScalarGridSpec(
            num_scalar_prefetch=prefetch, grid=grid, in_specs=in_specs, out_specs=out_specs,
            scratch_shapes=scratch_shapes)
    else:
        kw.update(in_specs=in_specs, out_specs=out_specs, scratch_shapes=scratch_shapes)
        if grid:
            kw["grid"] = grid
    res = list(pl.pallas_call(body, **kw)(*args))
    outs = res[:n_out]
    return (outs[0] if single else outs), res[n_out:]


def _comm_only(comm, name):
    return _call(lambda: None, name=name, grid=(), in_specs=[], out_specs=[], out_shape=[], args=[], comm=comm)[1]


def _mm(a, b, *, name, ta=False, b_mode="nn", out_mode="plain", out_dtype=F32, tm=512, tn=512, tk=512,
        epilogue=None, extra=None, comm=None):
    if ta:
        kdim, m = a.shape
    else:
        m, kdim = a.shape
    if b_mode == "nn":
        n = b.shape[1]
    elif b_mode == "nt":
        n = b.shape[0]
    elif b_mode == "nn_slots":
        n = b.shape[0] * b.shape[2]
        tn = b.shape[2]
    else:
        n = b.shape[1]
        tk = b.shape[2]
    if out_mode == "slots":
        tn = n // N_DEV
    tm, tn, tk = min(tm, m), min(tn, n), min(tk, kdim)
    assert m % tm == 0 and n % tn == 0 and kdim % tk == 0, (name, m, n, kdim, tm, tn, tk)
    nk = kdim // tk

    a_spec = pl.BlockSpec((tk, tm), lambda i, j, k: (k, i)) if ta else pl.BlockSpec((tm, tk), lambda i, j, k: (i, k))
    if b_mode == "nn":
        b_spec = pl.BlockSpec((tk, tn), lambda i, j, k: (k, j))
    elif b_mode == "nt":
        b_spec = pl.BlockSpec((tn, tk), lambda i, j, k: (j, k))
    elif b_mode == "nn_slots":
        b_spec = pl.BlockSpec((None, tk, tn), lambda i, j, k: (j, k, 0))
    else:
        b_spec = pl.BlockSpec((None, tn, tk), lambda i, j, k: (k, j, 0))
    tb = b_mode in ("nt", "nt_slots")
    if out_mode == "plain":
        o_shape, o_spec = (m, n), pl.BlockSpec((tm, tn), lambda i, j, k: (i, j))
    else:
        o_shape, o_spec = (N_DEV, m, tn), pl.BlockSpec((None, tm, tn), lambda i, j, k: (j, i, 0))
    dims = (((0 if ta else 1,), (1 if tb else 0,)), ((), ()))

    in_specs = [a_spec, b_spec]
    args = [a, b]
    if epilogue == "relu2_bwd":
        in_specs.append(pl.BlockSpec((tm, tn), lambda i, j, k: (i, j)))
        args.append(extra)
    if epilogue == "relu2":
        out_shape = [jax.ShapeDtypeStruct(o_shape, BF16), jax.ShapeDtypeStruct((n, m), BF16)]
        out_specs = [o_spec, pl.BlockSpec((tn, tm), lambda i, j, k: (j, i))]
    else:
        out_shape = jax.ShapeDtypeStruct(o_shape, out_dtype)
        out_specs = o_spec

    def finish(refs, acc):
        if epilogue == "relu2":
            r = jnp.maximum(acc, 0.0)
            r2 = (r * r).astype(BF16)
            refs[2][...] = r2
            refs[3][...] = r2.T
        elif epilogue == "relu2_bwd":
            refs[3][...] = (acc * (2.0 * jnp.sqrt(refs[2][...].astype(F32)))).astype(out_dtype)
        else:
            refs[2][...] = acc.astype(out_dtype)

    def body(*refs):
        part = lax.dot_general(refs[0][...], refs[1][...], dims, preferred_element_type=F32)
        if nk == 1:
            finish(refs, part)
            return
        acc_ref = refs[-1]
        k = pl.program_id(2)

        @pl.when(k == 0)
        def _():
            acc_ref[...] = part

        @pl.when(jnp.logical_and(k > 0, k < nk - 1))
        def _():
            acc_ref[...] += part

        @pl.when(k == nk - 1)
        def _():
            finish(refs, acc_ref[...] + part)

    outs, couts = _call(
        body, name=name, grid=(m // tm, n // tn, nk), in_specs=in_specs, out_specs=out_specs, out_shape=out_shape,
        args=args, scratch_shapes=[] if nk == 1 else [pltpu.VMEM((tm, tn), F32)],
        sem=("parallel", "parallel", "arbitrary"), comm=comm)
    return outs if comm is None else (outs, couts)


def _row_specs(d, nt):
    row = pl.BlockSpec((TM, d), lambda t: (t, 0))
    vec = pl.BlockSpec((1, d), lambda t: (0, 0))
    mod = pl.BlockSpec((None, None, MOD_ROWS, d), lambda t: (t // nt, jnp.minimum(t % nt, 1), 0, 0))
    return row, vec, mod


def _rms(x):
    r = lax.rsqrt(jnp.mean(x * x, axis=1, keepdims=True) + EPS)
    return r, x * r


def _norm_mod_fwd(h, g, modsel, i_sh, i_sc, nt, name, comm=None):
    m, d = h.shape
    row, vec, mod = _row_specs(d, nt)

    def body(h_ref, g_ref, ms_ref, u_ref, ut_ref):
        _, xh = _rms(h_ref[...])
        ms = ms_ref[...]
        u = (xh * g_ref[...] * (1.0 + ms[i_sc:i_sc + 1]) + ms[i_sh:i_sh + 1]).astype(BF16)
        u_ref[...] = u
        ut_ref[...] = u.T

    outs, couts = _call(
        body, name=name, grid=(m // TM,), in_specs=[row, vec, mod],
        out_specs=[row, pl.BlockSpec((d, TM), lambda t: (0, t))],
        out_shape=[jax.ShapeDtypeStruct((m, d), BF16), jax.ShapeDtypeStruct((d, m), BF16)],
        args=[h, g, modsel], sem=("parallel",), comm=comm)
    return outs if comm is None else (outs, couts)


def _acc_rows(t, nt, dvec_ref, rows):
    first = (t % nt) <= 1

    @pl.when(first)
    def _():
        dvec_ref[...] = rows

    @pl.when(jnp.logical_not(first))
    def _():
        dvec_ref[...] += rows


def _norm_mod_bwd(h, g, modsel, du, dh_in, i_sh, i_sc, nt, name, comm=None, latent_only=False):
    m, d = h.shape
    row, vec, mod = _row_specs(d, nt)
    dh_rows, dh_spec = m, row
    if latent_only:
        dh_rows = m // nt * (nt - 1)
        dh_spec = pl.BlockSpec((TM, d), lambda t: ((t // nt) * (nt - 1) + jnp.maximum(t % nt, 1) - 1, 0))

    def body(h_ref, g_ref, ms_ref, du_ref, dhi_ref, dh_ref, dvec_ref):
        t = pl.program_id(0)
        r, xh = _rms(h_ref[...])
        g_ = g_ref[...]
        ms = ms_ref[...]
        du_ = du_ref[...]
        y = xh * g_
        dy = du_ * (1.0 + ms[i_sc:i_sc + 1])
        dxh = dy * g_
        dx = r * (dxh - xh * jnp.mean(dxh * xh, axis=1, keepdims=True))
        dh_ref[...] = dhi_ref[...] + dx
        rows = jnp.concatenate([
            jnp.sum(du_, axis=0, keepdims=True), jnp.sum(du_ * y, axis=0, keepdims=True),
            jnp.sum(dy * xh, axis=0, keepdims=True), jnp.zeros((MOD_ROWS - 3, d), F32)], axis=0)
        _acc_rows(t, nt, dvec_ref, rows)

    outs, couts = _call(
        body, name=name, grid=(m // TM,), in_specs=[row, vec, mod, row, row], out_specs=[dh_spec, mod],
        out_shape=[jax.ShapeDtypeStruct((dh_rows, d), F32), jax.ShapeDtypeStruct(modsel.shape, F32)],
        args=[h, g, modsel, du, dh_in], sem=("arbitrary",), comm=comm)
    return outs if comm is None else (outs, couts)


def _gate_fwd(h, z, g, modsel, i_g, nt, name):
    m, d = h.shape
    row, vec, mod = _row_specs(d, nt)

    def body(h_ref, z_ref, g_ref, ms_ref, o_ref):
        _, xh = _rms(z_ref[...])
        ms = ms_ref[...]
        o_ref[...] = h_ref[...] + ms[i_g:i_g + 1] * (xh * g_ref[...])

    return pl.pallas_call(
        body, name=name, grid=(m // TM,), in_specs=[row, row, vec, mod], out_specs=row,
        out_shape=jax.ShapeDtypeStruct((m, d), F32), compiler_params=_params(("parallel",)),
    )(h, z, g, modsel)


def _gate_bwd(z, g, modsel, dh, i_g, nt, name):
    m, d = z.shape
    row, vec, mod = _row_specs(d, nt)

    def body(z_ref, g_ref, ms_ref, dh_ref, dz_ref, dvec_ref):
        t = pl.program_id(0)
        r, xh = _rms(z_ref[...])
        g_ = g_ref[...]
        ms = ms_ref[...]
        dh_ = dh_ref[...]
        dy = dh_ * ms[i_g:i_g + 1]
        dxh = dy * g_
        dz_ref[...] = (r * (dxh - xh * jnp.mean(dxh * xh, axis=1, keepdims=True))).astype(BF16)
        rows = jnp.concatenate([
            jnp.sum(dh_ * (xh * g_), axis=0, keepdims=True), jnp.sum(dy * xh, axis=0, keepdims=True),
            jnp.zeros((MOD_ROWS - 2, d), F32)], axis=0)
        _acc_rows(t, nt, dvec_ref, rows)

    return pl.pallas_call(
        body, name=name, grid=(m // TM,), in_specs=[row, vec, mod, row], out_specs=[row, mod],
        out_shape=[jax.ShapeDtypeStruct((m, d), BF16), jax.ShapeDtypeStruct(modsel.shape, F32)],
        compiler_params=_params(("arbitrary",)),
    )(z, g, modsel, dh)


def _loss_grad(h, target, c_rows, nt, name):
    m, d = h.shape
    row = pl.BlockSpec((TM, d), lambda t: (t, 0))
    ntl = nt - 1
    tgt = pl.BlockSpec((TM, d), lambda t: ((t // nt) * ntl + jnp.maximum(t % nt, 1) - 1, 0))
    acc = pl.BlockSpec((8, 128), lambda t: (0, 0))

    def body(h_ref, t_ref, dh_ref, ss_ref):
        t = pl.program_id(0)

        @pl.when(t == 0)
        def _():
            ss_ref[...] = jnp.zeros_like(ss_ref)

        @pl.when(t % nt == 0)
        def _():
            dh_ref[...] = jnp.zeros_like(dh_ref)

        @pl.when(t % nt != 0)
        def _():
            e = h_ref[...] - t_ref[...]
            dh_ref[...] = e * (1.0 / d)
            ss_ref[...] += jnp.sum(e * e)

    return pl.pallas_call(
        body, name=name, grid=(m // TM,), in_specs=[row, tgt], out_specs=[row, acc],
        out_shape=[jax.ShapeDtypeStruct((m, d), F32), jax.ShapeDtypeStruct((8, 128), F32)],
        compiler_params=_params(("arbitrary",)),
    )(h, target)


QA, KA, VA, QB, KB, VB = 0, 512, 640, 768, 1280, 1408
PROJ_W = 1536
Q_SCALE = HEAD_DIM ** -0.5


def _swap16(x):
    lane = lax.broadcasted_iota(jnp.int32, x.shape, 1)
    n = x.shape[1]
    return jnp.where((lane % 32) < 16, pltpu.roll(x, n - 16, 1), pltpu.roll(x, 16, 1))


def _seg_mean(x, e):
    hi = x.astype(BF16)
    lo = (x - hi.astype(F32)).astype(BF16)
    return jnp.dot(hi, e, preferred_element_type=F32) + jnp.dot(lo, e, preferred_element_type=F32)


def _rope_tables(t_rows, c_rows):
    s = t_rows - c_rows
    row_ids = jnp.repeat(jnp.arange(s // GRID_W, dtype=jnp.int32), GRID_W).astype(F32)
    col_ids = jnp.tile(jnp.arange(GRID_W, dtype=jnp.int32), s // GRID_W).astype(F32)
    axis_dim = HEAD_DIM // 2
    inv = ROPE_THETA ** (-jnp.arange(0, axis_dim, 2, dtype=F32) / axis_dim)
    ang_r = row_ids[:, None] * inv[None, :]
    ang_c = col_ids[:, None] * inv[None, :]
    cos = jnp.concatenate([jnp.cos(ang_r), jnp.cos(ang_r), jnp.cos(ang_c), jnp.cos(ang_c)], axis=1)
    sin = jnp.concatenate([-jnp.sin(ang_r), jnp.sin(ang_r), -jnp.sin(ang_c), jnp.sin(ang_c)], axis=1)
    cos = jnp.concatenate([jnp.ones((c_rows, HEAD_DIM), F32), cos], axis=0)
    sin = jnp.concatenate([jnp.zeros((c_rows, HEAD_DIM), F32), sin], axis=0)
    return jnp.tile(cos, (1, 8)), jnp.tile(sin, (1, 8))


def _head_mean_matrix():
    i = np.arange(512)
    return jnp.asarray((i[:, None] // HEAD_DIM == i[None, :] // HEAD_DIM).astype(np.float32) / HEAD_DIM, dtype=BF16)


def _interleave_kv(k, v):
    return jnp.concatenate([k[:, :64], v[:, :64], k[:, 64:], v[:, 64:]], axis=1)


def _prep_fwd(proj, qn, kn, cos, sin, emat, nt, name):
    m = proj.shape[0]
    specs = [
        pl.BlockSpec((TM, PROJ_W), lambda t: (t, 0)),
        pl.BlockSpec((1, 512), lambda t: (0, 0)), pl.BlockSpec((1, 128), lambda t: (0, 0)),
        pl.BlockSpec((TM, 512), lambda t: (t % nt, 0)), pl.BlockSpec((TM, 512), lambda t: (t % nt, 0)),
        pl.BlockSpec((512, 512), lambda t: (0, 0)),
    ]

    def body(p_ref, qn_ref, kn_ref, cos_ref, sin_ref, e_ref, q_ref, kv_ref):
        cos_, sin_, e = cos_ref[...], sin_ref[...], e_ref[...]

        def rope(x, w):
            return x * cos_[:, :w] + _swap16(x) * sin_[:, :w]

        def norm(x, g, w):
            return x * lax.rsqrt(_seg_mean(x * x, e[:w, :w]) + EPS) * g

        qa = rope(norm(p_ref[:, QA:QA + 512], qn_ref[...], 512), 512)
        qb = rope(p_ref[:, QB:QB + 512], 512)
        q_ref[:, 0:512] = (qa * Q_SCALE).astype(BF16)
        q_ref[:, 512:1024] = (qb * Q_SCALE).astype(BF16)
        ka = rope(norm(p_ref[:, KA:KA + 128], kn_ref[...], 128), 128)
        kb = rope(p_ref[:, KB:KB + 128], 128)
        kv_ref[:, 0:256] = _interleave_kv(ka, p_ref[:, VA:VA + 128]).astype(BF16)
        kv_ref[:, 256:512] = _interleave_kv(kb, p_ref[:, VB:VB + 128]).astype(BF16)

    return pl.pallas_call(
        body, name=name, grid=(m // TM,), in_specs=specs,
        out_specs=[pl.BlockSpec((TM, 1024), lambda t: (t, 0)), pl.BlockSpec((TM, 512), lambda t: (t, 0))],
        out_shape=[jax.ShapeDtypeStruct((m, 1024), BF16), jax.ShapeDtypeStruct((m, 512), BF16)],
        compiler_params=_params(("parallel",)),
    )(proj, qn, kn, cos, sin, emat)


def _prep_bwd(proj, dq, dkv, qn, kn, cos, sin, emat, nt, name, comm=None):
    m = proj.shape[0]
    specs = [
        pl.BlockSpec((TM, PROJ_W), lambda t: (t, 0)),
        pl.BlockSpec((TM, 1024), lambda t: (t, 0)), pl.BlockSpec((TM, 512), lambda t: (t, 0)),
        pl.BlockSpec((1, 512), lambda t: (0, 0)), pl.BlockSpec((1, 128), lambda t: (0, 0)),
        pl.BlockSpec((TM, 512), lambda t: (t % nt, 0)), pl.BlockSpec((TM, 512), lambda t: (t % nt, 0)),
        pl.BlockSpec((512, 512), lambda t: (0, 0)),
    ]

    def body(p_ref, dq_ref, dkv_ref, qn_ref, kn_ref, cos_ref, sin_ref, e_ref, dp_ref, dqn_ref, dkn_ref):
        t = pl.program_id(0)
        cos_, sin_, e = cos_ref[...], sin_ref[...], e_ref[...]

        @pl.when(t == 0)
        def _():
            dqn_ref[...] = jnp.zeros_like(dqn_ref)
            dkn_ref[...] = jnp.zeros_like(dkn_ref)

        def unrope(dy, w):
            return dy * cos_[:, :w] + _swap16(dy * sin_[:, :w])

        def norm_bwd(x, g, dy, w):
            r = lax.rsqrt(_seg_mean(x * x, e[:w, :w]) + EPS)
            xh = x * r
            dxh = dy * g
            dx = r * (dxh - xh * _seg_mean(dxh * xh, e[:w, :w]))
            return dx, jnp.sum(dy * xh, axis=0, keepdims=True)

        dqa, dgq = norm_bwd(p_ref[:, QA:QA + 512], qn_ref[...], unrope(dq_ref[:, 0:512] * Q_SCALE, 512), 512)
        dp_ref[:, QA:QA + 512] = dqa.astype(BF16)
        dp_ref[:, QB:QB + 512] = unrope(dq_ref[:, 512:1024] * Q_SCALE, 512).astype(BF16)
        da = dkv_ref[:, 0:256]
        db = dkv_ref[:, 256:512]
        dka = jnp.concatenate([da[:, 0:64], da[:, 128:192]], axis=1)
        dva = jnp.concatenate([da[:, 64:128], da[:, 192:256]], axis=1)
        dkb = jnp.concatenate([db[:, 0:64], db[:, 128:192]], axis=1)
        dvb = jnp.concatenate([db[:, 64:128], db[:, 192:256]], axis=1)
        dka, dgk = norm_bwd(p_ref[:, KA:KA + 128], kn_ref[...], unrope(dka, 128), 128)
        dp_ref[:, KA:KA + 128] = dka.astype(BF16)
        dp_ref[:, VA:VA + 128] = dva.astype(BF16)
        dp_ref[:, KB:KB + 128] = unrope(dkb, 128).astype(BF16)
        dp_ref[:, VB:VB + 128] = dvb.astype(BF16)
        dqn_ref[0:1, :] += dgq
        dkn_ref[0:1, :] += dgk

    outs, couts = _call(
        body, name=name, grid=(m // TM,), in_specs=specs,
        out_specs=[pl.BlockSpec((TM, PROJ_W), lambda t: (t, 0)), pl.BlockSpec((8, 512), lambda t: (0, 0)),
                   pl.BlockSpec((8, 128), lambda t: (0, 0))],
        out_shape=[jax.ShapeDtypeStruct((m, PROJ_W), BF16), jax.ShapeDtypeStruct((8, 512), F32),
                   jax.ShapeDtypeStruct((8, 128), F32)],
        args=[proj, dq, dkv, qn, kn, cos, sin, emat], sem=("arbitrary",), comm=comm)
    return outs if comm is None else (outs, couts)


def _attn_case(t, hg, kv_ref, c_rows, t_rows, fn, keys_first=False):
    wl = TM + 2 * WINDOW
    kvd = lambda a, n: kv_ref[pl.ds(a, n), :]
    dense = hg < 2
    ctx = t == 0

    @pl.when(jnp.logical_and(dense, ctx))
    def _():
        kv = kvd(0, c_rows)
        fn(kv[:, :64], kv[:, 64:], None, False, [(0, c_rows)])

    @pl.when(jnp.logical_and(dense, jnp.logical_not(ctx)))
    def _():
        kv = kvd(0, t_rows)
        fn(kv[:, :64], kv[:, 64:], None, False, [(0, t_rows)])

    @pl.when(jnp.logical_and(jnp.logical_not(dense), ctx))
    def _():
        kv = kvd(0, c_rows)
        fn(kv[:, :64], kv[:, 64:], None, True, [(0, c_rows)])

    @pl.when(jnp.logical_and(jnp.logical_not(dense), jnp.logical_not(ctx)))
    def _():
        start = pl.multiple_of(jnp.minimum(c_rows + (t - 1) * TM - WINDOW, t_rows - wl), 128)
        kv = jnp.concatenate([kvd(0, c_rows), kvd(start, wl)], axis=0)
        shape = (c_rows + wl, TM) if keys_first else (TM, c_rows + wl)
        q_i = lax.broadcasted_iota(jnp.int32, shape, 1 if keys_first else 0)
        k_i = lax.broadcasted_iota(jnp.int32, shape, 0 if keys_first else 1)
        qpos = (t - 1) * TM + q_i
        kpos = start - 2 * c_rows + k_i
        mask = jnp.logical_or(k_i < c_rows, jnp.logical_and(jnp.abs(kpos - qpos) <= WINDOW, kpos >= 0))
        fn(kv[:, :64], kv[:, 64:], mask, True, [(0, c_rows), (start, wl)])


def _head_columns(cols):
    lane = lax.broadcasted_iota(jnp.int32, (TM, 128), 1)
    out = jnp.zeros((TM, 128), F32)
    for g, col in enumerate(cols):
        out = jnp.where(lane == g, col, out)
    return out


def _attn_specs(t_rows):
    nt = t_rows // TM
    q_spec = pl.BlockSpec((None, TM, Q_WIDTH), lambda b, hg, t, s: (b, t, hg))
    kv_spec = pl.BlockSpec((None, t_rows, 128), lambda b, hg, t, s: (b, 0, hg))
    lse_spec = pl.BlockSpec((None, TM, 128), lambda b, hg, t, s: (hg, b * nt + t, 0))
    return q_spec, kv_spec, lse_spec


def _attn_fwd(q_all, kv_all, sink8, c_rows, name, comm=None):
    bl, t_rows, _ = q_all.shape
    q_spec, kv_spec, lse_spec = _attn_specs(t_rows)

    def body(sink_ref, q_ref, kv_ref, o_ref, ot_ref, lse_ref):
        hg, t = pl.program_id(1), pl.program_id(2)

        def fn(k, v, mask, use_sink, spans):
            outs, lses = [], []
            v_one = jnp.concatenate([v, jnp.ones(v.shape, BF16)], axis=1)
            def scores(g):
                s = lax.dot_general(q_ref[:, g * 64:(g + 1) * 64], k, (((1,), (1,)), ((), ())),
                                    preferred_element_type=F32)
                return s if mask is None else jnp.where(mask, s, NEG_BIG)

            s_next = scores(0)
            for g in range(GROUP):
                s = s_next
                if g + 1 < GROUP:
                    s_next = scores(g + 1)
                mx = jnp.max(s, axis=1, keepdims=True)
                if use_sink:
                    sink = sink_ref[jnp.maximum(hg - 2, 0) * GROUP + g]
                    mx = jnp.maximum(mx, sink)
                pv = jnp.dot(jnp.exp(s - mx).astype(BF16), v_one, preferred_element_type=F32)
                l = pv[:, 64:65]
                if use_sink:
                    l = l + jnp.exp(sink - mx)
                outs.append(pv[:, :64] * (1.0 / l))
                lses.append(mx + jnp.log(l))
            o = jnp.concatenate(outs, axis=1).astype(BF16)
            o_ref[...] = o
            ot_ref[...] = o.T
            lse_ref[...] = _head_columns(lses)

        _attn_case(t, hg, kv_ref, c_rows, t_rows, fn)

    nt = t_rows // TM
    ot_spec = pl.BlockSpec((Q_WIDTH, TM), lambda b, hg, t, s: (hg, b * nt + t))
    outs, couts = _call(
        body, name=name, grid=(bl, N_HG, nt), in_specs=[q_spec, kv_spec], out_specs=[q_spec, ot_spec, lse_spec],
        out_shape=[jax.ShapeDtypeStruct(q_all.shape, BF16), jax.ShapeDtypeStruct((N_HG * Q_WIDTH, bl * t_rows), BF16),
                   jax.ShapeDtypeStruct((N_HG, bl * t_rows, 128), F32)],
        args=[sink8, q_all, kv_all], prefetch=1, sem=("parallel", "parallel", "arbitrary"), comm=comm)
    return outs if comm is None else (outs, couts)


def _attn_bwd(q_all, kv_all, do, o, lse, sink8, c_rows, name, comm=None):
    bl, t_rows, _ = q_all.shape
    q_spec, kv_spec, lse_spec = _attn_specs(t_rows)
    ds_spec = pl.BlockSpec((None, None, 8, 128), lambda b, hg, t, s: (b, hg, 0, 0))

    def body(sink_ref, q_ref, kv_ref, do_ref, o_ref, lse_ref, dq_ref, dkv_ref, dsk_ref):
        hg, t = pl.program_id(1), pl.program_id(2)

        @pl.when(t == 0)
        def _():
            dkv_ref[...] = jnp.zeros_like(dkv_ref)
            dsk_ref[...] = jnp.zeros_like(dsk_ref)

        lse_rows = lse_ref[...].T
        dd_cols = [jnp.sum(do_ref[:, g * 64:(g + 1) * 64].astype(F32) * o_ref[:, g * 64:(g + 1) * 64].astype(F32),
                           axis=1, keepdims=True) for g in range(GROUP)]
        dd_rows = _head_columns(dd_cols).T

        def fn(k, v, mask, use_sink, spans):
            k_t = k.T
            dq_t, dsinks = [], []
            dk = jnp.zeros(k.shape, F32)
            dv = jnp.zeros(v.shape, F32)
            def products(g):
                s = lax.dot_general(k, q_ref[:, g * 64:(g + 1) * 64], (((1,), (1,)), ((), ())),
                                    preferred_element_type=F32)
                dp = lax.dot_general(v, do_ref[:, g * 64:(g + 1) * 64], (((1,), (1,)), ((), ())),
                                     preferred_element_type=F32)
                return (s if mask is None else jnp.where(mask, s, NEG_BIG)), dp

            nxt = products(0)
            for g in range(GROUP):
                q = q_ref[:, g * 64:(g + 1) * 64]
                do_g = do_ref[:, g * 64:(g + 1) * 64]
                lse_g, dd_g = lse_rows[g:g + 1, :], dd_rows[g:g + 1, :]
                s, dp = nxt
                if g + 1 < GROUP:
                    nxt = products(g + 1)
                pb = jnp.exp(s - lse_g).astype(BF16)
                ds = (pb.astype(F32) * (dp - dd_g)).astype(BF16)
                dk = dk + jnp.dot(ds, q, preferred_element_type=F32)
                dv = dv + jnp.dot(pb, do_g, preferred_element_type=F32)
                dq_t.append(jnp.dot(k_t, ds, preferred_element_type=F32))
                if use_sink:
                    p_sink = jnp.exp(sink_ref[jnp.maximum(hg - 2, 0) * GROUP + g] - lse_g)
                    dsinks.append(jnp.broadcast_to(-jnp.sum(p_sink * dd_g, axis=1, keepdims=True), (1, 128)))
            dq_ref[...] = jnp.concatenate(dq_t, axis=0).T
            dkv = jnp.concatenate([dk, dv], axis=1)
            off = 0
            for start, size in spans:
                dkv_ref[pl.ds(start, size), :] += dkv[off:off + size]
                off += size
            if use_sink:
                dsk_ref[0:GROUP, :] += jnp.concatenate(dsinks, axis=0)

        _attn_case(t, hg, kv_ref, c_rows, t_rows, fn, keys_first=True)

    outs, couts = _call(
        body, name=name, grid=(bl, N_HG, t_rows // TM), in_specs=[q_spec, kv_spec, q_spec, q_spec, lse_spec],
        out_specs=[q_spec, kv_spec, ds_spec],
        out_shape=[jax.ShapeDtypeStruct(q_all.shape, F32), jax.ShapeDtypeStruct(kv_all.shape, F32),
                   jax.ShapeDtypeStruct((bl, N_HG, 8, 128), F32)],
        args=[sink8, q_all, kv_all, do, o, lse], prefetch=1, sem=("parallel", "parallel", "arbitrary"), comm=comm)
    return outs if comm is None else (outs, couts)


def _silu(x):
    return x * jax.nn.sigmoid(x)


def _ada_fwd(c_rows, w_ada, b_cols, name, comm=None):
    nl, d, w = w_ada.shape
    r = c_rows.shape[0]

    def body(c_ref, w_ref, b_ref, o_ref):
        s = _silu(c_ref[...]).astype(BF16)
        o_ref[...] = jnp.dot(s, w_ref[...].astype(BF16), preferred_element_type=F32) + b_ref[...]

    outs, couts = _call(
        body, name=name, grid=(nl,),
        in_specs=[pl.BlockSpec((r, d), lambda l: (0, 0)), pl.BlockSpec((None, d, w), lambda l: (l, 0, 0)),
                  pl.BlockSpec((None, 1, w), lambda l: (l, 0, 0))],
        out_specs=pl.BlockSpec((None, r, w), lambda l: (l, 0, 0)),
        out_shape=jax.ShapeDtypeStruct((nl, r, w), F32), args=[c_rows, w_ada, b_cols], sem=("parallel",), comm=comm)
    return outs if comm is None else (outs, couts)


def _ada_bwd(c_rows, c_ctx, dmod, w_ada, name):
    nl, d, w = w_ada.shape
    r = c_rows.shape[0]

    def body(c_ref, cc_ref, g_ref, w_ref, dw_ref, dc_ref):
        l = pl.program_id(0)
        s = _silu(c_ref[...]).astype(BF16)
        gm = g_ref[...].astype(BF16)
        dw_ref[...] = lax.dot_general(s, gm, (((0,), (0,)), ((), ())), preferred_element_type=F32)
        ds = lax.dot_general(gm, w_ref[...].astype(BF16), (((1,), (1,)), ((), ())), preferred_element_type=F32)
        rows = lax.broadcasted_iota(jnp.int32, ds.shape, 0)
        dsc = jnp.sum(jnp.where(rows % ADA_ROWS == 2, ds, 0.0), axis=0, keepdims=True)
        x = cc_ref[...]
        sg = jax.nn.sigmoid(x)
        dcc = dsc * (sg * (1.0 + x * (1.0 - sg)))
        out = jnp.concatenate([dcc, jnp.zeros((7, d), F32)], axis=0)

        @pl.when(l == 0)
        def _():
            dc_ref[...] = out

        @pl.when(l != 0)
        def _():
            dc_ref[...] += out

    return pl.pallas_call(
        body, name=name, grid=(nl,),
        in_specs=[pl.BlockSpec((r, d), lambda l: (0, 0)), pl.BlockSpec((1, d), lambda l: (0, 0)),
                  pl.BlockSpec((None, r, w), lambda l: (l, 0, 0)), pl.BlockSpec((None, d, w), lambda l: (l, 0, 0))],
        out_specs=[pl.BlockSpec((None, d, w), lambda l: (l, 0, 0)), pl.BlockSpec((8, d), lambda l: (0, 0))],
        out_shape=[jax.ShapeDtypeStruct((nl, d, w), F32), jax.ShapeDtypeStruct((8, d), F32)],
        compiler_params=_params(("arbitrary",)),
    )(c_rows, c_ctx, dmod, w_ada)


def _adam_math(w, g, m, v):
    m = ADAM_B1 * m + (1.0 - ADAM_B1) * g
    v = ADAM_B2 * v + (1.0 - ADAM_B2) * (g * g)
    m_hat = m / (1.0 - ADAM_B1 ** ADAM_STEP)
    v_hat = v / (1.0 - ADAM_B2 ** ADAM_STEP)
    delta = -ADAM_LR * (m_hat / (jnp.sqrt(v_hat) + ADAM_EPS) + ADAM_WD * w)
    return delta, m, v


def _adamw(w, m, v, g_own, g_recv, name, rows=256):
    nl, r, c = w.shape
    tr = min(rows, r)
    spec = pl.BlockSpec((None, tr, c), lambda l, i: (l, i, 0))
    per_layer = isinstance(g_own, (list, tuple))
    own = list(g_own) if per_layer else [g_own]
    recv = [] if g_recv is None else list(g_recv)
    in_specs = [spec] * 3 + [pl.BlockSpec((tr, c), lambda l, i: (i, 0)) if per_layer else spec] * len(own)
    in_specs += [pl.BlockSpec((N_DEV, tr, c), lambda l, i: (0, i, 0))] * len(recv)

    def body(*refs):
        w_ref, m_ref, v_ref = refs[:3]
        own_refs, recv_refs = refs[3:3 + len(own)], refs[3 + len(own):3 + len(own) + len(recv)]
        go_ref, d_ref, mo_ref, vo_ref = refs[-4:]

        def update(li):
            g = own_refs[li][...].astype(F32)
            if recv:
                for k in range(1, N_DEV):
                    g = g + recv_refs[li][k].astype(F32)
            delta, m_, v_ = _adam_math(w_ref[...], g, m_ref[...], v_ref[...])
            go_ref[...] = g
            d_ref[...] = delta
            mo_ref[...] = m_
            vo_ref[...] = v_

        if per_layer:
            for li in range(nl):
                pl.when(pl.program_id(0) == li)(functools.partial(update, li))
        else:
            update(0)

    return pl.pallas_call(
        body, name=name, grid=(nl, r // tr), in_specs=in_specs, out_specs=[spec] * 4,
        out_shape=[jax.ShapeDtypeStruct(w.shape, F32)] * 4, compiler_params=_params(("parallel", "parallel")),
    )(w, m, v, *own, *recv)


def _small_adamw(w, m, v, g_all, name):
    def body(w_ref, m_ref, v_ref, g_ref, go_ref, d_ref, mo_ref, vo_ref):
        g = g_ref[0]
        for k in range(1, N_DEV):
            g = g + g_ref[k]
        delta, m_, v_ = _adam_math(w_ref[...], g, m_ref[...], v_ref[...])
        go_ref[...] = g
        d_ref[...] = delta
        mo_ref[...] = m_
        vo_ref[...] = v_

    return pl.pallas_call(
        body, name=name, out_shape=[jax.ShapeDtypeStruct(w.shape, F32)] * 4, compiler_params=_params(),
    )(w, m, v, g_all)


SMALL = ("c_ctx", "b_ada", "g_pre_mix", "g_post_mix", "g_pre_mlp", "g_post_mlp", "q_norm", "k_norm", "sink")


def _pack_small(parts):
    flat = jnp.concatenate([parts[n].reshape(-1) for n in SMALL])
    rows = -(-flat.shape[0] // 1024) * 8
    return jnp.pad(flat, (0, rows * 128 - flat.shape[0])).reshape(rows, 128)


def _unpack_small(packed, like):
    flat = packed.reshape(-1)
    out, off = {}, 0
    for n in SMALL:
        size = int(np.prod(like[n].shape))
        out[n] = flat[off:off + size].reshape(like[n].shape)
        off += size
    return out


def kernel(x, c, ctx, c_ctx, w_ada, b_ada, g_pre_mix, g_post_mix, g_pre_mlp, g_post_mlp, w_in, q_norm, k_norm, sink, w_out, w_up, w_down, loss_target, m_c_ctx, m_w_ada, m_b_ada, m_g_pre_mix, m_g_post_mix, m_g_pre_mlp, m_g_post_mlp, m_w_in, m_q_norm, m_k_norm, m_sink, m_w_out, m_w_up, m_w_down, v_c_ctx, v_w_ada, v_b_ada, v_g_pre_mix, v_g_post_mix, v_g_pre_mlp, v_g_post_mlp, v_w_in, v_q_norm, v_k_norm, v_sink, v_w_out, v_w_up, v_w_down):
    bl, s_rows, d = x.shape
    c_rows = ctx.shape[1]
    assert c_rows == TM and s_rows % TM == 0 and bl == 2
    t_rows = c_rows + s_rows
    nt = t_rows // TM
    m_rows = bl * t_rows
    nl = w_in.shape[0]
    ada_w = w_ada.shape[2]
    d_ff = w_up.shape[2] * N_DEV
    me = _my_index()

    shard = lambda w_, l: w_[l].astype(BF16)
    c_pad = jnp.concatenate([c, c_ctx[None, :], jnp.zeros((ADA_ROWS - bl - 1, d), F32)], axis=0)
    c_all, = _comm_only(_Comm([(c_pad, GATHER)]), "gather_c")
    c_all = c_all.reshape(N_DEV * ADA_ROWS, d)
    gathered = {0: {}}

    def layer_weights(l):
        g_ = gathered[l]
        w_out_f = g_["w_out"].reshape(-1, d)
        w_down_f = g_["w_down"].reshape(d_ff, d)
        return dict(
            w_in_t=g_["w_in"].transpose(0, 2, 1).reshape(PROJ_W, d), w_out_f=w_out_f, w_out_t=w_out_f.T,
            w_up_s=g_["w_up"], w_up_t=g_["w_up"].transpose(0, 2, 1).reshape(d_ff, d),
            w_down_f=w_down_f, w_down_t=w_down_f.T)

    big = dict(tm=1536, tn=512)
    deep = dict(tm=1536, tn=256, tk=d_ff)
    wide = dict(tm=1024, tn=256, tk=m_rows)

    b_cols = lax.dynamic_slice(b_ada, (0, me * ada_w), (nl, ada_w))[:, None, :]
    mod_cols, (gathered[0]["w_in"],) = _ada_fwd(c_all, w_ada, b_cols, "ada_fwd",
                                                comm=_Comm([(shard(w_in, 0), GATHER)]))
    mod_slots = mod_cols.reshape(nl, N_DEV, ADA_ROWS, ada_w).transpose(1, 0, 2, 3)
    mod_g, = _comm_only(_Comm([(mod_slots, TO_OWNER)]), "exchange_mod")
    mine = mod_g.transpose(1, 2, 0, 3).reshape(nl, ADA_ROWS, N_MOD, d)
    pad = jnp.zeros((bl, 2, MOD_ROWS - N_MOD, d), F32)
    modsel = [jnp.concatenate([jnp.stack([jnp.broadcast_to(mine[l, bl], (bl, N_MOD, d)), mine[l, :bl]], axis=1), pad],
                              axis=2) for l in range(nl)]

    cos, sin = _rope_tables(t_rows, c_rows)
    emat = _head_mean_matrix()
    row = lambda a: a[None, :]
    qn = [jnp.tile(q_norm[l], 8)[None, :] for l in range(nl)]
    kn = [jnp.tile(k_norm[l], 2)[None, :] for l in range(nl)]

    h = jnp.concatenate([ctx, x], axis=1).reshape(m_rows, d)
    saved = []
    weights_of = {}
    for l in range(nl):
        if l == 0:
            (u, u_t), (gathered[0]["w_out"],) = _norm_mod_fwd(
                h, row(g_pre_mix[l]), modsel[l], 0, 1, nt, f"mix_mod_fwd{l}", comm=_Comm([(shard(w_out, 0), GATHER)]))
        else:
            u, u_t = _norm_mod_fwd(h, row(g_pre_mix[l]), modsel[l], 0, 1, nt, f"mix_mod_fwd{l}")
        w_in_f = gathered[l]["w_in"].transpose(1, 0, 2).reshape(d, PROJ_W)
        proj = _mm(u, w_in_f, name=f"mm_in{l}", tk=d, **big)
        q_all, kv_all = _prep_fwd(proj, qn[l], kn[l], cos, sin, emat, nt, f"prep_fwd{l}")
        (o, o_t, lse), (w_up_g, w_down_g) = _attn_fwd(
            q_all.reshape(bl, t_rows, 1024), kv_all.reshape(bl, t_rows, 512), sink[l], c_rows, f"attn_fwd{l}",
            comm=_Comm([(shard(w_up, l), GATHER), (shard(w_down, l), GATHER)]))
        o = o.reshape(m_rows, 1024)
        gathered[l].update(w_up=w_up_g, w_down=w_down_g)
        wl = weights_of[l] = layer_weights(l)
        mix = _mm(o, wl["w_out_f"], name=f"mm_out{l}", tk=1024, **big)
        h_mid = _gate_fwd(h, mix, row(g_post_mix[l]), modsel[l], 2, nt, f"mix_gate_fwd{l}")
        v_in, v_t = _norm_mod_fwd(h_mid, row(g_pre_mlp[l]), modsel[l], 3, 4, nt, f"mlp_mod_fwd{l}")
        more = l + 1 < nl
        res_up = _mm(v_in, wl["w_up_s"], name=f"mm_up{l}", b_mode="nn_slots", epilogue="relu2", tk=d,
                     comm=_Comm([(shard(w_in, l + 1), GATHER)]) if more else None, **big)
        (r_act, r_t), nxt_in = res_up if more else (res_up, None)
        res_down = _mm(r_act, wl["w_down_f"], name=f"mm_down{l}",
                       comm=_Comm([(shard(w_out, l + 1), GATHER)]) if more else None, **deep)
        y, nxt_out = res_down if more else (res_down, None)
        if more:
            gathered[l + 1] = dict(w_in=nxt_in[0], w_out=nxt_out[0])
        h_out = _gate_fwd(h_mid, y, row(g_post_mlp[l]), modsel[l], 5, nt, f"mlp_gate_fwd{l}")
        saved.append((h, u_t, proj, q_all, kv_all, o, o_t, lse, mix, h_mid, v_t, r_act, r_t, y))
        h = h_out

    dh, ss = _loss_grad(h, loss_target.reshape(bl * s_rows, d), c_rows, nt, "loss_grad")
    loss = lax.psum(0.5 * ss[0, 0] / d, MESH_AXES)

    small_g = {n: [None] * nl for n in SMALL if n not in ("c_ctx", "b_ada")}
    dmod_rows = []
    slots = {n: [None] * nl for n in ("w_in", "w_out", "w_up", "w_down")}
    recvd = {n: [None] * nl for n in slots}
    send = lambda n, l_: (slots[n][l_], TO_OWNER_XOR)
    for l in reversed(range(nl)):
        h_in, u_t, proj, q_all, kv_all, o, o_t, lse, mix, h_mid, v_t, r_act, r_t, y = saved[l]
        wl = weights_of[l]
        later = l + 1 < nl
        dy, dvec_g2 = _gate_bwd(y, row(g_post_mlp[l]), modsel[l], dh, 5, nt, f"mlp_gate_bwd{l}")
        res = _mm(dy, wl["w_down_t"], name=f"mm_da{l}", epilogue="relu2_bwd", extra=r_act, out_dtype=BF16, tk=d,
                  comm=_Comm([send("w_out", l + 1)]) if later else None, **big)
        da = res[0] if later else res
        if later:
            recvd["w_out"][l + 1] = res[1][0]
        res = _mm(r_t, dy, name=f"mm_dw_down{l}", out_dtype=BF16,
                  comm=_Comm([send("w_in", l + 1)]) if later else None, **wide)
        dw_down = res[0] if later else res
        if later:
            recvd["w_in"][l + 1] = res[1][0]
        dw_up = _mm(v_t, da, name=f"mm_dw_up{l}", out_mode="slots", out_dtype=BF16, **wide)
        slots["w_down"][l] = dw_down.reshape(N_DEV, -1, d)
        slots["w_up"][l] = dw_up
        dv = _mm(da, wl["w_up_t"], name=f"mm_dv{l}", **deep)
        dh, dvec_m2 = _norm_mod_bwd(h_mid, row(g_pre_mlp[l]), modsel[l], dv, dh, 3, 4, nt, f"mlp_mod_bwd{l}")
        dmix, dvec_g1 = _gate_bwd(mix, row(g_post_mix[l]), modsel[l], dh, 2, nt, f"mix_gate_bwd{l}")
        do = _mm(dmix, wl["w_out_t"], name=f"mm_do{l}", out_dtype=BF16, tk=d, **big)
        dw_out = _mm(o_t, dmix, name=f"mm_dw_out{l}", out_dtype=BF16, **wide)
        slots["w_out"][l] = dw_out.reshape(N_DEV, -1, d)
        (dq, dkv, dsk), (recvd["w_down"][l], recvd["w_up"][l]) = _attn_bwd(
            q_all.reshape(bl, t_rows, 1024), kv_all.reshape(bl, t_rows, 512), do.reshape(bl, t_rows, 1024),
            o.reshape(bl, t_rows, 1024), lse, sink[l], c_rows, f"attn_bwd{l}",
            comm=_Comm([send("w_down", l), send("w_up", l)]))
        last = l == 0
        res = _prep_bwd(proj, dq.reshape(m_rows, 1024), dkv.reshape(m_rows, 512), qn[l], kn[l], cos, sin, emat, nt,
                        f"prep_bwd{l}", comm=_Comm([send("w_out", l)]) if last else None)
        dproj, dqn, dkn = res[0] if last else res
        if last:
            recvd["w_out"][l] = res[1][0]
        dw_in = _mm(u_t, dproj, name=f"mm_dw_in{l}", out_dtype=BF16, **wide)
        slots["w_in"][l] = dw_in.reshape(d, N_DEV, PROJ_W // N_DEV).transpose(1, 0, 2)
        res = _mm(dproj, wl["w_in_t"], name=f"mm_du{l}", tk=PROJ_W,
                  comm=_Comm([send("w_in", l)]) if last else None, **big)
        du = res[0] if last else res
        if last:
            recvd["w_in"][l] = res[1][0]
        dh, dvec_m1 = _norm_mod_bwd(h_in, row(g_pre_mix[l]), modsel[l], du, dh, 0, 1, nt, f"mix_mod_bwd{l}",
                                    latent_only=last)

        small_g["g_pre_mix"][l] = jnp.sum(dvec_m1[:, :, 2], axis=(0, 1))
        small_g["g_post_mix"][l] = jnp.sum(dvec_g1[:, :, 1], axis=(0, 1))
        small_g["g_pre_mlp"][l] = jnp.sum(dvec_m2[:, :, 2], axis=(0, 1))
        small_g["g_post_mlp"][l] = jnp.sum(dvec_g2[:, :, 1], axis=(0, 1))
        small_g["q_norm"][l] = jnp.sum(dqn[0].reshape(8, HEAD_DIM), axis=0)
        small_g["k_norm"][l] = jnp.sum(dkn[0].reshape(2, HEAD_DIM), axis=0)
        small_g["sink"][l] = jnp.sum(dsk[:, 2:, :GROUP, 0], axis=0).reshape(-1)
        dms = jnp.stack([dvec_m1[:, :, 0], dvec_m1[:, :, 1], dvec_g1[:, :, 0],
                         dvec_m2[:, :, 0], dvec_m2[:, :, 1], dvec_g2[:, :, 0]], axis=2)
        rows = jnp.concatenate([dms[:, 1], jnp.sum(dms[:, 0], axis=0)[None]], axis=0)
        dmod_rows.append(jnp.pad(rows.reshape(bl + 1, N_MOD * d), ((0, ADA_ROWS - bl - 1), (0, 0))))
    dmod_rows = dmod_rows[::-1]
    grad_x = dh.reshape(bl, s_rows, d)

    dmod_slots = jnp.stack(dmod_rows).reshape(nl, ADA_ROWS, N_DEV, ada_w).transpose(2, 0, 1, 3)
    dmod_g, = _comm_only(_Comm([(dmod_slots, TO_OWNER)]), "exchange_dmod")
    dmod_mine = dmod_g.transpose(1, 0, 2, 3).reshape(nl, N_DEV * ADA_ROWS, ada_w)
    dw_ada, dcc = _ada_bwd(c_all, c_ctx[None, :], dmod_mine, w_ada, "ada_bwd")

    parts = {n: jnp.stack(small_g[n]) for n in small_g}
    parts["c_ctx"] = dcc[0]
    parts["b_ada"] = jnp.stack([jnp.sum(r_[: bl + 1], axis=0) for r_ in dmod_rows])
    weights = dict(c_ctx=c_ctx, b_ada=b_ada, g_pre_mix=g_pre_mix, g_post_mix=g_post_mix, g_pre_mlp=g_pre_mlp,
                   g_post_mlp=g_post_mlp, q_norm=q_norm, k_norm=k_norm, sink=sink)
    moms = dict(c_ctx=m_c_ctx, b_ada=m_b_ada, g_pre_mix=m_g_pre_mix, g_post_mix=m_g_post_mix, g_pre_mlp=m_g_pre_mlp,
                g_post_mlp=m_g_post_mlp, q_norm=m_q_norm, k_norm=m_k_norm, sink=m_sink)
    vels = dict(c_ctx=v_c_ctx, b_ada=v_b_ada, g_pre_mix=v_g_pre_mix, g_post_mix=v_g_post_mix, g_pre_mlp=v_g_pre_mlp,
                g_post_mlp=v_g_post_mlp, q_norm=v_q_norm, k_norm=v_k_norm, sink=v_sink)
    small_all, = _comm_only(_Comm([(_pack_small(parts), GATHER)]), "gather_small")
    s_out = _small_adamw(_pack_small(weights), _pack_small(moms), _pack_small(vels), small_all, "adamw_small")
    s_g, s_d, s_m, s_v = [_unpack_small(a, weights) for a in s_out]

    res = {}
    for n, w_, m_, v_ in (("w_in", w_in, m_w_in, v_w_in), ("w_out", w_out, m_w_out, v_w_out),
                          ("w_up", w_up, m_w_up, v_w_up), ("w_down", w_down, m_w_down, v_w_down)):
        own = [lax.dynamic_index_in_dim(slots[n][l], me, axis=0, keepdims=False) for l in range(nl)]
        res[n] = _adamw(w_, m_, v_, own, recvd[n], f"adamw_{n}")
    res["w_ada"] = _adamw(w_ada, m_w_ada, v_w_ada, dw_ada, None, "adamw_w_ada")

    order = ("c_ctx", "w_ada", "b_ada", "g_pre_mix", "g_post_mix", "g_pre_mlp", "g_post_mlp", "w_in", "q_norm",
             "k_norm", "sink", "w_out", "w_up", "w_down")
    outs = [loss, grad_x]
    for i, small in enumerate((s_g, s_d, s_m, s_v)):
        outs += [small[n] if n in small else res[n][i] for n in order]
    return tuple(outs)
```

```python
import functools

import jax
import jax.numpy as jnp
import numpy as np
from jax import lax
from jax.experimental import pallas as pl
from jax.experimental.pallas import tpu as pltpu

F32 = jnp.float32
BF16 = jnp.bfloat16

HEAD_DIM = 64
GROUP = 4
N_HG = 4
Q_WIDTH = GROUP * HEAD_DIM
WINDOW = 128
GRID_W = 64
ROPE_THETA = 10000.0
EPS = 1e-6
NEG_BIG = -1e30
N_MOD = 6
MOD_ROWS = 8
TM = 256
N_DEV = 8
ADA_ROWS = 8
VMEM_LIMIT = 56 * 1024 * 1024

ADAM_LR = 0.001
ADAM_B1 = 0.9
ADAM_B2 = 0.999
ADAM_EPS = 1e-08
ADAM_WD = 0.01
ADAM_STEP = 10

MESH_AXES = ("x", "y", "c")


def _params(sem=None):
    kw = dict(vmem_limit_bytes=VMEM_LIMIT)
    if sem is not None:
        kw["dimension_semantics"] = sem
    return pltpu.CompilerParams(**kw)


def _my_index():
    return 4 * lax.axis_index("x") + 2 * lax.axis_index("y") + lax.axis_index("c")


def _peer(k):
    x, y, c = lax.axis_index("x"), lax.axis_index("y"), lax.axis_index("c")
    kx, ky, kc = (k >> 2) & 1, (k >> 1) & 1, k & 1
    px = (1 - x) if kx else x
    py = (1 - y) if ky else y
    pc = (1 - c) if kc else c
    return (px, py, pc), 4 * px + 2 * py + pc


GATHER, TO_OWNER, TO_OWNER_XOR = "gather", "to_owner", "to_owner_xor"


class _Comm:
    def __init__(self, items):
        self.items = list(items)
        self.arrays = [a for a, _ in self.items]

    def out_shapes(self):
        return [jax.ShapeDtypeStruct(((N_DEV,) + a.shape) if kind == GATHER else a.shape, a.dtype)
                for a, kind in self.items]

    def sem_shapes(self):
        n = len(self.items) * N_DEV
        return [pltpu.SemaphoreType.DMA((n,)), pltpu.SemaphoreType.DMA((n,))]

    def _copies(self, in_refs, out_refs, send_sems, recv_sems):
        me = _my_index()
        local, remote = [], []
        for i, ((_, kind), x_ref, o_ref) in enumerate(zip(self.items, in_refs, out_refs)):
            base = i * N_DEV
            own_src = x_ref if kind == GATHER else x_ref.at[me]
            own_dst = o_ref.at[0] if kind == TO_OWNER_XOR else o_ref.at[me]
            local.append(pltpu.make_async_copy(own_src, own_dst, send_sems.at[base]))
            for k in range(1, N_DEV):
                peer, pidx = _peer(k)
                remote.append(pltpu.make_async_remote_copy(
                    src_ref=x_ref if kind == GATHER else x_ref.at[pidx],
                    dst_ref=o_ref.at[k] if kind == TO_OWNER_XOR else o_ref.at[me],
                    send_sem=send_sems.at[base + k], recv_sem=recv_sems.at[base + k],
                    device_id=peer, device_id_type=pl.DeviceIdType.MESH))
        return local, remote

    def start(self, in_refs, out_refs, send_sems, recv_sems):
        local, remote = self._copies(in_refs, out_refs, send_sems, recv_sems)
        for cp in local + remote:
            cp.start()

    def wait(self, in_refs, out_refs, send_sems, recv_sems):
        local, remote = self._copies(in_refs, out_refs, send_sems, recv_sems)
        for cp in remote:
            cp.wait_recv()
        for cp in remote:
            cp.wait_send()
        for cp in local:
            cp.wait()


def _call(body, *, name, grid, in_specs, out_specs, out_shape, args, scratch_shapes=(), prefetch=0, sem=None,
          comm=None):
    single = not isinstance(out_shape, (list, tuple))
    out_shape = [out_shape] if single else list(out_shape)
    out_specs = [out_specs] if single else list(out_specs)
    in_specs, scratch_shapes, args = list(in_specs), list(scratch_shapes), list(args)
    n_in, n_out = len(in_specs), len(out_shape)
    if comm is not None:
        nc = len(comm.arrays)
        hbm = pl.BlockSpec(memory_space=pl.ANY)
        inner = body

        def body(*refs):
            pre, r = refs[:prefetch], refs[prefetch:]
            ins, cin = r[:n_in], r[n_in:n_in + nc]
            outs, cout = r[n_in + nc:n_in + nc + n_out], r[n_in + nc + n_out:n_in + 2 * nc + n_out]
            scr, sems = r[n_in + 2 * nc + n_out:len(r) - 2], r[len(r) - 2:]
            ids = [pl.program_id(i) for i in range(len(grid))]

            def when(flags, fn):
                if flags:
                    pl.when(functools.reduce(jnp.logical_and, flags))(fn)
                else:
                    fn()

            when([i == 0 for i in ids], lambda: comm.start(cin, cout, *sems))
            inner(*pre, *ins, *outs, *scr)
            when([i == n - 1 for i, n in zip(ids, grid)], lambda: comm.wait(cin, cout, *sems))

        in_specs += [hbm] * nc
        out_specs += [hbm] * nc
        out_shape += comm.out_shapes()
        scratch_shapes += comm.sem_shapes()
        args += comm.arrays
        sem = ("arbitrary",) * len(grid)
    kw = dict(name=name, out_shape=out_shape, compiler_params=_params(sem if grid else None))
    if prefetch:
        kw["grid_spec"] = pltpu.PrefetchScalarGridSpec(
            num_scalar_prefetch=prefetch, grid=grid, in_specs=in_specs, out_specs=out_specs,
            scratch_shapes=scratch_shapes)
    else:
        kw.update(in_specs=in_specs, out_specs=out_specs, scratch_shapes=scratch_shapes)
        if grid:
            kw["grid"] = grid
    res = list(pl.pallas_call(body, **kw)(*args))
    outs = res[:n_out]
    return (outs[0] if single else outs), res[n_out:]


def _comm_only(comm, name):
    return _call(lambda: None, name=name, grid=(), in_specs=[], out_specs=[], out_shape=[], args=[], comm=comm)[1]


def _mm(a, b, *, name, ta=False, b_mode="nn", out_mode="plain", out_dtype=F32, tm=512, tn=512, tk=512,
        epilogue=None, extra=None, comm=None):
    if ta:
        kdim, m = a.shape
    else:
        m, kdim = a.shape
    if b_mode == "nn":
        n = b.shape[1]
    elif b_mode == "nt":
        n = b.shape[0]
    elif b_mode == "nn_slots":
        n = b.shape[0] * b.shape[2]
        tn = b.shape[2]
    else:
        n = b.shape[1]
        tk = b.shape[2]
    if out_mode == "slots":
        tn = n // N_DEV
    tm, tn, tk = min(tm, m), min(tn, n), min(tk, kdim)
    assert m % tm == 0 and n % tn == 0 and kdim % tk == 0, (name, m, n, kdim, tm, tn, tk)
    nk = kdim // tk

    a_spec = pl.BlockSpec((tk, tm), lambda i, j, k: (k, i)) if ta else pl.BlockSpec((tm, tk), lambda i, j, k: (i, k))
    if b_mode == "nn":
        b_spec = pl.BlockSpec((tk, tn), lambda i, j, k: (k, j))
    elif b_mode == "nt":
        b_spec = pl.BlockSpec((tn, tk), lambda i, j, k: (j, k))
    elif b_mode == "nn_slots":
        b_spec = pl.BlockSpec((None, tk, tn), lambda i, j, k: (j, k, 0))
    else:
        b_spec = pl.BlockSpec((None, tn, tk), lambda i, j, k: (k, j, 0))
    tb = b_mode in ("nt", "nt_slots")
    if out_mode == "plain":
        o_shape, o_spec = (m, n), pl.BlockSpec((tm, tn), lambda i, j, k: (i, j))
    else:
        o_shape, o_spec = (N_DEV, m, tn), pl.BlockSpec((None, tm, tn), lambda i, j, k: (j, i, 0))
    dims = (((0 if ta else 1,), (1 if tb else 0,)), ((), ()))

    in_specs = [a_spec, b_spec]
    args = [a, b]
    if epilogue == "relu2_bwd":
        in_specs.append(pl.BlockSpec((tm, tn), lambda i, j, k: (i, j)))
        args.append(extra)
    if epilogue == "relu2":
        out_shape = [jax.ShapeDtypeStruct(o_shape, BF16), jax.ShapeDtypeStruct((n, m), BF16)]
        out_specs = [o_spec, pl.BlockSpec((tn, tm), lambda i, j, k: (j, i))]
    else:
        out_shape = jax.ShapeDtypeStruct(o_shape, out_dtype)
        out_specs = o_spec

    def finish(refs, acc):
        if epilogue == "relu2":
            r = jnp.maximum(acc, 0.0)
            r2 = (r * r).astype(BF16)
            refs[2][...] = r2
            refs[3][...] = r2.T
        elif epilogue == "relu2_bwd":
            refs[3][...] = (acc * (2.0 * jnp.sqrt(refs[2][...].astype(F32)))).astype(out_dtype)
        else:
            refs[2][...] = acc.astype(out_dtype)

    def body(*refs):
        part = lax.dot_general(refs[0][...], refs[1][...], dims, preferred_element_type=F32)
        if nk == 1:
            finish(refs, part)
            return
        acc_ref = refs[-1]
        k = pl.program_id(2)

        @pl.when(k == 0)
        def _():
            acc_ref[...] = part

        @pl.when(jnp.logical_and(k > 0, k < nk - 1))
        def _():
            acc_ref[...] += part

        @pl.when(k == nk - 1)
        def _():
            finish(refs, acc_ref[...] + part)

    outs, couts = _call(
        body, name=name, grid=(m // tm, n // tn, nk), in_specs=in_specs, out_specs=out_specs, out_shape=out_shape,
        args=args, scratch_shapes=[] if nk == 1 else [pltpu.VMEM((tm, tn), F32)],
        sem=("parallel", "parallel", "arbitrary"), comm=comm)
    return outs if comm is None else (outs, couts)


def _row_specs(d, nt):
    row = pl.BlockSpec((TM, d), lambda t: (t, 0))
    vec = pl.BlockSpec((1, d), lambda t: (0, 0))
    mod = pl.BlockSpec((None, None, MOD_ROWS, d), lambda t: (t // nt, jnp.minimum(t % nt, 1), 0, 0))
    return row, vec, mod


def _rms(x):
    r = lax.rsqrt(jnp.mean(x * x, axis=1, keepdims=True) + EPS)
    return r, x * r


def _norm_mod_fwd(h, g, modsel, i_sh, i_sc, nt, name, comm=None):
    m, d = h.shape
    row, vec, mod = _row_specs(d, nt)

    def body(h_ref, g_ref, ms_ref, u_ref, ut_ref):
        _, xh = _rms(h_ref[...])
        ms = ms_ref[...]
        u = (xh * g_ref[...] * (1.0 + ms[i_sc:i_sc + 1]) + ms[i_sh:i_sh + 1]).astype(BF16)
        u_ref[...] = u
        ut_ref[...] = u.T

    outs, couts = _call(
        body, name=name, grid=(m // TM,), in_specs=[row, vec, mod],
        out_specs=[row, pl.BlockSpec((d, TM), lambda t: (0, t))],
        out_shape=[jax.ShapeDtypeStruct((m, d), BF16), jax.ShapeDtypeStruct((d, m), BF16)],
        args=[h, g, modsel], sem=("parallel",), comm=comm)
    return outs if comm is None else (outs, couts)


def _acc_rows(t, nt, dvec_ref, rows):
    first = (t % nt) <= 1

    @pl.when(first)
    def _():
        dvec_ref[...] = rows

    @pl.when(jnp.logical_not(first))
    def _():
        dvec_ref[...] += rows


def _norm_mod_bwd(h, g, modsel, du, dh_in, i_sh, i_sc, nt, name, comm=None, latent_only=False):
    m, d = h.shape
    row, vec, mod = _row_specs(d, nt)
    dh_rows, dh_spec = m, row
    if latent_only:
        dh_rows = m // nt * (nt - 1)
        dh_spec = pl.BlockSpec((TM, d), lambda t: ((t // nt) * (nt - 1) + jnp.maximum(t % nt, 1) - 1, 0))

    def body(h_ref, g_ref, ms_ref, du_ref, dhi_ref, dh_ref, dvec_ref):
        t = pl.program_id(0)
        r, xh = _rms(h_ref[...])
        g_ = g_ref[...]
        ms = ms_ref[...]
        du_ = du_ref[...]
        y = xh * g_
        dy = du_ * (1.0 + ms[i_sc:i_sc + 1])
        dxh = dy * g_
        dx = r * (dxh - xh * jnp.mean(dxh * xh, axis=1, keepdims=True))
        dh_ref[...] = dhi_ref[...] + dx
        rows = jnp.concatenate([
            jnp.sum(du_, axis=0, keepdims=True), jnp.sum(du_ * y, axis=0, keepdims=True),
            jnp.sum(dy * xh, axis=0, keepdims=True), jnp.zeros((MOD_ROWS - 3, d), F32)], axis=0)
        _acc_rows(t, nt, dvec_ref, rows)

    outs, couts = _call(
        body, name=name, grid=(m // TM,), in_specs=[row, vec, mod, row, row], out_specs=[dh_spec, mod],
        out_shape=[jax.ShapeDtypeStruct((dh_rows, d), F32), jax.ShapeDtypeStruct(modsel.shape, F32)],
        args=[h, g, modsel, du, dh_in], sem=("arbitrary",), comm=comm)
    return outs if comm is None else (outs, couts)


def _gate_rows(dh_, xh, g_, gate):
    dy = dh_ * gate
    return dy * g_, jnp.sum(dh_ * (xh * g_), axis=0, keepdims=True), jnp.sum(dy * xh, axis=0, keepdims=True)


def _gate_mod_fwd(h, z, g_post, ms_gate, i_g, g_pre, ms_mod, i_sh, i_sc, nt, name):
    m, d = h.shape
    row, vec, mod = _row_specs(d, nt)

    def body(h_ref, z_ref, gp_ref, msg_ref, gq_ref, msm_ref, hn_ref, u_ref, ut_ref):
        _, zh = _rms(z_ref[...])
        hn = h_ref[...] + msg_ref[...][i_g:i_g + 1] * (zh * gp_ref[...])
        hn_ref[...] = hn
        _, xh = _rms(hn)
        ms = msm_ref[...]
        u = (xh * gq_ref[...] * (1.0 + ms[i_sc:i_sc + 1]) + ms[i_sh:i_sh + 1]).astype(BF16)
        u_ref[...] = u
        ut_ref[...] = u.T

    return pl.pallas_call(
        body, name=name, grid=(m // TM,), in_specs=[row, row, vec, mod, vec, mod],
        out_specs=[row, row, pl.BlockSpec((d, TM), lambda t: (0, t))],
        out_shape=[jax.ShapeDtypeStruct((m, d), F32), jax.ShapeDtypeStruct((m, d), BF16),
                   jax.ShapeDtypeStruct((d, m), BF16)],
        compiler_params=_params(("parallel",)),
    )(h, z, g_post, ms_gate, g_pre, ms_mod)


def _mod_gate_bwd(h, g_pre, ms_mod, i_sh, i_sc, du, dh_in, z, g_post, ms_gate, i_g, nt, name):
    m, d = h.shape
    row, vec, mod = _row_specs(d, nt)

    def body(h_ref, gq_ref, msm_ref, du_ref, dhi_ref, z_ref, gp_ref, msg_ref, dh_ref, dvm_ref, dz_ref, dvg_ref):
        t = pl.program_id(0)
        r, xh = _rms(h_ref[...])
        gq = gq_ref[...]
        ms = msm_ref[...]
        du_ = du_ref[...]
        dy = du_ * (1.0 + ms[i_sc:i_sc + 1])
        dxh = dy * gq
        dh_ = dhi_ref[...] + r * (dxh - xh * jnp.mean(dxh * xh, axis=1, keepdims=True))
        dh_ref[...] = dh_
        _acc_rows(t, nt, dvm_ref, jnp.concatenate([
            jnp.sum(du_, axis=0, keepdims=True), jnp.sum(du_ * (xh * gq), axis=0, keepdims=True),
            jnp.sum(dy * xh, axis=0, keepdims=True), jnp.zeros((MOD_ROWS - 3, d), F32)], axis=0))
        rz, zh = _rms(z_ref[...])
        dzh, d_gate, d_gp = _gate_rows(dh_, zh, gp_ref[...], msg_ref[...][i_g:i_g + 1])
        dz_ref[...] = (rz * (dzh - zh * jnp.mean(dzh * zh, axis=1, keepdims=True))).astype(BF16)
        _acc_rows(t, nt, dvg_ref, jnp.concatenate([d_gate, d_gp, jnp.zeros((MOD_ROWS - 2, d), F32)], axis=0))

    return pl.pallas_call(
        body, name=name, grid=(m // TM,), in_specs=[row, vec, mod, row, row, row, vec, mod],
        out_specs=[row, mod, row, mod],
        out_shape=[jax.ShapeDtypeStruct((m, d), F32), jax.ShapeDtypeStruct(ms_mod.shape, F32),
                   jax.ShapeDtypeStruct((m, d), BF16), jax.ShapeDtypeStruct(ms_gate.shape, F32)],
        compiler_params=_params(("arbitrary",)),
    )(h, g_pre, ms_mod, du, dh_in, z, g_post, ms_gate)


def _gate_loss_bwd(h, z, g_post, modsel, i_g, target, nt, name):
    m, d = h.shape
    row, vec, mod = _row_specs(d, nt)
    ntl = nt - 1
    tgt = pl.BlockSpec((TM, d), lambda t: ((t // nt) * ntl + jnp.maximum(t % nt, 1) - 1, 0))
    acc = pl.BlockSpec((8, 128), lambda t: (0, 0))

    def body(h_ref, z_ref, gp_ref, ms_ref, t_ref, dh_ref, dz_ref, dvg_ref, ss_ref):
        t = pl.program_id(0)

        @pl.when(t == 0)
        def _():
            ss_ref[...] = jnp.zeros_like(ss_ref)

        latent = (t % nt != 0).astype(F32)
        rz, zh = _rms(z_ref[...])
        gp = gp_ref[...]
        gate = ms_ref[...][i_g:i_g + 1]
        e = h_ref[...] + gate * (zh * gp) - t_ref[...]
        ss_ref[...] += latent * jnp.sum(e * e)
        dh_ = e * (latent / d)
        dh_ref[...] = dh_
        dzh, d_gate, d_gp = _gate_rows(dh_, zh, gp, gate)
        dz_ref[...] = (rz * (dzh - zh * jnp.mean(dzh * zh, axis=1, keepdims=True))).astype(BF16)
        _acc_rows(t, nt, dvg_ref, jnp.concatenate([d_gate, d_gp, jnp.zeros((MOD_ROWS - 2, d), F32)], axis=0))

    return pl.pallas_call(
        body, name=name, grid=(m // TM,), in_specs=[row, row, vec, mod, tgt], out_specs=[row, row, mod, acc],
        out_shape=[jax.ShapeDtypeStruct((m, d), F32), jax.ShapeDtypeStruct((m, d), BF16),
                   jax.ShapeDtypeStruct(modsel.shape, F32), jax.ShapeDtypeStruct((8, 128), F32)],
        compiler_params=_params(("arbitrary",)),
    )(h, z, g_post, modsel, target)


QA, KA, VA, QB, KB, VB = 0, 512, 640, 768, 1280, 1408
PROJ_W = 1536
Q_SCALE = HEAD_DIM ** -0.5


def _swap16(x):
    lane = lax.broadcasted_iota(jnp.int32, x.shape, 1)
    n = x.shape[1]
    return jnp.where((lane % 32) < 16, pltpu.roll(x, n - 16, 1), pltpu.roll(x, 16, 1))


def _seg_mean(x, e):
    hi = x.astype(BF16)
    lo = (x - hi.astype(F32)).astype(BF16)
    return jnp.dot(hi, e, preferred_element_type=F32) + jnp.dot(lo, e, preferred_element_type=F32)


def _rope_tables(t_rows, c_rows):
    s = t_rows - c_rows
    row_ids = jnp.repeat(jnp.arange(s // GRID_W, dtype=jnp.int32), GRID_W).astype(F32)
    col_ids = jnp.tile(jnp.arange(GRID_W, dtype=jnp.int32), s // GRID_W).astype(F32)
    axis_dim = HEAD_DIM // 2
    inv = ROPE_THETA ** (-jnp.arange(0, axis_dim, 2, dtype=F32) / axis_dim)
    ang_r = row_ids[:, None] * inv[None, :]
    ang_c = col_ids[:, None] * inv[None, :]
    cos = jnp.concatenate([jnp.cos(ang_r), jnp.cos(ang_r), jnp.cos(ang_c), jnp.cos(ang_c)], axis=1)
    sin = jnp.concatenate([-jnp.sin(ang_r), jnp.sin(ang_r), -jnp.sin(ang_c), jnp.sin(ang_c)], axis=1)
    cos = jnp.concatenate([jnp.ones((c_rows, HEAD_DIM), F32), cos], axis=0)
    sin = jnp.concatenate([jnp.zeros((c_rows, HEAD_DIM), F32), sin], axis=0)
    return jnp.tile(cos, (1, 8)), jnp.tile(sin, (1, 8))


def _head_mean_matrix():
    i = np.arange(512)
    return jnp.asarray((i[:, None] // HEAD_DIM == i[None, :] // HEAD_DIM).astype(np.float32) / HEAD_DIM, dtype=BF16)


def _interleave_kv(k, v):
    return jnp.concatenate([k[:, :64], v[:, :64], k[:, 64:], v[:, 64:]], axis=1)


def _prep_fwd(proj, qn, kn, cos, sin, emat, nt, name):
    m = proj.shape[0]
    specs = [
        pl.BlockSpec((TM, PROJ_W), lambda t: (t, 0)),
        pl.BlockSpec((1, 512), lambda t: (0, 0)), pl.BlockSpec((1, 128), lambda t: (0, 0)),
        pl.BlockSpec((TM, 512), lambda t: (t % nt, 0)), pl.BlockSpec((TM, 512), lambda t: (t % nt, 0)),
        pl.BlockSpec((512, 512), lambda t: (0, 0)),
    ]

    def body(p_ref, qn_ref, kn_ref, cos_ref, sin_ref, e_ref, q_ref, kv_ref):
        cos_, sin_, e = cos_ref[...], sin_ref[...], e_ref[...]

        def rope(x, w):
            return x * cos_[:, :w] + _swap16(x) * sin_[:, :w]

        def norm(x, g, w):
            return x * lax.rsqrt(_seg_mean(x * x, e[:w, :w]) + EPS) * g

        qa = rope(norm(p_ref[:, QA:QA + 512], qn_ref[...], 512), 512)
        qb = rope(p_ref[:, QB:QB + 512], 512)
        q_ref[:, 0:512] = (qa * Q_SCALE).astype(BF16)
        q_ref[:, 512:1024] = (qb * Q_SCALE).astype(BF16)
        ka = rope(norm(p_ref[:, KA:KA + 128], kn_ref[...], 128), 128)
        kb = rope(p_ref[:, KB:KB + 128], 128)
        kv_ref[:, 0:256] = _interleave_kv(ka, p_ref[:, VA:VA + 128]).astype(BF16)
        kv_ref[:, 256:512] = _interleave_kv(kb, p_ref[:, VB:VB + 128]).astype(BF16)

    return pl.pallas_call(
        body, name=name, grid=(m // TM,), in_specs=specs,
        out_specs=[pl.BlockSpec((TM, 1024), lambda t: (t, 0)), pl.BlockSpec((TM, 512), lambda t: (t, 0))],
        out_shape=[jax.ShapeDtypeStruct((m, 1024), BF16), jax.ShapeDtypeStruct((m, 512), BF16)],
        compiler_params=_params(("parallel",)),
    )(proj, qn, kn, cos, sin, emat)


def _prep_bwd(proj, dq, dkv, qn, kn, cos, sin, emat, nt, name, comm=None):
    m = proj.shape[0]
    specs = [
        pl.BlockSpec((TM, PROJ_W), lambda t: (t, 0)),
        pl.BlockSpec((TM, 1024), lambda t: (t, 0)), pl.BlockSpec((TM, 512), lambda t: (t, 0)),
        pl.BlockSpec((1, 512), lambda t: (0, 0)), pl.BlockSpec((1, 128), lambda t: (0, 0)),
        pl.BlockSpec((TM, 512), lambda t: (t % nt, 0)), pl.BlockSpec((TM, 512), lambda t: (t % nt, 0)),
        pl.BlockSpec((512, 512), lambda t: (0, 0)),
    ]

    def body(p_ref, dq_ref, dkv_ref, qn_ref, kn_ref, cos_ref, sin_ref, e_ref, dp_ref, dqn_ref, dkn_ref):
        t = pl.program_id(0)
        cos_, sin_, e = cos_ref[...], sin_ref[...], e_ref[...]

        @pl.when(t == 0)
        def _():
            dqn_ref[...] = jnp.zeros_like(dqn_ref)
            dkn_ref[...] = jnp.zeros_like(dkn_ref)

        def unrope(dy, w):
            return dy * cos_[:, :w] + _swap16(dy * sin_[:, :w])

        def norm_bwd(x, g, dy, w):
            r = lax.rsqrt(_seg_mean(x * x, e[:w, :w]) + EPS)
            xh = x * r
            dxh = dy * g
            dx = r * (dxh - xh * _seg_mean(dxh * xh, e[:w, :w]))
            return dx, jnp.sum(dy * xh, axis=0, keepdims=True)

        dqa, dgq = norm_bwd(p_ref[:, QA:QA + 512], qn_ref[...], unrope(dq_ref[:, 0:512] * Q_SCALE, 512), 512)
        dp_ref[:, QA:QA + 512] = dqa.astype(BF16)
        dp_ref[:, QB:QB + 512] = unrope(dq_ref[:, 512:1024] * Q_SCALE, 512).astype(BF16)
        da = dkv_ref[:, 0:256]
        db = dkv_ref[:, 256:512]
        dka = jnp.concatenate([da[:, 0:64], da[:, 128:192]], axis=1)
        dva = jnp.concatenate([da[:, 64:128], da[:, 192:256]], axis=1)
        dkb = jnp.concatenate([db[:, 0:64], db[:, 128:192]], axis=1)
        dvb = jnp.concatenate([db[:, 64:128], db[:, 192:256]], axis=1)
        dka, dgk = norm_bwd(p_ref[:, KA:KA + 128], kn_ref[...], unrope(dka, 128), 128)
        dp_ref[:, KA:KA + 128] = dka.astype(BF16)
        dp_ref[:, VA:VA + 128] = dva.astype(BF16)
        dp_ref[:, KB:KB + 128] = unrope(dkb, 128).astype(BF16)
        dp_ref[:, VB:VB + 128] = dvb.astype(BF16)
        dqn_ref[0:1, :] += dgq
        dkn_ref[0:1, :] += dgk

    outs, couts = _call(
        body, name=name, grid=(m // TM,), in_specs=specs,
        out_specs=[pl.BlockSpec((TM, PROJ_W), lambda t: (t, 0)), pl.BlockSpec((8, 512), lambda t: (0, 0)),
                   pl.BlockSpec((8, 128), lambda t: (0, 0))],
        out_shape=[jax.ShapeDtypeStruct((m, PROJ_W), BF16), jax.ShapeDtypeStruct((8, 512), F32),
                   jax.ShapeDtypeStruct((8, 128), F32)],
        args=[proj, dq, dkv, qn, kn, cos, sin, emat], sem=("arbitrary",), comm=comm)
    return outs if comm is None else (outs, couts)


def _attn_case(t, hg, kv_ref, c_rows, t_rows, fn, keys_first=False):
    wl = TM + 2 * WINDOW
    kvd = lambda a, n: kv_ref[pl.ds(a, n), :]
    dense = hg < 2
    ctx = t == 0

    @pl.when(jnp.logical_and(dense, ctx))
    def _():
        kv = kvd(0, c_rows)
        fn(kv[:, :64], kv[:, 64:], None, False, [(0, c_rows)])

    @pl.when(jnp.logical_and(dense, jnp.logical_not(ctx)))
    def _():
        kv = kvd(0, t_rows)
        fn(kv[:, :64], kv[:, 64:], None, False, [(0, t_rows)])

    @pl.when(jnp.logical_and(jnp.logical_not(dense), ctx))
    def _():
        kv = kvd(0, c_rows)
        fn(kv[:, :64], kv[:, 64:], None, True, [(0, c_rows)])

    @pl.when(jnp.logical_and(jnp.logical_not(dense), jnp.logical_not(ctx)))
    def _():
        start = pl.multiple_of(jnp.minimum(c_rows + (t - 1) * TM - WINDOW, t_rows - wl), 128)
        kv = jnp.concatenate([kvd(0, c_rows), kvd(start, wl)], axis=0)
        shape = (c_rows + wl, TM) if keys_first else (TM, c_rows + wl)
        q_i = lax.broadcasted_iota(jnp.int32, shape, 1 if keys_first else 0)
        k_i = lax.broadcasted_iota(jnp.int32, shape, 0 if keys_first else 1)
        qpos = (t - 1) * TM + q_i
        kpos = start - 2 * c_rows + k_i
        mask = jnp.logical_or(k_i < c_rows, jnp.logical_and(jnp.abs(kpos - qpos) <= WINDOW, kpos >= 0))
        fn(kv[:, :64], kv[:, 64:], mask, True, [(0, c_rows), (start, wl)])


def _head_columns(cols):
    lane = lax.broadcasted_iota(jnp.int32, (TM, 128), 1)
    out = jnp.zeros((TM, 128), F32)
    for g, col in enumerate(cols):
        out = jnp.where(lane == g, col, out)
    return out


def _attn_specs(t_rows):
    nt = t_rows // TM
    q_spec = pl.BlockSpec((None, TM, Q_WIDTH), lambda b, hg, t, s: (b, t, hg))
    kv_spec = pl.BlockSpec((None, t_rows, 128), lambda b, hg, t, s: (b, 0, hg))
    lse_spec = pl.BlockSpec((None, TM, 128), lambda b, hg, t, s: (hg, b * nt + t, 0))
    return q_spec, kv_spec, lse_spec


def _attn_fwd(q_all, kv_all, sink8, c_rows, name, comm=None):
    bl, t_rows, _ = q_all.shape
    q_spec, kv_spec, lse_spec = _attn_specs(t_rows)

    def body(sink_ref, q_ref, kv_ref, o_ref, ot_ref, lse_ref):
        hg, t = pl.program_id(1), pl.program_id(2)

        def fn(k, v, mask, use_sink, spans):
            outs, lses = [], []
            v_one = jnp.concatenate([v, jnp.ones(v.shape, BF16)], axis=1)
            def scores(g):
                s = lax.dot_general(q_ref[:, g * 64:(g + 1) * 64], k, (((1,), (1,)), ((), ())),
                                    preferred_element_type=F32)
                return s if mask is None else jnp.where(mask, s, NEG_BIG)

            s_next = scores(0)
            for g in range(GROUP):
                s = s_next
                if g + 1 < GROUP:
                    s_next = scores(g + 1)
                mx = jnp.max(s, axis=1, keepdims=True)
                if use_sink:
                    sink = sink_ref[jnp.maximum(hg - 2, 0) * GROUP + g]
                    mx = jnp.maximum(mx, sink)
                pv = jnp.dot(jnp.exp(s - mx).astype(BF16), v_one, preferred_element_type=F32)
                l = pv[:, 64:65]
                if use_sink:
                    l = l + jnp.exp(sink - mx)
                outs.append(pv[:, :64] * (1.0 / l))
                lses.append(mx + jnp.log(l))
            o = jnp.concatenate(outs, axis=1).astype(BF16)
            o_ref[...] = o
            ot_ref[...] = o.T
            lse_ref[...] = _head_columns(lses)

        _attn_case(t, hg, kv_ref, c_rows, t_rows, fn)

    nt = t_rows // TM
    ot_spec = pl.BlockSpec((Q_WIDTH, TM), lambda b, hg, t, s: (hg, b * nt + t))
    outs, couts = _call(
        body, name=name, grid=(bl, N_HG, nt), in_specs=[q_spec, kv_spec], out_specs=[q_spec, ot_spec, lse_spec],
        out_shape=[jax.ShapeDtypeStruct(q_all.shape, BF16), jax.ShapeDtypeStruct((N_HG * Q_WIDTH, bl * t_rows), BF16),
                   jax.ShapeDtypeStruct((N_HG, bl * t_rows, 128), F32)],
        args=[sink8, q_all, kv_all], prefetch=1, sem=("parallel", "parallel", "arbitrary"), comm=comm)
    return outs if comm is None else (outs, couts)


def _attn_bwd(q_all, kv_all, do, o, lse, sink8, c_rows, name, comm=None):
    bl, t_rows, _ = q_all.shape
    q_spec, kv_spec, lse_spec = _attn_specs(t_rows)
    ds_spec = pl.BlockSpec((None, None, 8, 128), lambda b, hg, t, s: (b, hg, 0, 0))

    def body(sink_ref, q_ref, kv_ref, do_ref, o_ref, lse_ref, dq_ref, dkv_ref, dsk_ref):
        hg, t = pl.program_id(1), pl.program_id(2)

        @pl.when(t == 0)
        def _():
            dkv_ref[...] = jnp.zeros_like(dkv_ref)
            dsk_ref[...] = jnp.zeros_like(dsk_ref)

        lse_rows = lse_ref[...].T
        dd_cols = [jnp.sum(do_ref[:, g * 64:(g + 1) * 64].astype(F32) * o_ref[:, g * 64:(g + 1) * 64].astype(F32),
                           axis=1, keepdims=True) for g in range(GROUP)]
        dd_rows = _head_columns(dd_cols).T

        def fn(k, v, mask, use_sink, spans):
            k_t = k.T
            dq_t, dsinks = [], []
            dk = jnp.zeros(k.shape, F32)
            dv = jnp.zeros(v.shape, F32)
            def products(g):
                s = lax.dot_general(k, q_ref[:, g * 64:(g + 1) * 64], (((1,), (1,)), ((), ())),
                                    preferred_element_type=F32)
                dp = lax.dot_general(v, do_ref[:, g * 64:(g + 1) * 64], (((1,), (1,)), ((), ())),
                                     preferred_element_type=F32)
                return (s if mask is None else jnp.where(mask, s, NEG_BIG)), dp

            nxt = products(0)
            for g in range(GROUP):
                q = q_ref[:, g * 64:(g + 1) * 64]
                do_g = do_ref[:, g * 64:(g + 1) * 64]
                lse_g, dd_g = lse_rows[g:g + 1, :], dd_rows[g:g + 1, :]
                s, dp = nxt
                if g + 1 < GROUP:
                    nxt = products(g + 1)
                pb = jnp.exp(s - lse_g).astype(BF16)
                ds = (pb.astype(F32) * (dp - dd_g)).astype(BF16)
                dk = dk + jnp.dot(ds, q, preferred_element_type=F32)
                dv = dv + jnp.dot(pb, do_g, preferred_element_type=F32)
                dq_t.append(jnp.dot(k_t, ds, preferred_element_type=F32))
                if use_sink:
                    p_sink = jnp.exp(sink_ref[jnp.maximum(hg - 2, 0) * GROUP + g] - lse_g)
                    dsinks.append(jnp.broadcast_to(-jnp.sum(p_sink * dd_g, axis=1, keepdims=True), (1, 128)))
            dq_ref[...] = jnp.concatenate(dq_t, axis=0).T
            dkv = jnp.concatenate([dk, dv], axis=1)
            off = 0
            for start, size in spans:
                dkv_ref[pl.ds(start, size), :] += dkv[off:off + size]
                off += size
            if use_sink:
                dsk_ref[0:GROUP, :] += jnp.concatenate(dsinks, axis=0)

        _attn_case(t, hg, kv_ref, c_rows, t_rows, fn, keys_first=True)

    outs, couts = _call(
        body, name=name, grid=(bl, N_HG, t_rows // TM), in_specs=[q_spec, kv_spec, q_spec, q_spec, lse_spec],
        out_specs=[q_spec, kv_spec, ds_spec],
        out_shape=[jax.ShapeDtypeStruct(q_all.shape, F32), jax.ShapeDtypeStruct(kv_all.shape, F32),
                   jax.ShapeDtypeStruct((bl, N_HG, 8, 128), F32)],
        args=[sink8, q_all, kv_all, do, o, lse], prefetch=1, sem=("parallel", "parallel", "arbitrary"), comm=comm)
    return outs if comm is None else (outs, couts)


def _silu(x):
    return x * jax.nn.sigmoid(x)


def _ada_fwd(c_rows, w_ada, b_cols, name, comm=None):
    nl, d, w = w_ada.shape
    r = c_rows.shape[0]

    def body(c_ref, w_ref, b_ref, o_ref):
        s = _silu(c_ref[...]).astype(BF16)
        o_ref[...] = jnp.dot(s, w_ref[...].astype(BF16), preferred_element_type=F32) + b_ref[...]

    outs, couts = _call(
        body, name=name, grid=(nl,),
        in_specs=[pl.BlockSpec((r, d), lambda l: (0, 0)), pl.BlockSpec((None, d, w), lambda l: (l, 0, 0)),
                  pl.BlockSpec((None, 1, w), lambda l: (l, 0, 0))],
        out_specs=pl.BlockSpec((None, r, w), lambda l: (l, 0, 0)),
        out_shape=jax.ShapeDtypeStruct((nl, r, w), F32), args=[c_rows, w_ada, b_cols], sem=("parallel",), comm=comm)
    return outs if comm is None else (outs, couts)


def _ada_bwd(c_rows, c_ctx, dmod, w_ada, name):
    nl, d, w = w_ada.shape
    r = c_rows.shape[0]

    def body(c_ref, cc_ref, g_ref, w_ref, dw_ref, dc_ref):
        l = pl.program_id(0)
        s = _silu(c_ref[...]).astype(BF16)
        gm = g_ref[...].astype(BF16)
        dw_ref[...] = lax.dot_general(s, gm, (((0,), (0,)), ((), ())), preferred_element_type=F32)
        ds = lax.dot_general(gm, w_ref[...].astype(BF16), (((1,), (1,)), ((), ())), preferred_element_type=F32)
        rows = lax.broadcasted_iota(jnp.int32, ds.shape, 0)
        dsc = jnp.sum(jnp.where(rows % ADA_ROWS == 2, ds, 0.0), axis=0, keepdims=True)
        x = cc_ref[...]
        sg = jax.nn.sigmoid(x)
        dcc = dsc * (sg * (1.0 + x * (1.0 - sg)))
        out = jnp.concatenate([dcc, jnp.zeros((7, d), F32)], axis=0)

        @pl.when(l == 0)
        def _():
            dc_ref[...] = out

        @pl.when(l != 0)
        def _():
            dc_ref[...] += out

    return pl.pallas_call(
        body, name=name, grid=(nl,),
        in_specs=[pl.BlockSpec((r, d), lambda l: (0, 0)), pl.BlockSpec((1, d), lambda l: (0, 0)),
                  pl.BlockSpec((None, r, w), lambda l: (l, 0, 0)), pl.BlockSpec((None, d, w), lambda l: (l, 0, 0))],
        out_specs=[pl.BlockSpec((None, d, w), lambda l: (l, 0, 0)), pl.BlockSpec((8, d), lambda l: (0, 0))],
        out_shape=[jax.ShapeDtypeStruct((nl, d, w), F32), jax.ShapeDtypeStruct((8, d), F32)],
        compiler_params=_params(("arbitrary",)),
    )(c_rows, c_ctx, dmod, w_ada)


def _adam_math(w, g, m, v):
    m = ADAM_B1 * m + (1.0 - ADAM_B1) * g
    v = ADAM_B2 * v + (1.0 - ADAM_B2) * (g * g)
    m_hat = m / (1.0 - ADAM_B1 ** ADAM_STEP)
    v_hat = v / (1.0 - ADAM_B2 ** ADAM_STEP)
    delta = -ADAM_LR * (m_hat / (jnp.sqrt(v_hat) + ADAM_EPS) + ADAM_WD * w)
    return delta, m, v


def _adamw(w, m, v, g_own, g_recv, name, rows=256):
    nl, r, c = w.shape
    tr = min(rows, r)
    spec = pl.BlockSpec((None, tr, c), lambda l, i: (l, i, 0))
    per_layer = isinstance(g_own, (list, tuple))
    own = list(g_own) if per_layer else [g_own]
    recv = [] if g_recv is None else list(g_recv)
    in_specs = [spec] * 3 + [pl.BlockSpec((tr, c), lambda l, i: (i, 0)) if per_layer else spec] * len(own)
    in_specs += [pl.BlockSpec((N_DEV, tr, c), lambda l, i: (0, i, 0))] * len(recv)

    def body(*refs):
        w_ref, m_ref, v_ref = refs[:3]
        own_refs, recv_refs = refs[3:3 + len(own)], refs[3 + len(own):3 + len(own) + len(recv)]
        go_ref, d_ref, mo_ref, vo_ref = refs[-4:]

        def update(li):
            g = own_refs[li][...].astype(F32)
            if recv:
                for k in range(1, N_DEV):
                    g = g + recv_refs[li][k].astype(F32)
            delta, m_, v_ = _adam_math(w_ref[...], g, m_ref[...], v_ref[...])
            go_ref[...] = g
            d_ref[...] = delta
            mo_ref[...] = m_
            vo_ref[...] = v_

        if per_layer:
            for li in range(nl):
                pl.when(pl.program_id(0) == li)(functools.partial(update, li))
        else:
            update(0)

    return pl.pallas_call(
        body, name=name, grid=(nl, r // tr), in_specs=in_specs, out_specs=[spec] * 4,
        out_shape=[jax.ShapeDtypeStruct(w.shape, F32)] * 4, compiler_params=_params(("parallel", "parallel")),
    )(w, m, v, *own, *recv)


def _small_adamw(w, m, v, g_all, name):
    def body(w_ref, m_ref, v_ref, g_ref, go_ref, d_ref, mo_ref, vo_ref):
        g = g_ref[0]
        for k in range(1, N_DEV):
            g = g + g_ref[k]
        delta, m_, v_ = _adam_math(w_ref[...], g, m_ref[...], v_ref[...])
        go_ref[...] = g
        d_ref[...] = delta
        mo_ref[...] = m_
        vo_ref[...] = v_

    return pl.pallas_call(
        body, name=name, out_shape=[jax.ShapeDtypeStruct(w.shape, F32)] * 4, compiler_params=_params(),
    )(w, m, v, g_all)


SMALL = ("c_ctx", "b_ada", "g_pre_mix", "g_post_mix", "g_pre_mlp", "g_post_mlp", "q_norm", "k_norm", "sink")


def _pack_small(parts):
    flat = jnp.concatenate([parts[n].reshape(-1) for n in SMALL])
    rows = -(-flat.shape[0] // 1024) * 8
    return jnp.pad(flat, (0, rows * 128 - flat.shape[0])).reshape(rows, 128)


def _unpack_small(packed, like):
    flat = packed.reshape(-1)
    out, off = {}, 0
    for n in SMALL:
        size = int(np.prod(like[n].shape))
        out[n] = flat[off:off + size].reshape(like[n].shape)
        off += size
    return out


def kernel(x, c, ctx, c_ctx, w_ada, b_ada, g_pre_mix, g_post_mix, g_pre_mlp, g_post_mlp, w_in, q_norm, k_norm, sink, w_out, w_up, w_down, loss_target, m_c_ctx, m_w_ada, m_b_ada, m_g_pre_mix, m_g_post_mix, m_g_pre_mlp, m_g_post_mlp, m_w_in, m_q_norm, m_k_norm, m_sink, m_w_out, m_w_up, m_w_down, v_c_ctx, v_w_ada, v_b_ada, v_g_pre_mix, v_g_post_mix, v_g_pre_mlp, v_g_post_mlp, v_w_in, v_q_norm, v_k_norm, v_sink, v_w_out, v_w_up, v_w_down):
    bl, s_rows, d = x.shape
    c_rows = ctx.shape[1]
    assert c_rows == TM and s_rows % TM == 0 and bl == 2
    t_rows = c_rows + s_rows
    nt = t_rows // TM
    m_rows = bl * t_rows
    nl = w_in.shape[0]
    ada_w = w_ada.shape[2]
    d_ff = w_up.shape[2] * N_DEV
    me = _my_index()

    shard = lambda w_, l: w_[l].astype(BF16)
    c_pad = jnp.concatenate([c, c_ctx[None, :], jnp.zeros((ADA_ROWS - bl - 1, d), F32)], axis=0)
    c_all, = _comm_only(_Comm([(c_pad, GATHER)]), "gather_c")
    c_all = c_all.reshape(N_DEV * ADA_ROWS, d)
    gathered = {0: {}}

    def layer_weights(l):
        g_ = gathered[l]
        w_out_f = g_["w_out"].reshape(-1, d)
        w_down_f = g_["w_down"].reshape(d_ff, d)
        return dict(
            w_in_t=g_["w_in"].transpose(0, 2, 1).reshape(PROJ_W, d), w_out_f=w_out_f, w_out_t=w_out_f.T,
            w_up_s=g_["w_up"], w_up_t=g_["w_up"].transpose(0, 2, 1).reshape(d_ff, d),
            w_down_f=w_down_f, w_down_t=w_down_f.T)

    big = dict(tm=1536, tn=512)
    deep = dict(tm=1536, tn=256, tk=d_ff)
    wide = dict(tm=1024, tn=256, tk=m_rows)

    b_cols = lax.dynamic_slice(b_ada, (0, me * ada_w), (nl, ada_w))[:, None, :]
    mod_cols, (gathered[0]["w_in"],) = _ada_fwd(c_all, w_ada, b_cols, "ada_fwd",
                                                comm=_Comm([(shard(w_in, 0), GATHER)]))
    mod_slots = mod_cols.reshape(nl, N_DEV, ADA_ROWS, ada_w).transpose(1, 0, 2, 3)
    mod_g, = _comm_only(_Comm([(mod_slots, TO_OWNER)]), "exchange_mod")
    mine = mod_g.transpose(1, 2, 0, 3).reshape(nl, ADA_ROWS, N_MOD, d)
    pad = jnp.zeros((bl, 2, MOD_ROWS - N_MOD, d), F32)
    modsel = [jnp.concatenate([jnp.stack([jnp.broadcast_to(mine[l, bl], (bl, N_MOD, d)), mine[l, :bl]], axis=1), pad],
                              axis=2) for l in range(nl)]

    cos, sin = _rope_tables(t_rows, c_rows)
    emat = _head_mean_matrix()
    row = lambda a: a[None, :]
    qn = [jnp.tile(q_norm[l], 8)[None, :] for l in range(nl)]
    kn = [jnp.tile(k_norm[l], 2)[None, :] for l in range(nl)]

    h = jnp.concatenate([ctx, x], axis=1).reshape(m_rows, d)
    target = loss_target.reshape(bl * s_rows, d)
    saved = []
    weights_of = {}
    (u, u_t), (gathered[0]["w_out"],) = _norm_mod_fwd(
        h, row(g_pre_mix[0]), modsel[0], 0, 1, nt, "mix_mod_fwd0", comm=_Comm([(shard(w_out, 0), GATHER)]))
    for l in range(nl):
        w_in_f = gathered[l]["w_in"].transpose(1, 0, 2).reshape(d, PROJ_W)
        proj = _mm(u, w_in_f, name=f"mm_in{l}", tk=d, **big)
        q_all, kv_all = _prep_fwd(proj, qn[l], kn[l], cos, sin, emat, nt, f"prep_fwd{l}")
        (o, o_t, lse), (w_up_g, w_down_g) = _attn_fwd(
            q_all.reshape(bl, t_rows, 1024), kv_all.reshape(bl, t_rows, 512), sink[l], c_rows, f"attn_fwd{l}",
            comm=_Comm([(shard(w_up, l), GATHER), (shard(w_down, l), GATHER)]))
        o = o.reshape(m_rows, 1024)
        gathered[l].update(w_up=w_up_g, w_down=w_down_g)
        wl = weights_of[l] = layer_weights(l)
        mix = _mm(o, wl["w_out_f"], name=f"mm_out{l}", tk=1024, **big)
        h_mid, v_in, v_t = _gate_mod_fwd(h, mix, row(g_post_mix[l]), modsel[l], 2, row(g_pre_mlp[l]), modsel[l], 3, 4,
                                         nt, f"mix_gate_mlp_mod_fwd{l}")
        more = l + 1 < nl
        res_up = _mm(v_in, wl["w_up_s"], name=f"mm_up{l}", b_mode="nn_slots", epilogue="relu2", tk=d,
                     comm=_Comm([(shard(w_in, l + 1), GATHER)]) if more else None, **big)
        (r_act, r_t), nxt_in = res_up if more else (res_up, None)
        res_down = _mm(r_act, wl["w_down_f"], name=f"mm_down{l}",
                       comm=_Comm([(shard(w_out, l + 1), GATHER)]) if more else None, **deep)
        y, nxt_out = res_down if more else (res_down, None)
        saved.append((h, u_t, proj, q_all, kv_all, o, o_t, lse, mix, h_mid, v_t, r_act, r_t, y))
        if more:
            gathered[l + 1] = dict(w_in=nxt_in[0], w_out=nxt_out[0])
            h, u, u_t = _gate_mod_fwd(h_mid, y, row(g_post_mlp[l]), modsel[l], 5, row(g_pre_mix[l + 1]), modsel[l + 1],
                                      0, 1, nt, f"mlp_gate_mix_mod_fwd{l}")

    dh, dy, dvec_g2, ss = _gate_loss_bwd(h_mid, y, row(g_post_mlp[nl - 1]), modsel[nl - 1], 5, target, nt, "gate_loss_bwd")
    loss = lax.psum(0.5 * ss[0, 0] / d, MESH_AXES)
    small_g = {n: [None] * nl for n in SMALL if n not in ("c_ctx", "b_ada")}
    dvecs = {l: {} for l in range(nl)}
    dvecs[nl - 1]["g2"] = dvec_g2
    slots = {n: [None] * nl for n in ("w_in", "w_out", "w_up", "w_down")}
    recvd = {n: [None] * nl for n in slots}
    send = lambda n, l_: (slots[n][l_], TO_OWNER_XOR)
    for l in reversed(range(nl)):
        h_in, u_t, proj, q_all, kv_all, o, o_t, lse, mix, h_mid, v_t, r_act, r_t, y = saved[l]
        wl = weights_of[l]
        later = l + 1 < nl
        res = _mm(dy, wl["w_down_t"], name=f"mm_da{l}", epilogue="relu2_bwd", extra=r_act, out_dtype=BF16, tk=d,
                  comm=_Comm([send("w_out", l + 1)]) if later else None, **big)
        da = res[0] if later else res
        if later:
            recvd["w_out"][l + 1] = res[1][0]
        res = _mm(r_t, dy, name=f"mm_dw_down{l}", out_dtype=BF16,
                  comm=_Comm([send("w_in", l + 1)]) if later else None, **wide)
        dw_down = res[0] if later else res
        if later:
            recvd["w_in"][l + 1] = res[1][0]
        dw_up = _mm(v_t, da, name=f"mm_dw_up{l}", out_mode="slots", out_dtype=BF16, **wide)
        slots["w_down"][l] = dw_down.reshape(N_DEV, -1, d)
        slots["w_up"][l] = dw_up
        dv = _mm(da, wl["w_up_t"], name=f"mm_dv{l}", **deep)
        dh, dvecs[l]["m2"], dmix, dvecs[l]["g1"] = _mod_gate_bwd(
            h_mid, row(g_pre_mlp[l]), modsel[l], 3, 4, dv, dh, mix, row(g_post_mix[l]), modsel[l], 2, nt,
            f"mlp_mod_mix_gate_bwd{l}")
        do = _mm(dmix, wl["w_out_t"], name=f"mm_do{l}", out_dtype=BF16, tk=d, **big)
        dw_out = _mm(o_t, dmix, name=f"mm_dw_out{l}", out_dtype=BF16, **wide)
        slots["w_out"][l] = dw_out.reshape(N_DEV, -1, d)
        (dq, dkv, dsk), (recvd["w_down"][l], recvd["w_up"][l]) = _attn_bwd(
            q_all.reshape(bl, t_rows, 1024), kv_all.reshape(bl, t_rows, 512), do.reshape(bl, t_rows, 1024),
            o.reshape(bl, t_rows, 1024), lse, sink[l], c_rows, f"attn_bwd{l}",
            comm=_Comm([send("w_down", l), send("w_up", l)]))
        last = l == 0
        res = _prep_bwd(proj, dq.reshape(m_rows, 1024), dkv.reshape(m_rows, 512), qn[l], kn[l], cos, sin, emat, nt,
                        f"prep_bwd{l}", comm=_Comm([send("w_out", l)]) if last else None)
        dproj, dqn, dkn = res[0] if last else res
        if last:
            recvd["w_out"][l] = res[1][0]
        dw_in = _mm(u_t, dproj, name=f"mm_dw_in{l}", out_dtype=BF16, **wide)
        slots["w_in"][l] = dw_in.reshape(d, N_DEV, PROJ_W // N_DEV).transpose(1, 0, 2)
        res = _mm(dproj, wl["w_in_t"], name=f"mm_du{l}", tk=PROJ_W,
                  comm=_Comm([send("w_in", l)]) if last else None, **big)
        du = res[0] if last else res
        if last:
            recvd["w_in"][l] = res[1][0]
            dh, dvecs[l]["m1"] = _norm_mod_bwd(h_in, row(g_pre_mix[l]), modsel[l], du, dh, 0, 1, nt, f"mix_mod_bwd{l}",
                                               latent_only=True)
        else:
            dh, dvecs[l]["m1"], dy, dvecs[l - 1]["g2"] = _mod_gate_bwd(
                h_in, row(g_pre_mix[l]), modsel[l], 0, 1, du, dh, saved[l - 1][-1], row(g_post_mlp[l - 1]),
                modsel[l - 1], 5, nt, f"mix_mod_mlp_gate_bwd{l}")
        small_g["q_norm"][l] = jnp.sum(dqn[0].reshape(8, HEAD_DIM), axis=0)
        small_g["k_norm"][l] = jnp.sum(dkn[0].reshape(2, HEAD_DIM), axis=0)
        small_g["sink"][l] = jnp.sum(dsk[:, 2:, :GROUP, 0], axis=0).reshape(-1)

    dmod_rows = []
    for l in range(nl):
        m1, g1, m2, g2 = (dvecs[l][n] for n in ("m1", "g1", "m2", "g2"))
        small_g["g_pre_mix"][l] = jnp.sum(m1[:, :, 2], axis=(0, 1))
        small_g["g_post_mix"][l] = jnp.sum(g1[:, :, 1], axis=(0, 1))
        small_g["g_pre_mlp"][l] = jnp.sum(m2[:, :, 2], axis=(0, 1))
        small_g["g_post_mlp"][l] = jnp.sum(g2[:, :, 1], axis=(0, 1))
        dms = jnp.stack([m1[:, :, 0], m1[:, :, 1], g1[:, :, 0], m2[:, :, 0], m2[:, :, 1], g2[:, :, 0]], axis=2)
        rows = jnp.concatenate([dms[:, 1], jnp.sum(dms[:, 0], axis=0)[None]], axis=0)
        dmod_rows.append(jnp.pad(rows.reshape(bl + 1, N_MOD * d), ((0, ADA_ROWS - bl - 1), (0, 0))))
    grad_x = dh.reshape(bl, s_rows, d)

    dmod_slots = jnp.stack(dmod_rows).reshape(nl, ADA_ROWS, N_DEV, ada_w).transpose(2, 0, 1, 3)
    dmod_g, = _comm_only(_Comm([(dmod_slots, TO_OWNER)]), "exchange_dmod")
    dmod_mine = dmod_g.transpose(1, 0, 2, 3).reshape(nl, N_DEV * ADA_ROWS, ada_w)
    dw_ada, dcc = _ada_bwd(c_all, c_ctx[None, :], dmod_mine, w_ada, "ada_bwd")

    parts = {n: jnp.stack(small_g[n]) for n in small_g}
    parts["c_ctx"] = dcc[0]
    parts["b_ada"] = jnp.stack([jnp.sum(r_[: bl + 1], axis=0) for r_ in dmod_rows])
    weights = dict(c_ctx=c_ctx, b_ada=b_ada, g_pre_mix=g_pre_mix, g_post_mix=g_post_mix, g_pre_mlp=g_pre_mlp,
                   g_post_mlp=g_post_mlp, q_norm=q_norm, k_norm=k_norm, sink=sink)
    moms = dict(c_ctx=m_c_ctx, b_ada=m_b_ada, g_pre_mix=m_g_pre_mix, g_post_mix=m_g_post_mix, g_pre_mlp=m_g_pre_mlp,
                g_post_mlp=m_g_post_mlp, q_norm=m_q_norm, k_norm=m_k_norm, sink=m_sink)
    vels = dict(c_ctx=v_c_ctx, b_ada=v_b_ada, g_pre_mix=v_g_pre_mix, g_post_mix=v_g_post_mix, g_pre_mlp=v_g_pre_mlp,
                g_post_mlp=v_g_post_mlp, q_norm=v_q_norm, k_norm=v_k_norm, sink=v_sink)
    small_all, = _comm_only(_Comm([(_pack_small(parts), GATHER)]), "gather_small")
    s_out = _small_adamw(_pack_small(weights), _pack_small(moms), _pack_small(vels), small_all, "adamw_small")
    s_g, s_d, s_m, s_v = [_unpack_small(a, weights) for a in s_out]

    res = {}
    for n, w_, m_, v_ in (("w_in", w_in, m_w_in, v_w_in), ("w_out", w_out, m_w_out, v_w_out),
                          ("w_up", w_up, m_w_up, v_w_up), ("w_down", w_down, m_w_down, v_w_down)):
        own = [lax.dynamic_index_in_dim(slots[n][l], me, axis=0, keepdims=False) for l in range(nl)]
        res[n] = _adamw(w_, m_, v_, own, recvd[n], f"adamw_{n}")
    res["w_ada"] = _adamw(w_ada, m_w_ada, v_w_ada, dw_ada, None, "adamw_w_ada")

    order = ("c_ctx", "w_ada", "b_ada", "g_pre_mix", "g_post_mix", "g_pre_mlp", "g_post_mlp", "w_in", "q_norm",
             "k_norm", "sink", "w_out", "w_up", "w_down")
    outs = [loss, grad_x]
    for i, small in enumerate((s_g, s_d, s_m, s_v)):
        outs += [small[n] if n in small else res[n][i] for n in order]
    return tuple(outs)
```

```python
import functools

import jax
import jax.numpy as jnp
import numpy as np
from jax import lax
from jax.experimental import pallas as pl
from jax.experimental.pallas import tpu as pltpu

F32 = jnp.float32
BF16 = jnp.bfloat16

HEAD_DIM = 64
GROUP = 4
N_HG = 4
Q_WIDTH = GROUP * HEAD_DIM
WINDOW = 128
GRID_W = 64
ROPE_THETA = 10000.0
EPS = 1e-6
NEG_BIG = -1e30
N_MOD = 6
MOD_ROWS = 8
TM = 256
N_DEV = 8
ADA_ROWS = 8
VMEM_LIMIT = 56 * 1024 * 1024

ADAM_LR = 0.001
ADAM_B1 = 0.9
ADAM_B2 = 0.999
ADAM_EPS = 1e-08
ADAM_WD = 0.01
ADAM_STEP = 10

MESH_AXES = ("x", "y", "c")


def _params(sem=None):
    kw = dict(vmem_limit_bytes=VMEM_LIMIT)
    if sem is not None:
        kw["dimension_semantics"] = sem
    return pltpu.CompilerParams(**kw)


def _my_index():
    return 4 * lax.axis_index("x") + 2 * lax.axis_index("y") + lax.axis_index("c")


def _peer(k):
    x, y, c = lax.axis_index("x"), lax.axis_index("y"), lax.axis_index("c")
    kx, ky, kc = (k >> 2) & 1, (k >> 1) & 1, k & 1
    px = (1 - x) if kx else x
    py = (1 - y) if ky else y
    pc = (1 - c) if kc else c
    return (px, py, pc), 4 * px + 2 * py + pc


GATHER, TO_OWNER, TO_OWNER_XOR = "gather", "to_owner", "to_owner_xor"


class _Comm:
    def __init__(self, items):
        self.items = list(items)
        self.arrays = [a for a, _ in self.items]

    def out_shapes(self):
        return [jax.ShapeDtypeStruct(((N_DEV,) + a.shape) if kind == GATHER else a.shape, a.dtype)
                for a, kind in self.items]

    def sem_shapes(self):
        n = len(self.items) * N_DEV
        return [pltpu.SemaphoreType.DMA((n,)), pltpu.SemaphoreType.DMA((n,))]

    def _copies(self, in_refs, out_refs, send_sems, recv_sems):
        me = _my_index()
        local, remote = [], []
        for i, ((_, kind), x_ref, o_ref) in enumerate(zip(self.items, in_refs, out_refs)):
            base = i * N_DEV
            own_src = x_ref if kind == GATHER else x_ref.at[me]
            own_dst = o_ref.at[0] if kind == TO_OWNER_XOR else o_ref.at[me]
            local.append(pltpu.make_async_copy(own_src, own_dst, send_sems.at[base]))
            for k in range(1, N_DEV):
                peer, pidx = _peer(k)
                remote.append(pltpu.make_async_remote_copy(
                    src_ref=x_ref if kind == GATHER else x_ref.at[pidx],
                    dst_ref=o_ref.at[k] if kind == TO_OWNER_XOR else o_ref.at[me],
                    send_sem=send_sems.at[base + k], recv_sem=recv_sems.at[base + k],
                    device_id=peer, device_id_type=pl.DeviceIdType.MESH))
        return local, remote

    def start(self, in_refs, out_refs, send_sems, recv_sems):
        local, remote = self._copies(in_refs, out_refs, send_sems, recv_sems)
        for cp in local + remote:
            cp.start()

    def wait(self, in_refs, out_refs, send_sems, recv_sems):
        local, remote = self._copies(in_refs, out_refs, send_sems, recv_sems)
        for cp in remote:
            cp.wait_recv()
        for cp in remote:
            cp.wait_send()
        for cp in local:
            cp.wait()


def _call(body, *, name, grid, in_specs, out_specs, out_shape, args, scratch_shapes=(), prefetch=0, sem=None,
          comm=None):
    single = not isinstance(out_shape, (list, tuple))
    out_shape = [out_shape] if single else list(out_shape)
    out_specs = [out_specs] if single else list(out_specs)
    in_specs, scratch_shapes, args = list(in_specs), list(scratch_shapes), list(args)
    n_in, n_out = len(in_specs), len(out_shape)
    if comm is not None:
        nc = len(comm.arrays)
        hbm = pl.BlockSpec(memory_space=pl.ANY)
        inner = body

        def body(*refs):
            pre, r = refs[:prefetch], refs[prefetch:]
            ins, cin = r[:n_in], r[n_in:n_in + nc]
            outs, cout = r[n_in + nc:n_in + nc + n_out], r[n_in + nc + n_out:n_in + 2 * nc + n_out]
            scr, sems = r[n_in + 2 * nc + n_out:len(r) - 2], r[len(r) - 2:]
            ids = [pl.program_id(i) for i in range(len(grid))]

            def when(flags, fn):
                if flags:
                    pl.when(functools.reduce(jnp.logical_and, flags))(fn)
                else:
                    fn()

            when([i == 0 for i in ids], lambda: comm.start(cin, cout, *sems))
            inner(*pre, *ins, *outs, *scr)
            when([i == n - 1 for i, n in zip(ids, grid)], lambda: comm.wait(cin, cout, *sems))

        in_specs += [hbm] * nc
        out_specs += [hbm] * nc
        out_shape += comm.out_shapes()
        scratch_shapes += comm.sem_shapes()
        args += comm.arrays
        sem = ("arbitrary",) * len(grid)
    kw = dict(name=name, out_shape=out_shape, compiler_params=_params(sem if grid else None))
    if prefetch:
        kw["grid_spec"] = pltpu.PrefetchScalarGridSpec(
            num_scalar_prefetch=prefetch, grid=grid, in_specs=in_specs, out_specs=out_specs,
            scratch_shapes=scratch_shapes)
    else:
        kw.update(in_specs=in_specs, out_specs=out_specs, scratch_shapes=scratch_shapes)
        if grid:
            kw["grid"] = grid
    res = list(pl.pallas_call(body, **kw)(*args))
    outs = res[:n_out]
    return (outs[0] if single else outs), res[n_out:]


def _comm_only(comm, name):
    return _call(lambda: None, name=name, grid=(), in_specs=[], out_specs=[], out_shape=[], args=[], comm=comm)[1]


def _mm(a, b, *, name, ta=False, b_mode="nn", out_mode="plain", out_dtype=F32, tm=512, tn=512, tk=512,
        epilogue=None, extra=None, comm=None):
    if ta:
        kdim, m = a.shape
    else:
        m, kdim = a.shape
    if b_mode == "nn":
        n = b.shape[1]
    elif b_mode == "nt":
        n = b.shape[0]
    elif b_mode == "nn_slots":
        n = b.shape[0] * b.shape[2]
        tn = b.shape[2]
    else:
        n = b.shape[1]
        tk = b.shape[2]
    if out_mode == "slots":
        tn = n // N_DEV
    tm, tn, tk = min(tm, m), min(tn, n), min(tk, kdim)
    assert m % tm == 0 and n % tn == 0 and kdim % tk == 0, (name, m, n, kdim, tm, tn, tk)
    nk = kdim // tk

    a_spec = pl.BlockSpec((tk, tm), lambda i, j, k: (k, i)) if ta else pl.BlockSpec((tm, tk), lambda i, j, k: (i, k))
    if b_mode == "nn":
        b_spec = pl.BlockSpec((tk, tn), lambda i, j, k: (k, j))
    elif b_mode == "nt":
        b_spec = pl.BlockSpec((tn, tk), lambda i, j, k: (j, k))
    elif b_mode == "nn_slots":
        b_spec = pl.BlockSpec((None, tk, tn), lambda i, j, k: (j, k, 0))
    else:
        b_spec = pl.BlockSpec((None, tn, tk), lambda i, j, k: (k, j, 0))
    tb = b_mode in ("nt", "nt_slots")
    if out_mode == "plain":
        o_shape, o_spec = (m, n), pl.BlockSpec((tm, tn), lambda i, j, k: (i, j))
    else:
        o_shape, o_spec = (N_DEV, m, tn), pl.BlockSpec((None, tm, tn), lambda i, j, k: (j, i, 0))
    dims = (((0 if ta else 1,), (1 if tb else 0,)), ((), ()))

    in_specs = [a_spec, b_spec]
    args = [a, b]
    if epilogue == "relu2_bwd":
        in_specs.append(pl.BlockSpec((tm, tn), lambda i, j, k: (i, j)))
        args.append(extra)
    if epilogue == "relu2":
        out_shape = [jax.ShapeDtypeStruct(o_shape, BF16), jax.ShapeDtypeStruct((n, m), BF16)]
        out_specs = [o_spec, pl.BlockSpec((tn, tm), lambda i, j, k: (j, i))]
    else:
        out_shape = jax.ShapeDtypeStruct(o_shape, out_dtype)
        out_specs = o_spec

    def finish(refs, acc):
        if epilogue == "relu2":
            r = jnp.maximum(acc, 0.0)
            r2 = (r * r).astype(BF16)
            refs[2][...] = r2
            refs[3][...] = r2.T
        elif epilogue == "relu2_bwd":
            refs[3][...] = (acc * (2.0 * jnp.sqrt(refs[2][...].astype(F32)))).astype(out_dtype)
        else:
            refs[2][...] = acc.astype(out_dtype)

    def body(*refs):
        part = lax.dot_general(refs[0][...], refs[1][...], dims, preferred_element_type=F32)
        if nk == 1:
            finish(refs, part)
            return
        acc_ref = refs[-1]
        k = pl.program_id(2)

        @pl.when(k == 0)
        def _():
            acc_ref[...] = part

        @pl.when(jnp.logical_and(k > 0, k < nk - 1))
        def _():
            acc_ref[...] += part

        @pl.when(k == nk - 1)
        def _():
            finish(refs, acc_ref[...] + part)

    outs, couts = _call(
        body, name=name, grid=(m // tm, n // tn, nk), in_specs=in_specs, out_specs=out_specs, out_shape=out_shape,
        args=args, scratch_shapes=[] if nk == 1 else [pltpu.VMEM((tm, tn), F32)],
        sem=("parallel", "parallel", "arbitrary"), comm=comm)
    return outs if comm is None else (outs, couts)


def _row_specs(d, nt):
    row = pl.BlockSpec((TM, d), lambda t: (t, 0))
    vec = pl.BlockSpec((1, d), lambda t: (0, 0))
    mod = pl.BlockSpec((None, None, MOD_ROWS, d), lambda t: (t // nt, jnp.minimum(t % nt, 1), 0, 0))
    return row, vec, mod


def _rms(x):
    r = lax.rsqrt(jnp.mean(x * x, axis=1, keepdims=True) + EPS)
    return r, x * r


def _token_stream(ctx, x, nt, name, comm=None):
    bl, c_rows, d = ctx.shape
    m = bl * nt * TM

    def body(ctx_ref, x_ref, h_ref):
        t = pl.program_id(0)

        @pl.when(t % nt == 0)
        def _():
            h_ref[...] = ctx_ref[...]

        @pl.when(t % nt != 0)
        def _():
            h_ref[...] = x_ref[...]

    outs, couts = _call(
        body, name=name, grid=(m // TM,),
        in_specs=[pl.BlockSpec((None, TM, d), lambda t: (t // nt, 0, 0)),
                  pl.BlockSpec((None, TM, d), lambda t: (t // nt, jnp.maximum(t % nt, 1) - 1, 0))],
        out_specs=pl.BlockSpec((TM, d), lambda t: (t, 0)), out_shape=jax.ShapeDtypeStruct((m, d), F32),
        args=[ctx, x], sem=("parallel",), comm=comm)
    return outs if comm is None else (outs, couts)


def _norm_mod_fwd(h, g, modsel, i_sh, i_sc, nt, name, comm=None):
    m, d = h.shape
    row, vec, mod = _row_specs(d, nt)

    def body(h_ref, g_ref, ms_ref, u_ref, ut_ref):
        _, xh = _rms(h_ref[...])
        ms = ms_ref[...]
        u = (xh * g_ref[...] * (1.0 + ms[i_sc:i_sc + 1]) + ms[i_sh:i_sh + 1]).astype(BF16)
        u_ref[...] = u
        ut_ref[...] = u.T

    outs, couts = _call(
        body, name=name, grid=(m // TM,), in_specs=[row, vec, mod],
        out_specs=[row, pl.BlockSpec((d, TM), lambda t: (0, t))],
        out_shape=[jax.ShapeDtypeStruct((m, d), BF16), jax.ShapeDtypeStruct((d, m), BF16)],
        args=[h, g, modsel], sem=("parallel",), comm=comm)
    return outs if comm is None else (outs, couts)


def _acc_rows(t, nt, dvec_ref, rows):
    first = (t % nt) <= 1

    @pl.when(first)
    def _():
        dvec_ref[...] = rows

    @pl.when(jnp.logical_not(first))
    def _():
        dvec_ref[...] += rows


def _norm_mod_bwd(h, g, modsel, du, dh_in, i_sh, i_sc, nt, name, comm=None, latent_only=False):
    m, d = h.shape
    row, vec, mod = _row_specs(d, nt)
    dh_rows, dh_spec = m, row
    if latent_only:
        dh_rows = m // nt * (nt - 1)
        dh_spec = pl.BlockSpec((TM, d), lambda t: ((t // nt) * (nt - 1) + jnp.maximum(t % nt, 1) - 1, 0))

    def body(h_ref, g_ref, ms_ref, du_ref, dhi_ref, dh_ref, dvec_ref):
        t = pl.program_id(0)
        r, xh = _rms(h_ref[...])
        g_ = g_ref[...]
        ms = ms_ref[...]
        du_ = du_ref[...]
        y = xh * g_
        dy = du_ * (1.0 + ms[i_sc:i_sc + 1])
        dxh = dy * g_
        dx = r * (dxh - xh * jnp.mean(dxh * xh, axis=1, keepdims=True))
        dh_ref[...] = dhi_ref[...] + dx
        rows = jnp.concatenate([
            jnp.sum(du_, axis=0, keepdims=True), jnp.sum(du_ * y, axis=0, keepdims=True),
            jnp.sum(dy * xh, axis=0, keepdims=True), jnp.zeros((MOD_ROWS - 3, d), F32)], axis=0)
        _acc_rows(t, nt, dvec_ref, rows)

    outs, couts = _call(
        body, name=name, grid=(m // TM,), in_specs=[row, vec, mod, row, row], out_specs=[dh_spec, mod],
        out_shape=[jax.ShapeDtypeStruct((dh_rows, d), F32), jax.ShapeDtypeStruct(modsel.shape, F32)],
        args=[h, g, modsel, du, dh_in], sem=("arbitrary",), comm=comm)
    return outs if comm is None else (outs, couts)


def _gate_rows(dh_, xh, g_, gate):
    dy = dh_ * gate
    return dy * g_, jnp.sum(dh_ * (xh * g_), axis=0, keepdims=True), jnp.sum(dy * xh, axis=0, keepdims=True)


def _gate_mod_fwd(h, z, g_post, ms_gate, i_g, g_pre, ms_mod, i_sh, i_sc, nt, name):
    m, d = h.shape
    row, vec, mod = _row_specs(d, nt)

    def body(h_ref, z_ref, gp_ref, msg_ref, gq_ref, msm_ref, hn_ref, u_ref, ut_ref):
        _, zh = _rms(z_ref[...])
        hn = h_ref[...] + msg_ref[...][i_g:i_g + 1] * (zh * gp_ref[...])
        hn_ref[...] = hn
        _, xh = _rms(hn)
        ms = msm_ref[...]
        u = (xh * gq_ref[...] * (1.0 + ms[i_sc:i_sc + 1]) + ms[i_sh:i_sh + 1]).astype(BF16)
        u_ref[...] = u
        ut_ref[...] = u.T

    return pl.pallas_call(
        body, name=name, grid=(m // TM,), in_specs=[row, row, vec, mod, vec, mod],
        out_specs=[row, row, pl.BlockSpec((d, TM), lambda t: (0, t))],
        out_shape=[jax.ShapeDtypeStruct((m, d), F32), jax.ShapeDtypeStruct((m, d), BF16),
                   jax.ShapeDtypeStruct((d, m), BF16)],
        compiler_params=_params(("parallel",)),
    )(h, z, g_post, ms_gate, g_pre, ms_mod)


def _mod_gate_bwd(h, g_pre, ms_mod, i_sh, i_sc, du, dh_in, z, g_post, ms_gate, i_g, nt, name):
    m, d = h.shape
    row, vec, mod = _row_specs(d, nt)

    def body(h_ref, gq_ref, msm_ref, du_ref, dhi_ref, z_ref, gp_ref, msg_ref, dh_ref, dvm_ref, dz_ref, dvg_ref):
        t = pl.program_id(0)
        r, xh = _rms(h_ref[...])
        gq = gq_ref[...]
        ms = msm_ref[...]
        du_ = du_ref[...]
        dy = du_ * (1.0 + ms[i_sc:i_sc + 1])
        dxh = dy * gq
        dh_ = dhi_ref[...] + r * (dxh - xh * jnp.mean(dxh * xh, axis=1, keepdims=True))
        dh_ref[...] = dh_
        _acc_rows(t, nt, dvm_ref, jnp.concatenate([
            jnp.sum(du_, axis=0, keepdims=True), jnp.sum(du_ * (xh * gq), axis=0, keepdims=True),
            jnp.sum(dy * xh, axis=0, keepdims=True), jnp.zeros((MOD_ROWS - 3, d), F32)], axis=0))
        rz, zh = _rms(z_ref[...])
        dzh, d_gate, d_gp = _gate_rows(dh_, zh, gp_ref[...], msg_ref[...][i_g:i_g + 1])
        dz_ref[...] = (rz * (dzh - zh * jnp.mean(dzh * zh, axis=1, keepdims=True))).astype(BF16)
        _acc_rows(t, nt, dvg_ref, jnp.concatenate([d_gate, d_gp, jnp.zeros((MOD_ROWS - 2, d), F32)], axis=0))

    return pl.pallas_call(
        body, name=name, grid=(m // TM,), in_specs=[row, vec, mod, row, row, row, vec, mod],
        out_specs=[row, mod, row, mod],
        out_shape=[jax.ShapeDtypeStruct((m, d), F32), jax.ShapeDtypeStruct(ms_mod.shape, F32),
                   jax.ShapeDtypeStruct((m, d), BF16), jax.ShapeDtypeStruct(ms_gate.shape, F32)],
        compiler_params=_params(("arbitrary",)),
    )(h, g_pre, ms_mod, du, dh_in, z, g_post, ms_gate)


def _gate_loss_bwd(h, z, g_post, modsel, i_g, target, nt, name):
    m, d = h.shape
    row, vec, mod = _row_specs(d, nt)
    ntl = nt - 1
    tgt = pl.BlockSpec((TM, d), lambda t: ((t // nt) * ntl + jnp.maximum(t % nt, 1) - 1, 0))
    acc = pl.BlockSpec((8, 128), lambda t: (0, 0))

    def body(h_ref, z_ref, gp_ref, ms_ref, t_ref, dh_ref, dz_ref, dvg_ref, ss_ref):
        t = pl.program_id(0)

        @pl.when(t == 0)
        def _():
            ss_ref[...] = jnp.zeros_like(ss_ref)

        latent = (t % nt != 0).astype(F32)
        rz, zh = _rms(z_ref[...])
        gp = gp_ref[...]
        gate = ms_ref[...][i_g:i_g + 1]
        e = h_ref[...] + gate * (zh * gp) - t_ref[...]
        ss_ref[...] += latent * jnp.sum(e * e)
        dh_ = e * (latent / d)
        dh_ref[...] = dh_
        dzh, d_gate, d_gp = _gate_rows(dh_, zh, gp, gate)
        dz_ref[...] = (rz * (dzh - zh * jnp.mean(dzh * zh, axis=1, keepdims=True))).astype(BF16)
        _acc_rows(t, nt, dvg_ref, jnp.concatenate([d_gate, d_gp, jnp.zeros((MOD_ROWS - 2, d), F32)], axis=0))

    return pl.pallas_call(
        body, name=name, grid=(m // TM,), in_specs=[row, row, vec, mod, tgt], out_specs=[row, row, mod, acc],
        out_shape=[jax.ShapeDtypeStruct((m, d), F32), jax.ShapeDtypeStruct((m, d), BF16),
                   jax.ShapeDtypeStruct(modsel.shape, F32), jax.ShapeDtypeStruct((8, 128), F32)],
        compiler_params=_params(("arbitrary",)),
    )(h, z, g_post, modsel, target)


QA, KA, VA, QB, KB, VB = 0, 512, 640, 768, 1280, 1408
PROJ_W = 1536
Q_SCALE = HEAD_DIM ** -0.5


def _swap16(x):
    lane = lax.broadcasted_iota(jnp.int32, x.shape, 1)
    n = x.shape[1]
    return jnp.where((lane % 32) < 16, pltpu.roll(x, n - 16, 1), pltpu.roll(x, 16, 1))


def _seg_mean(x, e):
    hi = x.astype(BF16)
    lo = (x - hi.astype(F32)).astype(BF16)
    return jnp.dot(hi, e, preferred_element_type=F32) + jnp.dot(lo, e, preferred_element_type=F32)


def _rope_tables(t_rows, c_rows):
    s = t_rows - c_rows
    row_ids = jnp.repeat(jnp.arange(s // GRID_W, dtype=jnp.int32), GRID_W).astype(F32)
    col_ids = jnp.tile(jnp.arange(GRID_W, dtype=jnp.int32), s // GRID_W).astype(F32)
    axis_dim = HEAD_DIM // 2
    inv = ROPE_THETA ** (-jnp.arange(0, axis_dim, 2, dtype=F32) / axis_dim)
    ang_r = row_ids[:, None] * inv[None, :]
    ang_c = col_ids[:, None] * inv[None, :]
    cos = jnp.concatenate([jnp.cos(ang_r), jnp.cos(ang_r), jnp.cos(ang_c), jnp.cos(ang_c)], axis=1)
    sin = jnp.concatenate([-jnp.sin(ang_r), jnp.sin(ang_r), -jnp.sin(ang_c), jnp.sin(ang_c)], axis=1)
    cos = jnp.concatenate([jnp.ones((c_rows, HEAD_DIM), F32), cos], axis=0)
    sin = jnp.concatenate([jnp.zeros((c_rows, HEAD_DIM), F32), sin], axis=0)
    return jnp.tile(cos, (1, 8)), jnp.tile(sin, (1, 8))


def _head_mean_matrix():
    i = np.arange(512)
    return jnp.asarray((i[:, None] // HEAD_DIM == i[None, :] // HEAD_DIM).astype(np.float32) / HEAD_DIM, dtype=BF16)


def _interleave_kv(k, v):
    return jnp.concatenate([k[:, :64], v[:, :64], k[:, 64:], v[:, 64:]], axis=1)


def _prep_fwd(proj, qn, kn, cos, sin, emat, nt, name):
    m = proj.shape[0]
    specs = [
        pl.BlockSpec((TM, PROJ_W), lambda t: (t, 0)),
        pl.BlockSpec((1, 512), lambda t: (0, 0)), pl.BlockSpec((1, 128), lambda t: (0, 0)),
        pl.BlockSpec((TM, 512), lambda t: (t % nt, 0)), pl.BlockSpec((TM, 512), lambda t: (t % nt, 0)),
        pl.BlockSpec((512, 512), lambda t: (0, 0)),
    ]

    def body(p_ref, qn_ref, kn_ref, cos_ref, sin_ref, e_ref, q_ref, kv_ref):
        cos_, sin_, e = cos_ref[...], sin_ref[...], e_ref[...]

        def rope(x, w):
            return x * cos_[:, :w] + _swap16(x) * sin_[:, :w]

        def norm(x, g, w):
            return x * lax.rsqrt(_seg_mean(x * x, e[:w, :w]) + EPS) * g

        qa = rope(norm(p_ref[:, QA:QA + 512], qn_ref[...], 512), 512)
        qb = rope(p_ref[:, QB:QB + 512], 512)
        q_ref[:, 0:512] = (qa * Q_SCALE).astype(BF16)
        q_ref[:, 512:1024] = (qb * Q_SCALE).astype(BF16)
        ka = rope(norm(p_ref[:, KA:KA + 128], kn_ref[...], 128), 128)
        kb = rope(p_ref[:, KB:KB + 128], 128)
        kv_ref[:, 0:256] = _interleave_kv(ka, p_ref[:, VA:VA + 128]).astype(BF16)
        kv_ref[:, 256:512] = _interleave_kv(kb, p_ref[:, VB:VB + 128]).astype(BF16)

    return pl.pallas_call(
        body, name=name, grid=(m // TM,), in_specs=specs,
        out_specs=[pl.BlockSpec((TM, 1024), lambda t: (t, 0)), pl.BlockSpec((TM, 512), lambda t: (t, 0))],
        out_shape=[jax.ShapeDtypeStruct((m, 1024), BF16), jax.ShapeDtypeStruct((m, 512), BF16)],
        compiler_params=_params(("parallel",)),
    )(proj, qn, kn, cos, sin, emat)


def _prep_bwd(proj, dq, dkv, qn, kn, cos, sin, emat, nt, name, comm=None):
    m = proj.shape[0]
    specs = [
        pl.BlockSpec((TM, PROJ_W), lambda t: (t, 0)),
        pl.BlockSpec((TM, 1024), lambda t: (t, 0)), pl.BlockSpec((TM, 512), lambda t: (t, 0)),
        pl.BlockSpec((1, 512), lambda t: (0, 0)), pl.BlockSpec((1, 128), lambda t: (0, 0)),
        pl.BlockSpec((TM, 512), lambda t: (t % nt, 0)), pl.BlockSpec((TM, 512), lambda t: (t % nt, 0)),
        pl.BlockSpec((512, 512), lambda t: (0, 0)),
    ]

    def body(p_ref, dq_ref, dkv_ref, qn_ref, kn_ref, cos_ref, sin_ref, e_ref, dp_ref, dqn_ref, dkn_ref):
        t = pl.program_id(0)
        cos_, sin_, e = cos_ref[...], sin_ref[...], e_ref[...]

        @pl.when(t == 0)
        def _():
            dqn_ref[...] = jnp.zeros_like(dqn_ref)
            dkn_ref[...] = jnp.zeros_like(dkn_ref)

        def unrope(dy, w):
            return dy * cos_[:, :w] + _swap16(dy * sin_[:, :w])

        def norm_bwd(x, g, dy, w):
            r = lax.rsqrt(_seg_mean(x * x, e[:w, :w]) + EPS)
            xh = x * r
            dxh = dy * g
            dx = r * (dxh - xh * _seg_mean(dxh * xh, e[:w, :w]))
            return dx, jnp.sum(dy * xh, axis=0, keepdims=True)

        dqa, dgq = norm_bwd(p_ref[:, QA:QA + 512], qn_ref[...], unrope(dq_ref[:, 0:512] * Q_SCALE, 512), 512)
        dp_ref[:, QA:QA + 512] = dqa.astype(BF16)
        dp_ref[:, QB:QB + 512] = unrope(dq_ref[:, 512:1024] * Q_SCALE, 512).astype(BF16)
        da = dkv_ref[:, 0:256]
        db = dkv_ref[:, 256:512]
        dka = jnp.concatenate([da[:, 0:64], da[:, 128:192]], axis=1)
        dva = jnp.concatenate([da[:, 64:128], da[:, 192:256]], axis=1)
        dkb = jnp.concatenate([db[:, 0:64], db[:, 128:192]], axis=1)
        dvb = jnp.concatenate([db[:, 64:128], db[:, 192:256]], axis=1)
        dka, dgk = norm_bwd(p_ref[:, KA:KA + 128], kn_ref[...], unrope(dka, 128), 128)
        dp_ref[:, KA:KA + 128] = dka.astype(BF16)
        dp_ref[:, VA:VA + 128] = dva.astype(BF16)
        dp_ref[:, KB:KB + 128] = unrope(dkb, 128).astype(BF16)
        dp_ref[:, VB:VB + 128] = dvb.astype(BF16)
        dqn_ref[0:1, :] += dgq
        dkn_ref[0:1, :] += dgk

    outs, couts = _call(
        body, name=name, grid=(m // TM,), in_specs=specs,
        out_specs=[pl.BlockSpec((TM, PROJ_W), lambda t: (t, 0)), pl.BlockSpec((8, 512), lambda t: (0, 0)),
                   pl.BlockSpec((8, 128), lambda t: (0, 0))],
        out_shape=[jax.ShapeDtypeStruct((m, PROJ_W), BF16), jax.ShapeDtypeStruct((8, 512), F32),
                   jax.ShapeDtypeStruct((8, 128), F32)],
        args=[proj, dq, dkv, qn, kn, cos, sin, emat], sem=("arbitrary",), comm=comm)
    return outs if comm is None else (outs, couts)


def _attn_case(t, hg, kv_ref, c_rows, t_rows, fn, keys_first=False):
    wl = TM + 2 * WINDOW
    kvd = lambda a, n: kv_ref[pl.ds(a, n), :]
    dense = hg < 2
    ctx = t == 0

    @pl.when(jnp.logical_and(dense, ctx))
    def _():
        kv = kvd(0, c_rows)
        fn(kv[:, :64], kv[:, 64:], None, False, [(0, c_rows)])

    @pl.when(jnp.logical_and(dense, jnp.logical_not(ctx)))
    def _():
        kv = kvd(0, t_rows)
        fn(kv[:, :64], kv[:, 64:], None, False, [(0, t_rows)])

    @pl.when(jnp.logical_and(jnp.logical_not(dense), ctx))
    def _():
        kv = kvd(0, c_rows)
        fn(kv[:, :64], kv[:, 64:], None, True, [(0, c_rows)])

    @pl.when(jnp.logical_and(jnp.logical_not(dense), jnp.logical_not(ctx)))
    def _():
        start = pl.multiple_of(jnp.minimum(c_rows + (t - 1) * TM - WINDOW, t_rows - wl), 128)
        kv = jnp.concatenate([kvd(0, c_rows), kvd(start, wl)], axis=0)
        shape = (c_rows + wl, TM) if keys_first else (TM, c_rows + wl)
        q_i = lax.broadcasted_iota(jnp.int32, shape, 1 if keys_first else 0)
        k_i = lax.broadcasted_iota(jnp.int32, shape, 0 if keys_first else 1)
        qpos = (t - 1) * TM + q_i
        kpos = start - 2 * c_rows + k_i
        mask = jnp.logical_or(k_i < c_rows, jnp.logical_and(jnp.abs(kpos - qpos) <= WINDOW, kpos >= 0))
        fn(kv[:, :64], kv[:, 64:], mask, True, [(0, c_rows), (start, wl)])


def _head_columns(cols):
    lane = lax.broadcasted_iota(jnp.int32, (TM, 128), 1)
    out = jnp.zeros((TM, 128), F32)
    for g, col in enumerate(cols):
        out = jnp.where(lane == g, col, out)
    return out


def _attn_specs(t_rows):
    nt = t_rows // TM
    q_spec = pl.BlockSpec((None, TM, Q_WIDTH), lambda b, hg, t, s: (b, t, hg))
    kv_spec = pl.BlockSpec((None, t_rows, 128), lambda b, hg, t, s: (b, 0, hg))
    lse_spec = pl.BlockSpec((None, TM, 128), lambda b, hg, t, s: (hg, b * nt + t, 0))
    return q_spec, kv_spec, lse_spec


def _attn_fwd(q_all, kv_all, sink8, c_rows, name, comm=None):
    bl, t_rows, _ = q_all.shape
    q_spec, kv_spec, lse_spec = _attn_specs(t_rows)

    def body(sink_ref, q_ref, kv_ref, o_ref, ot_ref, lse_ref):
        hg, t = pl.program_id(1), pl.program_id(2)

        def fn(k, v, mask, use_sink, spans):
            outs, lses = [], []
            v_one = jnp.concatenate([v, jnp.ones(v.shape, BF16)], axis=1)
            def scores(g):
                s = lax.dot_general(q_ref[:, g * 64:(g + 1) * 64], k, (((1,), (1,)), ((), ())),
                                    preferred_element_type=F32)
                return s if mask is None else jnp.where(mask, s, NEG_BIG)

            s_next = scores(0)
            for g in range(GROUP):
                s = s_next
                if g + 1 < GROUP:
                    s_next = scores(g + 1)
                mx = jnp.max(s, axis=1, keepdims=True)
                if use_sink:
                    sink = sink_ref[jnp.maximum(hg - 2, 0) * GROUP + g]
                    mx = jnp.maximum(mx, sink)
                pv = jnp.dot(jnp.exp(s - mx).astype(BF16), v_one, preferred_element_type=F32)
                l = pv[:, 64:65]
                if use_sink:
                    l = l + jnp.exp(sink - mx)
                outs.append(pv[:, :64] * (1.0 / l))
                lses.append(mx + jnp.log(l))
            o = jnp.concatenate(outs, axis=1).astype(BF16)
            o_ref[...] = o
            ot_ref[...] = o.T
            lse_ref[...] = _head_columns(lses)

        _attn_case(t, hg, kv_ref, c_rows, t_rows, fn)

    nt = t_rows // TM
    ot_spec = pl.BlockSpec((Q_WIDTH, TM), lambda b, hg, t, s: (hg, b * nt + t))
    outs, couts = _call(
        body, name=name, grid=(bl, N_HG, nt), in_specs=[q_spec, kv_spec], out_specs=[q_spec, ot_spec, lse_spec],
        out_shape=[jax.ShapeDtypeStruct(q_all.shape, BF16), jax.ShapeDtypeStruct((N_HG * Q_WIDTH, bl * t_rows), BF16),
                   jax.ShapeDtypeStruct((N_HG, bl * t_rows, 128), F32)],
        args=[sink8, q_all, kv_all], prefetch=1, sem=("parallel", "parallel", "arbitrary"), comm=comm)
    return outs if comm is None else (outs, couts)


def _attn_bwd(q_all, kv_all, do, o, lse, sink8, c_rows, name, comm=None):
    bl, t_rows, _ = q_all.shape
    q_spec, kv_spec, lse_spec = _attn_specs(t_rows)
    ds_spec = pl.BlockSpec((None, None, 8, 128), lambda b, hg, t, s: (b, hg, 0, 0))

    def body(sink_ref, q_ref, kv_ref, do_ref, o_ref, lse_ref, dq_ref, dkv_ref, dsk_ref):
        hg, t = pl.program_id(1), pl.program_id(2)

        @pl.when(t == 0)
        def _():
            dkv_ref[...] = jnp.zeros_like(dkv_ref)
            dsk_ref[...] = jnp.zeros_like(dsk_ref)

        lse_rows = lse_ref[...].T
        dd_cols = [jnp.sum(do_ref[:, g * 64:(g + 1) * 64].astype(F32) * o_ref[:, g * 64:(g + 1) * 64].astype(F32),
                           axis=1, keepdims=True) for g in range(GROUP)]
        dd_rows = _head_columns(dd_cols).T

        def fn(k, v, mask, use_sink, spans):
            k_t = k.T
            dq_t, dsinks = [], []
            dk = jnp.zeros(k.shape, F32)
            dv = jnp.zeros(v.shape, F32)
            def products(g):
                s = lax.dot_general(k, q_ref[:, g * 64:(g + 1) * 64], (((1,), (1,)), ((), ())),
                                    preferred_element_type=F32)
                dp = lax.dot_general(v, do_ref[:, g * 64:(g + 1) * 64], (((1,), (1,)), ((), ())),
                                     preferred_element_type=F32)
                return (s if mask is None else jnp.where(mask, s, NEG_BIG)), dp

            nxt = products(0)
            for g in range(GROUP):
                q = q_ref[:, g * 64:(g + 1) * 64]
                do_g = do_ref[:, g * 64:(g + 1) * 64]
                lse_g, dd_g = lse_rows[g:g + 1, :], dd_rows[g:g + 1, :]
                s, dp = nxt
                if g + 1 < GROUP:
                    nxt = products(g + 1)
                pb = jnp.exp(s - lse_g).astype(BF16)
                ds = (pb.astype(F32) * (dp - dd_g)).astype(BF16)
                dk = dk + jnp.dot(ds, q, preferred_element_type=F32)
                dv = dv + jnp.dot(pb, do_g, preferred_element_type=F32)
                dq_t.append(jnp.dot(k_t, ds, preferred_element_type=F32))
                if use_sink:
                    p_sink = jnp.exp(sink_ref[jnp.maximum(hg - 2, 0) * GROUP + g] - lse_g)
                    dsinks.append(jnp.broadcast_to(-jnp.sum(p_sink * dd_g, axis=1, keepdims=True), (1, 128)))
            dq_ref[...] = jnp.concatenate(dq_t, axis=0).T
            dkv = jnp.concatenate([dk, dv], axis=1)
            off = 0
            for start, size in spans:
                dkv_ref[pl.ds(start, size), :] += dkv[off:off + size]
                off += size
            if use_sink:
                dsk_ref[0:GROUP, :] += jnp.concatenate(dsinks, axis=0)

        _attn_case(t, hg, kv_ref, c_rows, t_rows, fn, keys_first=True)

    outs, couts = _call(
        body, name=name, grid=(bl, N_HG, t_rows // TM), in_specs=[q_spec, kv_spec, q_spec, q_spec, lse_spec],
        out_specs=[q_spec, kv_spec, ds_spec],
        out_shape=[jax.ShapeDtypeStruct(q_all.shape, F32), jax.ShapeDtypeStruct(kv_all.shape, F32),
                   jax.ShapeDtypeStruct((bl, N_HG, 8, 128), F32)],
        args=[sink8, q_all, kv_all, do, o, lse], prefetch=1, sem=("parallel", "parallel", "arbitrary"), comm=comm)
    return outs if comm is None else (outs, couts)


def _silu(x):
    return x * jax.nn.sigmoid(x)


def _ada_fwd(c_rows, w_ada, b_cols, name, comm=None):
    nl, d, w = w_ada.shape
    r = c_rows.shape[0]

    def body(c_ref, w_ref, b_ref, o_ref):
        s = _silu(c_ref[...]).astype(BF16)
        o_ref[...] = jnp.dot(s, w_ref[...].astype(BF16), preferred_element_type=F32) + b_ref[...]

    outs, couts = _call(
        body, name=name, grid=(nl,),
        in_specs=[pl.BlockSpec((r, d), lambda l: (0, 0)), pl.BlockSpec((None, d, w), lambda l: (l, 0, 0)),
                  pl.BlockSpec((None, 1, w), lambda l: (l, 0, 0))],
        out_specs=pl.BlockSpec((None, r, w), lambda l: (l, 0, 0)),
        out_shape=jax.ShapeDtypeStruct((nl, r, w), F32), args=[c_rows, w_ada, b_cols], sem=("parallel",), comm=comm)
    return outs if comm is None else (outs, couts)


def _ada_bwd(c_rows, c_ctx, dmod, w_ada, name):
    nl, d, w = w_ada.shape
    r = c_rows.shape[0]

    def body(c_ref, cc_ref, g_ref, w_ref, dw_ref, dc_ref):
        l = pl.program_id(0)
        s = _silu(c_ref[...]).astype(BF16)
        gm = g_ref[...].astype(BF16)
        dw_ref[...] = lax.dot_general(s, gm, (((0,), (0,)), ((), ())), preferred_element_type=F32)
        ds = lax.dot_general(gm, w_ref[...].astype(BF16), (((1,), (1,)), ((), ())), preferred_element_type=F32)
        rows = lax.broadcasted_iota(jnp.int32, ds.shape, 0)
        dsc = jnp.sum(jnp.where(rows % ADA_ROWS == 2, ds, 0.0), axis=0, keepdims=True)
        x = cc_ref[...]
        sg = jax.nn.sigmoid(x)
        dcc = dsc * (sg * (1.0 + x * (1.0 - sg)))
        out = jnp.concatenate([dcc, jnp.zeros((7, d), F32)], axis=0)

        @pl.when(l == 0)
        def _():
            dc_ref[...] = out

        @pl.when(l != 0)
        def _():
            dc_ref[...] += out

    return pl.pallas_call(
        body, name=name, grid=(nl,),
        in_specs=[pl.BlockSpec((r, d), lambda l: (0, 0)), pl.BlockSpec((1, d), lambda l: (0, 0)),
                  pl.BlockSpec((None, r, w), lambda l: (l, 0, 0)), pl.BlockSpec((None, d, w), lambda l: (l, 0, 0))],
        out_specs=[pl.BlockSpec((None, d, w), lambda l: (l, 0, 0)), pl.BlockSpec((8, d), lambda l: (0, 0))],
        out_shape=[jax.ShapeDtypeStruct((nl, d, w), F32), jax.ShapeDtypeStruct((8, d), F32)],
        compiler_params=_params(("arbitrary",)),
    )(c_rows, c_ctx, dmod, w_ada)


def _adam_math(w, g, m, v):
    m = ADAM_B1 * m + (1.0 - ADAM_B1) * g
    v = ADAM_B2 * v + (1.0 - ADAM_B2) * (g * g)
    m_hat = m / (1.0 - ADAM_B1 ** ADAM_STEP)
    v_hat = v / (1.0 - ADAM_B2 ** ADAM_STEP)
    delta = -ADAM_LR * (m_hat / (jnp.sqrt(v_hat) + ADAM_EPS) + ADAM_WD * w)
    return delta, m, v


def _adamw(w, m, v, g_own, g_recv, name, rows=256):
    nl, r, c = w.shape
    tr = min(rows, r)
    spec = pl.BlockSpec((None, tr, c), lambda l, i: (l, i, 0))
    per_layer = isinstance(g_own, (list, tuple))
    own = list(g_own) if per_layer else [g_own]
    recv = [] if g_recv is None else list(g_recv)
    in_specs = [spec] * 3 + [pl.BlockSpec((tr, c), lambda l, i: (i, 0)) if per_layer else spec] * len(own)
    in_specs += [pl.BlockSpec((N_DEV, tr, c), lambda l, i: (0, i, 0))] * len(recv)

    def body(*refs):
        w_ref, m_ref, v_ref = refs[:3]
        own_refs, recv_refs = refs[3:3 + len(own)], refs[3 + len(own):3 + len(own) + len(recv)]
        go_ref, d_ref, mo_ref, vo_ref = refs[-4:]

        def update(li):
            g = own_refs[li][...].astype(F32)
            if recv:
                for k in range(1, N_DEV):
                    g = g + recv_refs[li][k].astype(F32)
            delta, m_, v_ = _adam_math(w_ref[...], g, m_ref[...], v_ref[...])
            go_ref[...] = g
            d_ref[...] = delta
            mo_ref[...] = m_
            vo_ref[...] = v_

        if per_layer:
            for li in range(nl):
                pl.when(pl.program_id(0) == li)(functools.partial(update, li))
        else:
            update(0)

    return pl.pallas_call(
        body, name=name, grid=(nl, r // tr), in_specs=in_specs, out_specs=[spec] * 4,
        out_shape=[jax.ShapeDtypeStruct(w.shape, F32)] * 4, compiler_params=_params(("parallel", "parallel")),
    )(w, m, v, *own, *recv)


def _small_adamw(w, m, v, g_all, name):
    def body(w_ref, m_ref, v_ref, g_ref, go_ref, d_ref, mo_ref, vo_ref):
        g = g_ref[0]
        for k in range(1, N_DEV):
            g = g + g_ref[k]
        delta, m_, v_ = _adam_math(w_ref[...], g, m_ref[...], v_ref[...])
        go_ref[...] = g
        d_ref[...] = delta
        mo_ref[...] = m_
        vo_ref[...] = v_

    return pl.pallas_call(
        body, name=name, out_shape=[jax.ShapeDtypeStruct(w.shape, F32)] * 4, compiler_params=_params(),
    )(w, m, v, g_all)


SMALL = ("c_ctx", "b_ada", "g_pre_mix", "g_post_mix", "g_pre_mlp", "g_post_mlp", "q_norm", "k_norm", "sink")


def _pack_small(parts):
    flat = jnp.concatenate([parts[n].reshape(-1) for n in SMALL])
    rows = -(-flat.shape[0] // 1024) * 8
    return jnp.pad(flat, (0, rows * 128 - flat.shape[0])).reshape(rows, 128)


def _unpack_small(packed, like):
    flat = packed.reshape(-1)
    out, off = {}, 0
    for n in SMALL:
        size = int(np.prod(like[n].shape))
        out[n] = flat[off:off + size].reshape(like[n].shape)
        off += size
    return out


def kernel(x, c, ctx, c_ctx, w_ada, b_ada, g_pre_mix, g_post_mix, g_pre_mlp, g_post_mlp, w_in, q_norm, k_norm, sink, w_out, w_up, w_down, loss_target, m_c_ctx, m_w_ada, m_b_ada, m_g_pre_mix, m_g_post_mix, m_g_pre_mlp, m_g_post_mlp, m_w_in, m_q_norm, m_k_norm, m_sink, m_w_out, m_w_up, m_w_down, v_c_ctx, v_w_ada, v_b_ada, v_g_pre_mix, v_g_post_mix, v_g_pre_mlp, v_g_post_mlp, v_w_in, v_q_norm, v_k_norm, v_sink, v_w_out, v_w_up, v_w_down):
    bl, s_rows, d = x.shape
    c_rows = ctx.shape[1]
    assert c_rows == TM and s_rows % TM == 0 and bl == 2
    t_rows = c_rows + s_rows
    nt = t_rows // TM
    m_rows = bl * t_rows
    nl = w_in.shape[0]
    ada_w = w_ada.shape[2]
    d_ff = w_up.shape[2] * N_DEV
    me = _my_index()

    shard = lambda w_, l: w_[l].astype(BF16)
    c_pad = jnp.concatenate([c, c_ctx[None, :], jnp.zeros((ADA_ROWS - bl - 1, d), F32)], axis=0)
    gathered = {0: {}}
    h, (c_all, gathered[0]["w_in"]) = _token_stream(
        ctx, x, nt, "token_stream", comm=_Comm([(c_pad, GATHER), (shard(w_in, 0), GATHER)]))
    c_all = c_all.reshape(N_DEV * ADA_ROWS, d)

    def layer_weights(l):
        g_ = gathered[l]
        w_out_f = g_["w_out"].reshape(-1, d)
        w_down_f = g_["w_down"].reshape(d_ff, d)
        return dict(
            w_in_t=g_["w_in"].transpose(0, 2, 1).reshape(PROJ_W, d), w_out_f=w_out_f, w_out_t=w_out_f.T,
            w_up_s=g_["w_up"], w_up_t=g_["w_up"].transpose(0, 2, 1).reshape(d_ff, d),
            w_down_f=w_down_f, w_down_t=w_down_f.T)

    big = dict(tm=2304, tn=512)
    deep = dict(tm=1536, tn=256, tk=d_ff)
    wide = dict(tm=1024, tn=512, tk=m_rows)

    b_cols = lax.dynamic_slice(b_ada, (0, me * ada_w), (nl, ada_w))[:, None, :]
    mod_cols = _ada_fwd(c_all, w_ada, b_cols, "ada_fwd")
    mod_slots = mod_cols.reshape(nl, N_DEV, ADA_ROWS, ada_w).transpose(1, 0, 2, 3)
    mod_g, = _comm_only(_Comm([(mod_slots, TO_OWNER)]), "exchange_mod")
    mine = mod_g.transpose(1, 2, 0, 3).reshape(nl, ADA_ROWS, N_MOD, d)
    pad = jnp.zeros((bl, 2, MOD_ROWS - N_MOD, d), F32)
    modsel = [jnp.concatenate([jnp.stack([jnp.broadcast_to(mine[l, bl], (bl, N_MOD, d)), mine[l, :bl]], axis=1), pad],
                              axis=2) for l in range(nl)]

    cos, sin = _rope_tables(t_rows, c_rows)
    emat = _head_mean_matrix()
    row = lambda a: a[None, :]
    qn = [jnp.tile(q_norm[l], 8)[None, :] for l in range(nl)]
    kn = [jnp.tile(k_norm[l], 2)[None, :] for l in range(nl)]

    target = loss_target.reshape(bl * s_rows, d)
    saved = []
    weights_of = {}
    (u, u_t), (gathered[0]["w_out"],) = _norm_mod_fwd(
        h, row(g_pre_mix[0]), modsel[0], 0, 1, nt, "mix_mod_fwd0", comm=_Comm([(shard(w_out, 0), GATHER)]))
    for l in range(nl):
        w_in_f = gathered[l]["w_in"].transpose(1, 0, 2).reshape(d, PROJ_W)
        proj = _mm(u, w_in_f, name=f"mm_in{l}", tk=d, **big)
        q_all, kv_all = _prep_fwd(proj, qn[l], kn[l], cos, sin, emat, nt, f"prep_fwd{l}")
        (o, o_t, lse), (w_up_g, w_down_g) = _attn_fwd(
            q_all.reshape(bl, t_rows, 1024), kv_all.reshape(bl, t_rows, 512), sink[l], c_rows, f"attn_fwd{l}",
            comm=_Comm([(shard(w_up, l), GATHER), (shard(w_down, l), GATHER)]))
        o = o.reshape(m_rows, 1024)
        gathered[l].update(w_up=w_up_g, w_down=w_down_g)
        wl = weights_of[l] = layer_weights(l)
        mix = _mm(o, wl["w_out_f"], name=f"mm_out{l}", tk=1024, **big)
        h_mid, v_in, v_t = _gate_mod_fwd(h, mix, row(g_post_mix[l]), modsel[l], 2, row(g_pre_mlp[l]), modsel[l], 3, 4,
                                         nt, f"mix_gate_mlp_mod_fwd{l}")
        more = l + 1 < nl
        res_up = _mm(v_in, wl["w_up_s"], name=f"mm_up{l}", b_mode="nn_slots", epilogue="relu2", tk=d,
                     comm=_Comm([(shard(w_in, l + 1), GATHER)]) if more else None, **big)
        (r_act, r_t), nxt_in = res_up if more else (res_up, None)
        res_down = _mm(r_act, wl["w_down_f"], name=f"mm_down{l}",
                       comm=_Comm([(shard(w_out, l + 1), GATHER)]) if more else None, **deep)
        y, nxt_out = res_down if more else (res_down, None)
        saved.append((h, u_t, proj, q_all, kv_all, o, o_t, lse, mix, h_mid, v_t, r_act, r_t, y))
        if more:
            gathered[l + 1] = dict(w_in=nxt_in[0], w_out=nxt_out[0])
            h, u, u_t = _gate_mod_fwd(h_mid, y, row(g_post_mlp[l]), modsel[l], 5, row(g_pre_mix[l + 1]), modsel[l + 1],
                                      0, 1, nt, f"mlp_gate_mix_mod_fwd{l}")

    dh, dy, dvec_g2, ss = _gate_loss_bwd(h_mid, y, row(g_post_mlp[nl - 1]), modsel[nl - 1], 5, target, nt, "gate_loss_bwd")
    loss = lax.psum(0.5 * ss[0, 0] / d, MESH_AXES)
    small_g = {n: [None] * nl for n in SMALL if n not in ("c_ctx", "b_ada")}
    dvecs = {l: {} for l in range(nl)}
    dvecs[nl - 1]["g2"] = dvec_g2
    slots = {n: [None] * nl for n in ("w_in", "w_out", "w_up", "w_down")}
    recvd = {n: [None] * nl for n in slots}
    send = lambda n, l_: (slots[n][l_], TO_OWNER_XOR)
    for l in reversed(range(nl)):
        h_in, u_t, proj, q_all, kv_all, o, o_t, lse, mix, h_mid, v_t, r_act, r_t, y = saved[l]
        wl = weights_of[l]
        later = l + 1 < nl
        res = _mm(dy, wl["w_down_t"], name=f"mm_da{l}", epilogue="relu2_bwd", extra=r_act, out_dtype=BF16, tk=d,
                  comm=_Comm([send("w_out", l + 1)]) if later else None, **big)
        da = res[0] if later else res
        if later:
            recvd["w_out"][l + 1] = res[1][0]
        res = _mm(r_t, dy, name=f"mm_dw_down{l}", out_dtype=BF16,
                  comm=_Comm([send("w_in", l + 1)]) if later else None, **wide)
        dw_down = res[0] if later else res
        if later:
            recvd["w_in"][l + 1] = res[1][0]
        dw_up = _mm(v_t, da, name=f"mm_dw_up{l}", out_mode="slots", out_dtype=BF16, **wide)
        slots["w_down"][l] = dw_down.reshape(N_DEV, -1, d)
        slots["w_up"][l] = dw_up
        dv = _mm(da, wl["w_up_t"], name=f"mm_dv{l}", **deep)
        dh, dvecs[l]["m2"], dmix, dvecs[l]["g1"] = _mod_gate_bwd(
            h_mid, row(g_pre_mlp[l]), modsel[l], 3, 4, dv, dh, mix, row(g_post_mix[l]), modsel[l], 2, nt,
            f"mlp_mod_mix_gate_bwd{l}")
        do = _mm(dmix, wl["w_out_t"], name=f"mm_do{l}", out_dtype=BF16, tk=d, **big)
        dw_out = _mm(o_t, dmix, name=f"mm_dw_out{l}", out_dtype=BF16, **wide)
        slots["w_out"][l] = dw_out.reshape(N_DEV, -1, d)
        (dq, dkv, dsk), (recvd["w_down"][l], recvd["w_up"][l]) = _attn_bwd(
            q_all.reshape(bl, t_rows, 1024), kv_all.reshape(bl, t_rows, 512), do.reshape(bl, t_rows, 1024),
            o.reshape(bl, t_rows, 1024), lse, sink[l], c_rows, f"attn_bwd{l}",
            comm=_Comm([send("w_down", l), send("w_up", l)]))
        last = l == 0
        res = _prep_bwd(proj, dq.reshape(m_rows, 1024), dkv.reshape(m_rows, 512), qn[l], kn[l], cos, sin, emat, nt,
                        f"prep_bwd{l}", comm=_Comm([send("w_out", l)]) if last else None)
        dproj, dqn, dkn = res[0] if last else res
        if last:
            recvd["w_out"][l] = res[1][0]
        dw_in = _mm(u_t, dproj, name=f"mm_dw_in{l}", out_dtype=BF16, **wide)
        slots["w_in"][l] = dw_in.reshape(d, N_DEV, PROJ_W // N_DEV).transpose(1, 0, 2)
        du = _mm(dproj, wl["w_in_t"], name=f"mm_du{l}", tk=PROJ_W, **big)
        if last:
            (dh, dvecs[l]["m1"]), (recvd["w_in"][l],) = _norm_mod_bwd(
                h_in, row(g_pre_mix[l]), modsel[l], du, dh, 0, 1, nt, f"mix_mod_bwd{l}", latent_only=True,
                comm=_Comm([send("w_in", l)]))
        else:
            dh, dvecs[l]["m1"], dy, dvecs[l - 1]["g2"] = _mod_gate_bwd(
                h_in, row(g_pre_mix[l]), modsel[l], 0, 1, du, dh, saved[l - 1][-1], row(g_post_mlp[l - 1]),
                modsel[l - 1], 5, nt, f"mix_mod_mlp_gate_bwd{l}")
        small_g["q_norm"][l] = jnp.sum(dqn[0].reshape(8, HEAD_DIM), axis=0)
        small_g["k_norm"][l] = jnp.sum(dkn[0].reshape(2, HEAD_DIM), axis=0)
        small_g["sink"][l] = jnp.sum(dsk[:, 2:, :GROUP, 0], axis=0).reshape(-1)

    dmod_rows = []
    for l in range(nl):
        m1, g1, m2, g2 = (dvecs[l][n] for n in ("m1", "g1", "m2", "g2"))
        small_g["g_pre_mix"][l] = jnp.sum(m1[:, :, 2], axis=(0, 1))
        small_g["g_post_mix"][l] = jnp.sum(g1[:, :, 1], axis=(0, 1))
        small_g["g_pre_mlp"][l] = jnp.sum(m2[:, :, 2], axis=(0, 1))
        small_g["g_post_mlp"][l] = jnp.sum(g2[:, :, 1], axis=(0, 1))
        dms = jnp.stack([m1[:, :, 0], m1[:, :, 1], g1[:, :, 0], m2[:, :, 0], m2[:, :, 1], g2[:, :, 0]], axis=2)
        rows = jnp.concatenate([dms[:, 1], jnp.sum(dms[:, 0], axis=0)[None]], axis=0)
        dmod_rows.append(jnp.pad(rows.reshape(bl + 1, N_MOD * d), ((0, ADA_ROWS - bl - 1), (0, 0))))
    grad_x = dh.reshape(bl, s_rows, d)

    dmod_slots = jnp.stack(dmod_rows).reshape(nl, ADA_ROWS, N_DEV, ada_w).transpose(2, 0, 1, 3)
    dmod_g, = _comm_only(_Comm([(dmod_slots, TO_OWNER)]), "exchange_dmod")
    dmod_mine = dmod_g.transpose(1, 0, 2, 3).reshape(nl, N_DEV * ADA_ROWS, ada_w)
    dw_ada, dcc = _ada_bwd(c_all, c_ctx[None, :], dmod_mine, w_ada, "ada_bwd")

    parts = {n: jnp.stack(small_g[n]) for n in small_g}
    parts["c_ctx"] = dcc[0]
    parts["b_ada"] = jnp.stack([jnp.sum(r_[: bl + 1], axis=0) for r_ in dmod_rows])
    weights = dict(c_ctx=c_ctx, b_ada=b_ada, g_pre_mix=g_pre_mix, g_post_mix=g_post_mix, g_pre_mlp=g_pre_mlp,
                   g_post_mlp=g_post_mlp, q_norm=q_norm, k_norm=k_norm, sink=sink)
    moms = dict(c_ctx=m_c_ctx, b_ada=m_b_ada, g_pre_mix=m_g_pre_mix, g_post_mix=m_g_post_mix, g_pre_mlp=m_g_pre_mlp,
                g_post_mlp=m_g_post_mlp, q_norm=m_q_norm, k_norm=m_k_norm, sink=m_sink)
    vels = dict(c_ctx=v_c_ctx, b_ada=v_b_ada, g_pre_mix=v_g_pre_mix, g_post_mix=v_g_post_mix, g_pre_mlp=v_g_pre_mlp,
                g_post_mlp=v_g_post_mlp, q_norm=v_q_norm, k_norm=v_k_norm, sink=v_sink)
    small_all, = _comm_only(_Comm([(_pack_small(parts), GATHER)]), "gather_small")
    s_out = _small_adamw(_pack_small(weights), _pack_small(moms), _pack_small(vels), small_all, "adamw_small")
    s_g, s_d, s_m, s_v = [_unpack_small(a, weights) for a in s_out]

    res = {}
    for n, w_, m_, v_ in (("w_in", w_in, m_w_in, v_w_in), ("w_out", w_out, m_w_out, v_w_out),
                          ("w_up", w_up, m_w_up, v_w_up), ("w_down", w_down, m_w_down, v_w_down)):
        own = [lax.dynamic_index_in_dim(slots[n][l], me, axis=0, keepdims=False) for l in range(nl)]
        res[n] = _adamw(w_, m_, v_, own, recvd[n], f"adamw_{n}")
    res["w_ada"] = _adamw(w_ada, m_w_ada, v_w_ada, dw_ada, None, "adamw_w_ada")

    order = ("c_ctx", "w_ada", "b_ada", "g_pre_mix", "g_post_mix", "g_pre_mlp", "g_post_mlp", "w_in", "q_norm",
             "k_norm", "sink", "w_out", "w_up", "w_down")
    outs = [loss, grad_x]
    for i, small in enumerate((s_g, s_d, s_m, s_v)):
        outs += [small[n] if n in small else res[n][i] for n in order]
    return tuple(outs)
```

```python
import functools

import jax
import jax.numpy as jnp
import numpy as np
from jax import lax
from jax.experimental import pallas as pl
from jax.experimental.pallas import tpu as pltpu

F32 = jnp.float32
BF16 = jnp.bfloat16

HEAD_DIM = 64
GROUP = 4
N_HG = 4
Q_WIDTH = GROUP * HEAD_DIM
WINDOW = 128
GRID_W = 64
ROPE_THETA = 10000.0
EPS = 1e-6
NEG_BIG = -1e30
N_MOD = 6
MOD_ROWS = 8
TM = 256
N_DEV = 8
ADA_ROWS = 8
VMEM_LIMIT = 56 * 1024 * 1024

ADAM_LR = 0.001
ADAM_B1 = 0.9
ADAM_B2 = 0.999
ADAM_EPS = 1e-08
ADAM_WD = 0.01
ADAM_STEP = 10


def _params(sem=None):
    kw = dict(vmem_limit_bytes=VMEM_LIMIT)
    if sem is not None:
        kw["dimension_semantics"] = sem
    return pltpu.CompilerParams(**kw)


def _my_index():
    return 4 * lax.axis_index("x") + 2 * lax.axis_index("y") + lax.axis_index("c")


def _peer(k):
    x, y, c = lax.axis_index("x"), lax.axis_index("y"), lax.axis_index("c")
    kx, ky, kc = (k >> 2) & 1, (k >> 1) & 1, k & 1
    px = (1 - x) if kx else x
    py = (1 - y) if ky else y
    pc = (1 - c) if kc else c
    return (px, py, pc), 4 * px + 2 * py + pc


GATHER, TO_OWNER, TO_OWNER_XOR = "gather", "to_owner", "to_owner_xor"


class _Comm:
    def __init__(self, items):
        self.items = list(items)
        self.arrays = [a for a, _ in self.items]

    def out_shapes(self):
        return [jax.ShapeDtypeStruct(((N_DEV,) + a.shape) if kind == GATHER else a.shape, a.dtype)
                for a, kind in self.items]

    def sem_shapes(self):
        n = len(self.items) * N_DEV
        return [pltpu.SemaphoreType.DMA((n,)), pltpu.SemaphoreType.DMA((n,))]

    def _copies(self, in_refs, out_refs, send_sems, recv_sems):
        me = _my_index()
        local, remote = [], []
        for i, ((_, kind), x_ref, o_ref) in enumerate(zip(self.items, in_refs, out_refs)):
            base = i * N_DEV
            own_src = x_ref if kind == GATHER else x_ref.at[me]
            own_dst = o_ref.at[0] if kind == TO_OWNER_XOR else o_ref.at[me]
            local.append(pltpu.make_async_copy(own_src, own_dst, send_sems.at[base]))
            for k in range(1, N_DEV):
                peer, pidx = _peer(k)
                remote.append(pltpu.make_async_remote_copy(
                    src_ref=x_ref if kind == GATHER else x_ref.at[pidx],
                    dst_ref=o_ref.at[k] if kind == TO_OWNER_XOR else o_ref.at[me],
                    send_sem=send_sems.at[base + k], recv_sem=recv_sems.at[base + k],
                    device_id=peer, device_id_type=pl.DeviceIdType.MESH))
        return local, remote

    def start(self, in_refs, out_refs, send_sems, recv_sems):
        local, remote = self._copies(in_refs, out_refs, send_sems, recv_sems)
        for cp in local + remote:
            cp.start()

    def wait(self, in_refs, out_refs, send_sems, recv_sems):
        local, remote = self._copies(in_refs, out_refs, send_sems, recv_sems)
        for cp in remote:
            cp.wait_recv()
        for cp in remote:
            cp.wait_send()
        for cp in local:
            cp.wait()


def _call(body, *, name, grid, in_specs, out_specs, out_shape, args, scratch_shapes=(), prefetch=0, sem=None,
          comm=None):
    single = not isinstance(out_shape, (list, tuple))
    out_shape = [out_shape] if single else list(out_shape)
    out_specs = [out_specs] if single else list(out_specs)
    in_specs, scratch_shapes, args = list(in_specs), list(scratch_shapes), list(args)
    n_in, n_out = len(in_specs), len(out_shape)
    if comm is not None:
        nc = len(comm.arrays)
        hbm = pl.BlockSpec(memory_space=pl.ANY)
        inner = body

        def body(*refs):
            pre, r = refs[:prefetch], refs[prefetch:]
            ins, cin = r[:n_in], r[n_in:n_in + nc]
            outs, cout = r[n_in + nc:n_in + nc + n_out], r[n_in + nc + n_out:n_in + 2 * nc + n_out]
            scr, sems = r[n_in + 2 * nc + n_out:len(r) - 2], r[len(r) - 2:]
            ids = [pl.program_id(i) for i in range(len(grid))]

            def when(flags, fn):
                if flags:
                    pl.when(functools.reduce(jnp.logical_and, flags))(fn)
                else:
                    fn()

            when([i == 0 for i in ids], lambda: comm.start(cin, cout, *sems))
            inner(*pre, *ins, *outs, *scr)
            when([i == n - 1 for i, n in zip(ids, grid)], lambda: comm.wait(cin, cout, *sems))

        in_specs += [hbm] * nc
        out_specs += [hbm] * nc
        out_shape += comm.out_shapes()
        scratch_shapes += comm.sem_shapes()
        args += comm.arrays
        sem = ("arbitrary",) * len(grid)
    kw = dict(name=name, out_shape=out_shape, compiler_params=_params(sem if grid else None))
    if prefetch:
        kw["grid_spec"] = pltpu.PrefetchScalarGridSpec(
            num_scalar_prefetch=prefetch, grid=grid, in_specs=in_specs, out_specs=out_specs,
            scratch_shapes=scratch_shapes)
    else:
        kw.update(in_specs=in_specs, out_specs=out_specs, scratch_shapes=scratch_shapes)
        if grid:
            kw["grid"] = grid
    res = list(pl.pallas_call(body, **kw)(*args))
    outs = res[:n_out]
    return (outs[0] if single else outs), res[n_out:]


def _comm_only(comm, name):
    return _call(lambda: None, name=name, grid=(), in_specs=[], out_specs=[], out_shape=[], args=[], comm=comm)[1]


def _mm(a, b, *, name, ta=False, b_mode="nn", out_mode="plain", out_dtype=F32, tm=512, tn=512, tk=512,
        epilogue=None, extra=None, comm=None):
    if ta:
        kdim, m = a.shape
    else:
        m, kdim = a.shape
    if b_mode == "nn":
        n = b.shape[1]
    elif b_mode == "nt":
        n = b.shape[0]
    elif b_mode == "nn_slots":
        n = b.shape[0] * b.shape[2]
        tn = b.shape[2]
    else:
        n = b.shape[1]
        tk = b.shape[2]
    if out_mode == "slots":
        tn = n // N_DEV
    tm, tn, tk = min(tm, m), min(tn, n), min(tk, kdim)
    assert m % tm == 0 and n % tn == 0 and kdim % tk == 0, (name, m, n, kdim, tm, tn, tk)
    nk = kdim // tk

    a_spec = pl.BlockSpec((tk, tm), lambda i, j, k: (k, i)) if ta else pl.BlockSpec((tm, tk), lambda i, j, k: (i, k))
    if b_mode == "nn":
        b_spec = pl.BlockSpec((tk, tn), lambda i, j, k: (k, j))
    elif b_mode == "nt":
        b_spec = pl.BlockSpec((tn, tk), lambda i, j, k: (j, k))
    elif b_mode == "nn_slots":
        b_spec = pl.BlockSpec((None, tk, tn), lambda i, j, k: (j, k, 0))
    else:
        b_spec = pl.BlockSpec((None, tn, tk), lambda i, j, k: (k, j, 0))
    tb = b_mode in ("nt", "nt_slots")
    if out_mode == "plain":
        o_shape, o_spec = (m, n), pl.BlockSpec((tm, tn), lambda i, j, k: (i, j))
    else:
        o_shape, o_spec = (N_DEV, m, tn), pl.BlockSpec((None, tm, tn), lambda i, j, k: (j, i, 0))
    dims = (((0 if ta else 1,), (1 if tb else 0,)), ((), ()))

    in_specs = [a_spec, b_spec]
    args = [a, b]
    if epilogue == "relu2_bwd":
        in_specs.append(pl.BlockSpec((tm, tn), lambda i, j, k: (i, j)))
        args.append(extra)
    if epilogue == "relu2":
        out_shape = [jax.ShapeDtypeStruct(o_shape, BF16), jax.ShapeDtypeStruct((n, m), BF16)]
        out_specs = [o_spec, pl.BlockSpec((tn, tm), lambda i, j, k: (j, i))]
    else:
        out_shape = jax.ShapeDtypeStruct(o_shape, out_dtype)
        out_specs = o_spec

    def finish(refs, acc):
        if epilogue == "relu2":
            r = jnp.maximum(acc, 0.0)
            r2 = (r * r).astype(BF16)
            refs[2][...] = r2
            refs[3][...] = r2.T
        elif epilogue == "relu2_bwd":
            refs[3][...] = (acc * (2.0 * jnp.sqrt(refs[2][...].astype(F32)))).astype(out_dtype)
        else:
            refs[2][...] = acc.astype(out_dtype)

    def body(*refs):
        part = lax.dot_general(refs[0][...], refs[1][...], dims, preferred_element_type=F32)
        if nk == 1:
            finish(refs, part)
            return
        acc_ref = refs[-1]
        k = pl.program_id(2)

        @pl.when(k == 0)
        def _():
            acc_ref[...] = part

        @pl.when(jnp.logical_and(k > 0, k < nk - 1))
        def _():
            acc_ref[...] += part

        @pl.when(k == nk - 1)
        def _():
            finish(refs, acc_ref[...] + part)

    outs, couts = _call(
        body, name=name, grid=(m // tm, n // tn, nk), in_specs=in_specs, out_specs=out_specs, out_shape=out_shape,
        args=args, scratch_shapes=[] if nk == 1 else [pltpu.VMEM((tm, tn), F32)],
        sem=("parallel", "parallel", "arbitrary"), comm=comm)
    return outs if comm is None else (outs, couts)


def _row_specs(d, nt):
    row = pl.BlockSpec((TM, d), lambda t: (t, 0))
    vec = pl.BlockSpec((1, d), lambda t: (0, 0))
    mod = pl.BlockSpec((None, None, MOD_ROWS, d), lambda t: (t // nt, jnp.minimum(t % nt, 1), 0, 0))
    return row, vec, mod


def _rms(x):
    r = lax.rsqrt(jnp.mean(x * x, axis=1, keepdims=True) + EPS)
    return r, x * r


def _token_stream(ctx, x, nt, name, comm=None):
    bl, c_rows, d = ctx.shape
    m = bl * nt * TM

    def body(ctx_ref, x_ref, h_ref):
        t = pl.program_id(0)

        @pl.when(t % nt == 0)
        def _():
            h_ref[...] = ctx_ref[...]

        @pl.when(t % nt != 0)
        def _():
            h_ref[...] = x_ref[...]

    outs, couts = _call(
        body, name=name, grid=(m // TM,),
        in_specs=[pl.BlockSpec((None, TM, d), lambda t: (t // nt, 0, 0)),
                  pl.BlockSpec((None, TM, d), lambda t: (t // nt, jnp.maximum(t % nt, 1) - 1, 0))],
        out_specs=pl.BlockSpec((TM, d), lambda t: (t, 0)), out_shape=jax.ShapeDtypeStruct((m, d), F32),
        args=[ctx, x], sem=("parallel",), comm=comm)
    return outs if comm is None else (outs, couts)


def _norm_mod_fwd(h, g, modsel, i_sh, i_sc, nt, name, comm=None):
    m, d = h.shape
    row, vec, mod = _row_specs(d, nt)

    def body(h_ref, g_ref, ms_ref, u_ref, ut_ref):
        _, xh = _rms(h_ref[...])
        ms = ms_ref[...]
        u = (xh * g_ref[...] * (1.0 + ms[i_sc:i_sc + 1]) + ms[i_sh:i_sh + 1]).astype(BF16)
        u_ref[...] = u
        ut_ref[...] = u.T

    outs, couts = _call(
        body, name=name, grid=(m // TM,), in_specs=[row, vec, mod],
        out_specs=[row, pl.BlockSpec((d, TM), lambda t: (0, t))],
        out_shape=[jax.ShapeDtypeStruct((m, d), BF16), jax.ShapeDtypeStruct((d, m), BF16)],
        args=[h, g, modsel], sem=("parallel",), comm=comm)
    return outs if comm is None else (outs, couts)


def _acc_rows(t, nt, dvec_ref, rows):
    first = (t % nt) <= 1

    @pl.when(first)
    def _():
        dvec_ref[...] = rows

    @pl.when(jnp.logical_not(first))
    def _():
        dvec_ref[...] += rows


def _norm_mod_bwd(h, g, modsel, du, dh_in, i_sh, i_sc, nt, name, comm=None, latent_only=False):
    m, d = h.shape
    row, vec, mod = _row_specs(d, nt)
    dh_rows, dh_spec = m, row
    if latent_only:
        dh_rows = m // nt * (nt - 1)
        dh_spec = pl.BlockSpec((TM, d), lambda t: ((t // nt) * (nt - 1) + jnp.maximum(t % nt, 1) - 1, 0))

    def body(h_ref, g_ref, ms_ref, du_ref, dhi_ref, dh_ref, dvec_ref):
        t = pl.program_id(0)
        r, xh = _rms(h_ref[...])
        g_ = g_ref[...]
        ms = ms_ref[...]
        du_ = du_ref[...]
        y = xh * g_
        dy = du_ * (1.0 + ms[i_sc:i_sc + 1])
        dxh = dy * g_
        dx = r * (dxh - xh * jnp.mean(dxh * xh, axis=1, keepdims=True))
        dh_ref[...] = dhi_ref[...] + dx
        rows = jnp.concatenate([
            jnp.sum(du_, axis=0, keepdims=True), jnp.sum(du_ * y, axis=0, keepdims=True),
            jnp.sum(dy * xh, axis=0, keepdims=True), jnp.zeros((MOD_ROWS - 3, d), F32)], axis=0)
        _acc_rows(t, nt, dvec_ref, rows)

    outs, couts = _call(
        body, name=name, grid=(m // TM,), in_specs=[row, vec, mod, row, row], out_specs=[dh_spec, mod],
        out_shape=[jax.ShapeDtypeStruct((dh_rows, d), F32), jax.ShapeDtypeStruct(modsel.shape, F32)],
        args=[h, g, modsel, du, dh_in], sem=("arbitrary",), comm=comm)
    return outs if comm is None else (outs, couts)


def _gate_rows(dh_, xh, g_, gate):
    dy = dh_ * gate
    return dy * g_, jnp.sum(dh_ * (xh * g_), axis=0, keepdims=True), jnp.sum(dy * xh, axis=0, keepdims=True)


def _gate_mod_fwd(h, z, g_post, ms_gate, i_g, g_pre, ms_mod, i_sh, i_sc, nt, name):
    m, d = h.shape
    row, vec, mod = _row_specs(d, nt)

    def body(h_ref, z_ref, gp_ref, msg_ref, gq_ref, msm_ref, hn_ref, u_ref, ut_ref):
        _, zh = _rms(z_ref[...])
        hn = h_ref[...] + msg_ref[...][i_g:i_g + 1] * (zh * gp_ref[...])
        hn_ref[...] = hn
        _, xh = _rms(hn)
        ms = msm_ref[...]
        u = (xh * gq_ref[...] * (1.0 + ms[i_sc:i_sc + 1]) + ms[i_sh:i_sh + 1]).astype(BF16)
        u_ref[...] = u
        ut_ref[...] = u.T

    return pl.pallas_call(
        body, name=name, grid=(m // TM,), in_specs=[row, row, vec, mod, vec, mod],
        out_specs=[row, row, pl.BlockSpec((d, TM), lambda t: (0, t))],
        out_shape=[jax.ShapeDtypeStruct((m, d), F32), jax.ShapeDtypeStruct((m, d), BF16),
                   jax.ShapeDtypeStruct((d, m), BF16)],
        compiler_params=_params(("parallel",)),
    )(h, z, g_post, ms_gate, g_pre, ms_mod)


def _mod_gate_bwd(h, g_pre, ms_mod, i_sh, i_sc, du, dh_in, z, g_post, ms_gate, i_g, nt, name):
    m, d = h.shape
    row, vec, mod = _row_specs(d, nt)

    def body(h_ref, gq_ref, msm_ref, du_ref, dhi_ref, z_ref, gp_ref, msg_ref, dh_ref, dvm_ref, dz_ref, dvg_ref):
        t = pl.program_id(0)
        r, xh = _rms(h_ref[...])
        gq = gq_ref[...]
        ms = msm_ref[...]
        du_ = du_ref[...]
        dy = du_ * (1.0 + ms[i_sc:i_sc + 1])
        dxh = dy * gq
        dh_ = dhi_ref[...] + r * (dxh - xh * jnp.mean(dxh * xh, axis=1, keepdims=True))
        dh_ref[...] = dh_
        _acc_rows(t, nt, dvm_ref, jnp.concatenate([
            jnp.sum(du_, axis=0, keepdims=True), jnp.sum(du_ * (xh * gq), axis=0, keepdims=True),
            jnp.sum(dy * xh, axis=0, keepdims=True), jnp.zeros((MOD_ROWS - 3, d), F32)], axis=0))
        rz, zh = _rms(z_ref[...])
        dzh, d_gate, d_gp = _gate_rows(dh_, zh, gp_ref[...], msg_ref[...][i_g:i_g + 1])
        dz_ref[...] = (rz * (dzh - zh * jnp.mean(dzh * zh, axis=1, keepdims=True))).astype(BF16)
        _acc_rows(t, nt, dvg_ref, jnp.concatenate([d_gate, d_gp, jnp.zeros((MOD_ROWS - 2, d), F32)], axis=0))

    return pl.pallas_call(
        body, name=name, grid=(m // TM,), in_specs=[row, vec, mod, row, row, row, vec, mod],
        out_specs=[row, mod, row, mod],
        out_shape=[jax.ShapeDtypeStruct((m, d), F32), jax.ShapeDtypeStruct(ms_mod.shape, F32),
                   jax.ShapeDtypeStruct((m, d), BF16), jax.ShapeDtypeStruct(ms_gate.shape, F32)],
        compiler_params=_params(("arbitrary",)),
    )(h, g_pre, ms_mod, du, dh_in, z, g_post, ms_gate)


def _gate_loss_bwd(h, z, g_post, modsel, i_g, target, nt, name):
    m, d = h.shape
    row, vec, mod = _row_specs(d, nt)
    ntl = nt - 1
    tgt = pl.BlockSpec((TM, d), lambda t: ((t // nt) * ntl + jnp.maximum(t % nt, 1) - 1, 0))
    acc = pl.BlockSpec((8, 128), lambda t: (0, 0))

    def body(h_ref, z_ref, gp_ref, ms_ref, t_ref, dh_ref, dz_ref, dvg_ref, ss_ref):
        t = pl.program_id(0)

        @pl.when(t == 0)
        def _():
            ss_ref[...] = jnp.zeros_like(ss_ref)

        latent = (t % nt != 0).astype(F32)
        rz, zh = _rms(z_ref[...])
        gp = gp_ref[...]
        gate = ms_ref[...][i_g:i_g + 1]
        e = h_ref[...] + gate * (zh * gp) - t_ref[...]
        ss_ref[...] += latent * jnp.sum(e * e)
        dh_ = e * (latent / d)
        dh_ref[...] = dh_
        dzh, d_gate, d_gp = _gate_rows(dh_, zh, gp, gate)
        dz_ref[...] = (rz * (dzh - zh * jnp.mean(dzh * zh, axis=1, keepdims=True))).astype(BF16)
        _acc_rows(t, nt, dvg_ref, jnp.concatenate([d_gate, d_gp, jnp.zeros((MOD_ROWS - 2, d), F32)], axis=0))

    return pl.pallas_call(
        body, name=name, grid=(m // TM,), in_specs=[row, row, vec, mod, tgt], out_specs=[row, row, mod, acc],
        out_shape=[jax.ShapeDtypeStruct((m, d), F32), jax.ShapeDtypeStruct((m, d), BF16),
                   jax.ShapeDtypeStruct(modsel.shape, F32), jax.ShapeDtypeStruct((8, 128), F32)],
        compiler_params=_params(("arbitrary",)),
    )(h, z, g_post, modsel, target)


QA, KA, VA, QB, KB, VB = 0, 512, 640, 768, 1280, 1408
PROJ_W = 1536
Q_SCALE = HEAD_DIM ** -0.5
LOG2E = 1.4426950408889634


def _swap16(x):
    lane = lax.broadcasted_iota(jnp.int32, x.shape, 1)
    n = x.shape[1]
    return jnp.where((lane % 32) < 16, pltpu.roll(x, n - 16, 1), pltpu.roll(x, 16, 1))


def _seg_mean(x, e):
    hi = x.astype(BF16)
    lo = (x - hi.astype(F32)).astype(BF16)
    return jnp.dot(hi, e, preferred_element_type=F32) + jnp.dot(lo, e, preferred_element_type=F32)


def _rope_tables(t_rows, c_rows):
    s = t_rows - c_rows
    row_ids = jnp.repeat(jnp.arange(s // GRID_W, dtype=jnp.int32), GRID_W).astype(F32)
    col_ids = jnp.tile(jnp.arange(GRID_W, dtype=jnp.int32), s // GRID_W).astype(F32)
    axis_dim = HEAD_DIM // 2
    inv = ROPE_THETA ** (-jnp.arange(0, axis_dim, 2, dtype=F32) / axis_dim)
    ang_r = row_ids[:, None] * inv[None, :]
    ang_c = col_ids[:, None] * inv[None, :]
    cos = jnp.concatenate([jnp.cos(ang_r), jnp.cos(ang_r), jnp.cos(ang_c), jnp.cos(ang_c)], axis=1)
    sin = jnp.concatenate([-jnp.sin(ang_r), jnp.sin(ang_r), -jnp.sin(ang_c), jnp.sin(ang_c)], axis=1)
    cos = jnp.concatenate([jnp.ones((c_rows, HEAD_DIM), F32), cos], axis=0)
    sin = jnp.concatenate([jnp.zeros((c_rows, HEAD_DIM), F32), sin], axis=0)
    return jnp.tile(cos, (1, 8)), jnp.tile(sin, (1, 8))


def _head_mean_matrix():
    i = np.arange(512)
    return jnp.asarray((i[:, None] // HEAD_DIM == i[None, :] // HEAD_DIM).astype(np.float32) / HEAD_DIM, dtype=BF16)


def _interleave_kv(k, v):
    return jnp.concatenate([k[:, :64], v[:, :64], k[:, 64:], v[:, 64:]], axis=1)


def _prep_fwd(proj, qn, kn, cos, sin, emat, nt, name):
    m = proj.shape[0]
    specs = [
        pl.BlockSpec((TM, PROJ_W), lambda t: (t, 0)),
        pl.BlockSpec((1, 512), lambda t: (0, 0)), pl.BlockSpec((1, 128), lambda t: (0, 0)),
        pl.BlockSpec((TM, 512), lambda t: (t % nt, 0)), pl.BlockSpec((TM, 512), lambda t: (t % nt, 0)),
        pl.BlockSpec((512, 512), lambda t: (0, 0)),
    ]

    def body(p_ref, qn_ref, kn_ref, cos_ref, sin_ref, e_ref, q_ref, kv_ref):
        cos_, sin_, e = cos_ref[...], sin_ref[...], e_ref[...]

        def rope(x, w):
            return x * cos_[:, :w] + _swap16(x) * sin_[:, :w]

        def norm(x, g, w):
            return x * lax.rsqrt(_seg_mean(x * x, e[:w, :w]) + EPS) * g

        qa = rope(norm(p_ref[:, QA:QA + 512], qn_ref[...], 512), 512)
        qb = rope(p_ref[:, QB:QB + 512], 512)
        q_ref[:, 0:512] = (qa * (Q_SCALE * LOG2E)).astype(BF16)
        q_ref[:, 512:1024] = (qb * (Q_SCALE * LOG2E)).astype(BF16)
        ka = rope(norm(p_ref[:, KA:KA + 128], kn_ref[...], 128), 128)
        kb = rope(p_ref[:, KB:KB + 128], 128)
        kv_ref[:, 0:256] = _interleave_kv(ka, p_ref[:, VA:VA + 128]).astype(BF16)
        kv_ref[:, 256:512] = _interleave_kv(kb, p_ref[:, VB:VB + 128]).astype(BF16)

    return pl.pallas_call(
        body, name=name, grid=(m // TM,), in_specs=specs,
        out_specs=[pl.BlockSpec((TM, 1024), lambda t: (t, 0)), pl.BlockSpec((TM, 512), lambda t: (t, 0))],
        out_shape=[jax.ShapeDtypeStruct((m, 1024), BF16), jax.ShapeDtypeStruct((m, 512), BF16)],
        compiler_params=_params(("parallel",)),
    )(proj, qn, kn, cos, sin, emat)


def _prep_bwd(proj, dq, dkv, qn, kn, cos, sin, emat, nt, name, comm=None):
    m = proj.shape[0]
    specs = [
        pl.BlockSpec((TM, PROJ_W), lambda t: (t, 0)),
        pl.BlockSpec((TM, 1024), lambda t: (t, 0)), pl.BlockSpec((TM, 512), lambda t: (t, 0)),
        pl.BlockSpec((1, 512), lambda t: (0, 0)), pl.BlockSpec((1, 128), lambda t: (0, 0)),
        pl.BlockSpec((TM, 512), lambda t: (t % nt, 0)), pl.BlockSpec((TM, 512), lambda t: (t % nt, 0)),
        pl.BlockSpec((512, 512), lambda t: (0, 0)),
    ]

    def body(p_ref, dq_ref, dkv_ref, qn_ref, kn_ref, cos_ref, sin_ref, e_ref, dp_ref, dqn_ref, dkn_ref):
        t = pl.program_id(0)
        cos_, sin_, e = cos_ref[...], sin_ref[...], e_ref[...]

        @pl.when(t == 0)
        def _():
            dqn_ref[...] = jnp.zeros_like(dqn_ref)
            dkn_ref[...] = jnp.zeros_like(dkn_ref)

        def unrope(dy, w):
            return dy * cos_[:, :w] + _swap16(dy * sin_[:, :w])

        def norm_bwd(x, g, dy, w):
            r = lax.rsqrt(_seg_mean(x * x, e[:w, :w]) + EPS)
            xh = x * r
            dxh = dy * g
            dx = r * (dxh - xh * _seg_mean(dxh * xh, e[:w, :w]))
            return dx, jnp.sum(dy * xh, axis=0, keepdims=True)

        dqa, dgq = norm_bwd(p_ref[:, QA:QA + 512], qn_ref[...], unrope(dq_ref[:, 0:512] * Q_SCALE, 512), 512)
        dp_ref[:, QA:QA + 512] = dqa.astype(BF16)
        dp_ref[:, QB:QB + 512] = unrope(dq_ref[:, 512:1024] * Q_SCALE, 512).astype(BF16)
        da = dkv_ref[:, 0:256]
        db = dkv_ref[:, 256:512]
        dka = jnp.concatenate([da[:, 0:64], da[:, 128:192]], axis=1)
        dva = jnp.concatenate([da[:, 64:128], da[:, 192:256]], axis=1)
        dkb = jnp.concatenate([db[:, 0:64], db[:, 128:192]], axis=1)
        dvb = jnp.concatenate([db[:, 64:128], db[:, 192:256]], axis=1)
        dka, dgk = norm_bwd(p_ref[:, KA:KA + 128], kn_ref[...], unrope(dka, 128), 128)
        dp_ref[:, KA:KA + 128] = dka.astype(BF16)
        dp_ref[:, VA:VA + 128] = dva.astype(BF16)
        dp_ref[:, KB:KB + 128] = unrope(dkb, 128).astype(BF16)
        dp_ref[:, VB:VB + 128] = dvb.astype(BF16)
        dqn_ref[0:1, :] += dgq
        dkn_ref[0:1, :] += dgk

    outs, couts = _call(
        body, name=name, grid=(m // TM,), in_specs=specs,
        out_specs=[pl.BlockSpec((TM, PROJ_W), lambda t: (t, 0)), pl.BlockSpec((8, 512), lambda t: (0, 0)),
                   pl.BlockSpec((8, 128), lambda t: (0, 0))],
        out_shape=[jax.ShapeDtypeStruct((m, PROJ_W), BF16), jax.ShapeDtypeStruct((8, 512), F32),
                   jax.ShapeDtypeStruct((8, 128), F32)],
        args=[proj, dq, dkv, qn, kn, cos, sin, emat], sem=("arbitrary",), comm=comm)
    return outs if comm is None else (outs, couts)


def _attn_case(t, hg, kv_ref, c_rows, t_rows, fn, keys_first=False):
    wl = TM + 2 * WINDOW
    kvd = lambda a, n: kv_ref[pl.ds(a, n), :]
    dense = hg < 2
    ctx = t == 0

    @pl.when(jnp.logical_and(dense, ctx))
    def _():
        kv = kvd(0, c_rows)
        fn(kv[:, :64], kv[:, 64:], None, False, [(0, c_rows)])

    @pl.when(jnp.logical_and(dense, jnp.logical_not(ctx)))
    def _():
        kv = kvd(0, t_rows)
        fn(kv[:, :64], kv[:, 64:], None, False, [(0, t_rows)])

    @pl.when(jnp.logical_and(jnp.logical_not(dense), ctx))
    def _():
        kv = kvd(0, c_rows)
        fn(kv[:, :64], kv[:, 64:], None, True, [(0, c_rows)])

    @pl.when(jnp.logical_and(jnp.logical_not(dense), jnp.logical_not(ctx)))
    def _():
        start = pl.multiple_of(jnp.minimum(c_rows + (t - 1) * TM - WINDOW, t_rows - wl), 128)
        kv = jnp.concatenate([kvd(0, c_rows), kvd(start, wl)], axis=0)
        shape = (c_rows + wl, TM) if keys_first else (TM, c_rows + wl)
        q_i = lax.broadcasted_iota(jnp.int32, shape, 1 if keys_first else 0)
        k_i = lax.broadcasted_iota(jnp.int32, shape, 0 if keys_first else 1)
        qpos = (t - 1) * TM + q_i
        kpos = start - 2 * c_rows + k_i
        mask = jnp.logical_or(k_i < c_rows, jnp.logical_and(jnp.abs(kpos - qpos) <= WINDOW, kpos >= 0))
        fn(kv[:, :64], kv[:, 64:], mask, True, [(0, c_rows), (start, wl)])


def _head_columns(cols):
    lane = lax.broadcasted_iota(jnp.int32, (TM, 128), 1)
    out = jnp.zeros((TM, 128), F32)
    for g, col in enumerate(cols):
        out = jnp.where(lane == g, col, out)
    return out


def _attn_specs(t_rows):
    nt = t_rows // TM
    q_spec = pl.BlockSpec((None, TM, Q_WIDTH), lambda b, hg, t, s: (b, t, hg))
    kv_spec = pl.BlockSpec((None, t_rows, 128), lambda b, hg, t, s: (b, 0, hg))
    lse_spec = pl.BlockSpec((None, TM, 128), lambda b, hg, t, s: (hg, b * nt + t, 0))
    return q_spec, kv_spec, lse_spec


def _attn_fwd(q_all, kv_all, sink8, c_rows, name, comm=None):
    bl, t_rows, _ = q_all.shape
    q_spec, kv_spec, lse_spec = _attn_specs(t_rows)

    def body(sink_ref, q_ref, kv_ref, o_ref, ot_ref, lse_ref):
        hg, t = pl.program_id(1), pl.program_id(2)

        def fn(k, v, mask, use_sink, spans):
            outs, lses = [], []
            v_one = jnp.concatenate([v, jnp.ones(v.shape, BF16)], axis=1)
            def scores(g):
                s = lax.dot_general(q_ref[:, g * 64:(g + 1) * 64], k, (((1,), (1,)), ((), ())),
                                    preferred_element_type=F32)
                return s if mask is None else jnp.where(mask, s, NEG_BIG)

            s_next = scores(0)
            for g in range(GROUP):
                s = s_next
                if g + 1 < GROUP:
                    s_next = scores(g + 1)
                mx = jnp.max(s, axis=1, keepdims=True)
                if use_sink:
                    sink = sink_ref[jnp.maximum(hg - 2, 0) * GROUP + g] * LOG2E
                    mx = jnp.maximum(mx, sink)
                pv = jnp.dot(jnp.exp2(s - mx).astype(BF16), v_one, preferred_element_type=F32)
                l = pv[:, 64:65]
                if use_sink:
                    l = l + jnp.exp2(sink - mx)
                outs.append(pv[:, :64] * (1.0 / l))
                lses.append(mx + jnp.log2(l))
            o = jnp.concatenate(outs, axis=1).astype(BF16)
            o_ref[...] = o
            ot_ref[...] = o.T
            lse_ref[...] = _head_columns(lses)

        _attn_case(t, hg, kv_ref, c_rows, t_rows, fn)

    nt = t_rows // TM
    ot_spec = pl.BlockSpec((Q_WIDTH, TM), lambda b, hg, t, s: (hg, b * nt + t))
    outs, couts = _call(
        body, name=name, grid=(bl, N_HG, nt), in_specs=[q_spec, kv_spec], out_specs=[q_spec, ot_spec, lse_spec],
        out_shape=[jax.ShapeDtypeStruct(q_all.shape, BF16), jax.ShapeDtypeStruct((N_HG * Q_WIDTH, bl * t_rows), BF16),
                   jax.ShapeDtypeStruct((N_HG, bl * t_rows, 128), F32)],
        args=[sink8, q_all, kv_all], prefetch=1, sem=("parallel", "parallel", "arbitrary"), comm=comm)
    return outs if comm is None else (outs, couts)


def _attn_bwd(q_all, kv_all, do, o, lse, sink8, c_rows, name, comm=None):
    bl, t_rows, _ = q_all.shape
    q_spec, kv_spec, lse_spec = _attn_specs(t_rows)
    ds_spec = pl.BlockSpec((None, None, 8, 128), lambda b, hg, t, s: (b, hg, 0, 0))

    def body(sink_ref, q_ref, kv_ref, do_ref, o_ref, lse_ref, dq_ref, dkv_ref, dsk_ref):
        hg, t = pl.program_id(1), pl.program_id(2)

        @pl.when(t == 0)
        def _():
            dkv_ref[...] = jnp.zeros_like(dkv_ref)
            dsk_ref[...] = jnp.zeros_like(dsk_ref)

        lse_rows = lse_ref[...].T
        dd_cols = [jnp.sum(do_ref[:, g * 64:(g + 1) * 64].astype(F32) * o_ref[:, g * 64:(g + 1) * 64].astype(F32),
                           axis=1, keepdims=True) for g in range(GROUP)]
        dd_rows = _head_columns(dd_cols).T

        def fn(k, v, mask, use_sink, spans):
            k_t = k.T
            dq_t, dsinks = [], []
            dk = jnp.zeros(k.shape, F32)
            dv = jnp.zeros(v.shape, F32)
            def products(g):
                s = lax.dot_general(k, q_ref[:, g * 64:(g + 1) * 64], (((1,), (1,)), ((), ())),
                                    preferred_element_type=F32)
                dp = lax.dot_general(v, do_ref[:, g * 64:(g + 1) * 64], (((1,), (1,)), ((), ())),
                                     preferred_element_type=F32)
                return (s if mask is None else jnp.where(mask, s, NEG_BIG)), dp

            nxt = products(0)
            for g in range(GROUP):
                q = q_ref[:, g * 64:(g + 1) * 64]
                do_g = do_ref[:, g * 64:(g + 1) * 64]
                lse_g, dd_g = lse_rows[g:g + 1, :], dd_rows[g:g + 1, :]
                s, dp = nxt
                if g + 1 < GROUP:
                    nxt = products(g + 1)
                pb = jnp.exp2(s - lse_g).astype(BF16)
                ds = (pb.astype(F32) * (dp - dd_g)).astype(BF16)
                dk = dk + jnp.dot(ds, q, preferred_element_type=F32)
                dv = dv + jnp.dot(pb, do_g, preferred_element_type=F32)
                dq_t.append(jnp.dot(k_t, ds, preferred_element_type=F32))
                if use_sink:
                    p_sink = jnp.exp2(sink_ref[jnp.maximum(hg - 2, 0) * GROUP + g] * LOG2E - lse_g)
                    dsinks.append(jnp.broadcast_to(-jnp.sum(p_sink * dd_g, axis=1, keepdims=True), (1, 128)))
            dq_ref[...] = jnp.concatenate(dq_t, axis=0).T
            dkv = jnp.concatenate([dk * (1.0 / LOG2E), dv], axis=1)
            off = 0
            for start, size in spans:
                dkv_ref[pl.ds(start, size), :] += dkv[off:off + size]
                off += size
            if use_sink:
                dsk_ref[0:GROUP, :] += jnp.concatenate(dsinks, axis=0)

        _attn_case(t, hg, kv_ref, c_rows, t_rows, fn, keys_first=True)

    outs, couts = _call(
        body, name=name, grid=(bl, N_HG, t_rows // TM), in_specs=[q_spec, kv_spec, q_spec, q_spec, lse_spec],
        out_specs=[q_spec, kv_spec, ds_spec],
        out_shape=[jax.ShapeDtypeStruct(q_all.shape, F32), jax.ShapeDtypeStruct(kv_all.shape, F32),
                   jax.ShapeDtypeStruct((bl, N_HG, 8, 128), F32)],
        args=[sink8, q_all, kv_all, do, o, lse], prefetch=1, sem=("parallel", "parallel", "arbitrary"), comm=comm)
    return outs if comm is None else (outs, couts)


def _silu(x):
    return x * jax.nn.sigmoid(x)


def _ada_fwd(c_rows, w_ada, b_cols, name, comm=None):
    nl, d, w = w_ada.shape
    r = c_rows.shape[0]

    def body(c_ref, w_ref, b_ref, o_ref):
        s = _silu(c_ref[...]).astype(BF16)
        o_ref[...] = jnp.dot(s, w_ref[...].astype(BF16), preferred_element_type=F32) + b_ref[...]

    outs, couts = _call(
        body, name=name, grid=(nl,),
        in_specs=[pl.BlockSpec((r, d), lambda l: (0, 0)), pl.BlockSpec((None, d, w), lambda l: (l, 0, 0)),
                  pl.BlockSpec((None, 1, w), lambda l: (l, 0, 0))],
        out_specs=pl.BlockSpec((None, r, w), lambda l: (l, 0, 0)),
        out_shape=jax.ShapeDtypeStruct((nl, r, w), F32), args=[c_rows, w_ada, b_cols], sem=("parallel",), comm=comm)
    return outs if comm is None else (outs, couts)


def _ada_bwd(c_rows, c_ctx, dmod, w_ada, name):
    nl, d, w = w_ada.shape
    r = c_rows.shape[0]

    def body(c_ref, cc_ref, g_ref, w_ref, dw_ref, dc_ref):
        l = pl.program_id(0)
        s = _silu(c_ref[...]).astype(BF16)
        gm = g_ref[...].astype(BF16)
        dw_ref[...] = lax.dot_general(s, gm, (((0,), (0,)), ((), ())), preferred_element_type=F32)
        ds = lax.dot_general(gm, w_ref[...].astype(BF16), (((1,), (1,)), ((), ())), preferred_element_type=F32)
        rows = lax.broadcasted_iota(jnp.int32, ds.shape, 0)
        dsc = jnp.sum(jnp.where(rows % ADA_ROWS == 2, ds, 0.0), axis=0, keepdims=True)
        x = cc_ref[...]
        sg = jax.nn.sigmoid(x)
        dcc = dsc * (sg * (1.0 + x * (1.0 - sg)))
        out = jnp.concatenate([dcc, jnp.zeros((7, d), F32)], axis=0)

        @pl.when(l == 0)
        def _():
            dc_ref[...] = out

        @pl.when(l != 0)
        def _():
            dc_ref[...] += out

    return pl.pallas_call(
        body, name=name, grid=(nl,),
        in_specs=[pl.BlockSpec((r, d), lambda l: (0, 0)), pl.BlockSpec((1, d), lambda l: (0, 0)),
                  pl.BlockSpec((None, r, w), lambda l: (l, 0, 0)), pl.BlockSpec((None, d, w), lambda l: (l, 0, 0))],
        out_specs=[pl.BlockSpec((None, d, w), lambda l: (l, 0, 0)), pl.BlockSpec((8, d), lambda l: (0, 0))],
        out_shape=[jax.ShapeDtypeStruct((nl, d, w), F32), jax.ShapeDtypeStruct((8, d), F32)],
        compiler_params=_params(("arbitrary",)),
    )(c_rows, c_ctx, dmod, w_ada)


def _adam_math(w, g, m, v):
    m = ADAM_B1 * m + (1.0 - ADAM_B1) * g
    v = ADAM_B2 * v + (1.0 - ADAM_B2) * (g * g)
    m_hat = m / (1.0 - ADAM_B1 ** ADAM_STEP)
    v_hat = v / (1.0 - ADAM_B2 ** ADAM_STEP)
    delta = -ADAM_LR * (m_hat / (jnp.sqrt(v_hat) + ADAM_EPS) + ADAM_WD * w)
    return delta, m, v


def _adamw(w, m, v, g_own, g_recv, name, rows=256):
    nl, r, c = w.shape
    tr = min(rows, r)
    spec = pl.BlockSpec((None, tr, c), lambda l, i: (l, i, 0))
    per_layer = isinstance(g_own, (list, tuple))
    own = list(g_own) if per_layer else [g_own]
    recv = [] if g_recv is None else list(g_recv)
    in_specs = [spec] * 3 + [pl.BlockSpec((tr, c), lambda l, i: (i, 0)) if per_layer else spec] * len(own)
    in_specs += [pl.BlockSpec((N_DEV, tr, c), lambda l, i: (0, i, 0))] * len(recv)

    def body(*refs):
        w_ref, m_ref, v_ref = refs[:3]
        own_refs, recv_refs = refs[3:3 + len(own)], refs[3 + len(own):3 + len(own) + len(recv)]
        go_ref, d_ref, mo_ref, vo_ref = refs[-4:]

        def update(li):
            g = own_refs[li][...].astype(F32)
            if recv:
                for k in range(1, N_DEV):
                    g = g + recv_refs[li][k].astype(F32)
            delta, m_, v_ = _adam_math(w_ref[...], g, m_ref[...], v_ref[...])
            go_ref[...] = g
            d_ref[...] = delta
            mo_ref[...] = m_
            vo_ref[...] = v_

        if per_layer:
            for li in range(nl):
                pl.when(pl.program_id(0) == li)(functools.partial(update, li))
        else:
            update(0)

    return pl.pallas_call(
        body, name=name, grid=(nl, r // tr), in_specs=in_specs, out_specs=[spec] * 4,
        out_shape=[jax.ShapeDtypeStruct(w.shape, F32)] * 4, compiler_params=_params(("parallel", "parallel")),
    )(w, m, v, *own, *recv)


def _small_adamw(w, m, v, g_all, name):
    def body(w_ref, m_ref, v_ref, g_ref, go_ref, d_ref, mo_ref, vo_ref):
        g = g_ref[0]
        for k in range(1, N_DEV):
            g = g + g_ref[k]
        delta, m_, v_ = _adam_math(w_ref[...], g, m_ref[...], v_ref[...])
        go_ref[...] = g
        d_ref[...] = delta
        mo_ref[...] = m_
        vo_ref[...] = v_

    return pl.pallas_call(
        body, name=name, out_shape=[jax.ShapeDtypeStruct(w.shape, F32)] * 4, compiler_params=_params(),
    )(w, m, v, g_all)


SMALL = ("c_ctx", "b_ada", "g_pre_mix", "g_post_mix", "g_pre_mlp", "g_post_mlp", "q_norm", "k_norm", "sink", "loss")


def _pack_small(parts):
    flat = jnp.concatenate([parts[n].reshape(-1) for n in SMALL])
    rows = -(-flat.shape[0] // 1024) * 8
    return jnp.pad(flat, (0, rows * 128 - flat.shape[0])).reshape(rows, 128)


def _unpack_small(packed, like):
    flat = packed.reshape(-1)
    out, off = {}, 0
    for n in SMALL:
        size = int(np.prod(like[n].shape))
        out[n] = flat[off:off + size].reshape(like[n].shape)
        off += size
    return out


def kernel(x, c, ctx, c_ctx, w_ada, b_ada, g_pre_mix, g_post_mix, g_pre_mlp, g_post_mlp, w_in, q_norm, k_norm, sink, w_out, w_up, w_down, loss_target, m_c_ctx, m_w_ada, m_b_ada, m_g_pre_mix, m_g_post_mix, m_g_pre_mlp, m_g_post_mlp, m_w_in, m_q_norm, m_k_norm, m_sink, m_w_out, m_w_up, m_w_down, v_c_ctx, v_w_ada, v_b_ada, v_g_pre_mix, v_g_post_mix, v_g_pre_mlp, v_g_post_mlp, v_w_in, v_q_norm, v_k_norm, v_sink, v_w_out, v_w_up, v_w_down):
    bl, s_rows, d = x.shape
    c_rows = ctx.shape[1]
    assert c_rows == TM and s_rows % TM == 0 and bl == 2
    t_rows = c_rows + s_rows
    nt = t_rows // TM
    m_rows = bl * t_rows
    nl = w_in.shape[0]
    ada_w = w_ada.shape[2]
    d_ff = w_up.shape[2] * N_DEV
    me = _my_index()

    shard = lambda w_, l: w_[l].astype(BF16)
    c_pad = jnp.concatenate([c, c_ctx[None, :], jnp.zeros((ADA_ROWS - bl - 1, d), F32)], axis=0)
    gathered = {0: {}}
    h, (c_all, gathered[0]["w_in"]) = _token_stream(
        ctx, x, nt, "token_stream", comm=_Comm([(c_pad, GATHER), (shard(w_in, 0), GATHER)]))
    c_all = c_all.reshape(N_DEV * ADA_ROWS, d)

    def layer_weights(l):
        g_ = gathered[l]
        w_out_f = g_["w_out"].reshape(-1, d)
        w_down_f = g_["w_down"].reshape(d_ff, d)
        return dict(
            w_in_t=g_["w_in"].transpose(0, 2, 1).reshape(PROJ_W, d), w_out_f=w_out_f, w_out_t=w_out_f.T,
            w_up_s=g_["w_up"], w_up_t=g_["w_up"].transpose(0, 2, 1).reshape(d_ff, d),
            w_down_f=w_down_f, w_down_t=w_down_f.T)

    big = dict(tm=2304, tn=512)
    deep = dict(tm=1536, tn=512, tk=d_ff)
    wide = dict(tm=1024, tn=512, tk=m_rows)

    b_cols = lax.dynamic_slice(b_ada, (0, me * ada_w), (nl, ada_w))[:, None, :]
    mod_cols = _ada_fwd(c_all, w_ada, b_cols, "ada_fwd")
    mod_slots = mod_cols.reshape(nl, N_DEV, ADA_ROWS, ada_w).transpose(1, 0, 2, 3)
    mod_g, = _comm_only(_Comm([(mod_slots, TO_OWNER)]), "exchange_mod")
    mine = mod_g.transpose(1, 2, 0, 3).reshape(nl, ADA_ROWS, N_MOD, d)
    pad = jnp.zeros((bl, 2, MOD_ROWS - N_MOD, d), F32)
    modsel = [jnp.concatenate([jnp.stack([jnp.broadcast_to(mine[l, bl], (bl, N_MOD, d)), mine[l, :bl]], axis=1), pad],
                              axis=2) for l in range(nl)]

    cos, sin = _rope_tables(t_rows, c_rows)
    emat = _head_mean_matrix()
    row = lambda a: a[None, :]
    qn = [jnp.tile(q_norm[l], 8)[None, :] for l in range(nl)]
    kn = [jnp.tile(k_norm[l], 2)[None, :] for l in range(nl)]

    target = loss_target.reshape(bl * s_rows, d)
    saved = []
    weights_of = {}
    (u, u_t), (gathered[0]["w_out"],) = _norm_mod_fwd(
        h, row(g_pre_mix[0]), modsel[0], 0, 1, nt, "mix_mod_fwd0", comm=_Comm([(shard(w_out, 0), GATHER)]))
    for l in range(nl):
        w_in_f = gathered[l]["w_in"].transpose(1, 0, 2).reshape(d, PROJ_W)
        proj = _mm(u, w_in_f, name=f"mm_in{l}", tk=d, **big)
        q_all, kv_all = _prep_fwd(proj, qn[l], kn[l], cos, sin, emat, nt, f"prep_fwd{l}")
        (o, o_t, lse), (w_up_g, w_down_g) = _attn_fwd(
            q_all.reshape(bl, t_rows, 1024), kv_all.reshape(bl, t_rows, 512), sink[l], c_rows, f"attn_fwd{l}",
            comm=_Comm([(shard(w_up, l), GATHER), (shard(w_down, l), GATHER)]))
        o = o.reshape(m_rows, 1024)
        gathered[l].update(w_up=w_up_g, w_down=w_down_g)
        wl = weights_of[l] = layer_weights(l)
        mix = _mm(o, wl["w_out_f"], name=f"mm_out{l}", tk=1024, **big)
        h_mid, v_in, v_t = _gate_mod_fwd(h, mix, row(g_post_mix[l]), modsel[l], 2, row(g_pre_mlp[l]), modsel[l], 3, 4,
                                         nt, f"mix_gate_mlp_mod_fwd{l}")
        more = l + 1 < nl
        res_up = _mm(v_in, wl["w_up_s"], name=f"mm_up{l}", b_mode="nn_slots", epilogue="relu2", tk=d,
                     comm=_Comm([(shard(w_in, l + 1), GATHER)]) if more else None, **big)
        (r_act, r_t), nxt_in = res_up if more else (res_up, None)
        res_down = _mm(r_act, wl["w_down_f"], name=f"mm_down{l}",
                       comm=_Comm([(shard(w_out, l + 1), GATHER)]) if more else None, **deep)
        y, nxt_out = res_down if more else (res_down, None)
        saved.append((h, u_t, proj, q_all, kv_all, o, o_t, lse, mix, h_mid, v_t, r_act, r_t, y))
        if more:
            gathered[l + 1] = dict(w_in=nxt_in[0], w_out=nxt_out[0])
            h, u, u_t = _gate_mod_fwd(h_mid, y, row(g_post_mlp[l]), modsel[l], 5, row(g_pre_mix[l + 1]), modsel[l + 1],
                                      0, 1, nt, f"mlp_gate_mix_mod_fwd{l}")

    dh, dy, dvec_g2, ss = _gate_loss_bwd(h_mid, y, row(g_post_mlp[nl - 1]), modsel[nl - 1], 5, target, nt, "gate_loss_bwd")
    small_g = {n: [None] * nl for n in SMALL if n not in ("c_ctx", "b_ada", "loss")}
    dvecs = {l: {} for l in range(nl)}
    dvecs[nl - 1]["g2"] = dvec_g2
    slots = {n: [None] * nl for n in ("w_in", "w_out", "w_up", "w_down")}
    recvd = {n: [None] * nl for n in slots}
    send = lambda n, l_: (slots[n][l_], TO_OWNER_XOR)
    for l in reversed(range(nl)):
        h_in, u_t, proj, q_all, kv_all, o, o_t, lse, mix, h_mid, v_t, r_act, r_t, y = saved[l]
        wl = weights_of[l]
        later = l + 1 < nl
        res = _mm(dy, wl["w_down_t"], name=f"mm_da{l}", epilogue="relu2_bwd", extra=r_act, out_dtype=BF16, tk=d,
                  comm=_Comm([send("w_out", l + 1)]) if later else None, **big)
        da = res[0] if later else res
        if later:
            recvd["w_out"][l + 1] = res[1][0]
        res = _mm(r_t, dy, name=f"mm_dw_down{l}", out_dtype=BF16,
                  comm=_Comm([send("w_in", l + 1)]) if later else None, **wide)
        dw_down = res[0] if later else res
        if later:
            recvd["w_in"][l + 1] = res[1][0]
        dw_up = _mm(v_t, da, name=f"mm_dw_up{l}", out_mode="slots", out_dtype=BF16, **wide)
        slots["w_down"][l] = dw_down.reshape(N_DEV, -1, d)
        slots["w_up"][l] = dw_up
        dv = _mm(da, wl["w_up_t"], name=f"mm_dv{l}", **deep)
        dh, dvecs[l]["m2"], dmix, dvecs[l]["g1"] = _mod_gate_bwd(
            h_mid, row(g_pre_mlp[l]), modsel[l], 3, 4, dv, dh, mix, row(g_post_mix[l]), modsel[l], 2, nt,
            f"mlp_mod_mix_gate_bwd{l}")
        do = _mm(dmix, wl["w_out_t"], name=f"mm_do{l}", out_dtype=BF16, tk=d, **big)
        dw_out = _mm(o_t, dmix, name=f"mm_dw_out{l}", out_dtype=BF16, **wide)
        slots["w_out"][l] = dw_out.reshape(N_DEV, -1, d)
        (dq, dkv, dsk), (recvd["w_down"][l], recvd["w_up"][l]) = _attn_bwd(
            q_all.reshape(bl, t_rows, 1024), kv_all.reshape(bl, t_rows, 512), do.reshape(bl, t_rows, 1024),
            o.reshape(bl, t_rows, 1024), lse, sink[l], c_rows, f"attn_bwd{l}",
            comm=_Comm([send("w_down", l), send("w_up", l)]))
        last = l == 0
        res = _prep_bwd(proj, dq.reshape(m_rows, 1024), dkv.reshape(m_rows, 512), qn[l], kn[l], cos, sin, emat, nt,
                        f"prep_bwd{l}", comm=_Comm([send("w_out", l)]) if last else None)
        dproj, dqn, dkn = res[0] if last else res
        if last:
            recvd["w_out"][l] = res[1][0]
        dw_in = _mm(u_t, dproj, name=f"mm_dw_in{l}", out_dtype=BF16, **wide)
        slots["w_in"][l] = dw_in.reshape(d, N_DEV, PROJ_W // N_DEV).transpose(1, 0, 2)
        du = _mm(dproj, wl["w_in_t"], name=f"mm_du{l}", tk=PROJ_W, **big)
        if last:
            (dh, dvecs[l]["m1"]), (recvd["w_in"][l],) = _norm_mod_bwd(
                h_in, row(g_pre_mix[l]), modsel[l], du, dh, 0, 1, nt, f"mix_mod_bwd{l}", latent_only=True,
                comm=_Comm([send("w_in", l)]))
        else:
            dh, dvecs[l]["m1"], dy, dvecs[l - 1]["g2"] = _mod_gate_bwd(
                h_in, row(g_pre_mix[l]), modsel[l], 0, 1, du, dh, saved[l - 1][-1], row(g_post_mlp[l - 1]),
                modsel[l - 1], 5, nt, f"mix_mod_mlp_gate_bwd{l}")
        small_g["q_norm"][l] = jnp.sum(dqn[0].reshape(8, HEAD_DIM), axis=0)
        small_g["k_norm"][l] = jnp.sum(dkn[0].reshape(2, HEAD_DIM), axis=0)
        small_g["sink"][l] = jnp.sum(dsk[:, 2:, :GROUP, 0], axis=0).reshape(-1)

    dmod_rows = []
    for l in range(nl):
        m1, g1, m2, g2 = (dvecs[l][n] for n in ("m1", "g1", "m2", "g2"))
        small_g["g_pre_mix"][l] = jnp.sum(m1[:, :, 2], axis=(0, 1))
        small_g["g_post_mix"][l] = jnp.sum(g1[:, :, 1], axis=(0, 1))
        small_g["g_pre_mlp"][l] = jnp.sum(m2[:, :, 2], axis=(0, 1))
        small_g["g_post_mlp"][l] = jnp.sum(g2[:, :, 1], axis=(0, 1))
        dms = jnp.stack([m1[:, :, 0], m1[:, :, 1], g1[:, :, 0], m2[:, :, 0], m2[:, :, 1], g2[:, :, 0]], axis=2)
        rows = jnp.concatenate([dms[:, 1], jnp.sum(dms[:, 0], axis=0)[None]], axis=0)
        dmod_rows.append(jnp.pad(rows.reshape(bl + 1, N_MOD * d), ((0, ADA_ROWS - bl - 1), (0, 0))))
    grad_x = dh.reshape(bl, s_rows, d)

    dmod_slots = jnp.stack(dmod_rows).reshape(nl, ADA_ROWS, N_DEV, ada_w).transpose(2, 0, 1, 3)
    dmod_g, = _comm_only(_Comm([(dmod_slots, TO_OWNER)]), "exchange_dmod")
    dmod_mine = dmod_g.transpose(1, 0, 2, 3).reshape(nl, N_DEV * ADA_ROWS, ada_w)
    dw_ada, dcc = _ada_bwd(c_all, c_ctx[None, :], dmod_mine, w_ada, "ada_bwd")

    parts = {n: jnp.stack(small_g[n]) for n in small_g}
    parts["c_ctx"] = dcc[0]
    parts["loss"] = ss[0, 0:1]
    no_loss = jnp.zeros((1,), F32)
    parts["b_ada"] = jnp.stack([jnp.sum(r_[: bl + 1], axis=0) for r_ in dmod_rows])
    weights = dict(c_ctx=c_ctx, b_ada=b_ada, g_pre_mix=g_pre_mix, g_post_mix=g_post_mix, g_pre_mlp=g_pre_mlp,
                   g_post_mlp=g_post_mlp, q_norm=q_norm, k_norm=k_norm, sink=sink, loss=no_loss)
    moms = dict(c_ctx=m_c_ctx, b_ada=m_b_ada, g_pre_mix=m_g_pre_mix, g_post_mix=m_g_post_mix, g_pre_mlp=m_g_pre_mlp,
                g_post_mlp=m_g_post_mlp, q_norm=m_q_norm, k_norm=m_k_norm, sink=m_sink, loss=no_loss)
    vels = dict(c_ctx=v_c_ctx, b_ada=v_b_ada, g_pre_mix=v_g_pre_mix, g_post_mix=v_g_post_mix, g_pre_mlp=v_g_pre_mlp,
                g_post_mlp=v_g_post_mlp, q_norm=v_q_norm, k_norm=v_k_norm, sink=v_sink, loss=no_loss)
    small_all, = _comm_only(_Comm([(_pack_small(parts), GATHER)]), "gather_small")
    s_out = _small_adamw(_pack_small(weights), _pack_small(moms), _pack_small(vels), small_all, "adamw_small")
    s_g, s_d, s_m, s_v = [_unpack_small(a, weights) for a in s_out]
    loss = 0.5 * s_g["loss"][0] / d

    res = {}
    for n, w_, m_, v_ in (("w_in", w_in, m_w_in, v_w_in), ("w_out", w_out, m_w_out, v_w_out),
                          ("w_up", w_up, m_w_up, v_w_up), ("w_down", w_down, m_w_down, v_w_down)):
        own = [lax.dynamic_index_in_dim(slots[n][l], me, axis=0, keepdims=False) for l in range(nl)]
        res[n] = _adamw(w_, m_, v_, own, recvd[n], f"adamw_{n}")
    res["w_ada"] = _adamw(w_ada, m_w_ada, v_w_ada, dw_ada, None, "adamw_w_ada")

    order = ("c_ctx", "w_ada", "b_ada", "g_pre_mix", "g_post_mix", "g_pre_mlp", "g_post_mlp", "w_in", "q_norm",
             "k_norm", "sink", "w_out", "w_up", "w_down")
    outs = [loss, grad_x]
    for i, small in enumerate((s_g, s_d, s_m, s_v)):
        outs += [small[n] if n in small else res[n][i] for n in order]
    return tuple(outs)
```

```python
import functools

import jax
import jax.numpy as jnp
import numpy as np
from jax import lax
from jax.experimental import pallas as pl
from jax.experimental.pallas import tpu as pltpu

F32 = jnp.float32
BF16 = jnp.bfloat16

HEAD_DIM = 64
GROUP = 4
N_HG = 4
Q_WIDTH = GROUP * HEAD_DIM
WINDOW = 128
GRID_W = 64
ROPE_THETA = 10000.0
EPS = 1e-6
NEG_BIG = -1e30
N_MOD = 6
MOD_ROWS = 8
TM = 256
N_DEV = 8
ADA_ROWS = 8
VMEM_LIMIT = 56 * 1024 * 1024

ADAM_LR = 0.001
ADAM_B1 = 0.9
ADAM_B2 = 0.999
ADAM_EPS = 1e-08
ADAM_WD = 0.01
ADAM_STEP = 10


def _params(sem=None):
    kw = dict(vmem_limit_bytes=VMEM_LIMIT)
    if sem is not None:
        kw["dimension_semantics"] = sem
    return pltpu.CompilerParams(**kw)


def _my_index():
    return 4 * lax.axis_index("x") + 2 * lax.axis_index("y") + lax.axis_index("c")


def _peer(k):
    x, y, c = lax.axis_index("x"), lax.axis_index("y"), lax.axis_index("c")
    kx, ky, kc = (k >> 2) & 1, (k >> 1) & 1, k & 1
    px = (1 - x) if kx else x
    py = (1 - y) if ky else y
    pc = (1 - c) if kc else c
    return (px, py, pc), 4 * px + 2 * py + pc


GATHER, TO_OWNER, TO_OWNER_XOR = "gather", "to_owner", "to_owner_xor"


class _Comm:
    def __init__(self, items):
        self.items = list(items)
        self.arrays = [a for a, _ in self.items]

    def out_shapes(self):
        return [jax.ShapeDtypeStruct(((N_DEV,) + a.shape) if kind == GATHER else a.shape, a.dtype)
                for a, kind in self.items]

    def sem_shapes(self):
        n = len(self.items) * N_DEV
        return [pltpu.SemaphoreType.DMA((n,)), pltpu.SemaphoreType.DMA((n,))]

    def _copies(self, in_refs, out_refs, send_sems, recv_sems):
        me = _my_index()
        local, remote = [], []
        for i, ((_, kind), x_ref, o_ref) in enumerate(zip(self.items, in_refs, out_refs)):
            base = i * N_DEV
            own_src = x_ref if kind == GATHER else x_ref.at[me]
            own_dst = o_ref.at[0] if kind == TO_OWNER_XOR else o_ref.at[me]
            local.append(pltpu.make_async_copy(own_src, own_dst, send_sems.at[base]))
            for k in range(1, N_DEV):
                peer, pidx = _peer(k)
                remote.append(pltpu.make_async_remote_copy(
                    src_ref=x_ref if kind == GATHER else x_ref.at[pidx],
                    dst_ref=o_ref.at[k] if kind == TO_OWNER_XOR else o_ref.at[me],
                    send_sem=send_sems.at[base + k], recv_sem=recv_sems.at[base + k],
                    device_id=peer, device_id_type=pl.DeviceIdType.MESH))
        return local, remote

    def start(self, in_refs, out_refs, send_sems, recv_sems):
        local, remote = self._copies(in_refs, out_refs, send_sems, recv_sems)
        for cp in local + remote:
            cp.start()

    def wait(self, in_refs, out_refs, send_sems, recv_sems):
        local, remote = self._copies(in_refs, out_refs, send_sems, recv_sems)
        for cp in remote:
            cp.wait_recv()
        for cp in remote:
            cp.wait_send()
        for cp in local:
            cp.wait()


def _call(body, *, name, grid, in_specs, out_specs, out_shape, args, scratch_shapes=(), prefetch=0, sem=None,
          comm=None):
    single = not isinstance(out_shape, (list, tuple))
    out_shape = [out_shape] if single else list(out_shape)
    out_specs = [out_specs] if single else list(out_specs)
    in_specs, scratch_shapes, args = list(in_specs), list(scratch_shapes), list(args)
    n_in, n_out = len(in_specs), len(out_shape)
    if comm is not None:
        nc = len(comm.arrays)
        hbm = pl.BlockSpec(memory_space=pl.ANY)
        inner = body

        def body(*refs):
            pre, r = refs[:prefetch], refs[prefetch:]
            ins, cin = r[:n_in], r[n_in:n_in + nc]
            outs, cout = r[n_in + nc:n_in + nc + n_out], r[n_in + nc + n_out:n_in + 2 * nc + n_out]
            scr, sems = r[n_in + 2 * nc + n_out:len(r) - 2], r[len(r) - 2:]
            ids = [pl.program_id(i) for i in range(len(grid))]

            def when(flags, fn):
                if flags:
                    pl.when(functools.reduce(jnp.logical_and, flags))(fn)
                else:
                    fn()

            when([i == 0 for i in ids], lambda: comm.start(cin, cout, *sems))
            inner(*pre, *ins, *outs, *scr)
            when([i == n - 1 for i, n in zip(ids, grid)], lambda: comm.wait(cin, cout, *sems))

        in_specs += [hbm] * nc
        out_specs += [hbm] * nc
        out_shape += comm.out_shapes()
        scratch_shapes += comm.sem_shapes()
        args += comm.arrays
        sem = ("arbitrary",) * len(grid)
    kw = dict(name=name, out_shape=out_shape, compiler_params=_params(sem if grid else None))
    if prefetch:
        kw["grid_spec"] = pltpu.PrefetchScalarGridSpec(
            num_scalar_prefetch=prefetch, grid=grid, in_specs=in_specs, out_specs=out_specs,
            scratch_shapes=scratch_shapes)
    else:
        kw.update(in_specs=in_specs, out_specs=out_specs, scratch_shapes=scratch_shapes)
        if grid:
            kw["grid"] = grid
    res = list(pl.pallas_call(body, **kw)(*args))
    outs = res[:n_out]
    return (outs[0] if single else outs), res[n_out:]


def _comm_only(comm, name):
    return _call(lambda: None, name=name, grid=(), in_specs=[], out_specs=[], out_shape=[], args=[], comm=comm)[1]


def _mm(a, b, *, name, ta=False, b_mode="nn", out_mode="plain", out_dtype=F32, tm=512, tn=512, tk=512,
        epilogue=None, extra=None, comm=None):
    if ta:
        kdim, m = a.shape
    else:
        m, kdim = a.shape
    if b_mode == "nn":
        n = b.shape[1]
    elif b_mode == "nt":
        n = b.shape[0]
    elif b_mode == "nn_slots":
        n = b.shape[0] * b.shape[2]
        tn = b.shape[2]
    else:
        n = b.shape[1]
        tk = b.shape[2]
    if out_mode == "slots":
        tn = n // N_DEV
    tm, tn, tk = min(tm, m), min(tn, n), min(tk, kdim)
    assert m % tm == 0 and n % tn == 0 and kdim % tk == 0, (name, m, n, kdim, tm, tn, tk)
    nk = kdim // tk

    a_spec = pl.BlockSpec((tk, tm), lambda i, j, k: (k, i)) if ta else pl.BlockSpec((tm, tk), lambda i, j, k: (i, k))
    if b_mode == "nn":
        b_spec = pl.BlockSpec((tk, tn), lambda i, j, k: (k, j))
    elif b_mode == "nt":
        b_spec = pl.BlockSpec((tn, tk), lambda i, j, k: (j, k))
    elif b_mode == "nn_slots":
        b_spec = pl.BlockSpec((None, tk, tn), lambda i, j, k: (j, k, 0))
    else:
        b_spec = pl.BlockSpec((None, tn, tk), lambda i, j, k: (k, j, 0))
    tb = b_mode in ("nt", "nt_slots")
    if out_mode == "plain":
        o_shape, o_spec = (m, n), pl.BlockSpec((tm, tn), lambda i, j, k: (i, j))
    else:
        o_shape, o_spec = (N_DEV, m, tn), pl.BlockSpec((None, tm, tn), lambda i, j, k: (j, i, 0))
    dims = (((0 if ta else 1,), (1 if tb else 0,)), ((), ()))

    in_specs = [a_spec, b_spec]
    args = [a, b]
    if epilogue == "relu2_bwd":
        in_specs.append(pl.BlockSpec((tm, tn), lambda i, j, k: (i, j)))
        args.append(extra)
    if epilogue == "relu2":
        out_shape = [jax.ShapeDtypeStruct(o_shape, BF16), jax.ShapeDtypeStruct((n, m), BF16)]
        out_specs = [o_spec, pl.BlockSpec((tn, tm), lambda i, j, k: (j, i))]
    else:
        out_shape = jax.ShapeDtypeStruct(o_shape, out_dtype)
        out_specs = o_spec

    def finish(refs, acc):
        if epilogue == "relu2":
            r = jnp.maximum(acc, 0.0)
            r2 = (r * r).astype(BF16)
            refs[2][...] = r2
            refs[3][...] = r2.T
        elif epilogue == "relu2_bwd":
            refs[3][...] = (acc * (2.0 * jnp.sqrt(refs[2][...]).astype(F32))).astype(out_dtype)
        else:
            refs[2][...] = acc.astype(out_dtype)

    def body(*refs):
        part = lax.dot_general(refs[0][...], refs[1][...], dims, preferred_element_type=F32)
        if nk == 1:
            finish(refs, part)
            return
        acc_ref = refs[-1]
        k = pl.program_id(2)

        @pl.when(k == 0)
        def _():
            acc_ref[...] = part

        @pl.when(jnp.logical_and(k > 0, k < nk - 1))
        def _():
            acc_ref[...] += part

        @pl.when(k == nk - 1)
        def _():
            finish(refs, acc_ref[...] + part)

    outs, couts = _call(
        body, name=name, grid=(m // tm, n // tn, nk), in_specs=in_specs, out_specs=out_specs, out_shape=out_shape,
        args=args, scratch_shapes=[] if nk == 1 else [pltpu.VMEM((tm, tn), F32)],
        sem=("parallel", "parallel", "arbitrary"), comm=comm)
    return outs if comm is None else (outs, couts)


def _row_specs(d, nt):
    row = pl.BlockSpec((TM, d), lambda t: (t, 0))
    vec = pl.BlockSpec((1, d), lambda t: (0, 0))
    mod = pl.BlockSpec((None, None, MOD_ROWS, d), lambda t: (t // nt, jnp.minimum(t % nt, 1), 0, 0))
    return row, vec, mod


def _rms(x):
    r = lax.rsqrt(jnp.mean(x * x, axis=1, keepdims=True) + EPS)
    return r, x * r


def _token_stream(ctx, x, nt, name, comm=None):
    bl, c_rows, d = ctx.shape
    m = bl * nt * TM

    def body(ctx_ref, x_ref, h_ref):
        t = pl.program_id(0)

        @pl.when(t % nt == 0)
        def _():
            h_ref[...] = ctx_ref[...]

        @pl.when(t % nt != 0)
        def _():
            h_ref[...] = x_ref[...]

    outs, couts = _call(
        body, name=name, grid=(m // TM,),
        in_specs=[pl.BlockSpec((None, TM, d), lambda t: (t // nt, 0, 0)),
                  pl.BlockSpec((None, TM, d), lambda t: (t // nt, jnp.maximum(t % nt, 1) - 1, 0))],
        out_specs=pl.BlockSpec((TM, d), lambda t: (t, 0)), out_shape=jax.ShapeDtypeStruct((m, d), F32),
        args=[ctx, x], sem=("parallel",), comm=comm)
    return outs if comm is None else (outs, couts)


def _norm_mod_fwd(h, g, modsel, i_sh, i_sc, nt, name, comm=None):
    m, d = h.shape
    row, vec, mod = _row_specs(d, nt)

    def body(h_ref, g_ref, ms_ref, u_ref, ut_ref):
        _, xh = _rms(h_ref[...])
        ms = ms_ref[...]
        u = (xh * g_ref[...] * (1.0 + ms[i_sc:i_sc + 1]) + ms[i_sh:i_sh + 1]).astype(BF16)
        u_ref[...] = u
        ut_ref[...] = u.T

    outs, couts = _call(
        body, name=name, grid=(m // TM,), in_specs=[row, vec, mod],
        out_specs=[row, pl.BlockSpec((d, TM), lambda t: (0, t))],
        out_shape=[jax.ShapeDtypeStruct((m, d), BF16), jax.ShapeDtypeStruct((d, m), BF16)],
        args=[h, g, modsel], sem=("parallel",), comm=comm)
    return outs if comm is None else (outs, couts)


def _acc_rows(t, nt, dvec_ref, rows):
    first = (t % nt) <= 1

    @pl.when(first)
    def _():
        dvec_ref[...] = rows

    @pl.when(jnp.logical_not(first))
    def _():
        dvec_ref[...] += rows


def _norm_mod_bwd(h, g, modsel, du, dh_in, i_sh, i_sc, nt, name, comm=None, latent_only=False):
    m, d = h.shape
    row, vec, mod = _row_specs(d, nt)
    dh_rows, dh_spec = m, row
    if latent_only:
        dh_rows = m // nt * (nt - 1)
        dh_spec = pl.BlockSpec((TM, d), lambda t: ((t // nt) * (nt - 1) + jnp.maximum(t % nt, 1) - 1, 0))

    def body(h_ref, g_ref, ms_ref, du_ref, dhi_ref, dh_ref, dvec_ref):
        t = pl.program_id(0)
        r, xh = _rms(h_ref[...])
        g_ = g_ref[...]
        ms = ms_ref[...]
        du_ = du_ref[...]
        y = xh * g_
        dy = du_ * (1.0 + ms[i_sc:i_sc + 1])
        dxh = dy * g_
        dx = r * (dxh - xh * jnp.mean(dxh * xh, axis=1, keepdims=True))
        dh_ref[...] = dhi_ref[...] + dx
        rows = jnp.concatenate([
            jnp.sum(du_, axis=0, keepdims=True), jnp.sum(du_ * y, axis=0, keepdims=True),
            jnp.sum(dy * xh, axis=0, keepdims=True), jnp.zeros((MOD_ROWS - 3, d), F32)], axis=0)
        _acc_rows(t, nt, dvec_ref, rows)

    outs, couts = _call(
        body, name=name, grid=(m // TM,), in_specs=[row, vec, mod, row, row], out_specs=[dh_spec, mod],
        out_shape=[jax.ShapeDtypeStruct((dh_rows, d), F32), jax.ShapeDtypeStruct(modsel.shape, F32)],
        args=[h, g, modsel, du, dh_in], sem=("arbitrary",), comm=comm)
    return outs if comm is None else (outs, couts)


def _gate_rows(dh_, xh, g_, gate):
    dy = dh_ * gate
    return dy * g_, jnp.sum(dh_ * (xh * g_), axis=0, keepdims=True), jnp.sum(dy * xh, axis=0, keepdims=True)


def _gate_mod_fwd(h, z, g_post, ms_gate, i_g, g_pre, ms_mod, i_sh, i_sc, nt, name):
    m, d = h.shape
    row, vec, mod = _row_specs(d, nt)

    def body(h_ref, z_ref, gp_ref, msg_ref, gq_ref, msm_ref, hn_ref, u_ref, ut_ref):
        _, zh = _rms(z_ref[...])
        hn = h_ref[...] + msg_ref[...][i_g:i_g + 1] * (zh * gp_ref[...])
        hn_ref[...] = hn
        _, xh = _rms(hn)
        ms = msm_ref[...]
        u = (xh * gq_ref[...] * (1.0 + ms[i_sc:i_sc + 1]) + ms[i_sh:i_sh + 1]).astype(BF16)
        u_ref[...] = u
        ut_ref[...] = u.T

    return pl.pallas_call(
        body, name=name, grid=(m // TM,), in_specs=[row, row, vec, mod, vec, mod],
        out_specs=[row, row, pl.BlockSpec((d, TM), lambda t: (0, t))],
        out_shape=[jax.ShapeDtypeStruct((m, d), F32), jax.ShapeDtypeStruct((m, d), BF16),
                   jax.ShapeDtypeStruct((d, m), BF16)],
        compiler_params=_params(("parallel",)),
    )(h, z, g_post, ms_gate, g_pre, ms_mod)


def _mod_gate_bwd(h, g_pre, ms_mod, i_sh, i_sc, du, dh_in, z, g_post, ms_gate, i_g, nt, name):
    m, d = h.shape
    row, vec, mod = _row_specs(d, nt)

    def body(h_ref, gq_ref, msm_ref, du_ref, dhi_ref, z_ref, gp_ref, msg_ref, dh_ref, dvm_ref, dz_ref, dvg_ref):
        t = pl.program_id(0)
        r, xh = _rms(h_ref[...])
        gq = gq_ref[...]
        ms = msm_ref[...]
        du_ = du_ref[...]
        dy = du_ * (1.0 + ms[i_sc:i_sc + 1])
        dxh = dy * gq
        dh_ = dhi_ref[...] + r * (dxh - xh * jnp.mean(dxh * xh, axis=1, keepdims=True))
        dh_ref[...] = dh_
        _acc_rows(t, nt, dvm_ref, jnp.concatenate([
            jnp.sum(du_, axis=0, keepdims=True), jnp.sum(du_ * (xh * gq), axis=0, keepdims=True),
            jnp.sum(dy * xh, axis=0, keepdims=True), jnp.zeros((MOD_ROWS - 3, d), F32)], axis=0))
        rz, zh = _rms(z_ref[...])
        dzh, d_gate, d_gp = _gate_rows(dh_, zh, gp_ref[...], msg_ref[...][i_g:i_g + 1])
        dz_ref[...] = (rz * (dzh - zh * jnp.mean(dzh * zh, axis=1, keepdims=True))).astype(BF16)
        _acc_rows(t, nt, dvg_ref, jnp.concatenate([d_gate, d_gp, jnp.zeros((MOD_ROWS - 2, d), F32)], axis=0))

    return pl.pallas_call(
        body, name=name, grid=(m // TM,), in_specs=[row, vec, mod, row, row, row, vec, mod],
        out_specs=[row, mod, row, mod],
        out_shape=[jax.ShapeDtypeStruct((m, d), F32), jax.ShapeDtypeStruct(ms_mod.shape, F32),
                   jax.ShapeDtypeStruct((m, d), BF16), jax.ShapeDtypeStruct(ms_gate.shape, F32)],
        compiler_params=_params(("arbitrary",)),
    )(h, g_pre, ms_mod, du, dh_in, z, g_post, ms_gate)


def _gate_loss_bwd(h, z, g_post, modsel, i_g, target, nt, name):
    m, d = h.shape
    row, vec, mod = _row_specs(d, nt)
    ntl = nt - 1
    tgt = pl.BlockSpec((TM, d), lambda t: ((t // nt) * ntl + jnp.maximum(t % nt, 1) - 1, 0))
    acc = pl.BlockSpec((8, 128), lambda t: (0, 0))

    def body(h_ref, z_ref, gp_ref, ms_ref, t_ref, dh_ref, dz_ref, dvg_ref, ss_ref):
        t = pl.program_id(0)

        @pl.when(t == 0)
        def _():
            ss_ref[...] = jnp.zeros_like(ss_ref)

        latent = (t % nt != 0).astype(F32)
        rz, zh = _rms(z_ref[...])
        gp = gp_ref[...]
        gate = ms_ref[...][i_g:i_g + 1]
        e = h_ref[...] + gate * (zh * gp) - t_ref[...]
        ss_ref[...] += latent * jnp.sum(e * e)
        dh_ = e * (latent / d)
        dh_ref[...] = dh_
        dzh, d_gate, d_gp = _gate_rows(dh_, zh, gp, gate)
        dz_ref[...] = (rz * (dzh - zh * jnp.mean(dzh * zh, axis=1, keepdims=True))).astype(BF16)
        _acc_rows(t, nt, dvg_ref, jnp.concatenate([d_gate, d_gp, jnp.zeros((MOD_ROWS - 2, d), F32)], axis=0))

    return pl.pallas_call(
        body, name=name, grid=(m // TM,), in_specs=[row, row, vec, mod, tgt], out_specs=[row, row, mod, acc],
        out_shape=[jax.ShapeDtypeStruct((m, d), F32), jax.ShapeDtypeStruct((m, d), BF16),
                   jax.ShapeDtypeStruct(modsel.shape, F32), jax.ShapeDtypeStruct((8, 128), F32)],
        compiler_params=_params(("arbitrary",)),
    )(h, z, g_post, modsel, target)


QA, KA, VA, QB, KB, VB = 0, 512, 640, 768, 1280, 1408
PROJ_W = 1536
Q_SCALE = HEAD_DIM ** -0.5
LOG2E = 1.4426950408889634


def _swap16(x):
    lane = lax.broadcasted_iota(jnp.int32, x.shape, 1)
    n = x.shape[1]
    return jnp.where((lane % 32) < 16, pltpu.roll(x, n - 16, 1), pltpu.roll(x, 16, 1))


def _seg_mean(x, e):
    hi = x.astype(BF16)
    lo = (x - hi.astype(F32)).astype(BF16)
    w = e.shape[0]
    both = lambda a: jnp.dot(hi[:, a:a + w], e, preferred_element_type=F32) + jnp.dot(lo[:, a:a + w], e, preferred_element_type=F32)
    parts = [both(a) for a in range(0, x.shape[1], w)]
    return parts[0] if len(parts) == 1 else jnp.concatenate(parts, axis=1)


def _rope_tables(t_rows, c_rows):
    s = t_rows - c_rows
    row_ids = jnp.repeat(jnp.arange(s // GRID_W, dtype=jnp.int32), GRID_W).astype(F32)
    col_ids = jnp.tile(jnp.arange(GRID_W, dtype=jnp.int32), s // GRID_W).astype(F32)
    axis_dim = HEAD_DIM // 2
    inv = ROPE_THETA ** (-jnp.arange(0, axis_dim, 2, dtype=F32) / axis_dim)
    ang_r = row_ids[:, None] * inv[None, :]
    ang_c = col_ids[:, None] * inv[None, :]
    cos = jnp.concatenate([jnp.cos(ang_r), jnp.cos(ang_r), jnp.cos(ang_c), jnp.cos(ang_c)], axis=1)
    sin = jnp.concatenate([-jnp.sin(ang_r), jnp.sin(ang_r), -jnp.sin(ang_c), jnp.sin(ang_c)], axis=1)
    cos = jnp.concatenate([jnp.ones((c_rows, HEAD_DIM), F32), cos], axis=0)
    sin = jnp.concatenate([jnp.zeros((c_rows, HEAD_DIM), F32), sin], axis=0)
    return jnp.tile(cos, (1, 8)), jnp.tile(sin, (1, 8))


def _head_mean_matrix():
    i = np.arange(512)
    return jnp.asarray((i[:, None] // HEAD_DIM == i[None, :] // HEAD_DIM).astype(np.float32) / HEAD_DIM, dtype=BF16)


def _interleave_kv(k, v):
    return jnp.concatenate([k[:, :64], v[:, :64], k[:, 64:], v[:, 64:]], axis=1)


def _prep_fwd(proj, qn, kn, cos, sin, emat, nt, name):
    m = proj.shape[0]
    specs = [
        pl.BlockSpec((TM, PROJ_W), lambda t: (t, 0)),
        pl.BlockSpec((1, 512), lambda t: (0, 0)), pl.BlockSpec((1, 128), lambda t: (0, 0)),
        pl.BlockSpec((TM, 512), lambda t: (t % nt, 0)), pl.BlockSpec((TM, 512), lambda t: (t % nt, 0)),
        pl.BlockSpec((512, 512), lambda t: (0, 0)),
    ]

    def body(p_ref, qn_ref, kn_ref, cos_ref, sin_ref, e_ref, q_ref, kv_ref):
        cos_, sin_, e = cos_ref[...], sin_ref[...], e_ref[...]

        def rope(x, w):
            return x * cos_[:, :w] + _swap16(x) * sin_[:, :w]

        def norm(x, g, w):
            return x * lax.rsqrt(_seg_mean(x * x, e[:min(w, 256), :min(w, 256)]) + EPS) * g

        qa = rope(norm(p_ref[:, QA:QA + 512], qn_ref[...], 512), 512)
        qb = rope(p_ref[:, QB:QB + 512], 512)
        q_ref[:, 0:512] = (qa * (Q_SCALE * LOG2E)).astype(BF16)
        q_ref[:, 512:1024] = (qb * (Q_SCALE * LOG2E)).astype(BF16)
        ka = rope(norm(p_ref[:, KA:KA + 128], kn_ref[...], 128), 128)
        kb = rope(p_ref[:, KB:KB + 128], 128)
        kv_ref[:, 0:256] = _interleave_kv(ka, p_ref[:, VA:VA + 128]).astype(BF16)
        kv_ref[:, 256:512] = _interleave_kv(kb, p_ref[:, VB:VB + 128]).astype(BF16)

    return pl.pallas_call(
        body, name=name, grid=(m // TM,), in_specs=specs,
        out_specs=[pl.BlockSpec((TM, 1024), lambda t: (t, 0)), pl.BlockSpec((TM, 512), lambda t: (t, 0))],
        out_shape=[jax.ShapeDtypeStruct((m, 1024), BF16), jax.ShapeDtypeStruct((m, 512), BF16)],
        compiler_params=_params(("parallel",)),
    )(proj, qn, kn, cos, sin, emat)


def _prep_bwd(proj, dq, dkv, qn, kn, cos, sin, emat, nt, name, comm=None):
    m = proj.shape[0]
    specs = [
        pl.BlockSpec((TM, PROJ_W), lambda t: (t, 0)),
        pl.BlockSpec((TM, 1024), lambda t: (t, 0)), pl.BlockSpec((TM, 512), lambda t: (t, 0)),
        pl.BlockSpec((1, 512), lambda t: (0, 0)), pl.BlockSpec((1, 128), lambda t: (0, 0)),
        pl.BlockSpec((TM, 512), lambda t: (t % nt, 0)), pl.BlockSpec((TM, 512), lambda t: (t % nt, 0)),
        pl.BlockSpec((512, 512), lambda t: (0, 0)),
    ]

    def body(p_ref, dq_ref, dkv_ref, qn_ref, kn_ref, cos_ref, sin_ref, e_ref, dp_ref, dqn_ref, dkn_ref):
        t = pl.program_id(0)
        cos_, sin_, e = cos_ref[...], sin_ref[...], e_ref[...]

        @pl.when(t == 0)
        def _():
            dqn_ref[...] = jnp.zeros_like(dqn_ref)
            dkn_ref[...] = jnp.zeros_like(dkn_ref)

        def unrope(dy, w):
            return dy * cos_[:, :w] + _swap16(dy * sin_[:, :w])

        def norm_bwd(x, g, dy, w):
            r = lax.rsqrt(_seg_mean(x * x, e[:min(w, 256), :min(w, 256)]) + EPS)
            xh = x * r
            dxh = dy * g
            dx = r * (dxh - xh * _seg_mean(dxh * xh, e[:min(w, 256), :min(w, 256)]))
            return dx, jnp.sum(dy * xh, axis=0, keepdims=True)

        dqa, dgq = norm_bwd(p_ref[:, QA:QA + 512], qn_ref[...], unrope(dq_ref[:, 0:512] * Q_SCALE, 512), 512)
        dp_ref[:, QA:QA + 512] = dqa.astype(BF16)
        dp_ref[:, QB:QB + 512] = unrope(dq_ref[:, 512:1024] * Q_SCALE, 512).astype(BF16)
        da = dkv_ref[:, 0:256]
        db = dkv_ref[:, 256:512]
        dka = jnp.concatenate([da[:, 0:64], da[:, 128:192]], axis=1)
        dva = jnp.concatenate([da[:, 64:128], da[:, 192:256]], axis=1)
        dkb = jnp.concatenate([db[:, 0:64], db[:, 128:192]], axis=1)
        dvb = jnp.concatenate([db[:, 64:128], db[:, 192:256]], axis=1)
        dka, dgk = norm_bwd(p_ref[:, KA:KA + 128], kn_ref[...], unrope(dka, 128), 128)
        dp_ref[:, KA:KA + 128] = dka.astype(BF16)
        dp_ref[:, VA:VA + 128] = dva.astype(BF16)
        dp_ref[:, KB:KB + 128] = unrope(dkb, 128).astype(BF16)
        dp_ref[:, VB:VB + 128] = dvb.astype(BF16)
        dqn_ref[0:1, :] += dgq
        dkn_ref[0:1, :] += dgk

    outs, couts = _call(
        body, name=name, grid=(m // TM,), in_specs=specs,
        out_specs=[pl.BlockSpec((TM, PROJ_W), lambda t: (t, 0)), pl.BlockSpec((8, 512), lambda t: (0, 0)),
                   pl.BlockSpec((8, 128), lambda t: (0, 0))],
        out_shape=[jax.ShapeDtypeStruct((m, PROJ_W), BF16), jax.ShapeDtypeStruct((8, 512), F32),
                   jax.ShapeDtypeStruct((8, 128), F32)],
        args=[proj, dq, dkv, qn, kn, cos, sin, emat], sem=("arbitrary",), comm=comm)
    return outs if comm is None else (outs, couts)


def _attn_case(t, hg, kv_ref, c_rows, t_rows, fn, keys_first=False):
    wl = TM + 2 * WINDOW
    kvd = lambda a, n: kv_ref[pl.ds(a, n), :]
    dense = hg < 2
    ctx = t == 0

    @pl.when(jnp.logical_and(dense, ctx))
    def _():
        kv = kvd(0, c_rows)
        fn(kv[:, :64], kv[:, 64:], None, False, [(0, c_rows)])

    @pl.when(jnp.logical_and(dense, jnp.logical_not(ctx)))
    def _():
        kv = kvd(0, t_rows)
        fn(kv[:, :64], kv[:, 64:], None, False, [(0, t_rows)])

    @pl.when(jnp.logical_and(jnp.logical_not(dense), ctx))
    def _():
        kv = kvd(0, c_rows)
        fn(kv[:, :64], kv[:, 64:], None, True, [(0, c_rows)])

    @pl.when(jnp.logical_and(jnp.logical_not(dense), jnp.logical_not(ctx)))
    def _():
        start = pl.multiple_of(jnp.minimum(c_rows + (t - 1) * TM - WINDOW, t_rows - wl), 128)
        kv = jnp.concatenate([kvd(0, c_rows), kvd(start, wl)], axis=0)
        shape = (c_rows + wl, TM) if keys_first else (TM, c_rows + wl)
        q_i = lax.broadcasted_iota(jnp.int32, shape, 1 if keys_first else 0)
        k_i = lax.broadcasted_iota(jnp.int32, shape, 0 if keys_first else 1)
        qpos = (t - 1) * TM + q_i
        kpos = start - 2 * c_rows + k_i
        mask = jnp.logical_or(k_i < c_rows, jnp.logical_and(jnp.abs(kpos - qpos) <= WINDOW, kpos >= 0))
        fn(kv[:, :64], kv[:, 64:], mask, True, [(0, c_rows), (start, wl)])


def _head_columns(cols):
    lane = lax.broadcasted_iota(jnp.int32, (TM, 128), 1)
    out = jnp.zeros((TM, 128), F32)
    for g, col in enumerate(cols):
        out = jnp.where(lane == g, col, out)
    return out


def _attn_specs(t_rows):
    nt = t_rows // TM
    q_spec = pl.BlockSpec((None, TM, Q_WIDTH), lambda b, hg, t, s: (b, t, hg))
    kv_spec = pl.BlockSpec((None, t_rows, 128), lambda b, hg, t, s: (b, 0, hg))
    lse_spec = pl.BlockSpec((None, TM, 128), lambda b, hg, t, s: (hg, b * nt + t, 0))
    return q_spec, kv_spec, lse_spec


def _attn_fwd(q_all, kv_all, sink8, c_rows, name, comm=None):
    bl, t_rows, _ = q_all.shape
    q_spec, kv_spec, lse_spec = _attn_specs(t_rows)

    def body(sink_ref, q_ref, kv_ref, o_ref, ot_ref, lse_ref):
        hg, t = pl.program_id(1), pl.program_id(2)

        def fn(k, v, mask, use_sink, spans):
            outs, lses = [], []
            v_one = jnp.concatenate([v, jnp.ones(v.shape, BF16)], axis=1)
            def scores(g):
                s = lax.dot_general(q_ref[:, g * 64:(g + 1) * 64], k, (((1,), (1,)), ((), ())),
                                    preferred_element_type=F32)
                return s if mask is None else jnp.where(mask, s, NEG_BIG)

            s_next = scores(0)
            for g in range(GROUP):
                s = s_next
                if g + 1 < GROUP:
                    s_next = scores(g + 1)
                mx = jnp.max(s, axis=1, keepdims=True)
                if use_sink:
                    sink = sink_ref[jnp.maximum(hg - 2, 0) * GROUP + g] * LOG2E
                    mx = jnp.maximum(mx, sink)
                pv = jnp.dot(jnp.exp2(s - mx).astype(BF16), v_one, preferred_element_type=F32)
                l = pv[:, 64:65]
                if use_sink:
                    l = l + jnp.exp2(sink - mx)
                outs.append(pv[:, :64] * (1.0 / l))
                lses.append(mx + jnp.log2(l))
            o = jnp.concatenate(outs, axis=1).astype(BF16)
            o_ref[...] = o
            ot_ref[...] = o.T
            lse_ref[...] = _head_columns(lses)

        _attn_case(t, hg, kv_ref, c_rows, t_rows, fn)

    nt = t_rows // TM
    ot_spec = pl.BlockSpec((Q_WIDTH, TM), lambda b, hg, t, s: (hg, b * nt + t))
    outs, couts = _call(
        body, name=name, grid=(bl, N_HG, nt), in_specs=[q_spec, kv_spec], out_specs=[q_spec, ot_spec, lse_spec],
        out_shape=[jax.ShapeDtypeStruct(q_all.shape, BF16), jax.ShapeDtypeStruct((N_HG * Q_WIDTH, bl * t_rows), BF16),
                   jax.ShapeDtypeStruct((N_HG, bl * t_rows, 128), F32)],
        args=[sink8, q_all, kv_all], prefetch=1, sem=("parallel", "parallel", "arbitrary"), comm=comm)
    return outs if comm is None else (outs, couts)


def _attn_bwd(q_all, kv_all, do, o, lse, sink8, c_rows, name, comm=None):
    bl, t_rows, _ = q_all.shape
    q_spec, kv_spec, lse_spec = _attn_specs(t_rows)
    ds_spec = pl.BlockSpec((None, None, 8, 128), lambda b, hg, t, s: (b, hg, 0, 0))

    def body(sink_ref, q_ref, kv_ref, do_ref, o_ref, lse_ref, dq_ref, dkv_ref, dsk_ref):
        hg, t = pl.program_id(1), pl.program_id(2)

        @pl.when(t == 0)
        def _():
            dkv_ref[...] = jnp.zeros_like(dkv_ref)
            dsk_ref[...] = jnp.zeros_like(dsk_ref)

        lse_rows = lse_ref[...].T
        dd_cols = [jnp.sum(do_ref[:, g * 64:(g + 1) * 64].astype(F32) * o_ref[:, g * 64:(g + 1) * 64].astype(F32),
                           axis=1, keepdims=True) for g in range(GROUP)]
        dd_rows = _head_columns(dd_cols).T

        def fn(k, v, mask, use_sink, spans):
            k_t = k.T
            dq_t, dsinks = [], []
            dk = jnp.zeros(k.shape, F32)
            dv = jnp.zeros(v.shape, F32)
            def products(g):
                s = lax.dot_general(k, q_ref[:, g * 64:(g + 1) * 64], (((1,), (1,)), ((), ())),
                                    preferred_element_type=F32)
                dp = lax.dot_general(v, do_ref[:, g * 64:(g + 1) * 64], (((1,), (1,)), ((), ())),
                                     preferred_element_type=F32)
                return (s if mask is None else jnp.where(mask, s, NEG_BIG)), dp

            nxt = products(0)
            for g in range(GROUP):
                q = q_ref[:, g * 64:(g + 1) * 64]
                do_g = do_ref[:, g * 64:(g + 1) * 64]
                lse_g, dd_g = lse_rows[g:g + 1, :], dd_rows[g:g + 1, :]
                s, dp = nxt
                if g + 1 < GROUP:
                    nxt = products(g + 1)
                pb = jnp.exp2(s - lse_g).astype(BF16)
                ds = (pb.astype(F32) * (dp - dd_g)).astype(BF16)
                dk = dk + jnp.dot(ds, q, preferred_element_type=F32)
                dv = dv + jnp.dot(pb, do_g, preferred_element_type=F32)
                dq_t.append(jnp.dot(k_t, ds, preferred_element_type=F32))
                if use_sink:
                    p_sink = jnp.exp2(sink_ref[jnp.maximum(hg - 2, 0) * GROUP + g] * LOG2E - lse_g)
                    dsinks.append(jnp.broadcast_to(-jnp.sum(p_sink * dd_g, axis=1, keepdims=True), (1, 128)))
            dq_ref[...] = jnp.concatenate(dq_t, axis=0).T
            dkv = jnp.concatenate([dk * (1.0 / LOG2E), dv], axis=1)
            off = 0
            for start, size in spans:
                dkv_ref[pl.ds(start, size), :] += dkv[off:off + size]
                off += size
            if use_sink:
                dsk_ref[0:GROUP, :] += jnp.concatenate(dsinks, axis=0)

        _attn_case(t, hg, kv_ref, c_rows, t_rows, fn, keys_first=True)

    outs, couts = _call(
        body, name=name, grid=(bl, N_HG, t_rows // TM), in_specs=[q_spec, kv_spec, q_spec, q_spec, lse_spec],
        out_specs=[q_spec, kv_spec, ds_spec],
        out_shape=[jax.ShapeDtypeStruct(q_all.shape, F32), jax.ShapeDtypeStruct(kv_all.shape, F32),
                   jax.ShapeDtypeStruct((bl, N_HG, 8, 128), F32)],
        args=[sink8, q_all, kv_all, do, o, lse], prefetch=1, sem=("parallel", "parallel", "arbitrary"), comm=comm)
    return outs if comm is None else (outs, couts)


def _silu(x):
    return x * jax.nn.sigmoid(x)


def _ada_fwd(c_rows, w_ada, b_cols, name, comm=None):
    nl, d, w = w_ada.shape
    r = c_rows.shape[0]

    def body(c_ref, w_ref, b_ref, o_ref):
        s = _silu(c_ref[...]).astype(BF16)
        o_ref[...] = jnp.dot(s, w_ref[...].astype(BF16), preferred_element_type=F32) + b_ref[...]

    outs, couts = _call(
        body, name=name, grid=(nl,),
        in_specs=[pl.BlockSpec((r, d), lambda l: (0, 0)), pl.BlockSpec((None, d, w), lambda l: (l, 0, 0)),
                  pl.BlockSpec((None, 1, w), lambda l: (l, 0, 0))],
        out_specs=pl.BlockSpec((None, r, w), lambda l: (l, 0, 0)),
        out_shape=jax.ShapeDtypeStruct((nl, r, w), F32), args=[c_rows, w_ada, b_cols], sem=("parallel",), comm=comm)
    return outs if comm is None else (outs, couts)


def _ada_bwd(c_rows, c_ctx, dmod, w_ada, name):
    nl, d, w = w_ada.shape
    r = c_rows.shape[0]

    def body(c_ref, cc_ref, g_ref, w_ref, dw_ref, dc_ref):
        l = pl.program_id(0)
        s = _silu(c_ref[...]).astype(BF16)
        gm = g_ref[...].astype(BF16)
        dw_ref[...] = lax.dot_general(s, gm, (((0,), (0,)), ((), ())), preferred_element_type=F32)
        ds = lax.dot_general(gm, w_ref[...].astype(BF16), (((1,), (1,)), ((), ())), preferred_element_type=F32)
        rows = lax.broadcasted_iota(jnp.int32, ds.shape, 0)
        dsc = jnp.sum(jnp.where(rows % ADA_ROWS == 2, ds, 0.0), axis=0, keepdims=True)
        x = cc_ref[...]
        sg = jax.nn.sigmoid(x)
        dcc = dsc * (sg * (1.0 + x * (1.0 - sg)))
        out = jnp.concatenate([dcc, jnp.zeros((7, d), F32)], axis=0)

        @pl.when(l == 0)
        def _():
            dc_ref[...] = out

        @pl.when(l != 0)
        def _():
            dc_ref[...] += out

    return pl.pallas_call(
        body, name=name, grid=(nl,),
        in_specs=[pl.BlockSpec((r, d), lambda l: (0, 0)), pl.BlockSpec((1, d), lambda l: (0, 0)),
                  pl.BlockSpec((None, r, w), lambda l: (l, 0, 0)), pl.BlockSpec((None, d, w), lambda l: (l, 0, 0))],
        out_specs=[pl.BlockSpec((None, d, w), lambda l: (l, 0, 0)), pl.BlockSpec((8, d), lambda l: (0, 0))],
        out_shape=[jax.ShapeDtypeStruct((nl, d, w), F32), jax.ShapeDtypeStruct((8, d), F32)],
        compiler_params=_params(("arbitrary",)),
    )(c_rows, c_ctx, dmod, w_ada)


def _adam_math(w, g, m, v):
    m = ADAM_B1 * m + (1.0 - ADAM_B1) * g
    v = ADAM_B2 * v + (1.0 - ADAM_B2) * (g * g)
    m_hat = m / (1.0 - ADAM_B1 ** ADAM_STEP)
    v_hat = v / (1.0 - ADAM_B2 ** ADAM_STEP)
    delta = -ADAM_LR * (m_hat / (jnp.sqrt(v_hat) + ADAM_EPS) + ADAM_WD * w)
    return delta, m, v


def _adamw(w, m, v, g_own, g_recv, name, rows=256):
    nl, r, c = w.shape
    tr = min(rows, r)
    spec = pl.BlockSpec((None, tr, c), lambda l, i: (l, i, 0))
    per_layer = isinstance(g_own, (list, tuple))
    own = list(g_own) if per_layer else [g_own]
    recv = [] if g_recv is None else list(g_recv)
    n_i = r // tr

    def rows_of(li):
        return lambda l, i: jnp.where(l == li, i, jnp.where(l > li, n_i - 1, 0))

    in_specs = [spec] * 3
    in_specs += [pl.BlockSpec((tr, c), lambda l, i, f=rows_of(li): (f(l, i), 0)) if per_layer else spec
                 for li in range(len(own))]
    in_specs += [pl.BlockSpec((N_DEV, tr, c), lambda l, i, f=rows_of(li): (0, f(l, i), 0)) for li in range(len(recv))]

    def body(*refs):
        w_ref, m_ref, v_ref = refs[:3]
        own_refs, recv_refs = refs[3:3 + len(own)], refs[3 + len(own):3 + len(own) + len(recv)]
        go_ref, d_ref, mo_ref, vo_ref = refs[-4:]

        def update(li):
            g = own_refs[li][...].astype(F32)
            if recv:
                for k in range(1, N_DEV):
                    g = g + recv_refs[li][k].astype(F32)
            delta, m_, v_ = _adam_math(w_ref[...], g, m_ref[...], v_ref[...])
            go_ref[...] = g
            d_ref[...] = delta
            mo_ref[...] = m_
            vo_ref[...] = v_

        if per_layer:
            for li in range(nl):
                pl.when(pl.program_id(0) == li)(functools.partial(update, li))
        else:
            update(0)

    return pl.pallas_call(
        body, name=name, grid=(nl, n_i), in_specs=in_specs, out_specs=[spec] * 4,
        out_shape=[jax.ShapeDtypeStruct(w.shape, F32)] * 4, compiler_params=_params(("parallel", "parallel")),
    )(w, m, v, *own, *recv)


def _small_adamw(w, m, v, g_all, name):
    def body(w_ref, m_ref, v_ref, g_ref, go_ref, d_ref, mo_ref, vo_ref):
        g = g_ref[0]
        for k in range(1, N_DEV):
            g = g + g_ref[k]
        delta, m_, v_ = _adam_math(w_ref[...], g, m_ref[...], v_ref[...])
        go_ref[...] = g
        d_ref[...] = delta
        mo_ref[...] = m_
        vo_ref[...] = v_

    return pl.pallas_call(
        body, name=name, out_shape=[jax.ShapeDtypeStruct(w.shape, F32)] * 4, compiler_params=_params(),
    )(w, m, v, g_all)


SMALL = ("c_ctx", "b_ada", "g_pre_mix", "g_post_mix", "g_pre_mlp", "g_post_mlp", "q_norm", "k_norm", "sink", "loss")


def _pack_small(parts):
    flat = jnp.concatenate([parts[n].reshape(-1) for n in SMALL])
    rows = -(-flat.shape[0] // 1024) * 8
    return jnp.pad(flat, (0, rows * 128 - flat.shape[0])).reshape(rows, 128)


def _unpack_small(packed, like):
    flat = packed.reshape(-1)
    out, off = {}, 0
    for n in SMALL:
        size = int(np.prod(like[n].shape))
        out[n] = flat[off:off + size].reshape(like[n].shape)
        off += size
    return out


def kernel(x, c, ctx, c_ctx, w_ada, b_ada, g_pre_mix, g_post_mix, g_pre_mlp, g_post_mlp, w_in, q_norm, k_norm, sink, w_out, w_up, w_down, loss_target, m_c_ctx, m_w_ada, m_b_ada, m_g_pre_mix, m_g_post_mix, m_g_pre_mlp, m_g_post_mlp, m_w_in, m_q_norm, m_k_norm, m_sink, m_w_out, m_w_up, m_w_down, v_c_ctx, v_w_ada, v_b_ada, v_g_pre_mix, v_g_post_mix, v_g_pre_mlp, v_g_post_mlp, v_w_in, v_q_norm, v_k_norm, v_sink, v_w_out, v_w_up, v_w_down):
    bl, s_rows, d = x.shape
    c_rows = ctx.shape[1]
    assert c_rows == TM and s_rows % TM == 0 and bl == 2
    t_rows = c_rows + s_rows
    nt = t_rows // TM
    m_rows = bl * t_rows
    nl = w_in.shape[0]
    ada_w = w_ada.shape[2]
    d_ff = w_up.shape[2] * N_DEV
    me = _my_index()

    shard = lambda w_, l: w_[l].astype(BF16)
    c_pad = jnp.concatenate([c, c_ctx[None, :], jnp.zeros((ADA_ROWS - bl - 1, d), F32)], axis=0)
    gathered = {0: {}}
    h, (c_all, gathered[0]["w_in"]) = _token_stream(
        ctx, x, nt, "token_stream", comm=_Comm([(c_pad, GATHER), (shard(w_in, 0), GATHER)]))
    c_all = c_all.reshape(N_DEV * ADA_ROWS, d)

    def layer_weights(l):
        g_ = gathered[l]
        w_out_f = g_["w_out"].reshape(-1, d)
        w_down_f = g_["w_down"].reshape(d_ff, d)
        return dict(
            w_in_t=g_["w_in"].transpose(0, 2, 1).reshape(PROJ_W, d), w_out_f=w_out_f, w_out_t=w_out_f.T,
            w_up_s=g_["w_up"], w_up_t=g_["w_up"].transpose(0, 2, 1).reshape(d_ff, d),
            w_down_f=w_down_f, w_down_t=w_down_f.T)

    big = dict(tm=2304, tn=512)
    deep = dict(tm=1536, tn=512, tk=d_ff)
    wide = dict(tm=1024, tn=512, tk=m_rows)

    b_cols = lax.dynamic_slice(b_ada, (0, me * ada_w), (nl, ada_w))[:, None, :]
    mod_cols = _ada_fwd(c_all, w_ada, b_cols, "ada_fwd")
    mod_slots = mod_cols.reshape(nl, N_DEV, ADA_ROWS, ada_w).transpose(1, 0, 2, 3)
    mod_g, = _comm_only(_Comm([(mod_slots, TO_OWNER)]), "exchange_mod")
    mine = mod_g.transpose(1, 2, 0, 3).reshape(nl, ADA_ROWS, N_MOD, d)
    pad = jnp.zeros((bl, 2, MOD_ROWS - N_MOD, d), F32)
    modsel = [jnp.concatenate([jnp.stack([jnp.broadcast_to(mine[l, bl], (bl, N_MOD, d)), mine[l, :bl]], axis=1), pad],
                              axis=2) for l in range(nl)]

    cos, sin = _rope_tables(t_rows, c_rows)
    emat = _head_mean_matrix()
    row = lambda a: a[None, :]
    qn = [jnp.tile(q_norm[l], 8)[None, :] for l in range(nl)]
    kn = [jnp.tile(k_norm[l], 2)[None, :] for l in range(nl)]

    target = loss_target.reshape(bl * s_rows, d)
    saved = []
    weights_of = {}
    (u, u_t), (gathered[0]["w_out"],) = _norm_mod_fwd(
        h, row(g_pre_mix[0]), modsel[0], 0, 1, nt, "mix_mod_fwd0", comm=_Comm([(shard(w_out, 0), GATHER)]))
    for l in range(nl):
        w_in_f = gathered[l]["w_in"].transpose(1, 0, 2).reshape(d, PROJ_W)
        proj = _mm(u, w_in_f, name=f"mm_in{l}", tk=d, **big)
        q_all, kv_all = _prep_fwd(proj, qn[l], kn[l], cos, sin, emat, nt, f"prep_fwd{l}")
        (o, o_t, lse), (w_up_g, w_down_g) = _attn_fwd(
            q_all.reshape(bl, t_rows, 1024), kv_all.reshape(bl, t_rows, 512), sink[l], c_rows, f"attn_fwd{l}",
            comm=_Comm([(shard(w_up, l), GATHER), (shard(w_down, l), GATHER)]))
        o = o.reshape(m_rows, 1024)
        gathered[l].update(w_up=w_up_g, w_down=w_down_g)
        wl = weights_of[l] = layer_weights(l)
        mix = _mm(o, wl["w_out_f"], name=f"mm_out{l}", tk=1024, **big)
        h_mid, v_in, v_t = _gate_mod_fwd(h, mix, row(g_post_mix[l]), modsel[l], 2, row(g_pre_mlp[l]), modsel[l], 3, 4,
                                         nt, f"mix_gate_mlp_mod_fwd{l}")
        more = l + 1 < nl
        res_up = _mm(v_in, wl["w_up_s"], name=f"mm_up{l}", b_mode="nn_slots", epilogue="relu2", tk=d,
                     comm=_Comm([(shard(w_in, l + 1), GATHER)]) if more else None, **big)
        (r_act, r_t), nxt_in = res_up if more else (res_up, None)
        res_down = _mm(r_act, wl["w_down_f"], name=f"mm_down{l}",
                       comm=_Comm([(shard(w_out, l + 1), GATHER)]) if more else None, **deep)
        y, nxt_out = res_down if more else (res_down, None)
        saved.append((h, u_t, proj, q_all, kv_all, o, o_t, lse, mix, h_mid, v_t, r_act, r_t, y))
        if more:
            gathered[l + 1] = dict(w_in=nxt_in[0], w_out=nxt_out[0])
            h, u, u_t = _gate_mod_fwd(h_mid, y, row(g_post_mlp[l]), modsel[l], 5, row(g_pre_mix[l + 1]), modsel[l + 1],
                                      0, 1, nt, f"mlp_gate_mix_mod_fwd{l}")

    dh, dy, dvec_g2, ss = _gate_loss_bwd(h_mid, y, row(g_post_mlp[nl - 1]), modsel[nl - 1], 5, target, nt, "gate_loss_bwd")
    small_g = {n: [None] * nl for n in SMALL if n not in ("c_ctx", "b_ada", "loss")}
    dvecs = {l: {} for l in range(nl)}
    dvecs[nl - 1]["g2"] = dvec_g2
    slots = {n: [None] * nl for n in ("w_in", "w_out", "w_up", "w_down")}
    recvd = {n: [None] * nl for n in slots}
    send = lambda n, l_: (slots[n][l_], TO_OWNER_XOR)
    for l in reversed(range(nl)):
        h_in, u_t, proj, q_all, kv_all, o, o_t, lse, mix, h_mid, v_t, r_act, r_t, y = saved[l]
        wl = weights_of[l]
        later = l + 1 < nl
        res = _mm(dy, wl["w_down_t"], name=f"mm_da{l}", epilogue="relu2_bwd", extra=r_act, out_dtype=BF16, tk=d,
                  comm=_Comm([send("w_out", l + 1)]) if later else None, **big)
        da = res[0] if later else res
        if later:
            recvd["w_out"][l + 1] = res[1][0]
        res = _mm(r_t, dy, name=f"mm_dw_down{l}", out_dtype=BF16,
                  comm=_Comm([send("w_in", l + 1)]) if later else None, **wide)
        dw_down = res[0] if later else res
        if later:
            recvd["w_in"][l + 1] = res[1][0]
        dw_up = _mm(v_t, da, name=f"mm_dw_up{l}", out_mode="slots", out_dtype=BF16, **wide)
        slots["w_down"][l] = dw_down.reshape(N_DEV, -1, d)
        slots["w_up"][l] = dw_up
        dv = _mm(da, wl["w_up_t"], name=f"mm_dv{l}", **deep)
        dh, dvecs[l]["m2"], dmix, dvecs[l]["g1"] = _mod_gate_bwd(
            h_mid, row(g_pre_mlp[l]), modsel[l], 3, 4, dv, dh, mix, row(g_post_mix[l]), modsel[l], 2, nt,
            f"mlp_mod_mix_gate_bwd{l}")
        do = _mm(dmix, wl["w_out_t"], name=f"mm_do{l}", out_dtype=BF16, tk=d, **big)
        dw_out = _mm(o_t, dmix, name=f"mm_dw_out{l}", out_dtype=BF16, **wide)
        slots["w_out"][l] = dw_out.reshape(N_DEV, -1, d)
        (dq, dkv, dsk), (recvd["w_down"][l], recvd["w_up"][l]) = _attn_bwd(
            q_all.reshape(bl, t_rows, 1024), kv_all.reshape(bl, t_rows, 512), do.reshape(bl, t_rows, 1024),
            o.reshape(bl, t_rows, 1024), lse, sink[l], c_rows, f"attn_bwd{l}",
            comm=_Comm([send("w_down", l), send("w_up", l)]))
        last = l == 0
        res = _prep_bwd(proj, dq.reshape(m_rows, 1024), dkv.reshape(m_rows, 512), qn[l], kn[l], cos, sin, emat, nt,
                        f"prep_bwd{l}", comm=_Comm([send("w_out", l)]) if last else None)
        dproj, dqn, dkn = res[0] if last else res
        if last:
            recvd["w_out"][l] = res[1][0]
        dw_in = _mm(u_t, dproj, name=f"mm_dw_in{l}", out_dtype=BF16, **wide)
        slots["w_in"][l] = dw_in.reshape(d, N_DEV, PROJ_W // N_DEV).transpose(1, 0, 2)
        du = _mm(dproj, wl["w_in_t"], name=f"mm_du{l}", tk=PROJ_W, **big)
        if last:
            (dh, dvecs[l]["m1"]), (recvd["w_in"][l],) = _norm_mod_bwd(
                h_in, row(g_pre_mix[l]), modsel[l], du, dh, 0, 1, nt, f"mix_mod_bwd{l}", latent_only=True,
                comm=_Comm([send("w_in", l)]))
        else:
            dh, dvecs[l]["m1"], dy, dvecs[l - 1]["g2"] = _mod_gate_bwd(
                h_in, row(g_pre_mix[l]), modsel[l], 0, 1, du, dh, saved[l - 1][-1], row(g_post_mlp[l - 1]),
                modsel[l - 1], 5, nt, f"mix_mod_mlp_gate_bwd{l}")
        small_g["q_norm"][l] = jnp.sum(dqn[0].reshape(8, HEAD_DIM), axis=0)
        small_g["k_norm"][l] = jnp.sum(dkn[0].reshape(2, HEAD_DIM), axis=0)
        small_g["sink"][l] = jnp.sum(dsk[:, 2:, :GROUP, 0], axis=0).reshape(-1)

    dmod_rows = []
    for l in range(nl):
        m1, g1, m2, g2 = (dvecs[l][n] for n in ("m1", "g1", "m2", "g2"))
        small_g["g_pre_mix"][l] = jnp.sum(m1[:, :, 2], axis=(0, 1))
        small_g["g_post_mix"][l] = jnp.sum(g1[:, :, 1], axis=(0, 1))
        small_g["g_pre_mlp"][l] = jnp.sum(m2[:, :, 2], axis=(0, 1))
        small_g["g_post_mlp"][l] = jnp.sum(g2[:, :, 1], axis=(0, 1))
        dms = jnp.stack([m1[:, :, 0], m1[:, :, 1], g1[:, :, 0], m2[:, :, 0], m2[:, :, 1], g2[:, :, 0]], axis=2)
        rows = jnp.concatenate([dms[:, 1], jnp.sum(dms[:, 0], axis=0)[None]], axis=0)
        dmod_rows.append(jnp.pad(rows.reshape(bl + 1, N_MOD * d), ((0, ADA_ROWS - bl - 1), (0, 0))))
    grad_x = dh.reshape(bl, s_rows, d)

    dmod_slots = jnp.stack(dmod_rows).reshape(nl, ADA_ROWS, N_DEV, ada_w).transpose(2, 0, 1, 3)
    dmod_g, = _comm_only(_Comm([(dmod_slots, TO_OWNER)]), "exchange_dmod")
    dmod_mine = dmod_g.transpose(1, 0, 2, 3).reshape(nl, N_DEV * ADA_ROWS, ada_w)
    dw_ada, dcc = _ada_bwd(c_all, c_ctx[None, :], dmod_mine, w_ada, "ada_bwd")

    parts = {n: jnp.stack(small_g[n]) for n in small_g}
    parts["c_ctx"] = dcc[0]
    parts["loss"] = ss[0, 0:1]
    no_loss = jnp.zeros((1,), F32)
    parts["b_ada"] = jnp.stack([jnp.sum(r_[: bl + 1], axis=0) for r_ in dmod_rows])
    weights = dict(c_ctx=c_ctx, b_ada=b_ada, g_pre_mix=g_pre_mix, g_post_mix=g_post_mix, g_pre_mlp=g_pre_mlp,
                   g_post_mlp=g_post_mlp, q_norm=q_norm, k_norm=k_norm, sink=sink, loss=no_loss)
    moms = dict(c_ctx=m_c_ctx, b_ada=m_b_ada, g_pre_mix=m_g_pre_mix, g_post_mix=m_g_post_mix, g_pre_mlp=m_g_pre_mlp,
                g_post_mlp=m_g_post_mlp, q_norm=m_q_norm, k_norm=m_k_norm, sink=m_sink, loss=no_loss)
    vels = dict(c_ctx=v_c_ctx, b_ada=v_b_ada, g_pre_mix=v_g_pre_mix, g_post_mix=v_g_post_mix, g_pre_mlp=v_g_pre_mlp,
                g_post_mlp=v_g_post_mlp, q_norm=v_q_norm, k_norm=v_k_norm, sink=v_sink, loss=no_loss)
    small_all, = _comm_only(_Comm([(_pack_small(parts), GATHER)]), "gather_small")
    s_out = _small_adamw(_pack_small(weights), _pack_small(moms), _pack_small(vels), small_all, "adamw_small")
    s_g, s_d, s_m, s_v = [_unpack_small(a, weights) for a in s_out]
    loss = 0.5 * s_g["loss"][0] / d

    res = {}
    for n, w_, m_, v_ in (("w_in", w_in, m_w_in, v_w_in), ("w_out", w_out, m_w_out, v_w_out),
                          ("w_up", w_up, m_w_up, v_w_up), ("w_down", w_down, m_w_down, v_w_down)):
        own = [lax.dynamic_index_in_dim(slots[n][l], me, axis=0, keepdims=False) for l in range(nl)]
        res[n] = _adamw(w_, m_, v_, own, recvd[n], f"adamw_{n}")
    res["w_ada"] = _adamw(w_ada, m_w_ada, v_w_ada, dw_ada, None, "adamw_w_ada")

    order = ("c_ctx", "w_ada", "b_ada", "g_pre_mix", "g_post_mix", "g_pre_mlp", "g_post_mlp", "w_in", "q_norm",
             "k_norm", "sink", "w_out", "w_up", "w_down")
    outs = [loss, grad_x]
    for i, small in enumerate((s_g, s_d, s_m, s_v)):
        outs += [small[n] if n in small else res[n][i] for n in order]
    return tuple(outs)
```

```python
import functools

import jax
import jax.numpy as jnp
import numpy as np
from jax import lax
from jax.experimental import pallas as pl
from jax.experimental.pallas import tpu as pltpu

F32 = jnp.float32
BF16 = jnp.bfloat16

HEAD_DIM = 64
GROUP = 4
N_HG = 4
Q_WIDTH = GROUP * HEAD_DIM
WINDOW = 128
GRID_W = 64
ROPE_THETA = 10000.0
EPS = 1e-6
NEG_BIG = -1e30
N_MOD = 6
MOD_ROWS = 8
TM = 256
N_DEV = 8
ADA_ROWS = 8
VMEM_LIMIT = 56 * 1024 * 1024

ADAM_LR = 0.001
ADAM_B1 = 0.9
ADAM_B2 = 0.999
ADAM_EPS = 1e-08
ADAM_WD = 0.01
ADAM_STEP = 10


def _params(sem=None):
    kw = dict(vmem_limit_bytes=VMEM_LIMIT)
    if sem is not None:
        kw["dimension_semantics"] = sem
    return pltpu.CompilerParams(**kw)


def _my_index():
    return 4 * lax.axis_index("x") + 2 * lax.axis_index("y") + lax.axis_index("c")


def _peer(k):
    x, y, c = lax.axis_index("x"), lax.axis_index("y"), lax.axis_index("c")
    kx, ky, kc = (k >> 2) & 1, (k >> 1) & 1, k & 1
    px = (1 - x) if kx else x
    py = (1 - y) if ky else y
    pc = (1 - c) if kc else c
    return (px, py, pc), 4 * px + 2 * py + pc


GATHER, TO_OWNER, TO_OWNER_XOR = "gather", "to_owner", "to_owner_xor"


class _Comm:
    def __init__(self, items):
        self.items = list(items)
        self.arrays = [a for a, _ in self.items]

    def out_shapes(self):
        return [jax.ShapeDtypeStruct(((N_DEV,) + a.shape) if kind == GATHER else a.shape, a.dtype)
                for a, kind in self.items]

    def sem_shapes(self):
        n = len(self.items) * N_DEV
        return [pltpu.SemaphoreType.DMA((n,)), pltpu.SemaphoreType.DMA((n,))]

    def _copies(self, in_refs, out_refs, send_sems, recv_sems):
        me = _my_index()
        local, remote = [], []
        for i, ((_, kind), x_ref, o_ref) in enumerate(zip(self.items, in_refs, out_refs)):
            base = i * N_DEV
            own_src = x_ref if kind == GATHER else x_ref.at[me]
            own_dst = o_ref.at[0] if kind == TO_OWNER_XOR else o_ref.at[me]
            local.append(pltpu.make_async_copy(own_src, own_dst, send_sems.at[base]))
            for k in range(1, N_DEV):
                peer, pidx = _peer(k)
                remote.append(pltpu.make_async_remote_copy(
                    src_ref=x_ref if kind == GATHER else x_ref.at[pidx],
                    dst_ref=o_ref.at[k] if kind == TO_OWNER_XOR else o_ref.at[me],
                    send_sem=send_sems.at[base + k], recv_sem=recv_sems.at[base + k],
                    device_id=peer, device_id_type=pl.DeviceIdType.MESH))
        return local, remote

    def start(self, in_refs, out_refs, send_sems, recv_sems):
        local, remote = self._copies(in_refs, out_refs, send_sems, recv_sems)
        for cp in local + remote:
            cp.start()

    def wait(self, in_refs, out_refs, send_sems, recv_sems):
        local, remote = self._copies(in_refs, out_refs, send_sems, recv_sems)
        for cp in remote:
            cp.wait_recv()
        for cp in remote:
            cp.wait_send()
        for cp in local:
            cp.wait()


def _call(body, *, name, grid, in_specs, out_specs, out_shape, args, scratch_shapes=(), prefetch=0, sem=None,
          comm=None):
    single = not isinstance(out_shape, (list, tuple))
    out_shape = [out_shape] if single else list(out_shape)
    out_specs = [out_specs] if single else list(out_specs)
    in_specs, scratch_shapes, args = list(in_specs), list(scratch_shapes), list(args)
    n_in, n_out = len(in_specs), len(out_shape)
    if comm is not None:
        nc = len(comm.arrays)
        hbm = pl.BlockSpec(memory_space=pl.ANY)
        inner = body

        def body(*refs):
            pre, r = refs[:prefetch], refs[prefetch:]
            ins, cin = r[:n_in], r[n_in:n_in + nc]
            outs, cout = r[n_in + nc:n_in + nc + n_out], r[n_in + nc + n_out:n_in + 2 * nc + n_out]
            scr, sems = r[n_in + 2 * nc + n_out:len(r) - 2], r[len(r) - 2:]
            ids = [pl.program_id(i) for i in range(len(grid))]

            def when(flags, fn):
                if flags:
                    pl.when(functools.reduce(jnp.logical_and, flags))(fn)
                else:
                    fn()

            when([i == 0 for i in ids], lambda: comm.start(cin, cout, *sems))
            inner(*pre, *ins, *outs, *scr)
            when([i == n - 1 for i, n in zip(ids, grid)], lambda: comm.wait(cin, cout, *sems))

        in_specs += [hbm] * nc
        out_specs += [hbm] * nc
        out_shape += comm.out_shapes()
        scratch_shapes += comm.sem_shapes()
        args += comm.arrays
        sem = ("arbitrary",) * len(grid)
    kw = dict(name=name, out_shape=out_shape, compiler_params=_params(sem if grid else None))
    if prefetch:
        kw["grid_spec"] = pltpu.PrefetchScalarGridSpec(
            num_scalar_prefetch=prefetch, grid=grid, in_specs=in_specs, out_specs=out_specs,
            scratch_shapes=scratch_shapes)
    else:
        kw.update(in_specs=in_specs, out_specs=out_specs, scratch_shapes=scratch_shapes)
        if grid:
            kw["grid"] = grid
    res = list(pl.pallas_call(body, **kw)(*args))
    outs = res[:n_out]
    return (outs[0] if single else outs), res[n_out:]


def _comm_only(comm, name):
    return _call(lambda: None, name=name, grid=(), in_specs=[], out_specs=[], out_shape=[], args=[], comm=comm)[1]


def _mm(a, b, *, name, ta=False, b_mode="nn", out_mode="plain", out_dtype=F32, tm=512, tn=512, tk=512,
        epilogue=None, extra=None, comm=None):
    if ta:
        kdim, m = a.shape
    else:
        m, kdim = a.shape
    if b_mode == "nn":
        n = b.shape[1]
    elif b_mode == "nt":
        n = b.shape[0]
    elif b_mode == "nn_slots":
        n = b.shape[0] * b.shape[2]
        tn = b.shape[2]
    else:
        n = b.shape[1]
        tk = b.shape[2]
    if out_mode == "slots":
        tn = n // N_DEV
    tm, tn, tk = min(tm, m), min(tn, n), min(tk, kdim)
    assert m % tm == 0 and n % tn == 0 and kdim % tk == 0, (name, m, n, kdim, tm, tn, tk)
    nk = kdim // tk

    a_spec = pl.BlockSpec((tk, tm), lambda i, j, k: (k, i)) if ta else pl.BlockSpec((tm, tk), lambda i, j, k: (i, k))
    if b_mode == "nn":
        b_spec = pl.BlockSpec((tk, tn), lambda i, j, k: (k, j))
    elif b_mode == "nt":
        b_spec = pl.BlockSpec((tn, tk), lambda i, j, k: (j, k))
    elif b_mode == "nn_slots":
        b_spec = pl.BlockSpec((None, tk, tn), lambda i, j, k: (j, k, 0))
    else:
        b_spec = pl.BlockSpec((None, tn, tk), lambda i, j, k: (k, j, 0))
    tb = b_mode in ("nt", "nt_slots")
    if out_mode == "plain":
        o_shape, o_spec = (m, n), pl.BlockSpec((tm, tn), lambda i, j, k: (i, j))
    else:
        o_shape, o_spec = (N_DEV, m, tn), pl.BlockSpec((None, tm, tn), lambda i, j, k: (j, i, 0))
    dims = (((0 if ta else 1,), (1 if tb else 0,)), ((), ()))

    in_specs = [a_spec, b_spec]
    args = [a, b]
    if epilogue == "relu2_bwd":
        in_specs.append(pl.BlockSpec((tm, tn), lambda i, j, k: (i, j)))
        args.append(extra)
    if epilogue == "relu2":
        out_shape = [jax.ShapeDtypeStruct(o_shape, BF16), jax.ShapeDtypeStruct((n, m), BF16)]
        out_specs = [o_spec, pl.BlockSpec((tn, tm), lambda i, j, k: (j, i))]
    else:
        out_shape = jax.ShapeDtypeStruct(o_shape, out_dtype)
        out_specs = o_spec

    def finish(refs, acc):
        if epilogue == "relu2":
            r = jnp.maximum(acc, 0.0)
            r2 = (r * r).astype(BF16)
            refs[2][...] = r2
            refs[3][...] = r2.T
        elif epilogue == "relu2_bwd":
            refs[3][...] = (acc * (2.0 * jnp.sqrt(refs[2][...]).astype(F32))).astype(out_dtype)
        else:
            refs[2][...] = acc.astype(out_dtype)

    def body(*refs):
        part = lax.dot_general(refs[0][...], refs[1][...], dims, preferred_element_type=F32)
        if nk == 1:
            finish(refs, part)
            return
        acc_ref = refs[-1]
        k = pl.program_id(2)

        @pl.when(k == 0)
        def _():
            acc_ref[...] = part

        @pl.when(jnp.logical_and(k > 0, k < nk - 1))
        def _():
            acc_ref[...] += part

        @pl.when(k == nk - 1)
        def _():
            finish(refs, acc_ref[...] + part)

    outs, couts = _call(
        body, name=name, grid=(m // tm, n // tn, nk), in_specs=in_specs, out_specs=out_specs, out_shape=out_shape,
        args=args, scratch_shapes=[] if nk == 1 else [pltpu.VMEM((tm, tn), F32)],
        sem=("parallel", "parallel", "arbitrary"), comm=comm)
    return outs if comm is None else (outs, couts)


def _row_specs(d, nt):
    row = pl.BlockSpec((TM, d), lambda t: (t, 0))
    vec = pl.BlockSpec((1, d), lambda t: (0, 0))
    mod = pl.BlockSpec((None, None, MOD_ROWS, d), lambda t: (t // nt, jnp.minimum(t % nt, 1), 0, 0))
    return row, vec, mod


def _rms(x):
    r = lax.rsqrt(jnp.mean(x * x, axis=1, keepdims=True) + EPS)
    return r, x * r


def _token_stream(ctx, x, nt, name, comm=None):
    bl, c_rows, d = ctx.shape
    m = bl * nt * TM

    def body(ctx_ref, x_ref, h_ref):
        t = pl.program_id(0)

        @pl.when(t % nt == 0)
        def _():
            h_ref[...] = ctx_ref[...]

        @pl.when(t % nt != 0)
        def _():
            h_ref[...] = x_ref[...]

    outs, couts = _call(
        body, name=name, grid=(m // TM,),
        in_specs=[pl.BlockSpec((None, TM, d), lambda t: (t // nt, 0, 0)),
                  pl.BlockSpec((None, TM, d), lambda t: (t // nt, jnp.maximum(t % nt, 1) - 1, 0))],
        out_specs=pl.BlockSpec((TM, d), lambda t: (t, 0)), out_shape=jax.ShapeDtypeStruct((m, d), F32),
        args=[ctx, x], sem=("parallel",), comm=comm)
    return outs if comm is None else (outs, couts)


def _norm_mod_fwd(h, g, modsel, i_sh, i_sc, nt, name, comm=None):
    m, d = h.shape
    row, vec, mod = _row_specs(d, nt)

    def body(h_ref, g_ref, ms_ref, u_ref, ut_ref):
        _, xh = _rms(h_ref[...])
        ms = ms_ref[...]
        u = (xh * g_ref[...] * (1.0 + ms[i_sc:i_sc + 1]) + ms[i_sh:i_sh + 1]).astype(BF16)
        u_ref[...] = u
        ut_ref[...] = u.T

    outs, couts = _call(
        body, name=name, grid=(m // TM,), in_specs=[row, vec, mod],
        out_specs=[row, pl.BlockSpec((d, TM), lambda t: (0, t))],
        out_shape=[jax.ShapeDtypeStruct((m, d), BF16), jax.ShapeDtypeStruct((d, m), BF16)],
        args=[h, g, modsel], sem=("parallel",), comm=comm)
    return outs if comm is None else (outs, couts)


def _acc_rows(t, nt, dvec_ref, rows):
    first = (t % nt) <= 1

    @pl.when(first)
    def _():
        dvec_ref[...] = rows

    @pl.when(jnp.logical_not(first))
    def _():
        dvec_ref[...] += rows


def _norm_mod_bwd(h, g, modsel, du, dh_in, i_sh, i_sc, nt, name, comm=None, latent_only=False):
    m, d = h.shape
    row, vec, mod = _row_specs(d, nt)
    dh_rows, dh_spec = m, row
    if latent_only:
        dh_rows = m // nt * (nt - 1)
        dh_spec = pl.BlockSpec((TM, d), lambda t: ((t // nt) * (nt - 1) + jnp.maximum(t % nt, 1) - 1, 0))

    def body(h_ref, g_ref, ms_ref, du_ref, dhi_ref, dh_ref, dvec_ref):
        t = pl.program_id(0)
        r, xh = _rms(h_ref[...])
        g_ = g_ref[...]
        ms = ms_ref[...]
        du_ = du_ref[...]
        y = xh * g_
        dy = du_ * (1.0 + ms[i_sc:i_sc + 1])
        dxh = dy * g_
        dx = r * (dxh - xh * jnp.mean(dxh * xh, axis=1, keepdims=True))
        dh_ref[...] = dhi_ref[...] + dx
        rows = jnp.concatenate([
            jnp.sum(du_, axis=0, keepdims=True), jnp.sum(du_ * y, axis=0, keepdims=True),
            jnp.sum(dy * xh, axis=0, keepdims=True), jnp.zeros((MOD_ROWS - 3, d), F32)], axis=0)
        _acc_rows(t, nt, dvec_ref, rows)

    outs, couts = _call(
        body, name=name, grid=(m // TM,), in_specs=[row, vec, mod, row, row], out_specs=[dh_spec, mod],
        out_shape=[jax.ShapeDtypeStruct((dh_rows, d), F32), jax.ShapeDtypeStruct(modsel.shape, F32)],
        args=[h, g, modsel, du, dh_in], sem=("arbitrary",), comm=comm)
    return outs if comm is None else (outs, couts)


def _gate_rows(dh_, xh, g_, gate):
    dy = dh_ * gate
    return dy * g_, jnp.sum(dh_ * (xh * g_), axis=0, keepdims=True), jnp.sum(dy * xh, axis=0, keepdims=True)


def _gate_mod_fwd(h, z, g_post, ms_gate, i_g, g_pre, ms_mod, i_sh, i_sc, nt, name):
    m, d = h.shape
    row, vec, mod = _row_specs(d, nt)

    def body(h_ref, z_ref, gp_ref, msg_ref, gq_ref, msm_ref, hn_ref, u_ref, ut_ref):
        _, zh = _rms(z_ref[...])
        hn = h_ref[...] + msg_ref[...][i_g:i_g + 1] * (zh * gp_ref[...])
        hn_ref[...] = hn
        _, xh = _rms(hn)
        ms = msm_ref[...]
        u = (xh * gq_ref[...] * (1.0 + ms[i_sc:i_sc + 1]) + ms[i_sh:i_sh + 1]).astype(BF16)
        u_ref[...] = u
        ut_ref[...] = u.T

    return pl.pallas_call(
        body, name=name, grid=(m // TM,), in_specs=[row, row, vec, mod, vec, mod],
        out_specs=[row, row, pl.BlockSpec((d, TM), lambda t: (0, t))],
        out_shape=[jax.ShapeDtypeStruct((m, d), F32), jax.ShapeDtypeStruct((m, d), BF16),
                   jax.ShapeDtypeStruct((d, m), BF16)],
        compiler_params=_params(("parallel",)),
    )(h, z, g_post, ms_gate, g_pre, ms_mod)


def _mod_gate_bwd(h, g_pre, ms_mod, i_sh, i_sc, du, dh_in, z, g_post, ms_gate, i_g, nt, name):
    m, d = h.shape
    row, vec, mod = _row_specs(d, nt)

    def body(h_ref, gq_ref, msm_ref, du_ref, dhi_ref, z_ref, gp_ref, msg_ref, dh_ref, dvm_ref, dz_ref, dvg_ref):
        t = pl.program_id(0)
        r, xh = _rms(h_ref[...])
        gq = gq_ref[...]
        ms = msm_ref[...]
        du_ = du_ref[...]
        dy = du_ * (1.0 + ms[i_sc:i_sc + 1])
        dxh = dy * gq
        dh_ = dhi_ref[...] + r * (dxh - xh * jnp.mean(dxh * xh, axis=1, keepdims=True))
        dh_ref[...] = dh_
        _acc_rows(t, nt, dvm_ref, jnp.concatenate([
            jnp.sum(du_, axis=0, keepdims=True), jnp.sum(du_ * (xh * gq), axis=0, keepdims=True),
            jnp.sum(dy * xh, axis=0, keepdims=True), jnp.zeros((MOD_ROWS - 3, d), F32)], axis=0))
        rz, zh = _rms(z_ref[...])
        dzh, d_gate, d_gp = _gate_rows(dh_, zh, gp_ref[...], msg_ref[...][i_g:i_g + 1])
        dz_ref[...] = (rz * (dzh - zh * jnp.mean(dzh * zh, axis=1, keepdims=True))).astype(BF16)
        _acc_rows(t, nt, dvg_ref, jnp.concatenate([d_gate, d_gp, jnp.zeros((MOD_ROWS - 2, d), F32)], axis=0))

    return pl.pallas_call(
        body, name=name, grid=(m // TM,), in_specs=[row, vec, mod, row, row, row, vec, mod],
        out_specs=[row, mod, row, mod],
        out_shape=[jax.ShapeDtypeStruct((m, d), F32), jax.ShapeDtypeStruct(ms_mod.shape, F32),
                   jax.ShapeDtypeStruct((m, d), BF16), jax.ShapeDtypeStruct(ms_gate.shape, F32)],
        compiler_params=_params(("arbitrary",)),
    )(h, g_pre, ms_mod, du, dh_in, z, g_post, ms_gate)


def _gate_loss_bwd(h, z, g_post, modsel, i_g, target, nt, name):
    m, d = h.shape
    row, vec, mod = _row_specs(d, nt)
    ntl = nt - 1
    tgt = pl.BlockSpec((TM, d), lambda t: ((t // nt) * ntl + jnp.maximum(t % nt, 1) - 1, 0))
    acc = pl.BlockSpec((8, 128), lambda t: (0, 0))

    def body(h_ref, z_ref, gp_ref, ms_ref, t_ref, dh_ref, dz_ref, dvg_ref, ss_ref):
        t = pl.program_id(0)

        @pl.when(t == 0)
        def _():
            ss_ref[...] = jnp.zeros_like(ss_ref)

        latent = (t % nt != 0).astype(F32)
        rz, zh = _rms(z_ref[...])
        gp = gp_ref[...]
        gate = ms_ref[...][i_g:i_g + 1]
        e = h_ref[...] + gate * (zh * gp) - t_ref[...]
        ss_ref[...] += latent * jnp.sum(e * e)
        dh_ = e * (latent / d)
        dh_ref[...] = dh_
        dzh, d_gate, d_gp = _gate_rows(dh_, zh, gp, gate)
        dz_ref[...] = (rz * (dzh - zh * jnp.mean(dzh * zh, axis=1, keepdims=True))).astype(BF16)
        _acc_rows(t, nt, dvg_ref, jnp.concatenate([d_gate, d_gp, jnp.zeros((MOD_ROWS - 2, d), F32)], axis=0))

    return pl.pallas_call(
        body, name=name, grid=(m // TM,), in_specs=[row, row, vec, mod, tgt], out_specs=[row, row, mod, acc],
        out_shape=[jax.ShapeDtypeStruct((m, d), F32), jax.ShapeDtypeStruct((m, d), BF16),
                   jax.ShapeDtypeStruct(modsel.shape, F32), jax.ShapeDtypeStruct((8, 128), F32)],
        compiler_params=_params(("arbitrary",)),
    )(h, z, g_post, modsel, target)


QA, KA, VA, QB, KB, VB = 0, 512, 640, 768, 1280, 1408
PROJ_W = 1536
Q_SCALE = HEAD_DIM ** -0.5
LOG2E = 1.4426950408889634


def _swap16(x):
    lane = lax.broadcasted_iota(jnp.int32, x.shape, 1)
    n = x.shape[1]
    return jnp.where((lane % 32) < 16, pltpu.roll(x, n - 16, 1), pltpu.roll(x, 16, 1))


def _seg_mean(x, e):
    hi = x.astype(BF16)
    lo = (x - hi.astype(F32)).astype(BF16)
    w = e.shape[0]
    both = lambda a: jnp.dot(hi[:, a:a + w], e, preferred_element_type=F32) + jnp.dot(lo[:, a:a + w], e, preferred_element_type=F32)
    parts = [both(a) for a in range(0, x.shape[1], w)]
    return parts[0] if len(parts) == 1 else jnp.concatenate(parts, axis=1)


def _rope_tables(t_rows, c_rows):
    s = t_rows - c_rows
    row_ids = jnp.repeat(jnp.arange(s // GRID_W, dtype=jnp.int32), GRID_W).astype(F32)
    col_ids = jnp.tile(jnp.arange(GRID_W, dtype=jnp.int32), s // GRID_W).astype(F32)
    axis_dim = HEAD_DIM // 2
    inv = ROPE_THETA ** (-jnp.arange(0, axis_dim, 2, dtype=F32) / axis_dim)
    ang_r = row_ids[:, None] * inv[None, :]
    ang_c = col_ids[:, None] * inv[None, :]
    cos = jnp.concatenate([jnp.cos(ang_r), jnp.cos(ang_r), jnp.cos(ang_c), jnp.cos(ang_c)], axis=1)
    sin = jnp.concatenate([-jnp.sin(ang_r), jnp.sin(ang_r), -jnp.sin(ang_c), jnp.sin(ang_c)], axis=1)
    cos = jnp.concatenate([jnp.ones((c_rows, HEAD_DIM), F32), cos], axis=0)
    sin = jnp.concatenate([jnp.zeros((c_rows, HEAD_DIM), F32), sin], axis=0)
    return jnp.tile(cos, (1, 8)), jnp.tile(sin, (1, 8))


def _head_mean_matrix():
    i = np.arange(512)
    return jnp.asarray((i[:, None] // HEAD_DIM == i[None, :] // HEAD_DIM).astype(np.float32) / HEAD_DIM, dtype=BF16)


def _interleave_kv(k, v):
    return jnp.concatenate([k[:, :64], v[:, :64], k[:, 64:], v[:, 64:]], axis=1)


def _prep_fwd(proj, qn, kn, cos, sin, emat, nt, name):
    m = proj.shape[0]
    specs = [
        pl.BlockSpec((TM, PROJ_W), lambda t: (t, 0)),
        pl.BlockSpec((1, 512), lambda t: (0, 0)), pl.BlockSpec((1, 128), lambda t: (0, 0)),
        pl.BlockSpec((TM, 512), lambda t: (t % nt, 0)), pl.BlockSpec((TM, 512), lambda t: (t % nt, 0)),
        pl.BlockSpec((512, 512), lambda t: (0, 0)),
    ]

    def body(p_ref, qn_ref, kn_ref, cos_ref, sin_ref, e_ref, q_ref, kv_ref):
        cos_, sin_, e = cos_ref[...], sin_ref[...], e_ref[...]

        def rope(x, w):
            return x * cos_[:, :w] + _swap16(x) * sin_[:, :w]

        def norm(x, g, w):
            return x * lax.rsqrt(_seg_mean(x * x, e[:min(w, 256), :min(w, 256)]) + EPS) * g

        qa = rope(norm(p_ref[:, QA:QA + 512], qn_ref[...], 512), 512)
        qb = rope(p_ref[:, QB:QB + 512], 512)
        q_ref[:, 0:512] = (qa * (Q_SCALE * LOG2E)).astype(BF16)
        q_ref[:, 512:1024] = (qb * (Q_SCALE * LOG2E)).astype(BF16)
        ka = rope(norm(p_ref[:, KA:KA + 128], kn_ref[...], 128), 128)
        kb = rope(p_ref[:, KB:KB + 128], 128)
        kv_ref[:, 0:256] = _interleave_kv(ka, p_ref[:, VA:VA + 128]).astype(BF16)
        kv_ref[:, 256:512] = _interleave_kv(kb, p_ref[:, VB:VB + 128]).astype(BF16)

    return pl.pallas_call(
        body, name=name, grid=(m // TM,), in_specs=specs,
        out_specs=[pl.BlockSpec((TM, 1024), lambda t: (t, 0)), pl.BlockSpec((TM, 512), lambda t: (t, 0))],
        out_shape=[jax.ShapeDtypeStruct((m, 1024), BF16), jax.ShapeDtypeStruct((m, 512), BF16)],
        compiler_params=_params(("parallel",)),
    )(proj, qn, kn, cos, sin, emat)


def _prep_bwd(proj, dq, dkv, qn, kn, cos, sin, emat, nt, name, comm=None):
    m = proj.shape[0]
    specs = [
        pl.BlockSpec((TM, PROJ_W), lambda t: (t, 0)),
        pl.BlockSpec((TM, 1024), lambda t: (t, 0)), pl.BlockSpec((TM, 512), lambda t: (t, 0)),
        pl.BlockSpec((1, 512), lambda t: (0, 0)), pl.BlockSpec((1, 128), lambda t: (0, 0)),
        pl.BlockSpec((TM, 512), lambda t: (t % nt, 0)), pl.BlockSpec((TM, 512), lambda t: (t % nt, 0)),
        pl.BlockSpec((512, 512), lambda t: (0, 0)),
    ]

    def body(p_ref, dq_ref, dkv_ref, qn_ref, kn_ref, cos_ref, sin_ref, e_ref, dp_ref, dqn_ref, dkn_ref):
        t = pl.program_id(0)
        cos_, sin_, e = cos_ref[...], sin_ref[...], e_ref[...]

        @pl.when(t == 0)
        def _():
            dqn_ref[...] = jnp.zeros_like(dqn_ref)
            dkn_ref[...] = jnp.zeros_like(dkn_ref)

        def unrope(dy, w):
            return dy * cos_[:, :w] + _swap16(dy * sin_[:, :w])

        def norm_bwd(x, g, dy, w):
            r = lax.rsqrt(_seg_mean(x * x, e[:min(w, 256), :min(w, 256)]) + EPS)
            xh = x * r
            dxh = dy * g
            dx = r * (dxh - xh * _seg_mean(dxh * xh, e[:min(w, 256), :min(w, 256)]))
            return dx, jnp.sum(dy * xh, axis=0, keepdims=True)

        dqa, dgq = norm_bwd(p_ref[:, QA:QA + 512], qn_ref[...], unrope(dq_ref[:, 0:512] * Q_SCALE, 512), 512)
        dp_ref[:, QA:QA + 512] = dqa.astype(BF16)
        dp_ref[:, QB:QB + 512] = unrope(dq_ref[:, 512:1024] * Q_SCALE, 512).astype(BF16)
        da = dkv_ref[:, 0:256]
        db = dkv_ref[:, 256:512]
        dka = jnp.concatenate([da[:, 0:64], da[:, 128:192]], axis=1)
        dva = jnp.concatenate([da[:, 64:128], da[:, 192:256]], axis=1)
        dkb = jnp.concatenate([db[:, 0:64], db[:, 128:192]], axis=1)
        dvb = jnp.concatenate([db[:, 64:128], db[:, 192:256]], axis=1)
        dka, dgk = norm_bwd(p_ref[:, KA:KA + 128], kn_ref[...], unrope(dka, 128), 128)
        dp_ref[:, KA:KA + 128] = dka.astype(BF16)
        dp_ref[:, VA:VA + 128] = dva.astype(BF16)
        dp_ref[:, KB:KB + 128] = unrope(dkb, 128).astype(BF16)
        dp_ref[:, VB:VB + 128] = dvb.astype(BF16)
        dqn_ref[0:1, :] += dgq
        dkn_ref[0:1, :] += dgk

    outs, couts = _call(
        body, name=name, grid=(m // TM,), in_specs=specs,
        out_specs=[pl.BlockSpec((TM, PROJ_W), lambda t: (t, 0)), pl.BlockSpec((8, 512), lambda t: (0, 0)),
                   pl.BlockSpec((8, 128), lambda t: (0, 0))],
        out_shape=[jax.ShapeDtypeStruct((m, PROJ_W), BF16), jax.ShapeDtypeStruct((8, 512), F32),
                   jax.ShapeDtypeStruct((8, 128), F32)],
        args=[proj, dq, dkv, qn, kn, cos, sin, emat], sem=("arbitrary",), comm=comm)
    return outs if comm is None else (outs, couts)


def _attn_case(t, hg, kv_ref, c_rows, t_rows, fn, keys_first=False):
    wl = TM + 2 * WINDOW
    kvd = lambda a, n: kv_ref[pl.ds(a, n), :]
    dense = hg < 2
    ctx = t == 0

    @pl.when(jnp.logical_and(dense, ctx))
    def _():
        kv = kvd(0, c_rows)
        fn(kv[:, :64], kv[:, 64:], None, False, [(0, c_rows)])

    @pl.when(jnp.logical_and(dense, jnp.logical_not(ctx)))
    def _():
        kv = kvd(0, t_rows)
        fn(kv[:, :64], kv[:, 64:], None, False, [(0, t_rows)])

    @pl.when(jnp.logical_and(jnp.logical_not(dense), ctx))
    def _():
        kv = kvd(0, c_rows)
        fn(kv[:, :64], kv[:, 64:], None, True, [(0, c_rows)])

    @pl.when(jnp.logical_and(jnp.logical_not(dense), jnp.logical_not(ctx)))
    def _():
        start = pl.multiple_of(jnp.minimum(c_rows + (t - 1) * TM - WINDOW, t_rows - wl), 128)
        kv = jnp.concatenate([kvd(0, c_rows), kvd(start, wl)], axis=0)
        shape = (c_rows + wl, TM) if keys_first else (TM, c_rows + wl)
        q_i = lax.broadcasted_iota(jnp.int32, shape, 1 if keys_first else 0)
        k_i = lax.broadcasted_iota(jnp.int32, shape, 0 if keys_first else 1)
        qpos = (t - 1) * TM + q_i
        kpos = start - 2 * c_rows + k_i
        mask = jnp.logical_or(k_i < c_rows, jnp.logical_and(jnp.abs(kpos - qpos) <= WINDOW, kpos >= 0))
        fn(kv[:, :64], kv[:, 64:], mask, True, [(0, c_rows), (start, wl)])


def _head_columns(cols):
    lane = lax.broadcasted_iota(jnp.int32, (TM, 128), 1)
    out = jnp.zeros((TM, 128), F32)
    for g, col in enumerate(cols):
        out = jnp.where(lane == g, col, out)
    return out


def _attn_specs(t_rows):
    nt = t_rows // TM
    q_spec = pl.BlockSpec((None, TM, Q_WIDTH), lambda b, hg, t, s: (b, t, hg))
    kv_spec = pl.BlockSpec((None, t_rows, 128), lambda b, hg, t, s: (b, 0, hg))
    lse_spec = pl.BlockSpec((None, TM, 128), lambda b, hg, t, s: (hg, b * nt + t, 0))
    return q_spec, kv_spec, lse_spec


def _attn_fwd(q_all, kv_all, sink8, c_rows, name, comm=None):
    bl, t_rows, _ = q_all.shape
    q_spec, kv_spec, lse_spec = _attn_specs(t_rows)

    def body(sink_ref, q_ref, kv_ref, o_ref, ot_ref, lse_ref):
        hg, t = pl.program_id(1), pl.program_id(2)

        def fn(k, v, mask, use_sink, spans):
            outs, lses = [], []
            v_one = jnp.concatenate([v, jnp.ones(v.shape, BF16)], axis=1)
            def scores(g):
                s = lax.dot_general(q_ref[:, g * 64:(g + 1) * 64], k, (((1,), (1,)), ((), ())),
                                    preferred_element_type=F32)
                return s if mask is None else jnp.where(mask, s, NEG_BIG)

            s_next = scores(0)
            for g in range(GROUP):
                s = s_next
                if g + 1 < GROUP:
                    s_next = scores(g + 1)
                mx = jnp.max(s, axis=1, keepdims=True)
                if use_sink:
                    sink = sink_ref[jnp.maximum(hg - 2, 0) * GROUP + g] * LOG2E
                    mx = jnp.maximum(mx, sink)
                pv = jnp.dot(jnp.exp2(s - mx).astype(BF16), v_one, preferred_element_type=F32)
                l = pv[:, 64:65]
                if use_sink:
                    l = l + jnp.exp2(sink - mx)
                outs.append(pv[:, :64] * (1.0 / l))
                lses.append(mx + jnp.log2(l))
            o = jnp.concatenate(outs, axis=1).astype(BF16)
            o_ref[...] = o
            ot_ref[...] = o.T
            lse_ref[...] = _head_columns(lses)

        _attn_case(t, hg, kv_ref, c_rows, t_rows, fn)

    nt = t_rows // TM
    ot_spec = pl.BlockSpec((Q_WIDTH, TM), lambda b, hg, t, s: (hg, b * nt + t))
    outs, couts = _call(
        body, name=name, grid=(bl, N_HG, nt), in_specs=[q_spec, kv_spec], out_specs=[q_spec, ot_spec, lse_spec],
        out_shape=[jax.ShapeDtypeStruct(q_all.shape, BF16), jax.ShapeDtypeStruct((N_HG * Q_WIDTH, bl * t_rows), BF16),
                   jax.ShapeDtypeStruct((N_HG, bl * t_rows, 128), F32)],
        args=[sink8, q_all, kv_all], prefetch=1, sem=("parallel", "parallel", "arbitrary"), comm=comm)
    return outs if comm is None else (outs, couts)


def _attn_bwd(q_all, kv_all, do, o, lse, sink8, c_rows, name, comm=None):
    bl, t_rows, _ = q_all.shape
    q_spec, kv_spec, lse_spec = _attn_specs(t_rows)
    ds_spec = pl.BlockSpec((None, None, 8, 128), lambda b, hg, t, s: (b, hg, 0, 0))

    def body(sink_ref, q_ref, kv_ref, do_ref, o_ref, lse_ref, dq_ref, dkv_ref, dsk_ref):
        hg, t = pl.program_id(1), pl.program_id(2)

        @pl.when(t == 0)
        def _():
            dkv_ref[...] = jnp.zeros_like(dkv_ref)
            dsk_ref[...] = jnp.zeros_like(dsk_ref)

        lse_rows = lse_ref[...].T
        dd_cols = [jnp.sum(do_ref[:, g * 64:(g + 1) * 64].astype(F32) * o_ref[:, g * 64:(g + 1) * 64].astype(F32),
                           axis=1, keepdims=True) for g in range(GROUP)]
        dd_rows = _head_columns(dd_cols).T

        def fn(k, v, mask, use_sink, spans):
            k_t = k.T
            dq_t, dsinks = [], []
            dk = jnp.zeros(k.shape, F32)
            dv = jnp.zeros(v.shape, F32)
            def products(g):
                s = lax.dot_general(k, q_ref[:, g * 64:(g + 1) * 64], (((1,), (1,)), ((), ())),
                                    preferred_element_type=F32)
                dp = lax.dot_general(v, do_ref[:, g * 64:(g + 1) * 64], (((1,), (1,)), ((), ())),
                                     preferred_element_type=F32)
                return (s if mask is None else jnp.where(mask, s, NEG_BIG)), dp

            nxt = products(0)
            for g in range(GROUP):
                q = q_ref[:, g * 64:(g + 1) * 64]
                do_g = do_ref[:, g * 64:(g + 1) * 64]
                lse_g, dd_g = lse_rows[g:g + 1, :], dd_rows[g:g + 1, :]
                s, dp = nxt
                if g + 1 < GROUP:
                    nxt = products(g + 1)
                pb = jnp.exp2(s - lse_g).astype(BF16)
                ds = (pb.astype(F32) * (dp - dd_g)).astype(BF16)
                dk = dk + jnp.dot(ds, q, preferred_element_type=F32)
                dv = dv + jnp.dot(pb, do_g, preferred_element_type=F32)
                dq_t.append(jnp.dot(k_t, ds, preferred_element_type=F32))
                if use_sink:
                    p_sink = jnp.exp2(sink_ref[jnp.maximum(hg - 2, 0) * GROUP + g] * LOG2E - lse_g)
                    dsinks.append(jnp.broadcast_to(-jnp.sum(p_sink * dd_g, axis=1, keepdims=True), (1, 128)))
            dq_ref[...] = jnp.concatenate(dq_t, axis=0).T
            dkv = jnp.concatenate([dk * (1.0 / LOG2E), dv], axis=1)
            off = 0
            for start, size in spans:
                dkv_ref[pl.ds(start, size), :] += dkv[off:off + size]
                off += size
            if use_sink:
                dsk_ref[0:GROUP, :] += jnp.concatenate(dsinks, axis=0)

        _attn_case(t, hg, kv_ref, c_rows, t_rows, fn, keys_first=True)

    outs, couts = _call(
        body, name=name, grid=(bl, N_HG, t_rows // TM), in_specs=[q_spec, kv_spec, q_spec, q_spec, lse_spec],
        out_specs=[q_spec, kv_spec, ds_spec],
        out_shape=[jax.ShapeDtypeStruct(q_all.shape, F32), jax.ShapeDtypeStruct(kv_all.shape, F32),
                   jax.ShapeDtypeStruct((bl, N_HG, 8, 128), F32)],
        args=[sink8, q_all, kv_all, do, o, lse], prefetch=1, sem=("parallel", "parallel", "arbitrary"), comm=comm)
    return outs if comm is None else (outs, couts)


def _silu(x):
    return x * jax.nn.sigmoid(x)


def _ada_fwd(c_rows, w_ada, b_cols, name, comm=None):
    nl, d, w = w_ada.shape
    r = c_rows.shape[0]

    def body(c_ref, w_ref, b_ref, o_ref):
        s = _silu(c_ref[...]).astype(BF16)
        o_ref[...] = jnp.dot(s, w_ref[...].astype(BF16), preferred_element_type=F32) + b_ref[...]

    outs, couts = _call(
        body, name=name, grid=(nl,),
        in_specs=[pl.BlockSpec((r, d), lambda l: (0, 0)), pl.BlockSpec((None, d, w), lambda l: (l, 0, 0)),
                  pl.BlockSpec((None, 1, w), lambda l: (l, 0, 0))],
        out_specs=pl.BlockSpec((None, r, w), lambda l: (l, 0, 0)),
        out_shape=jax.ShapeDtypeStruct((nl, r, w), F32), args=[c_rows, w_ada, b_cols], sem=("parallel",), comm=comm)
    return outs if comm is None else (outs, couts)


def _ada_bwd(c_rows, c_ctx, dmod, w_ada, name):
    nl, d, w = w_ada.shape
    r = c_rows.shape[0]

    def body(c_ref, cc_ref, g_ref, w_ref, dw_ref, dc_ref):
        l = pl.program_id(0)
        s = _silu(c_ref[...]).astype(BF16)
        gm = g_ref[...].astype(BF16)
        dw_ref[...] = lax.dot_general(s, gm, (((0,), (0,)), ((), ())), preferred_element_type=F32)
        ds = lax.dot_general(gm, w_ref[...].astype(BF16), (((1,), (1,)), ((), ())), preferred_element_type=F32)
        rows = lax.broadcasted_iota(jnp.int32, ds.shape, 0)
        dsc = jnp.sum(jnp.where(rows % ADA_ROWS == 2, ds, 0.0), axis=0, keepdims=True)
        x = cc_ref[...]
        sg = jax.nn.sigmoid(x)
        dcc = dsc * (sg * (1.0 + x * (1.0 - sg)))
        out = jnp.concatenate([dcc, jnp.zeros((7, d), F32)], axis=0)

        @pl.when(l == 0)
        def _():
            dc_ref[...] = out

        @pl.when(l != 0)
        def _():
            dc_ref[...] += out

    return pl.pallas_call(
        body, name=name, grid=(nl,),
        in_specs=[pl.BlockSpec((r, d), lambda l: (0, 0)), pl.BlockSpec((1, d), lambda l: (0, 0)),
                  pl.BlockSpec((None, r, w), lambda l: (l, 0, 0)), pl.BlockSpec((None, d, w), lambda l: (l, 0, 0))],
        out_specs=[pl.BlockSpec((None, d, w), lambda l: (l, 0, 0)), pl.BlockSpec((8, d), lambda l: (0, 0))],
        out_shape=[jax.ShapeDtypeStruct((nl, d, w), F32), jax.ShapeDtypeStruct((8, d), F32)],
        compiler_params=_params(("arbitrary",)),
    )(c_rows, c_ctx, dmod, w_ada)


def _adam_math(w, g, m, v):
    m = ADAM_B1 * m + (1.0 - ADAM_B1) * g
    v = ADAM_B2 * v + (1.0 - ADAM_B2) * (g * g)
    m_hat = m / (1.0 - ADAM_B1 ** ADAM_STEP)
    v_hat = v / (1.0 - ADAM_B2 ** ADAM_STEP)
    delta = -ADAM_LR * (m_hat / (jnp.sqrt(v_hat) + ADAM_EPS) + ADAM_WD * w)
    return delta, m, v


def _adamw(w, m, v, g_own, g_recv, name, rows=256):
    nl, r, c = w.shape
    tr = min(rows, r)
    spec = pl.BlockSpec((None, tr, c), lambda l, i: (l, i, 0))
    per_layer = isinstance(g_own, (list, tuple))
    own = list(g_own) if per_layer else [g_own]
    recv = [] if g_recv is None else list(g_recv)
    n_i = r // tr

    def rows_of(li):
        return lambda l, i: jnp.where(l == li, i, jnp.where(l > li, n_i - 1, 0))

    in_specs = [spec] * 3
    in_specs += [pl.BlockSpec((tr, c), lambda l, i, f=rows_of(li): (f(l, i), 0)) if per_layer else spec
                 for li in range(len(own))]
    in_specs += [pl.BlockSpec((N_DEV, tr, c), lambda l, i, f=rows_of(li): (0, f(l, i), 0)) for li in range(len(recv))]

    def body(*refs):
        w_ref, m_ref, v_ref = refs[:3]
        own_refs, recv_refs = refs[3:3 + len(own)], refs[3 + len(own):3 + len(own) + len(recv)]
        go_ref, d_ref, mo_ref, vo_ref = refs[-4:]

        def update(li):
            g = own_refs[li][...].astype(F32)
            if recv:
                for k in range(1, N_DEV):
                    g = g + recv_refs[li][k].astype(F32)
            delta, m_, v_ = _adam_math(w_ref[...], g, m_ref[...], v_ref[...])
            go_ref[...] = g
            d_ref[...] = delta
            mo_ref[...] = m_
            vo_ref[...] = v_

        if per_layer:
            for li in range(nl):
                pl.when(pl.program_id(0) == li)(functools.partial(update, li))
        else:
            update(0)

    return pl.pallas_call(
        body, name=name, grid=(nl, n_i), in_specs=in_specs, out_specs=[spec] * 4,
        out_shape=[jax.ShapeDtypeStruct(w.shape, F32)] * 4, compiler_params=_params(("parallel", "parallel")),
    )(w, m, v, *own, *recv)


def _small_adamw(w, m, v, g_all, name):
    def body(w_ref, m_ref, v_ref, g_ref, go_ref, d_ref, mo_ref, vo_ref):
        g = g_ref[0]
        for k in range(1, N_DEV):
            g = g + g_ref[k]
        delta, m_, v_ = _adam_math(w_ref[...], g, m_ref[...], v_ref[...])
        go_ref[...] = g
        d_ref[...] = delta
        mo_ref[...] = m_
        vo_ref[...] = v_

    return pl.pallas_call(
        body, name=name, out_shape=[jax.ShapeDtypeStruct(w.shape, F32)] * 4, compiler_params=_params(),
    )(w, m, v, g_all)


SMALL = ("c_ctx", "b_ada", "g_pre_mix", "g_post_mix", "g_pre_mlp", "g_post_mlp", "q_norm", "k_norm", "sink", "loss")


def _pack_small(parts):
    flat = jnp.concatenate([parts[n].reshape(-1) for n in SMALL])
    rows = -(-flat.shape[0] // 1024) * 8
    return jnp.pad(flat, (0, rows * 128 - flat.shape[0])).reshape(rows, 128)


def _unpack_small(packed, like):
    flat = packed.reshape(-1)
    out, off = {}, 0
    for n in SMALL:
        size = int(np.prod(like[n].shape))
        out[n] = flat[off:off + size].reshape(like[n].shape)
        off += size
    return out


def kernel(x, c, ctx, c_ctx, w_ada, b_ada, g_pre_mix, g_post_mix, g_pre_mlp, g_post_mlp, w_in, q_norm, k_norm, sink, w_out, w_up, w_down, loss_target, m_c_ctx, m_w_ada, m_b_ada, m_g_pre_mix, m_g_post_mix, m_g_pre_mlp, m_g_post_mlp, m_w_in, m_q_norm, m_k_norm, m_sink, m_w_out, m_w_up, m_w_down, v_c_ctx, v_w_ada, v_b_ada, v_g_pre_mix, v_g_post_mix, v_g_pre_mlp, v_g_post_mlp, v_w_in, v_q_norm, v_k_norm, v_sink, v_w_out, v_w_up, v_w_down):
    bl, s_rows, d = x.shape
    c_rows = ctx.shape[1]
    assert c_rows == TM and s_rows % TM == 0 and bl == 2
    t_rows = c_rows + s_rows
    nt = t_rows // TM
    m_rows = bl * t_rows
    nl = w_in.shape[0]
    ada_w = w_ada.shape[2]
    d_ff = w_up.shape[2] * N_DEV
    me = _my_index()

    shard = lambda w_, l: w_[l].astype(BF16)
    c_pad = jnp.concatenate([c, c_ctx[None, :], jnp.zeros((ADA_ROWS - bl - 1, d), F32)], axis=0)
    gathered = {0: {}}
    h, (c_all, gathered[0]["w_in"]) = _token_stream(
        ctx, x, nt, "token_stream", comm=_Comm([(c_pad, GATHER), (shard(w_in, 0), GATHER)]))
    c_all = c_all.reshape(N_DEV * ADA_ROWS, d)

    def layer_weights(l):
        g_ = gathered[l]
        w_out_f = g_["w_out"].reshape(-1, d)
        w_down_f = g_["w_down"].reshape(d_ff, d)
        return dict(
            w_in_f=g_["w_in"].transpose(1, 0, 2).reshape(d, PROJ_W), w_out_f=w_out_f,
            w_up_s=g_["w_up"], w_up_t=g_["w_up"].transpose(0, 2, 1).reshape(d_ff, d), w_down_f=w_down_f)

    big = dict(tm=2304, tn=512)
    deep = dict(tm=1536, tn=512, tk=d_ff)
    wide = dict(tm=1024, tn=512, tk=m_rows)

    b_cols = lax.dynamic_slice(b_ada, (0, me * ada_w), (nl, ada_w))[:, None, :]
    mod_cols = _ada_fwd(c_all, w_ada, b_cols, "ada_fwd")
    mod_slots = mod_cols.reshape(nl, N_DEV, ADA_ROWS, ada_w).transpose(1, 0, 2, 3)
    mod_g, = _comm_only(_Comm([(mod_slots, TO_OWNER)]), "exchange_mod")
    mine = mod_g.transpose(1, 2, 0, 3).reshape(nl, ADA_ROWS, N_MOD, d)
    pad = jnp.zeros((bl, 2, MOD_ROWS - N_MOD, d), F32)
    modsel = [jnp.concatenate([jnp.stack([jnp.broadcast_to(mine[l, bl], (bl, N_MOD, d)), mine[l, :bl]], axis=1), pad],
                              axis=2) for l in range(nl)]

    cos, sin = _rope_tables(t_rows, c_rows)
    emat = _head_mean_matrix()
    row = lambda a: a[None, :]
    qn = [jnp.tile(q_norm[l], 8)[None, :] for l in range(nl)]
    kn = [jnp.tile(k_norm[l], 2)[None, :] for l in range(nl)]

    target = loss_target.reshape(bl * s_rows, d)
    saved = []
    weights_of = {}
    (u, u_t), (gathered[0]["w_out"],) = _norm_mod_fwd(
        h, row(g_pre_mix[0]), modsel[0], 0, 1, nt, "mix_mod_fwd0", comm=_Comm([(shard(w_out, 0), GATHER)]))
    for l in range(nl):
        w_in_f = gathered[l]["w_in"].transpose(1, 0, 2).reshape(d, PROJ_W)
        proj = _mm(u, w_in_f, name=f"mm_in{l}", tk=d, **big)
        q_all, kv_all = _prep_fwd(proj, qn[l], kn[l], cos, sin, emat, nt, f"prep_fwd{l}")
        (o, o_t, lse), (w_up_g, w_down_g) = _attn_fwd(
            q_all.reshape(bl, t_rows, 1024), kv_all.reshape(bl, t_rows, 512), sink[l], c_rows, f"attn_fwd{l}",
            comm=_Comm([(shard(w_up, l), GATHER), (shard(w_down, l), GATHER)]))
        o = o.reshape(m_rows, 1024)
        gathered[l].update(w_up=w_up_g, w_down=w_down_g)
        wl = weights_of[l] = layer_weights(l)
        mix = _mm(o, wl["w_out_f"], name=f"mm_out{l}", tk=1024, **big)
        h_mid, v_in, v_t = _gate_mod_fwd(h, mix, row(g_post_mix[l]), modsel[l], 2, row(g_pre_mlp[l]), modsel[l], 3, 4,
                                         nt, f"mix_gate_mlp_mod_fwd{l}")
        more = l + 1 < nl
        res_up = _mm(v_in, wl["w_up_s"], name=f"mm_up{l}", b_mode="nn_slots", epilogue="relu2", tk=d,
                     comm=_Comm([(shard(w_in, l + 1), GATHER)]) if more else None, **big)
        (r_act, r_t), nxt_in = res_up if more else (res_up, None)
        res_down = _mm(r_act, wl["w_down_f"], name=f"mm_down{l}",
                       comm=_Comm([(shard(w_out, l + 1), GATHER)]) if more else None, **deep)
        y, nxt_out = res_down if more else (res_down, None)
        saved.append((h, u_t, proj, q_all, kv_all, o, o_t, lse, mix, h_mid, v_t, r_act, r_t, y))
        if more:
            gathered[l + 1] = dict(w_in=nxt_in[0], w_out=nxt_out[0])
            h, u, u_t = _gate_mod_fwd(h_mid, y, row(g_post_mlp[l]), modsel[l], 5, row(g_pre_mix[l + 1]), modsel[l + 1],
                                      0, 1, nt, f"mlp_gate_mix_mod_fwd{l}")

    dh, dy, dvec_g2, ss = _gate_loss_bwd(h_mid, y, row(g_post_mlp[nl - 1]), modsel[nl - 1], 5, target, nt, "gate_loss_bwd")
    small_g = {n: [None] * nl for n in SMALL if n not in ("c_ctx", "b_ada", "loss")}
    dvecs = {l: {} for l in range(nl)}
    dvecs[nl - 1]["g2"] = dvec_g2
    slots = {n: [None] * nl for n in ("w_in", "w_out", "w_up", "w_down")}
    recvd = {n: [None] * nl for n in slots}
    send = lambda n, l_: (slots[n][l_], TO_OWNER_XOR)
    for l in reversed(range(nl)):
        h_in, u_t, proj, q_all, kv_all, o, o_t, lse, mix, h_mid, v_t, r_act, r_t, y = saved[l]
        wl = weights_of[l]
        later = l + 1 < nl
        res = _mm(dy, wl["w_down_f"], name=f"mm_da{l}", b_mode="nt", epilogue="relu2_bwd", extra=r_act, out_dtype=BF16, tk=d,
                  comm=_Comm([send("w_out", l + 1)]) if later else None, **big)
        da = res[0] if later else res
        if later:
            recvd["w_out"][l + 1] = res[1][0]
        res = _mm(r_t, dy, name=f"mm_dw_down{l}", out_dtype=BF16,
                  comm=_Comm([send("w_in", l + 1)]) if later else None, **wide)
        dw_down = res[0] if later else res
        if later:
            recvd["w_in"][l + 1] = res[1][0]
        dw_up = _mm(v_t, da, name=f"mm_dw_up{l}", out_mode="slots", out_dtype=BF16, **wide)
        slots["w_down"][l] = dw_down.reshape(N_DEV, -1, d)
        slots["w_up"][l] = dw_up
        dv = _mm(da, wl["w_up_t"], name=f"mm_dv{l}", **deep)
        dh, dvecs[l]["m2"], dmix, dvecs[l]["g1"] = _mod_gate_bwd(
            h_mid, row(g_pre_mlp[l]), modsel[l], 3, 4, dv, dh, mix, row(g_post_mix[l]), modsel[l], 2, nt,
            f"mlp_mod_mix_gate_bwd{l}")
        do = _mm(dmix, wl["w_out_f"], name=f"mm_do{l}", b_mode="nt", out_dtype=BF16, tk=d, **big)
        dw_out = _mm(o_t, dmix, name=f"mm_dw_out{l}", out_dtype=BF16, **wide)
        slots["w_out"][l] = dw_out.reshape(N_DEV, -1, d)
        (dq, dkv, dsk), (recvd["w_down"][l], recvd["w_up"][l]) = _attn_bwd(
            q_all.reshape(bl, t_rows, 1024), kv_all.reshape(bl, t_rows, 512), do.reshape(bl, t_rows, 1024),
            o.reshape(bl, t_rows, 1024), lse, sink[l], c_rows, f"attn_bwd{l}",
            comm=_Comm([send("w_down", l), send("w_up", l)]))
        last = l == 0
        res = _prep_bwd(proj, dq.reshape(m_rows, 1024), dkv.reshape(m_rows, 512), qn[l], kn[l], cos, sin, emat, nt,
                        f"prep_bwd{l}", comm=_Comm([send("w_out", l)]) if last else None)
        dproj, dqn, dkn = res[0] if last else res
        if last:
            recvd["w_out"][l] = res[1][0]
        dw_in = _mm(u_t, dproj, name=f"mm_dw_in{l}", out_dtype=BF16, **wide)
        slots["w_in"][l] = dw_in.reshape(d, N_DEV, PROJ_W // N_DEV).transpose(1, 0, 2)
        du = _mm(dproj, wl["w_in_f"], name=f"mm_du{l}", b_mode="nt", tk=PROJ_W, **big)
        if last:
            (dh, dvecs[l]["m1"]), (recvd["w_in"][l],) = _norm_mod_bwd(
                h_in, row(g_pre_mix[l]), modsel[l], du, dh, 0, 1, nt, f"mix_mod_bwd{l}", latent_only=True,
                comm=_Comm([send("w_in", l)]))
        else:
            dh, dvecs[l]["m1"], dy, dvecs[l - 1]["g2"] = _mod_gate_bwd(
                h_in, row(g_pre_mix[l]), modsel[l], 0, 1, du, dh, saved[l - 1][-1], row(g_post_mlp[l - 1]),
                modsel[l - 1], 5, nt, f"mix_mod_mlp_gate_bwd{l}")
        small_g["q_norm"][l] = jnp.sum(dqn[0].reshape(8, HEAD_DIM), axis=0)
        small_g["k_norm"][l] = jnp.sum(dkn[0].reshape(2, HEAD_DIM), axis=0)
        small_g["sink"][l] = jnp.sum(dsk[:, 2:, :GROUP, 0], axis=0).reshape(-1)

    dmod_rows = []
    for l in range(nl):
        m1, g1, m2, g2 = (dvecs[l][n] for n in ("m1", "g1", "m2", "g2"))
        small_g["g_pre_mix"][l] = jnp.sum(m1[:, :, 2], axis=(0, 1))
        small_g["g_post_mix"][l] = jnp.sum(g1[:, :, 1], axis=(0, 1))
        small_g["g_pre_mlp"][l] = jnp.sum(m2[:, :, 2], axis=(0, 1))
        small_g["g_post_mlp"][l] = jnp.sum(g2[:, :, 1], axis=(0, 1))
        dms = jnp.stack([m1[:, :, 0], m1[:, :, 1], g1[:, :, 0], m2[:, :, 0], m2[:, :, 1], g2[:, :, 0]], axis=2)
        rows = jnp.concatenate([dms[:, 1], jnp.sum(dms[:, 0], axis=0)[None]], axis=0)
        dmod_rows.append(jnp.pad(rows.reshape(bl + 1, N_MOD * d), ((0, ADA_ROWS - bl - 1), (0, 0))))
    grad_x = dh.reshape(bl, s_rows, d)

    dmod_slots = jnp.stack(dmod_rows).reshape(nl, ADA_ROWS, N_DEV, ada_w).transpose(2, 0, 1, 3)
    dmod_g, = _comm_only(_Comm([(dmod_slots, TO_OWNER)]), "exchange_dmod")
    dmod_mine = dmod_g.transpose(1, 0, 2, 3).reshape(nl, N_DEV * ADA_ROWS, ada_w)
    dw_ada, dcc = _ada_bwd(c_all, c_ctx[None, :], dmod_mine, w_ada, "ada_bwd")

    parts = {n: jnp.stack(small_g[n]) for n in small_g}
    parts["c_ctx"] = dcc[0]
    parts["loss"] = ss[0, 0:1]
    no_loss = jnp.zeros((1,), F32)
    parts["b_ada"] = jnp.stack([jnp.sum(r_[: bl + 1], axis=0) for r_ in dmod_rows])
    weights = dict(c_ctx=c_ctx, b_ada=b_ada, g_pre_mix=g_pre_mix, g_post_mix=g_post_mix, g_pre_mlp=g_pre_mlp,
                   g_post_mlp=g_post_mlp, q_norm=q_norm, k_norm=k_norm, sink=sink, loss=no_loss)
    moms = dict(c_ctx=m_c_ctx, b_ada=m_b_ada, g_pre_mix=m_g_pre_mix, g_post_mix=m_g_post_mix, g_pre_mlp=m_g_pre_mlp,
                g_post_mlp=m_g_post_mlp, q_norm=m_q_norm, k_norm=m_k_norm, sink=m_sink, loss=no_loss)
    vels = dict(c_ctx=v_c_ctx, b_ada=v_b_ada, g_pre_mix=v_g_pre_mix, g_post_mix=v_g_post_mix, g_pre_mlp=v_g_pre_mlp,
                g_post_mlp=v_g_post_mlp, q_norm=v_q_norm, k_norm=v_k_norm, sink=v_sink, loss=no_loss)
    small_all, = _comm_only(_Comm([(_pack_small(parts), GATHER)]), "gather_small")
    s_out = _small_adamw(_pack_small(weights), _pack_small(moms), _pack_small(vels), small_all, "adamw_small")
    s_g, s_d, s_m, s_v = [_unpack_small(a, weights) for a in s_out]
    loss = 0.5 * s_g["loss"][0] / d

    res = {}
    for n, w_, m_, v_ in (("w_in", w_in, m_w_in, v_w_in), ("w_out", w_out, m_w_out, v_w_out),
                          ("w_up", w_up, m_w_up, v_w_up), ("w_down", w_down, m_w_down, v_w_down)):
        own = [lax.dynamic_index_in_dim(slots[n][l], me, axis=0, keepdims=False) for l in range(nl)]
        res[n] = _adamw(w_, m_, v_, own, recvd[n], f"adamw_{n}")
    res["w_ada"] = _adamw(w_ada, m_w_ada, v_w_ada, dw_ada, None, "adamw_w_ada")

    order = ("c_ctx", "w_ada", "b_ada", "g_pre_mix", "g_post_mix", "g_pre_mlp", "g_post_mlp", "w_in", "q_norm",
             "k_norm", "sink", "w_out", "w_up", "w_down")
    outs = [loss, grad_x]
    for i, small in enumerate((s_g, s_d, s_m, s_v)):
        outs += [small[n] if n in small else res[n][i] for n in order]
    return tuple(outs)
```

```python
import functools

import jax
import jax.numpy as jnp
import numpy as np
from jax import lax
from jax.experimental import pallas as pl
from jax.experimental.pallas import tpu as pltpu

F32 = jnp.float32
BF16 = jnp.bfloat16

HEAD_DIM = 64
GROUP = 4
N_HG = 4
Q_WIDTH = GROUP * HEAD_DIM
WINDOW = 128
GRID_W = 64
ROPE_THETA = 10000.0
EPS = 1e-6
NEG_BIG = -1e30
N_MOD = 6
MOD_ROWS = 8
TM = 256
N_DEV = 8
ADA_ROWS = 8
VMEM_LIMIT = 56 * 1024 * 1024

ADAM_LR = 0.001
ADAM_B1 = 0.9
ADAM_B2 = 0.999
ADAM_EPS = 1e-08
ADAM_WD = 0.01
ADAM_STEP = 10


def _params(sem=None):
    kw = dict(vmem_limit_bytes=VMEM_LIMIT)
    if sem is not None:
        kw["dimension_semantics"] = sem
    return pltpu.CompilerParams(**kw)


def _my_index():
    return 4 * lax.axis_index("x") + 2 * lax.axis_index("y") + lax.axis_index("c")


def _peer(k):
    x, y, c = lax.axis_index("x"), lax.axis_index("y"), lax.axis_index("c")
    kx, ky, kc = (k >> 2) & 1, (k >> 1) & 1, k & 1
    px = (1 - x) if kx else x
    py = (1 - y) if ky else y
    pc = (1 - c) if kc else c
    return (px, py, pc), 4 * px + 2 * py + pc


GATHER, GATHER_VIA_SIBLING, TO_OWNER, TO_OWNER_XOR = "gather", "gather_via_sibling", "to_owner", "to_owner_xor"


class _Comm:
    def __init__(self, items):
        self.items = list(items)
        self.arrays = [a for a, _ in self.items]

    def out_shapes(self):
        return [jax.ShapeDtypeStruct(((N_DEV,) + a.shape) if kind in (GATHER, GATHER_VIA_SIBLING) else a.shape, a.dtype)
                for a, kind in self.items]

    def sem_shapes(self):
        n = len(self.items) * N_DEV
        return [pltpu.SemaphoreType.DMA((n,)), pltpu.SemaphoreType.DMA((n,))]

    def _two_level(self, x_ref, o_ref, send_sems, recv_sems, base):
        x, y, c = lax.axis_index("x"), lax.axis_index("y"), lax.axis_index("c")
        me, sibling = (x, y, c), (x, y, 1 - c)
        chips = [(1 - x, y), (x, 1 - y), (1 - x, 1 - y)]
        block = lambda p: o_ref.at[4 * p[0] + 2 * p[1] + p[2]]

        def copy(k, owner, to, src=None):
            return pltpu.make_async_remote_copy(
                src_ref=block(owner) if src is None else src, dst_ref=block(owner), send_sem=send_sems.at[base + k],
                recv_sem=recv_sems.at[base + k], device_id=to, device_id_type=pl.DeviceIdType.MESH)

        first = [copy(1, me, sibling, src=x_ref)] + [copy(2 + j, me, (*chip, c), src=x_ref) for j, chip in enumerate(chips)]
        relay = [(copy(2 + j, (*chip, c), me), copy(5 + j, (*chip, c), sibling)) for j, chip in enumerate(chips)]
        last = [copy(1, sibling, me)] + [copy(5 + j, (*chip, 1 - c), me) for j, chip in enumerate(chips)]
        return first, relay, last

    def _copies(self, in_refs, out_refs, send_sems, recv_sems):
        me = _my_index()
        local, remote = [], []
        for i, ((_, kind), x_ref, o_ref) in enumerate(zip(self.items, in_refs, out_refs)):
            base = i * N_DEV
            if kind == GATHER_VIA_SIBLING:
                local.append(pltpu.make_async_copy(x_ref, o_ref.at[me], send_sems.at[base]))
                continue
            own_src = x_ref if kind == GATHER else x_ref.at[me]
            own_dst = o_ref.at[0] if kind == TO_OWNER_XOR else o_ref.at[me]
            local.append(pltpu.make_async_copy(own_src, own_dst, send_sems.at[base]))
            for k in range(1, N_DEV):
                peer, pidx = _peer(k)
                remote.append(pltpu.make_async_remote_copy(
                    src_ref=x_ref if kind == GATHER else x_ref.at[pidx],
                    dst_ref=o_ref.at[k] if kind == TO_OWNER_XOR else o_ref.at[me],
                    send_sem=send_sems.at[base + k], recv_sem=recv_sems.at[base + k],
                    device_id=peer, device_id_type=pl.DeviceIdType.MESH))
        return local, remote

    def start(self, in_refs, out_refs, send_sems, recv_sems):
        local, remote = self._copies(in_refs, out_refs, send_sems, recv_sems)
        for cp in local + remote:
            cp.start()
        for i, ((_, kind), x_ref, o_ref) in enumerate(zip(self.items, in_refs, out_refs)):
            if kind == GATHER_VIA_SIBLING:
                for cp in self._two_level(x_ref, o_ref, send_sems, recv_sems, i * N_DEV)[0]:
                    cp.start()

    def wait(self, in_refs, out_refs, send_sems, recv_sems):
        local, remote = self._copies(in_refs, out_refs, send_sems, recv_sems)
        for cp in remote:
            cp.wait_recv()
        for cp in remote:
            cp.wait_send()
        for i, ((_, kind), x_ref, o_ref) in enumerate(zip(self.items, in_refs, out_refs)):
            if kind == GATHER_VIA_SIBLING:
                first, relay, last = self._two_level(x_ref, o_ref, send_sems, recv_sems, i * N_DEV)
                for arrival, onward in relay:
                    arrival.wait_recv()
                    onward.start()
                for cp in last:
                    cp.wait_recv()
                for cp in first + [onward for _, onward in relay]:
                    cp.wait_send()
        for cp in local:
            cp.wait()


def _call(body, *, name, grid, in_specs, out_specs, out_shape, args, scratch_shapes=(), prefetch=0, sem=None,
          comm=None):
    single = not isinstance(out_shape, (list, tuple))
    out_shape = [out_shape] if single else list(out_shape)
    out_specs = [out_specs] if single else list(out_specs)
    in_specs, scratch_shapes, args = list(in_specs), list(scratch_shapes), list(args)
    n_in, n_out = len(in_specs), len(out_shape)
    if comm is not None:
        nc = len(comm.arrays)
        hbm = pl.BlockSpec(memory_space=pl.ANY)
        inner = body

        def body(*refs):
            pre, r = refs[:prefetch], refs[prefetch:]
            ins, cin = r[:n_in], r[n_in:n_in + nc]
            outs, cout = r[n_in + nc:n_in + nc + n_out], r[n_in + nc + n_out:n_in + 2 * nc + n_out]
            scr, sems = r[n_in + 2 * nc + n_out:len(r) - 2], r[len(r) - 2:]
            ids = [pl.program_id(i) for i in range(len(grid))]

            def when(flags, fn):
                if flags:
                    pl.when(functools.reduce(jnp.logical_and, flags))(fn)
                else:
                    fn()

            when([i == 0 for i in ids], lambda: comm.start(cin, cout, *sems))
            inner(*pre, *ins, *outs, *scr)
            when([i == n - 1 for i, n in zip(ids, grid)], lambda: comm.wait(cin, cout, *sems))

        in_specs += [hbm] * nc
        out_specs += [hbm] * nc
        out_shape += comm.out_shapes()
        scratch_shapes += comm.sem_shapes()
        args += comm.arrays
        sem = ("arbitrary",) * len(grid)
    kw = dict(name=name, out_shape=out_shape, compiler_params=_params(sem if grid else None))
    if prefetch:
        kw["grid_spec"] = pltpu.PrefetchScalarGridSpec(
            num_scalar_prefetch=prefetch, grid=grid, in_specs=in_specs, out_specs=out_specs,
            scratch_shapes=scratch_shapes)
    else:
        kw.update(in_specs=in_specs, out_specs=out_specs, scratch_shapes=scratch_shapes)
        if grid:
            kw["grid"] = grid
    res = list(pl.pallas_call(body, **kw)(*args))
    outs = res[:n_out]
    return (outs[0] if single else outs), res[n_out:]


def _comm_only(comm, name):
    return _call(lambda: None, name=name, grid=(), in_specs=[], out_specs=[], out_shape=[], args=[], comm=comm)[1]


def _mm(a, b, *, name, ta=False, b_mode="nn", out_mode="plain", out_dtype=F32, tm=512, tn=512, tk=512,
        epilogue=None, extra=None, comm=None):
    if ta:
        kdim, m = a.shape
    else:
        m, kdim = a.shape
    if b_mode == "nn":
        n = b.shape[1]
    elif b_mode == "nt":
        n = b.shape[0]
    elif b_mode == "nn_slots":
        n = b.shape[0] * b.shape[2]
        tn = b.shape[2]
    else:
        n = b.shape[1]
        tk = b.shape[2]
    if out_mode == "slots":
        tn = n // N_DEV
    tm, tn, tk = min(tm, m), min(tn, n), min(tk, kdim)
    assert m % tm == 0 and n % tn == 0 and kdim % tk == 0, (name, m, n, kdim, tm, tn, tk)
    nk = kdim // tk

    a_spec = pl.BlockSpec((tk, tm), lambda i, j, k: (k, i)) if ta else pl.BlockSpec((tm, tk), lambda i, j, k: (i, k))
    if b_mode == "nn":
        b_spec = pl.BlockSpec((tk, tn), lambda i, j, k: (k, j))
    elif b_mode == "nt":
        b_spec = pl.BlockSpec((tn, tk), lambda i, j, k: (j, k))
    elif b_mode == "nn_slots":
        b_spec = pl.BlockSpec((None, tk, tn), lambda i, j, k: (j, k, 0))
    else:
        b_spec = pl.BlockSpec((None, tn, tk), lambda i, j, k: (k, j, 0))
    tb = b_mode in ("nt", "nt_slots")
    if out_mode == "plain":
        o_shape, o_spec = (m, n), pl.BlockSpec((tm, tn), lambda i, j, k: (i, j))
    else:
        o_shape, o_spec = (N_DEV, m, tn), pl.BlockSpec((None, tm, tn), lambda i, j, k: (j, i, 0))
    dims = (((0 if ta else 1,), (1 if tb else 0,)), ((), ()))

    in_specs = [a_spec, b_spec]
    args = [a, b]
    if epilogue == "relu2_bwd":
        in_specs.append(pl.BlockSpec((tm, tn), lambda i, j, k: (i, j)))
        args.append(extra)
    if epilogue == "relu2":
        out_shape = [jax.ShapeDtypeStruct(o_shape, BF16), jax.ShapeDtypeStruct((n, m), BF16)]
        out_specs = [o_spec, pl.BlockSpec((tn, tm), lambda i, j, k: (j, i))]
    else:
        out_shape = jax.ShapeDtypeStruct(o_shape, out_dtype)
        out_specs = o_spec

    def finish(refs, acc):
        if epilogue == "relu2":
            r = jnp.maximum(acc, 0.0)
            r2 = (r * r).astype(BF16)
            refs[2][...] = r2
            refs[3][...] = r2.T
        elif epilogue == "relu2_bwd":
            refs[3][...] = (acc * (2.0 * jnp.sqrt(refs[2][...]).astype(F32))).astype(out_dtype)
        else:
            refs[2][...] = acc.astype(out_dtype)

    def body(*refs):
        part = lax.dot_general(refs[0][...], refs[1][...], dims, preferred_element_type=F32)
        if nk == 1:
            finish(refs, part)
            return
        acc_ref = refs[-1]
        k = pl.program_id(2)

        @pl.when(k == 0)
        def _():
            acc_ref[...] = part

        @pl.when(jnp.logical_and(k > 0, k < nk - 1))
        def _():
            acc_ref[...] += part

        @pl.when(k == nk - 1)
        def _():
            finish(refs, acc_ref[...] + part)

    outs, couts = _call(
        body, name=name, grid=(m // tm, n // tn, nk), in_specs=in_specs, out_specs=out_specs, out_shape=out_shape,
        args=args, scratch_shapes=[] if nk == 1 else [pltpu.VMEM((tm, tn), F32)],
        sem=("parallel", "parallel", "arbitrary"), comm=comm)
    return outs if comm is None else (outs, couts)


def _row_specs(d, nt):
    row = pl.BlockSpec((TM, d), lambda t: (t, 0))
    vec = pl.BlockSpec((1, d), lambda t: (0, 0))
    mod = pl.BlockSpec((None, None, MOD_ROWS, d), lambda t: (t // nt, jnp.minimum(t % nt, 1), 0, 0))
    return row, vec, mod


def _rms(x):
    r = lax.rsqrt(jnp.mean(x * x, axis=1, keepdims=True) + EPS)
    return r, x * r


def _token_stream(ctx, x, nt, name, comm=None):
    bl, c_rows, d = ctx.shape
    m = bl * nt * TM

    def body(ctx_ref, x_ref, h_ref):
        t = pl.program_id(0)

        @pl.when(t % nt == 0)
        def _():
            h_ref[...] = ctx_ref[...]

        @pl.when(t % nt != 0)
        def _():
            h_ref[...] = x_ref[...]

    outs, couts = _call(
        body, name=name, grid=(m // TM,),
        in_specs=[pl.BlockSpec((None, TM, d), lambda t: (t // nt, 0, 0)),
                  pl.BlockSpec((None, TM, d), lambda t: (t // nt, jnp.maximum(t % nt, 1) - 1, 0))],
        out_specs=pl.BlockSpec((TM, d), lambda t: (t, 0)), out_shape=jax.ShapeDtypeStruct((m, d), F32),
        args=[ctx, x], sem=("parallel",), comm=comm)
    return outs if comm is None else (outs, couts)


def _norm_mod_fwd(h, g, modsel, i_sh, i_sc, nt, name, comm=None):
    m, d = h.shape
    row, vec, mod = _row_specs(d, nt)

    def body(h_ref, g_ref, ms_ref, u_ref, ut_ref):
        _, xh = _rms(h_ref[...])
        ms = ms_ref[...]
        u = (xh * g_ref[...] * (1.0 + ms[i_sc:i_sc + 1]) + ms[i_sh:i_sh + 1]).astype(BF16)
        u_ref[...] = u
        ut_ref[...] = u.T

    outs, couts = _call(
        body, name=name, grid=(m // TM,), in_specs=[row, vec, mod],
        out_specs=[row, pl.BlockSpec((d, TM), lambda t: (0, t))],
        out_shape=[jax.ShapeDtypeStruct((m, d), BF16), jax.ShapeDtypeStruct((d, m), BF16)],
        args=[h, g, modsel], sem=("parallel",), comm=comm)
    return outs if comm is None else (outs, couts)


def _acc_rows(t, nt, dvec_ref, rows):
    first = (t % nt) <= 1

    @pl.when(first)
    def _():
        dvec_ref[...] = rows

    @pl.when(jnp.logical_not(first))
    def _():
        dvec_ref[...] += rows


def _norm_mod_bwd(h, g, modsel, du, dh_in, i_sh, i_sc, nt, name, comm=None, latent_only=False):
    m, d = h.shape
    row, vec, mod = _row_specs(d, nt)
    dh_rows, dh_spec = m, row
    if latent_only:
        dh_rows = m // nt * (nt - 1)
        dh_spec = pl.BlockSpec((TM, d), lambda t: ((t // nt) * (nt - 1) + jnp.maximum(t % nt, 1) - 1, 0))

    def body(h_ref, g_ref, ms_ref, du_ref, dhi_ref, dh_ref, dvec_ref):
        t = pl.program_id(0)
        r, xh = _rms(h_ref[...])
        g_ = g_ref[...]
        ms = ms_ref[...]
        du_ = du_ref[...]
        y = xh * g_
        dy = du_ * (1.0 + ms[i_sc:i_sc + 1])
        dxh = dy * g_
        dx = r * (dxh - xh * jnp.mean(dxh * xh, axis=1, keepdims=True))
        dh_ref[...] = dhi_ref[...] + dx
        rows = jnp.concatenate([
            jnp.sum(du_, axis=0, keepdims=True), jnp.sum(du_ * y, axis=0, keepdims=True),
            jnp.sum(dy * xh, axis=0, keepdims=True), jnp.zeros((MOD_ROWS - 3, d), F32)], axis=0)
        _acc_rows(t, nt, dvec_ref, rows)

    outs, couts = _call(
        body, name=name, grid=(m // TM,), in_specs=[row, vec, mod, row, row], out_specs=[dh_spec, mod],
        out_shape=[jax.ShapeDtypeStruct((dh_rows, d), F32), jax.ShapeDtypeStruct(modsel.shape, F32)],
        args=[h, g, modsel, du, dh_in], sem=("arbitrary",), comm=comm)
    return outs if comm is None else (outs, couts)


def _gate_rows(dh_, xh, g_, gate):
    dy = dh_ * gate
    return dy * g_, jnp.sum(dh_ * (xh * g_), axis=0, keepdims=True), jnp.sum(dy * xh, axis=0, keepdims=True)


def _gate_mod_fwd(h, z, g_post, ms_gate, i_g, g_pre, ms_mod, i_sh, i_sc, nt, name):
    m, d = h.shape
    row, vec, mod = _row_specs(d, nt)

    def body(h_ref, z_ref, gp_ref, msg_ref, gq_ref, msm_ref, hn_ref, u_ref, ut_ref):
        _, zh = _rms(z_ref[...])
        hn = h_ref[...] + msg_ref[...][i_g:i_g + 1] * (zh * gp_ref[...])
        hn_ref[...] = hn
        _, xh = _rms(hn)
        ms = msm_ref[...]
        u = (xh * gq_ref[...] * (1.0 + ms[i_sc:i_sc + 1]) + ms[i_sh:i_sh + 1]).astype(BF16)
        u_ref[...] = u
        ut_ref[...] = u.T

    return pl.pallas_call(
        body, name=name, grid=(m // TM,), in_specs=[row, row, vec, mod, vec, mod],
        out_specs=[row, row, pl.BlockSpec((d, TM), lambda t: (0, t))],
        out_shape=[jax.ShapeDtypeStruct((m, d), F32), jax.ShapeDtypeStruct((m, d), BF16),
                   jax.ShapeDtypeStruct((d, m), BF16)],
        compiler_params=_params(("parallel",)),
    )(h, z, g_post, ms_gate, g_pre, ms_mod)


def _mod_gate_bwd(h, g_pre, ms_mod, i_sh, i_sc, du, dh_in, z, g_post, ms_gate, i_g, nt, name):
    m, d = h.shape
    row, vec, mod = _row_specs(d, nt)

    def body(h_ref, gq_ref, msm_ref, du_ref, dhi_ref, z_ref, gp_ref, msg_ref, dh_ref, dvm_ref, dz_ref, dvg_ref):
        t = pl.program_id(0)
        r, xh = _rms(h_ref[...])
        gq = gq_ref[...]
        ms = msm_ref[...]
        du_ = du_ref[...]
        dy = du_ * (1.0 + ms[i_sc:i_sc + 1])
        dxh = dy * gq
        dh_ = dhi_ref[...] + r * (dxh - xh * jnp.mean(dxh * xh, axis=1, keepdims=True))
        dh_ref[...] = dh_
        _acc_rows(t, nt, dvm_ref, jnp.concatenate([
            jnp.sum(du_, axis=0, keepdims=True), jnp.sum(du_ * (xh * gq), axis=0, keepdims=True),
            jnp.sum(dy * xh, axis=0, keepdims=True), jnp.zeros((MOD_ROWS - 3, d), F32)], axis=0))
        rz, zh = _rms(z_ref[...])
        dzh, d_gate, d_gp = _gate_rows(dh_, zh, gp_ref[...], msg_ref[...][i_g:i_g + 1])
        dz_ref[...] = (rz * (dzh - zh * jnp.mean(dzh * zh, axis=1, keepdims=True))).astype(BF16)
        _acc_rows(t, nt, dvg_ref, jnp.concatenate([d_gate, d_gp, jnp.zeros((MOD_ROWS - 2, d), F32)], axis=0))

    return pl.pallas_call(
        body, name=name, grid=(m // TM,), in_specs=[row, vec, mod, row, row, row, vec, mod],
        out_specs=[row, mod, row, mod],
        out_shape=[jax.ShapeDtypeStruct((m, d), F32), jax.ShapeDtypeStruct(ms_mod.shape, F32),
                   jax.ShapeDtypeStruct((m, d), BF16), jax.ShapeDtypeStruct(ms_gate.shape, F32)],
        compiler_params=_params(("arbitrary",)),
    )(h, g_pre, ms_mod, du, dh_in, z, g_post, ms_gate)


def _gate_loss_bwd(h, z, g_post, modsel, i_g, target, nt, name):
    m, d = h.shape
    row, vec, mod = _row_specs(d, nt)
    ntl = nt - 1
    tgt = pl.BlockSpec((TM, d), lambda t: ((t // nt) * ntl + jnp.maximum(t % nt, 1) - 1, 0))
    acc = pl.BlockSpec((8, 128), lambda t: (0, 0))

    def body(h_ref, z_ref, gp_ref, ms_ref, t_ref, dh_ref, dz_ref, dvg_ref, ss_ref):
        t = pl.program_id(0)

        @pl.when(t == 0)
        def _():
            ss_ref[...] = jnp.zeros_like(ss_ref)

        latent = (t % nt != 0).astype(F32)
        rz, zh = _rms(z_ref[...])
        gp = gp_ref[...]
        gate = ms_ref[...][i_g:i_g + 1]
        e = h_ref[...] + gate * (zh * gp) - t_ref[...]
        ss_ref[...] += latent * jnp.sum(e * e)
        dh_ = e * (latent / d)
        dh_ref[...] = dh_
        dzh, d_gate, d_gp = _gate_rows(dh_, zh, gp, gate)
        dz_ref[...] = (rz * (dzh - zh * jnp.mean(dzh * zh, axis=1, keepdims=True))).astype(BF16)
        _acc_rows(t, nt, dvg_ref, jnp.concatenate([d_gate, d_gp, jnp.zeros((MOD_ROWS - 2, d), F32)], axis=0))

    return pl.pallas_call(
        body, name=name, grid=(m // TM,), in_specs=[row, row, vec, mod, tgt], out_specs=[row, row, mod, acc],
        out_shape=[jax.ShapeDtypeStruct((m, d), F32), jax.ShapeDtypeStruct((m, d), BF16),
                   jax.ShapeDtypeStruct(modsel.shape, F32), jax.ShapeDtypeStruct((8, 128), F32)],
        compiler_params=_params(("arbitrary",)),
    )(h, z, g_post, modsel, target)


QA, KA, VA, QB, KB, VB = 0, 512, 640, 768, 1280, 1408
PROJ_W = 1536
Q_SCALE = HEAD_DIM ** -0.5
LOG2E = 1.4426950408889634


def _swap16(x):
    lane = lax.broadcasted_iota(jnp.int32, x.shape, 1)
    n = x.shape[1]
    return jnp.where((lane % 32) < 16, pltpu.roll(x, n - 16, 1), pltpu.roll(x, 16, 1))


def _seg_mean(x, e):
    hi = x.astype(BF16)
    lo = (x - hi.astype(F32)).astype(BF16)
    w = e.shape[0]
    both = lambda a: jnp.dot(hi[:, a:a + w], e, preferred_element_type=F32) + jnp.dot(lo[:, a:a + w], e, preferred_element_type=F32)
    parts = [both(a) for a in range(0, x.shape[1], w)]
    return parts[0] if len(parts) == 1 else jnp.concatenate(parts, axis=1)


def _rope_tables(t_rows, c_rows):
    s = t_rows - c_rows
    row_ids = jnp.repeat(jnp.arange(s // GRID_W, dtype=jnp.int32), GRID_W).astype(F32)
    col_ids = jnp.tile(jnp.arange(GRID_W, dtype=jnp.int32), s // GRID_W).astype(F32)
    axis_dim = HEAD_DIM // 2
    inv = ROPE_THETA ** (-jnp.arange(0, axis_dim, 2, dtype=F32) / axis_dim)
    ang_r = row_ids[:, None] * inv[None, :]
    ang_c = col_ids[:, None] * inv[None, :]
    cos = jnp.concatenate([jnp.cos(ang_r), jnp.cos(ang_r), jnp.cos(ang_c), jnp.cos(ang_c)], axis=1)
    sin = jnp.concatenate([-jnp.sin(ang_r), jnp.sin(ang_r), -jnp.sin(ang_c), jnp.sin(ang_c)], axis=1)
    cos = jnp.concatenate([jnp.ones((c_rows, HEAD_DIM), F32), cos], axis=0)
    sin = jnp.concatenate([jnp.zeros((c_rows, HEAD_DIM), F32), sin], axis=0)
    return jnp.tile(cos, (1, 8)), jnp.tile(sin, (1, 8))


def _head_mean_matrix():
    i = np.arange(512)
    return jnp.asarray((i[:, None] // HEAD_DIM == i[None, :] // HEAD_DIM).astype(np.float32) / HEAD_DIM, dtype=BF16)


def _interleave_kv(k, v):
    return jnp.concatenate([k[:, :64], v[:, :64], k[:, 64:], v[:, 64:]], axis=1)


def _prep_fwd(proj, qn, kn, cos, sin, emat, nt, name):
    m = proj.shape[0]
    specs = [
        pl.BlockSpec((TM, PROJ_W), lambda t: (t, 0)),
        pl.BlockSpec((1, 512), lambda t: (0, 0)), pl.BlockSpec((1, 128), lambda t: (0, 0)),
        pl.BlockSpec((TM, 512), lambda t: (t % nt, 0)), pl.BlockSpec((TM, 512), lambda t: (t % nt, 0)),
        pl.BlockSpec((512, 512), lambda t: (0, 0)),
    ]

    def body(p_ref, qn_ref, kn_ref, cos_ref, sin_ref, e_ref, q_ref, kv_ref):
        cos_, sin_, e = cos_ref[...], sin_ref[...], e_ref[...]

        def rope(x, w):
            return x * cos_[:, :w] + _swap16(x) * sin_[:, :w]

        def norm(x, g, w):
            return x * lax.rsqrt(_seg_mean(x * x, e[:min(w, 256), :min(w, 256)]) + EPS) * g

        qa = rope(norm(p_ref[:, QA:QA + 512], qn_ref[...], 512), 512)
        qb = rope(p_ref[:, QB:QB + 512], 512)
        q_ref[:, 0:512] = (qa * (Q_SCALE * LOG2E)).astype(BF16)
        q_ref[:, 512:1024] = (qb * (Q_SCALE * LOG2E)).astype(BF16)
        ka = rope(norm(p_ref[:, KA:KA + 128], kn_ref[...], 128), 128)
        kb = rope(p_ref[:, KB:KB + 128], 128)
        kv_ref[:, 0:256] = _interleave_kv(ka, p_ref[:, VA:VA + 128]).astype(BF16)
        kv_ref[:, 256:512] = _interleave_kv(kb, p_ref[:, VB:VB + 128]).astype(BF16)

    return pl.pallas_call(
        body, name=name, grid=(m // TM,), in_specs=specs,
        out_specs=[pl.BlockSpec((TM, 1024), lambda t: (t, 0)), pl.BlockSpec((TM, 512), lambda t: (t, 0))],
        out_shape=[jax.ShapeDtypeStruct((m, 1024), BF16), jax.ShapeDtypeStruct((m, 512), BF16)],
        compiler_params=_params(("parallel",)),
    )(proj, qn, kn, cos, sin, emat)


def _prep_bwd(proj, dq, dkv, qn, kn, cos, sin, emat, nt, name, comm=None):
    m = proj.shape[0]
    specs = [
        pl.BlockSpec((TM, PROJ_W), lambda t: (t, 0)),
        pl.BlockSpec((TM, 1024), lambda t: (t, 0)), pl.BlockSpec((TM, 512), lambda t: (t, 0)),
        pl.BlockSpec((1, 512), lambda t: (0, 0)), pl.BlockSpec((1, 128), lambda t: (0, 0)),
        pl.BlockSpec((TM, 512), lambda t: (t % nt, 0)), pl.BlockSpec((TM, 512), lambda t: (t % nt, 0)),
        pl.BlockSpec((512, 512), lambda t: (0, 0)),
    ]

    def body(p_ref, dq_ref, dkv_ref, qn_ref, kn_ref, cos_ref, sin_ref, e_ref, dp_ref, dqn_ref, dkn_ref):
        t = pl.program_id(0)
        cos_, sin_, e = cos_ref[...], sin_ref[...], e_ref[...]

        @pl.when(t == 0)
        def _():
            dqn_ref[...] = jnp.zeros_like(dqn_ref)
            dkn_ref[...] = jnp.zeros_like(dkn_ref)

        def unrope(dy, w):
            return dy * cos_[:, :w] + _swap16(dy * sin_[:, :w])

        def norm_bwd(x, g, dy, w):
            r = lax.rsqrt(_seg_mean(x * x, e[:min(w, 256), :min(w, 256)]) + EPS)
            xh = x * r
            dxh = dy * g
            dx = r * (dxh - xh * _seg_mean(dxh * xh, e[:min(w, 256), :min(w, 256)]))
            return dx, jnp.sum(dy * xh, axis=0, keepdims=True)

        dqa, dgq = norm_bwd(p_ref[:, QA:QA + 512], qn_ref[...], unrope(dq_ref[:, 0:512] * Q_SCALE, 512), 512)
        dp_ref[:, QA:QA + 512] = dqa.astype(BF16)
        dp_ref[:, QB:QB + 512] = unrope(dq_ref[:, 512:1024] * Q_SCALE, 512).astype(BF16)
        da = dkv_ref[:, 0:256]
        db = dkv_ref[:, 256:512]
        dka = jnp.concatenate([da[:, 0:64], da[:, 128:192]], axis=1)
        dva = jnp.concatenate([da[:, 64:128], da[:, 192:256]], axis=1)
        dkb = jnp.concatenate([db[:, 0:64], db[:, 128:192]], axis=1)
        dvb = jnp.concatenate([db[:, 64:128], db[:, 192:256]], axis=1)
        dka, dgk = norm_bwd(p_ref[:, KA:KA + 128], kn_ref[...], unrope(dka, 128), 128)
        dp_ref[:, KA:KA + 128] = dka.astype(BF16)
        dp_ref[:, VA:VA + 128] = dva.astype(BF16)
        dp_ref[:, KB:KB + 128] = unrope(dkb, 128).astype(BF16)
        dp_ref[:, VB:VB + 128] = dvb.astype(BF16)
        dqn_ref[0:1, :] += dgq
        dkn_ref[0:1, :] += dgk

    outs, couts = _call(
        body, name=name, grid=(m // TM,), in_specs=specs,
        out_specs=[pl.BlockSpec((TM, PROJ_W), lambda t: (t, 0)), pl.BlockSpec((8, 512), lambda t: (0, 0)),
                   pl.BlockSpec((8, 128), lambda t: (0, 0))],
        out_shape=[jax.ShapeDtypeStruct((m, PROJ_W), BF16), jax.ShapeDtypeStruct((8, 512), F32),
                   jax.ShapeDtypeStruct((8, 128), F32)],
        args=[proj, dq, dkv, qn, kn, cos, sin, emat], sem=("arbitrary",), comm=comm)
    return outs if comm is None else (outs, couts)


def _attn_case(t, hg, kv_ref, c_rows, t_rows, fn, keys_first=False):
    wl = TM + 2 * WINDOW
    kvd = lambda a, n: kv_ref[pl.ds(a, n), :]
    dense = hg < 2
    ctx = t == 0

    @pl.when(jnp.logical_and(dense, ctx))
    def _():
        kv = kvd(0, c_rows)
        fn(kv[:, :64], kv[:, 64:], None, False, [(0, c_rows)])

    @pl.when(jnp.logical_and(dense, jnp.logical_not(ctx)))
    def _():
        kv = kvd(0, t_rows)
        fn(kv[:, :64], kv[:, 64:], None, False, [(0, t_rows)])

    @pl.when(jnp.logical_and(jnp.logical_not(dense), ctx))
    def _():
        kv = kvd(0, c_rows)
        fn(kv[:, :64], kv[:, 64:], None, True, [(0, c_rows)])

    @pl.when(jnp.logical_and(jnp.logical_not(dense), jnp.logical_not(ctx)))
    def _():
        start = pl.multiple_of(jnp.minimum(c_rows + (t - 1) * TM - WINDOW, t_rows - wl), 128)
        kv = jnp.concatenate([kvd(0, c_rows), kvd(start, wl)], axis=0)
        shape = (c_rows + wl, TM) if keys_first else (TM, c_rows + wl)
        q_i = lax.broadcasted_iota(jnp.int32, shape, 1 if keys_first else 0)
        k_i = lax.broadcasted_iota(jnp.int32, shape, 0 if keys_first else 1)
        qpos = (t - 1) * TM + q_i
        kpos = start - 2 * c_rows + k_i
        mask = jnp.logical_or(k_i < c_rows, jnp.logical_and(jnp.abs(kpos - qpos) <= WINDOW, kpos >= 0))
        fn(kv[:, :64], kv[:, 64:], mask, True, [(0, c_rows), (start, wl)])


def _head_columns(cols):
    lane = lax.broadcasted_iota(jnp.int32, (TM, 128), 1)
    out = jnp.zeros((TM, 128), F32)
    for g, col in enumerate(cols):
        out = jnp.where(lane == g, col, out)
    return out


def _attn_specs(t_rows):
    nt = t_rows // TM
    q_spec = pl.BlockSpec((None, TM, Q_WIDTH), lambda b, hg, t, s: (b, t, hg))
    kv_spec = pl.BlockSpec((None, t_rows, 128), lambda b, hg, t, s: (b, 0, hg))
    lse_spec = pl.BlockSpec((None, TM, 128), lambda b, hg, t, s: (hg, b * nt + t, 0))
    return q_spec, kv_spec, lse_spec


def _attn_fwd(q_all, kv_all, sink8, c_rows, name, comm=None):
    bl, t_rows, _ = q_all.shape
    q_spec, kv_spec, lse_spec = _attn_specs(t_rows)

    def body(sink_ref, q_ref, kv_ref, o_ref, ot_ref, lse_ref):
        hg, t = pl.program_id(1), pl.program_id(2)

        def fn(k, v, mask, use_sink, spans):
            outs, lses = [], []
            v_one = jnp.concatenate([v, jnp.ones(v.shape, BF16)], axis=1)
            def scores(g):
                s = lax.dot_general(q_ref[:, g * 64:(g + 1) * 64], k, (((1,), (1,)), ((), ())),
                                    preferred_element_type=F32)
                return s if mask is None else jnp.where(mask, s, NEG_BIG)

            s_next = scores(0)
            for g in range(GROUP):
                s = s_next
                if g + 1 < GROUP:
                    s_next = scores(g + 1)
                mx = jnp.max(s, axis=1, keepdims=True)
                if use_sink:
                    sink = sink_ref[jnp.maximum(hg - 2, 0) * GROUP + g] * LOG2E
                    mx = jnp.maximum(mx, sink)
                pv = jnp.dot(jnp.exp2(s - mx).astype(BF16), v_one, preferred_element_type=F32)
                l = pv[:, 64:65]
                if use_sink:
                    l = l + jnp.exp2(sink - mx)
                outs.append(pv[:, :64] * (1.0 / l))
                lses.append(mx + jnp.log2(l))
            o = jnp.concatenate(outs, axis=1).astype(BF16)
            o_ref[...] = o
            ot_ref[...] = o.T
            lse_ref[...] = _head_columns(lses)

        _attn_case(t, hg, kv_ref, c_rows, t_rows, fn)

    nt = t_rows // TM
    ot_spec = pl.BlockSpec((Q_WIDTH, TM), lambda b, hg, t, s: (hg, b * nt + t))
    outs, couts = _call(
        body, name=name, grid=(bl, N_HG, nt), in_specs=[q_spec, kv_spec], out_specs=[q_spec, ot_spec, lse_spec],
        out_shape=[jax.ShapeDtypeStruct(q_all.shape, BF16), jax.ShapeDtypeStruct((N_HG * Q_WIDTH, bl * t_rows), BF16),
                   jax.ShapeDtypeStruct((N_HG, bl * t_rows, 128), F32)],
        args=[sink8, q_all, kv_all], prefetch=1, sem=("parallel", "parallel", "arbitrary"), comm=comm)
    return outs if comm is None else (outs, couts)


def _attn_bwd(q_all, kv_all, do, o, lse, sink8, c_rows, name, comm=None):
    bl, t_rows, _ = q_all.shape
    q_spec, kv_spec, lse_spec = _attn_specs(t_rows)
    ds_spec = pl.BlockSpec((None, None, 8, 128), lambda b, hg, t, s: (b, hg, 0, 0))

    def body(sink_ref, q_ref, kv_ref, do_ref, o_ref, lse_ref, dq_ref, dkv_ref, dsk_ref):
        hg, t = pl.program_id(1), pl.program_id(2)

        @pl.when(t == 0)
        def _():
            dkv_ref[...] = jnp.zeros_like(dkv_ref)
            dsk_ref[...] = jnp.zeros_like(dsk_ref)

        lse_rows = lse_ref[...].T
        dd_cols = [jnp.sum(do_ref[:, g * 64:(g + 1) * 64].astype(F32) * o_ref[:, g * 64:(g + 1) * 64].astype(F32),
                           axis=1, keepdims=True) for g in range(GROUP)]
        dd_rows = _head_columns(dd_cols).T

        def fn(k, v, mask, use_sink, spans):
            k_t = k.T
            dq_t, dsinks = [], []
            dk = jnp.zeros(k.shape, F32)
            dv = jnp.zeros(v.shape, F32)
            def products(g):
                s = lax.dot_general(k, q_ref[:, g * 64:(g + 1) * 64], (((1,), (1,)), ((), ())),
                                    preferred_element_type=F32)
                dp = lax.dot_general(v, do_ref[:, g * 64:(g + 1) * 64], (((1,), (1,)), ((), ())),
                                     preferred_element_type=F32)
                return (s if mask is None else jnp.where(mask, s, NEG_BIG)), dp

            nxt = products(0)
            for g in range(GROUP):
                q = q_ref[:, g * 64:(g + 1) * 64]
                do_g = do_ref[:, g * 64:(g + 1) * 64]
                lse_g, dd_g = lse_rows[g:g + 1, :], dd_rows[g:g + 1, :]
                s, dp = nxt
                if g + 1 < GROUP:
                    nxt = products(g + 1)
                pb = jnp.exp2(s - lse_g).astype(BF16)
                ds = (pb.astype(F32) * (dp - dd_g)).astype(BF16)
                dk = dk + jnp.dot(ds, q, preferred_element_type=F32)
                dv = dv + jnp.dot(pb, do_g, preferred_element_type=F32)
                dq_t.append(jnp.dot(k_t, ds, preferred_element_type=F32))
                if use_sink:
                    p_sink = jnp.exp2(sink_ref[jnp.maximum(hg - 2, 0) * GROUP + g] * LOG2E - lse_g)
                    dsinks.append(jnp.broadcast_to(-jnp.sum(p_sink * dd_g, axis=1, keepdims=True), (1, 128)))
            dq_ref[...] = jnp.concatenate(dq_t, axis=0).T
            dkv = jnp.concatenate([dk * (1.0 / LOG2E), dv], axis=1)
            off = 0
            for start, size in spans:
                dkv_ref[pl.ds(start, size), :] += dkv[off:off + size]
                off += size
            if use_sink:
                dsk_ref[0:GROUP, :] += jnp.concatenate(dsinks, axis=0)

        _attn_case(t, hg, kv_ref, c_rows, t_rows, fn, keys_first=True)

    outs, couts = _call(
        body, name=name, grid=(bl, N_HG, t_rows // TM), in_specs=[q_spec, kv_spec, q_spec, q_spec, lse_spec],
        out_specs=[q_spec, kv_spec, ds_spec],
        out_shape=[jax.ShapeDtypeStruct(q_all.shape, F32), jax.ShapeDtypeStruct(kv_all.shape, F32),
                   jax.ShapeDtypeStruct((bl, N_HG, 8, 128), F32)],
        args=[sink8, q_all, kv_all, do, o, lse], prefetch=1, sem=("parallel", "parallel", "arbitrary"), comm=comm)
    return outs if comm is None else (outs, couts)


def _silu(x):
    return x * jax.nn.sigmoid(x)


def _ada_fwd(c_rows, w_ada, b_cols, name, comm=None):
    nl, d, w = w_ada.shape
    r = c_rows.shape[0]

    def body(c_ref, w_ref, b_ref, o_ref):
        s = _silu(c_ref[...]).astype(BF16)
        o_ref[...] = jnp.dot(s, w_ref[...].astype(BF16), preferred_element_type=F32) + b_ref[...]

    outs, couts = _call(
        body, name=name, grid=(nl,),
        in_specs=[pl.BlockSpec((r, d), lambda l: (0, 0)), pl.BlockSpec((None, d, w), lambda l: (l, 0, 0)),
                  pl.BlockSpec((None, 1, w), lambda l: (l, 0, 0))],
        out_specs=pl.BlockSpec((None, r, w), lambda l: (l, 0, 0)),
        out_shape=jax.ShapeDtypeStruct((nl, r, w), F32), args=[c_rows, w_ada, b_cols], sem=("parallel",), comm=comm)
    return outs if comm is None else (outs, couts)


def _ada_bwd(c_rows, c_ctx, dmod, w_ada, name):
    nl, d, w = w_ada.shape
    r = c_rows.shape[0]

    def body(c_ref, cc_ref, g_ref, w_ref, dw_ref, dc_ref):
        l = pl.program_id(0)
        s = _silu(c_ref[...]).astype(BF16)
        gm = g_ref[...].astype(BF16)
        dw_ref[...] = lax.dot_general(s, gm, (((0,), (0,)), ((), ())), preferred_element_type=F32)
        ds = lax.dot_general(gm, w_ref[...].astype(BF16), (((1,), (1,)), ((), ())), preferred_element_type=F32)
        rows = lax.broadcasted_iota(jnp.int32, ds.shape, 0)
        dsc = jnp.sum(jnp.where(rows % ADA_ROWS == 2, ds, 0.0), axis=0, keepdims=True)
        x = cc_ref[...]
        sg = jax.nn.sigmoid(x)
        dcc = dsc * (sg * (1.0 + x * (1.0 - sg)))
        out = jnp.concatenate([dcc, jnp.zeros((7, d), F32)], axis=0)

        @pl.when(l == 0)
        def _():
            dc_ref[...] = out

        @pl.when(l != 0)
        def _():
            dc_ref[...] += out

    return pl.pallas_call(
        body, name=name, grid=(nl,),
        in_specs=[pl.BlockSpec((r, d), lambda l: (0, 0)), pl.BlockSpec((1, d), lambda l: (0, 0)),
                  pl.BlockSpec((None, r, w), lambda l: (l, 0, 0)), pl.BlockSpec((None, d, w), lambda l: (l, 0, 0))],
        out_specs=[pl.BlockSpec((None, d, w), lambda l: (l, 0, 0)), pl.BlockSpec((8, d), lambda l: (0, 0))],
        out_shape=[jax.ShapeDtypeStruct((nl, d, w), F32), jax.ShapeDtypeStruct((8, d), F32)],
        compiler_params=_params(("arbitrary",)),
    )(c_rows, c_ctx, dmod, w_ada)


def _adam_math(w, g, m, v):
    m = ADAM_B1 * m + (1.0 - ADAM_B1) * g
    v = ADAM_B2 * v + (1.0 - ADAM_B2) * (g * g)
    m_hat = m / (1.0 - ADAM_B1 ** ADAM_STEP)
    v_hat = v / (1.0 - ADAM_B2 ** ADAM_STEP)
    delta = -ADAM_LR * (m_hat / (jnp.sqrt(v_hat) + ADAM_EPS) + ADAM_WD * w)
    return delta, m, v


def _adamw(w, m, v, g_own, g_recv, name, rows=256):
    nl, r, c = w.shape
    tr = min(rows, r)
    spec = pl.BlockSpec((None, tr, c), lambda l, i: (l, i, 0))
    per_layer = g_own is None
    own = [] if per_layer else [g_own]
    recv = [] if g_recv is None else list(g_recv)
    n_i = r // tr

    def rows_of(li):
        return lambda l, i: jnp.where(l == li, i, jnp.where(l > li, n_i - 1, 0))

    in_specs = [spec] * (3 + len(own))
    in_specs += [pl.BlockSpec((N_DEV, tr, c), lambda l, i, f=rows_of(li): (0, f(l, i), 0)) for li in range(len(recv))]

    def body(*refs):
        w_ref, m_ref, v_ref = refs[:3]
        own_refs, recv_refs = refs[3:3 + len(own)], refs[3 + len(own):3 + len(own) + len(recv)]
        go_ref, d_ref, mo_ref, vo_ref = refs[-4:]

        def update(li):
            if recv:
                g = recv_refs[li][0].astype(F32)
                for k in range(1, N_DEV):
                    g = g + recv_refs[li][k].astype(F32)
            else:
                g = own_refs[0][...]
            delta, m_, v_ = _adam_math(w_ref[...], g, m_ref[...], v_ref[...])
            go_ref[...] = g
            d_ref[...] = delta
            mo_ref[...] = m_
            vo_ref[...] = v_

        if per_layer:
            for li in range(nl):
                pl.when(pl.program_id(0) == li)(functools.partial(update, li))
        else:
            update(0)

    return pl.pallas_call(
        body, name=name, grid=(nl, n_i), in_specs=in_specs, out_specs=[spec] * 4,
        out_shape=[jax.ShapeDtypeStruct(w.shape, F32)] * 4, compiler_params=_params(("parallel", "parallel")),
    )(w, m, v, *own, *recv)


def _small_adamw(w, m, v, g_all, name):
    def body(w_ref, m_ref, v_ref, g_ref, go_ref, d_ref, mo_ref, vo_ref):
        g = g_ref[0]
        for k in range(1, N_DEV):
            g = g + g_ref[k]
        delta, m_, v_ = _adam_math(w_ref[...], g, m_ref[...], v_ref[...])
        go_ref[...] = g
        d_ref[...] = delta
        mo_ref[...] = m_
        vo_ref[...] = v_

    return pl.pallas_call(
        body, name=name, out_shape=[jax.ShapeDtypeStruct(w.shape, F32)] * 4, compiler_params=_params(),
    )(w, m, v, g_all)


SMALL = ("c_ctx", "b_ada", "g_pre_mix", "g_post_mix", "g_pre_mlp", "g_post_mlp", "q_norm", "k_norm", "sink", "loss")


def _pack_small(parts):
    flat = jnp.concatenate([parts[n].reshape(-1) for n in SMALL])
    rows = -(-flat.shape[0] // 1024) * 8
    return jnp.pad(flat, (0, rows * 128 - flat.shape[0])).reshape(rows, 128)


def _unpack_small(packed, like):
    flat = packed.reshape(-1)
    out, off = {}, 0
    for n in SMALL:
        size = int(np.prod(like[n].shape))
        out[n] = flat[off:off + size].reshape(like[n].shape)
        off += size
    return out


def kernel(x, c, ctx, c_ctx, w_ada, b_ada, g_pre_mix, g_post_mix, g_pre_mlp, g_post_mlp, w_in, q_norm, k_norm, sink, w_out, w_up, w_down, loss_target, m_c_ctx, m_w_ada, m_b_ada, m_g_pre_mix, m_g_post_mix, m_g_pre_mlp, m_g_post_mlp, m_w_in, m_q_norm, m_k_norm, m_sink, m_w_out, m_w_up, m_w_down, v_c_ctx, v_w_ada, v_b_ada, v_g_pre_mix, v_g_post_mix, v_g_pre_mlp, v_g_post_mlp, v_w_in, v_q_norm, v_k_norm, v_sink, v_w_out, v_w_up, v_w_down):
    bl, s_rows, d = x.shape
    c_rows = ctx.shape[1]
    assert c_rows == TM and s_rows % TM == 0 and bl == 2
    t_rows = c_rows + s_rows
    nt = t_rows // TM
    m_rows = bl * t_rows
    nl = w_in.shape[0]
    ada_w = w_ada.shape[2]
    d_ff = w_up.shape[2] * N_DEV
    me = _my_index()

    shard = lambda w_, l: w_[l].astype(BF16)
    c_pad = jnp.concatenate([c, c_ctx[None, :], jnp.zeros((ADA_ROWS - bl - 1, d), F32)], axis=0)
    gathered = {0: {}}
    h, (c_all, gathered[0]["w_in"]) = _token_stream(
        ctx, x, nt, "token_stream", comm=_Comm([(c_pad, GATHER_VIA_SIBLING), (shard(w_in, 0), GATHER_VIA_SIBLING)]))
    c_all = c_all.reshape(N_DEV * ADA_ROWS, d)

    def layer_weights(l):
        g_ = gathered[l]
        w_out_f = g_["w_out"].reshape(-1, d)
        w_down_f = g_["w_down"].reshape(d_ff, d)
        return dict(
            w_in_f=g_["w_in"].transpose(1, 0, 2).reshape(d, PROJ_W), w_out_f=w_out_f,
            w_up_s=g_["w_up"], w_up_t=g_["w_up"].transpose(0, 2, 1).reshape(d_ff, d), w_down_f=w_down_f)

    big = dict(tm=2304, tn=512)
    deep = dict(tm=1536, tn=512, tk=d_ff)
    wide = dict(tm=1024, tn=512, tk=m_rows)

    b_cols = lax.dynamic_slice(b_ada, (0, me * ada_w), (nl, ada_w))[:, None, :]
    mod_cols = _ada_fwd(c_all, w_ada, b_cols, "ada_fwd")
    mod_slots = mod_cols.reshape(nl, N_DEV, ADA_ROWS, ada_w).transpose(1, 0, 2, 3)
    mod_g, = _comm_only(_Comm([(mod_slots, TO_OWNER)]), "exchange_mod")
    mine = mod_g.transpose(1, 2, 0, 3).reshape(nl, ADA_ROWS, N_MOD, d)
    pad = jnp.zeros((bl, 2, MOD_ROWS - N_MOD, d), F32)
    modsel = [jnp.concatenate([jnp.stack([jnp.broadcast_to(mine[l, bl], (bl, N_MOD, d)), mine[l, :bl]], axis=1), pad],
                              axis=2) for l in range(nl)]

    cos, sin = _rope_tables(t_rows, c_rows)
    emat = _head_mean_matrix()
    row = lambda a: a[None, :]
    qn = [jnp.tile(q_norm[l], 8)[None, :] for l in range(nl)]
    kn = [jnp.tile(k_norm[l], 2)[None, :] for l in range(nl)]

    target = loss_target.reshape(bl * s_rows, d)
    saved = []
    weights_of = {}
    (u, u_t), (gathered[0]["w_out"],) = _norm_mod_fwd(
        h, row(g_pre_mix[0]), modsel[0], 0, 1, nt, "mix_mod_fwd0", comm=_Comm([(shard(w_out, 0), GATHER)]))
    for l in range(nl):
        w_in_f = gathered[l]["w_in"].transpose(1, 0, 2).reshape(d, PROJ_W)
        proj = _mm(u, w_in_f, name=f"mm_in{l}", tk=d, **big)
        q_all, kv_all = _prep_fwd(proj, qn[l], kn[l], cos, sin, emat, nt, f"prep_fwd{l}")
        (o, o_t, lse), (w_up_g, w_down_g) = _attn_fwd(
            q_all.reshape(bl, t_rows, 1024), kv_all.reshape(bl, t_rows, 512), sink[l], c_rows, f"attn_fwd{l}",
            comm=_Comm([(shard(w_up, l), GATHER), (shard(w_down, l), GATHER)]))
        o = o.reshape(m_rows, 1024)
        gathered[l].update(w_up=w_up_g, w_down=w_down_g)
        wl = weights_of[l] = layer_weights(l)
        mix = _mm(o, wl["w_out_f"], name=f"mm_out{l}", tk=1024, **big)
        h_mid, v_in, v_t = _gate_mod_fwd(h, mix, row(g_post_mix[l]), modsel[l], 2, row(g_pre_mlp[l]), modsel[l], 3, 4,
                                         nt, f"mix_gate_mlp_mod_fwd{l}")
        more = l + 1 < nl
        res_up = _mm(v_in, wl["w_up_s"], name=f"mm_up{l}", b_mode="nn_slots", epilogue="relu2", tk=d,
                     comm=_Comm([(shard(w_in, l + 1), GATHER)]) if more else None, **big)
        (r_act, r_t), nxt_in = res_up if more else (res_up, None)
        res_down = _mm(r_act, wl["w_down_f"], name=f"mm_down{l}",
                       comm=_Comm([(shard(w_out, l + 1), GATHER)]) if more else None, **deep)
        y, nxt_out = res_down if more else (res_down, None)
        saved.append((h, u_t, proj, q_all, kv_all, o, o_t, lse, mix, h_mid, v_t, r_act, r_t, y))
        if more:
            gathered[l + 1] = dict(w_in=nxt_in[0], w_out=nxt_out[0])
            h, u, u_t = _gate_mod_fwd(h_mid, y, row(g_post_mlp[l]), modsel[l], 5, row(g_pre_mix[l + 1]), modsel[l + 1],
                                      0, 1, nt, f"mlp_gate_mix_mod_fwd{l}")

    dh, dy, dvec_g2, ss = _gate_loss_bwd(h_mid, y, row(g_post_mlp[nl - 1]), modsel[nl - 1], 5, target, nt, "gate_loss_bwd")
    small_g = {n: [None] * nl for n in SMALL if n not in ("c_ctx", "b_ada", "loss")}
    dvecs = {l: {} for l in range(nl)}
    dvecs[nl - 1]["g2"] = dvec_g2
    slots = {n: [None] * nl for n in ("w_in", "w_out", "w_up", "w_down")}
    recvd = {n: [None] * nl for n in slots}
    send = lambda n, l_: (slots[n][l_], TO_OWNER_XOR)
    for l in reversed(range(nl)):
        h_in, u_t, proj, q_all, kv_all, o, o_t, lse, mix, h_mid, v_t, r_act, r_t, y = saved[l]
        wl = weights_of[l]
        later = l + 1 < nl
        res = _mm(dy, wl["w_down_f"], name=f"mm_da{l}", b_mode="nt", epilogue="relu2_bwd", extra=r_act, out_dtype=BF16, tk=d,
                  comm=_Comm([send("w_out", l + 1)]) if later else None, **big)
        da = res[0] if later else res
        if later:
            recvd["w_out"][l + 1] = res[1][0]
        res = _mm(r_t, dy, name=f"mm_dw_down{l}", out_dtype=BF16,
                  comm=_Comm([send("w_in", l + 1)]) if later else None, **wide)
        dw_down = res[0] if later else res
        if later:
            recvd["w_in"][l + 1] = res[1][0]
        dw_up = _mm(v_t, da, name=f"mm_dw_up{l}", out_mode="slots", out_dtype=BF16, **wide)
        slots["w_down"][l] = dw_down.reshape(N_DEV, -1, d)
        slots["w_up"][l] = dw_up
        dv = _mm(da, wl["w_up_t"], name=f"mm_dv{l}", **deep)
        dh, dvecs[l]["m2"], dmix, dvecs[l]["g1"] = _mod_gate_bwd(
            h_mid, row(g_pre_mlp[l]), modsel[l], 3, 4, dv, dh, mix, row(g_post_mix[l]), modsel[l], 2, nt,
            f"mlp_mod_mix_gate_bwd{l}")
        do = _mm(dmix, wl["w_out_f"], name=f"mm_do{l}", b_mode="nt", out_dtype=BF16, tk=d, **big)
        dw_out = _mm(o_t, dmix, name=f"mm_dw_out{l}", out_dtype=BF16, **wide)
        slots["w_out"][l] = dw_out.reshape(N_DEV, -1, d)
        (dq, dkv, dsk), (recvd["w_down"][l], recvd["w_up"][l]) = _attn_bwd(
            q_all.reshape(bl, t_rows, 1024), kv_all.reshape(bl, t_rows, 512), do.reshape(bl, t_rows, 1024),
            o.reshape(bl, t_rows, 1024), lse, sink[l], c_rows, f"attn_bwd{l}",
            comm=_Comm([send("w_down", l), send("w_up", l)]))
        last = l == 0
        res = _prep_bwd(proj, dq.reshape(m_rows, 1024), dkv.reshape(m_rows, 512), qn[l], kn[l], cos, sin, emat, nt,
                        f"prep_bwd{l}", comm=_Comm([send("w_out", l)]) if last else None)
        dproj, dqn, dkn = res[0] if last else res
        if last:
            recvd["w_out"][l] = res[1][0]
        dw_in = _mm(u_t, dproj, name=f"mm_dw_in{l}", out_dtype=BF16, **wide)
        slots["w_in"][l] = dw_in.reshape(d, N_DEV, PROJ_W // N_DEV).transpose(1, 0, 2)
        du = _mm(dproj, wl["w_in_f"], name=f"mm_du{l}", b_mode="nt", tk=PROJ_W, **big)
        if last:
            (dh, dvecs[l]["m1"]), (recvd["w_in"][l],) = _norm_mod_bwd(
                h_in, row(g_pre_mix[l]), modsel[l], du, dh, 0, 1, nt, f"mix_mod_bwd{l}", latent_only=True,
                comm=_Comm([send("w_in", l)]))
        else:
            dh, dvecs[l]["m1"], dy, dvecs[l - 1]["g2"] = _mod_gate_bwd(
                h_in, row(g_pre_mix[l]), modsel[l], 0, 1, du, dh, saved[l - 1][-1], row(g_post_mlp[l - 1]),
                modsel[l - 1], 5, nt, f"mix_mod_mlp_gate_bwd{l}")
        small_g["q_norm"][l] = jnp.sum(dqn[0].reshape(8, HEAD_DIM), axis=0)
        small_g["k_norm"][l] = jnp.sum(dkn[0].reshape(2, HEAD_DIM), axis=0)
        small_g["sink"][l] = jnp.sum(dsk[:, 2:, :GROUP, 0], axis=0).reshape(-1)

    dmod_rows = []
    for l in range(nl):
        m1, g1, m2, g2 = (dvecs[l][n] for n in ("m1", "g1", "m2", "g2"))
        small_g["g_pre_mix"][l] = jnp.sum(m1[:, :, 2], axis=(0, 1))
        small_g["g_post_mix"][l] = jnp.sum(g1[:, :, 1], axis=(0, 1))
        small_g["g_pre_mlp"][l] = jnp.sum(m2[:, :, 2], axis=(0, 1))
        small_g["g_post_mlp"][l] = jnp.sum(g2[:, :, 1], axis=(0, 1))
        dms = jnp.stack([m1[:, :, 0], m1[:, :, 1], g1[:, :, 0], m2[:, :, 0], m2[:, :, 1], g2[:, :, 0]], axis=2)
        rows = jnp.concatenate([dms[:, 1], jnp.sum(dms[:, 0], axis=0)[None]], axis=0)
        dmod_rows.append(jnp.pad(rows.reshape(bl + 1, N_MOD * d), ((0, ADA_ROWS - bl - 1), (0, 0))))
    grad_x = dh.reshape(bl, s_rows, d)

    dmod_slots = jnp.stack(dmod_rows).reshape(nl, ADA_ROWS, N_DEV, ada_w).transpose(2, 0, 1, 3)
    dmod_g, = _comm_only(_Comm([(dmod_slots, TO_OWNER)]), "exchange_dmod")
    dmod_mine = dmod_g.transpose(1, 0, 2, 3).reshape(nl, N_DEV * ADA_ROWS, ada_w)
    dw_ada, dcc = _ada_bwd(c_all, c_ctx[None, :], dmod_mine, w_ada, "ada_bwd")

    parts = {n: jnp.stack(small_g[n]) for n in small_g}
    parts["c_ctx"] = dcc[0]
    parts["loss"] = ss[0, 0:1]
    no_loss = jnp.zeros((1,), F32)
    parts["b_ada"] = jnp.stack([jnp.sum(r_[: bl + 1], axis=0) for r_ in dmod_rows])
    weights = dict(c_ctx=c_ctx, b_ada=b_ada, g_pre_mix=g_pre_mix, g_post_mix=g_post_mix, g_pre_mlp=g_pre_mlp,
                   g_post_mlp=g_post_mlp, q_norm=q_norm, k_norm=k_norm, sink=sink, loss=no_loss)
    moms = dict(c_ctx=m_c_ctx, b_ada=m_b_ada, g_pre_mix=m_g_pre_mix, g_post_mix=m_g_post_mix, g_pre_mlp=m_g_pre_mlp,
                g_post_mlp=m_g_post_mlp, q_norm=m_q_norm, k_norm=m_k_norm, sink=m_sink, loss=no_loss)
    vels = dict(c_ctx=v_c_ctx, b_ada=v_b_ada, g_pre_mix=v_g_pre_mix, g_post_mix=v_g_post_mix, g_pre_mlp=v_g_pre_mlp,
                g_post_mlp=v_g_post_mlp, q_norm=v_q_norm, k_norm=v_k_norm, sink=v_sink, loss=no_loss)
    small_all, = _comm_only(_Comm([(_pack_small(parts), GATHER)]), "gather_small")
    s_out = _small_adamw(_pack_small(weights), _pack_small(moms), _pack_small(vels), small_all, "adamw_small")
    s_g, s_d, s_m, s_v = [_unpack_small(a, weights) for a in s_out]
    loss = 0.5 * s_g["loss"][0] / d

    res = {}
    for n, w_, m_, v_ in (("w_in", w_in, m_w_in, v_w_in), ("w_out", w_out, m_w_out, v_w_out),
                          ("w_up", w_up, m_w_up, v_w_up), ("w_down", w_down, m_w_down, v_w_down)):
        res[n] = _adamw(w_, m_, v_, None, recvd[n], f"adamw_{n}")
    res["w_ada"] = _adamw(w_ada, m_w_ada, v_w_ada, dw_ada, None, "adamw_w_ada")

    order = ("c_ctx", "w_ada", "b_ada", "g_pre_mix", "g_post_mix", "g_pre_mlp", "g_post_mlp", "w_in", "q_norm",
             "k_norm", "sink", "w_out", "w_up", "w_down")
    outs = [loss, grad_x]
    for i, small in enumerate((s_g, s_d, s_m, s_v)):
        outs += [small[n] if n in small else res[n][i] for n in order]
    return tuple(outs)
```

```python
import functools

import jax
import jax.numpy as jnp
import numpy as np
from jax import lax
from jax.experimental import pallas as pl
from jax.experimental.pallas import tpu as pltpu

F32 = jnp.float32
BF16 = jnp.bfloat16

HEAD_DIM = 64
GROUP = 4
N_HG = 4
Q_WIDTH = GROUP * HEAD_DIM
WINDOW = 128
GRID_W = 64
ROPE_THETA = 10000.0
EPS = 1e-6
NEG_BIG = -1e30
N_MOD = 6
MOD_ROWS = 8
TM = 256
N_DEV = 8
ADA_ROWS = 8
VMEM_LIMIT = 56 * 1024 * 1024

ADAM_LR = 0.001
ADAM_B1 = 0.9
ADAM_B2 = 0.999
ADAM_EPS = 1e-08
ADAM_WD = 0.01
ADAM_STEP = 10


def _params(sem=None):
    kw = dict(vmem_limit_bytes=VMEM_LIMIT)
    if sem is not None:
        kw["dimension_semantics"] = sem
    return pltpu.CompilerParams(**kw)


def _my_index():
    return 4 * lax.axis_index("x") + 2 * lax.axis_index("y") + lax.axis_index("c")


def _peer(k):
    x, y, c = lax.axis_index("x"), lax.axis_index("y"), lax.axis_index("c")
    kx, ky, kc = (k >> 2) & 1, (k >> 1) & 1, k & 1
    px = (1 - x) if kx else x
    py = (1 - y) if ky else y
    pc = (1 - c) if kc else c
    return (px, py, pc), 4 * px + 2 * py + pc


GATHER, GATHER_VIA_SIBLING, TO_OWNER, TO_OWNER_XOR = "gather", "gather_via_sibling", "to_owner", "to_owner_xor"


class _Comm:
    def __init__(self, items):
        self.items = list(items)
        self.arrays = [a for a, _ in self.items]

    def out_shapes(self):
        return [jax.ShapeDtypeStruct(((N_DEV,) + a.shape) if kind in (GATHER, GATHER_VIA_SIBLING) else a.shape, a.dtype)
                for a, kind in self.items]

    def sem_shapes(self):
        n = len(self.items) * N_DEV
        return [pltpu.SemaphoreType.DMA((n,)), pltpu.SemaphoreType.DMA((n,))]

    def _two_level(self, x_ref, o_ref, send_sems, recv_sems, base):
        x, y, c = lax.axis_index("x"), lax.axis_index("y"), lax.axis_index("c")
        me, sibling = (x, y, c), (x, y, 1 - c)
        chips = [(1 - x, y), (x, 1 - y), (1 - x, 1 - y)]
        block = lambda p: o_ref.at[4 * p[0] + 2 * p[1] + p[2]]

        def copy(k, owner, to, src=None):
            return pltpu.make_async_remote_copy(
                src_ref=block(owner) if src is None else src, dst_ref=block(owner), send_sem=send_sems.at[base + k],
                recv_sem=recv_sems.at[base + k], device_id=to, device_id_type=pl.DeviceIdType.MESH)

        first = [copy(1, me, sibling, src=x_ref)] + [copy(2 + j, me, (*chip, c), src=x_ref) for j, chip in enumerate(chips)]
        relay = [(copy(2 + j, (*chip, c), me), copy(5 + j, (*chip, c), sibling)) for j, chip in enumerate(chips)]
        last = [copy(1, sibling, me)] + [copy(5 + j, (*chip, 1 - c), me) for j, chip in enumerate(chips)]
        return first, relay, last

    def _copies(self, in_refs, out_refs, send_sems, recv_sems):
        me = _my_index()
        local, remote = [], []
        for i, ((_, kind), x_ref, o_ref) in enumerate(zip(self.items, in_refs, out_refs)):
            base = i * N_DEV
            if kind == GATHER_VIA_SIBLING:
                local.append(pltpu.make_async_copy(x_ref, o_ref.at[me], send_sems.at[base]))
                continue
            own_src = x_ref if kind == GATHER else x_ref.at[me]
            own_dst = o_ref.at[0] if kind == TO_OWNER_XOR else o_ref.at[me]
            local.append(pltpu.make_async_copy(own_src, own_dst, send_sems.at[base]))
            for k in range(1, N_DEV):
                peer, pidx = _peer(k)
                remote.append(pltpu.make_async_remote_copy(
                    src_ref=x_ref if kind == GATHER else x_ref.at[pidx],
                    dst_ref=o_ref.at[k] if kind == TO_OWNER_XOR else o_ref.at[me],
                    send_sem=send_sems.at[base + k], recv_sem=recv_sems.at[base + k],
                    device_id=peer, device_id_type=pl.DeviceIdType.MESH))
        return local, remote

    def start(self, in_refs, out_refs, send_sems, recv_sems):
        local, remote = self._copies(in_refs, out_refs, send_sems, recv_sems)
        for cp in local + remote:
            cp.start()
        for i, ((_, kind), x_ref, o_ref) in enumerate(zip(self.items, in_refs, out_refs)):
            if kind == GATHER_VIA_SIBLING:
                for cp in self._two_level(x_ref, o_ref, send_sems, recv_sems, i * N_DEV)[0]:
                    cp.start()

    def wait(self, in_refs, out_refs, send_sems, recv_sems):
        local, remote = self._copies(in_refs, out_refs, send_sems, recv_sems)
        for cp in remote:
            cp.wait_recv()
        for cp in remote:
            cp.wait_send()
        for i, ((_, kind), x_ref, o_ref) in enumerate(zip(self.items, in_refs, out_refs)):
            if kind == GATHER_VIA_SIBLING:
                first, relay, last = self._two_level(x_ref, o_ref, send_sems, recv_sems, i * N_DEV)
                for arrival, onward in relay:
                    arrival.wait_recv()
                    onward.start()
                for cp in last:
                    cp.wait_recv()
                for cp in first + [onward for _, onward in relay]:
                    cp.wait_send()
        for cp in local:
            cp.wait()


def _call(body, *, name, grid, in_specs, out_specs, out_shape, args, scratch_shapes=(), prefetch=0, sem=None,
          comm=None):
    single = not isinstance(out_shape, (list, tuple))
    out_shape = [out_shape] if single else list(out_shape)
    out_specs = [out_specs] if single else list(out_specs)
    in_specs, scratch_shapes, args = list(in_specs), list(scratch_shapes), list(args)
    n_in, n_out = len(in_specs), len(out_shape)
    if comm is not None:
        nc = len(comm.arrays)
        hbm = pl.BlockSpec(memory_space=pl.ANY)
        inner = body

        def body(*refs):
            pre, r = refs[:prefetch], refs[prefetch:]
            ins, cin = r[:n_in], r[n_in:n_in + nc]
            outs, cout = r[n_in + nc:n_in + nc + n_out], r[n_in + nc + n_out:n_in + 2 * nc + n_out]
            scr, sems = r[n_in + 2 * nc + n_out:len(r) - 2], r[len(r) - 2:]
            ids = [pl.program_id(i) for i in range(len(grid))]

            def when(flags, fn):
                if flags:
                    pl.when(functools.reduce(jnp.logical_and, flags))(fn)
                else:
                    fn()

            when([i == 0 for i in ids], lambda: comm.start(cin, cout, *sems))
            inner(*pre, *ins, *outs, *scr)
            when([i == n - 1 for i, n in zip(ids, grid)], lambda: comm.wait(cin, cout, *sems))

        in_specs += [hbm] * nc
        out_specs += [hbm] * nc
        out_shape += comm.out_shapes()
        scratch_shapes += comm.sem_shapes()
        args += comm.arrays
        sem = ("arbitrary",) * len(grid)
    kw = dict(name=name, out_shape=out_shape, compiler_params=_params(sem if grid else None))
    if prefetch:
        kw["grid_spec"] = pltpu.PrefetchScalarGridSpec(
            num_scalar_prefetch=prefetch, grid=grid, in_specs=in_specs, out_specs=out_specs,
            scratch_shapes=scratch_shapes)
    else:
        kw.update(in_specs=in_specs, out_specs=out_specs, scratch_shapes=scratch_shapes)
        if grid:
            kw["grid"] = grid
    res = list(pl.pallas_call(body, **kw)(*args))
    outs = res[:n_out]
    return (outs[0] if single else outs), res[n_out:]


def _comm_only(comm, name):
    return _call(lambda: None, name=name, grid=(), in_specs=[], out_specs=[], out_shape=[], args=[], comm=comm)[1]


def _mm(a, b, *, name, ta=False, b_mode="nn", out_mode="plain", out_dtype=F32, tm=512, tn=512, tk=512,
        epilogue=None, extra=None, comm=None):
    if ta:
        kdim, m = a.shape
    else:
        m, kdim = a.shape
    if b_mode == "nn":
        n = b.shape[1]
    elif b_mode == "nt":
        n = b.shape[0]
    elif b_mode == "nn_slots":
        n = b.shape[0] * b.shape[2]
        tn = b.shape[2]
    else:
        n = b.shape[1]
        tk = b.shape[2]
    if out_mode == "slots":
        tn = n // N_DEV
    tm, tn, tk = min(tm, m), min(tn, n), min(tk, kdim)
    assert m % tm == 0 and n % tn == 0 and kdim % tk == 0, (name, m, n, kdim, tm, tn, tk)
    nk = kdim // tk

    a_spec = pl.BlockSpec((tk, tm), lambda i, j, k: (k, i)) if ta else pl.BlockSpec((tm, tk), lambda i, j, k: (i, k))
    if b_mode == "nn":
        b_spec = pl.BlockSpec((tk, tn), lambda i, j, k: (k, j))
    elif b_mode == "nt":
        b_spec = pl.BlockSpec((tn, tk), lambda i, j, k: (j, k))
    elif b_mode == "nn_slots":
        b_spec = pl.BlockSpec((None, tk, tn), lambda i, j, k: (j, k, 0))
    else:
        b_spec = pl.BlockSpec((None, tn, tk), lambda i, j, k: (k, j, 0))
    tb = b_mode in ("nt", "nt_slots")
    if out_mode == "plain":
        o_shape, o_spec = (m, n), pl.BlockSpec((tm, tn), lambda i, j, k: (i, j))
    else:
        o_shape, o_spec = (N_DEV, m, tn), pl.BlockSpec((None, tm, tn), lambda i, j, k: (j, i, 0))
    dims = (((0 if ta else 1,), (1 if tb else 0,)), ((), ()))

    in_specs = [a_spec, b_spec]
    args = [a, b]
    if epilogue == "relu2_bwd":
        in_specs.append(pl.BlockSpec((tm, tn), lambda i, j, k: (i, j)))
        args.append(extra)
    if epilogue == "relu2":
        out_shape = [jax.ShapeDtypeStruct(o_shape, BF16), jax.ShapeDtypeStruct((n, m), BF16)]
        out_specs = [o_spec, pl.BlockSpec((tn, tm), lambda i, j, k: (j, i))]
    else:
        out_shape = jax.ShapeDtypeStruct(o_shape, out_dtype)
        out_specs = o_spec

    def finish(refs, acc):
        if epilogue == "relu2":
            r = jnp.maximum(acc, 0.0)
            r2 = (r * r).astype(BF16)
            refs[2][...] = r2
            refs[3][...] = r2.T
        elif epilogue == "relu2_bwd":
            refs[3][...] = (acc * (2.0 * jnp.sqrt(refs[2][...]).astype(F32))).astype(out_dtype)
        else:
            refs[2][...] = acc.astype(out_dtype)

    def body(*refs):
        part = lax.dot_general(refs[0][...], refs[1][...], dims, preferred_element_type=F32)
        if nk == 1:
            finish(refs, part)
            return
        acc_ref = refs[-1]
        k = pl.program_id(2)

        @pl.when(k == 0)
        def _():
            acc_ref[...] = part

        @pl.when(jnp.logical_and(k > 0, k < nk - 1))
        def _():
            acc_ref[...] += part

        @pl.when(k == nk - 1)
        def _():
            finish(refs, acc_ref[...] + part)

    outs, couts = _call(
        body, name=name, grid=(m // tm, n // tn, nk), in_specs=in_specs, out_specs=out_specs, out_shape=out_shape,
        args=args, scratch_shapes=[] if nk == 1 else [pltpu.VMEM((tm, tn), F32)],
        sem=("parallel", "parallel", "arbitrary"), comm=comm)
    return outs if comm is None else (outs, couts)


def _row_specs(d, nt):
    row = pl.BlockSpec((TM, d), lambda t: (t, 0))
    vec = pl.BlockSpec((1, d), lambda t: (0, 0))
    mod = pl.BlockSpec((None, None, MOD_ROWS, d), lambda t: (t // nt, jnp.minimum(t % nt, 1), 0, 0))
    return row, vec, mod


def _rms(x):
    r = lax.rsqrt(jnp.mean(x * x, axis=1, keepdims=True) + EPS)
    return r, x * r


def _token_stream(ctx, x, nt, name, comm=None):
    bl, c_rows, d = ctx.shape
    m = bl * nt * TM

    def body(ctx_ref, x_ref, h_ref):
        t = pl.program_id(0)

        @pl.when(t % nt == 0)
        def _():
            h_ref[...] = ctx_ref[...]

        @pl.when(t % nt != 0)
        def _():
            h_ref[...] = x_ref[...]

    outs, couts = _call(
        body, name=name, grid=(m // TM,),
        in_specs=[pl.BlockSpec((None, TM, d), lambda t: (t // nt, 0, 0)),
                  pl.BlockSpec((None, TM, d), lambda t: (t // nt, jnp.maximum(t % nt, 1) - 1, 0))],
        out_specs=pl.BlockSpec((TM, d), lambda t: (t, 0)), out_shape=jax.ShapeDtypeStruct((m, d), F32),
        args=[ctx, x], sem=("parallel",), comm=comm)
    return outs if comm is None else (outs, couts)


def _norm_mod_fwd(h, g, modsel, i_sh, i_sc, nt, name, comm=None):
    m, d = h.shape
    row, vec, mod = _row_specs(d, nt)

    def body(h_ref, g_ref, ms_ref, u_ref, ut_ref):
        _, xh = _rms(h_ref[...])
        ms = ms_ref[...]
        u = (xh * g_ref[...] * (1.0 + ms[i_sc:i_sc + 1]) + ms[i_sh:i_sh + 1]).astype(BF16)
        u_ref[...] = u
        ut_ref[...] = u.T

    outs, couts = _call(
        body, name=name, grid=(m // TM,), in_specs=[row, vec, mod],
        out_specs=[row, pl.BlockSpec((d, TM), lambda t: (0, t))],
        out_shape=[jax.ShapeDtypeStruct((m, d), BF16), jax.ShapeDtypeStruct((d, m), BF16)],
        args=[h, g, modsel], sem=("parallel",), comm=comm)
    return outs if comm is None else (outs, couts)


def _acc_rows(t, nt, dvec_ref, rows):
    first = (t % nt) <= 1

    @pl.when(first)
    def _():
        dvec_ref[...] = rows

    @pl.when(jnp.logical_not(first))
    def _():
        dvec_ref[...] += rows


def _norm_mod_bwd(h, g, modsel, du, dh_in, i_sh, i_sc, nt, name, comm=None, latent_only=False):
    m, d = h.shape
    row, vec, mod = _row_specs(d, nt)
    dh_rows, dh_spec = m, row
    if latent_only:
        dh_rows = m // nt * (nt - 1)
        dh_spec = pl.BlockSpec((TM, d), lambda t: ((t // nt) * (nt - 1) + jnp.maximum(t % nt, 1) - 1, 0))

    def body(h_ref, g_ref, ms_ref, du_ref, dhi_ref, dh_ref, dvec_ref):
        t = pl.program_id(0)
        r, xh = _rms(h_ref[...])
        g_ = g_ref[...]
        ms = ms_ref[...]
        du_ = du_ref[...]
        y = xh * g_
        dy = du_ * (1.0 + ms[i_sc:i_sc + 1])
        dxh = dy * g_
        dx = r * (dxh - xh * jnp.mean(dxh * xh, axis=1, keepdims=True))
        dh_ref[...] = dhi_ref[...] + dx
        rows = jnp.concatenate([
            jnp.sum(du_, axis=0, keepdims=True), jnp.sum(du_ * y, axis=0, keepdims=True),
            jnp.sum(dy * xh, axis=0, keepdims=True), jnp.zeros((MOD_ROWS - 3, d), F32)], axis=0)
        _acc_rows(t, nt, dvec_ref, rows)

    outs, couts = _call(
        body, name=name, grid=(m // TM,), in_specs=[row, vec, mod, row, row], out_specs=[dh_spec, mod],
        out_shape=[jax.ShapeDtypeStruct((dh_rows, d), F32), jax.ShapeDtypeStruct(modsel.shape, F32)],
        args=[h, g, modsel, du, dh_in], sem=("arbitrary",), comm=comm)
    return outs if comm is None else (outs, couts)


def _gate_rows(dh_, xh, g_, gate):
    dy = dh_ * gate
    return dy * g_, jnp.sum(dh_ * (xh * g_), axis=0, keepdims=True), jnp.sum(dy * xh, axis=0, keepdims=True)


def _gate_mod_fwd(h, z, g_post, ms_gate, i_g, g_pre, ms_mod, i_sh, i_sc, nt, name):
    m, d = h.shape
    row, vec, mod = _row_specs(d, nt)

    def body(h_ref, z_ref, gp_ref, msg_ref, gq_ref, msm_ref, hn_ref, u_ref, ut_ref):
        _, zh = _rms(z_ref[...])
        hn = h_ref[...] + msg_ref[...][i_g:i_g + 1] * (zh * gp_ref[...])
        hn_ref[...] = hn
        _, xh = _rms(hn)
        ms = msm_ref[...]
        u = (xh * gq_ref[...] * (1.0 + ms[i_sc:i_sc + 1]) + ms[i_sh:i_sh + 1]).astype(BF16)
        u_ref[...] = u
        ut_ref[...] = u.T

    return pl.pallas_call(
        body, name=name, grid=(m // TM,), in_specs=[row, row, vec, mod, vec, mod],
        out_specs=[row, row, pl.BlockSpec((d, TM), lambda t: (0, t))],
        out_shape=[jax.ShapeDtypeStruct((m, d), F32), jax.ShapeDtypeStruct((m, d), BF16),
                   jax.ShapeDtypeStruct((d, m), BF16)],
        compiler_params=_params(("parallel",)),
    )(h, z, g_post, ms_gate, g_pre, ms_mod)


def _mod_gate_bwd(h, g_pre, ms_mod, i_sh, i_sc, du, dh_in, z, g_post, ms_gate, i_g, nt, name):
    m, d = h.shape
    row, vec, mod = _row_specs(d, nt)

    def body(h_ref, gq_ref, msm_ref, du_ref, dhi_ref, z_ref, gp_ref, msg_ref, dh_ref, dvm_ref, dz_ref, dvg_ref):
        t = pl.program_id(0)
        r, xh = _rms(h_ref[...])
        gq = gq_ref[...]
        ms = msm_ref[...]
        du_ = du_ref[...]
        dy = du_ * (1.0 + ms[i_sc:i_sc + 1])
        dxh = dy * gq
        dh_ = dhi_ref[...] + r * (dxh - xh * jnp.mean(dxh * xh, axis=1, keepdims=True))
        dh_ref[...] = dh_
        _acc_rows(t, nt, dvm_ref, jnp.concatenate([
            jnp.sum(du_, axis=0, keepdims=True), jnp.sum(du_ * (xh * gq), axis=0, keepdims=True),
            jnp.sum(dy * xh, axis=0, keepdims=True), jnp.zeros((MOD_ROWS - 3, d), F32)], axis=0))
        rz, zh = _rms(z_ref[...])
        dzh, d_gate, d_gp = _gate_rows(dh_, zh, gp_ref[...], msg_ref[...][i_g:i_g + 1])
        dz_ref[...] = (rz * (dzh - zh * jnp.mean(dzh * zh, axis=1, keepdims=True))).astype(BF16)
        _acc_rows(t, nt, dvg_ref, jnp.concatenate([d_gate, d_gp, jnp.zeros((MOD_ROWS - 2, d), F32)], axis=0))

    return pl.pallas_call(
        body, name=name, grid=(m // TM,), in_specs=[row, vec, mod, row, row, row, vec, mod],
        out_specs=[row, mod, row, mod],
        out_shape=[jax.ShapeDtypeStruct((m, d), F32), jax.ShapeDtypeStruct(ms_mod.shape, F32),
                   jax.ShapeDtypeStruct((m, d), BF16), jax.ShapeDtypeStruct(ms_gate.shape, F32)],
        compiler_params=_params(("arbitrary",)),
    )(h, g_pre, ms_mod, du, dh_in, z, g_post, ms_gate)


def _gate_loss_bwd(h, z, g_post, modsel, i_g, target, nt, name):
    m, d = h.shape
    row, vec, mod = _row_specs(d, nt)
    ntl = nt - 1
    tgt = pl.BlockSpec((TM, d), lambda t: ((t // nt) * ntl + jnp.maximum(t % nt, 1) - 1, 0))
    acc = pl.BlockSpec((8, 128), lambda t: (0, 0))

    def body(h_ref, z_ref, gp_ref, ms_ref, t_ref, dh_ref, dz_ref, dvg_ref, ss_ref):
        t = pl.program_id(0)

        @pl.when(t == 0)
        def _():
            ss_ref[...] = jnp.zeros_like(ss_ref)

        latent = (t % nt != 0).astype(F32)
        rz, zh = _rms(z_ref[...])
        gp = gp_ref[...]
        gate = ms_ref[...][i_g:i_g + 1]
        e = h_ref[...] + gate * (zh * gp) - t_ref[...]
        ss_ref[...] += latent * jnp.sum(e * e)
        dh_ = e * (latent / d)
        dh_ref[...] = dh_
        dzh, d_gate, d_gp = _gate_rows(dh_, zh, gp, gate)
        dz_ref[...] = (rz * (dzh - zh * jnp.mean(dzh * zh, axis=1, keepdims=True))).astype(BF16)
        _acc_rows(t, nt, dvg_ref, jnp.concatenate([d_gate, d_gp, jnp.zeros((MOD_ROWS - 2, d), F32)], axis=0))

    return pl.pallas_call(
        body, name=name, grid=(m // TM,), in_specs=[row, row, vec, mod, tgt], out_specs=[row, row, mod, acc],
        out_shape=[jax.ShapeDtypeStruct((m, d), F32), jax.ShapeDtypeStruct((m, d), BF16),
                   jax.ShapeDtypeStruct(modsel.shape, F32), jax.ShapeDtypeStruct((8, 128), F32)],
        compiler_params=_params(("arbitrary",)),
    )(h, z, g_post, modsel, target)


QA, KA, VA, QB, KB, VB = 0, 512, 640, 768, 1280, 1408
PROJ_W = 1536
Q_SCALE = HEAD_DIM ** -0.5
LOG2E = 1.4426950408889634


def _swap16(x):
    lane = lax.broadcasted_iota(jnp.int32, x.shape, 1)
    n = x.shape[1]
    return jnp.where((lane % 32) < 16, pltpu.roll(x, n - 16, 1), pltpu.roll(x, 16, 1))


def _seg_mean(x, e):
    hi = x.astype(BF16)
    lo = (x - hi.astype(F32)).astype(BF16)
    w = e.shape[0]
    both = lambda a: jnp.dot(hi[:, a:a + w], e, preferred_element_type=F32) + jnp.dot(lo[:, a:a + w], e, preferred_element_type=F32)
    parts = [both(a) for a in range(0, x.shape[1], w)]
    return parts[0] if len(parts) == 1 else jnp.concatenate(parts, axis=1)


def _rope_tables(t_rows, c_rows):
    s = t_rows - c_rows
    row_ids = jnp.repeat(jnp.arange(s // GRID_W, dtype=jnp.int32), GRID_W).astype(F32)
    col_ids = jnp.tile(jnp.arange(GRID_W, dtype=jnp.int32), s // GRID_W).astype(F32)
    axis_dim = HEAD_DIM // 2
    inv = ROPE_THETA ** (-jnp.arange(0, axis_dim, 2, dtype=F32) / axis_dim)
    ang_r = row_ids[:, None] * inv[None, :]
    ang_c = col_ids[:, None] * inv[None, :]
    cos = jnp.concatenate([jnp.cos(ang_r), jnp.cos(ang_r), jnp.cos(ang_c), jnp.cos(ang_c)], axis=1)
    sin = jnp.concatenate([-jnp.sin(ang_r), jnp.sin(ang_r), -jnp.sin(ang_c), jnp.sin(ang_c)], axis=1)
    cos = jnp.concatenate([jnp.ones((c_rows, HEAD_DIM), F32), cos], axis=0)
    sin = jnp.concatenate([jnp.zeros((c_rows, HEAD_DIM), F32), sin], axis=0)
    return jnp.tile(cos, (1, 8)), jnp.tile(sin, (1, 8))


def _head_mean_matrix():
    i = np.arange(512)
    return jnp.asarray((i[:, None] // HEAD_DIM == i[None, :] // HEAD_DIM).astype(np.float32) / HEAD_DIM, dtype=BF16)


def _interleave_kv(k, v):
    return jnp.concatenate([k[:, :64], v[:, :64], k[:, 64:], v[:, 64:]], axis=1)


def _prep_fwd(proj, qn, kn, cos, sin, emat, nt, name):
    m = proj.shape[0]
    specs = [
        pl.BlockSpec((TM, PROJ_W), lambda t: (t, 0)),
        pl.BlockSpec((1, 512), lambda t: (0, 0)), pl.BlockSpec((1, 128), lambda t: (0, 0)),
        pl.BlockSpec((TM, 512), lambda t: (t % nt, 0)), pl.BlockSpec((TM, 512), lambda t: (t % nt, 0)),
        pl.BlockSpec((512, 512), lambda t: (0, 0)),
    ]

    def body(p_ref, qn_ref, kn_ref, cos_ref, sin_ref, e_ref, q_ref, kv_ref):
        cos_, sin_, e = cos_ref[...], sin_ref[...], e_ref[...]

        def rope(x, w):
            return x * cos_[:, :w] + _swap16(x) * sin_[:, :w]

        def norm(x, g, w):
            return x * lax.rsqrt(_seg_mean(x * x, e[:min(w, 256), :min(w, 256)]) + EPS) * g

        qa = rope(norm(p_ref[:, QA:QA + 512], qn_ref[...], 512), 512)
        qb = rope(p_ref[:, QB:QB + 512], 512)
        q_ref[:, 0:512] = (qa * (Q_SCALE * LOG2E)).astype(BF16)
        q_ref[:, 512:1024] = (qb * (Q_SCALE * LOG2E)).astype(BF16)
        ka = rope(norm(p_ref[:, KA:KA + 128], kn_ref[...], 128), 128)
        kb = rope(p_ref[:, KB:KB + 128], 128)
        kv_ref[:, 0:256] = _interleave_kv(ka, p_ref[:, VA:VA + 128]).astype(BF16)
        kv_ref[:, 256:512] = _interleave_kv(kb, p_ref[:, VB:VB + 128]).astype(BF16)

    return pl.pallas_call(
        body, name=name, grid=(m // TM,), in_specs=specs,
        out_specs=[pl.BlockSpec((TM, 1024), lambda t: (t, 0)), pl.BlockSpec((TM, 512), lambda t: (t, 0))],
        out_shape=[jax.ShapeDtypeStruct((m, 1024), BF16), jax.ShapeDtypeStruct((m, 512), BF16)],
        compiler_params=_params(("parallel",)),
    )(proj, qn, kn, cos, sin, emat)


def _prep_bwd(proj, dq, dkv, qn, kn, cos, sin, emat, nt, name, comm=None):
    m = proj.shape[0]
    specs = [
        pl.BlockSpec((TM, PROJ_W), lambda t: (t, 0)),
        pl.BlockSpec((TM, 1024), lambda t: (t, 0)), pl.BlockSpec((TM, 512), lambda t: (t, 0)),
        pl.BlockSpec((1, 512), lambda t: (0, 0)), pl.BlockSpec((1, 128), lambda t: (0, 0)),
        pl.BlockSpec((TM, 512), lambda t: (t % nt, 0)), pl.BlockSpec((TM, 512), lambda t: (t % nt, 0)),
        pl.BlockSpec((512, 512), lambda t: (0, 0)),
    ]

    def body(p_ref, dq_ref, dkv_ref, qn_ref, kn_ref, cos_ref, sin_ref, e_ref, dp_ref, dqn_ref, dkn_ref):
        t = pl.program_id(0)
        cos_, sin_, e = cos_ref[...], sin_ref[...], e_ref[...]

        @pl.when(t == 0)
        def _():
            dqn_ref[...] = jnp.zeros_like(dqn_ref)
            dkn_ref[...] = jnp.zeros_like(dkn_ref)

        def unrope(dy, w):
            return dy * cos_[:, :w] + _swap16(dy * sin_[:, :w])

        def norm_bwd(x, g, dy, w):
            r = lax.rsqrt(_seg_mean(x * x, e[:min(w, 256), :min(w, 256)]) + EPS)
            xh = x * r
            dxh = dy * g
            dx = r * (dxh - xh * _seg_mean(dxh * xh, e[:min(w, 256), :min(w, 256)]))
            return dx, jnp.sum(dy * xh, axis=0, keepdims=True)

        dqa, dgq = norm_bwd(p_ref[:, QA:QA + 512], qn_ref[...], unrope(dq_ref[:, 0:512] * Q_SCALE, 512), 512)
        dp_ref[:, QA:QA + 512] = dqa.astype(BF16)
        dp_ref[:, QB:QB + 512] = unrope(dq_ref[:, 512:1024] * Q_SCALE, 512).astype(BF16)
        da = dkv_ref[:, 0:256]
        db = dkv_ref[:, 256:512]
        dka = jnp.concatenate([da[:, 0:64], da[:, 128:192]], axis=1)
        dva = jnp.concatenate([da[:, 64:128], da[:, 192:256]], axis=1)
        dkb = jnp.concatenate([db[:, 0:64], db[:, 128:192]], axis=1)
        dvb = jnp.concatenate([db[:, 64:128], db[:, 192:256]], axis=1)
        dka, dgk = norm_bwd(p_ref[:, KA:KA + 128], kn_ref[...], unrope(dka, 128), 128)
        dp_ref[:, KA:KA + 128] = dka.astype(BF16)
        dp_ref[:, VA:VA + 128] = dva.astype(BF16)
        dp_ref[:, KB:KB + 128] = unrope(dkb, 128).astype(BF16)
        dp_ref[:, VB:VB + 128] = dvb.astype(BF16)
        dqn_ref[0:1, :] += dgq
        dkn_ref[0:1, :] += dgk

    outs, couts = _call(
        body, name=name, grid=(m // TM,), in_specs=specs,
        out_specs=[pl.BlockSpec((TM, PROJ_W), lambda t: (t, 0)), pl.BlockSpec((8, 512), lambda t: (0, 0)),
                   pl.BlockSpec((8, 128), lambda t: (0, 0))],
        out_shape=[jax.ShapeDtypeStruct((m, PROJ_W), BF16), jax.ShapeDtypeStruct((8, 512), F32),
                   jax.ShapeDtypeStruct((8, 128), F32)],
        args=[proj, dq, dkv, qn, kn, cos, sin, emat], sem=("arbitrary",), comm=comm)
    return outs if comm is None else (outs, couts)


def _attn_case(t, hg, kv_ref, c_rows, t_rows, fn, keys_first=False):
    wl = TM + 2 * WINDOW
    kvd = lambda a, n: kv_ref[pl.ds(a, n), :]
    dense = hg < 2
    ctx = t == 0

    @pl.when(jnp.logical_and(dense, ctx))
    def _():
        kv = kvd(0, c_rows)
        fn(kv[:, :64], kv[:, 64:], None, False, [(0, c_rows)])

    @pl.when(jnp.logical_and(dense, jnp.logical_not(ctx)))
    def _():
        kv = kvd(0, t_rows)
        fn(kv[:, :64], kv[:, 64:], None, False, [(0, t_rows)])

    @pl.when(jnp.logical_and(jnp.logical_not(dense), ctx))
    def _():
        kv = kvd(0, c_rows)
        fn(kv[:, :64], kv[:, 64:], None, True, [(0, c_rows)])

    @pl.when(jnp.logical_and(jnp.logical_not(dense), jnp.logical_not(ctx)))
    def _():
        start = pl.multiple_of(jnp.minimum(c_rows + (t - 1) * TM - WINDOW, t_rows - wl), 128)
        kv = jnp.concatenate([kvd(0, c_rows), kvd(start, wl)], axis=0)
        shape = (c_rows + wl, TM) if keys_first else (TM, c_rows + wl)
        q_i = lax.broadcasted_iota(jnp.int32, shape, 1 if keys_first else 0)
        k_i = lax.broadcasted_iota(jnp.int32, shape, 0 if keys_first else 1)
        qpos = (t - 1) * TM + q_i
        kpos = start - 2 * c_rows + k_i
        mask = jnp.logical_or(k_i < c_rows, jnp.logical_and(jnp.abs(kpos - qpos) <= WINDOW, kpos >= 0))
        fn(kv[:, :64], kv[:, 64:], mask, True, [(0, c_rows), (start, wl)])


def _head_columns(cols):
    lane = lax.broadcasted_iota(jnp.int32, (TM, 128), 1)
    out = jnp.zeros((TM, 128), F32)
    for g, col in enumerate(cols):
        out = jnp.where(lane == g, col, out)
    return out


def _attn_specs(t_rows):
    nt = t_rows // TM
    q_spec = pl.BlockSpec((None, TM, Q_WIDTH), lambda b, hg, t, s: (b, t, hg))
    kv_spec = pl.BlockSpec((None, t_rows, 128), lambda b, hg, t, s: (b, 0, hg))
    lse_spec = pl.BlockSpec((None, TM, 128), lambda b, hg, t, s: (hg, b * nt + t, 0))
    return q_spec, kv_spec, lse_spec


def _attn_fwd(q_all, kv_all, sink8, c_rows, name, comm=None):
    bl, t_rows, _ = q_all.shape
    q_spec, kv_spec, lse_spec = _attn_specs(t_rows)

    def body(sink_ref, q_ref, kv_ref, o_ref, ot_ref, lse_ref):
        hg, t = pl.program_id(1), pl.program_id(2)

        def fn(k, v, mask, use_sink, spans):
            outs, lses = [], []
            v_one = jnp.concatenate([v, jnp.ones(v.shape, BF16)], axis=1)
            def scores(g):
                s = lax.dot_general(q_ref[:, g * 64:(g + 1) * 64], k, (((1,), (1,)), ((), ())),
                                    preferred_element_type=F32)
                return s if mask is None else jnp.where(mask, s, NEG_BIG)

            s_next = scores(0)
            for g in range(GROUP):
                s = s_next
                if g + 1 < GROUP:
                    s_next = scores(g + 1)
                mx = jnp.max(s, axis=1, keepdims=True)
                if use_sink:
                    sink = sink_ref[jnp.maximum(hg - 2, 0) * GROUP + g] * LOG2E
                    mx = jnp.maximum(mx, sink)
                pv = jnp.dot(jnp.exp2(s - mx).astype(BF16), v_one, preferred_element_type=F32)
                l = pv[:, 64:65]
                if use_sink:
                    l = l + jnp.exp2(sink - mx)
                outs.append(pv[:, :64] * (1.0 / l))
                lses.append(mx + jnp.log2(l))
            o = jnp.concatenate(outs, axis=1).astype(BF16)
            o_ref[...] = o
            ot_ref[...] = o.T
            lse_ref[...] = _head_columns(lses)

        _attn_case(t, hg, kv_ref, c_rows, t_rows, fn)

    nt = t_rows // TM
    ot_spec = pl.BlockSpec((Q_WIDTH, TM), lambda b, hg, t, s: (hg, b * nt + t))
    outs, couts = _call(
        body, name=name, grid=(bl, N_HG, nt), in_specs=[q_spec, kv_spec], out_specs=[q_spec, ot_spec, lse_spec],
        out_shape=[jax.ShapeDtypeStruct(q_all.shape, BF16), jax.ShapeDtypeStruct((N_HG * Q_WIDTH, bl * t_rows), BF16),
                   jax.ShapeDtypeStruct((N_HG, bl * t_rows, 128), F32)],
        args=[sink8, q_all, kv_all], prefetch=1, sem=("parallel", "parallel", "arbitrary"), comm=comm)
    return outs if comm is None else (outs, couts)


def _attn_bwd(q_all, kv_all, do, o, lse, sink8, c_rows, name, comm=None):
    bl, t_rows, _ = q_all.shape
    q_spec, kv_spec, lse_spec = _attn_specs(t_rows)
    ds_spec = pl.BlockSpec((None, None, 8, 128), lambda b, hg, t, s: (b, hg, 0, 0))

    def body(sink_ref, q_ref, kv_ref, do_ref, o_ref, lse_ref, dq_ref, dkv_ref, dsk_ref):
        hg, t = pl.program_id(1), pl.program_id(2)

        @pl.when(t == 0)
        def _():
            dkv_ref[...] = jnp.zeros_like(dkv_ref)
            dsk_ref[...] = jnp.zeros_like(dsk_ref)

        lse_rows = lse_ref[...].T
        dd_cols = [jnp.sum(do_ref[:, g * 64:(g + 1) * 64].astype(F32) * o_ref[:, g * 64:(g + 1) * 64].astype(F32),
                           axis=1, keepdims=True) for g in range(GROUP)]
        dd_rows = _head_columns(dd_cols).T

        def fn(k, v, mask, use_sink, spans):
            k_t = k.T
            dq_t, dsinks = [], []
            dk = jnp.zeros(k.shape, F32)
            dv = jnp.zeros(v.shape, F32)
            def products(g):
                s = lax.dot_general(k, q_ref[:, g * 64:(g + 1) * 64], (((1,), (1,)), ((), ())),
                                    preferred_element_type=F32)
                dp = lax.dot_general(v, do_ref[:, g * 64:(g + 1) * 64], (((1,), (1,)), ((), ())),
                                     preferred_element_type=F32)
                return (s if mask is None else jnp.where(mask, s, NEG_BIG)), dp

            nxt = products(0)
            for g in range(GROUP):
                q = q_ref[:, g * 64:(g + 1) * 64]
                do_g = do_ref[:, g * 64:(g + 1) * 64]
                lse_g, dd_g = lse_rows[g:g + 1, :], dd_rows[g:g + 1, :]
                s, dp = nxt
                if g + 1 < GROUP:
                    nxt = products(g + 1)
                pb = jnp.exp2(s - lse_g).astype(BF16)
                ds = (pb.astype(F32) * (dp - dd_g)).astype(BF16)
                dk = dk + jnp.dot(ds, q, preferred_element_type=F32)
                dv = dv + jnp.dot(pb, do_g, preferred_element_type=F32)
                dq_t.append(jnp.dot(k_t, ds, preferred_element_type=F32))
                if use_sink:
                    p_sink = jnp.exp2(sink_ref[jnp.maximum(hg - 2, 0) * GROUP + g] * LOG2E - lse_g)
                    dsinks.append(jnp.broadcast_to(-jnp.sum(p_sink * dd_g, axis=1, keepdims=True), (1, 128)))
            dq_ref[...] = jnp.concatenate(dq_t, axis=0).T
            dkv = jnp.concatenate([dk * (1.0 / LOG2E), dv], axis=1)
            off = 0
            for start, size in spans:
                dkv_ref[pl.ds(start, size), :] += dkv[off:off + size]
                off += size
            if use_sink:
                dsk_ref[0:GROUP, :] += jnp.concatenate(dsinks, axis=0)

        _attn_case(t, hg, kv_ref, c_rows, t_rows, fn, keys_first=True)

    outs, couts = _call(
        body, name=name, grid=(bl, N_HG, t_rows // TM), in_specs=[q_spec, kv_spec, q_spec, q_spec, lse_spec],
        out_specs=[q_spec, kv_spec, ds_spec],
        out_shape=[jax.ShapeDtypeStruct(q_all.shape, F32), jax.ShapeDtypeStruct(kv_all.shape, F32),
                   jax.ShapeDtypeStruct((bl, N_HG, 8, 128), F32)],
        args=[sink8, q_all, kv_all, do, o, lse], prefetch=1, sem=("parallel", "parallel", "arbitrary"), comm=comm)
    return outs if comm is None else (outs, couts)


def _silu(x):
    return x * jax.nn.sigmoid(x)


def _ada_fwd(c_rows, w_ada, b_cols, name, comm=None):
    nl, d, w = w_ada.shape
    r = c_rows.shape[0]

    def body(c_ref, w_ref, b_ref, o_ref):
        s = _silu(c_ref[...]).astype(BF16)
        o_ref[...] = jnp.dot(s, w_ref[...].astype(BF16), preferred_element_type=F32) + b_ref[...]

    outs, couts = _call(
        body, name=name, grid=(nl,),
        in_specs=[pl.BlockSpec((r, d), lambda l: (0, 0)), pl.BlockSpec((None, d, w), lambda l: (l, 0, 0)),
                  pl.BlockSpec((None, 1, w), lambda l: (l, 0, 0))],
        out_specs=pl.BlockSpec((None, r, w), lambda l: (l, 0, 0)),
        out_shape=jax.ShapeDtypeStruct((nl, r, w), F32), args=[c_rows, w_ada, b_cols], sem=("parallel",), comm=comm)
    return outs if comm is None else (outs, couts)


def _ada_bwd(c_rows, c_ctx, dmod, w_ada, name):
    nl, d, w = w_ada.shape
    r = c_rows.shape[0]

    def body(c_ref, cc_ref, g_ref, w_ref, dw_ref, dc_ref):
        l = pl.program_id(0)
        s = _silu(c_ref[...]).astype(BF16)
        gm = g_ref[...].astype(BF16)
        dw_ref[...] = lax.dot_general(s, gm, (((0,), (0,)), ((), ())), preferred_element_type=F32)
        ds = lax.dot_general(gm, w_ref[...].astype(BF16), (((1,), (1,)), ((), ())), preferred_element_type=F32)
        rows = lax.broadcasted_iota(jnp.int32, ds.shape, 0)
        dsc = jnp.sum(jnp.where(rows % ADA_ROWS == 2, ds, 0.0), axis=0, keepdims=True)
        x = cc_ref[...]
        sg = jax.nn.sigmoid(x)
        dcc = dsc * (sg * (1.0 + x * (1.0 - sg)))
        out = jnp.concatenate([dcc, jnp.zeros((7, d), F32)], axis=0)

        @pl.when(l == 0)
        def _():
            dc_ref[...] = out

        @pl.when(l != 0)
        def _():
            dc_ref[...] += out

    return pl.pallas_call(
        body, name=name, grid=(nl,),
        in_specs=[pl.BlockSpec((r, d), lambda l: (0, 0)), pl.BlockSpec((1, d), lambda l: (0, 0)),
                  pl.BlockSpec((None, r, w), lambda l: (l, 0, 0)), pl.BlockSpec((None, d, w), lambda l: (l, 0, 0))],
        out_specs=[pl.BlockSpec((None, d, w), lambda l: (l, 0, 0)), pl.BlockSpec((8, d), lambda l: (0, 0))],
        out_shape=[jax.ShapeDtypeStruct((nl, d, w), F32), jax.ShapeDtypeStruct((8, d), F32)],
        compiler_params=_params(("arbitrary",)),
    )(c_rows, c_ctx, dmod, w_ada)


def _adam_math(w, g, m, v):
    m = ADAM_B1 * m + (1.0 - ADAM_B1) * g
    v = ADAM_B2 * v + (1.0 - ADAM_B2) * (g * g)
    m_hat = m / (1.0 - ADAM_B1 ** ADAM_STEP)
    v_hat = v / (1.0 - ADAM_B2 ** ADAM_STEP)
    delta = -ADAM_LR * (m_hat / (jnp.sqrt(v_hat) + ADAM_EPS) + ADAM_WD * w)
    return delta, m, v


def _adamw(w, m, v, g_own, g_recv, name, rows=256):
    nl, r, c = w.shape
    tr = min(rows, r)
    spec = pl.BlockSpec((None, tr, c), lambda l, i: (l, i, 0))
    per_layer = g_own is None
    own = [] if per_layer else [g_own]
    recv = [] if g_recv is None else list(g_recv)
    n_i = r // tr

    def rows_of(li):
        return lambda l, i: jnp.where(l == li, i, jnp.where(l > li, n_i - 1, 0))

    in_specs = [spec] * (3 + len(own))
    in_specs += [pl.BlockSpec((N_DEV, tr, c), lambda l, i, f=rows_of(li): (0, f(l, i), 0)) for li in range(len(recv))]

    def body(*refs):
        w_ref, m_ref, v_ref = refs[:3]
        own_refs, recv_refs = refs[3:3 + len(own)], refs[3 + len(own):3 + len(own) + len(recv)]
        go_ref, d_ref, mo_ref, vo_ref = refs[-4:]

        def update(li):
            if recv:
                g = recv_refs[li][0].astype(F32)
                for k in range(1, N_DEV):
                    g = g + recv_refs[li][k].astype(F32)
            else:
                g = own_refs[0][...]
            delta, m_, v_ = _adam_math(w_ref[...], g, m_ref[...], v_ref[...])
            go_ref[...] = g
            d_ref[...] = delta
            mo_ref[...] = m_
            vo_ref[...] = v_

        if per_layer:
            for li in range(nl):
                pl.when(pl.program_id(0) == li)(functools.partial(update, li))
        else:
            update(0)

    return pl.pallas_call(
        body, name=name, grid=(nl, n_i), in_specs=in_specs, out_specs=[spec] * 4,
        out_shape=[jax.ShapeDtypeStruct(w.shape, F32)] * 4, compiler_params=_params(("parallel", "parallel")),
    )(w, m, v, *own, *recv)


def _small_adamw(w, m, v, g_all, name):
    def body(w_ref, m_ref, v_ref, g_ref, go_ref, d_ref, mo_ref, vo_ref):
        g = g_ref[0]
        for k in range(1, N_DEV):
            g = g + g_ref[k]
        delta, m_, v_ = _adam_math(w_ref[...], g, m_ref[...], v_ref[...])
        go_ref[...] = g
        d_ref[...] = delta
        mo_ref[...] = m_
        vo_ref[...] = v_

    return pl.pallas_call(
        body, name=name, out_shape=[jax.ShapeDtypeStruct(w.shape, F32)] * 4, compiler_params=_params(),
    )(w, m, v, g_all)


SMALL = ("c_ctx", "b_ada", "g_pre_mix", "g_post_mix", "g_pre_mlp", "g_post_mlp", "q_norm", "k_norm", "sink", "loss")


def _pack_small(parts):
    flat = jnp.concatenate([parts[n].reshape(-1) for n in SMALL])
    rows = -(-flat.shape[0] // 1024) * 8
    return jnp.pad(flat, (0, rows * 128 - flat.shape[0])).reshape(rows, 128)


def _unpack_small(packed, like):
    flat = packed.reshape(-1)
    out, off = {}, 0
    for n in SMALL:
        size = int(np.prod(like[n].shape))
        out[n] = flat[off:off + size].reshape(like[n].shape)
        off += size
    return out


def kernel(x, c, ctx, c_ctx, w_ada, b_ada, g_pre_mix, g_post_mix, g_pre_mlp, g_post_mlp, w_in, q_norm, k_norm, sink, w_out, w_up, w_down, loss_target, m_c_ctx, m_w_ada, m_b_ada, m_g_pre_mix, m_g_post_mix, m_g_pre_mlp, m_g_post_mlp, m_w_in, m_q_norm, m_k_norm, m_sink, m_w_out, m_w_up, m_w_down, v_c_ctx, v_w_ada, v_b_ada, v_g_pre_mix, v_g_post_mix, v_g_pre_mlp, v_g_post_mlp, v_w_in, v_q_norm, v_k_norm, v_sink, v_w_out, v_w_up, v_w_down):
    bl, s_rows, d = x.shape
    c_rows = ctx.shape[1]
    assert c_rows == TM and s_rows % TM == 0 and bl == 2
    t_rows = c_rows + s_rows
    nt = t_rows // TM
    m_rows = bl * t_rows
    nl = w_in.shape[0]
    ada_w = w_ada.shape[2]
    d_ff = w_up.shape[2] * N_DEV
    me = _my_index()

    shard = lambda w_, l: w_[l].astype(BF16)
    c_pad = jnp.concatenate([c, c_ctx[None, :], jnp.zeros((ADA_ROWS - bl - 1, d), F32)], axis=0)
    gathered = {0: {}}
    h, (c_all, gathered[0]["w_in"]) = _token_stream(
        ctx, x, nt, "token_stream", comm=_Comm([(c_pad, GATHER_VIA_SIBLING), (shard(w_in, 0), GATHER_VIA_SIBLING)]))
    c_all = c_all.reshape(N_DEV * ADA_ROWS, d)

    def layer_weights(l):
        g_ = gathered[l]
        w_out_f = g_["w_out"].reshape(-1, d)
        w_down_f = g_["w_down"].reshape(d_ff, d)
        return dict(
            w_in_f=g_["w_in"].transpose(1, 0, 2).reshape(d, PROJ_W), w_out_f=w_out_f,
            w_up_s=g_["w_up"], w_up_t=g_["w_up"].transpose(0, 2, 1).reshape(d_ff, d), w_down_f=w_down_f)

    big = dict(tm=2304, tn=512)
    deep = dict(tm=1536, tn=512, tk=d_ff)
    wide = dict(tm=1024, tn=512, tk=m_rows)

    b_cols = lax.dynamic_slice(b_ada, (0, me * ada_w), (nl, ada_w))[:, None, :]
    mod_cols = _ada_fwd(c_all, w_ada, b_cols, "ada_fwd")
    mod_slots = mod_cols.reshape(nl, N_DEV, ADA_ROWS, ada_w).transpose(1, 0, 2, 3)
    mod_g, = _comm_only(_Comm([(mod_slots, TO_OWNER)]), "exchange_mod")
    mine = mod_g.transpose(1, 2, 0, 3).reshape(nl, ADA_ROWS, N_MOD, d)
    pad = jnp.zeros((bl, 2, MOD_ROWS - N_MOD, d), F32)
    modsel = [jnp.concatenate([jnp.stack([jnp.broadcast_to(mine[l, bl], (bl, N_MOD, d)), mine[l, :bl]], axis=1), pad],
                              axis=2) for l in range(nl)]

    cos, sin = _rope_tables(t_rows, c_rows)
    emat = _head_mean_matrix()
    row = lambda a: a[None, :]
    qn = [jnp.tile(q_norm[l], 8)[None, :] for l in range(nl)]
    kn = [jnp.tile(k_norm[l], 2)[None, :] for l in range(nl)]

    target = loss_target.reshape(bl * s_rows, d)
    saved = []
    weights_of = {}
    (u, u_t), (gathered[0]["w_out"],) = _norm_mod_fwd(
        h, row(g_pre_mix[0]), modsel[0], 0, 1, nt, "mix_mod_fwd0", comm=_Comm([(shard(w_out, 0), GATHER_VIA_SIBLING)]))
    for l in range(nl):
        w_in_f = gathered[l]["w_in"].transpose(1, 0, 2).reshape(d, PROJ_W)
        proj = _mm(u, w_in_f, name=f"mm_in{l}", tk=d, **big)
        q_all, kv_all = _prep_fwd(proj, qn[l], kn[l], cos, sin, emat, nt, f"prep_fwd{l}")
        (o, o_t, lse), (w_up_g, w_down_g) = _attn_fwd(
            q_all.reshape(bl, t_rows, 1024), kv_all.reshape(bl, t_rows, 512), sink[l], c_rows, f"attn_fwd{l}",
            comm=_Comm([(shard(w_up, l), GATHER), (shard(w_down, l), GATHER)]))
        o = o.reshape(m_rows, 1024)
        gathered[l].update(w_up=w_up_g, w_down=w_down_g)
        wl = weights_of[l] = layer_weights(l)
        mix = _mm(o, wl["w_out_f"], name=f"mm_out{l}", tk=1024, **big)
        h_mid, v_in, v_t = _gate_mod_fwd(h, mix, row(g_post_mix[l]), modsel[l], 2, row(g_pre_mlp[l]), modsel[l], 3, 4,
                                         nt, f"mix_gate_mlp_mod_fwd{l}")
        more = l + 1 < nl
        res_up = _mm(v_in, wl["w_up_s"], name=f"mm_up{l}", b_mode="nn_slots", epilogue="relu2", tk=d,
                     comm=_Comm([(shard(w_in, l + 1), GATHER)]) if more else None, **big)
        (r_act, r_t), nxt_in = res_up if more else (res_up, None)
        res_down = _mm(r_act, wl["w_down_f"], name=f"mm_down{l}",
                       comm=_Comm([(shard(w_out, l + 1), GATHER)]) if more else None, **deep)
        y, nxt_out = res_down if more else (res_down, None)
        saved.append((h, u_t, proj, q_all, kv_all, o, o_t, lse, mix, h_mid, v_t, r_act, r_t, y))
        if more:
            gathered[l + 1] = dict(w_in=nxt_in[0], w_out=nxt_out[0])
            h, u, u_t = _gate_mod_fwd(h_mid, y, row(g_post_mlp[l]), modsel[l], 5, row(g_pre_mix[l + 1]), modsel[l + 1],
                                      0, 1, nt, f"mlp_gate_mix_mod_fwd{l}")

    dh, dy, dvec_g2, ss = _gate_loss_bwd(h_mid, y, row(g_post_mlp[nl - 1]), modsel[nl - 1], 5, target, nt, "gate_loss_bwd")
    small_g = {n: [None] * nl for n in SMALL if n not in ("c_ctx", "b_ada", "loss")}
    dvecs = {l: {} for l in range(nl)}
    dvecs[nl - 1]["g2"] = dvec_g2
    slots = {n: [None] * nl for n in ("w_in", "w_out", "w_up", "w_down")}
    recvd = {n: [None] * nl for n in slots}
    send = lambda n, l_: (slots[n][l_], TO_OWNER_XOR)
    for l in reversed(range(nl)):
        h_in, u_t, proj, q_all, kv_all, o, o_t, lse, mix, h_mid, v_t, r_act, r_t, y = saved[l]
        wl = weights_of[l]
        da = _mm(dy, wl["w_down_f"], name=f"mm_da{l}", b_mode="nt", epilogue="relu2_bwd", extra=r_act, out_dtype=BF16,
                 tk=d, **big)
        dw_down = _mm(r_t, dy, name=f"mm_dw_down{l}", out_dtype=BF16, **wide)
        dw_up = _mm(v_t, da, name=f"mm_dw_up{l}", out_mode="slots", out_dtype=BF16, **wide)
        slots["w_down"][l] = dw_down.reshape(N_DEV, -1, d)
        slots["w_up"][l] = dw_up
        dv = _mm(da, wl["w_up_t"], name=f"mm_dv{l}", **deep)
        dh, dvecs[l]["m2"], dmix, dvecs[l]["g1"] = _mod_gate_bwd(
            h_mid, row(g_pre_mlp[l]), modsel[l], 3, 4, dv, dh, mix, row(g_post_mix[l]), modsel[l], 2, nt,
            f"mlp_mod_mix_gate_bwd{l}")
        do = _mm(dmix, wl["w_out_f"], name=f"mm_do{l}", b_mode="nt", out_dtype=BF16, tk=d, **big)
        dw_out = _mm(o_t, dmix, name=f"mm_dw_out{l}", out_dtype=BF16, **wide)
        slots["w_out"][l] = dw_out.reshape(N_DEV, -1, d)
        going = [("w_down", l), ("w_up", l)] + ([("w_out", l + 1), ("w_in", l + 1)] if l + 1 < nl else [])
        (dq, dkv, dsk), arrived = _attn_bwd(
            q_all.reshape(bl, t_rows, 1024), kv_all.reshape(bl, t_rows, 512), do.reshape(bl, t_rows, 1024),
            o.reshape(bl, t_rows, 1024), lse, sink[l], c_rows, f"attn_bwd{l}", comm=_Comm([send(*g_) for g_ in going]))
        for (n_, l_), got in zip(going, arrived):
            recvd[n_][l_] = got
        last = l == 0
        res = _prep_bwd(proj, dq.reshape(m_rows, 1024), dkv.reshape(m_rows, 512), qn[l], kn[l], cos, sin, emat, nt,
                        f"prep_bwd{l}", comm=_Comm([send("w_out", l)]) if last else None)
        dproj, dqn, dkn = res[0] if last else res
        if last:
            recvd["w_out"][l] = res[1][0]
        dw_in = _mm(u_t, dproj, name=f"mm_dw_in{l}", out_dtype=BF16, **wide)
        slots["w_in"][l] = dw_in.reshape(d, N_DEV, PROJ_W // N_DEV).transpose(1, 0, 2)
        du = _mm(dproj, wl["w_in_f"], name=f"mm_du{l}", b_mode="nt", tk=PROJ_W, **big)
        if last:
            (dh, dvecs[l]["m1"]), (recvd["w_in"][l],) = _norm_mod_bwd(
                h_in, row(g_pre_mix[l]), modsel[l], du, dh, 0, 1, nt, f"mix_mod_bwd{l}", latent_only=True,
                comm=_Comm([send("w_in", l)]))
        else:
            dh, dvecs[l]["m1"], dy, dvecs[l - 1]["g2"] = _mod_gate_bwd(
                h_in, row(g_pre_mix[l]), modsel[l], 0, 1, du, dh, saved[l - 1][-1], row(g_post_mlp[l - 1]),
                modsel[l - 1], 5, nt, f"mix_mod_mlp_gate_bwd{l}")
        small_g["q_norm"][l] = jnp.sum(dqn[0].reshape(8, HEAD_DIM), axis=0)
        small_g["k_norm"][l] = jnp.sum(dkn[0].reshape(2, HEAD_DIM), axis=0)
        small_g["sink"][l] = jnp.sum(dsk[:, 2:, :GROUP, 0], axis=0).reshape(-1)

    dmod_rows = []
    for l in range(nl):
        m1, g1, m2, g2 = (dvecs[l][n] for n in ("m1", "g1", "m2", "g2"))
        small_g["g_pre_mix"][l] = jnp.sum(m1[:, :, 2], axis=(0, 1))
        small_g["g_post_mix"][l] = jnp.sum(g1[:, :, 1], axis=(0, 1))
        small_g["g_pre_mlp"][l] = jnp.sum(m2[:, :, 2], axis=(0, 1))
        small_g["g_post_mlp"][l] = jnp.sum(g2[:, :, 1], axis=(0, 1))
        dms = jnp.stack([m1[:, :, 0], m1[:, :, 1], g1[:, :, 0], m2[:, :, 0], m2[:, :, 1], g2[:, :, 0]], axis=2)
        rows = jnp.concatenate([dms[:, 1], jnp.sum(dms[:, 0], axis=0)[None]], axis=0)
        dmod_rows.append(jnp.pad(rows.reshape(bl + 1, N_MOD * d), ((0, ADA_ROWS - bl - 1), (0, 0))))
    grad_x = dh.reshape(bl, s_rows, d)

    dmod_slots = jnp.stack(dmod_rows).reshape(nl, ADA_ROWS, N_DEV, ada_w).transpose(2, 0, 1, 3)
    dmod_g, = _comm_only(_Comm([(dmod_slots, TO_OWNER)]), "exchange_dmod")
    dmod_mine = dmod_g.transpose(1, 0, 2, 3).reshape(nl, N_DEV * ADA_ROWS, ada_w)
    dw_ada, dcc = _ada_bwd(c_all, c_ctx[None, :], dmod_mine, w_ada, "ada_bwd")

    parts = {n: jnp.stack(small_g[n]) for n in small_g}
    parts["c_ctx"] = dcc[0]
    parts["loss"] = ss[0, 0:1]
    no_loss = jnp.zeros((1,), F32)
    parts["b_ada"] = jnp.stack([jnp.sum(r_[: bl + 1], axis=0) for r_ in dmod_rows])
    weights = dict(c_ctx=c_ctx, b_ada=b_ada, g_pre_mix=g_pre_mix, g_post_mix=g_post_mix, g_pre_mlp=g_pre_mlp,
                   g_post_mlp=g_post_mlp, q_norm=q_norm, k_norm=k_norm, sink=sink, loss=no_loss)
    moms = dict(c_ctx=m_c_ctx, b_ada=m_b_ada, g_pre_mix=m_g_pre_mix, g_post_mix=m_g_post_mix, g_pre_mlp=m_g_pre_mlp,
                g_post_mlp=m_g_post_mlp, q_norm=m_q_norm, k_norm=m_k_norm, sink=m_sink, loss=no_loss)
    vels = dict(c_ctx=v_c_ctx, b_ada=v_b_ada, g_pre_mix=v_g_pre_mix, g_post_mix=v_g_post_mix, g_pre_mlp=v_g_pre_mlp,
                g_post_mlp=v_g_post_mlp, q_norm=v_q_norm, k_norm=v_k_norm, sink=v_sink, loss=no_loss)
    small_all, = _comm_only(_Comm([(_pack_small(parts), GATHER)]), "gather_small")
    s_out = _small_adamw(_pack_small(weights), _pack_small(moms), _pack_small(vels), small_all, "adamw_small")
    s_g, s_d, s_m, s_v = [_unpack_small(a, weights) for a in s_out]
    loss = 0.5 * s_g["loss"][0] / d

    res = {}
    for n, w_, m_, v_ in (("w_in", w_in, m_w_in, v_w_in), ("w_out", w_out, m_w_out, v_w_out),
                          ("w_up", w_up, m_w_up, v_w_up), ("w_down", w_down, m_w_down, v_w_down)):
        res[n] = _adamw(w_, m_, v_, None, recvd[n], f"adamw_{n}")
    res["w_ada"] = _adamw(w_ada, m_w_ada, v_w_ada, dw_ada, None, "adamw_w_ada")

    order = ("c_ctx", "w_ada", "b_ada", "g_pre_mix", "g_post_mix", "g_pre_mlp", "g_post_mlp", "w_in", "q_norm",
             "k_norm", "sink", "w_out", "w_up", "w_down")
    outs = [loss, grad_x]
    for i, small in enumerate((s_g, s_d, s_m, s_v)):
        outs += [small[n] if n in small else res[n][i] for n in order]
    return tuple(outs)
```

```python
import functools

import jax
import jax.numpy as jnp
import numpy as np
from jax import lax
from jax.experimental import pallas as pl
from jax.experimental.pallas import tpu as pltpu

F32 = jnp.float32
BF16 = jnp.bfloat16

HEAD_DIM = 64
GROUP = 4
N_HG = 4
Q_WIDTH = GROUP * HEAD_DIM
WINDOW = 128
GRID_W = 64
ROPE_THETA = 10000.0
EPS = 1e-6
NEG_BIG = -1e30
N_MOD = 6
MOD_ROWS = 8
TM = 256
N_DEV = 8
ADA_ROWS = 8
VMEM_LIMIT = 56 * 1024 * 1024

ADAM_LR = 0.001
ADAM_B1 = 0.9
ADAM_B2 = 0.999
ADAM_EPS = 1e-08
ADAM_WD = 0.01
ADAM_STEP = 10


def _params(sem=None):
    kw = dict(vmem_limit_bytes=VMEM_LIMIT)
    if sem is not None:
        kw["dimension_semantics"] = sem
    return pltpu.CompilerParams(**kw)


def _my_index():
    return 4 * lax.axis_index("x") + 2 * lax.axis_index("y") + lax.axis_index("c")


def _peer(k):
    x, y, c = lax.axis_index("x"), lax.axis_index("y"), lax.axis_index("c")
    kx, ky, kc = (k >> 2) & 1, (k >> 1) & 1, k & 1
    px = (1 - x) if kx else x
    py = (1 - y) if ky else y
    pc = (1 - c) if kc else c
    return (px, py, pc), 4 * px + 2 * py + pc


GATHER, GATHER_VIA_SIBLING, TO_OWNER, TO_OWNER_XOR = "gather", "gather_via_sibling", "to_owner", "to_owner_xor"


class _Comm:
    def __init__(self, items):
        self.items = list(items)
        self.arrays = [a for a, _ in self.items]

    def out_shapes(self):
        return [jax.ShapeDtypeStruct(((N_DEV,) + a.shape) if kind in (GATHER, GATHER_VIA_SIBLING) else a.shape, a.dtype)
                for a, kind in self.items]

    def sem_shapes(self):
        n = len(self.items) * N_DEV
        return [pltpu.SemaphoreType.DMA((n,)), pltpu.SemaphoreType.DMA((n,))]

    def _two_level(self, x_ref, o_ref, send_sems, recv_sems, base):
        x, y, c = lax.axis_index("x"), lax.axis_index("y"), lax.axis_index("c")
        me, sibling = (x, y, c), (x, y, 1 - c)
        chips = [(1 - x, y), (x, 1 - y), (1 - x, 1 - y)]
        block = lambda p: o_ref.at[4 * p[0] + 2 * p[1] + p[2]]

        def copy(k, owner, to, src=None):
            return pltpu.make_async_remote_copy(
                src_ref=block(owner) if src is None else src, dst_ref=block(owner), send_sem=send_sems.at[base + k],
                recv_sem=recv_sems.at[base + k], device_id=to, device_id_type=pl.DeviceIdType.MESH)

        first = [copy(1, me, sibling, src=x_ref)] + [copy(2 + j, me, (*chip, c), src=x_ref) for j, chip in enumerate(chips)]
        relay = [(copy(2 + j, (*chip, c), me), copy(5 + j, (*chip, c), sibling)) for j, chip in enumerate(chips)]
        last = [copy(1, sibling, me)] + [copy(5 + j, (*chip, 1 - c), me) for j, chip in enumerate(chips)]
        return first, relay, last

    def _copies(self, in_refs, out_refs, send_sems, recv_sems):
        me = _my_index()
        local, remote = [], []
        for i, ((_, kind), x_ref, o_ref) in enumerate(zip(self.items, in_refs, out_refs)):
            base = i * N_DEV
            if kind == GATHER_VIA_SIBLING:
                local.append(pltpu.make_async_copy(x_ref, o_ref.at[me], send_sems.at[base]))
                continue
            own_src = x_ref if kind == GATHER else x_ref.at[me]
            own_dst = o_ref.at[0] if kind == TO_OWNER_XOR else o_ref.at[me]
            local.append(pltpu.make_async_copy(own_src, own_dst, send_sems.at[base]))
            for k in range(1, N_DEV):
                peer, pidx = _peer(k)
                remote.append(pltpu.make_async_remote_copy(
                    src_ref=x_ref if kind == GATHER else x_ref.at[pidx],
                    dst_ref=o_ref.at[k] if kind == TO_OWNER_XOR else o_ref.at[me],
                    send_sem=send_sems.at[base + k], recv_sem=recv_sems.at[base + k],
                    device_id=peer, device_id_type=pl.DeviceIdType.MESH))
        return local, remote

    def start(self, in_refs, out_refs, send_sems, recv_sems):
        local, remote = self._copies(in_refs, out_refs, send_sems, recv_sems)
        for cp in local + remote:
            cp.start()
        for i, ((_, kind), x_ref, o_ref) in enumerate(zip(self.items, in_refs, out_refs)):
            if kind == GATHER_VIA_SIBLING:
                for cp in self._two_level(x_ref, o_ref, send_sems, recv_sems, i * N_DEV)[0]:
                    cp.start()

    def wait(self, in_refs, out_refs, send_sems, recv_sems):
        local, remote = self._copies(in_refs, out_refs, send_sems, recv_sems)
        for cp in remote:
            cp.wait_recv()
        for cp in remote:
            cp.wait_send()
        for i, ((_, kind), x_ref, o_ref) in enumerate(zip(self.items, in_refs, out_refs)):
            if kind == GATHER_VIA_SIBLING:
                first, relay, last = self._two_level(x_ref, o_ref, send_sems, recv_sems, i * N_DEV)
                for arrival, onward in relay:
                    arrival.wait_recv()
                    onward.start()
                for cp in last:
                    cp.wait_recv()
                for cp in first + [onward for _, onward in relay]:
                    cp.wait_send()
        for cp in local:
            cp.wait()


def _call(body, *, name, grid, in_specs, out_specs, out_shape, args, scratch_shapes=(), prefetch=0, sem=None,
          comm=None):
    single = not isinstance(out_shape, (list, tuple))
    out_shape = [out_shape] if single else list(out_shape)
    out_specs = [out_specs] if single else list(out_specs)
    in_specs, scratch_shapes, args = list(in_specs), list(scratch_shapes), list(args)
    n_in, n_out = len(in_specs), len(out_shape)
    if comm is not None:
        nc = len(comm.arrays)
        hbm = pl.BlockSpec(memory_space=pl.ANY)
        inner = body

        def body(*refs):
            pre, r = refs[:prefetch], refs[prefetch:]
            ins, cin = r[:n_in], r[n_in:n_in + nc]
            outs, cout = r[n_in + nc:n_in + nc + n_out], r[n_in + nc + n_out:n_in + 2 * nc + n_out]
            scr, sems = r[n_in + 2 * nc + n_out:len(r) - 2], r[len(r) - 2:]
            ids = [pl.program_id(i) for i in range(len(grid))]

            def when(flags, fn):
                if flags:
                    pl.when(functools.reduce(jnp.logical_and, flags))(fn)
                else:
                    fn()

            when([i == 0 for i in ids], lambda: comm.start(cin, cout, *sems))
            inner(*pre, *ins, *outs, *scr)
            when([i == n - 1 for i, n in zip(ids, grid)], lambda: comm.wait(cin, cout, *sems))

        in_specs += [hbm] * nc
        out_specs += [hbm] * nc
        out_shape += comm.out_shapes()
        scratch_shapes += comm.sem_shapes()
        args += comm.arrays
        sem = ("arbitrary",) * len(grid)
    kw = dict(name=name, out_shape=out_shape, compiler_params=_params(sem if grid else None))
    if prefetch:
        kw["grid_spec"] = pltpu.PrefetchScalarGridSpec(
            num_scalar_prefetch=prefetch, grid=grid, in_specs=in_specs, out_specs=out_specs,
            scratch_shapes=scratch_shapes)
    else:
        kw.update(in_specs=in_specs, out_specs=out_specs, scratch_shapes=scratch_shapes)
        if grid:
            kw["grid"] = grid
    res = list(pl.pallas_call(body, **kw)(*args))
    outs = res[:n_out]
    return (outs[0] if single else outs), res[n_out:]


def _comm_only(comm, name):
    return _call(lambda: None, name=name, grid=(), in_specs=[], out_specs=[], out_shape=[], args=[], comm=comm)[1]


def _mm(a, b, *, name, ta=False, b_mode="nn", out_mode="plain", out_dtype=F32, tm=512, tn=512, tk=512,
        epilogue=None, extra=None, comm=None):
    if ta:
        kdim, m = a.shape
    else:
        m, kdim = a.shape
    if b_mode == "nn":
        n = b.shape[1]
    elif b_mode == "nt":
        n = b.shape[0]
    elif b_mode == "nn_slots":
        n = b.shape[0] * b.shape[2]
        tn = b.shape[2]
    else:
        n = b.shape[1]
        tk = b.shape[2]
    if out_mode == "slots":
        tn = n // N_DEV
    tm, tn, tk = min(tm, m), min(tn, n), min(tk, kdim)
    assert m % tm == 0 and n % tn == 0 and kdim % tk == 0, (name, m, n, kdim, tm, tn, tk)
    nk = kdim // tk

    a_spec = pl.BlockSpec((tk, tm), lambda i, j, k: (k, i)) if ta else pl.BlockSpec((tm, tk), lambda i, j, k: (i, k))
    if b_mode == "nn":
        b_spec = pl.BlockSpec((tk, tn), lambda i, j, k: (k, j))
    elif b_mode == "nt":
        b_spec = pl.BlockSpec((tn, tk), lambda i, j, k: (j, k))
    elif b_mode == "nn_slots":
        b_spec = pl.BlockSpec((None, tk, tn), lambda i, j, k: (j, k, 0))
    else:
        b_spec = pl.BlockSpec((None, tn, tk), lambda i, j, k: (k, j, 0))
    tb = b_mode in ("nt", "nt_slots")
    if out_mode == "plain":
        o_shape, o_spec = (m, n), pl.BlockSpec((tm, tn), lambda i, j, k: (i, j))
    else:
        o_shape, o_spec = (N_DEV, m, tn), pl.BlockSpec((None, tm, tn), lambda i, j, k: (j, i, 0))
    dims = (((0 if ta else 1,), (1 if tb else 0,)), ((), ()))

    in_specs = [a_spec, b_spec]
    args = [a, b]
    if epilogue == "relu2_bwd":
        in_specs.append(pl.BlockSpec((tm, tn), lambda i, j, k: (i, j)))
        args.append(extra)
    if epilogue == "relu2":
        out_shape = [jax.ShapeDtypeStruct(o_shape, BF16), jax.ShapeDtypeStruct((n, m), BF16)]
        out_specs = [o_spec, pl.BlockSpec((tn, tm), lambda i, j, k: (j, i))]
    else:
        out_shape = jax.ShapeDtypeStruct(o_shape, out_dtype)
        out_specs = o_spec

    def finish(refs, acc):
        if epilogue == "relu2":
            r = jnp.maximum(acc, 0.0)
            r2 = (r * r).astype(BF16)
            refs[2][...] = r2
            refs[3][...] = r2.T
        elif epilogue == "relu2_bwd":
            refs[3][...] = (acc * (2.0 * jnp.sqrt(refs[2][...]).astype(F32))).astype(out_dtype)
        else:
            refs[2][...] = acc.astype(out_dtype)

    def body(*refs):
        part = lax.dot_general(refs[0][...], refs[1][...], dims, preferred_element_type=F32)
        if nk == 1:
            finish(refs, part)
            return
        acc_ref = refs[-1]
        k = pl.program_id(2)

        @pl.when(k == 0)
        def _():
            acc_ref[...] = part

        @pl.when(jnp.logical_and(k > 0, k < nk - 1))
        def _():
            acc_ref[...] += part

        @pl.when(k == nk - 1)
        def _():
            finish(refs, acc_ref[...] + part)

    outs, couts = _call(
        body, name=name, grid=(m // tm, n // tn, nk), in_specs=in_specs, out_specs=out_specs, out_shape=out_shape,
        args=args, scratch_shapes=[] if nk == 1 else [pltpu.VMEM((tm, tn), F32)],
        sem=("parallel", "parallel", "arbitrary"), comm=comm)
    return outs if comm is None else (outs, couts)


def _row_specs(d, nt):
    row = pl.BlockSpec((TM, d), lambda t: (t, 0))
    vec = pl.BlockSpec((1, d), lambda t: (0, 0))
    mod = pl.BlockSpec((None, None, MOD_ROWS, d), lambda t: (t // nt, jnp.minimum(t % nt, 1), 0, 0))
    return row, vec, mod


def _rms(x):
    r = lax.rsqrt(jnp.mean(x * x, axis=1, keepdims=True) + EPS)
    return r, x * r


def _token_stream(ctx, x, nt, name, comm=None):
    bl, c_rows, d = ctx.shape
    m = bl * nt * TM

    def body(ctx_ref, x_ref, h_ref):
        t = pl.program_id(0)

        @pl.when(t % nt == 0)
        def _():
            h_ref[...] = ctx_ref[...]

        @pl.when(t % nt != 0)
        def _():
            h_ref[...] = x_ref[...]

    outs, couts = _call(
        body, name=name, grid=(m // TM,),
        in_specs=[pl.BlockSpec((None, TM, d), lambda t: (t // nt, 0, 0)),
                  pl.BlockSpec((None, TM, d), lambda t: (t // nt, jnp.maximum(t % nt, 1) - 1, 0))],
        out_specs=pl.BlockSpec((TM, d), lambda t: (t, 0)), out_shape=jax.ShapeDtypeStruct((m, d), F32),
        args=[ctx, x], sem=("parallel",), comm=comm)
    return outs if comm is None else (outs, couts)


def _norm_mod_fwd(h, g, modsel, i_sh, i_sc, nt, name, comm=None):
    m, d = h.shape
    row, vec, mod = _row_specs(d, nt)

    def body(h_ref, g_ref, ms_ref, u_ref, ut_ref):
        _, xh = _rms(h_ref[...])
        ms = ms_ref[...]
        u = (xh * g_ref[...] * (1.0 + ms[i_sc:i_sc + 1]) + ms[i_sh:i_sh + 1]).astype(BF16)
        u_ref[...] = u
        ut_ref[...] = u.T

    outs, couts = _call(
        body, name=name, grid=(m // TM,), in_specs=[row, vec, mod],
        out_specs=[row, pl.BlockSpec((d, TM), lambda t: (0, t))],
        out_shape=[jax.ShapeDtypeStruct((m, d), BF16), jax.ShapeDtypeStruct((d, m), BF16)],
        args=[h, g, modsel], sem=("parallel",), comm=comm)
    return outs if comm is None else (outs, couts)


def _acc_rows(t, nt, dvec_ref, rows):
    first = (t % nt) <= 1

    @pl.when(first)
    def _():
        dvec_ref[...] = rows

    @pl.when(jnp.logical_not(first))
    def _():
        dvec_ref[...] += rows


def _norm_mod_bwd(h, g, modsel, du, dh_in, i_sh, i_sc, nt, name, comm=None, latent_only=False):
    m, d = h.shape
    row, vec, mod = _row_specs(d, nt)
    dh_rows, dh_spec = m, row
    if latent_only:
        dh_rows = m // nt * (nt - 1)
        dh_spec = pl.BlockSpec((TM, d), lambda t: ((t // nt) * (nt - 1) + jnp.maximum(t % nt, 1) - 1, 0))

    def body(h_ref, g_ref, ms_ref, du_ref, dhi_ref, dh_ref, dvec_ref):
        t = pl.program_id(0)
        r, xh = _rms(h_ref[...])
        g_ = g_ref[...]
        ms = ms_ref[...]
        du_ = du_ref[...]
        y = xh * g_
        dy = du_ * (1.0 + ms[i_sc:i_sc + 1])
        dxh = dy * g_
        dx = r * (dxh - xh * jnp.mean(dxh * xh, axis=1, keepdims=True))
        dh_ref[...] = dhi_ref[...] + dx
        rows = jnp.concatenate([
            jnp.sum(du_, axis=0, keepdims=True), jnp.sum(du_ * y, axis=0, keepdims=True),
            jnp.sum(dy * xh, axis=0, keepdims=True), jnp.zeros((MOD_ROWS - 3, d), F32)], axis=0)
        _acc_rows(t, nt, dvec_ref, rows)

    outs, couts = _call(
        body, name=name, grid=(m // TM,), in_specs=[row, vec, mod, row, row], out_specs=[dh_spec, mod],
        out_shape=[jax.ShapeDtypeStruct((dh_rows, d), F32), jax.ShapeDtypeStruct(modsel.shape, F32)],
        args=[h, g, modsel, du, dh_in], sem=("arbitrary",), comm=comm)
    return outs if comm is None else (outs, couts)


def _gate_rows(dh_, xh, g_, gate):
    dy = dh_ * gate
    return dy * g_, jnp.sum(dh_ * (xh * g_), axis=0, keepdims=True), jnp.sum(dy * xh, axis=0, keepdims=True)


def _gate_mod_fwd(h, z, g_post, ms_gate, i_g, g_pre, ms_mod, i_sh, i_sc, nt, name):
    m, d = h.shape
    row, vec, mod = _row_specs(d, nt)

    def body(h_ref, z_ref, gp_ref, msg_ref, gq_ref, msm_ref, hn_ref, u_ref, ut_ref):
        _, zh = _rms(z_ref[...])
        hn = h_ref[...] + msg_ref[...][i_g:i_g + 1] * (zh * gp_ref[...])
        hn_ref[...] = hn
        _, xh = _rms(hn)
        ms = msm_ref[...]
        u = (xh * gq_ref[...] * (1.0 + ms[i_sc:i_sc + 1]) + ms[i_sh:i_sh + 1]).astype(BF16)
        u_ref[...] = u
        ut_ref[...] = u.T

    return pl.pallas_call(
        body, name=name, grid=(m // TM,), in_specs=[row, row, vec, mod, vec, mod],
        out_specs=[row, row, pl.BlockSpec((d, TM), lambda t: (0, t))],
        out_shape=[jax.ShapeDtypeStruct((m, d), F32), jax.ShapeDtypeStruct((m, d), BF16),
                   jax.ShapeDtypeStruct((d, m), BF16)],
        compiler_params=_params(("parallel",)),
    )(h, z, g_post, ms_gate, g_pre, ms_mod)


def _mod_gate_bwd(h, g_pre, ms_mod, i_sh, i_sc, du, dh_in, z, g_post, ms_gate, i_g, nt, name):
    m, d = h.shape
    row, vec, mod = _row_specs(d, nt)

    def body(h_ref, gq_ref, msm_ref, du_ref, dhi_ref, z_ref, gp_ref, msg_ref, dh_ref, dvm_ref, dz_ref, dvg_ref):
        t = pl.program_id(0)
        r, xh = _rms(h_ref[...])
        gq = gq_ref[...]
        ms = msm_ref[...]
        du_ = du_ref[...]
        dy = du_ * (1.0 + ms[i_sc:i_sc + 1])
        dxh = dy * gq
        dh_ = dhi_ref[...] + r * (dxh - xh * jnp.mean(dxh * xh, axis=1, keepdims=True))
        dh_ref[...] = dh_
        _acc_rows(t, nt, dvm_ref, jnp.concatenate([
            jnp.sum(du_, axis=0, keepdims=True), jnp.sum(du_ * (xh * gq), axis=0, keepdims=True),
            jnp.sum(dy * xh, axis=0, keepdims=True), jnp.zeros((MOD_ROWS - 3, d), F32)], axis=0))
        rz, zh = _rms(z_ref[...])
        dzh, d_gate, d_gp = _gate_rows(dh_, zh, gp_ref[...], msg_ref[...][i_g:i_g + 1])
        dz_ref[...] = (rz * (dzh - zh * jnp.mean(dzh * zh, axis=1, keepdims=True))).astype(BF16)
        _acc_rows(t, nt, dvg_ref, jnp.concatenate([d_gate, d_gp, jnp.zeros((MOD_ROWS - 2, d), F32)], axis=0))

    return pl.pallas_call(
        body, name=name, grid=(m // TM,), in_specs=[row, vec, mod, row, row, row, vec, mod],
        out_specs=[row, mod, row, mod],
        out_shape=[jax.ShapeDtypeStruct((m, d), F32), jax.ShapeDtypeStruct(ms_mod.shape, F32),
                   jax.ShapeDtypeStruct((m, d), BF16), jax.ShapeDtypeStruct(ms_gate.shape, F32)],
        compiler_params=_params(("arbitrary",)),
    )(h, g_pre, ms_mod, du, dh_in, z, g_post, ms_gate)


def _gate_loss_bwd(h, z, g_post, modsel, i_g, target, nt, name):
    m, d = h.shape
    row, vec, mod = _row_specs(d, nt)
    ntl = nt - 1
    tgt = pl.BlockSpec((TM, d), lambda t: ((t // nt) * ntl + jnp.maximum(t % nt, 1) - 1, 0))
    acc = pl.BlockSpec((8, 128), lambda t: (0, 0))

    def body(h_ref, z_ref, gp_ref, ms_ref, t_ref, dh_ref, dz_ref, dvg_ref, ss_ref):
        t = pl.program_id(0)

        @pl.when(t == 0)
        def _():
            ss_ref[...] = jnp.zeros_like(ss_ref)

        latent = (t % nt != 0).astype(F32)
        rz, zh = _rms(z_ref[...])
        gp = gp_ref[...]
        gate = ms_ref[...][i_g:i_g + 1]
        e = h_ref[...] + gate * (zh * gp) - t_ref[...]
        ss_ref[...] += latent * jnp.sum(e * e)
        dh_ = e * (latent / d)
        dh_ref[...] = dh_
        dzh, d_gate, d_gp = _gate_rows(dh_, zh, gp, gate)
        dz_ref[...] = (rz * (dzh - zh * jnp.mean(dzh * zh, axis=1, keepdims=True))).astype(BF16)
        _acc_rows(t, nt, dvg_ref, jnp.concatenate([d_gate, d_gp, jnp.zeros((MOD_ROWS - 2, d), F32)], axis=0))

    return pl.pallas_call(
        body, name=name, grid=(m // TM,), in_specs=[row, row, vec, mod, tgt], out_specs=[row, row, mod, acc],
        out_shape=[jax.ShapeDtypeStruct((m, d), F32), jax.ShapeDtypeStruct((m, d), BF16),
                   jax.ShapeDtypeStruct(modsel.shape, F32), jax.ShapeDtypeStruct((8, 128), F32)],
        compiler_params=_params(("arbitrary",)),
    )(h, z, g_post, modsel, target)


QA, KA, VA, QB, KB, VB = 0, 512, 640, 768, 1280, 1408
PROJ_W = 1536
Q_SCALE = HEAD_DIM ** -0.5
LOG2E = 1.4426950408889634


def _swap16(x):
    lane = lax.broadcasted_iota(jnp.int32, x.shape, 1)
    n = x.shape[1]
    return jnp.where((lane % 32) < 16, pltpu.roll(x, n - 16, 1), pltpu.roll(x, 16, 1))


def _seg_mean(x, e):
    hi = x.astype(BF16)
    lo = (x - hi.astype(F32)).astype(BF16)
    w = e.shape[0]
    both = lambda a: jnp.dot(hi[:, a:a + w], e, preferred_element_type=F32) + jnp.dot(lo[:, a:a + w], e, preferred_element_type=F32)
    parts = [both(a) for a in range(0, x.shape[1], w)]
    return parts[0] if len(parts) == 1 else jnp.concatenate(parts, axis=1)


def _rope_tables(t_rows, c_rows):
    s = t_rows - c_rows
    row_ids = jnp.repeat(jnp.arange(s // GRID_W, dtype=jnp.int32), GRID_W).astype(F32)
    col_ids = jnp.tile(jnp.arange(GRID_W, dtype=jnp.int32), s // GRID_W).astype(F32)
    axis_dim = HEAD_DIM // 2
    inv = ROPE_THETA ** (-jnp.arange(0, axis_dim, 2, dtype=F32) / axis_dim)
    ang_r = row_ids[:, None] * inv[None, :]
    ang_c = col_ids[:, None] * inv[None, :]
    cos = jnp.concatenate([jnp.cos(ang_r), jnp.cos(ang_r), jnp.cos(ang_c), jnp.cos(ang_c)], axis=1)
    sin = jnp.concatenate([-jnp.sin(ang_r), jnp.sin(ang_r), -jnp.sin(ang_c), jnp.sin(ang_c)], axis=1)
    cos = jnp.concatenate([jnp.ones((c_rows, HEAD_DIM), F32), cos], axis=0)
    sin = jnp.concatenate([jnp.zeros((c_rows, HEAD_DIM), F32), sin], axis=0)
    return jnp.tile(cos, (1, 8)), jnp.tile(sin, (1, 8))


def _head_mean_matrix():
    i = np.arange(512)
    return jnp.asarray((i[:, None] // HEAD_DIM == i[None, :] // HEAD_DIM).astype(np.float32) / HEAD_DIM, dtype=BF16)


def _interleave_kv(k, v):
    return jnp.concatenate([k[:, :64], v[:, :64], k[:, 64:], v[:, 64:]], axis=1)


def _prep_fwd(proj, qn, kn, cos, sin, emat, nt, name):
    m = proj.shape[0]
    specs = [
        pl.BlockSpec((TM, PROJ_W), lambda t: (t, 0)),
        pl.BlockSpec((1, 512), lambda t: (0, 0)), pl.BlockSpec((1, 128), lambda t: (0, 0)),
        pl.BlockSpec((TM, 512), lambda t: (t % nt, 0)), pl.BlockSpec((TM, 512), lambda t: (t % nt, 0)),
        pl.BlockSpec((512, 512), lambda t: (0, 0)),
    ]

    def body(p_ref, qn_ref, kn_ref, cos_ref, sin_ref, e_ref, q_ref, kv_ref):
        cos_, sin_, e = cos_ref[...], sin_ref[...], e_ref[...]

        def rope(x, w):
            return x * cos_[:, :w] + _swap16(x) * sin_[:, :w]

        def norm(x, g, w):
            return x * lax.rsqrt(_seg_mean(x * x, e[:min(w, 256), :min(w, 256)]) + EPS) * g

        qa = rope(norm(p_ref[:, QA:QA + 512], qn_ref[...], 512), 512)
        qb = rope(p_ref[:, QB:QB + 512], 512)
        q_ref[:, 0:512] = (qa * (Q_SCALE * LOG2E)).astype(BF16)
        q_ref[:, 512:1024] = (qb * (Q_SCALE * LOG2E)).astype(BF16)
        ka = rope(norm(p_ref[:, KA:KA + 128], kn_ref[...], 128), 128)
        kb = rope(p_ref[:, KB:KB + 128], 128)
        kv_ref[:, 0:256] = _interleave_kv(ka, p_ref[:, VA:VA + 128]).astype(BF16)
        kv_ref[:, 256:512] = _interleave_kv(kb, p_ref[:, VB:VB + 128]).astype(BF16)

    return pl.pallas_call(
        body, name=name, grid=(m // TM,), in_specs=specs,
        out_specs=[pl.BlockSpec((TM, 1024), lambda t: (t, 0)), pl.BlockSpec((TM, 512), lambda t: (t, 0))],
        out_shape=[jax.ShapeDtypeStruct((m, 1024), BF16), jax.ShapeDtypeStruct((m, 512), BF16)],
        compiler_params=_params(("parallel",)),
    )(proj, qn, kn, cos, sin, emat)


def _prep_bwd(proj, dq, dkv, qn, kn, cos, sin, emat, nt, name, comm=None):
    m = proj.shape[0]
    specs = [
        pl.BlockSpec((TM, PROJ_W), lambda t: (t, 0)),
        pl.BlockSpec((TM, 1024), lambda t: (t, 0)), pl.BlockSpec((TM, 512), lambda t: (t, 0)),
        pl.BlockSpec((1, 512), lambda t: (0, 0)), pl.BlockSpec((1, 128), lambda t: (0, 0)),
        pl.BlockSpec((TM, 512), lambda t: (t % nt, 0)), pl.BlockSpec((TM, 512), lambda t: (t % nt, 0)),
        pl.BlockSpec((512, 512), lambda t: (0, 0)),
    ]

    def body(p_ref, dq_ref, dkv_ref, qn_ref, kn_ref, cos_ref, sin_ref, e_ref, dp_ref, dqn_ref, dkn_ref):
        t = pl.program_id(0)
        cos_, sin_, e = cos_ref[...], sin_ref[...], e_ref[...]

        @pl.when(t == 0)
        def _():
            dqn_ref[...] = jnp.zeros_like(dqn_ref)
            dkn_ref[...] = jnp.zeros_like(dkn_ref)

        def unrope(dy, w):
            return dy * cos_[:, :w] + _swap16(dy * sin_[:, :w])

        def norm_bwd(x, g, dy, w):
            r = lax.rsqrt(_seg_mean(x * x, e[:min(w, 256), :min(w, 256)]) + EPS)
            xh = x * r
            dxh = dy * g
            dx = r * (dxh - xh * _seg_mean(dxh * xh, e[:min(w, 256), :min(w, 256)]))
            return dx, jnp.sum(dy * xh, axis=0, keepdims=True)

        dqa, dgq = norm_bwd(p_ref[:, QA:QA + 512], qn_ref[...], unrope(dq_ref[:, 0:512] * Q_SCALE, 512), 512)
        dp_ref[:, QA:QA + 512] = dqa.astype(BF16)
        dp_ref[:, QB:QB + 512] = unrope(dq_ref[:, 512:1024] * Q_SCALE, 512).astype(BF16)
        da = dkv_ref[:, 0:256]
        db = dkv_ref[:, 256:512]
        dka = jnp.concatenate([da[:, 0:64], da[:, 128:192]], axis=1)
        dva = jnp.concatenate([da[:, 64:128], da[:, 192:256]], axis=1)
        dkb = jnp.concatenate([db[:, 0:64], db[:, 128:192]], axis=1)
        dvb = jnp.concatenate([db[:, 64:128], db[:, 192:256]], axis=1)
        dka, dgk = norm_bwd(p_ref[:, KA:KA + 128], kn_ref[...], unrope(dka, 128), 128)
        dp_ref[:, KA:KA + 128] = dka.astype(BF16)
        dp_ref[:, VA:VA + 128] = dva.astype(BF16)
        dp_ref[:, KB:KB + 128] = unrope(dkb, 128).astype(BF16)
        dp_ref[:, VB:VB + 128] = dvb.astype(BF16)
        dqn_ref[0:1, :] += dgq
        dkn_ref[0:1, :] += dgk

    outs, couts = _call(
        body, name=name, grid=(m // TM,), in_specs=specs,
        out_specs=[pl.BlockSpec((TM, PROJ_W), lambda t: (t, 0)), pl.BlockSpec((8, 512), lambda t: (0, 0)),
                   pl.BlockSpec((8, 128), lambda t: (0, 0))],
        out_shape=[jax.ShapeDtypeStruct((m, PROJ_W), BF16), jax.ShapeDtypeStruct((8, 512), F32),
                   jax.ShapeDtypeStruct((8, 128), F32)],
        args=[proj, dq, dkv, qn, kn, cos, sin, emat], sem=("arbitrary",), comm=comm)
    return outs if comm is None else (outs, couts)


def _attn_case(t, hg, kv_ref, c_rows, t_rows, fn, keys_first=False):
    wl = TM + 2 * WINDOW
    kvd = lambda a, n: kv_ref[pl.ds(a, n), :]
    dense = hg < 2
    ctx = t == 0

    @pl.when(jnp.logical_and(dense, ctx))
    def _():
        kv = kvd(0, c_rows)
        fn(kv[:, :64], kv[:, 64:], None, False, [(0, c_rows)])

    @pl.when(jnp.logical_and(dense, jnp.logical_not(ctx)))
    def _():
        kv = kvd(0, t_rows)
        fn(kv[:, :64], kv[:, 64:], None, False, [(0, t_rows)])

    @pl.when(jnp.logical_and(jnp.logical_not(dense), ctx))
    def _():
        kv = kvd(0, c_rows)
        fn(kv[:, :64], kv[:, 64:], None, True, [(0, c_rows)])

    @pl.when(jnp.logical_and(jnp.logical_not(dense), jnp.logical_not(ctx)))
    def _():
        start = pl.multiple_of(jnp.minimum(c_rows + (t - 1) * TM - WINDOW, t_rows - wl), 128)
        kv = jnp.concatenate([kvd(0, c_rows), kvd(start, wl)], axis=0)
        shape = (c_rows + wl, TM) if keys_first else (TM, c_rows + wl)
        q_i = lax.broadcasted_iota(jnp.int32, shape, 1 if keys_first else 0)
        k_i = lax.broadcasted_iota(jnp.int32, shape, 0 if keys_first else 1)
        qpos = (t - 1) * TM + q_i
        kpos = start - 2 * c_rows + k_i
        mask = jnp.logical_or(k_i < c_rows, jnp.logical_and(jnp.abs(kpos - qpos) <= WINDOW, kpos >= 0))
        fn(kv[:, :64], kv[:, 64:], mask, True, [(0, c_rows), (start, wl)])


def _head_columns(cols):
    lane = lax.broadcasted_iota(jnp.int32, (TM, 128), 1)
    out = jnp.zeros((TM, 128), F32)
    for g, col in enumerate(cols):
        out = jnp.where(lane == g, col, out)
    return out


def _attn_specs(t_rows):
    nt = t_rows // TM
    q_spec = pl.BlockSpec((None, TM, Q_WIDTH), lambda b, hg, t, s: (b, t, hg))
    kv_spec = pl.BlockSpec((None, t_rows, 128), lambda b, hg, t, s: (b, 0, hg))
    lse_spec = pl.BlockSpec((None, TM, 128), lambda b, hg, t, s: (hg, b * nt + t, 0))
    return q_spec, kv_spec, lse_spec


def _attn_fwd(q_all, kv_all, sink8, c_rows, name, comm=None):
    bl, t_rows, _ = q_all.shape
    q_spec, kv_spec, lse_spec = _attn_specs(t_rows)

    def body(sink_ref, q_ref, kv_ref, o_ref, ot_ref, lse_ref):
        hg, t = pl.program_id(1), pl.program_id(2)

        def fn(k, v, mask, use_sink, spans):
            outs, lses = [], []
            v_one = jnp.concatenate([v, jnp.ones(v.shape, BF16)], axis=1)
            def scores(g):
                s = lax.dot_general(q_ref[:, g * 64:(g + 1) * 64], k, (((1,), (1,)), ((), ())),
                                    preferred_element_type=F32)
                return s if mask is None else jnp.where(mask, s, NEG_BIG)

            s_next = scores(0)
            for g in range(GROUP):
                s = s_next
                if g + 1 < GROUP:
                    s_next = scores(g + 1)
                mx = jnp.max(s, axis=1, keepdims=True)
                if use_sink:
                    sink = sink_ref[jnp.maximum(hg - 2, 0) * GROUP + g] * LOG2E
                    mx = jnp.maximum(mx, sink)
                pv = jnp.dot(jnp.exp2(s - mx).astype(BF16), v_one, preferred_element_type=F32)
                l = pv[:, 64:65]
                if use_sink:
                    l = l + jnp.exp2(sink - mx)
                outs.append(pv[:, :64] * (1.0 / l))
                lses.append(mx + jnp.log2(l))
            o = jnp.concatenate(outs, axis=1).astype(BF16)
            o_ref[...] = o
            ot_ref[...] = o.T
            lse_ref[...] = _head_columns(lses)

        _attn_case(t, hg, kv_ref, c_rows, t_rows, fn)

    nt = t_rows // TM
    ot_spec = pl.BlockSpec((Q_WIDTH, TM), lambda b, hg, t, s: (hg, b * nt + t))
    outs, couts = _call(
        body, name=name, grid=(bl, N_HG, nt), in_specs=[q_spec, kv_spec], out_specs=[q_spec, ot_spec, lse_spec],
        out_shape=[jax.ShapeDtypeStruct(q_all.shape, BF16), jax.ShapeDtypeStruct((N_HG * Q_WIDTH, bl * t_rows), BF16),
                   jax.ShapeDtypeStruct((N_HG, bl * t_rows, 128), F32)],
        args=[sink8, q_all, kv_all], prefetch=1, sem=("parallel", "parallel", "arbitrary"), comm=comm)
    return outs if comm is None else (outs, couts)


def _attn_bwd(q_all, kv_all, do, o, lse, sink8, c_rows, name, comm=None):
    bl, t_rows, _ = q_all.shape
    q_spec, kv_spec, lse_spec = _attn_specs(t_rows)
    ds_spec = pl.BlockSpec((None, None, 8, 128), lambda b, hg, t, s: (b, hg, 0, 0))

    def body(sink_ref, q_ref, kv_ref, do_ref, o_ref, lse_ref, dq_ref, dkv_ref, dsk_ref):
        hg, t = pl.program_id(1), pl.program_id(2)

        @pl.when(t == 0)
        def _():
            dkv_ref[...] = jnp.zeros_like(dkv_ref)
            dsk_ref[...] = jnp.zeros_like(dsk_ref)

        lse_rows = lse_ref[...].T
        dd_cols = [jnp.sum(do_ref[:, g * 64:(g + 1) * 64].astype(F32) * o_ref[:, g * 64:(g + 1) * 64].astype(F32),
                           axis=1, keepdims=True) for g in range(GROUP)]
        dd_rows = _head_columns(dd_cols).T

        def fn(k, v, mask, use_sink, spans):
            k_t = k.T
            dq_t, dsinks = [], []
            dk = jnp.zeros(k.shape, F32)
            dv = jnp.zeros(v.shape, F32)
            def products(g):
                s = lax.dot_general(k, q_ref[:, g * 64:(g + 1) * 64], (((1,), (1,)), ((), ())),
                                    preferred_element_type=F32)
                dp = lax.dot_general(v, do_ref[:, g * 64:(g + 1) * 64], (((1,), (1,)), ((), ())),
                                     preferred_element_type=F32)
                return (s if mask is None else jnp.where(mask, s, NEG_BIG)), dp

            nxt = products(0)
            for g in range(GROUP):
                q = q_ref[:, g * 64:(g + 1) * 64]
                do_g = do_ref[:, g * 64:(g + 1) * 64]
                lse_g, dd_g = lse_rows[g:g + 1, :], dd_rows[g:g + 1, :]
                s, dp = nxt
                if g + 1 < GROUP:
                    nxt = products(g + 1)
                pb = jnp.exp2(s - lse_g).astype(BF16)
                ds = (pb.astype(F32) * (dp - dd_g)).astype(BF16)
                dk = dk + jnp.dot(ds, q, preferred_element_type=F32)
                dv = dv + jnp.dot(pb, do_g, preferred_element_type=F32)
                dq_t.append(jnp.dot(k_t, ds, preferred_element_type=F32))
                if use_sink:
                    p_sink = jnp.exp2(sink_ref[jnp.maximum(hg - 2, 0) * GROUP + g] * LOG2E - lse_g)
                    dsinks.append(jnp.broadcast_to(-jnp.sum(p_sink * dd_g, axis=1, keepdims=True), (1, 128)))
            dq_ref[...] = jnp.concatenate(dq_t, axis=0).T
            dkv = jnp.concatenate([dk * (1.0 / LOG2E), dv], axis=1)
            off = 0
            for start, size in spans:
                dkv_ref[pl.ds(start, size), :] += dkv[off:off + size]
                off += size
            if use_sink:
                dsk_ref[0:GROUP, :] += jnp.concatenate(dsinks, axis=0)

        _attn_case(t, hg, kv_ref, c_rows, t_rows, fn, keys_first=True)

    outs, couts = _call(
        body, name=name, grid=(bl, N_HG, t_rows // TM), in_specs=[q_spec, kv_spec, q_spec, q_spec, lse_spec],
        out_specs=[q_spec, kv_spec, ds_spec],
        out_shape=[jax.ShapeDtypeStruct(q_all.shape, F32), jax.ShapeDtypeStruct(kv_all.shape, F32),
                   jax.ShapeDtypeStruct((bl, N_HG, 8, 128), F32)],
        args=[sink8, q_all, kv_all, do, o, lse], prefetch=1, sem=("parallel", "parallel", "arbitrary"), comm=comm)
    return outs if comm is None else (outs, couts)


def _silu(x):
    return x * jax.nn.sigmoid(x)


def _ada_fwd(c_rows, w_ada, b_cols, name, comm=None):
    nl, d, w = w_ada.shape
    r = c_rows.shape[0]

    def body(c_ref, w_ref, b_ref, o_ref):
        s = _silu(c_ref[...]).astype(BF16)
        o_ref[...] = jnp.dot(s, w_ref[...].astype(BF16), preferred_element_type=F32) + b_ref[...]

    outs, couts = _call(
        body, name=name, grid=(nl,),
        in_specs=[pl.BlockSpec((r, d), lambda l: (0, 0)), pl.BlockSpec((None, d, w), lambda l: (l, 0, 0)),
                  pl.BlockSpec((None, 1, w), lambda l: (l, 0, 0))],
        out_specs=pl.BlockSpec((None, r, w), lambda l: (l, 0, 0)),
        out_shape=jax.ShapeDtypeStruct((nl, r, w), F32), args=[c_rows, w_ada, b_cols], sem=("parallel",), comm=comm)
    return outs if comm is None else (outs, couts)


def _ada_bwd(c_rows, c_ctx, dmod, w_ada, name):
    nl, d, w = w_ada.shape
    r = c_rows.shape[0]

    def body(c_ref, cc_ref, g_ref, w_ref, dw_ref, dc_ref):
        l = pl.program_id(0)
        s = _silu(c_ref[...]).astype(BF16)
        gm = g_ref[...].astype(BF16)
        dw_ref[...] = lax.dot_general(s, gm, (((0,), (0,)), ((), ())), preferred_element_type=F32)
        ds = lax.dot_general(gm, w_ref[...].astype(BF16), (((1,), (1,)), ((), ())), preferred_element_type=F32)
        rows = lax.broadcasted_iota(jnp.int32, ds.shape, 0)
        dsc = jnp.sum(jnp.where(rows % ADA_ROWS == 2, ds, 0.0), axis=0, keepdims=True)
        x = cc_ref[...]
        sg = jax.nn.sigmoid(x)
        dcc = dsc * (sg * (1.0 + x * (1.0 - sg)))
        out = jnp.concatenate([dcc, jnp.zeros((7, d), F32)], axis=0)

        @pl.when(l == 0)
        def _():
            dc_ref[...] = out

        @pl.when(l != 0)
        def _():
            dc_ref[...] += out

    return pl.pallas_call(
        body, name=name, grid=(nl,),
        in_specs=[pl.BlockSpec((r, d), lambda l: (0, 0)), pl.BlockSpec((1, d), lambda l: (0, 0)),
                  pl.BlockSpec((None, r, w), lambda l: (l, 0, 0)), pl.BlockSpec((None, d, w), lambda l: (l, 0, 0))],
        out_specs=[pl.BlockSpec((None, d, w), lambda l: (l, 0, 0)), pl.BlockSpec((8, d), lambda l: (0, 0))],
        out_shape=[jax.ShapeDtypeStruct((nl, d, w), F32), jax.ShapeDtypeStruct((8, d), F32)],
        compiler_params=_params(("arbitrary",)),
    )(c_rows, c_ctx, dmod, w_ada)


def _adam_math(w, g, m, v):
    m = ADAM_B1 * m + (1.0 - ADAM_B1) * g
    v = ADAM_B2 * v + (1.0 - ADAM_B2) * (g * g)
    m_hat = m / (1.0 - ADAM_B1 ** ADAM_STEP)
    v_hat = v / (1.0 - ADAM_B2 ** ADAM_STEP)
    delta = -ADAM_LR * (m_hat / (jnp.sqrt(v_hat) + ADAM_EPS) + ADAM_WD * w)
    return delta, m, v


def _adamw(w, m, v, g_own, g_recv, name, rows=256):
    nl, r, c = w.shape
    tr = min(rows, r)
    spec = pl.BlockSpec((None, tr, c), lambda l, i: (l, i, 0))
    per_layer = g_own is None
    own = [] if per_layer else [g_own]
    recv = [] if g_recv is None else list(g_recv)
    n_i = r // tr

    def rows_of(li):
        return lambda l, i: jnp.where(l == li, i, jnp.where(l > li, n_i - 1, 0))

    in_specs = [spec] * (3 + len(own))
    in_specs += [pl.BlockSpec((N_DEV, tr, c), lambda l, i, f=rows_of(li): (0, f(l, i), 0)) for li in range(len(recv))]

    def body(*refs):
        w_ref, m_ref, v_ref = refs[:3]
        own_refs, recv_refs = refs[3:3 + len(own)], refs[3 + len(own):3 + len(own) + len(recv)]
        go_ref, d_ref, mo_ref, vo_ref = refs[-4:]

        def update(li):
            if recv:
                g = recv_refs[li][0].astype(F32)
                for k in range(1, N_DEV):
                    g = g + recv_refs[li][k].astype(F32)
            else:
                g = own_refs[0][...]
            delta, m_, v_ = _adam_math(w_ref[...], g, m_ref[...], v_ref[...])
            go_ref[...] = g
            d_ref[...] = delta
            mo_ref[...] = m_
            vo_ref[...] = v_

        if per_layer:
            for li in range(nl):
                pl.when(pl.program_id(0) == li)(functools.partial(update, li))
        else:
            update(0)

    return pl.pallas_call(
        body, name=name, grid=(nl, n_i), in_specs=in_specs, out_specs=[spec] * 4,
        out_shape=[jax.ShapeDtypeStruct(w.shape, F32)] * 4, compiler_params=_params(("parallel", "parallel")),
    )(w, m, v, *own, *recv)


def _small_adamw(w, m, v, g_all, name):
    def body(w_ref, m_ref, v_ref, g_ref, go_ref, d_ref, mo_ref, vo_ref):
        g = g_ref[0]
        for k in range(1, N_DEV):
            g = g + g_ref[k]
        delta, m_, v_ = _adam_math(w_ref[...], g, m_ref[...], v_ref[...])
        go_ref[...] = g
        d_ref[...] = delta
        mo_ref[...] = m_
        vo_ref[...] = v_

    return pl.pallas_call(
        body, name=name, out_shape=[jax.ShapeDtypeStruct(w.shape, F32)] * 4, compiler_params=_params(),
    )(w, m, v, g_all)


SMALL = ("c_ctx", "b_ada", "g_pre_mix", "g_post_mix", "g_pre_mlp", "g_post_mlp", "q_norm", "k_norm", "sink", "loss")


def _pack_small(parts):
    flat = jnp.concatenate([parts[n].reshape(-1) for n in SMALL])
    rows = -(-flat.shape[0] // 1024) * 8
    return jnp.pad(flat, (0, rows * 128 - flat.shape[0])).reshape(rows, 128)


def _unpack_small(packed, like):
    flat = packed.reshape(-1)
    out, off = {}, 0
    for n in SMALL:
        size = int(np.prod(like[n].shape))
        out[n] = flat[off:off + size].reshape(like[n].shape)
        off += size
    return out


def kernel(x, c, ctx, c_ctx, w_ada, b_ada, g_pre_mix, g_post_mix, g_pre_mlp, g_post_mlp, w_in, q_norm, k_norm, sink, w_out, w_up, w_down, loss_target, m_c_ctx, m_w_ada, m_b_ada, m_g_pre_mix, m_g_post_mix, m_g_pre_mlp, m_g_post_mlp, m_w_in, m_q_norm, m_k_norm, m_sink, m_w_out, m_w_up, m_w_down, v_c_ctx, v_w_ada, v_b_ada, v_g_pre_mix, v_g_post_mix, v_g_pre_mlp, v_g_post_mlp, v_w_in, v_q_norm, v_k_norm, v_sink, v_w_out, v_w_up, v_w_down):
    bl, s_rows, d = x.shape
    c_rows = ctx.shape[1]
    assert c_rows == TM and s_rows % TM == 0 and bl == 2
    t_rows = c_rows + s_rows
    nt = t_rows // TM
    m_rows = bl * t_rows
    nl = w_in.shape[0]
    ada_w = w_ada.shape[2]
    d_ff = w_up.shape[2] * N_DEV
    me = _my_index()

    shard = lambda w_, l: w_[l].astype(BF16)
    c_pad = jnp.concatenate([c, c_ctx[None, :], jnp.zeros((ADA_ROWS - bl - 1, d), F32)], axis=0)
    gathered = {0: {}}
    h, (c_all, gathered[0]["w_in"]) = _token_stream(
        ctx, x, nt, "token_stream", comm=_Comm([(c_pad, GATHER_VIA_SIBLING), (shard(w_in, 0), GATHER_VIA_SIBLING)]))
    c_all = c_all.reshape(N_DEV * ADA_ROWS, d)

    def layer_weights(l):
        g_ = gathered[l]
        w_out_f = g_["w_out"].reshape(-1, d)
        w_down_f = g_["w_down"].reshape(d_ff, d)
        return dict(
            w_in_f=g_["w_in"].transpose(1, 0, 2).reshape(d, PROJ_W), w_out_f=w_out_f,
            w_up_s=g_["w_up"], w_up_t=g_["w_up"].transpose(0, 2, 1).reshape(d_ff, d), w_down_f=w_down_f)

    big = dict(tm=2304, tn=512)
    deep = dict(tm=1536, tn=512, tk=d_ff)
    wide = dict(tm=1024, tn=512, tk=m_rows)

    b_cols = lax.dynamic_slice(b_ada, (0, me * ada_w), (nl, ada_w))[:, None, :]
    mod_cols = _ada_fwd(c_all, w_ada, b_cols, "ada_fwd")
    mod_slots = mod_cols.reshape(nl, N_DEV, ADA_ROWS, ada_w).transpose(1, 0, 2, 3)
    mod_g, = _comm_only(_Comm([(mod_slots, TO_OWNER)]), "exchange_mod")
    mine = mod_g.transpose(1, 2, 0, 3).reshape(nl, ADA_ROWS, N_MOD, d)
    pad = jnp.zeros((bl, 2, MOD_ROWS - N_MOD, d), F32)
    modsel = [jnp.concatenate([jnp.stack([jnp.broadcast_to(mine[l, bl], (bl, N_MOD, d)), mine[l, :bl]], axis=1), pad],
                              axis=2) for l in range(nl)]

    cos, sin = _rope_tables(t_rows, c_rows)
    emat = _head_mean_matrix()
    row = lambda a: a[None, :]
    qn = [jnp.tile(q_norm[l], 8)[None, :] for l in range(nl)]
    kn = [jnp.tile(k_norm[l], 2)[None, :] for l in range(nl)]

    target = loss_target.reshape(bl * s_rows, d)
    saved = []
    weights_of = {}
    (u, u_t), (gathered[0]["w_out"],) = _norm_mod_fwd(
        h, row(g_pre_mix[0]), modsel[0], 0, 1, nt, "mix_mod_fwd0", comm=_Comm([(shard(w_out, 0), GATHER_VIA_SIBLING)]))
    for l in range(nl):
        w_in_f = gathered[l]["w_in"].transpose(1, 0, 2).reshape(d, PROJ_W)
        proj = _mm(u, w_in_f, name=f"mm_in{l}", tk=d, **big)
        q_all, kv_all = _prep_fwd(proj, qn[l], kn[l], cos, sin, emat, nt, f"prep_fwd{l}")
        (o, o_t, lse), (w_up_g, w_down_g) = _attn_fwd(
            q_all.reshape(bl, t_rows, 1024), kv_all.reshape(bl, t_rows, 512), sink[l], c_rows, f"attn_fwd{l}",
            comm=_Comm([(shard(w_up, l), GATHER), (shard(w_down, l), GATHER)]))
        o = o.reshape(m_rows, 1024)
        gathered[l].update(w_up=w_up_g, w_down=w_down_g)
        wl = weights_of[l] = layer_weights(l)
        mix = _mm(o, wl["w_out_f"], name=f"mm_out{l}", tk=1024, **big)
        h_mid, v_in, v_t = _gate_mod_fwd(h, mix, row(g_post_mix[l]), modsel[l], 2, row(g_pre_mlp[l]), modsel[l], 3, 4,
                                         nt, f"mix_gate_mlp_mod_fwd{l}")
        more = l + 1 < nl
        res_up = _mm(v_in, wl["w_up_s"], name=f"mm_up{l}", b_mode="nn_slots", epilogue="relu2", tk=d,
                     comm=_Comm([(shard(w_in, l + 1), GATHER)]) if more else None, **big)
        (r_act, r_t), nxt_in = res_up if more else (res_up, None)
        res_down = _mm(r_act, wl["w_down_f"], name=f"mm_down{l}",
                       comm=_Comm([(shard(w_out, l + 1), GATHER)]) if more else None, **deep)
        y, nxt_out = res_down if more else (res_down, None)
        saved.append((h, u_t, proj, q_all, kv_all, o, o_t, lse, mix, h_mid, v_t, r_act, r_t, y))
        if more:
            gathered[l + 1] = dict(w_in=nxt_in[0], w_out=nxt_out[0])
            h, u, u_t = _gate_mod_fwd(h_mid, y, row(g_post_mlp[l]), modsel[l], 5, row(g_pre_mix[l + 1]), modsel[l + 1],
                                      0, 1, nt, f"mlp_gate_mix_mod_fwd{l}")

    dh, dy, dvec_g2, ss = _gate_loss_bwd(h_mid, y, row(g_post_mlp[nl - 1]), modsel[nl - 1], 5, target, nt, "gate_loss_bwd")
    small_g = {n: [None] * nl for n in SMALL if n not in ("c_ctx", "b_ada", "loss")}
    dvecs = {l: {} for l in range(nl)}
    dvecs[nl - 1]["g2"] = dvec_g2
    slots = {n: [None] * nl for n in ("w_in", "w_out", "w_up", "w_down")}
    recvd = {n: [None] * nl for n in slots}
    send = lambda n, l_: (slots[n][l_], TO_OWNER_XOR)
    for l in reversed(range(nl)):
        h_in, u_t, proj, q_all, kv_all, o, o_t, lse, mix, h_mid, v_t, r_act, r_t, y = saved[l]
        wl = weights_of[l]
        da = _mm(dy, wl["w_down_f"], name=f"mm_da{l}", b_mode="nt", epilogue="relu2_bwd", extra=r_act, out_dtype=BF16,
                 tk=d, **big)
        dw_down = _mm(r_t, dy, name=f"mm_dw_down{l}", out_dtype=BF16, **wide)
        dw_up = _mm(v_t, da, name=f"mm_dw_up{l}", out_mode="slots", out_dtype=BF16, **wide)
        slots["w_down"][l] = dw_down.reshape(N_DEV, -1, d)
        slots["w_up"][l] = dw_up
        dv = _mm(da, wl["w_up_t"], name=f"mm_dv{l}", **deep)
        dh, dvecs[l]["m2"], dmix, dvecs[l]["g1"] = _mod_gate_bwd(
            h_mid, row(g_pre_mlp[l]), modsel[l], 3, 4, dv, dh, mix, row(g_post_mix[l]), modsel[l], 2, nt,
            f"mlp_mod_mix_gate_bwd{l}")
        do = _mm(dmix, wl["w_out_f"], name=f"mm_do{l}", b_mode="nt", out_dtype=BF16, tk=d, **big)
        dw_out = _mm(o_t, dmix, name=f"mm_dw_out{l}", out_dtype=BF16, **wide)
        slots["w_out"][l] = dw_out.reshape(N_DEV, -1, d)
        going = [("w_down", l), ("w_up", l)] + ([("w_out", l + 1), ("w_in", l + 1)] if l + 1 < nl else [])
        (dq, dkv, dsk), arrived = _attn_bwd(
            q_all.reshape(bl, t_rows, 1024), kv_all.reshape(bl, t_rows, 512), do.reshape(bl, t_rows, 1024),
            o.reshape(bl, t_rows, 1024), lse, sink[l], c_rows, f"attn_bwd{l}", comm=_Comm([send(*g_) for g_ in going]))
        for (n_, l_), got in zip(going, arrived):
            recvd[n_][l_] = got
        last = l == 0
        res = _prep_bwd(proj, dq.reshape(m_rows, 1024), dkv.reshape(m_rows, 512), qn[l], kn[l], cos, sin, emat, nt,
                        f"prep_bwd{l}", comm=_Comm([send("w_out", l)]) if last else None)
        dproj, dqn, dkn = res[0] if last else res
        if last:
            recvd["w_out"][l] = res[1][0]
        dw_in = _mm(u_t, dproj, name=f"mm_dw_in{l}", out_dtype=BF16, **wide)
        slots["w_in"][l] = dw_in.reshape(d, N_DEV, PROJ_W // N_DEV).transpose(1, 0, 2)
        if last:
            halves = [(slots["w_in"][l][:, :d // 2], TO_OWNER_XOR), (slots["w_in"][l][:, d // 2:], TO_OWNER_XOR)]
            du, (got_a,) = _mm(dproj, wl["w_in_f"], name=f"mm_du{l}", b_mode="nt", tk=PROJ_W,
                               comm=_Comm(halves[:1]), **big)
            (dh, dvecs[l]["m1"]), (got_b,) = _norm_mod_bwd(
                h_in, row(g_pre_mix[l]), modsel[l], du, dh, 0, 1, nt, f"mix_mod_bwd{l}", latent_only=True,
                comm=_Comm(halves[1:]))
            recvd["w_in"][l] = jnp.concatenate([got_a, got_b], axis=1)
        else:
            du = _mm(dproj, wl["w_in_f"], name=f"mm_du{l}", b_mode="nt", tk=PROJ_W, **big)
            dh, dvecs[l]["m1"], dy, dvecs[l - 1]["g2"] = _mod_gate_bwd(
                h_in, row(g_pre_mix[l]), modsel[l], 0, 1, du, dh, saved[l - 1][-1], row(g_post_mlp[l - 1]),
                modsel[l - 1], 5, nt, f"mix_mod_mlp_gate_bwd{l}")
        small_g["q_norm"][l] = jnp.sum(dqn[0].reshape(8, HEAD_DIM), axis=0)
        small_g["k_norm"][l] = jnp.sum(dkn[0].reshape(2, HEAD_DIM), axis=0)
        small_g["sink"][l] = jnp.sum(dsk[:, 2:, :GROUP, 0], axis=0).reshape(-1)

    dmod_rows = []
    for l in range(nl):
        m1, g1, m2, g2 = (dvecs[l][n] for n in ("m1", "g1", "m2", "g2"))
        small_g["g_pre_mix"][l] = jnp.sum(m1[:, :, 2], axis=(0, 1))
        small_g["g_post_mix"][l] = jnp.sum(g1[:, :, 1], axis=(0, 1))
        small_g["g_pre_mlp"][l] = jnp.sum(m2[:, :, 2], axis=(0, 1))
        small_g["g_post_mlp"][l] = jnp.sum(g2[:, :, 1], axis=(0, 1))
        dms = jnp.stack([m1[:, :, 0], m1[:, :, 1], g1[:, :, 0], m2[:, :, 0], m2[:, :, 1], g2[:, :, 0]], axis=2)
        rows = jnp.concatenate([dms[:, 1], jnp.sum(dms[:, 0], axis=0)[None]], axis=0)
        dmod_rows.append(jnp.pad(rows.reshape(bl + 1, N_MOD * d), ((0, ADA_ROWS - bl - 1), (0, 0))))
    grad_x = dh.reshape(bl, s_rows, d)

    dmod_slots = jnp.stack(dmod_rows).reshape(nl, ADA_ROWS, N_DEV, ada_w).transpose(2, 0, 1, 3)
    dmod_g, = _comm_only(_Comm([(dmod_slots, TO_OWNER)]), "exchange_dmod")
    dmod_mine = dmod_g.transpose(1, 0, 2, 3).reshape(nl, N_DEV * ADA_ROWS, ada_w)
    dw_ada, dcc = _ada_bwd(c_all, c_ctx[None, :], dmod_mine, w_ada, "ada_bwd")

    parts = {n: jnp.stack(small_g[n]) for n in small_g}
    parts["c_ctx"] = dcc[0]
    parts["loss"] = ss[0, 0:1]
    no_loss = jnp.zeros((1,), F32)
    parts["b_ada"] = jnp.stack([jnp.sum(r_[: bl + 1], axis=0) for r_ in dmod_rows])
    weights = dict(c_ctx=c_ctx, b_ada=b_ada, g_pre_mix=g_pre_mix, g_post_mix=g_post_mix, g_pre_mlp=g_pre_mlp,
                   g_post_mlp=g_post_mlp, q_norm=q_norm, k_norm=k_norm, sink=sink, loss=no_loss)
    moms = dict(c_ctx=m_c_ctx, b_ada=m_b_ada, g_pre_mix=m_g_pre_mix, g_post_mix=m_g_post_mix, g_pre_mlp=m_g_pre_mlp,
                g_post_mlp=m_g_post_mlp, q_norm=m_q_norm, k_norm=m_k_norm, sink=m_sink, loss=no_loss)
    vels = dict(c_ctx=v_c_ctx, b_ada=v_b_ada, g_pre_mix=v_g_pre_mix, g_post_mix=v_g_post_mix, g_pre_mlp=v_g_pre_mlp,
                g_post_mlp=v_g_post_mlp, q_norm=v_q_norm, k_norm=v_k_norm, sink=v_sink, loss=no_loss)
    small_all, = _comm_only(_Comm([(_pack_small(parts), GATHER)]), "gather_small")
    s_out = _small_adamw(_pack_small(weights), _pack_small(moms), _pack_small(vels), small_all, "adamw_small")
    s_g, s_d, s_m, s_v = [_unpack_small(a, weights) for a in s_out]
    loss = 0.5 * s_g["loss"][0] / d

    res = {}
    for n, w_, m_, v_ in (("w_in", w_in, m_w_in, v_w_in), ("w_out", w_out, m_w_out, v_w_out),
                          ("w_up", w_up, m_w_up, v_w_up), ("w_down", w_down, m_w_down, v_w_down)):
        res[n] = _adamw(w_, m_, v_, None, recvd[n], f"adamw_{n}")
    res["w_ada"] = _adamw(w_ada, m_w_ada, v_w_ada, dw_ada, None, "adamw_w_ada")

    order = ("c_ctx", "w_ada", "b_ada", "g_pre_mix", "g_post_mix", "g_pre_mlp", "g_post_mlp", "w_in", "q_norm",
             "k_norm", "sink", "w_out", "w_up", "w_down")
    outs = [loss, grad_x]
    for i, small in enumerate((s_g, s_d, s_m, s_v)):
        outs += [small[n] if n in small else res[n][i] for n in order]
    return tuple(outs)
```

```python
import functools

import jax
import jax.numpy as jnp
import numpy as np
from jax import lax
from jax.experimental import pallas as pl
from jax.experimental.pallas import tpu as pltpu

F32 = jnp.float32
BF16 = jnp.bfloat16

HEAD_DIM = 64
GROUP = 4
N_HG = 4
Q_WIDTH = GROUP * HEAD_DIM
WINDOW = 128
GRID_W = 64
ROPE_THETA = 10000.0
EPS = 1e-6
NEG_BIG = -1e30
N_MOD = 6
MOD_ROWS = 8
TM = 256
N_DEV = 8
ADA_ROWS = 8
VMEM_LIMIT = 56 * 1024 * 1024

ADAM_LR = 0.001
ADAM_B1 = 0.9
ADAM_B2 = 0.999
ADAM_EPS = 1e-08
ADAM_WD = 0.01
ADAM_STEP = 10


def _params(sem=None):
    kw = dict(vmem_limit_bytes=VMEM_LIMIT)
    if sem is not None:
        kw["dimension_semantics"] = sem
    return pltpu.CompilerParams(**kw)


def _my_index():
    return 4 * lax.axis_index("x") + 2 * lax.axis_index("y") + lax.axis_index("c")


def _peer(k):
    x, y, c = lax.axis_index("x"), lax.axis_index("y"), lax.axis_index("c")
    kx, ky, kc = (k >> 2) & 1, (k >> 1) & 1, k & 1
    px = (1 - x) if kx else x
    py = (1 - y) if ky else y
    pc = (1 - c) if kc else c
    return (px, py, pc), 4 * px + 2 * py + pc


GATHER, GATHER_VIA_SIBLING, TO_OWNER, TO_OWNER_XOR = "gather", "gather_via_sibling", "to_owner", "to_owner_xor"


class _Comm:
    def __init__(self, items):
        self.items = list(items)
        self.arrays = [a for a, _ in self.items]

    def out_shapes(self):
        return [jax.ShapeDtypeStruct(((N_DEV,) + a.shape) if kind in (GATHER, GATHER_VIA_SIBLING) else a.shape, a.dtype)
                for a, kind in self.items]

    def sem_shapes(self):
        n = len(self.items) * N_DEV
        return [pltpu.SemaphoreType.DMA((n,)), pltpu.SemaphoreType.DMA((n,))]

    def _two_level(self, x_ref, o_ref, send_sems, recv_sems, base):
        x, y, c = lax.axis_index("x"), lax.axis_index("y"), lax.axis_index("c")
        me, sibling = (x, y, c), (x, y, 1 - c)
        chips = [(1 - x, y), (x, 1 - y), (1 - x, 1 - y)]
        block = lambda p: o_ref.at[4 * p[0] + 2 * p[1] + p[2]]

        def copy(k, owner, to, src=None):
            return pltpu.make_async_remote_copy(
                src_ref=block(owner) if src is None else src, dst_ref=block(owner), send_sem=send_sems.at[base + k],
                recv_sem=recv_sems.at[base + k], device_id=to, device_id_type=pl.DeviceIdType.MESH)

        first = [copy(1, me, sibling, src=x_ref)] + [copy(2 + j, me, (*chip, c), src=x_ref) for j, chip in enumerate(chips)]
        relay = [(copy(2 + j, (*chip, c), me), copy(5 + j, (*chip, c), sibling)) for j, chip in enumerate(chips)]
        last = [copy(1, sibling, me)] + [copy(5 + j, (*chip, 1 - c), me) for j, chip in enumerate(chips)]
        return first, relay, last

    def _copies(self, in_refs, out_refs, send_sems, recv_sems):
        me = _my_index()
        local, remote = [], []
        for i, ((_, kind), x_ref, o_ref) in enumerate(zip(self.items, in_refs, out_refs)):
            base = i * N_DEV
            if kind == GATHER_VIA_SIBLING:
                local.append(pltpu.make_async_copy(x_ref, o_ref.at[me], send_sems.at[base]))
                continue
            own_src = x_ref if kind == GATHER else x_ref.at[me]
            own_dst = o_ref.at[0] if kind == TO_OWNER_XOR else o_ref.at[me]
            local.append(pltpu.make_async_copy(own_src, own_dst, send_sems.at[base]))
            for k in range(1, N_DEV):
                peer, pidx = _peer(k)
                remote.append(pltpu.make_async_remote_copy(
                    src_ref=x_ref if kind == GATHER else x_ref.at[pidx],
                    dst_ref=o_ref.at[k] if kind == TO_OWNER_XOR else o_ref.at[me],
                    send_sem=send_sems.at[base + k], recv_sem=recv_sems.at[base + k],
                    device_id=peer, device_id_type=pl.DeviceIdType.MESH))
        return local, remote

    def start(self, in_refs, out_refs, send_sems, recv_sems):
        local, remote = self._copies(in_refs, out_refs, send_sems, recv_sems)
        for cp in local + remote:
            cp.start()
        for i, ((_, kind), x_ref, o_ref) in enumerate(zip(self.items, in_refs, out_refs)):
            if kind == GATHER_VIA_SIBLING:
                for cp in self._two_level(x_ref, o_ref, send_sems, recv_sems, i * N_DEV)[0]:
                    cp.start()

    def wait(self, in_refs, out_refs, send_sems, recv_sems):
        local, remote = self._copies(in_refs, out_refs, send_sems, recv_sems)
        for cp in remote:
            cp.wait_recv()
        for cp in remote:
            cp.wait_send()
        for i, ((_, kind), x_ref, o_ref) in enumerate(zip(self.items, in_refs, out_refs)):
            if kind == GATHER_VIA_SIBLING:
                first, relay, last = self._two_level(x_ref, o_ref, send_sems, recv_sems, i * N_DEV)
                for arrival, onward in relay:
                    arrival.wait_recv()
                    onward.start()
                for cp in last:
                    cp.wait_recv()
                for cp in first + [onward for _, onward in relay]:
                    cp.wait_send()
        for cp in local:
            cp.wait()


def _call(body, *, name, grid, in_specs, out_specs, out_shape, args, scratch_shapes=(), prefetch=0, sem=None,
          comm=None):
    single = not isinstance(out_shape, (list, tuple))
    out_shape = [out_shape] if single else list(out_shape)
    out_specs = [out_specs] if single else list(out_specs)
    in_specs, scratch_shapes, args = list(in_specs), list(scratch_shapes), list(args)
    n_in, n_out = len(in_specs), len(out_shape)
    if comm is not None:
        nc = len(comm.arrays)
        hbm = pl.BlockSpec(memory_space=pl.ANY)
        inner = body

        def body(*refs):
            pre, r = refs[:prefetch], refs[prefetch:]
            ins, cin = r[:n_in], r[n_in:n_in + nc]
            outs, cout = r[n_in + nc:n_in + nc + n_out], r[n_in + nc + n_out:n_in + 2 * nc + n_out]
            scr, sems = r[n_in + 2 * nc + n_out:len(r) - 2], r[len(r) - 2:]
            ids = [pl.program_id(i) for i in range(len(grid))]

            def when(flags, fn):
                if flags:
                    pl.when(functools.reduce(jnp.logical_and, flags))(fn)
                else:
                    fn()

            when([i == 0 for i in ids], lambda: comm.start(cin, cout, *sems))
            inner(*pre, *ins, *outs, *scr)
            when([i == n - 1 for i, n in zip(ids, grid)], lambda: comm.wait(cin, cout, *sems))

        in_specs += [hbm] * nc
        out_specs += [hbm] * nc
        out_shape += comm.out_shapes()
        scratch_shapes += comm.sem_shapes()
        args += comm.arrays
        sem = ("arbitrary",) * len(grid)
    kw = dict(name=name, out_shape=out_shape, compiler_params=_params(sem if grid else None))
    if prefetch:
        kw["grid_spec"] = pltpu.PrefetchScalarGridSpec(
            num_scalar_prefetch=prefetch, grid=grid, in_specs=in_specs, out_specs=out_specs,
            scratch_shapes=scratch_shapes)
    else:
        kw.update(in_specs=in_specs, out_specs=out_specs, scratch_shapes=scratch_shapes)
        if grid:
            kw["grid"] = grid
    res = list(pl.pallas_call(body, **kw)(*args))
    outs = res[:n_out]
    return (outs[0] if single else outs), res[n_out:]


def _comm_only(comm, name):
    return _call(lambda: None, name=name, grid=(), in_specs=[], out_specs=[], out_shape=[], args=[], comm=comm)[1]


def _mm(a, b, *, name, ta=False, b_mode="nn", out_mode="plain", out_dtype=F32, tm=512, tn=512, tk=512,
        epilogue=None, extra=None, comm=None):
    if ta:
        kdim, m = a.shape
    else:
        m, kdim = a.shape
    if b_mode == "nn":
        n = b.shape[1]
    elif b_mode == "nt":
        n = b.shape[0]
    elif b_mode == "nn_slots":
        n = b.shape[0] * b.shape[2]
        tn = b.shape[2]
    else:
        n = b.shape[1]
        tk = b.shape[2]
    if out_mode == "slots":
        tn = n // N_DEV
    tm, tn, tk = min(tm, m), min(tn, n), min(tk, kdim)
    assert m % tm == 0 and n % tn == 0 and kdim % tk == 0, (name, m, n, kdim, tm, tn, tk)
    nk = kdim // tk

    a_spec = pl.BlockSpec((tk, tm), lambda i, j, k: (k, i)) if ta else pl.BlockSpec((tm, tk), lambda i, j, k: (i, k))
    if b_mode == "nn":
        b_spec = pl.BlockSpec((tk, tn), lambda i, j, k: (k, j))
    elif b_mode == "nt":
        b_spec = pl.BlockSpec((tn, tk), lambda i, j, k: (j, k))
    elif b_mode == "nn_slots":
        b_spec = pl.BlockSpec((None, tk, tn), lambda i, j, k: (j, k, 0))
    else:
        b_spec = pl.BlockSpec((None, tn, tk), lambda i, j, k: (k, j, 0))
    tb = b_mode in ("nt", "nt_slots")
    if out_mode == "plain":
        o_shape, o_spec = (m, n), pl.BlockSpec((tm, tn), lambda i, j, k: (i, j))
    else:
        o_shape, o_spec = (N_DEV, m, tn), pl.BlockSpec((None, tm, tn), lambda i, j, k: (j, i, 0))
    dims = (((0 if ta else 1,), (1 if tb else 0,)), ((), ()))

    in_specs = [a_spec, b_spec]
    args = [a, b]
    if epilogue == "relu2_bwd":
        in_specs.append(pl.BlockSpec((tm, tn), lambda i, j, k: (i, j)))
        args.append(extra)
    if epilogue == "relu2":
        out_shape = [jax.ShapeDtypeStruct(o_shape, BF16), jax.ShapeDtypeStruct((n, m), BF16)]
        out_specs = [o_spec, pl.BlockSpec((tn, tm), lambda i, j, k: (j, i))]
    else:
        out_shape = jax.ShapeDtypeStruct(o_shape, out_dtype)
        out_specs = o_spec

    def finish(refs, acc):
        if epilogue == "relu2":
            r = jnp.maximum(acc, 0.0)
            r2 = (r * r).astype(BF16)
            refs[2][...] = r2
            refs[3][...] = r2.T
        elif epilogue == "relu2_bwd":
            refs[3][...] = (acc * (2.0 * jnp.sqrt(refs[2][...]).astype(F32))).astype(out_dtype)
        else:
            refs[2][...] = acc.astype(out_dtype)

    def body(*refs):
        part = lax.dot_general(refs[0][...], refs[1][...], dims, preferred_element_type=F32)
        if nk == 1:
            finish(refs, part)
            return
        acc_ref = refs[-1]
        k = pl.program_id(2)

        @pl.when(k == 0)
        def _():
            acc_ref[...] = part

        @pl.when(jnp.logical_and(k > 0, k < nk - 1))
        def _():
            acc_ref[...] += part

        @pl.when(k == nk - 1)
        def _():
            finish(refs, acc_ref[...] + part)

    outs, couts = _call(
        body, name=name, grid=(m // tm, n // tn, nk), in_specs=in_specs, out_specs=out_specs, out_shape=out_shape,
        args=args, scratch_shapes=[] if nk == 1 else [pltpu.VMEM((tm, tn), F32)],
        sem=("parallel", "parallel", "arbitrary"), comm=comm)
    return outs if comm is None else (outs, couts)


def _row_specs(d, nt):
    row = pl.BlockSpec((TM, d), lambda t: (t, 0))
    vec = pl.BlockSpec((1, d), lambda t: (0, 0))
    mod = pl.BlockSpec((None, None, MOD_ROWS, d), lambda t: (t // nt, jnp.minimum(t % nt, 1), 0, 0))
    return row, vec, mod


def _rms(x):
    r = lax.rsqrt(jnp.mean(x * x, axis=1, keepdims=True) + EPS)
    return r, x * r


def _token_stream(ctx, x, nt, name, comm=None):
    bl, c_rows, d = ctx.shape
    m = bl * nt * TM

    def body(ctx_ref, x_ref, h_ref):
        t = pl.program_id(0)

        @pl.when(t % nt == 0)
        def _():
            h_ref[...] = ctx_ref[...]

        @pl.when(t % nt != 0)
        def _():
            h_ref[...] = x_ref[...]

    outs, couts = _call(
        body, name=name, grid=(m // TM,),
        in_specs=[pl.BlockSpec((None, TM, d), lambda t: (t // nt, 0, 0)),
                  pl.BlockSpec((None, TM, d), lambda t: (t // nt, jnp.maximum(t % nt, 1) - 1, 0))],
        out_specs=pl.BlockSpec((TM, d), lambda t: (t, 0)), out_shape=jax.ShapeDtypeStruct((m, d), F32),
        args=[ctx, x], sem=("parallel",), comm=comm)
    return outs if comm is None else (outs, couts)


def _norm_mod_fwd(h, g, modsel, i_sh, i_sc, nt, name, comm=None):
    m, d = h.shape
    row, vec, mod = _row_specs(d, nt)

    def body(h_ref, g_ref, ms_ref, u_ref, ut_ref):
        _, xh = _rms(h_ref[...])
        ms = ms_ref[...]
        u = (xh * g_ref[...] * (1.0 + ms[i_sc:i_sc + 1]) + ms[i_sh:i_sh + 1]).astype(BF16)
        u_ref[...] = u
        ut_ref[...] = u.T

    outs, couts = _call(
        body, name=name, grid=(m // TM,), in_specs=[row, vec, mod],
        out_specs=[row, pl.BlockSpec((d, TM), lambda t: (0, t))],
        out_shape=[jax.ShapeDtypeStruct((m, d), BF16), jax.ShapeDtypeStruct((d, m), BF16)],
        args=[h, g, modsel], sem=("parallel",), comm=comm)
    return outs if comm is None else (outs, couts)


def _acc_rows(t, nt, dvec_ref, rows):
    first = (t % nt) <= 1

    @pl.when(first)
    def _():
        dvec_ref[...] = rows

    @pl.when(jnp.logical_not(first))
    def _():
        dvec_ref[...] += rows


def _norm_mod_bwd(h, g, modsel, du, dh_in, i_sh, i_sc, nt, name, comm=None, latent_only=False):
    m, d = h.shape
    row, vec, mod = _row_specs(d, nt)
    dh_rows, dh_spec = m, row
    if latent_only:
        dh_rows = m // nt * (nt - 1)
        dh_spec = pl.BlockSpec((TM, d), lambda t: ((t // nt) * (nt - 1) + jnp.maximum(t % nt, 1) - 1, 0))

    def body(h_ref, g_ref, ms_ref, du_ref, dhi_ref, dh_ref, dvec_ref):
        t = pl.program_id(0)
        r, xh = _rms(h_ref[...])
        g_ = g_ref[...]
        ms = ms_ref[...]
        du_ = du_ref[...]
        y = xh * g_
        dy = du_ * (1.0 + ms[i_sc:i_sc + 1])
        dxh = dy * g_
        dx = r * (dxh - xh * jnp.mean(dxh * xh, axis=1, keepdims=True))
        dh_ref[...] = dhi_ref[...] + dx
        rows = jnp.concatenate([
            jnp.sum(du_, axis=0, keepdims=True), jnp.sum(du_ * y, axis=0, keepdims=True),
            jnp.sum(dy * xh, axis=0, keepdims=True), jnp.zeros((MOD_ROWS - 3, d), F32)], axis=0)
        _acc_rows(t, nt, dvec_ref, rows)

    outs, couts = _call(
        body, name=name, grid=(m // TM,), in_specs=[row, vec, mod, row, row], out_specs=[dh_spec, mod],
        out_shape=[jax.ShapeDtypeStruct((dh_rows, d), F32), jax.ShapeDtypeStruct(modsel.shape, F32)],
        args=[h, g, modsel, du, dh_in], sem=("arbitrary",), comm=comm)
    return outs if comm is None else (outs, couts)


def _gate_rows(dh_, xh, g_, gate):
    dy = dh_ * gate
    return dy * g_, jnp.sum(dh_ * (xh * g_), axis=0, keepdims=True), jnp.sum(dy * xh, axis=0, keepdims=True)


def _gate_mod_fwd(h, z, g_post, ms_gate, i_g, g_pre, ms_mod, i_sh, i_sc, nt, name):
    m, d = h.shape
    row, vec, mod = _row_specs(d, nt)

    def body(h_ref, z_ref, gp_ref, msg_ref, gq_ref, msm_ref, hn_ref, u_ref, ut_ref):
        _, zh = _rms(z_ref[...])
        hn = h_ref[...] + msg_ref[...][i_g:i_g + 1] * (zh * gp_ref[...])
        hn_ref[...] = hn
        _, xh = _rms(hn)
        ms = msm_ref[...]
        u = (xh * gq_ref[...] * (1.0 + ms[i_sc:i_sc + 1]) + ms[i_sh:i_sh + 1]).astype(BF16)
        u_ref[...] = u
        ut_ref[...] = u.T

    return pl.pallas_call(
        body, name=name, grid=(m // TM,), in_specs=[row, row, vec, mod, vec, mod],
        out_specs=[row, row, pl.BlockSpec((d, TM), lambda t: (0, t))],
        out_shape=[jax.ShapeDtypeStruct((m, d), F32), jax.ShapeDtypeStruct((m, d), BF16),
                   jax.ShapeDtypeStruct((d, m), BF16)],
        compiler_params=_params(("parallel",)),
    )(h, z, g_post, ms_gate, g_pre, ms_mod)


def _mod_gate_bwd(h, g_pre, ms_mod, i_sh, i_sc, du, dh_in, z, g_post, ms_gate, i_g, nt, name):
    m, d = h.shape
    row, vec, mod = _row_specs(d, nt)

    def body(h_ref, gq_ref, msm_ref, du_ref, dhi_ref, z_ref, gp_ref, msg_ref, dh_ref, dvm_ref, dz_ref, dvg_ref):
        t = pl.program_id(0)
        r, xh = _rms(h_ref[...])
        gq = gq_ref[...]
        ms = msm_ref[...]
        du_ = du_ref[...]
        dy = du_ * (1.0 + ms[i_sc:i_sc + 1])
        dxh = dy * gq
        dh_ = dhi_ref[...] + r * (dxh - xh * jnp.mean(dxh * xh, axis=1, keepdims=True))
        dh_ref[...] = dh_
        _acc_rows(t, nt, dvm_ref, jnp.concatenate([
            jnp.sum(du_, axis=0, keepdims=True), jnp.sum(du_ * (xh * gq), axis=0, keepdims=True),
            jnp.sum(dy * xh, axis=0, keepdims=True), jnp.zeros((MOD_ROWS - 3, d), F32)], axis=0))
        rz, zh = _rms(z_ref[...])
        dzh, d_gate, d_gp = _gate_rows(dh_, zh, gp_ref[...], msg_ref[...][i_g:i_g + 1])
        dz_ref[...] = (rz * (dzh - zh * jnp.mean(dzh * zh, axis=1, keepdims=True))).astype(BF16)
        _acc_rows(t, nt, dvg_ref, jnp.concatenate([d_gate, d_gp, jnp.zeros((MOD_ROWS - 2, d), F32)], axis=0))

    return pl.pallas_call(
        body, name=name, grid=(m // TM,), in_specs=[row, vec, mod, row, row, row, vec, mod],
        out_specs=[row, mod, row, mod],
        out_shape=[jax.ShapeDtypeStruct((m, d), F32), jax.ShapeDtypeStruct(ms_mod.shape, F32),
                   jax.ShapeDtypeStruct((m, d), BF16), jax.ShapeDtypeStruct(ms_gate.shape, F32)],
        compiler_params=_params(("arbitrary",)),
    )(h, g_pre, ms_mod, du, dh_in, z, g_post, ms_gate)


def _gate_loss_bwd(h, z, g_post, modsel, i_g, target, nt, name):
    m, d = h.shape
    row, vec, mod = _row_specs(d, nt)
    ntl = nt - 1
    tgt = pl.BlockSpec((TM, d), lambda t: ((t // nt) * ntl + jnp.maximum(t % nt, 1) - 1, 0))
    acc = pl.BlockSpec((8, 128), lambda t: (0, 0))

    def body(h_ref, z_ref, gp_ref, ms_ref, t_ref, dh_ref, dz_ref, dvg_ref, ss_ref):
        t = pl.program_id(0)

        @pl.when(t == 0)
        def _():
            ss_ref[...] = jnp.zeros_like(ss_ref)

        latent = (t % nt != 0).astype(F32)
        rz, zh = _rms(z_ref[...])
        gp = gp_ref[...]
        gate = ms_ref[...][i_g:i_g + 1]
        e = h_ref[...] + gate * (zh * gp) - t_ref[...]
        ss_ref[...] += latent * jnp.sum(e * e)
        dh_ = e * (latent / d)
        dh_ref[...] = dh_
        dzh, d_gate, d_gp = _gate_rows(dh_, zh, gp, gate)
        dz_ref[...] = (rz * (dzh - zh * jnp.mean(dzh * zh, axis=1, keepdims=True))).astype(BF16)
        _acc_rows(t, nt, dvg_ref, jnp.concatenate([d_gate, d_gp, jnp.zeros((MOD_ROWS - 2, d), F32)], axis=0))

    return pl.pallas_call(
        body, name=name, grid=(m // TM,), in_specs=[row, row, vec, mod, tgt], out_specs=[row, row, mod, acc],
        out_shape=[jax.ShapeDtypeStruct((m, d), F32), jax.ShapeDtypeStruct((m, d), BF16),
                   jax.ShapeDtypeStruct(modsel.shape, F32), jax.ShapeDtypeStruct((8, 128), F32)],
        compiler_params=_params(("arbitrary",)),
    )(h, z, g_post, modsel, target)


QA, KA, VA, QB, KB, VB = 0, 512, 640, 768, 1280, 1408
PROJ_W = 1536
Q_SCALE = HEAD_DIM ** -0.5
LOG2E = 1.4426950408889634


def _swap16(x):
    lane = lax.broadcasted_iota(jnp.int32, x.shape, 1)
    n = x.shape[1]
    return jnp.where((lane % 32) < 16, pltpu.roll(x, n - 16, 1), pltpu.roll(x, 16, 1))


def _seg_mean(x, e):
    hi = x.astype(BF16)
    lo = (x - hi.astype(F32)).astype(BF16)
    w = e.shape[0]
    both = lambda a: jnp.dot(hi[:, a:a + w], e, preferred_element_type=F32) + jnp.dot(lo[:, a:a + w], e, preferred_element_type=F32)
    parts = [both(a) for a in range(0, x.shape[1], w)]
    return parts[0] if len(parts) == 1 else jnp.concatenate(parts, axis=1)


def _rope_tables(t_rows, c_rows):
    s = t_rows - c_rows
    row_ids = jnp.repeat(jnp.arange(s // GRID_W, dtype=jnp.int32), GRID_W).astype(F32)
    col_ids = jnp.tile(jnp.arange(GRID_W, dtype=jnp.int32), s // GRID_W).astype(F32)
    axis_dim = HEAD_DIM // 2
    inv = ROPE_THETA ** (-jnp.arange(0, axis_dim, 2, dtype=F32) / axis_dim)
    ang_r = row_ids[:, None] * inv[None, :]
    ang_c = col_ids[:, None] * inv[None, :]
    cos = jnp.concatenate([jnp.cos(ang_r), jnp.cos(ang_r), jnp.cos(ang_c), jnp.cos(ang_c)], axis=1)
    sin = jnp.concatenate([-jnp.sin(ang_r), jnp.sin(ang_r), -jnp.sin(ang_c), jnp.sin(ang_c)], axis=1)
    cos = jnp.concatenate([jnp.ones((c_rows, HEAD_DIM), F32), cos], axis=0)
    sin = jnp.concatenate([jnp.zeros((c_rows, HEAD_DIM), F32), sin], axis=0)
    return jnp.tile(cos, (1, 8)), jnp.tile(sin, (1, 8))


def _head_mean_matrix():
    i = np.arange(512)
    return jnp.asarray((i[:, None] // HEAD_DIM == i[None, :] // HEAD_DIM).astype(np.float32) / HEAD_DIM, dtype=BF16)


def _interleave_kv(k, v):
    return jnp.concatenate([k[:, :64], v[:, :64], k[:, 64:], v[:, 64:]], axis=1)


def _prep_fwd(proj, qn, kn, cos, sin, emat, nt, name):
    m = proj.shape[0]
    specs = [
        pl.BlockSpec((TM, PROJ_W), lambda t: (t, 0)),
        pl.BlockSpec((1, 512), lambda t: (0, 0)), pl.BlockSpec((1, 128), lambda t: (0, 0)),
        pl.BlockSpec((TM, 512), lambda t: (t % nt, 0)), pl.BlockSpec((TM, 512), lambda t: (t % nt, 0)),
        pl.BlockSpec((512, 512), lambda t: (0, 0)),
    ]

    def body(p_ref, qn_ref, kn_ref, cos_ref, sin_ref, e_ref, q_ref, kv_ref):
        cos_, sin_, e = cos_ref[...], sin_ref[...], e_ref[...]

        def rope(x, w):
            return x * cos_[:, :w] + _swap16(x) * sin_[:, :w]

        def norm(x, g, w):
            return x * lax.rsqrt(_seg_mean(x * x, e[:min(w, 256), :min(w, 256)]) + EPS) * g

        qa = rope(norm(p_ref[:, QA:QA + 512], qn_ref[...], 512), 512)
        qb = rope(p_ref[:, QB:QB + 512], 512)
        q_ref[:, 0:512] = (qa * (Q_SCALE * LOG2E)).astype(BF16)
        q_ref[:, 512:1024] = (qb * (Q_SCALE * LOG2E)).astype(BF16)
        ka = rope(norm(p_ref[:, KA:KA + 128], kn_ref[...], 128), 128)
        kb = rope(p_ref[:, KB:KB + 128], 128)
        kv_ref[:, 0:256] = _interleave_kv(ka, p_ref[:, VA:VA + 128]).astype(BF16)
        kv_ref[:, 256:512] = _interleave_kv(kb, p_ref[:, VB:VB + 128]).astype(BF16)

    return pl.pallas_call(
        body, name=name, grid=(m // TM,), in_specs=specs,
        out_specs=[pl.BlockSpec((TM, 1024), lambda t: (t, 0)), pl.BlockSpec((TM, 512), lambda t: (t, 0))],
        out_shape=[jax.ShapeDtypeStruct((m, 1024), BF16), jax.ShapeDtypeStruct((m, 512), BF16)],
        compiler_params=_params(("parallel",)),
    )(proj, qn, kn, cos, sin, emat)


def _prep_bwd(proj, dq, dkv, qn, kn, cos, sin, emat, nt, name, comm=None):
    m = proj.shape[0]
    specs = [
        pl.BlockSpec((TM, PROJ_W), lambda t: (t, 0)),
        pl.BlockSpec((TM, 1024), lambda t: (t, 0)), pl.BlockSpec((TM, 512), lambda t: (t, 0)),
        pl.BlockSpec((1, 512), lambda t: (0, 0)), pl.BlockSpec((1, 128), lambda t: (0, 0)),
        pl.BlockSpec((TM, 512), lambda t: (t % nt, 0)), pl.BlockSpec((TM, 512), lambda t: (t % nt, 0)),
        pl.BlockSpec((512, 512), lambda t: (0, 0)),
    ]

    def body(p_ref, dq_ref, dkv_ref, qn_ref, kn_ref, cos_ref, sin_ref, e_ref, dp_ref, dqn_ref, dkn_ref):
        t = pl.program_id(0)
        cos_, sin_, e = cos_ref[...], sin_ref[...], e_ref[...]

        @pl.when(t == 0)
        def _():
            dqn_ref[...] = jnp.zeros_like(dqn_ref)
            dkn_ref[...] = jnp.zeros_like(dkn_ref)

        def unrope(dy, w):
            return dy * cos_[:, :w] + _swap16(dy * sin_[:, :w])

        def norm_bwd(x, g, dy, w):
            r = lax.rsqrt(_seg_mean(x * x, e[:min(w, 256), :min(w, 256)]) + EPS)
            xh = x * r
            dxh = dy * g
            dx = r * (dxh - xh * _seg_mean(dxh * xh, e[:min(w, 256), :min(w, 256)]))
            return dx, jnp.sum(dy * xh, axis=0, keepdims=True)

        dqa, dgq = norm_bwd(p_ref[:, QA:QA + 512], qn_ref[...], unrope(dq_ref[:, 0:512] * Q_SCALE, 512), 512)
        dp_ref[:, QA:QA + 512] = dqa.astype(BF16)
        dp_ref[:, QB:QB + 512] = unrope(dq_ref[:, 512:1024] * Q_SCALE, 512).astype(BF16)
        da = dkv_ref[:, 0:256]
        db = dkv_ref[:, 256:512]
        dka = jnp.concatenate([da[:, 0:64], da[:, 128:192]], axis=1)
        dva = jnp.concatenate([da[:, 64:128], da[:, 192:256]], axis=1)
        dkb = jnp.concatenate([db[:, 0:64], db[:, 128:192]], axis=1)
        dvb = jnp.concatenate([db[:, 64:128], db[:, 192:256]], axis=1)
        dka, dgk = norm_bwd(p_ref[:, KA:KA + 128], kn_ref[...], unrope(dka, 128), 128)
        dp_ref[:, KA:KA + 128] = dka.astype(BF16)
        dp_ref[:, VA:VA + 128] = dva.astype(BF16)
        dp_ref[:, KB:KB + 128] = unrope(dkb, 128).astype(BF16)
        dp_ref[:, VB:VB + 128] = dvb.astype(BF16)
        dqn_ref[0:1, :] += dgq
        dkn_ref[0:1, :] += dgk

    outs, couts = _call(
        body, name=name, grid=(m // TM,), in_specs=specs,
        out_specs=[pl.BlockSpec((TM, PROJ_W), lambda t: (t, 0)), pl.BlockSpec((8, 512), lambda t: (0, 0)),
                   pl.BlockSpec((8, 128), lambda t: (0, 0))],
        out_shape=[jax.ShapeDtypeStruct((m, PROJ_W), BF16), jax.ShapeDtypeStruct((8, 512), F32),
                   jax.ShapeDtypeStruct((8, 128), F32)],
        args=[proj, dq, dkv, qn, kn, cos, sin, emat], sem=("arbitrary",), comm=comm)
    return outs if comm is None else (outs, couts)


PAIR = 2
PAIR_HEADS = [(j, g) for j in range(PAIR) for g in range(GROUP)]


def _attn_case(t, pr, kv_ref, c_rows, t_rows, fn, keys_first=False):
    wl = TM + 2 * WINDOW
    kvd = lambda a, n: kv_ref[pl.ds(a, n), :]
    dense = pr == 0
    ctx = t == 0

    @pl.when(jnp.logical_and(dense, ctx))
    def _():
        fn(kvd(0, c_rows), None, False, [(0, c_rows)])

    @pl.when(jnp.logical_and(dense, jnp.logical_not(ctx)))
    def _():
        fn(kvd(0, t_rows), None, False, [(0, t_rows)])

    @pl.when(jnp.logical_and(jnp.logical_not(dense), ctx))
    def _():
        fn(kvd(0, c_rows), None, True, [(0, c_rows)])

    @pl.when(jnp.logical_and(jnp.logical_not(dense), jnp.logical_not(ctx)))
    def _():
        start = pl.multiple_of(jnp.minimum(c_rows + (t - 1) * TM - WINDOW, t_rows - wl), 128)
        kv = jnp.concatenate([kvd(0, c_rows), kvd(start, wl)], axis=0)
        shape = (c_rows + wl, TM) if keys_first else (TM, c_rows + wl)
        q_i = lax.broadcasted_iota(jnp.int32, shape, 1 if keys_first else 0)
        k_i = lax.broadcasted_iota(jnp.int32, shape, 0 if keys_first else 1)
        qpos = (t - 1) * TM + q_i
        kpos = start - 2 * c_rows + k_i
        mask = jnp.logical_or(k_i < c_rows, jnp.logical_and(jnp.abs(kpos - qpos) <= WINDOW, kpos >= 0))
        fn(kv, mask, True, [(0, c_rows), (start, wl)])


def _keys_values(kv, j):
    return kv[:, j * 128:j * 128 + 64], kv[:, j * 128 + 64:(j + 1) * 128]


def _head_cols(ref, j, g):
    a = j * Q_WIDTH + g * HEAD_DIM
    return ref[:, a:a + HEAD_DIM]


def _head_columns(cols):
    lane = lax.broadcasted_iota(jnp.int32, (TM, 128), 1)
    out = jnp.zeros((TM, 128), F32)
    for g, col in enumerate(cols):
        out = jnp.where(lane == g, col, out)
    return out


def _attn_specs(t_rows):
    nt = t_rows // TM
    q_spec = pl.BlockSpec((None, TM, PAIR * Q_WIDTH), lambda b, pr, t, s: (b, t, pr))
    kv_spec = pl.BlockSpec((None, t_rows, PAIR * 128), lambda b, pr, t, s: (b, 0, pr))
    lse_spec = pl.BlockSpec((PAIR, TM, 128), lambda b, pr, t, s: (pr, b * nt + t, 0))
    return q_spec, kv_spec, lse_spec


def _attn_fwd(q_all, kv_all, sink8, c_rows, name, comm=None):
    bl, t_rows, _ = q_all.shape
    q_spec, kv_spec, lse_spec = _attn_specs(t_rows)

    def body(sink_ref, q_ref, kv_ref, o_ref, ot_ref, lse_ref):
        pr, t = pl.program_id(1), pl.program_id(2)

        def fn(kv, mask, use_sink, spans):
            outs, lses = [], []
            ks = [_keys_values(kv, j)[0] for j in range(PAIR)]
            v_ones = [jnp.concatenate([_keys_values(kv, j)[1], jnp.ones((kv.shape[0], 64), BF16)], axis=1)
                      for j in range(PAIR)]

            def scores(i):
                j, g = PAIR_HEADS[i]
                s = lax.dot_general(_head_cols(q_ref, j, g), ks[j], (((1,), (1,)), ((), ())),
                                    preferred_element_type=F32)
                return s if mask is None else jnp.where(mask, s, NEG_BIG)

            s_next = scores(0)
            for i, (j, g) in enumerate(PAIR_HEADS):
                s = s_next
                if i + 1 < len(PAIR_HEADS):
                    s_next = scores(i + 1)
                mx = jnp.max(s, axis=1, keepdims=True)
                if use_sink:
                    sink = sink_ref[j * GROUP + g] * LOG2E
                    mx = jnp.maximum(mx, sink)
                pv = jnp.dot(jnp.exp2(s - mx).astype(BF16), v_ones[j], preferred_element_type=F32)
                l = pv[:, 64:65]
                if use_sink:
                    l = l + jnp.exp2(sink - mx)
                outs.append(pv[:, :64] * (1.0 / l))
                lses.append(mx + jnp.log2(l))
            o = jnp.concatenate(outs, axis=1).astype(BF16)
            o_ref[...] = o
            ot_ref[...] = o.T
            for j in range(PAIR):
                lse_ref[j] = _head_columns(lses[j * GROUP:(j + 1) * GROUP])

        _attn_case(t, pr, kv_ref, c_rows, t_rows, fn)

    nt = t_rows // TM
    ot_spec = pl.BlockSpec((PAIR * Q_WIDTH, TM), lambda b, pr, t, s: (pr, b * nt + t))
    outs, couts = _call(
        body, name=name, grid=(bl, N_HG // PAIR, nt), in_specs=[q_spec, kv_spec], out_specs=[q_spec, ot_spec, lse_spec],
        out_shape=[jax.ShapeDtypeStruct(q_all.shape, BF16), jax.ShapeDtypeStruct((N_HG * Q_WIDTH, bl * t_rows), BF16),
                   jax.ShapeDtypeStruct((N_HG, bl * t_rows, 128), F32)],
        args=[sink8, q_all, kv_all], prefetch=1, sem=("parallel", "parallel", "arbitrary"), comm=comm)
    return outs if comm is None else (outs, couts)


def _attn_bwd(q_all, kv_all, do, o, lse, sink8, c_rows, name, comm=None):
    bl, t_rows, _ = q_all.shape
    q_spec, kv_spec, lse_spec = _attn_specs(t_rows)
    ds_spec = pl.BlockSpec((None, PAIR, 8, 128), lambda b, pr, t, s: (b, pr, 0, 0))

    def body(sink_ref, q_ref, kv_ref, do_ref, o_ref, lse_ref, dq_ref, dkv_ref, dsk_ref):
        pr, t = pl.program_id(1), pl.program_id(2)

        @pl.when(t == 0)
        def _():
            dkv_ref[...] = jnp.zeros_like(dkv_ref)
            dsk_ref[...] = jnp.zeros_like(dsk_ref)

        lse_rows = [lse_ref[j].T for j in range(PAIR)]
        dd_rows = [_head_columns([jnp.sum(_head_cols(do_ref, j, g).astype(F32) * _head_cols(o_ref, j, g).astype(F32),
                                          axis=1, keepdims=True) for g in range(GROUP)]).T for j in range(PAIR)]

        def fn(kv, mask, use_sink, spans):
            ks, vs = zip(*[_keys_values(kv, j) for j in range(PAIR)])
            k_ts = [k.T for k in ks]
            dq_t = []
            dsinks = [[] for _ in range(PAIR)]
            dks = [jnp.zeros(ks[0].shape, F32) for _ in range(PAIR)]
            dvs = [jnp.zeros(ks[0].shape, F32) for _ in range(PAIR)]

            def products(i):
                j, g = PAIR_HEADS[i]
                s = lax.dot_general(ks[j], _head_cols(q_ref, j, g), (((1,), (1,)), ((), ())),
                                    preferred_element_type=F32)
                dp = lax.dot_general(vs[j], _head_cols(do_ref, j, g), (((1,), (1,)), ((), ())),
                                     preferred_element_type=F32)
                return (s if mask is None else jnp.where(mask, s, NEG_BIG)), dp

            nxt = products(0)
            for i, (j, g) in enumerate(PAIR_HEADS):
                q, do_g = _head_cols(q_ref, j, g), _head_cols(do_ref, j, g)
                lse_g, dd_g = lse_rows[j][g:g + 1, :], dd_rows[j][g:g + 1, :]
                s, dp = nxt
                if i + 1 < len(PAIR_HEADS):
                    nxt = products(i + 1)
                pb = jnp.exp2(s - lse_g).astype(BF16)
                ds = (pb.astype(F32) * (dp - dd_g)).astype(BF16)
                dks[j] = dks[j] + jnp.dot(ds, q, preferred_element_type=F32)
                dvs[j] = dvs[j] + jnp.dot(pb, do_g, preferred_element_type=F32)
                dq_t.append(jnp.dot(k_ts[j], ds, preferred_element_type=F32))
                if use_sink:
                    p_sink = jnp.exp2(sink_ref[j * GROUP + g] * LOG2E - lse_g)
                    dsinks[j].append(jnp.broadcast_to(-jnp.sum(p_sink * dd_g, axis=1, keepdims=True), (1, 128)))
            dq_ref[...] = jnp.concatenate(dq_t, axis=0).T
            dkv = jnp.concatenate([a for j in range(PAIR) for a in (dks[j] * (1.0 / LOG2E), dvs[j])], axis=1)
            off = 0
            for start, size in spans:
                dkv_ref[pl.ds(start, size), :] += dkv[off:off + size]
                off += size
            if use_sink:
                for j in range(PAIR):
                    dsk_ref[j, 0:GROUP, :] += jnp.concatenate(dsinks[j], axis=0)

        _attn_case(t, pr, kv_ref, c_rows, t_rows, fn, keys_first=True)

    outs, couts = _call(
        body, name=name, grid=(bl, N_HG // PAIR, t_rows // TM), in_specs=[q_spec, kv_spec, q_spec, q_spec, lse_spec],
        out_specs=[q_spec, kv_spec, ds_spec],
        out_shape=[jax.ShapeDtypeStruct(q_all.shape, F32), jax.ShapeDtypeStruct(kv_all.shape, F32),
                   jax.ShapeDtypeStruct((bl, N_HG, 8, 128), F32)],
        args=[sink8, q_all, kv_all, do, o, lse], prefetch=1, sem=("parallel", "parallel", "arbitrary"), comm=comm)
    return outs if comm is None else (outs, couts)


def _silu(x):
    return x * jax.nn.sigmoid(x)


def _ada_fwd(c_rows, w_ada, b_cols, name, comm=None):
    nl, d, w = w_ada.shape
    r = c_rows.shape[0]

    def body(c_ref, w_ref, b_ref, o_ref):
        s = _silu(c_ref[...]).astype(BF16)
        o_ref[...] = jnp.dot(s, w_ref[...].astype(BF16), preferred_element_type=F32) + b_ref[...]

    outs, couts = _call(
        body, name=name, grid=(nl,),
        in_specs=[pl.BlockSpec((r, d), lambda l: (0, 0)), pl.BlockSpec((None, d, w), lambda l: (l, 0, 0)),
                  pl.BlockSpec((None, 1, w), lambda l: (l, 0, 0))],
        out_specs=pl.BlockSpec((None, r, w), lambda l: (l, 0, 0)),
        out_shape=jax.ShapeDtypeStruct((nl, r, w), F32), args=[c_rows, w_ada, b_cols], sem=("parallel",), comm=comm)
    return outs if comm is None else (outs, couts)


def _ada_bwd(c_rows, c_ctx, dmod, w_ada, name):
    nl, d, w = w_ada.shape
    r = c_rows.shape[0]

    def body(c_ref, cc_ref, g_ref, w_ref, dw_ref, dc_ref):
        l = pl.program_id(0)
        s = _silu(c_ref[...]).astype(BF16)
        gm = g_ref[...].astype(BF16)
        dw_ref[...] = lax.dot_general(s, gm, (((0,), (0,)), ((), ())), preferred_element_type=F32)
        ds = lax.dot_general(gm, w_ref[...].astype(BF16), (((1,), (1,)), ((), ())), preferred_element_type=F32)
        rows = lax.broadcasted_iota(jnp.int32, ds.shape, 0)
        dsc = jnp.sum(jnp.where(rows % ADA_ROWS == 2, ds, 0.0), axis=0, keepdims=True)
        x = cc_ref[...]
        sg = jax.nn.sigmoid(x)
        dcc = dsc * (sg * (1.0 + x * (1.0 - sg)))
        out = jnp.concatenate([dcc, jnp.zeros((7, d), F32)], axis=0)

        @pl.when(l == 0)
        def _():
            dc_ref[...] = out

        @pl.when(l != 0)
        def _():
            dc_ref[...] += out

    return pl.pallas_call(
        body, name=name, grid=(nl,),
        in_specs=[pl.BlockSpec((r, d), lambda l: (0, 0)), pl.BlockSpec((1, d), lambda l: (0, 0)),
                  pl.BlockSpec((None, r, w), lambda l: (l, 0, 0)), pl.BlockSpec((None, d, w), lambda l: (l, 0, 0))],
        out_specs=[pl.BlockSpec((None, d, w), lambda l: (l, 0, 0)), pl.BlockSpec((8, d), lambda l: (0, 0))],
        out_shape=[jax.ShapeDtypeStruct((nl, d, w), F32), jax.ShapeDtypeStruct((8, d), F32)],
        compiler_params=_params(("arbitrary",)),
    )(c_rows, c_ctx, dmod, w_ada)


def _adam_math(w, g, m, v):
    m = ADAM_B1 * m + (1.0 - ADAM_B1) * g
    v = ADAM_B2 * v + (1.0 - ADAM_B2) * (g * g)
    m_hat = m / (1.0 - ADAM_B1 ** ADAM_STEP)
    v_hat = v / (1.0 - ADAM_B2 ** ADAM_STEP)
    delta = -ADAM_LR * (m_hat / (jnp.sqrt(v_hat) + ADAM_EPS) + ADAM_WD * w)
    return delta, m, v


def _adamw(w, m, v, g_own, g_recv, name, rows=256):
    nl, r, c = w.shape
    tr = min(rows, r)
    spec = pl.BlockSpec((None, tr, c), lambda l, i: (l, i, 0))
    per_layer = g_own is None
    own = [] if per_layer else [g_own]
    recv = [] if g_recv is None else list(g_recv)
    n_i = r // tr

    def rows_of(li):
        return lambda l, i: jnp.where(l == li, i, jnp.where(l > li, n_i - 1, 0))

    in_specs = [spec] * (3 + len(own))
    in_specs += [pl.BlockSpec((N_DEV, tr, c), lambda l, i, f=rows_of(li): (0, f(l, i), 0)) for li in range(len(recv))]

    def body(*refs):
        w_ref, m_ref, v_ref = refs[:3]
        own_refs, recv_refs = refs[3:3 + len(own)], refs[3 + len(own):3 + len(own) + len(recv)]
        go_ref, d_ref, mo_ref, vo_ref = refs[-4:]

        def update(li):
            if recv:
                g = recv_refs[li][0].astype(F32)
                for k in range(1, N_DEV):
                    g = g + recv_refs[li][k].astype(F32)
            else:
                g = own_refs[0][...]
            delta, m_, v_ = _adam_math(w_ref[...], g, m_ref[...], v_ref[...])
            go_ref[...] = g
            d_ref[...] = delta
            mo_ref[...] = m_
            vo_ref[...] = v_

        if per_layer:
            for li in range(nl):
                pl.when(pl.program_id(0) == li)(functools.partial(update, li))
        else:
            update(0)

    return pl.pallas_call(
        body, name=name, grid=(nl, n_i), in_specs=in_specs, out_specs=[spec] * 4,
        out_shape=[jax.ShapeDtypeStruct(w.shape, F32)] * 4, compiler_params=_params(("parallel", "parallel")),
    )(w, m, v, *own, *recv)


def _small_adamw(w, m, v, g_all, name):
    def body(w_ref, m_ref, v_ref, g_ref, go_ref, d_ref, mo_ref, vo_ref):
        g = g_ref[0]
        for k in range(1, N_DEV):
            g = g + g_ref[k]
        delta, m_, v_ = _adam_math(w_ref[...], g, m_ref[...], v_ref[...])
        go_ref[...] = g
        d_ref[...] = delta
        mo_ref[...] = m_
        vo_ref[...] = v_

    return pl.pallas_call(
        body, name=name, out_shape=[jax.ShapeDtypeStruct(w.shape, F32)] * 4, compiler_params=_params(),
    )(w, m, v, g_all)


SMALL = ("c_ctx", "b_ada", "g_pre_mix", "g_post_mix", "g_pre_mlp", "g_post_mlp", "q_norm", "k_norm", "sink", "loss")


def _pack_small(parts):
    flat = jnp.concatenate([parts[n].reshape(-1) for n in SMALL])
    rows = -(-flat.shape[0] // 1024) * 8
    return jnp.pad(flat, (0, rows * 128 - flat.shape[0])).reshape(rows, 128)


def _unpack_small(packed, like):
    flat = packed.reshape(-1)
    out, off = {}, 0
    for n in SMALL:
        size = int(np.prod(like[n].shape))
        out[n] = flat[off:off + size].reshape(like[n].shape)
        off += size
    return out


def kernel(x, c, ctx, c_ctx, w_ada, b_ada, g_pre_mix, g_post_mix, g_pre_mlp, g_post_mlp, w_in, q_norm, k_norm, sink, w_out, w_up, w_down, loss_target, m_c_ctx, m_w_ada, m_b_ada, m_g_pre_mix, m_g_post_mix, m_g_pre_mlp, m_g_post_mlp, m_w_in, m_q_norm, m_k_norm, m_sink, m_w_out, m_w_up, m_w_down, v_c_ctx, v_w_ada, v_b_ada, v_g_pre_mix, v_g_post_mix, v_g_pre_mlp, v_g_post_mlp, v_w_in, v_q_norm, v_k_norm, v_sink, v_w_out, v_w_up, v_w_down):
    bl, s_rows, d = x.shape
    c_rows = ctx.shape[1]
    assert c_rows == TM and s_rows % TM == 0 and bl == 2
    t_rows = c_rows + s_rows
    nt = t_rows // TM
    m_rows = bl * t_rows
    nl = w_in.shape[0]
    ada_w = w_ada.shape[2]
    d_ff = w_up.shape[2] * N_DEV
    me = _my_index()

    shard = lambda w_, l: w_[l].astype(BF16)
    c_pad = jnp.concatenate([c, c_ctx[None, :], jnp.zeros((ADA_ROWS - bl - 1, d), F32)], axis=0)
    gathered = {0: {}}
    h, (c_all, gathered[0]["w_in"]) = _token_stream(
        ctx, x, nt, "token_stream", comm=_Comm([(c_pad, GATHER_VIA_SIBLING), (shard(w_in, 0), GATHER_VIA_SIBLING)]))
    c_all = c_all.reshape(N_DEV * ADA_ROWS, d)

    def layer_weights(l):
        g_ = gathered[l]
        w_out_f = g_["w_out"].reshape(-1, d)
        w_down_f = g_["w_down"].reshape(d_ff, d)
        return dict(
            w_in_f=g_["w_in"].transpose(1, 0, 2).reshape(d, PROJ_W), w_out_f=w_out_f,
            w_up_s=g_["w_up"], w_up_t=g_["w_up"].transpose(0, 2, 1).reshape(d_ff, d), w_down_f=w_down_f)

    big = dict(tm=2304, tn=512)
    deep = dict(tm=1536, tn=512, tk=d_ff)
    wide = dict(tm=1024, tn=512, tk=m_rows)

    b_cols = lax.dynamic_slice(b_ada, (0, me * ada_w), (nl, ada_w))[:, None, :]
    mod_cols = _ada_fwd(c_all, w_ada, b_cols, "ada_fwd")
    mod_slots = mod_cols.reshape(nl, N_DEV, ADA_ROWS, ada_w).transpose(1, 0, 2, 3)
    mod_g, = _comm_only(_Comm([(mod_slots, TO_OWNER)]), "exchange_mod")
    mine = mod_g.transpose(1, 2, 0, 3).reshape(nl, ADA_ROWS, N_MOD, d)
    pad = jnp.zeros((bl, 2, MOD_ROWS - N_MOD, d), F32)
    modsel = [jnp.concatenate([jnp.stack([jnp.broadcast_to(mine[l, bl], (bl, N_MOD, d)), mine[l, :bl]], axis=1), pad],
                              axis=2) for l in range(nl)]

    cos, sin = _rope_tables(t_rows, c_rows)
    emat = _head_mean_matrix()
    row = lambda a: a[None, :]
    qn = [jnp.tile(q_norm[l], 8)[None, :] for l in range(nl)]
    kn = [jnp.tile(k_norm[l], 2)[None, :] for l in range(nl)]

    target = loss_target.reshape(bl * s_rows, d)
    saved = []
    weights_of = {}
    (u, u_t), (gathered[0]["w_out"],) = _norm_mod_fwd(
        h, row(g_pre_mix[0]), modsel[0], 0, 1, nt, "mix_mod_fwd0", comm=_Comm([(shard(w_out, 0), GATHER_VIA_SIBLING)]))
    for l in range(nl):
        w_in_f = gathered[l]["w_in"].transpose(1, 0, 2).reshape(d, PROJ_W)
        proj = _mm(u, w_in_f, name=f"mm_in{l}", tk=d, **big)
        q_all, kv_all = _prep_fwd(proj, qn[l], kn[l], cos, sin, emat, nt, f"prep_fwd{l}")
        (o, o_t, lse), (w_up_g, w_down_g) = _attn_fwd(
            q_all.reshape(bl, t_rows, 1024), kv_all.reshape(bl, t_rows, 512), sink[l], c_rows, f"attn_fwd{l}",
            comm=_Comm([(shard(w_up, l), GATHER), (shard(w_down, l), GATHER)]))
        o = o.reshape(m_rows, 1024)
        gathered[l].update(w_up=w_up_g, w_down=w_down_g)
        wl = weights_of[l] = layer_weights(l)
        mix = _mm(o, wl["w_out_f"], name=f"mm_out{l}", tk=1024, **big)
        h_mid, v_in, v_t = _gate_mod_fwd(h, mix, row(g_post_mix[l]), modsel[l], 2, row(g_pre_mlp[l]), modsel[l], 3, 4,
                                         nt, f"mix_gate_mlp_mod_fwd{l}")
        more = l + 1 < nl
        res_up = _mm(v_in, wl["w_up_s"], name=f"mm_up{l}", b_mode="nn_slots", epilogue="relu2", tk=d,
                     comm=_Comm([(shard(w_in, l + 1), GATHER)]) if more else None, **big)
        (r_act, r_t), nxt_in = res_up if more else (res_up, None)
        res_down = _mm(r_act, wl["w_down_f"], name=f"mm_down{l}",
                       comm=_Comm([(shard(w_out, l + 1), GATHER)]) if more else None, **deep)
        y, nxt_out = res_down if more else (res_down, None)
        saved.append((h, u_t, proj, q_all, kv_all, o, o_t, lse, mix, h_mid, v_t, r_act, r_t, y))
        if more:
            gathered[l + 1] = dict(w_in=nxt_in[0], w_out=nxt_out[0])
            h, u, u_t = _gate_mod_fwd(h_mid, y, row(g_post_mlp[l]), modsel[l], 5, row(g_pre_mix[l + 1]), modsel[l + 1],
                                      0, 1, nt, f"mlp_gate_mix_mod_fwd{l}")

    dh, dy, dvec_g2, ss = _gate_loss_bwd(h_mid, y, row(g_post_mlp[nl - 1]), modsel[nl - 1], 5, target, nt, "gate_loss_bwd")
    small_g = {n: [None] * nl for n in SMALL if n not in ("c_ctx", "b_ada", "loss")}
    dvecs = {l: {} for l in range(nl)}
    dvecs[nl - 1]["g2"] = dvec_g2
    slots = {n: [None] * nl for n in ("w_in", "w_out", "w_up", "w_down")}
    recvd = {n: [None] * nl for n in slots}
    send = lambda n, l_: (slots[n][l_], TO_OWNER_XOR)
    for l in reversed(range(nl)):
        h_in, u_t, proj, q_all, kv_all, o, o_t, lse, mix, h_mid, v_t, r_act, r_t, y = saved[l]
        wl = weights_of[l]
        da = _mm(dy, wl["w_down_f"], name=f"mm_da{l}", b_mode="nt", epilogue="relu2_bwd", extra=r_act, out_dtype=BF16,
                 tk=d, **big)
        dw_down = _mm(r_t, dy, name=f"mm_dw_down{l}", out_dtype=BF16, **wide)
        dw_up = _mm(v_t, da, name=f"mm_dw_up{l}", out_mode="slots", out_dtype=BF16, **wide)
        slots["w_down"][l] = dw_down.reshape(N_DEV, -1, d)
        slots["w_up"][l] = dw_up
        dv = _mm(da, wl["w_up_t"], name=f"mm_dv{l}", **deep)
        dh, dvecs[l]["m2"], dmix, dvecs[l]["g1"] = _mod_gate_bwd(
            h_mid, row(g_pre_mlp[l]), modsel[l], 3, 4, dv, dh, mix, row(g_post_mix[l]), modsel[l], 2, nt,
            f"mlp_mod_mix_gate_bwd{l}")
        do = _mm(dmix, wl["w_out_f"], name=f"mm_do{l}", b_mode="nt", out_dtype=BF16, tk=d, **big)
        dw_out = _mm(o_t, dmix, name=f"mm_dw_out{l}", out_dtype=BF16, **wide)
        slots["w_out"][l] = dw_out.reshape(N_DEV, -1, d)
        going = [("w_down", l), ("w_up", l)] + ([("w_out", l + 1), ("w_in", l + 1)] if l + 1 < nl else [])
        (dq, dkv, dsk), arrived = _attn_bwd(
            q_all.reshape(bl, t_rows, 1024), kv_all.reshape(bl, t_rows, 512), do.reshape(bl, t_rows, 1024),
            o.reshape(bl, t_rows, 1024), lse, sink[l], c_rows, f"attn_bwd{l}", comm=_Comm([send(*g_) for g_ in going]))
        for (n_, l_), got in zip(going, arrived):
            recvd[n_][l_] = got
        last = l == 0
        res = _prep_bwd(proj, dq.reshape(m_rows, 1024), dkv.reshape(m_rows, 512), qn[l], kn[l], cos, sin, emat, nt,
                        f"prep_bwd{l}", comm=_Comm([send("w_out", l)]) if last else None)
        dproj, dqn, dkn = res[0] if last else res
        if last:
            recvd["w_out"][l] = res[1][0]
        dw_in = _mm(u_t, dproj, name=f"mm_dw_in{l}", out_dtype=BF16, **wide)
        slots["w_in"][l] = dw_in.reshape(d, N_DEV, PROJ_W // N_DEV).transpose(1, 0, 2)
        if last:
            halves = [(slots["w_in"][l][:, :d // 2], TO_OWNER_XOR), (slots["w_in"][l][:, d // 2:], TO_OWNER_XOR)]
            du, (got_a,) = _mm(dproj, wl["w_in_f"], name=f"mm_du{l}", b_mode="nt", tk=PROJ_W,
                               comm=_Comm(halves[:1]), **big)
            (dh, dvecs[l]["m1"]), (got_b,) = _norm_mod_bwd(
                h_in, row(g_pre_mix[l]), modsel[l], du, dh, 0, 1, nt, f"mix_mod_bwd{l}", latent_only=True,
                comm=_Comm(halves[1:]))
            recvd["w_in"][l] = jnp.concatenate([got_a, got_b], axis=1)
        else:
            du = _mm(dproj, wl["w_in_f"], name=f"mm_du{l}", b_mode="nt", tk=PROJ_W, **big)
            dh, dvecs[l]["m1"], dy, dvecs[l - 1]["g2"] = _mod_gate_bwd(
                h_in, row(g_pre_mix[l]), modsel[l], 0, 1, du, dh, saved[l - 1][-1], row(g_post_mlp[l - 1]),
                modsel[l - 1], 5, nt, f"mix_mod_mlp_gate_bwd{l}")
        small_g["q_norm"][l] = jnp.sum(dqn[0].reshape(8, HEAD_DIM), axis=0)
        small_g["k_norm"][l] = jnp.sum(dkn[0].reshape(2, HEAD_DIM), axis=0)
        small_g["sink"][l] = jnp.sum(dsk[:, 2:, :GROUP, 0], axis=0).reshape(-1)

    dmod_rows = []
    for l in range(nl):
        m1, g1, m2, g2 = (dvecs[l][n] for n in ("m1", "g1", "m2", "g2"))
        small_g["g_pre_mix"][l] = jnp.sum(m1[:, :, 2], axis=(0, 1))
        small_g["g_post_mix"][l] = jnp.sum(g1[:, :, 1], axis=(0, 1))
        small_g["g_pre_mlp"][l] = jnp.sum(m2[:, :, 2], axis=(0, 1))
        small_g["g_post_mlp"][l] = jnp.sum(g2[:, :, 1], axis=(0, 1))
        dms = jnp.stack([m1[:, :, 0], m1[:, :, 1], g1[:, :, 0], m2[:, :, 0], m2[:, :, 1], g2[:, :, 0]], axis=2)
        rows = jnp.concatenate([dms[:, 1], jnp.sum(dms[:, 0], axis=0)[None]], axis=0)
        dmod_rows.append(jnp.pad(rows.reshape(bl + 1, N_MOD * d), ((0, ADA_ROWS - bl - 1), (0, 0))))
    grad_x = dh.reshape(bl, s_rows, d)

    dmod_slots = jnp.stack(dmod_rows).reshape(nl, ADA_ROWS, N_DEV, ada_w).transpose(2, 0, 1, 3)
    dmod_g, = _comm_only(_Comm([(dmod_slots, TO_OWNER)]), "exchange_dmod")
    dmod_mine = dmod_g.transpose(1, 0, 2, 3).reshape(nl, N_DEV * ADA_ROWS, ada_w)
    dw_ada, dcc = _ada_bwd(c_all, c_ctx[None, :], dmod_mine, w_ada, "ada_bwd")

    parts = {n: jnp.stack(small_g[n]) for n in small_g}
    parts["c_ctx"] = dcc[0]
    parts["loss"] = ss[0, 0:1]
    no_loss = jnp.zeros((1,), F32)
    parts["b_ada"] = jnp.stack([jnp.sum(r_[: bl + 1], axis=0) for r_ in dmod_rows])
    weights = dict(c_ctx=c_ctx, b_ada=b_ada, g_pre_mix=g_pre_mix, g_post_mix=g_post_mix, g_pre_mlp=g_pre_mlp,
                   g_post_mlp=g_post_mlp, q_norm=q_norm, k_norm=k_norm, sink=sink, loss=no_loss)
    moms = dict(c_ctx=m_c_ctx, b_ada=m_b_ada, g_pre_mix=m_g_pre_mix, g_post_mix=m_g_post_mix, g_pre_mlp=m_g_pre_mlp,
                g_post_mlp=m_g_post_mlp, q_norm=m_q_norm, k_norm=m_k_norm, sink=m_sink, loss=no_loss)
    vels = dict(c_ctx=v_c_ctx, b_ada=v_b_ada, g_pre_mix=v_g_pre_mix, g_post_mix=v_g_post_mix, g_pre_mlp=v_g_pre_mlp,
                g_post_mlp=v_g_post_mlp, q_norm=v_q_norm, k_norm=v_k_norm, sink=v_sink, loss=no_loss)
    small_all, = _comm_only(_Comm([(_pack_small(parts), GATHER)]), "gather_small")
    s_out = _small_adamw(_pack_small(weights), _pack_small(moms), _pack_small(vels), small_all, "adamw_small")
    s_g, s_d, s_m, s_v = [_unpack_small(a, weights) for a in s_out]
    loss = 0.5 * s_g["loss"][0] / d

    res = {}
    for n, w_, m_, v_ in (("w_in", w_in, m_w_in, v_w_in), ("w_out", w_out, m_w_out, v_w_out),
                          ("w_up", w_up, m_w_up, v_w_up), ("w_down", w_down, m_w_down, v_w_down)):
        res[n] = _adamw(w_, m_, v_, None, recvd[n], f"adamw_{n}")
    res["w_ada"] = _adamw(w_ada, m_w_ada, v_w_ada, dw_ada, None, "adamw_w_ada")

    order = ("c_ctx", "w_ada", "b_ada", "g_pre_mix", "g_post_mix", "g_pre_mlp", "g_post_mlp", "w_in", "q_norm",
             "k_norm", "sink", "w_out", "w_up", "w_down")
    outs = [loss, grad_x]
    for i, small in enumerate((s_g, s_d, s_m, s_v)):
        outs += [small[n] if n in small else res[n][i] for n in order]
    return tuple(outs)
```

```python
import functools

import jax
import jax.numpy as jnp
import numpy as np
from jax import lax
from jax.experimental import pallas as pl
from jax.experimental.pallas import tpu as pltpu

F32 = jnp.float32
BF16 = jnp.bfloat16

HEAD_DIM = 64
GROUP = 4
N_HG = 4
Q_WIDTH = GROUP * HEAD_DIM
WINDOW = 128
GRID_W = 64
ROPE_THETA = 10000.0
EPS = 1e-6
NEG_BIG = -1e30
N_MOD = 6
MOD_ROWS = 8
TM = 256
N_DEV = 8
ADA_ROWS = 8
VMEM_LIMIT = 56 * 1024 * 1024

ADAM_LR = 0.001
ADAM_B1 = 0.9
ADAM_B2 = 0.999
ADAM_EPS = 1e-08
ADAM_WD = 0.01
ADAM_STEP = 10


def _params(sem=None):
    kw = dict(vmem_limit_bytes=VMEM_LIMIT)
    if sem is not None:
        kw["dimension_semantics"] = sem
    return pltpu.CompilerParams(**kw)


def _my_index():
    return 4 * lax.axis_index("x") + 2 * lax.axis_index("y") + lax.axis_index("c")


def _peer(k):
    x, y, c = lax.axis_index("x"), lax.axis_index("y"), lax.axis_index("c")
    kx, ky, kc = (k >> 2) & 1, (k >> 1) & 1, k & 1
    px = (1 - x) if kx else x
    py = (1 - y) if ky else y
    pc = (1 - c) if kc else c
    return (px, py, pc), 4 * px + 2 * py + pc


GATHER, GATHER_VIA_SIBLING, TO_OWNER, TO_OWNER_XOR = "gather", "gather_via_sibling", "to_owner", "to_owner_xor"


class _Comm:
    def __init__(self, items):
        self.items = list(items)
        self.arrays = [a for a, _ in self.items]

    def out_shapes(self):
        return [jax.ShapeDtypeStruct(((N_DEV,) + a.shape) if kind in (GATHER, GATHER_VIA_SIBLING) else a.shape, a.dtype)
                for a, kind in self.items]

    def sem_shapes(self):
        n = len(self.items) * N_DEV
        return [pltpu.SemaphoreType.DMA((n,)), pltpu.SemaphoreType.DMA((n,))]

    def _two_level(self, x_ref, o_ref, send_sems, recv_sems, base):
        x, y, c = lax.axis_index("x"), lax.axis_index("y"), lax.axis_index("c")
        me, sibling = (x, y, c), (x, y, 1 - c)
        chips = [(1 - x, y), (x, 1 - y), (1 - x, 1 - y)]
        block = lambda p: o_ref.at[4 * p[0] + 2 * p[1] + p[2]]

        def copy(k, owner, to, src=None):
            return pltpu.make_async_remote_copy(
                src_ref=block(owner) if src is None else src, dst_ref=block(owner), send_sem=send_sems.at[base + k],
                recv_sem=recv_sems.at[base + k], device_id=to, device_id_type=pl.DeviceIdType.MESH)

        first = [copy(1, me, sibling, src=x_ref)] + [copy(2 + j, me, (*chip, c), src=x_ref) for j, chip in enumerate(chips)]
        relay = [(copy(2 + j, (*chip, c), me), copy(5 + j, (*chip, c), sibling)) for j, chip in enumerate(chips)]
        last = [copy(1, sibling, me)] + [copy(5 + j, (*chip, 1 - c), me) for j, chip in enumerate(chips)]
        return first, relay, last

    def _copies(self, in_refs, out_refs, send_sems, recv_sems):
        me = _my_index()
        local, remote = [], []
        for i, ((_, kind), x_ref, o_ref) in enumerate(zip(self.items, in_refs, out_refs)):
            base = i * N_DEV
            if kind == GATHER_VIA_SIBLING:
                local.append(pltpu.make_async_copy(x_ref, o_ref.at[me], send_sems.at[base]))
                continue
            own_src = x_ref if kind == GATHER else x_ref.at[me]
            own_dst = o_ref.at[0] if kind == TO_OWNER_XOR else o_ref.at[me]
            local.append(pltpu.make_async_copy(own_src, own_dst, send_sems.at[base]))
            for k in range(1, N_DEV):
                peer, pidx = _peer(k)
                remote.append(pltpu.make_async_remote_copy(
                    src_ref=x_ref if kind == GATHER else x_ref.at[pidx],
                    dst_ref=o_ref.at[k] if kind == TO_OWNER_XOR else o_ref.at[me],
                    send_sem=send_sems.at[base + k], recv_sem=recv_sems.at[base + k],
                    device_id=peer, device_id_type=pl.DeviceIdType.MESH))
        return local, remote

    def start(self, in_refs, out_refs, send_sems, recv_sems):
        local, remote = self._copies(in_refs, out_refs, send_sems, recv_sems)
        for cp in local + remote:
            cp.start()
        for i, ((_, kind), x_ref, o_ref) in enumerate(zip(self.items, in_refs, out_refs)):
            if kind == GATHER_VIA_SIBLING:
                for cp in self._two_level(x_ref, o_ref, send_sems, recv_sems, i * N_DEV)[0]:
                    cp.start()

    def wait(self, in_refs, out_refs, send_sems, recv_sems):
        local, remote = self._copies(in_refs, out_refs, send_sems, recv_sems)
        for cp in remote:
            cp.wait_recv()
        for cp in remote:
            cp.wait_send()
        for i, ((_, kind), x_ref, o_ref) in enumerate(zip(self.items, in_refs, out_refs)):
            if kind == GATHER_VIA_SIBLING:
                first, relay, last = self._two_level(x_ref, o_ref, send_sems, recv_sems, i * N_DEV)
                for arrival, onward in relay:
                    arrival.wait_recv()
                    onward.start()
                for cp in last:
                    cp.wait_recv()
                for cp in first + [onward for _, onward in relay]:
                    cp.wait_send()
        for cp in local:
            cp.wait()


def _call(body, *, name, grid, in_specs, out_specs, out_shape, args, scratch_shapes=(), prefetch=0, sem=None,
          comm=None):
    single = not isinstance(out_shape, (list, tuple))
    out_shape = [out_shape] if single else list(out_shape)
    out_specs = [out_specs] if single else list(out_specs)
    in_specs, scratch_shapes, args = list(in_specs), list(scratch_shapes), list(args)
    n_in, n_out = len(in_specs), len(out_shape)
    if comm is not None:
        nc = len(comm.arrays)
        hbm = pl.BlockSpec(memory_space=pl.ANY)
        inner = body

        def body(*refs):
            pre, r = refs[:prefetch], refs[prefetch:]
            ins, cin = r[:n_in], r[n_in:n_in + nc]
            outs, cout = r[n_in + nc:n_in + nc + n_out], r[n_in + nc + n_out:n_in + 2 * nc + n_out]
            scr, sems = r[n_in + 2 * nc + n_out:len(r) - 2], r[len(r) - 2:]
            ids = [pl.program_id(i) for i in range(len(grid))]

            def when(flags, fn):
                if flags:
                    pl.when(functools.reduce(jnp.logical_and, flags))(fn)
                else:
                    fn()

            when([i == 0 for i in ids], lambda: comm.start(cin, cout, *sems))
            inner(*pre, *ins, *outs, *scr)
            when([i == n - 1 for i, n in zip(ids, grid)], lambda: comm.wait(cin, cout, *sems))

        in_specs += [hbm] * nc
        out_specs += [hbm] * nc
        out_shape += comm.out_shapes()
        scratch_shapes += comm.sem_shapes()
        args += comm.arrays
        sem = ("arbitrary",) * len(grid)
    kw = dict(name=name, out_shape=out_shape, compiler_params=_params(sem if grid else None))
    if prefetch:
        kw["grid_spec"] = pltpu.PrefetchScalarGridSpec(
            num_scalar_prefetch=prefetch, grid=grid, in_specs=in_specs, out_specs=out_specs,
            scratch_shapes=scratch_shapes)
    else:
        kw.update(in_specs=in_specs, out_specs=out_specs, scratch_shapes=scratch_shapes)
        if grid:
            kw["grid"] = grid
    res = list(pl.pallas_call(body, **kw)(*args))
    outs = res[:n_out]
    return (outs[0] if single else outs), res[n_out:]


def _comm_only(comm, name):
    return _call(lambda: None, name=name, grid=(), in_specs=[], out_specs=[], out_shape=[], args=[], comm=comm)[1]


def _mm(a, b, *, name, ta=False, b_mode="nn", out_mode="plain", out_dtype=F32, tm=512, tn=512, tk=512,
        epilogue=None, extra=None, comm=None):
    if ta:
        kdim, m = a.shape
    else:
        m, kdim = a.shape
    if b_mode == "nn":
        n = b.shape[1]
    elif b_mode == "nt":
        n = b.shape[0]
    elif b_mode == "nn_slots":
        n = b.shape[0] * b.shape[2]
        tn = b.shape[2]
    else:
        n = b.shape[1]
        tk = b.shape[2]
    if out_mode == "slots":
        tn = n // N_DEV
    tm, tn, tk = min(tm, m), min(tn, n), min(tk, kdim)
    assert m % tm == 0 and n % tn == 0 and kdim % tk == 0, (name, m, n, kdim, tm, tn, tk)
    nk = kdim // tk

    a_spec = pl.BlockSpec((tk, tm), lambda i, j, k: (k, i)) if ta else pl.BlockSpec((tm, tk), lambda i, j, k: (i, k))
    if b_mode == "nn":
        b_spec = pl.BlockSpec((tk, tn), lambda i, j, k: (k, j))
    elif b_mode == "nt":
        b_spec = pl.BlockSpec((tn, tk), lambda i, j, k: (j, k))
    elif b_mode == "nn_slots":
        b_spec = pl.BlockSpec((None, tk, tn), lambda i, j, k: (j, k, 0))
    else:
        b_spec = pl.BlockSpec((None, tn, tk), lambda i, j, k: (k, j, 0))
    tb = b_mode in ("nt", "nt_slots")
    if out_mode == "plain":
        o_shape, o_spec = (m, n), pl.BlockSpec((tm, tn), lambda i, j, k: (i, j))
    else:
        o_shape, o_spec = (N_DEV, m, tn), pl.BlockSpec((None, tm, tn), lambda i, j, k: (j, i, 0))
    dims = (((0 if ta else 1,), (1 if tb else 0,)), ((), ()))

    in_specs = [a_spec, b_spec]
    args = [a, b]
    if epilogue == "relu2_bwd":
        in_specs.append(pl.BlockSpec((tm, tn), lambda i, j, k: (i, j)))
        args.append(extra)
    if epilogue == "relu2":
        out_shape = [jax.ShapeDtypeStruct(o_shape, BF16), jax.ShapeDtypeStruct((n, m), BF16)]
        out_specs = [o_spec, pl.BlockSpec((tn, tm), lambda i, j, k: (j, i))]
    else:
        out_shape = jax.ShapeDtypeStruct(o_shape, out_dtype)
        out_specs = o_spec

    def finish(refs, acc):
        if epilogue == "relu2":
            r = jnp.maximum(acc, 0.0)
            r2 = (r * r).astype(BF16)
            refs[2][...] = r2
            refs[3][...] = r2.T
        elif epilogue == "relu2_bwd":
            refs[3][...] = (acc * (2.0 * jnp.sqrt(refs[2][...]).astype(F32))).astype(out_dtype)
        else:
            refs[2][...] = acc.astype(out_dtype)

    def body(*refs):
        part = lax.dot_general(refs[0][...], refs[1][...], dims, preferred_element_type=F32)
        if nk == 1:
            finish(refs, part)
            return
        acc_ref = refs[-1]
        k = pl.program_id(2)

        @pl.when(k == 0)
        def _():
            acc_ref[...] = part

        @pl.when(jnp.logical_and(k > 0, k < nk - 1))
        def _():
            acc_ref[...] += part

        @pl.when(k == nk - 1)
        def _():
            finish(refs, acc_ref[...] + part)

    outs, couts = _call(
        body, name=name, grid=(m // tm, n // tn, nk), in_specs=in_specs, out_specs=out_specs, out_shape=out_shape,
        args=args, scratch_shapes=[] if nk == 1 else [pltpu.VMEM((tm, tn), F32)],
        sem=("parallel", "parallel", "arbitrary"), comm=comm)
    return outs if comm is None else (outs, couts)


def _row_specs(d, nt):
    row = pl.BlockSpec((TM, d), lambda t: (t, 0))
    vec = pl.BlockSpec((1, d), lambda t: (0, 0))
    mod = pl.BlockSpec((None, None, MOD_ROWS, d), lambda t: (t // nt, jnp.minimum(t % nt, 1), 0, 0))
    return row, vec, mod


def _rms(x):
    r = lax.rsqrt(jnp.mean(x * x, axis=1, keepdims=True) + EPS)
    return r, x * r


def _token_stream(ctx, x, nt, name, comm=None):
    bl, c_rows, d = ctx.shape
    m = bl * nt * TM

    def body(ctx_ref, x_ref, h_ref):
        t = pl.program_id(0)

        @pl.when(t % nt == 0)
        def _():
            h_ref[...] = ctx_ref[...]

        @pl.when(t % nt != 0)
        def _():
            h_ref[...] = x_ref[...]

    outs, couts = _call(
        body, name=name, grid=(m // TM,),
        in_specs=[pl.BlockSpec((None, TM, d), lambda t: (t // nt, 0, 0)),
                  pl.BlockSpec((None, TM, d), lambda t: (t // nt, jnp.maximum(t % nt, 1) - 1, 0))],
        out_specs=pl.BlockSpec((TM, d), lambda t: (t, 0)), out_shape=jax.ShapeDtypeStruct((m, d), F32),
        args=[ctx, x], sem=("parallel",), comm=comm)
    return outs if comm is None else (outs, couts)


def _norm_mod_fwd(h, g, modsel, i_sh, i_sc, nt, name, comm=None):
    m, d = h.shape
    row, vec, mod = _row_specs(d, nt)

    def body(h_ref, g_ref, ms_ref, u_ref, ut_ref):
        _, xh = _rms(h_ref[...])
        ms = ms_ref[...]
        u = (xh * g_ref[...] * (1.0 + ms[i_sc:i_sc + 1]) + ms[i_sh:i_sh + 1]).astype(BF16)
        u_ref[...] = u
        ut_ref[...] = u.T

    outs, couts = _call(
        body, name=name, grid=(m // TM,), in_specs=[row, vec, mod],
        out_specs=[row, pl.BlockSpec((d, TM), lambda t: (0, t))],
        out_shape=[jax.ShapeDtypeStruct((m, d), BF16), jax.ShapeDtypeStruct((d, m), BF16)],
        args=[h, g, modsel], sem=("parallel",), comm=comm)
    return outs if comm is None else (outs, couts)


def _acc_rows(t, nt, dvec_ref, rows):
    first = (t % nt) <= 1

    @pl.when(first)
    def _():
        dvec_ref[...] = rows

    @pl.when(jnp.logical_not(first))
    def _():
        dvec_ref[...] += rows


def _norm_mod_bwd(h, g, modsel, du, dh_in, i_sh, i_sc, nt, name, comm=None, latent_only=False):
    m, d = h.shape
    row, vec, mod = _row_specs(d, nt)
    dh_rows, dh_spec = m, row
    if latent_only:
        dh_rows = m // nt * (nt - 1)
        dh_spec = pl.BlockSpec((TM, d), lambda t: ((t // nt) * (nt - 1) + jnp.maximum(t % nt, 1) - 1, 0))

    def body(h_ref, g_ref, ms_ref, du_ref, dhi_ref, dh_ref, dvec_ref):
        t = pl.program_id(0)
        r, xh = _rms(h_ref[...])
        g_ = g_ref[...]
        ms = ms_ref[...]
        du_ = du_ref[...]
        y = xh * g_
        dy = du_ * (1.0 + ms[i_sc:i_sc + 1])
        dxh = dy * g_
        dx = r * (dxh - xh * jnp.mean(dxh * xh, axis=1, keepdims=True))
        dh_ref[...] = dhi_ref[...] + dx
        rows = jnp.concatenate([
            jnp.sum(du_, axis=0, keepdims=True), jnp.sum(du_ * y, axis=0, keepdims=True),
            jnp.sum(dy * xh, axis=0, keepdims=True), jnp.zeros((MOD_ROWS - 3, d), F32)], axis=0)
        _acc_rows(t, nt, dvec_ref, rows)

    outs, couts = _call(
        body, name=name, grid=(m // TM,), in_specs=[row, vec, mod, row, row], out_specs=[dh_spec, mod],
        out_shape=[jax.ShapeDtypeStruct((dh_rows, d), F32), jax.ShapeDtypeStruct(modsel.shape, F32)],
        args=[h, g, modsel, du, dh_in], sem=("arbitrary",), comm=comm)
    return outs if comm is None else (outs, couts)


def _gate_rows(dh_, xh, g_, gate):
    dy = dh_ * gate
    return dy * g_, jnp.sum(dh_ * (xh * g_), axis=0, keepdims=True), jnp.sum(dy * xh, axis=0, keepdims=True)


def _gate_mod_fwd(h, z, g_post, ms_gate, i_g, g_pre, ms_mod, i_sh, i_sc, nt, name):
    m, d = h.shape
    row, vec, mod = _row_specs(d, nt)

    def body(h_ref, z_ref, gp_ref, msg_ref, gq_ref, msm_ref, hn_ref, u_ref, ut_ref):
        _, zh = _rms(z_ref[...])
        hn = h_ref[...] + msg_ref[...][i_g:i_g + 1] * (zh * gp_ref[...])
        hn_ref[...] = hn
        _, xh = _rms(hn)
        ms = msm_ref[...]
        u = (xh * gq_ref[...] * (1.0 + ms[i_sc:i_sc + 1]) + ms[i_sh:i_sh + 1]).astype(BF16)
        u_ref[...] = u
        ut_ref[...] = u.T

    return pl.pallas_call(
        body, name=name, grid=(m // TM,), in_specs=[row, row, vec, mod, vec, mod],
        out_specs=[row, row, pl.BlockSpec((d, TM), lambda t: (0, t))],
        out_shape=[jax.ShapeDtypeStruct((m, d), F32), jax.ShapeDtypeStruct((m, d), BF16),
                   jax.ShapeDtypeStruct((d, m), BF16)],
        compiler_params=_params(("parallel",)),
    )(h, z, g_post, ms_gate, g_pre, ms_mod)


def _mod_gate_bwd(h, g_pre, ms_mod, i_sh, i_sc, du, dh_in, z, g_post, ms_gate, i_g, nt, name):
    m, d = h.shape
    row, vec, mod = _row_specs(d, nt)

    def body(h_ref, gq_ref, msm_ref, du_ref, dhi_ref, z_ref, gp_ref, msg_ref, dh_ref, dvm_ref, dz_ref, dvg_ref):
        t = pl.program_id(0)
        r, xh = _rms(h_ref[...])
        gq = gq_ref[...]
        ms = msm_ref[...]
        du_ = du_ref[...]
        dy = du_ * (1.0 + ms[i_sc:i_sc + 1])
        dxh = dy * gq
        dh_ = dhi_ref[...] + r * (dxh - xh * jnp.mean(dxh * xh, axis=1, keepdims=True))
        dh_ref[...] = dh_
        _acc_rows(t, nt, dvm_ref, jnp.concatenate([
            jnp.sum(du_, axis=0, keepdims=True), jnp.sum(du_ * (xh * gq), axis=0, keepdims=True),
            jnp.sum(dy * xh, axis=0, keepdims=True), jnp.zeros((MOD_ROWS - 3, d), F32)], axis=0))
        rz, zh = _rms(z_ref[...])
        dzh, d_gate, d_gp = _gate_rows(dh_, zh, gp_ref[...], msg_ref[...][i_g:i_g + 1])
        dz_ref[...] = (rz * (dzh - zh * jnp.mean(dzh * zh, axis=1, keepdims=True))).astype(BF16)
        _acc_rows(t, nt, dvg_ref, jnp.concatenate([d_gate, d_gp, jnp.zeros((MOD_ROWS - 2, d), F32)], axis=0))

    return pl.pallas_call(
        body, name=name, grid=(m // TM,), in_specs=[row, vec, mod, row, row, row, vec, mod],
        out_specs=[row, mod, row, mod],
        out_shape=[jax.ShapeDtypeStruct((m, d), F32), jax.ShapeDtypeStruct(ms_mod.shape, F32),
                   jax.ShapeDtypeStruct((m, d), BF16), jax.ShapeDtypeStruct(ms_gate.shape, F32)],
        compiler_params=_params(("arbitrary",)),
    )(h, g_pre, ms_mod, du, dh_in, z, g_post, ms_gate)


def _gate_loss_bwd(h, z, g_post, modsel, i_g, target, nt, name):
    m, d = h.shape
    row, vec, mod = _row_specs(d, nt)
    ntl = nt - 1
    tgt = pl.BlockSpec((TM, d), lambda t: ((t // nt) * ntl + jnp.maximum(t % nt, 1) - 1, 0))
    acc = pl.BlockSpec((8, 128), lambda t: (0, 0))

    def body(h_ref, z_ref, gp_ref, ms_ref, t_ref, dh_ref, dz_ref, dvg_ref, ss_ref):
        t = pl.program_id(0)

        @pl.when(t == 0)
        def _():
            ss_ref[...] = jnp.zeros_like(ss_ref)

        latent = (t % nt != 0).astype(F32)
        rz, zh = _rms(z_ref[...])
        gp = gp_ref[...]
        gate = ms_ref[...][i_g:i_g + 1]
        e = h_ref[...] + gate * (zh * gp) - t_ref[...]
        ss_ref[...] += latent * jnp.sum(e * e)
        dh_ = e * (latent / d)
        dh_ref[...] = dh_
        dzh, d_gate, d_gp = _gate_rows(dh_, zh, gp, gate)
        dz_ref[...] = (rz * (dzh - zh * jnp.mean(dzh * zh, axis=1, keepdims=True))).astype(BF16)
        _acc_rows(t, nt, dvg_ref, jnp.concatenate([d_gate, d_gp, jnp.zeros((MOD_ROWS - 2, d), F32)], axis=0))

    return pl.pallas_call(
        body, name=name, grid=(m // TM,), in_specs=[row, row, vec, mod, tgt], out_specs=[row, row, mod, acc],
        out_shape=[jax.ShapeDtypeStruct((m, d), F32), jax.ShapeDtypeStruct((m, d), BF16),
                   jax.ShapeDtypeStruct(modsel.shape, F32), jax.ShapeDtypeStruct((8, 128), F32)],
        compiler_params=_params(("arbitrary",)),
    )(h, z, g_post, modsel, target)


QA, KA, VA, QB, KB, VB = 0, 512, 640, 768, 1280, 1408
PROJ_W = 1536
Q_SCALE = HEAD_DIM ** -0.5
LOG2E = 1.4426950408889634


def _swap16(x):
    lane = lax.broadcasted_iota(jnp.int32, x.shape, 1)
    n = x.shape[1]
    return jnp.where((lane % 32) < 16, pltpu.roll(x, n - 16, 1), pltpu.roll(x, 16, 1))


def _seg_mean(x, e):
    hi = x.astype(BF16)
    lo = (x - hi.astype(F32)).astype(BF16)
    w = e.shape[0]
    both = lambda a: jnp.dot(hi[:, a:a + w], e, preferred_element_type=F32) + jnp.dot(lo[:, a:a + w], e, preferred_element_type=F32)
    parts = [both(a) for a in range(0, x.shape[1], w)]
    return parts[0] if len(parts) == 1 else jnp.concatenate(parts, axis=1)


def _rope_tables(t_rows, c_rows):
    s = t_rows - c_rows
    row_ids = jnp.repeat(jnp.arange(s // GRID_W, dtype=jnp.int32), GRID_W).astype(F32)
    col_ids = jnp.tile(jnp.arange(GRID_W, dtype=jnp.int32), s // GRID_W).astype(F32)
    axis_dim = HEAD_DIM // 2
    inv = ROPE_THETA ** (-jnp.arange(0, axis_dim, 2, dtype=F32) / axis_dim)
    ang_r = row_ids[:, None] * inv[None, :]
    ang_c = col_ids[:, None] * inv[None, :]
    cos = jnp.concatenate([jnp.cos(ang_r), jnp.cos(ang_r), jnp.cos(ang_c), jnp.cos(ang_c)], axis=1)
    sin = jnp.concatenate([-jnp.sin(ang_r), jnp.sin(ang_r), -jnp.sin(ang_c), jnp.sin(ang_c)], axis=1)
    cos = jnp.concatenate([jnp.ones((c_rows, HEAD_DIM), F32), cos], axis=0)
    sin = jnp.concatenate([jnp.zeros((c_rows, HEAD_DIM), F32), sin], axis=0)
    return jnp.tile(cos, (1, 8)), jnp.tile(sin, (1, 8))


def _head_mean_matrix():
    i = np.arange(512)
    return jnp.asarray((i[:, None] // HEAD_DIM == i[None, :] // HEAD_DIM).astype(np.float32) / HEAD_DIM, dtype=BF16)


def _interleave_kv(k, v):
    return jnp.concatenate([k[:, :64], v[:, :64], k[:, 64:], v[:, 64:]], axis=1)


def _prep_fwd(proj, qn, kn, cos, sin, emat, nt, name):
    m = proj.shape[0]
    specs = [
        pl.BlockSpec((TM, PROJ_W), lambda t: (t, 0)),
        pl.BlockSpec((1, 512), lambda t: (0, 0)), pl.BlockSpec((1, 128), lambda t: (0, 0)),
        pl.BlockSpec((TM, 512), lambda t: (t % nt, 0)), pl.BlockSpec((TM, 512), lambda t: (t % nt, 0)),
        pl.BlockSpec((512, 512), lambda t: (0, 0)),
    ]

    def body(p_ref, qn_ref, kn_ref, cos_ref, sin_ref, e_ref, q_ref, kv_ref):
        cos_, sin_, e = cos_ref[...], sin_ref[...], e_ref[...]

        def rope(x, w):
            return x * cos_[:, :w] + _swap16(x) * sin_[:, :w]

        def norm(x, g, w):
            return x * lax.rsqrt(_seg_mean(x * x, e[:min(w, 256), :min(w, 256)]) + EPS) * g

        qa = rope(norm(p_ref[:, QA:QA + 512], qn_ref[...], 512), 512)
        qb = rope(p_ref[:, QB:QB + 512], 512)
        q_ref[:, 0:512] = (qa * (Q_SCALE * LOG2E)).astype(BF16)
        q_ref[:, 512:1024] = (qb * (Q_SCALE * LOG2E)).astype(BF16)
        ka = rope(norm(p_ref[:, KA:KA + 128], kn_ref[...], 128), 128)
        kb = rope(p_ref[:, KB:KB + 128], 128)
        kv_ref[:, 0:256] = _interleave_kv(ka, p_ref[:, VA:VA + 128]).astype(BF16)
        kv_ref[:, 256:512] = _interleave_kv(kb, p_ref[:, VB:VB + 128]).astype(BF16)

    return pl.pallas_call(
        body, name=name, grid=(m // TM,), in_specs=specs,
        out_specs=[pl.BlockSpec((TM, 1024), lambda t: (t, 0)), pl.BlockSpec((TM, 512), lambda t: (t, 0))],
        out_shape=[jax.ShapeDtypeStruct((m, 1024), BF16), jax.ShapeDtypeStruct((m, 512), BF16)],
        compiler_params=_params(("parallel",)),
    )(proj, qn, kn, cos, sin, emat)


def _prep_bwd(proj, dq, dkv, qn, kn, cos, sin, emat, nt, name, comm=None):
    m = proj.shape[0]
    specs = [
        pl.BlockSpec((TM, PROJ_W), lambda t: (t, 0)),
        pl.BlockSpec((TM, 1024), lambda t: (t, 0)), pl.BlockSpec((TM, 512), lambda t: (t, 0)),
        pl.BlockSpec((1, 512), lambda t: (0, 0)), pl.BlockSpec((1, 128), lambda t: (0, 0)),
        pl.BlockSpec((TM, 512), lambda t: (t % nt, 0)), pl.BlockSpec((TM, 512), lambda t: (t % nt, 0)),
        pl.BlockSpec((512, 512), lambda t: (0, 0)),
    ]

    def body(p_ref, dq_ref, dkv_ref, qn_ref, kn_ref, cos_ref, sin_ref, e_ref, dp_ref, dqn_ref, dkn_ref):
        t = pl.program_id(0)
        cos_, sin_, e = cos_ref[...], sin_ref[...], e_ref[...]

        @pl.when(t == 0)
        def _():
            dqn_ref[...] = jnp.zeros_like(dqn_ref)
            dkn_ref[...] = jnp.zeros_like(dkn_ref)

        def unrope(dy, w):
            return dy * cos_[:, :w] + _swap16(dy * sin_[:, :w])

        def norm_bwd(x, g, dy, w):
            r = lax.rsqrt(_seg_mean(x * x, e[:min(w, 256), :min(w, 256)]) + EPS)
            xh = x * r
            dxh = dy * g
            dx = r * (dxh - xh * _seg_mean(dxh * xh, e[:min(w, 256), :min(w, 256)]))
            return dx, jnp.sum(dy * xh, axis=0, keepdims=True)

        dqa, dgq = norm_bwd(p_ref[:, QA:QA + 512], qn_ref[...], unrope(dq_ref[:, 0:512] * Q_SCALE, 512), 512)
        dp_ref[:, QA:QA + 512] = dqa.astype(BF16)
        dp_ref[:, QB:QB + 512] = unrope(dq_ref[:, 512:1024] * Q_SCALE, 512).astype(BF16)
        da = dkv_ref[:, 0:256]
        db = dkv_ref[:, 256:512]
        dka = jnp.concatenate([da[:, 0:64], da[:, 128:192]], axis=1)
        dva = jnp.concatenate([da[:, 64:128], da[:, 192:256]], axis=1)
        dkb = jnp.concatenate([db[:, 0:64], db[:, 128:192]], axis=1)
        dvb = jnp.concatenate([db[:, 64:128], db[:, 192:256]], axis=1)
        dka, dgk = norm_bwd(p_ref[:, KA:KA + 128], kn_ref[...], unrope(dka, 128), 128)
        dp_ref[:, KA:KA + 128] = dka.astype(BF16)
        dp_ref[:, VA:VA + 128] = dva.astype(BF16)
        dp_ref[:, KB:KB + 128] = unrope(dkb, 128).astype(BF16)
        dp_ref[:, VB:VB + 128] = dvb.astype(BF16)
        dqn_ref[0:1, :] += dgq
        dkn_ref[0:1, :] += dgk

    outs, couts = _call(
        body, name=name, grid=(m // TM,), in_specs=specs,
        out_specs=[pl.BlockSpec((TM, PROJ_W), lambda t: (t, 0)), pl.BlockSpec((8, 512), lambda t: (0, 0)),
                   pl.BlockSpec((8, 128), lambda t: (0, 0))],
        out_shape=[jax.ShapeDtypeStruct((m, PROJ_W), BF16), jax.ShapeDtypeStruct((8, 512), F32),
                   jax.ShapeDtypeStruct((8, 128), F32)],
        args=[proj, dq, dkv, qn, kn, cos, sin, emat], sem=("arbitrary",), comm=comm)
    return outs if comm is None else (outs, couts)


PAIR = 2
PAIR_HEADS = [(j, g) for j in range(PAIR) for g in range(GROUP)]
LOOKAHEAD = 4
LOOKAHEAD_BWD = 2


def _attn_case(t, pr, kv_ref, c_rows, t_rows, fn, keys_first=False):
    wl = TM + 2 * WINDOW
    kvd = lambda a, n: kv_ref[pl.ds(a, n), :]
    dense = pr == 0
    ctx = t == 0

    @pl.when(jnp.logical_and(dense, ctx))
    def _():
        fn(kvd(0, c_rows), None, False, [(0, c_rows)])

    @pl.when(jnp.logical_and(dense, jnp.logical_not(ctx)))
    def _():
        fn(kvd(0, t_rows), None, False, [(0, t_rows)])

    @pl.when(jnp.logical_and(jnp.logical_not(dense), ctx))
    def _():
        fn(kvd(0, c_rows), None, True, [(0, c_rows)])

    @pl.when(jnp.logical_and(jnp.logical_not(dense), jnp.logical_not(ctx)))
    def _():
        start = pl.multiple_of(jnp.minimum(c_rows + (t - 1) * TM - WINDOW, t_rows - wl), 128)
        kv = jnp.concatenate([kvd(0, c_rows), kvd(start, wl)], axis=0)
        shape = (c_rows + wl, TM) if keys_first else (TM, c_rows + wl)
        q_i = lax.broadcasted_iota(jnp.int32, shape, 1 if keys_first else 0)
        k_i = lax.broadcasted_iota(jnp.int32, shape, 0 if keys_first else 1)
        qpos = (t - 1) * TM + q_i
        kpos = start - 2 * c_rows + k_i
        mask = jnp.logical_or(k_i < c_rows, jnp.logical_and(jnp.abs(kpos - qpos) <= WINDOW, kpos >= 0))
        fn(kv, mask, True, [(0, c_rows), (start, wl)])


def _keys_values(kv, j):
    return kv[:, j * 128:j * 128 + 64], kv[:, j * 128 + 64:(j + 1) * 128]


def _head_cols(ref, j, g):
    a = j * Q_WIDTH + g * HEAD_DIM
    return ref[:, a:a + HEAD_DIM]


def _head_columns(cols):
    lane = lax.broadcasted_iota(jnp.int32, (TM, 128), 1)
    out = jnp.zeros((TM, 128), F32)
    for g, col in enumerate(cols):
        out = jnp.where(lane == g, col, out)
    return out


def _attn_specs(t_rows):
    nt = t_rows // TM
    q_spec = pl.BlockSpec((None, TM, PAIR * Q_WIDTH), lambda b, pr, t, s: (b, t, pr))
    kv_spec = pl.BlockSpec((None, t_rows, PAIR * 128), lambda b, pr, t, s: (b, 0, pr))
    lse_spec = pl.BlockSpec((PAIR, TM, 128), lambda b, pr, t, s: (pr, b * nt + t, 0))
    return q_spec, kv_spec, lse_spec


def _attn_fwd(q_all, kv_all, sink8, c_rows, name, comm=None):
    bl, t_rows, _ = q_all.shape
    q_spec, kv_spec, lse_spec = _attn_specs(t_rows)

    def body(sink_ref, q_ref, kv_ref, o_ref, ot_ref, lse_ref):
        pr, t = pl.program_id(1), pl.program_id(2)

        def fn(kv, mask, use_sink, spans):
            outs, lses = [], []
            ks = [_keys_values(kv, j)[0] for j in range(PAIR)]
            v_ones = [jnp.concatenate([_keys_values(kv, j)[1], jnp.ones((kv.shape[0], 64), BF16)], axis=1)
                      for j in range(PAIR)]

            def scores(i):
                j, g = PAIR_HEADS[i]
                s = lax.dot_general(_head_cols(q_ref, j, g), ks[j], (((1,), (1,)), ((), ())),
                                    preferred_element_type=F32)
                return s if mask is None else jnp.where(mask, s, NEG_BIG)

            ahead = [scores(i) for i in range(LOOKAHEAD)]
            for i, (j, g) in enumerate(PAIR_HEADS):
                s = ahead.pop(0)
                if i + LOOKAHEAD < len(PAIR_HEADS):
                    ahead.append(scores(i + LOOKAHEAD))
                mx = jnp.max(s, axis=1, keepdims=True)
                if use_sink:
                    sink = sink_ref[j * GROUP + g] * LOG2E
                    mx = jnp.maximum(mx, sink)
                pv = jnp.dot(jnp.exp2(s - mx).astype(BF16), v_ones[j], preferred_element_type=F32)
                l = pv[:, 64:65]
                if use_sink:
                    l = l + jnp.exp2(sink - mx)
                outs.append(pv[:, :64] * (1.0 / l))
                lses.append(mx + jnp.log2(l))
            o = jnp.concatenate(outs, axis=1).astype(BF16)
            o_ref[...] = o
            ot_ref[...] = o.T
            for j in range(PAIR):
                lse_ref[j] = _head_columns(lses[j * GROUP:(j + 1) * GROUP])

        _attn_case(t, pr, kv_ref, c_rows, t_rows, fn)

    nt = t_rows // TM
    ot_spec = pl.BlockSpec((PAIR * Q_WIDTH, TM), lambda b, pr, t, s: (pr, b * nt + t))
    outs, couts = _call(
        body, name=name, grid=(bl, N_HG // PAIR, nt), in_specs=[q_spec, kv_spec], out_specs=[q_spec, ot_spec, lse_spec],
        out_shape=[jax.ShapeDtypeStruct(q_all.shape, BF16), jax.ShapeDtypeStruct((N_HG * Q_WIDTH, bl * t_rows), BF16),
                   jax.ShapeDtypeStruct((N_HG, bl * t_rows, 128), F32)],
        args=[sink8, q_all, kv_all], prefetch=1, sem=("parallel", "parallel", "arbitrary"), comm=comm)
    return outs if comm is None else (outs, couts)


def _attn_bwd(q_all, kv_all, do, o, lse, sink8, c_rows, name, comm=None):
    bl, t_rows, _ = q_all.shape
    q_spec, kv_spec, lse_spec = _attn_specs(t_rows)
    ds_spec = pl.BlockSpec((None, PAIR, 8, 128), lambda b, pr, t, s: (b, pr, 0, 0))

    def body(sink_ref, q_ref, kv_ref, do_ref, o_ref, lse_ref, dq_ref, dkv_ref, dsk_ref):
        pr, t = pl.program_id(1), pl.program_id(2)

        @pl.when(t == 0)
        def _():
            dkv_ref[...] = jnp.zeros_like(dkv_ref)
            dsk_ref[...] = jnp.zeros_like(dsk_ref)

        lse_rows = [lse_ref[j].T for j in range(PAIR)]
        dd_rows = [_head_columns([jnp.sum(_head_cols(do_ref, j, g).astype(F32) * _head_cols(o_ref, j, g).astype(F32),
                                          axis=1, keepdims=True) for g in range(GROUP)]).T for j in range(PAIR)]

        def fn(kv, mask, use_sink, spans):
            ks, vs = zip(*[_keys_values(kv, j) for j in range(PAIR)])
            k_ts = [k.T for k in ks]
            dq_t = []
            dsinks = [[] for _ in range(PAIR)]
            dks = [jnp.zeros(ks[0].shape, F32) for _ in range(PAIR)]
            dvs = [jnp.zeros(ks[0].shape, F32) for _ in range(PAIR)]

            def products(i):
                j, g = PAIR_HEADS[i]
                s = lax.dot_general(ks[j], _head_cols(q_ref, j, g), (((1,), (1,)), ((), ())),
                                    preferred_element_type=F32)
                dp = lax.dot_general(vs[j], _head_cols(do_ref, j, g), (((1,), (1,)), ((), ())),
                                     preferred_element_type=F32)
                return (s if mask is None else jnp.where(mask, s, NEG_BIG)), dp

            ahead = [products(i) for i in range(LOOKAHEAD_BWD)]
            for i, (j, g) in enumerate(PAIR_HEADS):
                q, do_g = _head_cols(q_ref, j, g), _head_cols(do_ref, j, g)
                lse_g, dd_g = lse_rows[j][g:g + 1, :], dd_rows[j][g:g + 1, :]
                s, dp = ahead.pop(0)
                if i + LOOKAHEAD_BWD < len(PAIR_HEADS):
                    ahead.append(products(i + LOOKAHEAD_BWD))
                pb = jnp.exp2(s - lse_g).astype(BF16)
                ds = (pb.astype(F32) * (dp - dd_g)).astype(BF16)
                dks[j] = dks[j] + jnp.dot(ds, q, preferred_element_type=F32)
                dvs[j] = dvs[j] + jnp.dot(pb, do_g, preferred_element_type=F32)
                dq_t.append(jnp.dot(k_ts[j], ds, preferred_element_type=F32))
                if use_sink:
                    p_sink = jnp.exp2(sink_ref[j * GROUP + g] * LOG2E - lse_g)
                    dsinks[j].append(jnp.broadcast_to(-jnp.sum(p_sink * dd_g, axis=1, keepdims=True), (1, 128)))
            dq_ref[...] = jnp.concatenate(dq_t, axis=0).T
            dkv = jnp.concatenate([a for j in range(PAIR) for a in (dks[j] * (1.0 / LOG2E), dvs[j])], axis=1)
            off = 0
            for start, size in spans:
                dkv_ref[pl.ds(start, size), :] += dkv[off:off + size]
                off += size
            if use_sink:
                for j in range(PAIR):
                    dsk_ref[j, 0:GROUP, :] += jnp.concatenate(dsinks[j], axis=0)

        _attn_case(t, pr, kv_ref, c_rows, t_rows, fn, keys_first=True)

    outs, couts = _call(
        body, name=name, grid=(bl, N_HG // PAIR, t_rows // TM), in_specs=[q_spec, kv_spec, q_spec, q_spec, lse_spec],
        out_specs=[q_spec, kv_spec, ds_spec],
        out_shape=[jax.ShapeDtypeStruct(q_all.shape, F32), jax.ShapeDtypeStruct(kv_all.shape, F32),
                   jax.ShapeDtypeStruct((bl, N_HG, 8, 128), F32)],
        args=[sink8, q_all, kv_all, do, o, lse], prefetch=1, sem=("parallel", "parallel", "arbitrary"), comm=comm)
    return outs if comm is None else (outs, couts)


def _silu(x):
    return x * jax.nn.sigmoid(x)


def _ada_fwd(c_rows, w_ada, b_cols, name, comm=None):
    nl, d, w = w_ada.shape
    r = c_rows.shape[0]

    def body(c_ref, w_ref, b_ref, o_ref):
        s = _silu(c_ref[...]).astype(BF16)
        o_ref[...] = jnp.dot(s, w_ref[...].astype(BF16), preferred_element_type=F32) + b_ref[...]

    outs, couts = _call(
        body, name=name, grid=(nl,),
        in_specs=[pl.BlockSpec((r, d), lambda l: (0, 0)), pl.BlockSpec((None, d, w), lambda l: (l, 0, 0)),
                  pl.BlockSpec((None, 1, w), lambda l: (l, 0, 0))],
        out_specs=pl.BlockSpec((None, r, w), lambda l: (l, 0, 0)),
        out_shape=jax.ShapeDtypeStruct((nl, r, w), F32), args=[c_rows, w_ada, b_cols], sem=("parallel",), comm=comm)
    return outs if comm is None else (outs, couts)


def _ada_bwd(c_rows, c_ctx, dmod, w_ada, name):
    nl, d, w = w_ada.shape
    r = c_rows.shape[0]

    def body(c_ref, cc_ref, g_ref, w_ref, dw_ref, dc_ref):
        l = pl.program_id(0)
        s = _silu(c_ref[...]).astype(BF16)
        gm = g_ref[...].astype(BF16)
        dw_ref[...] = lax.dot_general(s, gm, (((0,), (0,)), ((), ())), preferred_element_type=F32)
        ds = lax.dot_general(gm, w_ref[...].astype(BF16), (((1,), (1,)), ((), ())), preferred_element_type=F32)
        rows = lax.broadcasted_iota(jnp.int32, ds.shape, 0)
        dsc = jnp.sum(jnp.where(rows % ADA_ROWS == 2, ds, 0.0), axis=0, keepdims=True)
        x = cc_ref[...]
        sg = jax.nn.sigmoid(x)
        dcc = dsc * (sg * (1.0 + x * (1.0 - sg)))
        out = jnp.concatenate([dcc, jnp.zeros((7, d), F32)], axis=0)

        @pl.when(l == 0)
        def _():
            dc_ref[...] = out

        @pl.when(l != 0)
        def _():
            dc_ref[...] += out

    return pl.pallas_call(
        body, name=name, grid=(nl,),
        in_specs=[pl.BlockSpec((r, d), lambda l: (0, 0)), pl.BlockSpec((1, d), lambda l: (0, 0)),
                  pl.BlockSpec((None, r, w), lambda l: (l, 0, 0)), pl.BlockSpec((None, d, w), lambda l: (l, 0, 0))],
        out_specs=[pl.BlockSpec((None, d, w), lambda l: (l, 0, 0)), pl.BlockSpec((8, d), lambda l: (0, 0))],
        out_shape=[jax.ShapeDtypeStruct((nl, d, w), F32), jax.ShapeDtypeStruct((8, d), F32)],
        compiler_params=_params(("arbitrary",)),
    )(c_rows, c_ctx, dmod, w_ada)


def _adam_math(w, g, m, v):
    m = ADAM_B1 * m + (1.0 - ADAM_B1) * g
    v = ADAM_B2 * v + (1.0 - ADAM_B2) * (g * g)
    m_hat = m / (1.0 - ADAM_B1 ** ADAM_STEP)
    v_hat = v / (1.0 - ADAM_B2 ** ADAM_STEP)
    delta = -ADAM_LR * (m_hat / (jnp.sqrt(v_hat) + ADAM_EPS) + ADAM_WD * w)
    return delta, m, v


def _adamw(w, m, v, g_own, g_recv, name, rows=256):
    nl, r, c = w.shape
    tr = min(rows, r)
    spec = pl.BlockSpec((None, tr, c), lambda l, i: (l, i, 0))
    per_layer = g_own is None
    own = [] if per_layer else [g_own]
    recv = [] if g_recv is None else list(g_recv)
    n_i = r // tr

    def rows_of(li):
        return lambda l, i: jnp.where(l == li, i, jnp.where(l > li, n_i - 1, 0))

    in_specs = [spec] * (3 + len(own))
    in_specs += [pl.BlockSpec((N_DEV, tr, c), lambda l, i, f=rows_of(li): (0, f(l, i), 0)) for li in range(len(recv))]

    def body(*refs):
        w_ref, m_ref, v_ref = refs[:3]
        own_refs, recv_refs = refs[3:3 + len(own)], refs[3 + len(own):3 + len(own) + len(recv)]
        go_ref, d_ref, mo_ref, vo_ref = refs[-4:]

        def update(li):
            if recv:
                g = recv_refs[li][0].astype(F32)
                for k in range(1, N_DEV):
                    g = g + recv_refs[li][k].astype(F32)
            else:
                g = own_refs[0][...]
            delta, m_, v_ = _adam_math(w_ref[...], g, m_ref[...], v_ref[...])
            go_ref[...] = g
            d_ref[...] = delta
            mo_ref[...] = m_
            vo_ref[...] = v_

        if per_layer:
            for li in range(nl):
                pl.when(pl.program_id(0) == li)(functools.partial(update, li))
        else:
            update(0)

    return pl.pallas_call(
        body, name=name, grid=(nl, n_i), in_specs=in_specs, out_specs=[spec] * 4,
        out_shape=[jax.ShapeDtypeStruct(w.shape, F32)] * 4, compiler_params=_params(("parallel", "parallel")),
    )(w, m, v, *own, *recv)


def _small_adamw(w, m, v, g_all, name):
    def body(w_ref, m_ref, v_ref, g_ref, go_ref, d_ref, mo_ref, vo_ref):
        g = g_ref[0]
        for k in range(1, N_DEV):
            g = g + g_ref[k]
        delta, m_, v_ = _adam_math(w_ref[...], g, m_ref[...], v_ref[...])
        go_ref[...] = g
        d_ref[...] = delta
        mo_ref[...] = m_
        vo_ref[...] = v_

    return pl.pallas_call(
        body, name=name, out_shape=[jax.ShapeDtypeStruct(w.shape, F32)] * 4, compiler_params=_params(),
    )(w, m, v, g_all)


SMALL = ("c_ctx", "b_ada", "g_pre_mix", "g_post_mix", "g_pre_mlp", "g_post_mlp", "q_norm", "k_norm", "sink", "loss")


def _pack_small(parts):
    flat = jnp.concatenate([parts[n].reshape(-1) for n in SMALL])
    rows = -(-flat.shape[0] // 1024) * 8
    return jnp.pad(flat, (0, rows * 128 - flat.shape[0])).reshape(rows, 128)


def _unpack_small(packed, like):
    flat = packed.reshape(-1)
    out, off = {}, 0
    for n in SMALL:
        size = int(np.prod(like[n].shape))
        out[n] = flat[off:off + size].reshape(like[n].shape)
        off += size
    return out


def kernel(x, c, ctx, c_ctx, w_ada, b_ada, g_pre_mix, g_post_mix, g_pre_mlp, g_post_mlp, w_in, q_norm, k_norm, sink, w_out, w_up, w_down, loss_target, m_c_ctx, m_w_ada, m_b_ada, m_g_pre_mix, m_g_post_mix, m_g_pre_mlp, m_g_post_mlp, m_w_in, m_q_norm, m_k_norm, m_sink, m_w_out, m_w_up, m_w_down, v_c_ctx, v_w_ada, v_b_ada, v_g_pre_mix, v_g_post_mix, v_g_pre_mlp, v_g_post_mlp, v_w_in, v_q_norm, v_k_norm, v_sink, v_w_out, v_w_up, v_w_down):
    bl, s_rows, d = x.shape
    c_rows = ctx.shape[1]
    assert c_rows == TM and s_rows % TM == 0 and bl == 2
    t_rows = c_rows + s_rows
    nt = t_rows // TM
    m_rows = bl * t_rows
    nl = w_in.shape[0]
    ada_w = w_ada.shape[2]
    d_ff = w_up.shape[2] * N_DEV
    me = _my_index()

    shard = lambda w_, l: w_[l].astype(BF16)
    c_pad = jnp.concatenate([c, c_ctx[None, :], jnp.zeros((ADA_ROWS - bl - 1, d), F32)], axis=0)
    gathered = {0: {}}
    h, (c_all, gathered[0]["w_in"]) = _token_stream(
        ctx, x, nt, "token_stream", comm=_Comm([(c_pad, GATHER_VIA_SIBLING), (shard(w_in, 0), GATHER_VIA_SIBLING)]))
    c_all = c_all.reshape(N_DEV * ADA_ROWS, d)

    def layer_weights(l):
        g_ = gathered[l]
        w_out_f = g_["w_out"].reshape(-1, d)
        w_down_f = g_["w_down"].reshape(d_ff, d)
        return dict(
            w_in_f=g_["w_in"].transpose(1, 0, 2).reshape(d, PROJ_W), w_out_f=w_out_f,
            w_up_s=g_["w_up"], w_up_t=g_["w_up"].transpose(0, 2, 1).reshape(d_ff, d), w_down_f=w_down_f)

    big = dict(tm=2304, tn=512)
    deep = dict(tm=1536, tn=512, tk=d_ff)
    wide = dict(tm=1024, tn=512, tk=m_rows)

    b_cols = lax.dynamic_slice(b_ada, (0, me * ada_w), (nl, ada_w))[:, None, :]
    mod_cols = _ada_fwd(c_all, w_ada, b_cols, "ada_fwd")
    mod_slots = mod_cols.reshape(nl, N_DEV, ADA_ROWS, ada_w).transpose(1, 0, 2, 3)
    mod_g, = _comm_only(_Comm([(mod_slots, TO_OWNER)]), "exchange_mod")
    mine = mod_g.transpose(1, 2, 0, 3).reshape(nl, ADA_ROWS, N_MOD, d)
    pad = jnp.zeros((bl, 2, MOD_ROWS - N_MOD, d), F32)
    modsel = [jnp.concatenate([jnp.stack([jnp.broadcast_to(mine[l, bl], (bl, N_MOD, d)), mine[l, :bl]], axis=1), pad],
                              axis=2) for l in range(nl)]

    cos, sin = _rope_tables(t_rows, c_rows)
    emat = _head_mean_matrix()
    row = lambda a: a[None, :]
    qn = [jnp.tile(q_norm[l], 8)[None, :] for l in range(nl)]
    kn = [jnp.tile(k_norm[l], 2)[None, :] for l in range(nl)]

    target = loss_target.reshape(bl * s_rows, d)
    saved = []
    weights_of = {}
    (u, u_t), (gathered[0]["w_out"],) = _norm_mod_fwd(
        h, row(g_pre_mix[0]), modsel[0], 0, 1, nt, "mix_mod_fwd0", comm=_Comm([(shard(w_out, 0), GATHER_VIA_SIBLING)]))
    for l in range(nl):
        w_in_f = gathered[l]["w_in"].transpose(1, 0, 2).reshape(d, PROJ_W)
        proj = _mm(u, w_in_f, name=f"mm_in{l}", tk=d, **big)
        q_all, kv_all = _prep_fwd(proj, qn[l], kn[l], cos, sin, emat, nt, f"prep_fwd{l}")
        (o, o_t, lse), (w_up_g, w_down_g) = _attn_fwd(
            q_all.reshape(bl, t_rows, 1024), kv_all.reshape(bl, t_rows, 512), sink[l], c_rows, f"attn_fwd{l}",
            comm=_Comm([(shard(w_up, l), GATHER), (shard(w_down, l), GATHER)]))
        o = o.reshape(m_rows, 1024)
        gathered[l].update(w_up=w_up_g, w_down=w_down_g)
        wl = weights_of[l] = layer_weights(l)
        mix = _mm(o, wl["w_out_f"], name=f"mm_out{l}", tk=1024, **big)
        h_mid, v_in, v_t = _gate_mod_fwd(h, mix, row(g_post_mix[l]), modsel[l], 2, row(g_pre_mlp[l]), modsel[l], 3, 4,
                                         nt, f"mix_gate_mlp_mod_fwd{l}")
        more = l + 1 < nl
        res_up = _mm(v_in, wl["w_up_s"], name=f"mm_up{l}", b_mode="nn_slots", epilogue="relu2", tk=d,
                     comm=_Comm([(shard(w_in, l + 1), GATHER)]) if more else None, **big)
        (r_act, r_t), nxt_in = res_up if more else (res_up, None)
        res_down = _mm(r_act, wl["w_down_f"], name=f"mm_down{l}",
                       comm=_Comm([(shard(w_out, l + 1), GATHER)]) if more else None, **deep)
        y, nxt_out = res_down if more else (res_down, None)
        saved.append((h, u_t, proj, q_all, kv_all, o, o_t, lse, mix, h_mid, v_t, r_act, r_t, y))
        if more:
            gathered[l + 1] = dict(w_in=nxt_in[0], w_out=nxt_out[0])
            h, u, u_t = _gate_mod_fwd(h_mid, y, row(g_post_mlp[l]), modsel[l], 5, row(g_pre_mix[l + 1]), modsel[l + 1],
                                      0, 1, nt, f"mlp_gate_mix_mod_fwd{l}")

    dh, dy, dvec_g2, ss = _gate_loss_bwd(h_mid, y, row(g_post_mlp[nl - 1]), modsel[nl - 1], 5, target, nt, "gate_loss_bwd")
    small_g = {n: [None] * nl for n in SMALL if n not in ("c_ctx", "b_ada", "loss")}
    dvecs = {l: {} for l in range(nl)}
    dvecs[nl - 1]["g2"] = dvec_g2
    slots = {n: [None] * nl for n in ("w_in", "w_out", "w_up", "w_down")}
    recvd = {n: [None] * nl for n in slots}
    send = lambda n, l_: (slots[n][l_], TO_OWNER_XOR)
    for l in reversed(range(nl)):
        h_in, u_t, proj, q_all, kv_all, o, o_t, lse, mix, h_mid, v_t, r_act, r_t, y = saved[l]
        wl = weights_of[l]
        da = _mm(dy, wl["w_down_f"], name=f"mm_da{l}", b_mode="nt", epilogue="relu2_bwd", extra=r_act, out_dtype=BF16,
                 tk=d, **big)
        dw_down = _mm(r_t, dy, name=f"mm_dw_down{l}", out_dtype=BF16, **wide)
        dw_up = _mm(v_t, da, name=f"mm_dw_up{l}", out_mode="slots", out_dtype=BF16, **wide)
        slots["w_down"][l] = dw_down.reshape(N_DEV, -1, d)
        slots["w_up"][l] = dw_up
        dv = _mm(da, wl["w_up_t"], name=f"mm_dv{l}", **deep)
        dh, dvecs[l]["m2"], dmix, dvecs[l]["g1"] = _mod_gate_bwd(
            h_mid, row(g_pre_mlp[l]), modsel[l], 3, 4, dv, dh, mix, row(g_post_mix[l]), modsel[l], 2, nt,
            f"mlp_mod_mix_gate_bwd{l}")
        do = _mm(dmix, wl["w_out_f"], name=f"mm_do{l}", b_mode="nt", out_dtype=BF16, tk=d, **big)
        dw_out = _mm(o_t, dmix, name=f"mm_dw_out{l}", out_dtype=BF16, **wide)
        slots["w_out"][l] = dw_out.reshape(N_DEV, -1, d)
        going = [("w_down", l), ("w_up", l)] + ([("w_out", l + 1), ("w_in", l + 1)] if l + 1 < nl else [])
        (dq, dkv, dsk), arrived = _attn_bwd(
            q_all.reshape(bl, t_rows, 1024), kv_all.reshape(bl, t_rows, 512), do.reshape(bl, t_rows, 1024),
            o.reshape(bl, t_rows, 1024), lse, sink[l], c_rows, f"attn_bwd{l}", comm=_Comm([send(*g_) for g_ in going]))
        for (n_, l_), got in zip(going, arrived):
            recvd[n_][l_] = got
        last = l == 0
        res = _prep_bwd(proj, dq.reshape(m_rows, 1024), dkv.reshape(m_rows, 512), qn[l], kn[l], cos, sin, emat, nt,
                        f"prep_bwd{l}", comm=_Comm([send("w_out", l)]) if last else None)
        dproj, dqn, dkn = res[0] if last else res
        if last:
            recvd["w_out"][l] = res[1][0]
        dw_in = _mm(u_t, dproj, name=f"mm_dw_in{l}", out_dtype=BF16, **wide)
        slots["w_in"][l] = dw_in.reshape(d, N_DEV, PROJ_W // N_DEV).transpose(1, 0, 2)
        if last:
            halves = [(slots["w_in"][l][:, :d // 2], TO_OWNER_XOR), (slots["w_in"][l][:, d // 2:], TO_OWNER_XOR)]
            du, (got_a,) = _mm(dproj, wl["w_in_f"], name=f"mm_du{l}", b_mode="nt", tk=PROJ_W,
                               comm=_Comm(halves[:1]), **big)
            (dh, dvecs[l]["m1"]), (got_b,) = _norm_mod_bwd(
                h_in, row(g_pre_mix[l]), modsel[l], du, dh, 0, 1, nt, f"mix_mod_bwd{l}", latent_only=True,
                comm=_Comm(halves[1:]))
            recvd["w_in"][l] = jnp.concatenate([got_a, got_b], axis=1)
        else:
            du = _mm(dproj, wl["w_in_f"], name=f"mm_du{l}", b_mode="nt", tk=PROJ_W, **big)
            dh, dvecs[l]["m1"], dy, dvecs[l - 1]["g2"] = _mod_gate_bwd(
                h_in, row(g_pre_mix[l]), modsel[l], 0, 1, du, dh, saved[l - 1][-1], row(g_post_mlp[l - 1]),
                modsel[l - 1], 5, nt, f"mix_mod_mlp_gate_bwd{l}")
        small_g["q_norm"][l] = jnp.sum(dqn[0].reshape(8, HEAD_DIM), axis=0)
        small_g["k_norm"][l] = jnp.sum(dkn[0].reshape(2, HEAD_DIM), axis=0)
        small_g["sink"][l] = jnp.sum(dsk[:, 2:, :GROUP, 0], axis=0).reshape(-1)

    dmod_rows = []
    for l in range(nl):
        m1, g1, m2, g2 = (dvecs[l][n] for n in ("m1", "g1", "m2", "g2"))
        small_g["g_pre_mix"][l] = jnp.sum(m1[:, :, 2], axis=(0, 1))
        small_g["g_post_mix"][l] = jnp.sum(g1[:, :, 1], axis=(0, 1))
        small_g["g_pre_mlp"][l] = jnp.sum(m2[:, :, 2], axis=(0, 1))
        small_g["g_post_mlp"][l] = jnp.sum(g2[:, :, 1], axis=(0, 1))
        dms = jnp.stack([m1[:, :, 0], m1[:, :, 1], g1[:, :, 0], m2[:, :, 0], m2[:, :, 1], g2[:, :, 0]], axis=2)
        rows = jnp.concatenate([dms[:, 1], jnp.sum(dms[:, 0], axis=0)[None]], axis=0)
        dmod_rows.append(jnp.pad(rows.reshape(bl + 1, N_MOD * d), ((0, ADA_ROWS - bl - 1), (0, 0))))
    grad_x = dh.reshape(bl, s_rows, d)

    dmod_slots = jnp.stack(dmod_rows).reshape(nl, ADA_ROWS, N_DEV, ada_w).transpose(2, 0, 1, 3)
    dmod_g, = _comm_only(_Comm([(dmod_slots, TO_OWNER)]), "exchange_dmod")
    dmod_mine = dmod_g.transpose(1, 0, 2, 3).reshape(nl, N_DEV * ADA_ROWS, ada_w)
    dw_ada, dcc = _ada_bwd(c_all, c_ctx[None, :], dmod_mine, w_ada, "ada_bwd")

    parts = {n: jnp.stack(small_g[n]) for n in small_g}
    parts["c_ctx"] = dcc[0]
    parts["loss"] = ss[0, 0:1]
    no_loss = jnp.zeros((1,), F32)
    parts["b_ada"] = jnp.stack([jnp.sum(r_[: bl + 1], axis=0) for r_ in dmod_rows])
    weights = dict(c_ctx=c_ctx, b_ada=b_ada, g_pre_mix=g_pre_mix, g_post_mix=g_post_mix, g_pre_mlp=g_pre_mlp,
                   g_post_mlp=g_post_mlp, q_norm=q_norm, k_norm=k_norm, sink=sink, loss=no_loss)
    moms = dict(c_ctx=m_c_ctx, b_ada=m_b_ada, g_pre_mix=m_g_pre_mix, g_post_mix=m_g_post_mix, g_pre_mlp=m_g_pre_mlp,
                g_post_mlp=m_g_post_mlp, q_norm=m_q_norm, k_norm=m_k_norm, sink=m_sink, loss=no_loss)
    vels = dict(c_ctx=v_c_ctx, b_ada=v_b_ada, g_pre_mix=v_g_pre_mix, g_post_mix=v_g_post_mix, g_pre_mlp=v_g_pre_mlp,
                g_post_mlp=v_g_post_mlp, q_norm=v_q_norm, k_norm=v_k_norm, sink=v_sink, loss=no_loss)
    small_all, = _comm_only(_Comm([(_pack_small(parts), GATHER)]), "gather_small")
    s_out = _small_adamw(_pack_small(weights), _pack_small(moms), _pack_small(vels), small_all, "adamw_small")
    s_g, s_d, s_m, s_v = [_unpack_small(a, weights) for a in s_out]
    loss = 0.5 * s_g["loss"][0] / d

    res = {}
    for n, w_, m_, v_ in (("w_in", w_in, m_w_in, v_w_in), ("w_out", w_out, m_w_out, v_w_out),
                          ("w_up", w_up, m_w_up, v_w_up), ("w_down", w_down, m_w_down, v_w_down)):
        res[n] = _adamw(w_, m_, v_, None, recvd[n], f"adamw_{n}")
    res["w_ada"] = _adamw(w_ada, m_w_ada, v_w_ada, dw_ada, None, "adamw_w_ada")

    order = ("c_ctx", "w_ada", "b_ada", "g_pre_mix", "g_post_mix", "g_pre_mlp", "g_post_mlp", "w_in", "q_norm",
             "k_norm", "sink", "w_out", "w_up", "w_down")
    outs = [loss, grad_x]
    for i, small in enumerate((s_g, s_d, s_m, s_v)):
        outs += [small[n] if n in small else res[n][i] for n in order]
    return tuple(outs)
```

```python
import functools

import jax
import jax.numpy as jnp
import numpy as np
from jax import lax
from jax.experimental import pallas as pl
from jax.experimental.pallas import tpu as pltpu

F32 = jnp.float32
BF16 = jnp.bfloat16

HEAD_DIM = 64
GROUP = 4
N_HG = 4
Q_WIDTH = GROUP * HEAD_DIM
WINDOW = 128
GRID_W = 64
ROPE_THETA = 10000.0
EPS = 1e-6
NEG_BIG = -1e30
N_MOD = 6
MOD_ROWS = 8
TM = 256
N_DEV = 8
ADA_ROWS = 8
VMEM_LIMIT = 56 * 1024 * 1024

ADAM_LR = 0.001
ADAM_B1 = 0.9
ADAM_B2 = 0.999
ADAM_EPS = 1e-08
ADAM_WD = 0.01
ADAM_STEP = 10


def _params(sem=None):
    kw = dict(vmem_limit_bytes=VMEM_LIMIT)
    if sem is not None:
        kw["dimension_semantics"] = sem
    return pltpu.CompilerParams(**kw)


def _my_index():
    return 4 * lax.axis_index("x") + 2 * lax.axis_index("y") + lax.axis_index("c")


def _peer(k):
    x, y, c = lax.axis_index("x"), lax.axis_index("y"), lax.axis_index("c")
    kx, ky, kc = (k >> 2) & 1, (k >> 1) & 1, k & 1
    px = (1 - x) if kx else x
    py = (1 - y) if ky else y
    pc = (1 - c) if kc else c
    return (px, py, pc), 4 * px + 2 * py + pc


GATHER, GATHER_VIA_SIBLING, TO_OWNER, TO_OWNER_XOR = "gather", "gather_via_sibling", "to_owner", "to_owner_xor"


class _Comm:
    def __init__(self, items):
        self.items = list(items)
        self.arrays = [a for a, _ in self.items]

    def out_shapes(self):
        return [jax.ShapeDtypeStruct(((N_DEV,) + a.shape) if kind in (GATHER, GATHER_VIA_SIBLING) else a.shape, a.dtype)
                for a, kind in self.items]

    def sem_shapes(self):
        n = len(self.items) * N_DEV
        return [pltpu.SemaphoreType.DMA((n,)), pltpu.SemaphoreType.DMA((n,))]

    def _two_level(self, x_ref, o_ref, send_sems, recv_sems, base):
        x, y, c = lax.axis_index("x"), lax.axis_index("y"), lax.axis_index("c")
        me, sibling = (x, y, c), (x, y, 1 - c)
        chips = [(1 - x, y), (x, 1 - y), (1 - x, 1 - y)]
        block = lambda p: o_ref.at[4 * p[0] + 2 * p[1] + p[2]]

        def copy(k, owner, to, src=None):
            return pltpu.make_async_remote_copy(
                src_ref=block(owner) if src is None else src, dst_ref=block(owner), send_sem=send_sems.at[base + k],
                recv_sem=recv_sems.at[base + k], device_id=to, device_id_type=pl.DeviceIdType.MESH)

        first = [copy(1, me, sibling, src=x_ref)] + [copy(2 + j, me, (*chip, c), src=x_ref) for j, chip in enumerate(chips)]
        relay = [(copy(2 + j, (*chip, c), me), copy(5 + j, (*chip, c), sibling)) for j, chip in enumerate(chips)]
        last = [copy(1, sibling, me)] + [copy(5 + j, (*chip, 1 - c), me) for j, chip in enumerate(chips)]
        return first, relay, last

    def _copies(self, in_refs, out_refs, send_sems, recv_sems):
        me = _my_index()
        local, remote = [], []
        for i, ((_, kind), x_ref, o_ref) in enumerate(zip(self.items, in_refs, out_refs)):
            base = i * N_DEV
            if kind == GATHER_VIA_SIBLING:
                local.append(pltpu.make_async_copy(x_ref, o_ref.at[me], send_sems.at[base]))
                continue
            own_src = x_ref if kind == GATHER else x_ref.at[me]
            own_dst = o_ref.at[0] if kind == TO_OWNER_XOR else o_ref.at[me]
            local.append(pltpu.make_async_copy(own_src, own_dst, send_sems.at[base]))
            for k in range(1, N_DEV):
                peer, pidx = _peer(k)
                remote.append(pltpu.make_async_remote_copy(
                    src_ref=x_ref if kind == GATHER else x_ref.at[pidx],
                    dst_ref=o_ref.at[k] if kind == TO_OWNER_XOR else o_ref.at[me],
                    send_sem=send_sems.at[base + k], recv_sem=recv_sems.at[base + k],
                    device_id=peer, device_id_type=pl.DeviceIdType.MESH))
        return local, remote

    def start(self, in_refs, out_refs, send_sems, recv_sems):
        local, remote = self._copies(in_refs, out_refs, send_sems, recv_sems)
        for cp in local + remote:
            cp.start()
        for i, ((_, kind), x_ref, o_ref) in enumerate(zip(self.items, in_refs, out_refs)):
            if kind == GATHER_VIA_SIBLING:
                for cp in self._two_level(x_ref, o_ref, send_sems, recv_sems, i * N_DEV)[0]:
                    cp.start()

    def relay(self, in_refs, out_refs, send_sems, recv_sems):
        for i, ((_, kind), x_ref, o_ref) in enumerate(zip(self.items, in_refs, out_refs)):
            if kind == GATHER_VIA_SIBLING:
                for arrival, onward in self._two_level(x_ref, o_ref, send_sems, recv_sems, i * N_DEV)[1]:
                    arrival.wait_recv()
                    onward.start()

    def wait(self, in_refs, out_refs, send_sems, recv_sems, relayed=False):
        local, remote = self._copies(in_refs, out_refs, send_sems, recv_sems)
        for cp in remote:
            cp.wait_recv()
        for cp in remote:
            cp.wait_send()
        if not relayed:
            self.relay(in_refs, out_refs, send_sems, recv_sems)
        for i, ((_, kind), x_ref, o_ref) in enumerate(zip(self.items, in_refs, out_refs)):
            if kind == GATHER_VIA_SIBLING:
                first, relay, last = self._two_level(x_ref, o_ref, send_sems, recv_sems, i * N_DEV)
                for cp in last:
                    cp.wait_recv()
                for cp in first + [onward for _, onward in relay]:
                    cp.wait_send()
        for cp in local:
            cp.wait()


def _call(body, *, name, grid, in_specs, out_specs, out_shape, args, scratch_shapes=(), prefetch=0, sem=None,
          comm=None):
    single = not isinstance(out_shape, (list, tuple))
    out_shape = [out_shape] if single else list(out_shape)
    out_specs = [out_specs] if single else list(out_specs)
    in_specs, scratch_shapes, args = list(in_specs), list(scratch_shapes), list(args)
    n_in, n_out = len(in_specs), len(out_shape)
    if comm is not None:
        nc = len(comm.arrays)
        hbm = pl.BlockSpec(memory_space=pl.ANY)
        inner = body

        def body(*refs):
            pre, r = refs[:prefetch], refs[prefetch:]
            ins, cin = r[:n_in], r[n_in:n_in + nc]
            outs, cout = r[n_in + nc:n_in + nc + n_out], r[n_in + nc + n_out:n_in + 2 * nc + n_out]
            scr, sems = r[n_in + 2 * nc + n_out:len(r) - 2], r[len(r) - 2:]
            ids = [pl.program_id(i) for i in range(len(grid))]

            def when(flags, fn):
                if flags:
                    pl.when(functools.reduce(jnp.logical_and, flags))(fn)
                else:
                    fn()

            when([i == 0 for i in ids], lambda: comm.start(cin, cout, *sems))
            inner(*pre, *ins, *outs, *scr)
            halfway = len(grid) > 1 and grid[0] > 1
            if halfway:
                when([i == (grid[0] // 2 if a == 0 else 0) for a, i in enumerate(ids)],
                     lambda: comm.relay(cin, cout, *sems))
            when([i == n - 1 for i, n in zip(ids, grid)], lambda: comm.wait(cin, cout, *sems, relayed=halfway))

        in_specs += [hbm] * nc
        out_specs += [hbm] * nc
        out_shape += comm.out_shapes()
        scratch_shapes += comm.sem_shapes()
        args += comm.arrays
        sem = ("arbitrary",) * len(grid)
    kw = dict(name=name, out_shape=out_shape, compiler_params=_params(sem if grid else None))
    if prefetch:
        kw["grid_spec"] = pltpu.PrefetchScalarGridSpec(
            num_scalar_prefetch=prefetch, grid=grid, in_specs=in_specs, out_specs=out_specs,
            scratch_shapes=scratch_shapes)
    else:
        kw.update(in_specs=in_specs, out_specs=out_specs, scratch_shapes=scratch_shapes)
        if grid:
            kw["grid"] = grid
    res = list(pl.pallas_call(body, **kw)(*args))
    outs = res[:n_out]
    return (outs[0] if single else outs), res[n_out:]


def _comm_only(comm, name):
    return _call(lambda: None, name=name, grid=(), in_specs=[], out_specs=[], out_shape=[], args=[], comm=comm)[1]


def _mm(a, b, *, name, ta=False, b_mode="nn", out_mode="plain", out_dtype=F32, tm=512, tn=512, tk=512,
        epilogue=None, extra=None, comm=None):
    if ta:
        kdim, m = a.shape
    else:
        m, kdim = a.shape
    if b_mode == "nn":
        n = b.shape[1]
    elif b_mode == "nt":
        n = b.shape[0]
    elif b_mode == "nn_slots":
        n = b.shape[0] * b.shape[2]
        tn = b.shape[2]
    else:
        n = b.shape[1]
        tk = b.shape[2]
    if out_mode == "slots":
        tn = n // N_DEV
    tm, tn, tk = min(tm, m), min(tn, n), min(tk, kdim)
    assert m % tm == 0 and n % tn == 0 and kdim % tk == 0, (name, m, n, kdim, tm, tn, tk)
    nk = kdim // tk

    a_spec = pl.BlockSpec((tk, tm), lambda i, j, k: (k, i)) if ta else pl.BlockSpec((tm, tk), lambda i, j, k: (i, k))
    if b_mode == "nn":
        b_spec = pl.BlockSpec((tk, tn), lambda i, j, k: (k, j))
    elif b_mode == "nt":
        b_spec = pl.BlockSpec((tn, tk), lambda i, j, k: (j, k))
    elif b_mode == "nn_slots":
        b_spec = pl.BlockSpec((None, tk, tn), lambda i, j, k: (j, k, 0))
    else:
        b_spec = pl.BlockSpec((None, tn, tk), lambda i, j, k: (k, j, 0))
    tb = b_mode in ("nt", "nt_slots")
    if out_mode == "plain":
        o_shape, o_spec = (m, n), pl.BlockSpec((tm, tn), lambda i, j, k: (i, j))
    else:
        o_shape, o_spec = (N_DEV, m, tn), pl.BlockSpec((None, tm, tn), lambda i, j, k: (j, i, 0))
    dims = (((0 if ta else 1,), (1 if tb else 0,)), ((), ()))

    in_specs = [a_spec, b_spec]
    args = [a, b]
    if epilogue == "relu2_bwd":
        in_specs.append(pl.BlockSpec((tm, tn), lambda i, j, k: (i, j)))
        args.append(extra)
    if epilogue == "relu2":
        out_shape = [jax.ShapeDtypeStruct(o_shape, BF16), jax.ShapeDtypeStruct((n, m), BF16)]
        out_specs = [o_spec, pl.BlockSpec((tn, tm), lambda i, j, k: (j, i))]
    else:
        out_shape = jax.ShapeDtypeStruct(o_shape, out_dtype)
        out_specs = o_spec

    def finish(refs, acc):
        if epilogue == "relu2":
            r = jnp.maximum(acc, 0.0)
            r2 = (r * r).astype(BF16)
            refs[2][...] = r2
            refs[3][...] = r2.T
        elif epilogue == "relu2_bwd":
            refs[3][...] = (acc * (2.0 * jnp.sqrt(refs[2][...]).astype(F32))).astype(out_dtype)
        else:
            refs[2][...] = acc.astype(out_dtype)

    def body(*refs):
        part = lax.dot_general(refs[0][...], refs[1][...], dims, preferred_element_type=F32)
        if nk == 1:
            finish(refs, part)
            return
        acc_ref = refs[-1]
        k = pl.program_id(2)

        @pl.when(k == 0)
        def _():
            acc_ref[...] = part

        @pl.when(jnp.logical_and(k > 0, k < nk - 1))
        def _():
            acc_ref[...] += part

        @pl.when(k == nk - 1)
        def _():
            finish(refs, acc_ref[...] + part)

    outs, couts = _call(
        body, name=name, grid=(m // tm, n // tn, nk), in_specs=in_specs, out_specs=out_specs, out_shape=out_shape,
        args=args, scratch_shapes=[] if nk == 1 else [pltpu.VMEM((tm, tn), F32)],
        sem=("parallel", "parallel", "arbitrary"), comm=comm)
    return outs if comm is None else (outs, couts)


def _row_specs(d, nt):
    row = pl.BlockSpec((TM, d), lambda t: (t, 0))
    vec = pl.BlockSpec((1, d), lambda t: (0, 0))
    mod = pl.BlockSpec((None, None, MOD_ROWS, d), lambda t: (t // nt, jnp.minimum(t % nt, 1), 0, 0))
    return row, vec, mod


def _rms(x):
    r = lax.rsqrt(jnp.mean(x * x, axis=1, keepdims=True) + EPS)
    return r, x * r


def _token_stream(ctx, x, nt, name, comm=None):
    bl, c_rows, d = ctx.shape
    m = bl * nt * TM

    def body(ctx_ref, x_ref, h_ref):
        t = pl.program_id(0)

        @pl.when(t % nt == 0)
        def _():
            h_ref[...] = ctx_ref[...]

        @pl.when(t % nt != 0)
        def _():
            h_ref[...] = x_ref[...]

    outs, couts = _call(
        body, name=name, grid=(m // TM,),
        in_specs=[pl.BlockSpec((None, TM, d), lambda t: (t // nt, 0, 0)),
                  pl.BlockSpec((None, TM, d), lambda t: (t // nt, jnp.maximum(t % nt, 1) - 1, 0))],
        out_specs=pl.BlockSpec((TM, d), lambda t: (t, 0)), out_shape=jax.ShapeDtypeStruct((m, d), F32),
        args=[ctx, x], sem=("parallel",), comm=comm)
    return outs if comm is None else (outs, couts)


def _norm_mod_fwd(h, g, modsel, i_sh, i_sc, nt, name, comm=None):
    m, d = h.shape
    row, vec, mod = _row_specs(d, nt)

    def body(h_ref, g_ref, ms_ref, u_ref, ut_ref):
        _, xh = _rms(h_ref[...])
        ms = ms_ref[...]
        u = (xh * g_ref[...] * (1.0 + ms[i_sc:i_sc + 1]) + ms[i_sh:i_sh + 1]).astype(BF16)
        u_ref[...] = u
        ut_ref[...] = u.T

    outs, couts = _call(
        body, name=name, grid=(m // TM,), in_specs=[row, vec, mod],
        out_specs=[row, pl.BlockSpec((d, TM), lambda t: (0, t))],
        out_shape=[jax.ShapeDtypeStruct((m, d), BF16), jax.ShapeDtypeStruct((d, m), BF16)],
        args=[h, g, modsel], sem=("parallel",), comm=comm)
    return outs if comm is None else (outs, couts)


def _acc_rows(t, nt, dvec_ref, rows):
    first = (t % nt) <= 1

    @pl.when(first)
    def _():
        dvec_ref[...] = rows

    @pl.when(jnp.logical_not(first))
    def _():
        dvec_ref[...] += rows


def _norm_mod_bwd(h, g, modsel, du, dh_in, i_sh, i_sc, nt, name, comm=None, latent_only=False):
    m, d = h.shape
    row, vec, mod = _row_specs(d, nt)
    dh_rows, dh_spec = m, row
    if latent_only:
        dh_rows = m // nt * (nt - 1)
        dh_spec = pl.BlockSpec((TM, d), lambda t: ((t // nt) * (nt - 1) + jnp.maximum(t % nt, 1) - 1, 0))

    def body(h_ref, g_ref, ms_ref, du_ref, dhi_ref, dh_ref, dvec_ref):
        t = pl.program_id(0)
        r, xh = _rms(h_ref[...])
        g_ = g_ref[...]
        ms = ms_ref[...]
        du_ = du_ref[...]
        y = xh * g_
        dy = du_ * (1.0 + ms[i_sc:i_sc + 1])
        dxh = dy * g_
        dx = r * (dxh - xh * jnp.mean(dxh * xh, axis=1, keepdims=True))
        dh_ref[...] = dhi_ref[...] + dx
        rows = jnp.concatenate([
            jnp.sum(du_, axis=0, keepdims=True), jnp.sum(du_ * y, axis=0, keepdims=True),
            jnp.sum(dy * xh, axis=0, keepdims=True), jnp.zeros((MOD_ROWS - 3, d), F32)], axis=0)
        _acc_rows(t, nt, dvec_ref, rows)

    outs, couts = _call(
        body, name=name, grid=(m // TM,), in_specs=[row, vec, mod, row, row], out_specs=[dh_spec, mod],
        out_shape=[jax.ShapeDtypeStruct((dh_rows, d), F32), jax.ShapeDtypeStruct(modsel.shape, F32)],
        args=[h, g, modsel, du, dh_in], sem=("arbitrary",), comm=comm)
    return outs if comm is None else (outs, couts)


def _gate_rows(dh_, xh, g_, gate):
    dy = dh_ * gate
    return dy * g_, jnp.sum(dh_ * (xh * g_), axis=0, keepdims=True), jnp.sum(dy * xh, axis=0, keepdims=True)


def _gate_mod_fwd(h, z, g_post, ms_gate, i_g, g_pre, ms_mod, i_sh, i_sc, nt, name):
    m, d = h.shape
    row, vec, mod = _row_specs(d, nt)

    def body(h_ref, z_ref, gp_ref, msg_ref, gq_ref, msm_ref, hn_ref, u_ref, ut_ref):
        _, zh = _rms(z_ref[...])
        hn = h_ref[...] + msg_ref[...][i_g:i_g + 1] * (zh * gp_ref[...])
        hn_ref[...] = hn
        _, xh = _rms(hn)
        ms = msm_ref[...]
        u = (xh * gq_ref[...] * (1.0 + ms[i_sc:i_sc + 1]) + ms[i_sh:i_sh + 1]).astype(BF16)
        u_ref[...] = u
        ut_ref[...] = u.T

    return pl.pallas_call(
        body, name=name, grid=(m // TM,), in_specs=[row, row, vec, mod, vec, mod],
        out_specs=[row, row, pl.BlockSpec((d, TM), lambda t: (0, t))],
        out_shape=[jax.ShapeDtypeStruct((m, d), F32), jax.ShapeDtypeStruct((m, d), BF16),
                   jax.ShapeDtypeStruct((d, m), BF16)],
        compiler_params=_params(("parallel",)),
    )(h, z, g_post, ms_gate, g_pre, ms_mod)


def _mod_gate_bwd(h, g_pre, ms_mod, i_sh, i_sc, du, dh_in, z, g_post, ms_gate, i_g, nt, name):
    m, d = h.shape
    row, vec, mod = _row_specs(d, nt)

    def body(h_ref, gq_ref, msm_ref, du_ref, dhi_ref, z_ref, gp_ref, msg_ref, dh_ref, dvm_ref, dz_ref, dvg_ref):
        t = pl.program_id(0)
        r, xh = _rms(h_ref[...])
        gq = gq_ref[...]
        ms = msm_ref[...]
        du_ = du_ref[...]
        dy = du_ * (1.0 + ms[i_sc:i_sc + 1])
        dxh = dy * gq
        dh_ = dhi_ref[...] + r * (dxh - xh * jnp.mean(dxh * xh, axis=1, keepdims=True))
        dh_ref[...] = dh_
        _acc_rows(t, nt, dvm_ref, jnp.concatenate([
            jnp.sum(du_, axis=0, keepdims=True), jnp.sum(du_ * (xh * gq), axis=0, keepdims=True),
            jnp.sum(dy * xh, axis=0, keepdims=True), jnp.zeros((MOD_ROWS - 3, d), F32)], axis=0))
        rz, zh = _rms(z_ref[...])
        dzh, d_gate, d_gp = _gate_rows(dh_, zh, gp_ref[...], msg_ref[...][i_g:i_g + 1])
        dz_ref[...] = (rz * (dzh - zh * jnp.mean(dzh * zh, axis=1, keepdims=True))).astype(BF16)
        _acc_rows(t, nt, dvg_ref, jnp.concatenate([d_gate, d_gp, jnp.zeros((MOD_ROWS - 2, d), F32)], axis=0))

    return pl.pallas_call(
        body, name=name, grid=(m // TM,), in_specs=[row, vec, mod, row, row, row, vec, mod],
        out_specs=[row, mod, row, mod],
        out_shape=[jax.ShapeDtypeStruct((m, d), F32), jax.ShapeDtypeStruct(ms_mod.shape, F32),
                   jax.ShapeDtypeStruct((m, d), BF16), jax.ShapeDtypeStruct(ms_gate.shape, F32)],
        compiler_params=_params(("arbitrary",)),
    )(h, g_pre, ms_mod, du, dh_in, z, g_post, ms_gate)


def _gate_loss_bwd(h, z, g_post, modsel, i_g, target, nt, name):
    m, d = h.shape
    row, vec, mod = _row_specs(d, nt)
    ntl = nt - 1
    tgt = pl.BlockSpec((TM, d), lambda t: ((t // nt) * ntl + jnp.maximum(t % nt, 1) - 1, 0))
    acc = pl.BlockSpec((8, 128), lambda t: (0, 0))

    def body(h_ref, z_ref, gp_ref, ms_ref, t_ref, dh_ref, dz_ref, dvg_ref, ss_ref):
        t = pl.program_id(0)

        @pl.when(t == 0)
        def _():
            ss_ref[...] = jnp.zeros_like(ss_ref)

        latent = (t % nt != 0).astype(F32)
        rz, zh = _rms(z_ref[...])
        gp = gp_ref[...]
        gate = ms_ref[...][i_g:i_g + 1]
        e = h_ref[...] + gate * (zh * gp) - t_ref[...]
        ss_ref[...] += latent * jnp.sum(e * e)
        dh_ = e * (latent / d)
        dh_ref[...] = dh_
        dzh, d_gate, d_gp = _gate_rows(dh_, zh, gp, gate)
        dz_ref[...] = (rz * (dzh - zh * jnp.mean(dzh * zh, axis=1, keepdims=True))).astype(BF16)
        _acc_rows(t, nt, dvg_ref, jnp.concatenate([d_gate, d_gp, jnp.zeros((MOD_ROWS - 2, d), F32)], axis=0))

    return pl.pallas_call(
        body, name=name, grid=(m // TM,), in_specs=[row, row, vec, mod, tgt], out_specs=[row, row, mod, acc],
        out_shape=[jax.ShapeDtypeStruct((m, d), F32), jax.ShapeDtypeStruct((m, d), BF16),
                   jax.ShapeDtypeStruct(modsel.shape, F32), jax.ShapeDtypeStruct((8, 128), F32)],
        compiler_params=_params(("arbitrary",)),
    )(h, z, g_post, modsel, target)


QA, KA, VA, QB, KB, VB = 0, 512, 640, 768, 1280, 1408
PROJ_W = 1536
Q_SCALE = HEAD_DIM ** -0.5
LOG2E = 1.4426950408889634


def _swap16(x):
    lane = lax.broadcasted_iota(jnp.int32, x.shape, 1)
    n = x.shape[1]
    return jnp.where((lane % 32) < 16, pltpu.roll(x, n - 16, 1), pltpu.roll(x, 16, 1))


def _seg_mean(x, e):
    hi = x.astype(BF16)
    lo = (x - hi.astype(F32)).astype(BF16)
    w = e.shape[0]
    both = lambda a: jnp.dot(hi[:, a:a + w], e, preferred_element_type=F32) + jnp.dot(lo[:, a:a + w], e, preferred_element_type=F32)
    parts = [both(a) for a in range(0, x.shape[1], w)]
    return parts[0] if len(parts) == 1 else jnp.concatenate(parts, axis=1)


def _rope_tables(t_rows, c_rows):
    s = t_rows - c_rows
    row_ids = jnp.repeat(jnp.arange(s // GRID_W, dtype=jnp.int32), GRID_W).astype(F32)
    col_ids = jnp.tile(jnp.arange(GRID_W, dtype=jnp.int32), s // GRID_W).astype(F32)
    axis_dim = HEAD_DIM // 2
    inv = ROPE_THETA ** (-jnp.arange(0, axis_dim, 2, dtype=F32) / axis_dim)
    ang_r = row_ids[:, None] * inv[None, :]
    ang_c = col_ids[:, None] * inv[None, :]
    cos = jnp.concatenate([jnp.cos(ang_r), jnp.cos(ang_r), jnp.cos(ang_c), jnp.cos(ang_c)], axis=1)
    sin = jnp.concatenate([-jnp.sin(ang_r), jnp.sin(ang_r), -jnp.sin(ang_c), jnp.sin(ang_c)], axis=1)
    cos = jnp.concatenate([jnp.ones((c_rows, HEAD_DIM), F32), cos], axis=0)
    sin = jnp.concatenate([jnp.zeros((c_rows, HEAD_DIM), F32), sin], axis=0)
    return jnp.tile(cos, (1, 8)), jnp.tile(sin, (1, 8))


def _head_mean_matrix():
    i = np.arange(512)
    return jnp.asarray((i[:, None] // HEAD_DIM == i[None, :] // HEAD_DIM).astype(np.float32) / HEAD_DIM, dtype=BF16)


def _interleave_kv(k, v):
    return jnp.concatenate([k[:, :64], v[:, :64], k[:, 64:], v[:, 64:]], axis=1)


def _prep_fwd(proj, qn, kn, cos, sin, emat, nt, name):
    m = proj.shape[0]
    specs = [
        pl.BlockSpec((TM, PROJ_W), lambda t: (t, 0)),
        pl.BlockSpec((1, 512), lambda t: (0, 0)), pl.BlockSpec((1, 128), lambda t: (0, 0)),
        pl.BlockSpec((TM, 512), lambda t: (t % nt, 0)), pl.BlockSpec((TM, 512), lambda t: (t % nt, 0)),
        pl.BlockSpec((512, 512), lambda t: (0, 0)),
    ]

    def body(p_ref, qn_ref, kn_ref, cos_ref, sin_ref, e_ref, q_ref, kv_ref):
        cos_, sin_, e = cos_ref[...], sin_ref[...], e_ref[...]

        def rope(x, w):
            return x * cos_[:, :w] + _swap16(x) * sin_[:, :w]

        def norm(x, g, w):
            return x * lax.rsqrt(_seg_mean(x * x, e[:min(w, 256), :min(w, 256)]) + EPS) * g

        qa = rope(norm(p_ref[:, QA:QA + 512], qn_ref[...], 512), 512)
        qb = rope(p_ref[:, QB:QB + 512], 512)
        q_ref[:, 0:512] = (qa * (Q_SCALE * LOG2E)).astype(BF16)
        q_ref[:, 512:1024] = (qb * (Q_SCALE * LOG2E)).astype(BF16)
        ka = rope(norm(p_ref[:, KA:KA + 128], kn_ref[...], 128), 128)
        kb = rope(p_ref[:, KB:KB + 128], 128)
        kv_ref[:, 0:256] = _interleave_kv(ka, p_ref[:, VA:VA + 128]).astype(BF16)
        kv_ref[:, 256:512] = _interleave_kv(kb, p_ref[:, VB:VB + 128]).astype(BF16)

    return pl.pallas_call(
        body, name=name, grid=(m // TM,), in_specs=specs,
        out_specs=[pl.BlockSpec((TM, 1024), lambda t: (t, 0)), pl.BlockSpec((TM, 512), lambda t: (t, 0))],
        out_shape=[jax.ShapeDtypeStruct((m, 1024), BF16), jax.ShapeDtypeStruct((m, 512), BF16)],
        compiler_params=_params(("parallel",)),
    )(proj, qn, kn, cos, sin, emat)


def _prep_bwd(proj, dq, dkv, qn, kn, cos, sin, emat, nt, name, comm=None):
    m = proj.shape[0]
    specs = [
        pl.BlockSpec((TM, PROJ_W), lambda t: (t, 0)),
        pl.BlockSpec((TM, 1024), lambda t: (t, 0)), pl.BlockSpec((TM, 512), lambda t: (t, 0)),
        pl.BlockSpec((1, 512), lambda t: (0, 0)), pl.BlockSpec((1, 128), lambda t: (0, 0)),
        pl.BlockSpec((TM, 512), lambda t: (t % nt, 0)), pl.BlockSpec((TM, 512), lambda t: (t % nt, 0)),
        pl.BlockSpec((512, 512), lambda t: (0, 0)),
    ]

    def body(p_ref, dq_ref, dkv_ref, qn_ref, kn_ref, cos_ref, sin_ref, e_ref, dp_ref, dqn_ref, dkn_ref):
        t = pl.program_id(0)
        cos_, sin_, e = cos_ref[...], sin_ref[...], e_ref[...]

        @pl.when(t == 0)
        def _():
            dqn_ref[...] = jnp.zeros_like(dqn_ref)
            dkn_ref[...] = jnp.zeros_like(dkn_ref)

        def unrope(dy, w):
            return dy * cos_[:, :w] + _swap16(dy * sin_[:, :w])

        def norm_bwd(x, g, dy, w):
            r = lax.rsqrt(_seg_mean(x * x, e[:min(w, 256), :min(w, 256)]) + EPS)
            xh = x * r
            dxh = dy * g
            dx = r * (dxh - xh * _seg_mean(dxh * xh, e[:min(w, 256), :min(w, 256)]))
            return dx, jnp.sum(dy * xh, axis=0, keepdims=True)

        dqa, dgq = norm_bwd(p_ref[:, QA:QA + 512], qn_ref[...], unrope(dq_ref[:, 0:512] * Q_SCALE, 512), 512)
        dp_ref[:, QA:QA + 512] = dqa.astype(BF16)
        dp_ref[:, QB:QB + 512] = unrope(dq_ref[:, 512:1024] * Q_SCALE, 512).astype(BF16)
        da = dkv_ref[:, 0:256]
        db = dkv_ref[:, 256:512]
        dka = jnp.concatenate([da[:, 0:64], da[:, 128:192]], axis=1)
        dva = jnp.concatenate([da[:, 64:128], da[:, 192:256]], axis=1)
        dkb = jnp.concatenate([db[:, 0:64], db[:, 128:192]], axis=1)
        dvb = jnp.concatenate([db[:, 64:128], db[:, 192:256]], axis=1)
        dka, dgk = norm_bwd(p_ref[:, KA:KA + 128], kn_ref[...], unrope(dka, 128), 128)
        dp_ref[:, KA:KA + 128] = dka.astype(BF16)
        dp_ref[:, VA:VA + 128] = dva.astype(BF16)
        dp_ref[:, KB:KB + 128] = unrope(dkb, 128).astype(BF16)
        dp_ref[:, VB:VB + 128] = dvb.astype(BF16)
        dqn_ref[0:1, :] += dgq
        dkn_ref[0:1, :] += dgk

    outs, couts = _call(
        body, name=name, grid=(m // TM,), in_specs=specs,
        out_specs=[pl.BlockSpec((TM, PROJ_W), lambda t: (t, 0)), pl.BlockSpec((8, 512), lambda t: (0, 0)),
                   pl.BlockSpec((8, 128), lambda t: (0, 0))],
        out_shape=[jax.ShapeDtypeStruct((m, PROJ_W), BF16), jax.ShapeDtypeStruct((8, 512), F32),
                   jax.ShapeDtypeStruct((8, 128), F32)],
        args=[proj, dq, dkv, qn, kn, cos, sin, emat], sem=("arbitrary",), comm=comm)
    return outs if comm is None else (outs, couts)


PAIR = 2
PAIR_HEADS = [(j, g) for j in range(PAIR) for g in range(GROUP)]
LOOKAHEAD = 4
LOOKAHEAD_BWD = 2


def _attn_case(t, pr, kv_ref, c_rows, t_rows, fn, keys_first=False):
    wl = TM + 2 * WINDOW
    kvd = lambda a, n: kv_ref[pl.ds(a, n), :]
    dense = pr == 0
    ctx = t == 0

    @pl.when(jnp.logical_and(dense, ctx))
    def _():
        fn(kvd(0, c_rows), None, False, [(0, c_rows)])

    @pl.when(jnp.logical_and(dense, jnp.logical_not(ctx)))
    def _():
        fn(kvd(0, t_rows), None, False, [(0, t_rows)])

    @pl.when(jnp.logical_and(jnp.logical_not(dense), ctx))
    def _():
        fn(kvd(0, c_rows), None, True, [(0, c_rows)])

    @pl.when(jnp.logical_and(jnp.logical_not(dense), jnp.logical_not(ctx)))
    def _():
        start = pl.multiple_of(jnp.minimum(c_rows + (t - 1) * TM - WINDOW, t_rows - wl), 128)
        kv = jnp.concatenate([kvd(0, c_rows), kvd(start, wl)], axis=0)
        shape = (c_rows + wl, TM) if keys_first else (TM, c_rows + wl)
        q_i = lax.broadcasted_iota(jnp.int32, shape, 1 if keys_first else 0)
        k_i = lax.broadcasted_iota(jnp.int32, shape, 0 if keys_first else 1)
        qpos = (t - 1) * TM + q_i
        kpos = start - 2 * c_rows + k_i
        mask = jnp.logical_or(k_i < c_rows, jnp.logical_and(jnp.abs(kpos - qpos) <= WINDOW, kpos >= 0))
        fn(kv, mask, True, [(0, c_rows), (start, wl)])


def _keys_values(kv, j):
    return kv[:, j * 128:j * 128 + 64], kv[:, j * 128 + 64:(j + 1) * 128]


def _head_cols(ref, j, g):
    a = j * Q_WIDTH + g * HEAD_DIM
    return ref[:, a:a + HEAD_DIM]


def _head_columns(cols):
    lane = lax.broadcasted_iota(jnp.int32, (TM, 128), 1)
    out = jnp.zeros((TM, 128), F32)
    for g, col in enumerate(cols):
        out = jnp.where(lane == g, col, out)
    return out


def _attn_specs(t_rows):
    nt = t_rows // TM
    q_spec = pl.BlockSpec((None, TM, PAIR * Q_WIDTH), lambda b, pr, t, s: (b, t, pr))
    kv_spec = pl.BlockSpec((None, t_rows, PAIR * 128), lambda b, pr, t, s: (b, 0, pr))
    lse_spec = pl.BlockSpec((PAIR, TM, 128), lambda b, pr, t, s: (pr, b * nt + t, 0))
    return q_spec, kv_spec, lse_spec


def _attn_fwd(q_all, kv_all, sink8, c_rows, name, comm=None):
    bl, t_rows, _ = q_all.shape
    q_spec, kv_spec, lse_spec = _attn_specs(t_rows)

    def body(sink_ref, q_ref, kv_ref, o_ref, ot_ref, lse_ref):
        pr, t = pl.program_id(1), pl.program_id(2)

        def fn(kv, mask, use_sink, spans):
            outs, lses = [], []
            ks = [_keys_values(kv, j)[0] for j in range(PAIR)]
            v_ones = [jnp.concatenate([_keys_values(kv, j)[1], jnp.ones((kv.shape[0], 64), BF16)], axis=1)
                      for j in range(PAIR)]

            def scores(i):
                j, g = PAIR_HEADS[i]
                s = lax.dot_general(_head_cols(q_ref, j, g), ks[j], (((1,), (1,)), ((), ())),
                                    preferred_element_type=F32)
                return s if mask is None else jnp.where(mask, s, NEG_BIG)

            ahead = [scores(i) for i in range(LOOKAHEAD)]
            for i, (j, g) in enumerate(PAIR_HEADS):
                s = ahead.pop(0)
                if i + LOOKAHEAD < len(PAIR_HEADS):
                    ahead.append(scores(i + LOOKAHEAD))
                mx = jnp.max(s, axis=1, keepdims=True)
                if use_sink:
                    sink = sink_ref[j * GROUP + g] * LOG2E
                    mx = jnp.maximum(mx, sink)
                pv = jnp.dot(jnp.exp2(s - mx).astype(BF16), v_ones[j], preferred_element_type=F32)
                l = pv[:, 64:65]
                if use_sink:
                    l = l + jnp.exp2(sink - mx)
                outs.append(pv[:, :64] * (1.0 / l))
                lses.append(mx + jnp.log2(l))
            o = jnp.concatenate(outs, axis=1).astype(BF16)
            o_ref[...] = o
            ot_ref[...] = o.T
            for j in range(PAIR):
                lse_ref[j] = _head_columns(lses[j * GROUP:(j + 1) * GROUP])

        _attn_case(t, pr, kv_ref, c_rows, t_rows, fn)

    nt = t_rows // TM
    ot_spec = pl.BlockSpec((PAIR * Q_WIDTH, TM), lambda b, pr, t, s: (pr, b * nt + t))
    outs, couts = _call(
        body, name=name, grid=(bl, N_HG // PAIR, nt), in_specs=[q_spec, kv_spec], out_specs=[q_spec, ot_spec, lse_spec],
        out_shape=[jax.ShapeDtypeStruct(q_all.shape, BF16), jax.ShapeDtypeStruct((N_HG * Q_WIDTH, bl * t_rows), BF16),
                   jax.ShapeDtypeStruct((N_HG, bl * t_rows, 128), F32)],
        args=[sink8, q_all, kv_all], prefetch=1, sem=("parallel", "parallel", "arbitrary"), comm=comm)
    return outs if comm is None else (outs, couts)


def _attn_bwd(q_all, kv_all, do, o, lse, sink8, c_rows, name, comm=None):
    bl, t_rows, _ = q_all.shape
    q_spec, kv_spec, lse_spec = _attn_specs(t_rows)
    ds_spec = pl.BlockSpec((None, PAIR, 8, 128), lambda b, pr, t, s: (b, pr, 0, 0))

    def body(sink_ref, q_ref, kv_ref, do_ref, o_ref, lse_ref, dq_ref, dkv_ref, dsk_ref):
        pr, t = pl.program_id(1), pl.program_id(2)

        @pl.when(t == 0)
        def _():
            dkv_ref[...] = jnp.zeros_like(dkv_ref)
            dsk_ref[...] = jnp.zeros_like(dsk_ref)

        lse_rows = [lse_ref[j].T for j in range(PAIR)]
        dd_rows = [_head_columns([jnp.sum(_head_cols(do_ref, j, g).astype(F32) * _head_cols(o_ref, j, g).astype(F32),
                                          axis=1, keepdims=True) for g in range(GROUP)]).T for j in range(PAIR)]

        def fn(kv, mask, use_sink, spans):
            ks, vs = zip(*[_keys_values(kv, j) for j in range(PAIR)])
            k_ts = [k.T for k in ks]
            dq_t = []
            dsinks = [[] for _ in range(PAIR)]
            dks = [jnp.zeros(ks[0].shape, F32) for _ in range(PAIR)]
            dvs = [jnp.zeros(ks[0].shape, F32) for _ in range(PAIR)]

            def products(i):
                j, g = PAIR_HEADS[i]
                s = lax.dot_general(ks[j], _head_cols(q_ref, j, g), (((1,), (1,)), ((), ())),
                                    preferred_element_type=F32)
                dp = lax.dot_general(vs[j], _head_cols(do_ref, j, g), (((1,), (1,)), ((), ())),
                                     preferred_element_type=F32)
                return (s if mask is None else jnp.where(mask, s, NEG_BIG)), dp

            ahead = [products(i) for i in range(LOOKAHEAD_BWD)]
            for i, (j, g) in enumerate(PAIR_HEADS):
                q, do_g = _head_cols(q_ref, j, g), _head_cols(do_ref, j, g)
                lse_g, dd_g = lse_rows[j][g:g + 1, :], dd_rows[j][g:g + 1, :]
                s, dp = ahead.pop(0)
                if i + LOOKAHEAD_BWD < len(PAIR_HEADS):
                    ahead.append(products(i + LOOKAHEAD_BWD))
                pb = jnp.exp2(s - lse_g).astype(BF16)
                ds = (pb.astype(F32) * (dp - dd_g)).astype(BF16)
                dks[j] = dks[j] + jnp.dot(ds, q, preferred_element_type=F32)
                dvs[j] = dvs[j] + jnp.dot(pb, do_g, preferred_element_type=F32)
                dq_t.append(jnp.dot(k_ts[j], ds, preferred_element_type=F32))
                if use_sink:
                    p_sink = jnp.exp2(sink_ref[j * GROUP + g] * LOG2E - lse_g)
                    dsinks[j].append(jnp.broadcast_to(-jnp.sum(p_sink * dd_g, axis=1, keepdims=True), (1, 128)))
            dq_ref[...] = jnp.concatenate(dq_t, axis=0).T
            dkv = jnp.concatenate([a for j in range(PAIR) for a in (dks[j] * (1.0 / LOG2E), dvs[j])], axis=1)
            off = 0
            for start, size in spans:
                dkv_ref[pl.ds(start, size), :] += dkv[off:off + size]
                off += size
            if use_sink:
                for j in range(PAIR):
                    dsk_ref[j, 0:GROUP, :] += jnp.concatenate(dsinks[j], axis=0)

        _attn_case(t, pr, kv_ref, c_rows, t_rows, fn, keys_first=True)

    outs, couts = _call(
        body, name=name, grid=(bl, N_HG // PAIR, t_rows // TM), in_specs=[q_spec, kv_spec, q_spec, q_spec, lse_spec],
        out_specs=[q_spec, kv_spec, ds_spec],
        out_shape=[jax.ShapeDtypeStruct(q_all.shape, F32), jax.ShapeDtypeStruct(kv_all.shape, F32),
                   jax.ShapeDtypeStruct((bl, N_HG, 8, 128), F32)],
        args=[sink8, q_all, kv_all, do, o, lse], prefetch=1, sem=("parallel", "parallel", "arbitrary"), comm=comm)
    return outs if comm is None else (outs, couts)


def _silu(x):
    return x * jax.nn.sigmoid(x)


def _ada_fwd(c_rows, w_ada, b_cols, name, comm=None):
    nl, d, w = w_ada.shape
    r = c_rows.shape[0]

    def body(c_ref, w_ref, b_ref, o_ref):
        s = _silu(c_ref[...]).astype(BF16)
        o_ref[...] = jnp.dot(s, w_ref[...].astype(BF16), preferred_element_type=F32) + b_ref[...]

    outs, couts = _call(
        body, name=name, grid=(nl,),
        in_specs=[pl.BlockSpec((r, d), lambda l: (0, 0)), pl.BlockSpec((None, d, w), lambda l: (l, 0, 0)),
                  pl.BlockSpec((None, 1, w), lambda l: (l, 0, 0))],
        out_specs=pl.BlockSpec((None, r, w), lambda l: (l, 0, 0)),
        out_shape=jax.ShapeDtypeStruct((nl, r, w), F32), args=[c_rows, w_ada, b_cols], sem=("parallel",), comm=comm)
    return outs if comm is None else (outs, couts)


def _ada_bwd(c_rows, c_ctx, dmod, w_ada, name):
    nl, d, w = w_ada.shape
    r = c_rows.shape[0]

    def body(c_ref, cc_ref, g_ref, w_ref, dw_ref, dc_ref):
        l = pl.program_id(0)
        s = _silu(c_ref[...]).astype(BF16)
        gm = g_ref[...].astype(BF16)
        dw_ref[...] = lax.dot_general(s, gm, (((0,), (0,)), ((), ())), preferred_element_type=F32)
        ds = lax.dot_general(gm, w_ref[...].astype(BF16), (((1,), (1,)), ((), ())), preferred_element_type=F32)
        rows = lax.broadcasted_iota(jnp.int32, ds.shape, 0)
        dsc = jnp.sum(jnp.where(rows % ADA_ROWS == 2, ds, 0.0), axis=0, keepdims=True)
        x = cc_ref[...]
        sg = jax.nn.sigmoid(x)
        dcc = dsc * (sg * (1.0 + x * (1.0 - sg)))
        out = jnp.concatenate([dcc, jnp.zeros((7, d), F32)], axis=0)

        @pl.when(l == 0)
        def _():
            dc_ref[...] = out

        @pl.when(l != 0)
        def _():
            dc_ref[...] += out

    return pl.pallas_call(
        body, name=name, grid=(nl,),
        in_specs=[pl.BlockSpec((r, d), lambda l: (0, 0)), pl.BlockSpec((1, d), lambda l: (0, 0)),
                  pl.BlockSpec((None, r, w), lambda l: (l, 0, 0)), pl.BlockSpec((None, d, w), lambda l: (l, 0, 0))],
        out_specs=[pl.BlockSpec((None, d, w), lambda l: (l, 0, 0)), pl.BlockSpec((8, d), lambda l: (0, 0))],
        out_shape=[jax.ShapeDtypeStruct((nl, d, w), F32), jax.ShapeDtypeStruct((8, d), F32)],
        compiler_params=_params(("arbitrary",)),
    )(c_rows, c_ctx, dmod, w_ada)


def _adam_math(w, g, m, v):
    m = ADAM_B1 * m + (1.0 - ADAM_B1) * g
    v = ADAM_B2 * v + (1.0 - ADAM_B2) * (g * g)
    m_hat = m / (1.0 - ADAM_B1 ** ADAM_STEP)
    v_hat = v / (1.0 - ADAM_B2 ** ADAM_STEP)
    delta = -ADAM_LR * (m_hat / (jnp.sqrt(v_hat) + ADAM_EPS) + ADAM_WD * w)
    return delta, m, v


def _adamw(w, m, v, g_own, g_recv, name, rows=256):
    nl, r, c = w.shape
    tr = min(rows, r)
    spec = pl.BlockSpec((None, tr, c), lambda l, i: (l, i, 0))
    per_layer = g_own is None
    own = [] if per_layer else [g_own]
    recv = [] if g_recv is None else list(g_recv)
    n_i = r // tr

    def rows_of(li):
        return lambda l, i: jnp.where(l == li, i, jnp.where(l > li, n_i - 1, 0))

    in_specs = [spec] * (3 + len(own))
    in_specs += [pl.BlockSpec((N_DEV, tr, c), lambda l, i, f=rows_of(li): (0, f(l, i), 0)) for li in range(len(recv))]

    def body(*refs):
        w_ref, m_ref, v_ref = refs[:3]
        own_refs, recv_refs = refs[3:3 + len(own)], refs[3 + len(own):3 + len(own) + len(recv)]
        go_ref, d_ref, mo_ref, vo_ref = refs[-4:]

        def update(li):
            if recv:
                g = recv_refs[li][0].astype(F32)
                for k in range(1, N_DEV):
                    g = g + recv_refs[li][k].astype(F32)
            else:
                g = own_refs[0][...]
            delta, m_, v_ = _adam_math(w_ref[...], g, m_ref[...], v_ref[...])
            go_ref[...] = g
            d_ref[...] = delta
            mo_ref[...] = m_
            vo_ref[...] = v_

        if per_layer:
            for li in range(nl):
                pl.when(pl.program_id(0) == li)(functools.partial(update, li))
        else:
            update(0)

    return pl.pallas_call(
        body, name=name, grid=(nl, n_i), in_specs=in_specs, out_specs=[spec] * 4,
        out_shape=[jax.ShapeDtypeStruct(w.shape, F32)] * 4, compiler_params=_params(("parallel", "parallel")),
    )(w, m, v, *own, *recv)


def _small_adamw(w, m, v, g_all, name):
    def body(w_ref, m_ref, v_ref, g_ref, go_ref, d_ref, mo_ref, vo_ref):
        g = g_ref[0]
        for k in range(1, N_DEV):
            g = g + g_ref[k]
        delta, m_, v_ = _adam_math(w_ref[...], g, m_ref[...], v_ref[...])
        go_ref[...] = g
        d_ref[...] = delta
        mo_ref[...] = m_
        vo_ref[...] = v_

    return pl.pallas_call(
        body, name=name, out_shape=[jax.ShapeDtypeStruct(w.shape, F32)] * 4, compiler_params=_params(),
    )(w, m, v, g_all)


SMALL = ("c_ctx", "b_ada", "g_pre_mix", "g_post_mix", "g_pre_mlp", "g_post_mlp", "q_norm", "k_norm", "sink", "loss")


def _pack_small(parts):
    flat = jnp.concatenate([parts[n].reshape(-1) for n in SMALL])
    rows = -(-flat.shape[0] // 1024) * 8
    return jnp.pad(flat, (0, rows * 128 - flat.shape[0])).reshape(rows, 128)


def _unpack_small(packed, like):
    flat = packed.reshape(-1)
    out, off = {}, 0
    for n in SMALL:
        size = int(np.prod(like[n].shape))
        out[n] = flat[off:off + size].reshape(like[n].shape)
        off += size
    return out


def kernel(x, c, ctx, c_ctx, w_ada, b_ada, g_pre_mix, g_post_mix, g_pre_mlp, g_post_mlp, w_in, q_norm, k_norm, sink, w_out, w_up, w_down, loss_target, m_c_ctx, m_w_ada, m_b_ada, m_g_pre_mix, m_g_post_mix, m_g_pre_mlp, m_g_post_mlp, m_w_in, m_q_norm, m_k_norm, m_sink, m_w_out, m_w_up, m_w_down, v_c_ctx, v_w_ada, v_b_ada, v_g_pre_mix, v_g_post_mix, v_g_pre_mlp, v_g_post_mlp, v_w_in, v_q_norm, v_k_norm, v_sink, v_w_out, v_w_up, v_w_down):
    bl, s_rows, d = x.shape
    c_rows = ctx.shape[1]
    assert c_rows == TM and s_rows % TM == 0 and bl == 2
    t_rows = c_rows + s_rows
    nt = t_rows // TM
    m_rows = bl * t_rows
    nl = w_in.shape[0]
    ada_w = w_ada.shape[2]
    d_ff = w_up.shape[2] * N_DEV
    me = _my_index()

    shard = lambda w_, l: w_[l].astype(BF16)
    c_pad = jnp.concatenate([c, c_ctx[None, :], jnp.zeros((ADA_ROWS - bl - 1, d), F32)], axis=0)
    gathered = {0: {}}
    h, (c_all, gathered[0]["w_in"]) = _token_stream(
        ctx, x, nt, "token_stream", comm=_Comm([(c_pad, GATHER_VIA_SIBLING), (shard(w_in, 0), GATHER_VIA_SIBLING)]))
    c_all = c_all.reshape(N_DEV * ADA_ROWS, d)

    def layer_weights(l):
        g_ = gathered[l]
        w_out_f = g_["w_out"].reshape(-1, d)
        w_down_f = g_["w_down"].reshape(d_ff, d)
        return dict(
            w_in_f=g_["w_in"].transpose(1, 0, 2).reshape(d, PROJ_W), w_out_f=w_out_f,
            w_up_s=g_["w_up"], w_up_t=g_["w_up"].transpose(0, 2, 1).reshape(d_ff, d), w_down_f=w_down_f)

    big = dict(tm=2304, tn=512)
    deep = dict(tm=1536, tn=512, tk=d_ff)
    wide = dict(tm=1024, tn=512, tk=m_rows)

    b_cols = lax.dynamic_slice(b_ada, (0, me * ada_w), (nl, ada_w))[:, None, :]
    mod_cols = _ada_fwd(c_all, w_ada, b_cols, "ada_fwd")
    mod_slots = mod_cols.reshape(nl, N_DEV, ADA_ROWS, ada_w).transpose(1, 0, 2, 3)
    mod_g, = _comm_only(_Comm([(mod_slots, TO_OWNER)]), "exchange_mod")
    mine = mod_g.transpose(1, 2, 0, 3).reshape(nl, ADA_ROWS, N_MOD, d)
    pad = jnp.zeros((bl, 2, MOD_ROWS - N_MOD, d), F32)
    modsel = [jnp.concatenate([jnp.stack([jnp.broadcast_to(mine[l, bl], (bl, N_MOD, d)), mine[l, :bl]], axis=1), pad],
                              axis=2) for l in range(nl)]

    cos, sin = _rope_tables(t_rows, c_rows)
    emat = _head_mean_matrix()
    row = lambda a: a[None, :]
    qn = [jnp.tile(q_norm[l], 8)[None, :] for l in range(nl)]
    kn = [jnp.tile(k_norm[l], 2)[None, :] for l in range(nl)]

    target = loss_target.reshape(bl * s_rows, d)
    saved = []
    weights_of = {}
    (u, u_t), (gathered[0]["w_out"],) = _norm_mod_fwd(
        h, row(g_pre_mix[0]), modsel[0], 0, 1, nt, "mix_mod_fwd0", comm=_Comm([(shard(w_out, 0), GATHER_VIA_SIBLING)]))
    for l in range(nl):
        w_in_f = gathered[l]["w_in"].transpose(1, 0, 2).reshape(d, PROJ_W)
        proj = _mm(u, w_in_f, name=f"mm_in{l}", tk=d, **big)
        q_all, kv_all = _prep_fwd(proj, qn[l], kn[l], cos, sin, emat, nt, f"prep_fwd{l}")
        (o, o_t, lse), (w_up_g, w_down_g) = _attn_fwd(
            q_all.reshape(bl, t_rows, 1024), kv_all.reshape(bl, t_rows, 512), sink[l], c_rows, f"attn_fwd{l}",
            comm=_Comm([(shard(w_up, l), GATHER_VIA_SIBLING), (shard(w_down, l), GATHER_VIA_SIBLING)]))
        o = o.reshape(m_rows, 1024)
        gathered[l].update(w_up=w_up_g, w_down=w_down_g)
        wl = weights_of[l] = layer_weights(l)
        mix = _mm(o, wl["w_out_f"], name=f"mm_out{l}", tk=1024, **big)
        h_mid, v_in, v_t = _gate_mod_fwd(h, mix, row(g_post_mix[l]), modsel[l], 2, row(g_pre_mlp[l]), modsel[l], 3, 4,
                                         nt, f"mix_gate_mlp_mod_fwd{l}")
        more = l + 1 < nl
        res_up = _mm(v_in, wl["w_up_s"], name=f"mm_up{l}", b_mode="nn_slots", epilogue="relu2", tk=d,
                     comm=_Comm([(shard(w_in, l + 1), GATHER)]) if more else None, **big)
        (r_act, r_t), nxt_in = res_up if more else (res_up, None)
        res_down = _mm(r_act, wl["w_down_f"], name=f"mm_down{l}",
                       comm=_Comm([(shard(w_out, l + 1), GATHER)]) if more else None, **deep)
        y, nxt_out = res_down if more else (res_down, None)
        saved.append((h, u_t, proj, q_all, kv_all, o, o_t, lse, mix, h_mid, v_t, r_act, r_t, y))
        if more:
            gathered[l + 1] = dict(w_in=nxt_in[0], w_out=nxt_out[0])
            h, u, u_t = _gate_mod_fwd(h_mid, y, row(g_post_mlp[l]), modsel[l], 5, row(g_pre_mix[l + 1]), modsel[l + 1],
                                      0, 1, nt, f"mlp_gate_mix_mod_fwd{l}")

    dh, dy, dvec_g2, ss = _gate_loss_bwd(h_mid, y, row(g_post_mlp[nl - 1]), modsel[nl - 1], 5, target, nt, "gate_loss_bwd")
    small_g = {n: [None] * nl for n in SMALL if n not in ("c_ctx", "b_ada", "loss")}
    dvecs = {l: {} for l in range(nl)}
    dvecs[nl - 1]["g2"] = dvec_g2
    slots = {n: [None] * nl for n in ("w_in", "w_out", "w_up", "w_down")}
    recvd = {n: [None] * nl for n in slots}
    send = lambda n, l_: (slots[n][l_], TO_OWNER_XOR)
    for l in reversed(range(nl)):
        h_in, u_t, proj, q_all, kv_all, o, o_t, lse, mix, h_mid, v_t, r_act, r_t, y = saved[l]
        wl = weights_of[l]
        da = _mm(dy, wl["w_down_f"], name=f"mm_da{l}", b_mode="nt", epilogue="relu2_bwd", extra=r_act, out_dtype=BF16,
                 tk=d, **big)
        dw_down = _mm(r_t, dy, name=f"mm_dw_down{l}", out_dtype=BF16, **wide)
        dw_up = _mm(v_t, da, name=f"mm_dw_up{l}", out_mode="slots", out_dtype=BF16, **wide)
        slots["w_down"][l] = dw_down.reshape(N_DEV, -1, d)
        slots["w_up"][l] = dw_up
        dv = _mm(da, wl["w_up_t"], name=f"mm_dv{l}", **deep)
        dh, dvecs[l]["m2"], dmix, dvecs[l]["g1"] = _mod_gate_bwd(
            h_mid, row(g_pre_mlp[l]), modsel[l], 3, 4, dv, dh, mix, row(g_post_mix[l]), modsel[l], 2, nt,
            f"mlp_mod_mix_gate_bwd{l}")
        do = _mm(dmix, wl["w_out_f"], name=f"mm_do{l}", b_mode="nt", out_dtype=BF16, tk=d, **big)
        dw_out = _mm(o_t, dmix, name=f"mm_dw_out{l}", out_dtype=BF16, **wide)
        slots["w_out"][l] = dw_out.reshape(N_DEV, -1, d)
        going = [("w_down", l), ("w_up", l)] + ([("w_out", l + 1), ("w_in", l + 1)] if l + 1 < nl else [])
        (dq, dkv, dsk), arrived = _attn_bwd(
            q_all.reshape(bl, t_rows, 1024), kv_all.reshape(bl, t_rows, 512), do.reshape(bl, t_rows, 1024),
            o.reshape(bl, t_rows, 1024), lse, sink[l], c_rows, f"attn_bwd{l}", comm=_Comm([send(*g_) for g_ in going]))
        for (n_, l_), got in zip(going, arrived):
            recvd[n_][l_] = got
        last = l == 0
        res = _prep_bwd(proj, dq.reshape(m_rows, 1024), dkv.reshape(m_rows, 512), qn[l], kn[l], cos, sin, emat, nt,
                        f"prep_bwd{l}", comm=_Comm([send("w_out", l)]) if last else None)
        dproj, dqn, dkn = res[0] if last else res
        if last:
            recvd["w_out"][l] = res[1][0]
        dw_in = _mm(u_t, dproj, name=f"mm_dw_in{l}", out_dtype=BF16, **wide)
        slots["w_in"][l] = dw_in.reshape(d, N_DEV, PROJ_W // N_DEV).transpose(1, 0, 2)
        if last:
            halves = [(slots["w_in"][l][:, :d // 2], TO_OWNER_XOR), (slots["w_in"][l][:, d // 2:], TO_OWNER_XOR)]
            du, (got_a,) = _mm(dproj, wl["w_in_f"], name=f"mm_du{l}", b_mode="nt", tk=PROJ_W,
                               comm=_Comm(halves[:1]), **big)
            (dh, dvecs[l]["m1"]), (got_b,) = _norm_mod_bwd(
                h_in, row(g_pre_mix[l]), modsel[l], du, dh, 0, 1, nt, f"mix_mod_bwd{l}", latent_only=True,
                comm=_Comm(halves[1:]))
            recvd["w_in"][l] = jnp.concatenate([got_a, got_b], axis=1)
        else:
            du = _mm(dproj, wl["w_in_f"], name=f"mm_du{l}", b_mode="nt", tk=PROJ_W, **big)
            dh, dvecs[l]["m1"], dy, dvecs[l - 1]["g2"] = _mod_gate_bwd(
                h_in, row(g_pre_mix[l]), modsel[l], 0, 1, du, dh, saved[l - 1][-1], row(g_post_mlp[l - 1]),
                modsel[l - 1], 5, nt, f"mix_mod_mlp_gate_bwd{l}")
        small_g["q_norm"][l] = jnp.sum(dqn[0].reshape(8, HEAD_DIM), axis=0)
        small_g["k_norm"][l] = jnp.sum(dkn[0].reshape(2, HEAD_DIM), axis=0)
        small_g["sink"][l] = jnp.sum(dsk[:, 2:, :GROUP, 0], axis=0).reshape(-1)

    dmod_rows = []
    for l in range(nl):
        m1, g1, m2, g2 = (dvecs[l][n] for n in ("m1", "g1", "m2", "g2"))
        small_g["g_pre_mix"][l] = jnp.sum(m1[:, :, 2], axis=(0, 1))
        small_g["g_post_mix"][l] = jnp.sum(g1[:, :, 1], axis=(0, 1))
        small_g["g_pre_mlp"][l] = jnp.sum(m2[:, :, 2], axis=(0, 1))
        small_g["g_post_mlp"][l] = jnp.sum(g2[:, :, 1], axis=(0, 1))
        dms = jnp.stack([m1[:, :, 0], m1[:, :, 1], g1[:, :, 0], m2[:, :, 0], m2[:, :, 1], g2[:, :, 0]], axis=2)
        rows = jnp.concatenate([dms[:, 1], jnp.sum(dms[:, 0], axis=0)[None]], axis=0)
        dmod_rows.append(jnp.pad(rows.reshape(bl + 1, N_MOD * d), ((0, ADA_ROWS - bl - 1), (0, 0))))
    grad_x = dh.reshape(bl, s_rows, d)

    dmod_slots = jnp.stack(dmod_rows).reshape(nl, ADA_ROWS, N_DEV, ada_w).transpose(2, 0, 1, 3)
    dmod_g, = _comm_only(_Comm([(dmod_slots, TO_OWNER)]), "exchange_dmod")
    dmod_mine = dmod_g.transpose(1, 0, 2, 3).reshape(nl, N_DEV * ADA_ROWS, ada_w)
    dw_ada, dcc = _ada_bwd(c_all, c_ctx[None, :], dmod_mine, w_ada, "ada_bwd")

    parts = {n: jnp.stack(small_g[n]) for n in small_g}
    parts["c_ctx"] = dcc[0]
    parts["loss"] = ss[0, 0:1]
    no_loss = jnp.zeros((1,), F32)
    parts["b_ada"] = jnp.stack([jnp.sum(r_[: bl + 1], axis=0) for r_ in dmod_rows])
    weights = dict(c_ctx=c_ctx, b_ada=b_ada, g_pre_mix=g_pre_mix, g_post_mix=g_post_mix, g_pre_mlp=g_pre_mlp,
                   g_post_mlp=g_post_mlp, q_norm=q_norm, k_norm=k_norm, sink=sink, loss=no_loss)
    moms = dict(c_ctx=m_c_ctx, b_ada=m_b_ada, g_pre_mix=m_g_pre_mix, g_post_mix=m_g_post_mix, g_pre_mlp=m_g_pre_mlp,
                g_post_mlp=m_g_post_mlp, q_norm=m_q_norm, k_norm=m_k_norm, sink=m_sink, loss=no_loss)
    vels = dict(c_ctx=v_c_ctx, b_ada=v_b_ada, g_pre_mix=v_g_pre_mix, g_post_mix=v_g_post_mix, g_pre_mlp=v_g_pre_mlp,
                g_post_mlp=v_g_post_mlp, q_norm=v_q_norm, k_norm=v_k_norm, sink=v_sink, loss=no_loss)
    small_all, = _comm_only(_Comm([(_pack_small(parts), GATHER)]), "gather_small")
    s_out = _small_adamw(_pack_small(weights), _pack_small(moms), _pack_small(vels), small_all, "adamw_small")
    s_g, s_d, s_m, s_v = [_unpack_small(a, weights) for a in s_out]
    loss = 0.5 * s_g["loss"][0] / d

    res = {}
    for n, w_, m_, v_ in (("w_in", w_in, m_w_in, v_w_in), ("w_out", w_out, m_w_out, v_w_out),
                          ("w_up", w_up, m_w_up, v_w_up), ("w_down", w_down, m_w_down, v_w_down)):
        res[n] = _adamw(w_, m_, v_, None, recvd[n], f"adamw_{n}")
    res["w_ada"] = _adamw(w_ada, m_w_ada, v_w_ada, dw_ada, None, "adamw_w_ada")

    order = ("c_ctx", "w_ada", "b_ada", "g_pre_mix", "g_post_mix", "g_pre_mlp", "g_post_mlp", "w_in", "q_norm",
             "k_norm", "sink", "w_out", "w_up", "w_down")
    outs = [loss, grad_x]
    for i, small in enumerate((s_g, s_d, s_m, s_v)):
        outs += [small[n] if n in small else res[n][i] for n in order]
    return tuple(outs)
```

```python
import functools

import jax
import jax.numpy as jnp
import numpy as np
from jax import lax
from jax.experimental import pallas as pl
from jax.experimental.pallas import tpu as pltpu

F32 = jnp.float32
BF16 = jnp.bfloat16

HEAD_DIM = 64
GROUP = 4
N_HG = 4
Q_WIDTH = GROUP * HEAD_DIM
WINDOW = 128
GRID_W = 64
ROPE_THETA = 10000.0
EPS = 1e-6
NEG_BIG = -1e30
N_MOD = 6
MOD_ROWS = 8
TM = 256
N_DEV = 8
ADA_ROWS = 8
VMEM_LIMIT = 56 * 1024 * 1024

ADAM_LR = 0.001
ADAM_B1 = 0.9
ADAM_B2 = 0.999
ADAM_EPS = 1e-08
ADAM_WD = 0.01
ADAM_STEP = 10


def _params(sem=None):
    kw = dict(vmem_limit_bytes=VMEM_LIMIT)
    if sem is not None:
        kw["dimension_semantics"] = sem
    return pltpu.CompilerParams(**kw)


def _my_index():
    return 4 * lax.axis_index("x") + 2 * lax.axis_index("y") + lax.axis_index("c")


def _peer(k):
    x, y, c = lax.axis_index("x"), lax.axis_index("y"), lax.axis_index("c")
    kx, ky, kc = (k >> 2) & 1, (k >> 1) & 1, k & 1
    px = (1 - x) if kx else x
    py = (1 - y) if ky else y
    pc = (1 - c) if kc else c
    return (px, py, pc), 4 * px + 2 * py + pc


GATHER, GATHER_VIA_SIBLING, TO_OWNER, TO_OWNER_XOR = "gather", "gather_via_sibling", "to_owner", "to_owner_xor"


class _Comm:
    def __init__(self, items):
        self.items = list(items)
        self.arrays = [a for a, _ in self.items]

    def out_shapes(self):
        return [jax.ShapeDtypeStruct(((N_DEV,) + a.shape) if kind in (GATHER, GATHER_VIA_SIBLING) else a.shape, a.dtype)
                for a, kind in self.items]

    def sem_shapes(self):
        n = len(self.items) * N_DEV
        return [pltpu.SemaphoreType.DMA((n,)), pltpu.SemaphoreType.DMA((n,))]

    def _two_level(self, x_ref, o_ref, send_sems, recv_sems, base):
        x, y, c = lax.axis_index("x"), lax.axis_index("y"), lax.axis_index("c")
        me, sibling = (x, y, c), (x, y, 1 - c)
        chips = [(1 - x, y), (x, 1 - y), (1 - x, 1 - y)]
        block = lambda p: o_ref.at[4 * p[0] + 2 * p[1] + p[2]]

        def copy(k, owner, to, src=None):
            return pltpu.make_async_remote_copy(
                src_ref=block(owner) if src is None else src, dst_ref=block(owner), send_sem=send_sems.at[base + k],
                recv_sem=recv_sems.at[base + k], device_id=to, device_id_type=pl.DeviceIdType.MESH)

        first = [copy(1, me, sibling, src=x_ref)] + [copy(2 + j, me, (*chip, c), src=x_ref) for j, chip in enumerate(chips)]
        relay = [(copy(2 + j, (*chip, c), me), copy(5 + j, (*chip, c), sibling)) for j, chip in enumerate(chips)]
        last = [copy(1, sibling, me)] + [copy(5 + j, (*chip, 1 - c), me) for j, chip in enumerate(chips)]
        return first, relay, last

    def _copies(self, in_refs, out_refs, send_sems, recv_sems):
        me = _my_index()
        local, remote = [], []
        for i, ((_, kind), x_ref, o_ref) in enumerate(zip(self.items, in_refs, out_refs)):
            base = i * N_DEV
            if kind == GATHER_VIA_SIBLING:
                local.append(pltpu.make_async_copy(x_ref, o_ref.at[me], send_sems.at[base]))
                continue
            own_src = x_ref if kind == GATHER else x_ref.at[me]
            own_dst = o_ref.at[0] if kind == TO_OWNER_XOR else o_ref.at[me]
            local.append(pltpu.make_async_copy(own_src, own_dst, send_sems.at[base]))
            for k in range(1, N_DEV):
                peer, pidx = _peer(k)
                remote.append(pltpu.make_async_remote_copy(
                    src_ref=x_ref if kind == GATHER else x_ref.at[pidx],
                    dst_ref=o_ref.at[k] if kind == TO_OWNER_XOR else o_ref.at[me],
                    send_sem=send_sems.at[base + k], recv_sem=recv_sems.at[base + k],
                    device_id=peer, device_id_type=pl.DeviceIdType.MESH))
        return local, remote

    def start(self, in_refs, out_refs, send_sems, recv_sems):
        local, remote = self._copies(in_refs, out_refs, send_sems, recv_sems)
        for cp in local + remote:
            cp.start()
        for i, ((_, kind), x_ref, o_ref) in enumerate(zip(self.items, in_refs, out_refs)):
            if kind == GATHER_VIA_SIBLING:
                for cp in self._two_level(x_ref, o_ref, send_sems, recv_sems, i * N_DEV)[0]:
                    cp.start()

    def relay(self, in_refs, out_refs, send_sems, recv_sems):
        for i, ((_, kind), x_ref, o_ref) in enumerate(zip(self.items, in_refs, out_refs)):
            if kind == GATHER_VIA_SIBLING:
                for arrival, onward in self._two_level(x_ref, o_ref, send_sems, recv_sems, i * N_DEV)[1]:
                    arrival.wait_recv()
                    onward.start()

    def wait(self, in_refs, out_refs, send_sems, recv_sems, relayed=False):
        local, remote = self._copies(in_refs, out_refs, send_sems, recv_sems)
        for cp in remote:
            cp.wait_recv()
        for cp in remote:
            cp.wait_send()
        if not relayed:
            self.relay(in_refs, out_refs, send_sems, recv_sems)
        for i, ((_, kind), x_ref, o_ref) in enumerate(zip(self.items, in_refs, out_refs)):
            if kind == GATHER_VIA_SIBLING:
                first, relay, last = self._two_level(x_ref, o_ref, send_sems, recv_sems, i * N_DEV)
                for cp in last:
                    cp.wait_recv()
                for cp in first + [onward for _, onward in relay]:
                    cp.wait_send()
        for cp in local:
            cp.wait()


def _call(body, *, name, grid, in_specs, out_specs, out_shape, args, scratch_shapes=(), prefetch=0, sem=None,
          comm=None):
    single = not isinstance(out_shape, (list, tuple))
    out_shape = [out_shape] if single else list(out_shape)
    out_specs = [out_specs] if single else list(out_specs)
    in_specs, scratch_shapes, args = list(in_specs), list(scratch_shapes), list(args)
    n_in, n_out = len(in_specs), len(out_shape)
    if comm is not None:
        nc = len(comm.arrays)
        hbm = pl.BlockSpec(memory_space=pl.ANY)
        inner = body

        def body(*refs):
            pre, r = refs[:prefetch], refs[prefetch:]
            ins, cin = r[:n_in], r[n_in:n_in + nc]
            outs, cout = r[n_in + nc:n_in + nc + n_out], r[n_in + nc + n_out:n_in + 2 * nc + n_out]
            scr, sems = r[n_in + 2 * nc + n_out:len(r) - 2], r[len(r) - 2:]
            ids = [pl.program_id(i) for i in range(len(grid))]

            def when(flags, fn):
                if flags:
                    pl.when(functools.reduce(jnp.logical_and, flags))(fn)
                else:
                    fn()

            when([i == 0 for i in ids], lambda: comm.start(cin, cout, *sems))
            inner(*pre, *ins, *outs, *scr)
            halfway = len(grid) > 1 and grid[0] > 1
            if halfway:
                when([i == (grid[0] // 2 if a == 0 else 0) for a, i in enumerate(ids)],
                     lambda: comm.relay(cin, cout, *sems))
            when([i == n - 1 for i, n in zip(ids, grid)], lambda: comm.wait(cin, cout, *sems, relayed=halfway))

        in_specs += [hbm] * nc
        out_specs += [hbm] * nc
        out_shape += comm.out_shapes()
        scratch_shapes += comm.sem_shapes()
        args += comm.arrays
        sem = ("arbitrary",) * len(grid)
    kw = dict(name=name, out_shape=out_shape, compiler_params=_params(sem if grid else None))
    if prefetch:
        kw["grid_spec"] = pltpu.PrefetchScalarGridSpec(
            num_scalar_prefetch=prefetch, grid=grid, in_specs=in_specs, out_specs=out_specs,
            scratch_shapes=scratch_shapes)
    else:
        kw.update(in_specs=in_specs, out_specs=out_specs, scratch_shapes=scratch_shapes)
        if grid:
            kw["grid"] = grid
    res = list(pl.pallas_call(body, **kw)(*args))
    outs = res[:n_out]
    return (outs[0] if single else outs), res[n_out:]


def _comm_only(comm, name):
    return _call(lambda: None, name=name, grid=(), in_specs=[], out_specs=[], out_shape=[], args=[], comm=comm)[1]


def _mm(a, b, *, name, ta=False, b_mode="nn", out_mode="plain", out_dtype=F32, tm=512, tn=512, tk=512,
        epilogue=None, extra=None, comm=None):
    if ta:
        kdim, m = a.shape
    else:
        m, kdim = a.shape
    if b_mode == "nn":
        n = b.shape[1]
    elif b_mode == "nt":
        n = b.shape[0]
    elif b_mode == "nn_slots":
        n = b.shape[0] * b.shape[2]
        tn = b.shape[2]
    else:
        n = b.shape[1]
        tk = b.shape[2]
    if out_mode == "slots":
        tn = n // N_DEV
    tm, tn, tk = min(tm, m), min(tn, n), min(tk, kdim)
    assert m % tm == 0 and n % tn == 0 and kdim % tk == 0, (name, m, n, kdim, tm, tn, tk)
    nk = kdim // tk

    a_spec = pl.BlockSpec((tk, tm), lambda i, j, k: (k, i)) if ta else pl.BlockSpec((tm, tk), lambda i, j, k: (i, k))
    if b_mode == "nn":
        b_spec = pl.BlockSpec((tk, tn), lambda i, j, k: (k, j))
    elif b_mode == "nt":
        b_spec = pl.BlockSpec((tn, tk), lambda i, j, k: (j, k))
    elif b_mode == "nn_slots":
        b_spec = pl.BlockSpec((None, tk, tn), lambda i, j, k: (j, k, 0))
    else:
        b_spec = pl.BlockSpec((None, tn, tk), lambda i, j, k: (k, j, 0))
    tb = b_mode in ("nt", "nt_slots")
    if out_mode == "plain":
        o_shape, o_spec = (m, n), pl.BlockSpec((tm, tn), lambda i, j, k: (i, j))
    else:
        o_shape, o_spec = (N_DEV, m, tn), pl.BlockSpec((None, tm, tn), lambda i, j, k: (j, i, 0))
    dims = (((0 if ta else 1,), (1 if tb else 0,)), ((), ()))

    in_specs = [a_spec, b_spec]
    args = [a, b]
    if epilogue == "relu2_bwd":
        in_specs.append(pl.BlockSpec((tm, tn), lambda i, j, k: (i, j)))
        args.append(extra)
    if epilogue == "relu2":
        out_shape = [jax.ShapeDtypeStruct(o_shape, BF16), jax.ShapeDtypeStruct((n, m), BF16)]
        out_specs = [o_spec, pl.BlockSpec((tn, tm), lambda i, j, k: (j, i))]
    else:
        out_shape = jax.ShapeDtypeStruct(o_shape, out_dtype)
        out_specs = o_spec

    def finish(refs, acc):
        if epilogue == "relu2":
            r = jnp.maximum(acc, 0.0)
            r2 = (r * r).astype(BF16)
            refs[2][...] = r2
            refs[3][...] = r2.T
        elif epilogue == "relu2_bwd":
            refs[3][...] = (acc * (2.0 * jnp.sqrt(refs[2][...]).astype(F32))).astype(out_dtype)
        else:
            refs[2][...] = acc.astype(out_dtype)

    def body(*refs):
        part = lax.dot_general(refs[0][...], refs[1][...], dims, preferred_element_type=F32)
        if nk == 1:
            finish(refs, part)
            return
        acc_ref = refs[-1]
        k = pl.program_id(2)

        @pl.when(k == 0)
        def _():
            acc_ref[...] = part

        @pl.when(jnp.logical_and(k > 0, k < nk - 1))
        def _():
            acc_ref[...] += part

        @pl.when(k == nk - 1)
        def _():
            finish(refs, acc_ref[...] + part)

    outs, couts = _call(
        body, name=name, grid=(m // tm, n // tn, nk), in_specs=in_specs, out_specs=out_specs, out_shape=out_shape,
        args=args, scratch_shapes=[] if nk == 1 else [pltpu.VMEM((tm, tn), F32)],
        sem=("parallel", "parallel", "arbitrary"), comm=comm)
    return outs if comm is None else (outs, couts)


def _row_specs(d, nt):
    row = pl.BlockSpec((TM, d), lambda t: (t, 0))
    vec = pl.BlockSpec((1, d), lambda t: (0, 0))
    mod = pl.BlockSpec((None, None, MOD_ROWS, d), lambda t: (t // nt, jnp.minimum(t % nt, 1), 0, 0))
    return row, vec, mod


def _rms(x):
    r = lax.rsqrt(jnp.mean(x * x, axis=1, keepdims=True) + EPS)
    return r, x * r


def _token_stream(ctx, x, nt, name, comm=None):
    bl, c_rows, d = ctx.shape
    m = bl * nt * TM

    def body(ctx_ref, x_ref, h_ref):
        t = pl.program_id(0)

        @pl.when(t % nt == 0)
        def _():
            h_ref[...] = ctx_ref[...]

        @pl.when(t % nt != 0)
        def _():
            h_ref[...] = x_ref[...]

    outs, couts = _call(
        body, name=name, grid=(m // TM,),
        in_specs=[pl.BlockSpec((None, TM, d), lambda t: (t // nt, 0, 0)),
                  pl.BlockSpec((None, TM, d), lambda t: (t // nt, jnp.maximum(t % nt, 1) - 1, 0))],
        out_specs=pl.BlockSpec((TM, d), lambda t: (t, 0)), out_shape=jax.ShapeDtypeStruct((m, d), F32),
        args=[ctx, x], sem=("parallel",), comm=comm)
    return outs if comm is None else (outs, couts)


def _norm_mod_fwd(h, g, modsel, i_sh, i_sc, nt, name, comm=None):
    m, d = h.shape
    row, vec, mod = _row_specs(d, nt)

    def body(h_ref, g_ref, ms_ref, u_ref, ut_ref):
        _, xh = _rms(h_ref[...])
        ms = ms_ref[...]
        u = (xh * g_ref[...] * (1.0 + ms[i_sc:i_sc + 1]) + ms[i_sh:i_sh + 1]).astype(BF16)
        u_ref[...] = u
        ut_ref[...] = u.T

    outs, couts = _call(
        body, name=name, grid=(m // TM,), in_specs=[row, vec, mod],
        out_specs=[row, pl.BlockSpec((d, TM), lambda t: (0, t))],
        out_shape=[jax.ShapeDtypeStruct((m, d), BF16), jax.ShapeDtypeStruct((d, m), BF16)],
        args=[h, g, modsel], sem=("parallel",), comm=comm)
    return outs if comm is None else (outs, couts)


def _acc_rows(t, nt, dvec_ref, rows):
    first = (t % nt) <= 1

    @pl.when(first)
    def _():
        dvec_ref[...] = rows

    @pl.when(jnp.logical_not(first))
    def _():
        dvec_ref[...] += rows


def _norm_mod_bwd(h, g, modsel, du, dh_in, i_sh, i_sc, nt, name, comm=None, latent_only=False):
    m, d = h.shape
    row, vec, mod = _row_specs(d, nt)
    dh_rows, dh_spec = m, row
    if latent_only:
        dh_rows = m // nt * (nt - 1)
        dh_spec = pl.BlockSpec((TM, d), lambda t: ((t // nt) * (nt - 1) + jnp.maximum(t % nt, 1) - 1, 0))

    def body(h_ref, g_ref, ms_ref, du_ref, dhi_ref, dh_ref, dvec_ref):
        t = pl.program_id(0)
        r, xh = _rms(h_ref[...])
        g_ = g_ref[...]
        ms = ms_ref[...]
        du_ = du_ref[...]
        y = xh * g_
        dy = du_ * (1.0 + ms[i_sc:i_sc + 1])
        dxh = dy * g_
        dx = r * (dxh - xh * jnp.mean(dxh * xh, axis=1, keepdims=True))
        dh_ref[...] = dhi_ref[...] + dx
        rows = jnp.concatenate([
            jnp.sum(du_, axis=0, keepdims=True), jnp.sum(du_ * y, axis=0, keepdims=True),
            jnp.sum(dy * xh, axis=0, keepdims=True), jnp.zeros((MOD_ROWS - 3, d), F32)], axis=0)
        _acc_rows(t, nt, dvec_ref, rows)

    outs, couts = _call(
        body, name=name, grid=(m // TM,), in_specs=[row, vec, mod, row, row], out_specs=[dh_spec, mod],
        out_shape=[jax.ShapeDtypeStruct((dh_rows, d), F32), jax.ShapeDtypeStruct(modsel.shape, F32)],
        args=[h, g, modsel, du, dh_in], sem=("arbitrary",), comm=comm)
    return outs if comm is None else (outs, couts)


def _gate_rows(dh_, xh, g_, gate):
    dy = dh_ * gate
    return dy * g_, jnp.sum(dh_ * (xh * g_), axis=0, keepdims=True), jnp.sum(dy * xh, axis=0, keepdims=True)


def _gate_mod_fwd(h, z, g_post, ms_gate, i_g, g_pre, ms_mod, i_sh, i_sc, nt, name):
    m, d = h.shape
    row, vec, mod = _row_specs(d, nt)

    def body(h_ref, z_ref, gp_ref, msg_ref, gq_ref, msm_ref, hn_ref, u_ref, ut_ref):
        _, zh = _rms(z_ref[...])
        hn = h_ref[...] + msg_ref[...][i_g:i_g + 1] * (zh * gp_ref[...])
        hn_ref[...] = hn
        _, xh = _rms(hn)
        ms = msm_ref[...]
        u = (xh * gq_ref[...] * (1.0 + ms[i_sc:i_sc + 1]) + ms[i_sh:i_sh + 1]).astype(BF16)
        u_ref[...] = u
        ut_ref[...] = u.T

    return pl.pallas_call(
        body, name=name, grid=(m // TM,), in_specs=[row, row, vec, mod, vec, mod],
        out_specs=[row, row, pl.BlockSpec((d, TM), lambda t: (0, t))],
        out_shape=[jax.ShapeDtypeStruct((m, d), F32), jax.ShapeDtypeStruct((m, d), BF16),
                   jax.ShapeDtypeStruct((d, m), BF16)],
        compiler_params=_params(("parallel",)),
    )(h, z, g_post, ms_gate, g_pre, ms_mod)


def _mod_gate_bwd(h, g_pre, ms_mod, i_sh, i_sc, du, dh_in, z, g_post, ms_gate, i_g, nt, name):
    m, d = h.shape
    row, vec, mod = _row_specs(d, nt)

    def body(h_ref, gq_ref, msm_ref, du_ref, dhi_ref, z_ref, gp_ref, msg_ref, dh_ref, dvm_ref, dz_ref, dvg_ref):
        t = pl.program_id(0)
        r, xh = _rms(h_ref[...])
        gq = gq_ref[...]
        ms = msm_ref[...]
        du_ = du_ref[...]
        dy = du_ * (1.0 + ms[i_sc:i_sc + 1])
        dxh = dy * gq
        dh_ = dhi_ref[...] + r * (dxh - xh * jnp.mean(dxh * xh, axis=1, keepdims=True))
        dh_ref[...] = dh_
        _acc_rows(t, nt, dvm_ref, jnp.concatenate([
            jnp.sum(du_, axis=0, keepdims=True), jnp.sum(du_ * (xh * gq), axis=0, keepdims=True),
            jnp.sum(dy * xh, axis=0, keepdims=True), jnp.zeros((MOD_ROWS - 3, d), F32)], axis=0))
        rz, zh = _rms(z_ref[...])
        dzh, d_gate, d_gp = _gate_rows(dh_, zh, gp_ref[...], msg_ref[...][i_g:i_g + 1])
        dz_ref[...] = (rz * (dzh - zh * jnp.mean(dzh * zh, axis=1, keepdims=True))).astype(BF16)
        _acc_rows(t, nt, dvg_ref, jnp.concatenate([d_gate, d_gp, jnp.zeros((MOD_ROWS - 2, d), F32)], axis=0))

    return pl.pallas_call(
        body, name=name, grid=(m // TM,), in_specs=[row, vec, mod, row, row, row, vec, mod],
        out_specs=[row, mod, row, mod],
        out_shape=[jax.ShapeDtypeStruct((m, d), F32), jax.ShapeDtypeStruct(ms_mod.shape, F32),
                   jax.ShapeDtypeStruct((m, d), BF16), jax.ShapeDtypeStruct(ms_gate.shape, F32)],
        compiler_params=_params(("arbitrary",)),
    )(h, g_pre, ms_mod, du, dh_in, z, g_post, ms_gate)


def _gate_loss_bwd(h, z, g_post, modsel, i_g, target, nt, name):
    m, d = h.shape
    row, vec, mod = _row_specs(d, nt)
    ntl = nt - 1
    tgt = pl.BlockSpec((TM, d), lambda t: ((t // nt) * ntl + jnp.maximum(t % nt, 1) - 1, 0))
    acc = pl.BlockSpec((8, 128), lambda t: (0, 0))

    def body(h_ref, z_ref, gp_ref, ms_ref, t_ref, dh_ref, dz_ref, dvg_ref, ss_ref):
        t = pl.program_id(0)

        @pl.when(t == 0)
        def _():
            ss_ref[...] = jnp.zeros_like(ss_ref)

        latent = (t % nt != 0).astype(F32)
        rz, zh = _rms(z_ref[...])
        gp = gp_ref[...]
        gate = ms_ref[...][i_g:i_g + 1]
        e = h_ref[...] + gate * (zh * gp) - t_ref[...]
        ss_ref[...] += latent * jnp.sum(e * e)
        dh_ = e * (latent / d)
        dh_ref[...] = dh_
        dzh, d_gate, d_gp = _gate_rows(dh_, zh, gp, gate)
        dz_ref[...] = (rz * (dzh - zh * jnp.mean(dzh * zh, axis=1, keepdims=True))).astype(BF16)
        _acc_rows(t, nt, dvg_ref, jnp.concatenate([d_gate, d_gp, jnp.zeros((MOD_ROWS - 2, d), F32)], axis=0))

    return pl.pallas_call(
        body, name=name, grid=(m // TM,), in_specs=[row, row, vec, mod, tgt], out_specs=[row, row, mod, acc],
        out_shape=[jax.ShapeDtypeStruct((m, d), F32), jax.ShapeDtypeStruct((m, d), BF16),
                   jax.ShapeDtypeStruct(modsel.shape, F32), jax.ShapeDtypeStruct((8, 128), F32)],
        compiler_params=_params(("arbitrary",)),
    )(h, z, g_post, modsel, target)


QA, KA, VA, QB, KB, VB = 0, 512, 640, 768, 1280, 1408
PROJ_W = 1536
Q_SCALE = HEAD_DIM ** -0.5
LOG2E = 1.4426950408889634


def _swap16(x):
    lane = lax.broadcasted_iota(jnp.int32, x.shape, 1)
    n = x.shape[1]
    return jnp.where((lane % 32) < 16, pltpu.roll(x, n - 16, 1), pltpu.roll(x, 16, 1))


def _seg_mean(x, e):
    hi = x.astype(BF16)
    lo = (x - hi.astype(F32)).astype(BF16)
    w = e.shape[0]
    both = lambda a: jnp.dot(hi[:, a:a + w], e, preferred_element_type=F32) + jnp.dot(lo[:, a:a + w], e, preferred_element_type=F32)
    parts = [both(a) for a in range(0, x.shape[1], w)]
    return parts[0] if len(parts) == 1 else jnp.concatenate(parts, axis=1)


def _rope_tables(t_rows, c_rows):
    s = t_rows - c_rows
    row_ids = jnp.repeat(jnp.arange(s // GRID_W, dtype=jnp.int32), GRID_W).astype(F32)
    col_ids = jnp.tile(jnp.arange(GRID_W, dtype=jnp.int32), s // GRID_W).astype(F32)
    axis_dim = HEAD_DIM // 2
    inv = ROPE_THETA ** (-jnp.arange(0, axis_dim, 2, dtype=F32) / axis_dim)
    ang_r = row_ids[:, None] * inv[None, :]
    ang_c = col_ids[:, None] * inv[None, :]
    cos = jnp.concatenate([jnp.cos(ang_r), jnp.cos(ang_r), jnp.cos(ang_c), jnp.cos(ang_c)], axis=1)
    sin = jnp.concatenate([-jnp.sin(ang_r), jnp.sin(ang_r), -jnp.sin(ang_c), jnp.sin(ang_c)], axis=1)
    cos = jnp.concatenate([jnp.ones((c_rows, HEAD_DIM), F32), cos], axis=0)
    sin = jnp.concatenate([jnp.zeros((c_rows, HEAD_DIM), F32), sin], axis=0)
    return jnp.tile(cos, (1, 8)), jnp.tile(sin, (1, 8))


def _head_mean_matrix():
    i = np.arange(512)
    return jnp.asarray((i[:, None] // HEAD_DIM == i[None, :] // HEAD_DIM).astype(np.float32) / HEAD_DIM, dtype=BF16)


def _interleave_kv(k, v):
    return jnp.concatenate([k[:, :64], v[:, :64], k[:, 64:], v[:, 64:]], axis=1)


def _prep_fwd(proj, qn, kn, cos, sin, emat, nt, name):
    m = proj.shape[0]
    specs = [
        pl.BlockSpec((TM, PROJ_W), lambda t: (t, 0)),
        pl.BlockSpec((1, 512), lambda t: (0, 0)), pl.BlockSpec((1, 128), lambda t: (0, 0)),
        pl.BlockSpec((TM, 512), lambda t: (t % nt, 0)), pl.BlockSpec((TM, 512), lambda t: (t % nt, 0)),
        pl.BlockSpec((512, 512), lambda t: (0, 0)),
    ]

    def body(p_ref, qn_ref, kn_ref, cos_ref, sin_ref, e_ref, q_ref, kv_ref):
        cos_, sin_, e = cos_ref[...], sin_ref[...], e_ref[...]

        def rope(x, w):
            return x * cos_[:, :w] + _swap16(x) * sin_[:, :w]

        def norm(x, g, w):
            return x * lax.rsqrt(_seg_mean(x * x, e[:min(w, 256), :min(w, 256)]) + EPS) * g

        qa = rope(norm(p_ref[:, QA:QA + 512], qn_ref[...], 512), 512)
        qb = rope(p_ref[:, QB:QB + 512], 512)
        q_ref[:, 0:512] = (qa * (Q_SCALE * LOG2E)).astype(BF16)
        q_ref[:, 512:1024] = (qb * (Q_SCALE * LOG2E)).astype(BF16)
        ka = rope(norm(p_ref[:, KA:KA + 128], kn_ref[...], 128), 128)
        kb = rope(p_ref[:, KB:KB + 128], 128)
        kv_ref[:, 0:256] = _interleave_kv(ka, p_ref[:, VA:VA + 128]).astype(BF16)
        kv_ref[:, 256:512] = _interleave_kv(kb, p_ref[:, VB:VB + 128]).astype(BF16)

    return pl.pallas_call(
        body, name=name, grid=(m // TM,), in_specs=specs,
        out_specs=[pl.BlockSpec((TM, 1024), lambda t: (t, 0)), pl.BlockSpec((TM, 512), lambda t: (t, 0))],
        out_shape=[jax.ShapeDtypeStruct((m, 1024), BF16), jax.ShapeDtypeStruct((m, 512), BF16)],
        compiler_params=_params(("parallel",)),
    )(proj, qn, kn, cos, sin, emat)


def _prep_bwd(proj, dq, dkv, qn, kn, cos, sin, emat, nt, name, comm=None):
    m = proj.shape[0]
    specs = [
        pl.BlockSpec((TM, PROJ_W), lambda t: (t, 0)),
        pl.BlockSpec((TM, 1024), lambda t: (t, 0)), pl.BlockSpec((TM, 512), lambda t: (t, 0)),
        pl.BlockSpec((1, 512), lambda t: (0, 0)), pl.BlockSpec((1, 128), lambda t: (0, 0)),
        pl.BlockSpec((TM, 512), lambda t: (t % nt, 0)), pl.BlockSpec((TM, 512), lambda t: (t % nt, 0)),
        pl.BlockSpec((512, 512), lambda t: (0, 0)),
    ]

    def body(p_ref, dq_ref, dkv_ref, qn_ref, kn_ref, cos_ref, sin_ref, e_ref, dp_ref, dqn_ref, dkn_ref):
        t = pl.program_id(0)
        cos_, sin_, e = cos_ref[...], sin_ref[...], e_ref[...]

        @pl.when(t == 0)
        def _():
            dqn_ref[...] = jnp.zeros_like(dqn_ref)
            dkn_ref[...] = jnp.zeros_like(dkn_ref)

        def unrope(dy, w):
            return dy * cos_[:, :w] + _swap16(dy * sin_[:, :w])

        def norm_bwd(x, g, dy, w):
            r = lax.rsqrt(_seg_mean(x * x, e[:min(w, 256), :min(w, 256)]) + EPS)
            xh = x * r
            dxh = dy * g
            dx = r * (dxh - xh * _seg_mean(dxh * xh, e[:min(w, 256), :min(w, 256)]))
            return dx, jnp.sum(dy * xh, axis=0, keepdims=True)

        dqa, dgq = norm_bwd(p_ref[:, QA:QA + 512], qn_ref[...], unrope(dq_ref[:, 0:512] * Q_SCALE, 512), 512)
        dp_ref[:, QA:QA + 512] = dqa.astype(BF16)
        dp_ref[:, QB:QB + 512] = unrope(dq_ref[:, 512:1024] * Q_SCALE, 512).astype(BF16)
        da = dkv_ref[:, 0:256]
        db = dkv_ref[:, 256:512]
        dka = jnp.concatenate([da[:, 0:64], da[:, 128:192]], axis=1)
        dva = jnp.concatenate([da[:, 64:128], da[:, 192:256]], axis=1)
        dkb = jnp.concatenate([db[:, 0:64], db[:, 128:192]], axis=1)
        dvb = jnp.concatenate([db[:, 64:128], db[:, 192:256]], axis=1)
        dka, dgk = norm_bwd(p_ref[:, KA:KA + 128], kn_ref[...], unrope(dka, 128), 128)
        dp_ref[:, KA:KA + 128] = dka.astype(BF16)
        dp_ref[:, VA:VA + 128] = dva.astype(BF16)
        dp_ref[:, KB:KB + 128] = unrope(dkb, 128).astype(BF16)
        dp_ref[:, VB:VB + 128] = dvb.astype(BF16)
        dqn_ref[0:1, :] += dgq
        dkn_ref[0:1, :] += dgk

    outs, couts = _call(
        body, name=name, grid=(m // TM,), in_specs=specs,
        out_specs=[pl.BlockSpec((TM, PROJ_W), lambda t: (t, 0)), pl.BlockSpec((8, 512), lambda t: (0, 0)),
                   pl.BlockSpec((8, 128), lambda t: (0, 0))],
        out_shape=[jax.ShapeDtypeStruct((m, PROJ_W), BF16), jax.ShapeDtypeStruct((8, 512), F32),
                   jax.ShapeDtypeStruct((8, 128), F32)],
        args=[proj, dq, dkv, qn, kn, cos, sin, emat], sem=("arbitrary",), comm=comm)
    return outs if comm is None else (outs, couts)


PAIR = 2
PAIR_HEADS = [(j, g) for j in range(PAIR) for g in range(GROUP)]
LOOKAHEAD = 4
LOOKAHEAD_BWD = 2


def _attn_case(t, pr, kv_ref, c_rows, t_rows, fn, keys_first=False):
    wl = TM + 2 * WINDOW
    kvd = lambda a, n: kv_ref[pl.ds(a, n), :]
    dense = pr == 0
    ctx = t == 0

    @pl.when(jnp.logical_and(dense, ctx))
    def _():
        fn(kvd(0, c_rows), None, False, [(0, c_rows)])

    @pl.when(jnp.logical_and(dense, jnp.logical_not(ctx)))
    def _():
        fn(kvd(0, t_rows), None, False, [(0, t_rows)])

    @pl.when(jnp.logical_and(jnp.logical_not(dense), ctx))
    def _():
        fn(kvd(0, c_rows), None, True, [(0, c_rows)])

    @pl.when(jnp.logical_and(jnp.logical_not(dense), jnp.logical_not(ctx)))
    def _():
        start = pl.multiple_of(jnp.minimum(c_rows + (t - 1) * TM - WINDOW, t_rows - wl), 128)
        kv = jnp.concatenate([kvd(0, c_rows), kvd(start, wl)], axis=0)
        shape = (c_rows + wl, TM) if keys_first else (TM, c_rows + wl)
        q_i = lax.broadcasted_iota(jnp.int32, shape, 1 if keys_first else 0)
        k_i = lax.broadcasted_iota(jnp.int32, shape, 0 if keys_first else 1)
        qpos = (t - 1) * TM + q_i
        kpos = start - 2 * c_rows + k_i
        mask = jnp.logical_or(k_i < c_rows, jnp.logical_and(jnp.abs(kpos - qpos) <= WINDOW, kpos >= 0))
        fn(kv, mask, True, [(0, c_rows), (start, wl)])


def _keys_values(kv, j):
    return kv[:, j * 128:j * 128 + 64], kv[:, j * 128 + 64:(j + 1) * 128]


def _head_cols(ref, j, g):
    a = j * Q_WIDTH + g * HEAD_DIM
    return ref[:, a:a + HEAD_DIM]


def _head_columns(cols):
    lane = lax.broadcasted_iota(jnp.int32, (TM, 128), 1)
    out = jnp.zeros((TM, 128), F32)
    for g, col in enumerate(cols):
        out = jnp.where(lane == g, col, out)
    return out


def _attn_specs(t_rows):
    nt = t_rows // TM
    q_spec = pl.BlockSpec((None, TM, PAIR * Q_WIDTH), lambda b, pr, t, s: (b, t, pr))
    kv_spec = pl.BlockSpec((None, t_rows, PAIR * 128), lambda b, pr, t, s: (b, 0, pr))
    lse_spec = pl.BlockSpec((PAIR, TM, 128), lambda b, pr, t, s: (pr, b * nt + t, 0))
    return q_spec, kv_spec, lse_spec


def _attn_fwd(q_all, kv_all, sink8, c_rows, name, comm=None):
    bl, t_rows, _ = q_all.shape
    q_spec, kv_spec, lse_spec = _attn_specs(t_rows)

    def body(sink_ref, q_ref, kv_ref, o_ref, ot_ref, lse_ref):
        pr, t = pl.program_id(1), pl.program_id(2)

        def fn(kv, mask, use_sink, spans):
            outs, lses = [], []
            ks = [_keys_values(kv, j)[0] for j in range(PAIR)]
            v_ones = [jnp.concatenate([_keys_values(kv, j)[1], jnp.ones((kv.shape[0], 64), BF16)], axis=1)
                      for j in range(PAIR)]

            def scores(i):
                j, g = PAIR_HEADS[i]
                s = lax.dot_general(_head_cols(q_ref, j, g), ks[j], (((1,), (1,)), ((), ())),
                                    preferred_element_type=F32)
                return s if mask is None else jnp.where(mask, s, NEG_BIG)

            ahead = [scores(i) for i in range(LOOKAHEAD)]
            for i, (j, g) in enumerate(PAIR_HEADS):
                s = ahead.pop(0)
                if i + LOOKAHEAD < len(PAIR_HEADS):
                    ahead.append(scores(i + LOOKAHEAD))
                mx = jnp.max(s, axis=1, keepdims=True)
                if use_sink:
                    sink = sink_ref[j * GROUP + g] * LOG2E
                    mx = jnp.maximum(mx, sink)
                pv = jnp.dot(jnp.exp2(s - mx).astype(BF16), v_ones[j], preferred_element_type=F32)
                l = pv[:, 64:65]
                if use_sink:
                    l = l + jnp.exp2(sink - mx)
                outs.append(pv[:, :64] * (1.0 / l))
                lses.append(mx + jnp.log2(l))
            o = jnp.concatenate(outs, axis=1).astype(BF16)
            o_ref[...] = o
            ot_ref[...] = o.T
            for j in range(PAIR):
                lse_ref[j] = _head_columns(lses[j * GROUP:(j + 1) * GROUP])

        _attn_case(t, pr, kv_ref, c_rows, t_rows, fn)

    nt = t_rows // TM
    ot_spec = pl.BlockSpec((PAIR * Q_WIDTH, TM), lambda b, pr, t, s: (pr, b * nt + t))
    outs, couts = _call(
        body, name=name, grid=(bl, N_HG // PAIR, nt), in_specs=[q_spec, kv_spec], out_specs=[q_spec, ot_spec, lse_spec],
        out_shape=[jax.ShapeDtypeStruct(q_all.shape, BF16), jax.ShapeDtypeStruct((N_HG * Q_WIDTH, bl * t_rows), BF16),
                   jax.ShapeDtypeStruct((N_HG, bl * t_rows, 128), F32)],
        args=[sink8, q_all, kv_all], prefetch=1, sem=("parallel", "parallel", "arbitrary"), comm=comm)
    return outs if comm is None else (outs, couts)


def _attn_bwd(q_all, kv_all, do, o, lse, sink8, c_rows, name, comm=None):
    bl, t_rows, _ = q_all.shape
    q_spec, kv_spec, lse_spec = _attn_specs(t_rows)
    ds_spec = pl.BlockSpec((None, PAIR, 8, 128), lambda b, pr, t, s: (b, pr, 0, 0))

    def body(sink_ref, q_ref, kv_ref, do_ref, o_ref, lse_ref, dq_ref, dkv_ref, dsk_ref):
        pr, t = pl.program_id(1), pl.program_id(2)

        @pl.when(t == 0)
        def _():
            dkv_ref[...] = jnp.zeros_like(dkv_ref)
            dsk_ref[...] = jnp.zeros_like(dsk_ref)

        lse_rows = [lse_ref[j].T for j in range(PAIR)]
        dd_rows = [_head_columns([jnp.sum(_head_cols(do_ref, j, g).astype(F32) * _head_cols(o_ref, j, g).astype(F32),
                                          axis=1, keepdims=True) for g in range(GROUP)]).T for j in range(PAIR)]

        def fn(kv, mask, use_sink, spans):
            ks, vs = zip(*[_keys_values(kv, j) for j in range(PAIR)])
            k_ts = [k.T for k in ks]
            dq_t = []
            dsinks = [[] for _ in range(PAIR)]
            dks = [jnp.zeros(ks[0].shape, F32) for _ in range(PAIR)]
            dvs = [jnp.zeros(ks[0].shape, F32) for _ in range(PAIR)]

            def products(i):
                j, g = PAIR_HEADS[i]
                s = lax.dot_general(ks[j], _head_cols(q_ref, j, g), (((1,), (1,)), ((), ())),
                                    preferred_element_type=F32)
                dp = lax.dot_general(vs[j], _head_cols(do_ref, j, g), (((1,), (1,)), ((), ())),
                                     preferred_element_type=F32)
                return (s if mask is None else jnp.where(mask, s, NEG_BIG)), dp

            ahead = [products(i) for i in range(LOOKAHEAD_BWD)]
            for i, (j, g) in enumerate(PAIR_HEADS):
                q, do_g = _head_cols(q_ref, j, g), _head_cols(do_ref, j, g)
                lse_g, dd_g = lse_rows[j][g:g + 1, :], dd_rows[j][g:g + 1, :]
                s, dp = ahead.pop(0)
                if i + LOOKAHEAD_BWD < len(PAIR_HEADS):
                    ahead.append(products(i + LOOKAHEAD_BWD))
                pb = jnp.exp2(s - lse_g).astype(BF16)
                ds = (pb.astype(F32) * (dp - dd_g)).astype(BF16)
                dks[j] = dks[j] + jnp.dot(ds, q, preferred_element_type=F32)
                dvs[j] = dvs[j] + jnp.dot(pb, do_g, preferred_element_type=F32)
                dq_t.append(jnp.dot(k_ts[j], ds, preferred_element_type=F32))
                if use_sink:
                    p_sink = jnp.exp2(sink_ref[j * GROUP + g] * LOG2E - lse_g)
                    dsinks[j].append(jnp.broadcast_to(-jnp.sum(p_sink * dd_g, axis=1, keepdims=True), (1, 128)))
            dq_ref[...] = jnp.concatenate(dq_t, axis=0).T
            dkv = jnp.concatenate([a for j in range(PAIR) for a in (dks[j] * (1.0 / LOG2E), dvs[j])], axis=1)
            off = 0
            for start, size in spans:
                dkv_ref[pl.ds(start, size), :] += dkv[off:off + size]
                off += size
            if use_sink:
                for j in range(PAIR):
                    dsk_ref[j, 0:GROUP, :] += jnp.concatenate(dsinks[j], axis=0)

        _attn_case(t, pr, kv_ref, c_rows, t_rows, fn, keys_first=True)

    outs, couts = _call(
        body, name=name, grid=(bl, N_HG // PAIR, t_rows // TM), in_specs=[q_spec, kv_spec, q_spec, q_spec, lse_spec],
        out_specs=[q_spec, kv_spec, ds_spec],
        out_shape=[jax.ShapeDtypeStruct(q_all.shape, F32), jax.ShapeDtypeStruct(kv_all.shape, F32),
                   jax.ShapeDtypeStruct((bl, N_HG, 8, 128), F32)],
        args=[sink8, q_all, kv_all, do, o, lse], prefetch=1, sem=("parallel", "parallel", "arbitrary"), comm=comm)
    return outs if comm is None else (outs, couts)


def _silu(x):
    return x * jax.nn.sigmoid(x)


def _ada_fwd(c_rows, w_ada, b_cols, name, comm=None):
    nl, d, w = w_ada.shape
    r = c_rows.shape[0]

    def body(c_ref, w_ref, b_ref, o_ref):
        s = _silu(c_ref[...]).astype(BF16)
        o_ref[...] = jnp.dot(s, w_ref[...].astype(BF16), preferred_element_type=F32) + b_ref[...]

    outs, couts = _call(
        body, name=name, grid=(nl,),
        in_specs=[pl.BlockSpec((r, d), lambda l: (0, 0)), pl.BlockSpec((None, d, w), lambda l: (l, 0, 0)),
                  pl.BlockSpec((None, 1, w), lambda l: (l, 0, 0))],
        out_specs=pl.BlockSpec((None, r, w), lambda l: (l, 0, 0)),
        out_shape=jax.ShapeDtypeStruct((nl, r, w), F32), args=[c_rows, w_ada, b_cols], sem=("parallel",), comm=comm)
    return outs if comm is None else (outs, couts)


def _ada_bwd(c_rows, c_ctx, dmod, w_ada, name):
    nl, d, w = w_ada.shape
    r = c_rows.shape[0]

    def body(c_ref, cc_ref, g_ref, w_ref, dw_ref, dc_ref):
        l = pl.program_id(0)
        s = _silu(c_ref[...]).astype(BF16)
        gm = g_ref[...].astype(BF16)
        dw_ref[...] = lax.dot_general(s, gm, (((0,), (0,)), ((), ())), preferred_element_type=F32)
        ds = lax.dot_general(gm, w_ref[...].astype(BF16), (((1,), (1,)), ((), ())), preferred_element_type=F32)
        rows = lax.broadcasted_iota(jnp.int32, ds.shape, 0)
        dsc = jnp.sum(jnp.where(rows % ADA_ROWS == 2, ds, 0.0), axis=0, keepdims=True)
        x = cc_ref[...]
        sg = jax.nn.sigmoid(x)
        dcc = dsc * (sg * (1.0 + x * (1.0 - sg)))
        out = jnp.concatenate([dcc, jnp.zeros((7, d), F32)], axis=0)

        @pl.when(l == 0)
        def _():
            dc_ref[...] = out

        @pl.when(l != 0)
        def _():
            dc_ref[...] += out

    return pl.pallas_call(
        body, name=name, grid=(nl,),
        in_specs=[pl.BlockSpec((r, d), lambda l: (0, 0)), pl.BlockSpec((1, d), lambda l: (0, 0)),
                  pl.BlockSpec((None, r, w), lambda l: (l, 0, 0)), pl.BlockSpec((None, d, w), lambda l: (l, 0, 0))],
        out_specs=[pl.BlockSpec((None, d, w), lambda l: (l, 0, 0)), pl.BlockSpec((8, d), lambda l: (0, 0))],
        out_shape=[jax.ShapeDtypeStruct((nl, d, w), F32), jax.ShapeDtypeStruct((8, d), F32)],
        compiler_params=_params(("arbitrary",)),
    )(c_rows, c_ctx, dmod, w_ada)


def _adam_math(w, g, m, v):
    m = ADAM_B1 * m + (1.0 - ADAM_B1) * g
    v = ADAM_B2 * v + (1.0 - ADAM_B2) * (g * g)
    m_hat = m / (1.0 - ADAM_B1 ** ADAM_STEP)
    v_hat = v / (1.0 - ADAM_B2 ** ADAM_STEP)
    delta = -ADAM_LR * (m_hat / (jnp.sqrt(v_hat) + ADAM_EPS) + ADAM_WD * w)
    return delta, m, v


def _adamw(w, m, v, g_own, g_recv, name, rows=256):
    nl, r, c = w.shape
    tr = min(rows, r)
    spec = pl.BlockSpec((None, tr, c), lambda l, i: (l, i, 0))
    per_layer = g_own is None
    own = [] if per_layer else [g_own]
    recv = [] if g_recv is None else list(g_recv)
    n_i = r // tr

    def rows_of(li):
        return lambda l, i: jnp.where(l == li, i, jnp.where(l > li, n_i - 1, 0))

    in_specs = [spec] * (3 + len(own))
    in_specs += [pl.BlockSpec((N_DEV, tr, c), lambda l, i, f=rows_of(li): (0, f(l, i), 0)) for li in range(len(recv))]

    def body(*refs):
        w_ref, m_ref, v_ref = refs[:3]
        own_refs, recv_refs = refs[3:3 + len(own)], refs[3 + len(own):3 + len(own) + len(recv)]
        go_ref, d_ref, mo_ref, vo_ref = refs[-4:]

        def update(li):
            if recv:
                g = recv_refs[li][0].astype(F32)
                for k in range(1, N_DEV):
                    g = g + recv_refs[li][k].astype(F32)
            else:
                g = own_refs[0][...]
            delta, m_, v_ = _adam_math(w_ref[...], g, m_ref[...], v_ref[...])
            go_ref[...] = g
            d_ref[...] = delta
            mo_ref[...] = m_
            vo_ref[...] = v_

        if per_layer:
            for li in range(nl):
                pl.when(pl.program_id(0) == li)(functools.partial(update, li))
        else:
            update(0)

    return pl.pallas_call(
        body, name=name, grid=(nl, n_i), in_specs=in_specs, out_specs=[spec] * 4,
        out_shape=[jax.ShapeDtypeStruct(w.shape, F32)] * 4, compiler_params=_params(("parallel", "parallel")),
    )(w, m, v, *own, *recv)


def _small_adamw(w, m, v, g_all, name):
    def body(w_ref, m_ref, v_ref, g_ref, go_ref, d_ref, mo_ref, vo_ref):
        g = g_ref[0]
        for k in range(1, N_DEV):
            g = g + g_ref[k]
        delta, m_, v_ = _adam_math(w_ref[...], g, m_ref[...], v_ref[...])
        go_ref[...] = g
        d_ref[...] = delta
        mo_ref[...] = m_
        vo_ref[...] = v_

    return pl.pallas_call(
        body, name=name, out_shape=[jax.ShapeDtypeStruct(w.shape, F32)] * 4, compiler_params=_params(),
    )(w, m, v, g_all)


SMALL = ("c_ctx", "b_ada", "g_pre_mix", "g_post_mix", "g_pre_mlp", "g_post_mlp", "q_norm", "k_norm", "sink", "loss")


def _pack_small(parts):
    flat = jnp.concatenate([parts[n].reshape(-1) for n in SMALL])
    rows = -(-flat.shape[0] // 1024) * 8
    return jnp.pad(flat, (0, rows * 128 - flat.shape[0])).reshape(rows, 128)


def _unpack_small(packed, like):
    flat = packed.reshape(-1)
    out, off = {}, 0
    for n in SMALL:
        size = int(np.prod(like[n].shape))
        out[n] = flat[off:off + size].reshape(like[n].shape)
        off += size
    return out


def kernel(x, c, ctx, c_ctx, w_ada, b_ada, g_pre_mix, g_post_mix, g_pre_mlp, g_post_mlp, w_in, q_norm, k_norm, sink, w_out, w_up, w_down, loss_target, m_c_ctx, m_w_ada, m_b_ada, m_g_pre_mix, m_g_post_mix, m_g_pre_mlp, m_g_post_mlp, m_w_in, m_q_norm, m_k_norm, m_sink, m_w_out, m_w_up, m_w_down, v_c_ctx, v_w_ada, v_b_ada, v_g_pre_mix, v_g_post_mix, v_g_pre_mlp, v_g_post_mlp, v_w_in, v_q_norm, v_k_norm, v_sink, v_w_out, v_w_up, v_w_down):
    bl, s_rows, d = x.shape
    c_rows = ctx.shape[1]
    assert c_rows == TM and s_rows % TM == 0 and bl == 2
    t_rows = c_rows + s_rows
    nt = t_rows // TM
    m_rows = bl * t_rows
    nl = w_in.shape[0]
    ada_w = w_ada.shape[2]
    d_ff = w_up.shape[2] * N_DEV
    me = _my_index()

    shard = lambda w_, l: w_[l].astype(BF16)
    c_pad = jnp.concatenate([c, c_ctx[None, :], jnp.zeros((ADA_ROWS - bl - 1, d), F32)], axis=0)
    gathered = {0: {}}
    h, (c_all, gathered[0]["w_in"]) = _token_stream(
        ctx, x, nt, "token_stream", comm=_Comm([(c_pad, GATHER_VIA_SIBLING), (shard(w_in, 0), GATHER_VIA_SIBLING)]))
    c_all = c_all.reshape(N_DEV * ADA_ROWS, d)

    def layer_weights(l):
        g_ = gathered[l]
        w_out_f = g_["w_out"].reshape(-1, d)
        w_down_f = g_["w_down"].reshape(d_ff, d)
        return dict(
            w_in_f=g_["w_in"].transpose(1, 0, 2).reshape(d, PROJ_W), w_out_f=w_out_f,
            w_up_s=g_["w_up"], w_up_t=g_["w_up"].transpose(0, 2, 1).reshape(d_ff, d), w_down_f=w_down_f)

    big = dict(tm=2304, tn=512)
    deep = dict(tm=1536, tn=512, tk=d_ff)
    wide = dict(tm=1024, tn=512, tk=m_rows)

    b_cols = lax.dynamic_slice(b_ada, (0, me * ada_w), (nl, ada_w))[:, None, :]
    mod_cols = _ada_fwd(c_all, w_ada, b_cols, "ada_fwd")
    mod_slots = mod_cols.reshape(nl, N_DEV, ADA_ROWS, ada_w).transpose(1, 0, 2, 3)
    mod_g, = _comm_only(_Comm([(mod_slots, TO_OWNER)]), "exchange_mod")
    mine = mod_g.transpose(1, 2, 0, 3).reshape(nl, ADA_ROWS, N_MOD, d)
    pad = jnp.zeros((bl, 2, MOD_ROWS - N_MOD, d), F32)
    modsel = [jnp.concatenate([jnp.stack([jnp.broadcast_to(mine[l, bl], (bl, N_MOD, d)), mine[l, :bl]], axis=1), pad],
                              axis=2) for l in range(nl)]

    cos, sin = _rope_tables(t_rows, c_rows)
    emat = _head_mean_matrix()
    row = lambda a: a[None, :]
    qn = [jnp.tile(q_norm[l], 8)[None, :] for l in range(nl)]
    kn = [jnp.tile(k_norm[l], 2)[None, :] for l in range(nl)]

    target = loss_target.reshape(bl * s_rows, d)
    saved = []
    weights_of = {}
    (u, u_t), (gathered[0]["w_out"],) = _norm_mod_fwd(
        h, row(g_pre_mix[0]), modsel[0], 0, 1, nt, "mix_mod_fwd0", comm=_Comm([(shard(w_out, 0), GATHER_VIA_SIBLING)]))
    for l in range(nl):
        w_in_f = gathered[l]["w_in"].transpose(1, 0, 2).reshape(d, PROJ_W)
        proj = _mm(u, w_in_f, name=f"mm_in{l}", tk=d, **big)
        q_all, kv_all = _prep_fwd(proj, qn[l], kn[l], cos, sin, emat, nt, f"prep_fwd{l}")
        (o, o_t, lse), (w_up_g, w_down_g) = _attn_fwd(
            q_all.reshape(bl, t_rows, 1024), kv_all.reshape(bl, t_rows, 512), sink[l], c_rows, f"attn_fwd{l}",
            comm=_Comm([(shard(w_up, l), GATHER_VIA_SIBLING), (shard(w_down, l), GATHER_VIA_SIBLING)]))
        o = o.reshape(m_rows, 1024)
        gathered[l].update(w_up=w_up_g, w_down=w_down_g)
        wl = weights_of[l] = layer_weights(l)
        mix = _mm(o, wl["w_out_f"], name=f"mm_out{l}", tk=1024, **big)
        h_mid, v_in, v_t = _gate_mod_fwd(h, mix, row(g_post_mix[l]), modsel[l], 2, row(g_pre_mlp[l]), modsel[l], 3, 4,
                                         nt, f"mix_gate_mlp_mod_fwd{l}")
        more = l + 1 < nl
        res_up = _mm(v_in, wl["w_up_s"], name=f"mm_up{l}", b_mode="nn_slots", epilogue="relu2", tk=d,
                     comm=_Comm([(shard(w_in, l + 1), GATHER_VIA_SIBLING)]) if more else None, **big)
        (r_act, r_t), nxt_in = res_up if more else (res_up, None)
        res_down = _mm(r_act, wl["w_down_f"], name=f"mm_down{l}",
                       comm=_Comm([(shard(w_out, l + 1), GATHER_VIA_SIBLING)]) if more else None, **deep)
        y, nxt_out = res_down if more else (res_down, None)
        saved.append((h, u_t, proj, q_all, kv_all, o, o_t, lse, mix, h_mid, v_t, r_act, r_t, y))
        if more:
            gathered[l + 1] = dict(w_in=nxt_in[0], w_out=nxt_out[0])
            h, u, u_t = _gate_mod_fwd(h_mid, y, row(g_post_mlp[l]), modsel[l], 5, row(g_pre_mix[l + 1]), modsel[l + 1],
                                      0, 1, nt, f"mlp_gate_mix_mod_fwd{l}")

    dh, dy, dvec_g2, ss = _gate_loss_bwd(h_mid, y, row(g_post_mlp[nl - 1]), modsel[nl - 1], 5, target, nt, "gate_loss_bwd")
    small_g = {n: [None] * nl for n in SMALL if n not in ("c_ctx", "b_ada", "loss")}
    dvecs = {l: {} for l in range(nl)}
    dvecs[nl - 1]["g2"] = dvec_g2
    slots = {n: [None] * nl for n in ("w_in", "w_out", "w_up", "w_down")}
    recvd = {n: [None] * nl for n in slots}
    send = lambda n, l_: (slots[n][l_], TO_OWNER_XOR)
    for l in reversed(range(nl)):
        h_in, u_t, proj, q_all, kv_all, o, o_t, lse, mix, h_mid, v_t, r_act, r_t, y = saved[l]
        wl = weights_of[l]
        da = _mm(dy, wl["w_down_f"], name=f"mm_da{l}", b_mode="nt", epilogue="relu2_bwd", extra=r_act, out_dtype=BF16,
                 tk=d, **big)
        dw_down = _mm(r_t, dy, name=f"mm_dw_down{l}", out_dtype=BF16, **wide)
        dw_up = _mm(v_t, da, name=f"mm_dw_up{l}", out_mode="slots", out_dtype=BF16, **wide)
        slots["w_down"][l] = dw_down.reshape(N_DEV, -1, d)
        slots["w_up"][l] = dw_up
        dv = _mm(da, wl["w_up_t"], name=f"mm_dv{l}", **deep)
        dh, dvecs[l]["m2"], dmix, dvecs[l]["g1"] = _mod_gate_bwd(
            h_mid, row(g_pre_mlp[l]), modsel[l], 3, 4, dv, dh, mix, row(g_post_mix[l]), modsel[l], 2, nt,
            f"mlp_mod_mix_gate_bwd{l}")
        do = _mm(dmix, wl["w_out_f"], name=f"mm_do{l}", b_mode="nt", out_dtype=BF16, tk=d, **big)
        dw_out = _mm(o_t, dmix, name=f"mm_dw_out{l}", out_dtype=BF16, **wide)
        slots["w_out"][l] = dw_out.reshape(N_DEV, -1, d)
        going = [("w_down", l), ("w_up", l)] + ([("w_out", l + 1), ("w_in", l + 1)] if l + 1 < nl else [])
        (dq, dkv, dsk), arrived = _attn_bwd(
            q_all.reshape(bl, t_rows, 1024), kv_all.reshape(bl, t_rows, 512), do.reshape(bl, t_rows, 1024),
            o.reshape(bl, t_rows, 1024), lse, sink[l], c_rows, f"attn_bwd{l}", comm=_Comm([send(*g_) for g_ in going]))
        for (n_, l_), got in zip(going, arrived):
            recvd[n_][l_] = got
        last = l == 0
        res = _prep_bwd(proj, dq.reshape(m_rows, 1024), dkv.reshape(m_rows, 512), qn[l], kn[l], cos, sin, emat, nt,
                        f"prep_bwd{l}", comm=_Comm([send("w_out", l)]) if last else None)
        dproj, dqn, dkn = res[0] if last else res
        if last:
            recvd["w_out"][l] = res[1][0]
        dw_in = _mm(u_t, dproj, name=f"mm_dw_in{l}", out_dtype=BF16, **wide)
        slots["w_in"][l] = dw_in.reshape(d, N_DEV, PROJ_W // N_DEV).transpose(1, 0, 2)
        if last:
            halves = [(slots["w_in"][l][:, :d // 2], TO_OWNER_XOR), (slots["w_in"][l][:, d // 2:], TO_OWNER_XOR)]
            du, (got_a,) = _mm(dproj, wl["w_in_f"], name=f"mm_du{l}", b_mode="nt", tk=PROJ_W,
                               comm=_Comm(halves[:1]), **big)
            (dh, dvecs[l]["m1"]), (got_b,) = _norm_mod_bwd(
                h_in, row(g_pre_mix[l]), modsel[l], du, dh, 0, 1, nt, f"mix_mod_bwd{l}", latent_only=True,
                comm=_Comm(halves[1:]))
            recvd["w_in"][l] = jnp.concatenate([got_a, got_b], axis=1)
        else:
            du = _mm(dproj, wl["w_in_f"], name=f"mm_du{l}", b_mode="nt", tk=PROJ_W, **big)
            dh, dvecs[l]["m1"], dy, dvecs[l - 1]["g2"] = _mod_gate_bwd(
                h_in, row(g_pre_mix[l]), modsel[l], 0, 1, du, dh, saved[l - 1][-1], row(g_post_mlp[l - 1]),
                modsel[l - 1], 5, nt, f"mix_mod_mlp_gate_bwd{l}")
        small_g["q_norm"][l] = jnp.sum(dqn[0].reshape(8, HEAD_DIM), axis=0)
        small_g["k_norm"][l] = jnp.sum(dkn[0].reshape(2, HEAD_DIM), axis=0)
        small_g["sink"][l] = jnp.sum(dsk[:, 2:, :GROUP, 0], axis=0).reshape(-1)

    dmod_rows = []
    for l in range(nl):
        m1, g1, m2, g2 = (dvecs[l][n] for n in ("m1", "g1", "m2", "g2"))
        small_g["g_pre_mix"][l] = jnp.sum(m1[:, :, 2], axis=(0, 1))
        small_g["g_post_mix"][l] = jnp.sum(g1[:, :, 1], axis=(0, 1))
        small_g["g_pre_mlp"][l] = jnp.sum(m2[:, :, 2], axis=(0, 1))
        small_g["g_post_mlp"][l] = jnp.sum(g2[:, :, 1], axis=(0, 1))
        dms = jnp.stack([m1[:, :, 0], m1[:, :, 1], g1[:, :, 0], m2[:, :, 0], m2[:, :, 1], g2[:, :, 0]], axis=2)
        rows = jnp.concatenate([dms[:, 1], jnp.sum(dms[:, 0], axis=0)[None]], axis=0)
        dmod_rows.append(jnp.pad(rows.reshape(bl + 1, N_MOD * d), ((0, ADA_ROWS - bl - 1), (0, 0))))
    grad_x = dh.reshape(bl, s_rows, d)

    dmod_slots = jnp.stack(dmod_rows).reshape(nl, ADA_ROWS, N_DEV, ada_w).transpose(2, 0, 1, 3)
    dmod_g, = _comm_only(_Comm([(dmod_slots, TO_OWNER)]), "exchange_dmod")
    dmod_mine = dmod_g.transpose(1, 0, 2, 3).reshape(nl, N_DEV * ADA_ROWS, ada_w)
    dw_ada, dcc = _ada_bwd(c_all, c_ctx[None, :], dmod_mine, w_ada, "ada_bwd")

    parts = {n: jnp.stack(small_g[n]) for n in small_g}
    parts["c_ctx"] = dcc[0]
    parts["loss"] = ss[0, 0:1]
    no_loss = jnp.zeros((1,), F32)
    parts["b_ada"] = jnp.stack([jnp.sum(r_[: bl + 1], axis=0) for r_ in dmod_rows])
    weights = dict(c_ctx=c_ctx, b_ada=b_ada, g_pre_mix=g_pre_mix, g_post_mix=g_post_mix, g_pre_mlp=g_pre_mlp,
                   g_post_mlp=g_post_mlp, q_norm=q_norm, k_norm=k_norm, sink=sink, loss=no_loss)
    moms = dict(c_ctx=m_c_ctx, b_ada=m_b_ada, g_pre_mix=m_g_pre_mix, g_post_mix=m_g_post_mix, g_pre_mlp=m_g_pre_mlp,
                g_post_mlp=m_g_post_mlp, q_norm=m_q_norm, k_norm=m_k_norm, sink=m_sink, loss=no_loss)
    vels = dict(c_ctx=v_c_ctx, b_ada=v_b_ada, g_pre_mix=v_g_pre_mix, g_post_mix=v_g_post_mix, g_pre_mlp=v_g_pre_mlp,
                g_post_mlp=v_g_post_mlp, q_norm=v_q_norm, k_norm=v_k_norm, sink=v_sink, loss=no_loss)
    small_all, = _comm_only(_Comm([(_pack_small(parts), GATHER)]), "gather_small")
    s_out = _small_adamw(_pack_small(weights), _pack_small(moms), _pack_small(vels), small_all, "adamw_small")
    s_g, s_d, s_m, s_v = [_unpack_small(a, weights) for a in s_out]
    loss = 0.5 * s_g["loss"][0] / d

    res = {}
    for n, w_, m_, v_ in (("w_in", w_in, m_w_in, v_w_in), ("w_out", w_out, m_w_out, v_w_out),
                          ("w_up", w_up, m_w_up, v_w_up), ("w_down", w_down, m_w_down, v_w_down)):
        res[n] = _adamw(w_, m_, v_, None, recvd[n], f"adamw_{n}")
    res["w_ada"] = _adamw(w_ada, m_w_ada, v_w_ada, dw_ada, None, "adamw_w_ada")

    order = ("c_ctx", "w_ada", "b_ada", "g_pre_mix", "g_post_mix", "g_pre_mlp", "g_post_mlp", "w_in", "q_norm",
             "k_norm", "sink", "w_out", "w_up", "w_down")
    outs = [loss, grad_x]
    for i, small in enumerate((s_g, s_d, s_m, s_v)):
        outs += [small[n] if n in small else res[n][i] for n in order]
    return tuple(outs)
```

```python
import functools

import jax
import jax.numpy as jnp
import numpy as np
from jax import lax
from jax.experimental import pallas as pl
from jax.experimental.pallas import tpu as pltpu

F32 = jnp.float32
BF16 = jnp.bfloat16

HEAD_DIM = 64
GROUP = 4
N_HG = 4
Q_WIDTH = GROUP * HEAD_DIM
WINDOW = 128
GRID_W = 64
ROPE_THETA = 10000.0
EPS = 1e-6
NEG_BIG = -1e30
N_MOD = 6
MOD_ROWS = 8
TM = 256
N_DEV = 8
ADA_ROWS = 8
VMEM_LIMIT = 56 * 1024 * 1024

ADAM_LR = 0.001
ADAM_B1 = 0.9
ADAM_B2 = 0.999
ADAM_EPS = 1e-08
ADAM_WD = 0.01
ADAM_STEP = 10


def _params(sem=None):
    kw = dict(vmem_limit_bytes=VMEM_LIMIT)
    if sem is not None:
        kw["dimension_semantics"] = sem
    return pltpu.CompilerParams(**kw)


def _my_index():
    return 4 * lax.axis_index("x") + 2 * lax.axis_index("y") + lax.axis_index("c")


def _peer(k):
    x, y, c = lax.axis_index("x"), lax.axis_index("y"), lax.axis_index("c")
    kx, ky, kc = (k >> 2) & 1, (k >> 1) & 1, k & 1
    px = (1 - x) if kx else x
    py = (1 - y) if ky else y
    pc = (1 - c) if kc else c
    return (px, py, pc), 4 * px + 2 * py + pc


GATHER, GATHER_VIA_SIBLING, TO_OWNER, TO_OWNER_XOR = "gather", "gather_via_sibling", "to_owner", "to_owner_xor"


class _Comm:
    def __init__(self, items):
        self.items = list(items)
        self.arrays = [a for a, _ in self.items]

    def out_shapes(self):
        return [jax.ShapeDtypeStruct(((N_DEV,) + a.shape) if kind in (GATHER, GATHER_VIA_SIBLING) else a.shape, a.dtype)
                for a, kind in self.items]

    def sem_shapes(self):
        n = len(self.items) * N_DEV
        return [pltpu.SemaphoreType.DMA((n,)), pltpu.SemaphoreType.DMA((n,))]

    def _two_level(self, x_ref, o_ref, send_sems, recv_sems, base):
        x, y, c = lax.axis_index("x"), lax.axis_index("y"), lax.axis_index("c")
        me, sibling = (x, y, c), (x, y, 1 - c)
        chips = [(1 - x, y), (x, 1 - y), (1 - x, 1 - y)]
        block = lambda p: o_ref.at[4 * p[0] + 2 * p[1] + p[2]]

        def copy(k, owner, to, src=None):
            return pltpu.make_async_remote_copy(
                src_ref=block(owner) if src is None else src, dst_ref=block(owner), send_sem=send_sems.at[base + k],
                recv_sem=recv_sems.at[base + k], device_id=to, device_id_type=pl.DeviceIdType.MESH)

        first = [copy(1, me, sibling, src=x_ref)] + [copy(2 + j, me, (*chip, c), src=x_ref) for j, chip in enumerate(chips)]
        relay = [(copy(2 + j, (*chip, c), me), copy(5 + j, (*chip, c), sibling)) for j, chip in enumerate(chips)]
        last = [copy(1, sibling, me)] + [copy(5 + j, (*chip, 1 - c), me) for j, chip in enumerate(chips)]
        return first, relay, last

    def _copies(self, in_refs, out_refs, send_sems, recv_sems):
        me = _my_index()
        local, remote = [], []
        for i, ((_, kind), x_ref, o_ref) in enumerate(zip(self.items, in_refs, out_refs)):
            base = i * N_DEV
            if kind == GATHER_VIA_SIBLING:
                local.append(pltpu.make_async_copy(x_ref, o_ref.at[me], send_sems.at[base]))
                continue
            own_src = x_ref if kind == GATHER else x_ref.at[me]
            own_dst = o_ref.at[0] if kind == TO_OWNER_XOR else o_ref.at[me]
            local.append(pltpu.make_async_copy(own_src, own_dst, send_sems.at[base]))
            for k in range(1, N_DEV):
                peer, pidx = _peer(k)
                remote.append(pltpu.make_async_remote_copy(
                    src_ref=x_ref if kind == GATHER else x_ref.at[pidx],
                    dst_ref=o_ref.at[k] if kind == TO_OWNER_XOR else o_ref.at[me],
                    send_sem=send_sems.at[base + k], recv_sem=recv_sems.at[base + k],
                    device_id=peer, device_id_type=pl.DeviceIdType.MESH))
        return local, remote

    def start(self, in_refs, out_refs, send_sems, recv_sems):
        local, remote = self._copies(in_refs, out_refs, send_sems, recv_sems)
        for cp in local + remote:
            cp.start()
        for i, ((_, kind), x_ref, o_ref) in enumerate(zip(self.items, in_refs, out_refs)):
            if kind == GATHER_VIA_SIBLING:
                for cp in self._two_level(x_ref, o_ref, send_sems, recv_sems, i * N_DEV)[0]:
                    cp.start()

    def relay(self, in_refs, out_refs, send_sems, recv_sems):
        for i, ((_, kind), x_ref, o_ref) in enumerate(zip(self.items, in_refs, out_refs)):
            if kind == GATHER_VIA_SIBLING:
                for arrival, onward in self._two_level(x_ref, o_ref, send_sems, recv_sems, i * N_DEV)[1]:
                    arrival.wait_recv()
                    onward.start()

    def wait(self, in_refs, out_refs, send_sems, recv_sems, relayed=False):
        local, remote = self._copies(in_refs, out_refs, send_sems, recv_sems)
        for cp in remote:
            cp.wait_recv()
        for cp in remote:
            cp.wait_send()
        if not relayed:
            self.relay(in_refs, out_refs, send_sems, recv_sems)
        for i, ((_, kind), x_ref, o_ref) in enumerate(zip(self.items, in_refs, out_refs)):
            if kind == GATHER_VIA_SIBLING:
                first, relay, last = self._two_level(x_ref, o_ref, send_sems, recv_sems, i * N_DEV)
                for cp in last:
                    cp.wait_recv()
                for cp in first + [onward for _, onward in relay]:
                    cp.wait_send()
        for cp in local:
            cp.wait()


def _call(body, *, name, grid, in_specs, out_specs, out_shape, args, scratch_shapes=(), prefetch=0, sem=None,
          comm=None):
    single = not isinstance(out_shape, (list, tuple))
    out_shape = [out_shape] if single else list(out_shape)
    out_specs = [out_specs] if single else list(out_specs)
    in_specs, scratch_shapes, args = list(in_specs), list(scratch_shapes), list(args)
    n_in, n_out = len(in_specs), len(out_shape)
    if comm is not None:
        nc = len(comm.arrays)
        hbm = pl.BlockSpec(memory_space=pl.ANY)
        inner = body

        def body(*refs):
            pre, r = refs[:prefetch], refs[prefetch:]
            ins, cin = r[:n_in], r[n_in:n_in + nc]
            outs, cout = r[n_in + nc:n_in + nc + n_out], r[n_in + nc + n_out:n_in + 2 * nc + n_out]
            scr, sems = r[n_in + 2 * nc + n_out:len(r) - 2], r[len(r) - 2:]
            ids = [pl.program_id(i) for i in range(len(grid))]

            def when(flags, fn):
                if flags:
                    pl.when(functools.reduce(jnp.logical_and, flags))(fn)
                else:
                    fn()

            when([i == 0 for i in ids], lambda: comm.start(cin, cout, *sems))
            inner(*pre, *ins, *outs, *scr)
            halfway = len(grid) > 1 and grid[0] > 1
            if halfway:
                when([i == (grid[0] // 2 if a == 0 else 0) for a, i in enumerate(ids)],
                     lambda: comm.relay(cin, cout, *sems))
            when([i == n - 1 for i, n in zip(ids, grid)], lambda: comm.wait(cin, cout, *sems, relayed=halfway))

        in_specs += [hbm] * nc
        out_specs += [hbm] * nc
        out_shape += comm.out_shapes()
        scratch_shapes += comm.sem_shapes()
        args += comm.arrays
        sem = ("arbitrary",) * len(grid)
    kw = dict(name=name, out_shape=out_shape, compiler_params=_params(sem if grid else None))
    if prefetch:
        kw["grid_spec"] = pltpu.PrefetchScalarGridSpec(
            num_scalar_prefetch=prefetch, grid=grid, in_specs=in_specs, out_specs=out_specs,
            scratch_shapes=scratch_shapes)
    else:
        kw.update(in_specs=in_specs, out_specs=out_specs, scratch_shapes=scratch_shapes)
        if grid:
            kw["grid"] = grid
    res = list(pl.pallas_call(body, **kw)(*args))
    outs = res[:n_out]
    return (outs[0] if single else outs), res[n_out:]


def _comm_only(comm, name):
    return _call(lambda: None, name=name, grid=(), in_specs=[], out_specs=[], out_shape=[], args=[], comm=comm)[1]


def _mm(a, b, *, name, ta=False, b_mode="nn", out_mode="plain", out_dtype=F32, tm=512, tn=512, tk=512,
        epilogue=None, extra=None, comm=None):
    if ta:
        kdim, m = a.shape
    else:
        m, kdim = a.shape
    if b_mode == "nn":
        n = b.shape[1]
    elif b_mode == "nt":
        n = b.shape[0]
    elif b_mode == "nn_slots":
        n = b.shape[0] * b.shape[2]
        tn = b.shape[2]
    else:
        n = b.shape[1]
        tk = b.shape[2]
    if out_mode == "slots":
        tn = n // N_DEV
    tm, tn, tk = min(tm, m), min(tn, n), min(tk, kdim)
    assert m % tm == 0 and n % tn == 0 and kdim % tk == 0, (name, m, n, kdim, tm, tn, tk)
    nk = kdim // tk

    a_spec = pl.BlockSpec((tk, tm), lambda i, j, k: (k, i)) if ta else pl.BlockSpec((tm, tk), lambda i, j, k: (i, k))
    if b_mode == "nn":
        b_spec = pl.BlockSpec((tk, tn), lambda i, j, k: (k, j))
    elif b_mode == "nt":
        b_spec = pl.BlockSpec((tn, tk), lambda i, j, k: (j, k))
    elif b_mode == "nn_slots":
        b_spec = pl.BlockSpec((None, tk, tn), lambda i, j, k: (j, k, 0))
    else:
        b_spec = pl.BlockSpec((None, tn, tk), lambda i, j, k: (k, j, 0))
    tb = b_mode in ("nt", "nt_slots")
    if out_mode == "plain":
        o_shape, o_spec = (m, n), pl.BlockSpec((tm, tn), lambda i, j, k: (i, j))
    else:
        o_shape, o_spec = (N_DEV, m, tn), pl.BlockSpec((None, tm, tn), lambda i, j, k: (j, i, 0))
    dims = (((0 if ta else 1,), (1 if tb else 0,)), ((), ()))

    in_specs = [a_spec, b_spec]
    args = [a, b]
    if epilogue == "relu2_bwd":
        in_specs.append(pl.BlockSpec((tm, tn), lambda i, j, k: (i, j)))
        args.append(extra)
    if epilogue == "relu2":
        out_shape = [jax.ShapeDtypeStruct(o_shape, BF16), jax.ShapeDtypeStruct((n, m), BF16)]
        out_specs = [o_spec, pl.BlockSpec((tn, tm), lambda i, j, k: (j, i))]
    else:
        out_shape = jax.ShapeDtypeStruct(o_shape, out_dtype)
        out_specs = o_spec

    def finish(refs, acc):
        if epilogue == "relu2":
            r = jnp.maximum(acc, 0.0)
            r2 = (r * r).astype(BF16)
            refs[2][...] = r2
            refs[3][...] = r2.T
        elif epilogue == "relu2_bwd":
            refs[3][...] = (acc * (2.0 * jnp.sqrt(refs[2][...]).astype(F32))).astype(out_dtype)
        else:
            refs[2][...] = acc.astype(out_dtype)

    def body(*refs):
        part = lax.dot_general(refs[0][...], refs[1][...], dims, preferred_element_type=F32)
        if nk == 1:
            finish(refs, part)
            return
        acc_ref = refs[-1]
        k = pl.program_id(2)

        @pl.when(k == 0)
        def _():
            acc_ref[...] = part

        @pl.when(jnp.logical_and(k > 0, k < nk - 1))
        def _():
            acc_ref[...] += part

        @pl.when(k == nk - 1)
        def _():
            finish(refs, acc_ref[...] + part)

    outs, couts = _call(
        body, name=name, grid=(m // tm, n // tn, nk), in_specs=in_specs, out_specs=out_specs, out_shape=out_shape,
        args=args, scratch_shapes=[] if nk == 1 else [pltpu.VMEM((tm, tn), F32)],
        sem=("parallel", "parallel", "arbitrary"), comm=comm)
    return outs if comm is None else (outs, couts)


def _row_specs(d, nt):
    row = pl.BlockSpec((TM, d), lambda t: (t, 0))
    vec = pl.BlockSpec((1, d), lambda t: (0, 0))
    mod = pl.BlockSpec((None, None, MOD_ROWS, d), lambda t: (t // nt, jnp.minimum(t % nt, 1), 0, 0))
    return row, vec, mod


def _rms(x):
    r = lax.rsqrt(jnp.mean(x * x, axis=1, keepdims=True) + EPS)
    return r, x * r


def _token_stream(ctx, x, nt, name, comm=None):
    bl, c_rows, d = ctx.shape
    m = bl * nt * TM

    def body(ctx_ref, x_ref, h_ref):
        t = pl.program_id(0)

        @pl.when(t % nt == 0)
        def _():
            h_ref[...] = ctx_ref[...]

        @pl.when(t % nt != 0)
        def _():
            h_ref[...] = x_ref[...]

    outs, couts = _call(
        body, name=name, grid=(m // TM,),
        in_specs=[pl.BlockSpec((None, TM, d), lambda t: (t // nt, 0, 0)),
                  pl.BlockSpec((None, TM, d), lambda t: (t // nt, jnp.maximum(t % nt, 1) - 1, 0))],
        out_specs=pl.BlockSpec((TM, d), lambda t: (t, 0)), out_shape=jax.ShapeDtypeStruct((m, d), F32),
        args=[ctx, x], sem=("parallel",), comm=comm)
    return outs if comm is None else (outs, couts)


def _norm_mod_fwd(h, g, modsel, i_sh, i_sc, nt, name, comm=None):
    m, d = h.shape
    row, vec, mod = _row_specs(d, nt)

    def body(h_ref, g_ref, ms_ref, u_ref, ut_ref):
        _, xh = _rms(h_ref[...])
        ms = ms_ref[...]
        u = (xh * g_ref[...] * (1.0 + ms[i_sc:i_sc + 1]) + ms[i_sh:i_sh + 1]).astype(BF16)
        u_ref[...] = u
        ut_ref[...] = u.T

    outs, couts = _call(
        body, name=name, grid=(m // TM,), in_specs=[row, vec, mod],
        out_specs=[row, pl.BlockSpec((d, TM), lambda t: (0, t))],
        out_shape=[jax.ShapeDtypeStruct((m, d), BF16), jax.ShapeDtypeStruct((d, m), BF16)],
        args=[h, g, modsel], sem=("parallel",), comm=comm)
    return outs if comm is None else (outs, couts)


def _acc_rows(t, nt, dvec_ref, rows):
    first = (t % nt) <= 1

    @pl.when(first)
    def _():
        dvec_ref[...] = rows

    @pl.when(jnp.logical_not(first))
    def _():
        dvec_ref[...] += rows


def _norm_mod_bwd(h, g, modsel, du, dh_in, i_sh, i_sc, nt, name, comm=None, latent_only=False):
    m, d = h.shape
    row, vec, mod = _row_specs(d, nt)
    dh_rows, dh_spec = m, row
    if latent_only:
        dh_rows = m // nt * (nt - 1)
        dh_spec = pl.BlockSpec((TM, d), lambda t: ((t // nt) * (nt - 1) + jnp.maximum(t % nt, 1) - 1, 0))

    def body(h_ref, g_ref, ms_ref, du_ref, dhi_ref, dh_ref, dvec_ref):
        t = pl.program_id(0)
        r, xh = _rms(h_ref[...])
        g_ = g_ref[...]
        ms = ms_ref[...]
        du_ = du_ref[...]
        y = xh * g_
        dy = du_ * (1.0 + ms[i_sc:i_sc + 1])
        dxh = dy * g_
        dx = r * (dxh - xh * jnp.mean(dxh * xh, axis=1, keepdims=True))
        dh_ref[...] = dhi_ref[...] + dx
        rows = jnp.concatenate([
            jnp.sum(du_, axis=0, keepdims=True), jnp.sum(du_ * y, axis=0, keepdims=True),
            jnp.sum(dy * xh, axis=0, keepdims=True), jnp.zeros((MOD_ROWS - 3, d), F32)], axis=0)
        _acc_rows(t, nt, dvec_ref, rows)

    outs, couts = _call(
        body, name=name, grid=(m // TM,), in_specs=[row, vec, mod, row, row], out_specs=[dh_spec, mod],
        out_shape=[jax.ShapeDtypeStruct((dh_rows, d), F32), jax.ShapeDtypeStruct(modsel.shape, F32)],
        args=[h, g, modsel, du, dh_in], sem=("arbitrary",), comm=comm)
    return outs if comm is None else (outs, couts)


def _gate_rows(dh_, xh, g_, gate):
    dy = dh_ * gate
    return dy * g_, jnp.sum(dh_ * (xh * g_), axis=0, keepdims=True), jnp.sum(dy * xh, axis=0, keepdims=True)


def _gate_mod_fwd(h, z, g_post, ms_gate, i_g, g_pre, ms_mod, i_sh, i_sc, nt, name):
    m, d = h.shape
    row, vec, mod = _row_specs(d, nt)

    def body(h_ref, z_ref, gp_ref, msg_ref, gq_ref, msm_ref, hn_ref, u_ref, ut_ref):
        _, zh = _rms(z_ref[...])
        hn = h_ref[...] + msg_ref[...][i_g:i_g + 1] * (zh * gp_ref[...])
        hn_ref[...] = hn
        _, xh = _rms(hn)
        ms = msm_ref[...]
        u = (xh * gq_ref[...] * (1.0 + ms[i_sc:i_sc + 1]) + ms[i_sh:i_sh + 1]).astype(BF16)
        u_ref[...] = u
        ut_ref[...] = u.T

    return pl.pallas_call(
        body, name=name, grid=(m // TM,), in_specs=[row, row, vec, mod, vec, mod],
        out_specs=[row, row, pl.BlockSpec((d, TM), lambda t: (0, t))],
        out_shape=[jax.ShapeDtypeStruct((m, d), F32), jax.ShapeDtypeStruct((m, d), BF16),
                   jax.ShapeDtypeStruct((d, m), BF16)],
        compiler_params=_params(("parallel",)),
    )(h, z, g_post, ms_gate, g_pre, ms_mod)


def _mod_gate_bwd(h, g_pre, ms_mod, i_sh, i_sc, du, dh_in, z, g_post, ms_gate, i_g, nt, name):
    m, d = h.shape
    row, vec, mod = _row_specs(d, nt)

    def body(h_ref, gq_ref, msm_ref, du_ref, dhi_ref, z_ref, gp_ref, msg_ref, dh_ref, dvm_ref, dz_ref, dvg_ref):
        t = pl.program_id(0)
        r, xh = _rms(h_ref[...])
        gq = gq_ref[...]
        ms = msm_ref[...]
        du_ = du_ref[...]
        dy = du_ * (1.0 + ms[i_sc:i_sc + 1])
        dxh = dy * gq
        dh_ = dhi_ref[...] + r * (dxh - xh * jnp.mean(dxh * xh, axis=1, keepdims=True))
        dh_ref[...] = dh_
        _acc_rows(t, nt, dvm_ref, jnp.concatenate([
            jnp.sum(du_, axis=0, keepdims=True), jnp.sum(du_ * (xh * gq), axis=0, keepdims=True),
            jnp.sum(dy * xh, axis=0, keepdims=True), jnp.zeros((MOD_ROWS - 3, d), F32)], axis=0))
        rz, zh = _rms(z_ref[...])
        dzh, d_gate, d_gp = _gate_rows(dh_, zh, gp_ref[...], msg_ref[...][i_g:i_g + 1])
        dz_ref[...] = (rz * (dzh - zh * jnp.mean(dzh * zh, axis=1, keepdims=True))).astype(BF16)
        _acc_rows(t, nt, dvg_ref, jnp.concatenate([d_gate, d_gp, jnp.zeros((MOD_ROWS - 2, d), F32)], axis=0))

    return pl.pallas_call(
        body, name=name, grid=(m // TM,), in_specs=[row, vec, mod, row, row, row, vec, mod],
        out_specs=[row, mod, row, mod],
        out_shape=[jax.ShapeDtypeStruct((m, d), F32), jax.ShapeDtypeStruct(ms_mod.shape, F32),
                   jax.ShapeDtypeStruct((m, d), BF16), jax.ShapeDtypeStruct(ms_gate.shape, F32)],
        compiler_params=_params(("arbitrary",)),
    )(h, g_pre, ms_mod, du, dh_in, z, g_post, ms_gate)


def _gate_loss_bwd(h, z, g_post, modsel, i_g, target, nt, name):
    m, d = h.shape
    row, vec, mod = _row_specs(d, nt)
    ntl = nt - 1
    tgt = pl.BlockSpec((TM, d), lambda t: ((t // nt) * ntl + jnp.maximum(t % nt, 1) - 1, 0))
    acc = pl.BlockSpec((8, 128), lambda t: (0, 0))

    def body(h_ref, z_ref, gp_ref, ms_ref, t_ref, dh_ref, dz_ref, dvg_ref, ss_ref):
        t = pl.program_id(0)

        @pl.when(t == 0)
        def _():
            ss_ref[...] = jnp.zeros_like(ss_ref)

        latent = (t % nt != 0).astype(F32)
        rz, zh = _rms(z_ref[...])
        gp = gp_ref[...]
        gate = ms_ref[...][i_g:i_g + 1]
        e = h_ref[...] + gate * (zh * gp) - t_ref[...]
        ss_ref[...] += latent * jnp.sum(e * e)
        dh_ = e * (latent / d)
        dh_ref[...] = dh_
        dzh, d_gate, d_gp = _gate_rows(dh_, zh, gp, gate)
        dz_ref[...] = (rz * (dzh - zh * jnp.mean(dzh * zh, axis=1, keepdims=True))).astype(BF16)
        _acc_rows(t, nt, dvg_ref, jnp.concatenate([d_gate, d_gp, jnp.zeros((MOD_ROWS - 2, d), F32)], axis=0))

    return pl.pallas_call(
        body, name=name, grid=(m // TM,), in_specs=[row, row, vec, mod, tgt], out_specs=[row, row, mod, acc],
        out_shape=[jax.ShapeDtypeStruct((m, d), F32), jax.ShapeDtypeStruct((m, d), BF16),
                   jax.ShapeDtypeStruct(modsel.shape, F32), jax.ShapeDtypeStruct((8, 128), F32)],
        compiler_params=_params(("arbitrary",)),
    )(h, z, g_post, modsel, target)


QA, KA, VA, QB, KB, VB = 0, 512, 640, 768, 1280, 1408
PROJ_W = 1536
Q_SCALE = HEAD_DIM ** -0.5
LOG2E = 1.4426950408889634


def _swap16(x):
    lane = lax.broadcasted_iota(jnp.int32, x.shape, 1)
    n = x.shape[1]
    return jnp.where((lane % 32) < 16, pltpu.roll(x, n - 16, 1), pltpu.roll(x, 16, 1))


def _seg_mean(x, e):
    hi = x.astype(BF16)
    lo = (x - hi.astype(F32)).astype(BF16)
    w = e.shape[0]
    both = lambda a: jnp.dot(hi[:, a:a + w], e, preferred_element_type=F32) + jnp.dot(lo[:, a:a + w], e, preferred_element_type=F32)
    parts = [both(a) for a in range(0, x.shape[1], w)]
    return parts[0] if len(parts) == 1 else jnp.concatenate(parts, axis=1)


def _rope_tables(t_rows, c_rows):
    s = t_rows - c_rows
    row_ids = jnp.repeat(jnp.arange(s // GRID_W, dtype=jnp.int32), GRID_W).astype(F32)
    col_ids = jnp.tile(jnp.arange(GRID_W, dtype=jnp.int32), s // GRID_W).astype(F32)
    axis_dim = HEAD_DIM // 2
    inv = ROPE_THETA ** (-jnp.arange(0, axis_dim, 2, dtype=F32) / axis_dim)
    ang_r = row_ids[:, None] * inv[None, :]
    ang_c = col_ids[:, None] * inv[None, :]
    cos = jnp.concatenate([jnp.cos(ang_r), jnp.cos(ang_r), jnp.cos(ang_c), jnp.cos(ang_c)], axis=1)
    sin = jnp.concatenate([-jnp.sin(ang_r), jnp.sin(ang_r), -jnp.sin(ang_c), jnp.sin(ang_c)], axis=1)
    cos = jnp.concatenate([jnp.ones((c_rows, HEAD_DIM), F32), cos], axis=0)
    sin = jnp.concatenate([jnp.zeros((c_rows, HEAD_DIM), F32), sin], axis=0)
    return jnp.tile(cos, (1, 8)), jnp.tile(sin, (1, 8))


def _head_mean_matrix():
    i = np.arange(512)
    return jnp.asarray((i[:, None] // HEAD_DIM == i[None, :] // HEAD_DIM).astype(np.float32) / HEAD_DIM, dtype=BF16)


def _interleave_kv(k, v):
    return jnp.concatenate([k[:, :64], v[:, :64], k[:, 64:], v[:, 64:]], axis=1)


def _prep_fwd(proj, qn, kn, cos, sin, emat, nt, name):
    m = proj.shape[0]
    specs = [
        pl.BlockSpec((TM, PROJ_W), lambda t: (t, 0)),
        pl.BlockSpec((1, 512), lambda t: (0, 0)), pl.BlockSpec((1, 128), lambda t: (0, 0)),
        pl.BlockSpec((TM, 512), lambda t: (t % nt, 0)), pl.BlockSpec((TM, 512), lambda t: (t % nt, 0)),
        pl.BlockSpec((512, 512), lambda t: (0, 0)),
    ]

    def body(p_ref, qn_ref, kn_ref, cos_ref, sin_ref, e_ref, q_ref, kv_ref):
        cos_, sin_, e = cos_ref[...], sin_ref[...], e_ref[...]

        def rope(x, w):
            return x * cos_[:, :w] + _swap16(x) * sin_[:, :w]

        def norm(x, g, w):
            return x * lax.rsqrt(_seg_mean(x * x, e[:min(w, 256), :min(w, 256)]) + EPS) * g

        qa = rope(norm(p_ref[:, QA:QA + 512], qn_ref[...], 512), 512)
        qb = rope(p_ref[:, QB:QB + 512], 512)
        q_ref[:, 0:512] = (qa * (Q_SCALE * LOG2E)).astype(BF16)
        q_ref[:, 512:1024] = (qb * (Q_SCALE * LOG2E)).astype(BF16)
        ka = rope(norm(p_ref[:, KA:KA + 128], kn_ref[...], 128), 128)
        kb = rope(p_ref[:, KB:KB + 128], 128)
        kv_ref[:, 0:256] = _interleave_kv(ka, p_ref[:, VA:VA + 128]).astype(BF16)
        kv_ref[:, 256:512] = _interleave_kv(kb, p_ref[:, VB:VB + 128]).astype(BF16)

    return pl.pallas_call(
        body, name=name, grid=(m // TM,), in_specs=specs,
        out_specs=[pl.BlockSpec((TM, 1024), lambda t: (t, 0)), pl.BlockSpec((TM, 512), lambda t: (t, 0))],
        out_shape=[jax.ShapeDtypeStruct((m, 1024), BF16), jax.ShapeDtypeStruct((m, 512), BF16)],
        compiler_params=_params(("parallel",)),
    )(proj, qn, kn, cos, sin, emat)


def _prep_bwd(proj, dq, dkv, qn, kn, cos, sin, emat, nt, name, comm=None):
    m = proj.shape[0]
    specs = [
        pl.BlockSpec((TM, PROJ_W), lambda t: (t, 0)),
        pl.BlockSpec((TM, 1024), lambda t: (t, 0)), pl.BlockSpec((TM, 512), lambda t: (t, 0)),
        pl.BlockSpec((1, 512), lambda t: (0, 0)), pl.BlockSpec((1, 128), lambda t: (0, 0)),
        pl.BlockSpec((TM, 512), lambda t: (t % nt, 0)), pl.BlockSpec((TM, 512), lambda t: (t % nt, 0)),
        pl.BlockSpec((512, 512), lambda t: (0, 0)),
    ]

    def body(p_ref, dq_ref, dkv_ref, qn_ref, kn_ref, cos_ref, sin_ref, e_ref, dp_ref, dqn_ref, dkn_ref):
        t = pl.program_id(0)
        cos_, sin_, e = cos_ref[...], sin_ref[...], e_ref[...]

        @pl.when(t == 0)
        def _():
            dqn_ref[...] = jnp.zeros_like(dqn_ref)
            dkn_ref[...] = jnp.zeros_like(dkn_ref)

        def unrope(dy, w):
            return dy * cos_[:, :w] + _swap16(dy * sin_[:, :w])

        def norm_bwd(x, g, dy, w):
            r = lax.rsqrt(_seg_mean(x * x, e[:min(w, 256), :min(w, 256)]) + EPS)
            xh = x * r
            dxh = dy * g
            dx = r * (dxh - xh * _seg_mean(dxh * xh, e[:min(w, 256), :min(w, 256)]))
            return dx, jnp.sum(dy * xh, axis=0, keepdims=True)

        dqa, dgq = norm_bwd(p_ref[:, QA:QA + 512], qn_ref[...], unrope(dq_ref[:, 0:512] * Q_SCALE, 512), 512)
        dp_ref[:, QA:QA + 512] = dqa.astype(BF16)
        dp_ref[:, QB:QB + 512] = unrope(dq_ref[:, 512:1024] * Q_SCALE, 512).astype(BF16)
        da = dkv_ref[:, 0:256]
        db = dkv_ref[:, 256:512]
        dka = jnp.concatenate([da[:, 0:64], da[:, 128:192]], axis=1)
        dva = jnp.concatenate([da[:, 64:128], da[:, 192:256]], axis=1)
        dkb = jnp.concatenate([db[:, 0:64], db[:, 128:192]], axis=1)
        dvb = jnp.concatenate([db[:, 64:128], db[:, 192:256]], axis=1)
        dka, dgk = norm_bwd(p_ref[:, KA:KA + 128], kn_ref[...], unrope(dka, 128), 128)
        dp_ref[:, KA:KA + 128] = dka.astype(BF16)
        dp_ref[:, VA:VA + 128] = dva.astype(BF16)
        dp_ref[:, KB:KB + 128] = unrope(dkb, 128).astype(BF16)
        dp_ref[:, VB:VB + 128] = dvb.astype(BF16)
        dqn_ref[0:1, :] += dgq
        dkn_ref[0:1, :] += dgk

    outs, couts = _call(
        body, name=name, grid=(m // TM,), in_specs=specs,
        out_specs=[pl.BlockSpec((TM, PROJ_W), lambda t: (t, 0)), pl.BlockSpec((8, 512), lambda t: (0, 0)),
                   pl.BlockSpec((8, 128), lambda t: (0, 0))],
        out_shape=[jax.ShapeDtypeStruct((m, PROJ_W), BF16), jax.ShapeDtypeStruct((8, 512), F32),
                   jax.ShapeDtypeStruct((8, 128), F32)],
        args=[proj, dq, dkv, qn, kn, cos, sin, emat], sem=("arbitrary",), comm=comm)
    return outs if comm is None else (outs, couts)


PAIR = 2
PAIR_HEADS = [(j, g) for j in range(PAIR) for g in range(GROUP)]
LOOKAHEAD = 4
LOOKAHEAD_BWD = 2


def _attn_case(t, pr, kv_ref, c_rows, t_rows, fn, keys_first=False):
    wl = TM + 2 * WINDOW
    kvd = lambda a, n: kv_ref[pl.ds(a, n), :]
    dense = pr == 0
    ctx = t == 0

    @pl.when(jnp.logical_and(dense, ctx))
    def _():
        fn(kvd(0, c_rows), None, False, [(0, c_rows)])

    @pl.when(jnp.logical_and(dense, jnp.logical_not(ctx)))
    def _():
        fn(kvd(0, t_rows), None, False, [(0, t_rows)])

    @pl.when(jnp.logical_and(jnp.logical_not(dense), ctx))
    def _():
        fn(kvd(0, c_rows), None, True, [(0, c_rows)])

    @pl.when(jnp.logical_and(jnp.logical_not(dense), jnp.logical_not(ctx)))
    def _():
        start = pl.multiple_of(jnp.minimum(c_rows + (t - 1) * TM - WINDOW, t_rows - wl), 128)
        kv = jnp.concatenate([kvd(0, c_rows), kvd(start, wl)], axis=0)
        shape = (c_rows + wl, TM) if keys_first else (TM, c_rows + wl)
        q_i = lax.broadcasted_iota(jnp.int32, shape, 1 if keys_first else 0)
        k_i = lax.broadcasted_iota(jnp.int32, shape, 0 if keys_first else 1)
        qpos = (t - 1) * TM + q_i
        kpos = start - 2 * c_rows + k_i
        mask = jnp.logical_or(k_i < c_rows, jnp.logical_and(jnp.abs(kpos - qpos) <= WINDOW, kpos >= 0))
        fn(kv, mask, True, [(0, c_rows), (start, wl)])


def _keys_values(kv, j):
    return kv[:, j * 128:j * 128 + 64], kv[:, j * 128 + 64:(j + 1) * 128]


def _head_cols(ref, j, g):
    a = j * Q_WIDTH + g * HEAD_DIM
    return ref[:, a:a + HEAD_DIM]


def _head_columns(cols):
    lane = lax.broadcasted_iota(jnp.int32, (TM, 128), 1)
    out = jnp.zeros((TM, 128), F32)
    for g, col in enumerate(cols):
        out = jnp.where(lane == g, col, out)
    return out


def _attn_specs(t_rows):
    nt = t_rows // TM
    q_spec = pl.BlockSpec((None, TM, PAIR * Q_WIDTH), lambda b, pr, t, s: (b, t, pr))
    kv_spec = pl.BlockSpec((None, t_rows, PAIR * 128), lambda b, pr, t, s: (b, 0, pr))
    lse_spec = pl.BlockSpec((PAIR, TM, 128), lambda b, pr, t, s: (pr, b * nt + t, 0))
    return q_spec, kv_spec, lse_spec


def _attn_fwd(q_all, kv_all, sink8, c_rows, name, comm=None):
    bl, t_rows, _ = q_all.shape
    q_spec, kv_spec, lse_spec = _attn_specs(t_rows)

    def body(sink_ref, q_ref, kv_ref, o_ref, ot_ref, lse_ref):
        pr, t = pl.program_id(1), pl.program_id(2)

        def fn(kv, mask, use_sink, spans):
            outs, lses = [], []
            ks = [_keys_values(kv, j)[0] for j in range(PAIR)]
            v_ones = [jnp.concatenate([_keys_values(kv, j)[1], jnp.ones((kv.shape[0], 64), BF16)], axis=1)
                      for j in range(PAIR)]

            def scores(i):
                j, g = PAIR_HEADS[i]
                s = lax.dot_general(_head_cols(q_ref, j, g), ks[j], (((1,), (1,)), ((), ())),
                                    preferred_element_type=F32)
                return s if mask is None else jnp.where(mask, s, NEG_BIG)

            ahead = [scores(i) for i in range(LOOKAHEAD)]
            for i, (j, g) in enumerate(PAIR_HEADS):
                s = ahead.pop(0)
                if i + LOOKAHEAD < len(PAIR_HEADS):
                    ahead.append(scores(i + LOOKAHEAD))
                mx = jnp.max(s, axis=1, keepdims=True)
                if use_sink:
                    sink = sink_ref[j * GROUP + g] * LOG2E
                    mx = jnp.maximum(mx, sink)
                pv = jnp.dot(jnp.exp2(s - mx).astype(BF16), v_ones[j], preferred_element_type=F32)
                l = pv[:, 64:65]
                if use_sink:
                    l = l + jnp.exp2(sink - mx)
                outs.append(pv[:, :64] * (1.0 / l))
                lses.append(mx + jnp.log2(l))
            o = jnp.concatenate(outs, axis=1).astype(BF16)
            o_ref[...] = o
            ot_ref[...] = o.T
            for j in range(PAIR):
                lse_ref[j] = _head_columns(lses[j * GROUP:(j + 1) * GROUP])

        _attn_case(t, pr, kv_ref, c_rows, t_rows, fn)

    nt = t_rows // TM
    ot_spec = pl.BlockSpec((PAIR * Q_WIDTH, TM), lambda b, pr, t, s: (pr, b * nt + t))
    outs, couts = _call(
        body, name=name, grid=(bl, N_HG // PAIR, nt), in_specs=[q_spec, kv_spec], out_specs=[q_spec, ot_spec, lse_spec],
        out_shape=[jax.ShapeDtypeStruct(q_all.shape, BF16), jax.ShapeDtypeStruct((N_HG * Q_WIDTH, bl * t_rows), BF16),
                   jax.ShapeDtypeStruct((N_HG, bl * t_rows, 128), F32)],
        args=[sink8, q_all, kv_all], prefetch=1, sem=("parallel", "parallel", "arbitrary"), comm=comm)
    return outs if comm is None else (outs, couts)


def _attn_bwd(q_all, kv_all, do, o, lse, sink8, c_rows, name, comm=None):
    bl, t_rows, _ = q_all.shape
    q_spec, kv_spec, lse_spec = _attn_specs(t_rows)
    ds_spec = pl.BlockSpec((None, PAIR, 8, 128), lambda b, pr, t, s: (b, pr, 0, 0))

    def body(sink_ref, q_ref, kv_ref, do_ref, o_ref, lse_ref, dq_ref, dkv_ref, dsk_ref):
        pr, t = pl.program_id(1), pl.program_id(2)

        @pl.when(t == 0)
        def _():
            dkv_ref[...] = jnp.zeros_like(dkv_ref)
            dsk_ref[...] = jnp.zeros_like(dsk_ref)

        lse_rows = [lse_ref[j].T for j in range(PAIR)]
        dd_rows = [_head_columns([jnp.sum(_head_cols(do_ref, j, g).astype(F32) * _head_cols(o_ref, j, g).astype(F32),
                                          axis=1, keepdims=True) for g in range(GROUP)]).T for j in range(PAIR)]

        def fn(kv, mask, use_sink, spans):
            ks, vs = zip(*[_keys_values(kv, j) for j in range(PAIR)])
            k_ts = [k.T for k in ks]
            dq_t = []
            dsinks = [[] for _ in range(PAIR)]
            dks = [jnp.zeros(ks[0].shape, F32) for _ in range(PAIR)]
            dvs = [jnp.zeros(ks[0].shape, F32) for _ in range(PAIR)]

            def products(i):
                j, g = PAIR_HEADS[i]
                s = lax.dot_general(ks[j], _head_cols(q_ref, j, g), (((1,), (1,)), ((), ())),
                                    preferred_element_type=F32)
                dp = lax.dot_general(vs[j], _head_cols(do_ref, j, g), (((1,), (1,)), ((), ())),
                                     preferred_element_type=F32)
                return (s if mask is None else jnp.where(mask, s, NEG_BIG)), dp

            ahead = [products(i) for i in range(LOOKAHEAD_BWD)]
            for i, (j, g) in enumerate(PAIR_HEADS):
                q, do_g = _head_cols(q_ref, j, g), _head_cols(do_ref, j, g)
                lse_g, dd_g = lse_rows[j][g:g + 1, :], dd_rows[j][g:g + 1, :]
                s, dp = ahead.pop(0)
                if i + LOOKAHEAD_BWD < len(PAIR_HEADS):
                    ahead.append(products(i + LOOKAHEAD_BWD))
                pb = jnp.exp2(s - lse_g).astype(BF16)
                ds = (pb.astype(F32) * (dp - dd_g)).astype(BF16)
                dks[j] = dks[j] + jnp.dot(ds, q, preferred_element_type=F32)
                dvs[j] = dvs[j] + jnp.dot(pb, do_g, preferred_element_type=F32)
                dq_t.append(jnp.dot(k_ts[j], ds, preferred_element_type=F32))
                if use_sink:
                    p_sink = jnp.exp2(sink_ref[j * GROUP + g] * LOG2E - lse_g)
                    dsinks[j].append(jnp.broadcast_to(-jnp.sum(p_sink * dd_g, axis=1, keepdims=True), (1, 128)))
            dq_ref[...] = jnp.concatenate(dq_t, axis=0).T
            dkv = jnp.concatenate([a for j in range(PAIR) for a in (dks[j] * (1.0 / LOG2E), dvs[j])], axis=1)
            off = 0
            for start, size in spans:
                dkv_ref[pl.ds(start, size), :] += dkv[off:off + size]
                off += size
            if use_sink:
                for j in range(PAIR):
                    dsk_ref[j, 0:GROUP, :] += jnp.concatenate(dsinks[j], axis=0)

        _attn_case(t, pr, kv_ref, c_rows, t_rows, fn, keys_first=True)

    outs, couts = _call(
        body, name=name, grid=(bl, N_HG // PAIR, t_rows // TM), in_specs=[q_spec, kv_spec, q_spec, q_spec, lse_spec],
        out_specs=[q_spec, kv_spec, ds_spec],
        out_shape=[jax.ShapeDtypeStruct(q_all.shape, F32), jax.ShapeDtypeStruct(kv_all.shape, F32),
                   jax.ShapeDtypeStruct((bl, N_HG, 8, 128), F32)],
        args=[sink8, q_all, kv_all, do, o, lse], prefetch=1, sem=("parallel", "parallel", "arbitrary"), comm=comm)
    return outs if comm is None else (outs, couts)


def _silu(x):
    return x * jax.nn.sigmoid(x)


def _ada_fwd(c_rows, w_ada, b_cols, name, comm=None):
    nl, d, w = w_ada.shape
    r = c_rows.shape[0]

    def body(c_ref, w_ref, b_ref, o_ref):
        s = _silu(c_ref[...]).astype(BF16)
        o_ref[...] = jnp.dot(s, w_ref[...].astype(BF16), preferred_element_type=F32) + b_ref[...]

    outs, couts = _call(
        body, name=name, grid=(nl,),
        in_specs=[pl.BlockSpec((r, d), lambda l: (0, 0)), pl.BlockSpec((None, d, w), lambda l: (l, 0, 0)),
                  pl.BlockSpec((None, 1, w), lambda l: (l, 0, 0))],
        out_specs=pl.BlockSpec((None, r, w), lambda l: (l, 0, 0)),
        out_shape=jax.ShapeDtypeStruct((nl, r, w), F32), args=[c_rows, w_ada, b_cols], sem=("parallel",), comm=comm)
    return outs if comm is None else (outs, couts)


def _ada_bwd(c_rows, c_ctx, dmod, w_ada, name):
    nl, d, w = w_ada.shape
    r = c_rows.shape[0]

    def body(c_ref, cc_ref, g_ref, w_ref, dw_ref, dc_ref):
        l = pl.program_id(0)
        s = _silu(c_ref[...]).astype(BF16)
        gm = g_ref[...].astype(BF16)
        dw_ref[...] = lax.dot_general(s, gm, (((0,), (0,)), ((), ())), preferred_element_type=F32)
        ds = lax.dot_general(gm, w_ref[...].astype(BF16), (((1,), (1,)), ((), ())), preferred_element_type=F32)
        rows = lax.broadcasted_iota(jnp.int32, ds.shape, 0)
        dsc = jnp.sum(jnp.where(rows % ADA_ROWS == 2, ds, 0.0), axis=0, keepdims=True)
        x = cc_ref[...]
        sg = jax.nn.sigmoid(x)
        dcc = dsc * (sg * (1.0 + x * (1.0 - sg)))
        out = jnp.concatenate([dcc, jnp.zeros((7, d), F32)], axis=0)

        @pl.when(l == 0)
        def _():
            dc_ref[...] = out

        @pl.when(l != 0)
        def _():
            dc_ref[...] += out

    return pl.pallas_call(
        body, name=name, grid=(nl,),
        in_specs=[pl.BlockSpec((r, d), lambda l: (0, 0)), pl.BlockSpec((1, d), lambda l: (0, 0)),
                  pl.BlockSpec((None, r, w), lambda l: (l, 0, 0)), pl.BlockSpec((None, d, w), lambda l: (l, 0, 0))],
        out_specs=[pl.BlockSpec((None, d, w), lambda l: (l, 0, 0)), pl.BlockSpec((8, d), lambda l: (0, 0))],
        out_shape=[jax.ShapeDtypeStruct((nl, d, w), F32), jax.ShapeDtypeStruct((8, d), F32)],
        compiler_params=_params(("arbitrary",)),
    )(c_rows, c_ctx, dmod, w_ada)


def _adam_math(w, g, m, v):
    m = ADAM_B1 * m + (1.0 - ADAM_B1) * g
    v = ADAM_B2 * v + (1.0 - ADAM_B2) * (g * g)
    m_hat = m / (1.0 - ADAM_B1 ** ADAM_STEP)
    v_hat = v / (1.0 - ADAM_B2 ** ADAM_STEP)
    delta = -ADAM_LR * (m_hat / (jnp.sqrt(v_hat) + ADAM_EPS) + ADAM_WD * w)
    return delta, m, v


def _adamw(w, m, v, g_own, g_recv, name, rows=256):
    nl, r, c = w.shape
    tr = min(rows, r)
    spec = pl.BlockSpec((None, tr, c), lambda l, i: (l, i, 0))
    per_layer = g_own is None
    own = [] if per_layer else [g_own]
    recv = [] if g_recv is None else list(g_recv)
    n_i = r // tr

    def rows_of(li):
        return lambda l, i: jnp.where(l == li, i, jnp.where(l > li, n_i - 1, 0))

    in_specs = [spec] * (3 + len(own))
    in_specs += [pl.BlockSpec((N_DEV, tr, c), lambda l, i, f=rows_of(li): (0, f(l, i), 0)) for li in range(len(recv))]

    def body(*refs):
        w_ref, m_ref, v_ref = refs[:3]
        own_refs, recv_refs = refs[3:3 + len(own)], refs[3 + len(own):3 + len(own) + len(recv)]
        go_ref, d_ref, mo_ref, vo_ref = refs[-4:]

        def update(li):
            if recv:
                g = recv_refs[li][0].astype(F32)
                for k in range(1, N_DEV):
                    g = g + recv_refs[li][k].astype(F32)
            else:
                g = own_refs[0][...]
            delta, m_, v_ = _adam_math(w_ref[...], g, m_ref[...], v_ref[...])
            go_ref[...] = g
            d_ref[...] = delta
            mo_ref[...] = m_
            vo_ref[...] = v_

        if per_layer:
            for li in range(nl):
                pl.when(pl.program_id(0) == li)(functools.partial(update, li))
        else:
            update(0)

    return pl.pallas_call(
        body, name=name, grid=(nl, n_i), in_specs=in_specs, out_specs=[spec] * 4,
        out_shape=[jax.ShapeDtypeStruct(w.shape, F32)] * 4, compiler_params=_params(("parallel", "parallel")),
    )(w, m, v, *own, *recv)


def _small_adamw(w, m, v, g_all, name):
    def body(w_ref, m_ref, v_ref, g_ref, go_ref, d_ref, mo_ref, vo_ref):
        g = g_ref[0]
        for k in range(1, N_DEV):
            g = g + g_ref[k]
        delta, m_, v_ = _adam_math(w_ref[...], g, m_ref[...], v_ref[...])
        go_ref[...] = g
        d_ref[...] = delta
        mo_ref[...] = m_
        vo_ref[...] = v_

    return pl.pallas_call(
        body, name=name, out_shape=[jax.ShapeDtypeStruct(w.shape, F32)] * 4, compiler_params=_params(),
    )(w, m, v, g_all)


SMALL = ("c_ctx", "b_ada", "g_pre_mix", "g_post_mix", "g_pre_mlp", "g_post_mlp", "q_norm", "k_norm", "sink", "loss")


def _pack_small(parts):
    flat = jnp.concatenate([parts[n].reshape(-1) for n in SMALL])
    rows = -(-flat.shape[0] // 1024) * 8
    return jnp.pad(flat, (0, rows * 128 - flat.shape[0])).reshape(rows, 128)


def _unpack_small(packed, like):
    flat = packed.reshape(-1)
    out, off = {}, 0
    for n in SMALL:
        size = int(np.prod(like[n].shape))
        out[n] = flat[off:off + size].reshape(like[n].shape)
        off += size
    return out


def kernel(x, c, ctx, c_ctx, w_ada, b_ada, g_pre_mix, g_post_mix, g_pre_mlp, g_post_mlp, w_in, q_norm, k_norm, sink, w_out, w_up, w_down, loss_target, m_c_ctx, m_w_ada, m_b_ada, m_g_pre_mix, m_g_post_mix, m_g_pre_mlp, m_g_post_mlp, m_w_in, m_q_norm, m_k_norm, m_sink, m_w_out, m_w_up, m_w_down, v_c_ctx, v_w_ada, v_b_ada, v_g_pre_mix, v_g_post_mix, v_g_pre_mlp, v_g_post_mlp, v_w_in, v_q_norm, v_k_norm, v_sink, v_w_out, v_w_up, v_w_down):
    bl, s_rows, d = x.shape
    c_rows = ctx.shape[1]
    assert c_rows == TM and s_rows % TM == 0 and bl == 2
    t_rows = c_rows + s_rows
    nt = t_rows // TM
    m_rows = bl * t_rows
    nl = w_in.shape[0]
    ada_w = w_ada.shape[2]
    d_ff = w_up.shape[2] * N_DEV
    me = _my_index()

    shard = lambda w_, l: w_[l].astype(BF16)
    c_pad = jnp.concatenate([c, c_ctx[None, :], jnp.zeros((ADA_ROWS - bl - 1, d), F32)], axis=0)
    gathered = {0: {}}
    h, (c_all, gathered[0]["w_in"]) = _token_stream(
        ctx, x, nt, "token_stream", comm=_Comm([(c_pad, GATHER_VIA_SIBLING), (shard(w_in, 0), GATHER_VIA_SIBLING)]))
    c_all = c_all.reshape(N_DEV * ADA_ROWS, d)

    def layer_weights(l):
        g_ = gathered[l]
        w_out_f = g_["w_out"].reshape(-1, d)
        w_down_f = g_["w_down"].reshape(d_ff, d)
        return dict(
            w_in_f=g_["w_in"].transpose(1, 0, 2).reshape(d, PROJ_W), w_out_f=w_out_f,
            w_up_s=g_["w_up"], w_up_t=g_["w_up"].transpose(0, 2, 1).reshape(d_ff, d), w_down_f=w_down_f)

    big = dict(tm=2304, tn=512)
    deep = dict(tm=1536, tn=512, tk=d_ff)
    wide = dict(tm=1024, tn=512, tk=m_rows)

    b_cols = lax.dynamic_slice(b_ada, (0, me * ada_w), (nl, ada_w))[:, None, :]
    mod_cols = _ada_fwd(c_all, w_ada, b_cols, "ada_fwd")
    mod_slots = mod_cols.reshape(nl, N_DEV, ADA_ROWS, ada_w).transpose(1, 0, 2, 3)
    mod_g, = _comm_only(_Comm([(mod_slots, TO_OWNER)]), "exchange_mod")
    mine = mod_g.transpose(1, 2, 0, 3).reshape(nl, ADA_ROWS, N_MOD, d)
    pad = jnp.zeros((bl, 2, MOD_ROWS - N_MOD, d), F32)
    modsel = [jnp.concatenate([jnp.stack([jnp.broadcast_to(mine[l, bl], (bl, N_MOD, d)), mine[l, :bl]], axis=1), pad],
                              axis=2) for l in range(nl)]

    cos, sin = _rope_tables(t_rows, c_rows)
    emat = _head_mean_matrix()
    row = lambda a: a[None, :]
    qn = [jnp.tile(q_norm[l], 8)[None, :] for l in range(nl)]
    kn = [jnp.tile(k_norm[l], 2)[None, :] for l in range(nl)]

    target = loss_target.reshape(bl * s_rows, d)
    saved = []
    weights_of = {}
    (u, u_t), (gathered[0]["w_out"],) = _norm_mod_fwd(
        h, row(g_pre_mix[0]), modsel[0], 0, 1, nt, "mix_mod_fwd0", comm=_Comm([(shard(w_out, 0), GATHER_VIA_SIBLING)]))
    for l in range(nl):
        w_in_f = gathered[l]["w_in"].transpose(1, 0, 2).reshape(d, PROJ_W)
        proj = _mm(u, w_in_f, name=f"mm_in{l}", tk=d, **big)
        q_all, kv_all = _prep_fwd(proj, qn[l], kn[l], cos, sin, emat, nt, f"prep_fwd{l}")
        (o, o_t, lse), (w_up_g, w_down_g) = _attn_fwd(
            q_all.reshape(bl, t_rows, 1024), kv_all.reshape(bl, t_rows, 512), sink[l], c_rows, f"attn_fwd{l}",
            comm=_Comm([(shard(w_up, l), GATHER_VIA_SIBLING), (shard(w_down, l), GATHER_VIA_SIBLING)]))
        o = o.reshape(m_rows, 1024)
        gathered[l].update(w_up=w_up_g, w_down=w_down_g)
        wl = weights_of[l] = layer_weights(l)
        mix = _mm(o, wl["w_out_f"], name=f"mm_out{l}", tk=1024, **big)
        h_mid, v_in, v_t = _gate_mod_fwd(h, mix, row(g_post_mix[l]), modsel[l], 2, row(g_pre_mlp[l]), modsel[l], 3, 4,
                                         nt, f"mix_gate_mlp_mod_fwd{l}")
        more = l + 1 < nl
        res_up = _mm(v_in, wl["w_up_s"], name=f"mm_up{l}", b_mode="nn_slots", epilogue="relu2", tk=d,
                     comm=_Comm([(shard(w_in, l + 1), GATHER_VIA_SIBLING)]) if more else None, **big)
        (r_act, r_t), nxt_in = res_up if more else (res_up, None)
        res_down = _mm(r_act, wl["w_down_f"], name=f"mm_down{l}",
                       comm=_Comm([(shard(w_out, l + 1), GATHER_VIA_SIBLING)]) if more else None, **deep)
        y, nxt_out = res_down if more else (res_down, None)
        saved.append((h, u_t, proj, q_all, kv_all, o, o_t, lse, mix, h_mid, v_t, r_act, r_t, y))
        if more:
            gathered[l + 1] = dict(w_in=nxt_in[0], w_out=nxt_out[0])
            h, u, u_t = _gate_mod_fwd(h_mid, y, row(g_post_mlp[l]), modsel[l], 5, row(g_pre_mix[l + 1]), modsel[l + 1],
                                      0, 1, nt, f"mlp_gate_mix_mod_fwd{l}")

    dh, dy, dvec_g2, ss = _gate_loss_bwd(h_mid, y, row(g_post_mlp[nl - 1]), modsel[nl - 1], 5, target, nt, "gate_loss_bwd")
    small_g = {n: [None] * nl for n in SMALL if n not in ("c_ctx", "b_ada", "loss")}
    dvecs = {l: {} for l in range(nl)}
    dvecs[nl - 1]["g2"] = dvec_g2
    slots = {n: [None] * nl for n in ("w_in", "w_out", "w_up", "w_down")}
    recvd = {n: [None] * nl for n in slots}
    send = lambda n, l_: (slots[n][l_], TO_OWNER_XOR)
    for l in reversed(range(nl)):
        h_in, u_t, proj, q_all, kv_all, o, o_t, lse, mix, h_mid, v_t, r_act, r_t, y = saved[l]
        wl = weights_of[l]
        da = _mm(dy, wl["w_down_f"], name=f"mm_da{l}", b_mode="nt", epilogue="relu2_bwd", extra=r_act, out_dtype=BF16,
                 tk=d, **big)
        dw_down = _mm(r_t, dy, name=f"mm_dw_down{l}", out_dtype=BF16, **wide)
        dw_up = _mm(v_t, da, name=f"mm_dw_up{l}", out_mode="slots", out_dtype=BF16, **wide)
        slots["w_down"][l] = dw_down.reshape(N_DEV, -1, d)
        slots["w_up"][l] = dw_up
        dv = _mm(da, wl["w_up_t"], name=f"mm_dv{l}", **deep)
        dh, dvecs[l]["m2"], dmix, dvecs[l]["g1"] = _mod_gate_bwd(
            h_mid, row(g_pre_mlp[l]), modsel[l], 3, 4, dv, dh, mix, row(g_post_mix[l]), modsel[l], 2, nt,
            f"mlp_mod_mix_gate_bwd{l}")
        do = _mm(dmix, wl["w_out_f"], name=f"mm_do{l}", b_mode="nt", out_dtype=BF16, tk=d, **big)
        dw_out = _mm(o_t, dmix, name=f"mm_dw_out{l}", out_dtype=BF16, **wide)
        slots["w_out"][l] = dw_out.reshape(N_DEV, -1, d)
        going = [("w_down", l), ("w_up", l), ("w_out", l)] + ([("w_in", l + 1)] if l + 1 < nl else [])
        (dq, dkv, dsk), arrived = _attn_bwd(
            q_all.reshape(bl, t_rows, 1024), kv_all.reshape(bl, t_rows, 512), do.reshape(bl, t_rows, 1024),
            o.reshape(bl, t_rows, 1024), lse, sink[l], c_rows, f"attn_bwd{l}", comm=_Comm([send(*g_) for g_ in going]))
        for (n_, l_), got in zip(going, arrived):
            recvd[n_][l_] = got
        last = l == 0
        dproj, dqn, dkn = _prep_bwd(proj, dq.reshape(m_rows, 1024), dkv.reshape(m_rows, 512), qn[l], kn[l], cos, sin,
                                    emat, nt, f"prep_bwd{l}")
        dw_in = _mm(u_t, dproj, name=f"mm_dw_in{l}", out_dtype=BF16, **wide)
        slots["w_in"][l] = dw_in.reshape(d, N_DEV, PROJ_W // N_DEV).transpose(1, 0, 2)
        if last:
            halves = [(slots["w_in"][l][:, :d // 2], TO_OWNER_XOR), (slots["w_in"][l][:, d // 2:], TO_OWNER_XOR)]
            du, (got_a,) = _mm(dproj, wl["w_in_f"], name=f"mm_du{l}", b_mode="nt", tk=PROJ_W,
                               comm=_Comm(halves[:1]), **big)
            (dh, dvecs[l]["m1"]), (got_b,) = _norm_mod_bwd(
                h_in, row(g_pre_mix[l]), modsel[l], du, dh, 0, 1, nt, f"mix_mod_bwd{l}", latent_only=True,
                comm=_Comm(halves[1:]))
            recvd["w_in"][l] = jnp.concatenate([got_a, got_b], axis=1)
        else:
            du = _mm(dproj, wl["w_in_f"], name=f"mm_du{l}", b_mode="nt", tk=PROJ_W, **big)
            dh, dvecs[l]["m1"], dy, dvecs[l - 1]["g2"] = _mod_gate_bwd(
                h_in, row(g_pre_mix[l]), modsel[l], 0, 1, du, dh, saved[l - 1][-1], row(g_post_mlp[l - 1]),
                modsel[l - 1], 5, nt, f"mix_mod_mlp_gate_bwd{l}")
        small_g["q_norm"][l] = jnp.sum(dqn[0].reshape(8, HEAD_DIM), axis=0)
        small_g["k_norm"][l] = jnp.sum(dkn[0].reshape(2, HEAD_DIM), axis=0)
        small_g["sink"][l] = jnp.sum(dsk[:, 2:, :GROUP, 0], axis=0).reshape(-1)

    dmod_rows = []
    for l in range(nl):
        m1, g1, m2, g2 = (dvecs[l][n] for n in ("m1", "g1", "m2", "g2"))
        small_g["g_pre_mix"][l] = jnp.sum(m1[:, :, 2], axis=(0, 1))
        small_g["g_post_mix"][l] = jnp.sum(g1[:, :, 1], axis=(0, 1))
        small_g["g_pre_mlp"][l] = jnp.sum(m2[:, :, 2], axis=(0, 1))
        small_g["g_post_mlp"][l] = jnp.sum(g2[:, :, 1], axis=(0, 1))
        dms = jnp.stack([m1[:, :, 0], m1[:, :, 1], g1[:, :, 0], m2[:, :, 0], m2[:, :, 1], g2[:, :, 0]], axis=2)
        rows = jnp.concatenate([dms[:, 1], jnp.sum(dms[:, 0], axis=0)[None]], axis=0)
        dmod_rows.append(jnp.pad(rows.reshape(bl + 1, N_MOD * d), ((0, ADA_ROWS - bl - 1), (0, 0))))
    grad_x = dh.reshape(bl, s_rows, d)

    dmod_slots = jnp.stack(dmod_rows).reshape(nl, ADA_ROWS, N_DEV, ada_w).transpose(2, 0, 1, 3)
    dmod_g, = _comm_only(_Comm([(dmod_slots, TO_OWNER)]), "exchange_dmod")
    dmod_mine = dmod_g.transpose(1, 0, 2, 3).reshape(nl, N_DEV * ADA_ROWS, ada_w)
    dw_ada, dcc = _ada_bwd(c_all, c_ctx[None, :], dmod_mine, w_ada, "ada_bwd")

    parts = {n: jnp.stack(small_g[n]) for n in small_g}
    parts["c_ctx"] = dcc[0]
    parts["loss"] = ss[0, 0:1]
    no_loss = jnp.zeros((1,), F32)
    parts["b_ada"] = jnp.stack([jnp.sum(r_[: bl + 1], axis=0) for r_ in dmod_rows])
    weights = dict(c_ctx=c_ctx, b_ada=b_ada, g_pre_mix=g_pre_mix, g_post_mix=g_post_mix, g_pre_mlp=g_pre_mlp,
                   g_post_mlp=g_post_mlp, q_norm=q_norm, k_norm=k_norm, sink=sink, loss=no_loss)
    moms = dict(c_ctx=m_c_ctx, b_ada=m_b_ada, g_pre_mix=m_g_pre_mix, g_post_mix=m_g_post_mix, g_pre_mlp=m_g_pre_mlp,
                g_post_mlp=m_g_post_mlp, q_norm=m_q_norm, k_norm=m_k_norm, sink=m_sink, loss=no_loss)
    vels = dict(c_ctx=v_c_ctx, b_ada=v_b_ada, g_pre_mix=v_g_pre_mix, g_post_mix=v_g_post_mix, g_pre_mlp=v_g_pre_mlp,
                g_post_mlp=v_g_post_mlp, q_norm=v_q_norm, k_norm=v_k_norm, sink=v_sink, loss=no_loss)
    small_all, = _comm_only(_Comm([(_pack_small(parts), GATHER)]), "gather_small")
    s_out = _small_adamw(_pack_small(weights), _pack_small(moms), _pack_small(vels), small_all, "adamw_small")
    s_g, s_d, s_m, s_v = [_unpack_small(a, weights) for a in s_out]
    loss = 0.5 * s_g["loss"][0] / d

    res = {}
    for n, w_, m_, v_ in (("w_in", w_in, m_w_in, v_w_in), ("w_out", w_out, m_w_out, v_w_out),
                          ("w_up", w_up, m_w_up, v_w_up), ("w_down", w_down, m_w_down, v_w_down)):
        res[n] = _adamw(w_, m_, v_, None, recvd[n], f"adamw_{n}")
    res["w_ada"] = _adamw(w_ada, m_w_ada, v_w_ada, dw_ada, None, "adamw_w_ada")

    order = ("c_ctx", "w_ada", "b_ada", "g_pre_mix", "g_post_mix", "g_pre_mlp", "g_post_mlp", "w_in", "q_norm",
             "k_norm", "sink", "w_out", "w_up", "w_down")
    outs = [loss, grad_x]
    for i, small in enumerate((s_g, s_d, s_m, s_v)):
        outs += [small[n] if n in small else res[n][i] for n in order]
    return tuple(outs)
```
